```python
import math
import jax, jax.numpy as jnp
from jax import lax
import numpy as np

D_MODEL = 1024
BATCH = 8
SEQ = 16384
DEPTH = 2

N_HEADS = 16
N_KV_HEADS = 2
GROUP = N_HEADS // N_KV_HEADS
HEAD_DIM = 64
WINDOW = 128
BLOCK = 128
CONV_WIDTH = 31
D_FF = 2816
FFN_CONV_WIDTH = 3
RMS_EPS = 1e-6
LN_EPS = 1e-5
N_MIXERS = 2
N_ATTN = (DEPTH + 1) // 2
N_CONV = DEPTH // 2

kernel_name = "hybrid_swa_sink_conformer_convffn"


def rmsnorm(x, g):
    xf = x.astype(jnp.float32)
    y = xf * lax.rsqrt(jnp.mean(xf * xf, axis=-1, keepdims=True) + RMS_EPS)
    return (y * g.astype(jnp.float32)).astype(x.dtype)


def layernorm(x, g, b):
    xf = x.astype(jnp.float32)
    mu = jnp.mean(xf, axis=-1, keepdims=True)
    var = jnp.mean(jnp.square(xf - mu), axis=-1, keepdims=True)
    y = (xf - mu) * lax.rsqrt(var + LN_EPS)
    return (y * g.astype(jnp.float32) + b.astype(jnp.float32)).astype(x.dtype)


def causal_dwconv(x, w, b):
    k, c = w.shape
    y = lax.conv_general_dilated(
        x, w[:, None, :].astype(x.dtype), window_strides=(1,),
        padding=((k - 1, 0),), dimension_numbers=("NWC", "WIO", "NWC"),
        feature_group_count=c)
    return y + b.astype(x.dtype)


def sliding_window_sink_attention(h, w_qkv, b_qkv, sinks, w_o, b_o):
    bsz, seq, _ = h.shape
    nb = seq // BLOCK
    qkv = h @ w_qkv + b_qkv
    q_dim = N_HEADS * HEAD_DIM
    kv_dim = N_KV_HEADS * HEAD_DIM
    q = qkv[..., :q_dim].reshape(bsz, nb, BLOCK, N_KV_HEADS, GROUP, HEAD_DIM)
    k = qkv[..., q_dim:q_dim + kv_dim].reshape(bsz, seq, N_KV_HEADS, HEAD_DIM)
    v = qkv[..., q_dim + kv_dim:].reshape(bsz, seq, N_KV_HEADS, HEAD_DIM)

    def band_keys(t):
        tp = jnp.pad(t, ((0, 0), (BLOCK, 0), (0, 0), (0, 0)))
        tp = tp.reshape(bsz, nb + 1, BLOCK, N_KV_HEADS, HEAD_DIM)
        return jnp.concatenate([tp[:, :-1], tp[:, 1:]], axis=2)

    kb = band_keys(k)
    vb = band_keys(v)
    scale = 1.0 / math.sqrt(HEAD_DIM)
    s = jnp.einsum("bnqhgd,bnkhd->bnhgqk", q, kb).astype(jnp.float32) * scale

    qi = jnp.arange(BLOCK)[:, None]
    ki = jnp.arange(2 * BLOCK)[None, :]
    dist = qi + BLOCK - ki
    band = (dist >= 0) & (dist < WINDOW)
    first = (jnp.arange(nb) == 0)[:, None, None] & (ki < BLOCK)[None]
    valid = band[None] & ~first
    s = jnp.where(valid[None, :, None, None], s, jnp.finfo(jnp.float32).min)

    sink = sinks.astype(jnp.float32).reshape(1, 1, N_KV_HEADS, GROUP, 1, 1)
    m = jnp.maximum(jnp.max(s, axis=-1, keepdims=True), sink)
    p = jnp.exp(s - m)
    denom = jnp.sum(p, axis=-1, keepdims=True) + jnp.exp(sink - m)
    probs = (p / denom).astype(v.dtype)
    o = jnp.einsum("bnhgqk,bnkhd->bnqhgd", probs, vb)
    o = o.reshape(bsz, seq, q_dim)
    return o @ w_o + b_o


def conformer_conv_module(h, w_pw1, b_pw1, w_dw, b_dw, ln_g, ln_b, w_pw2, b_pw2):
    a = h @ w_pw1 + b_pw1
    u = a[..., :D_MODEL] * jax.nn.sigmoid(a[..., D_MODEL:])
    u = causal_dwconv(u, w_dw, b_dw)
    u = jax.nn.silu(layernorm(u, ln_g, ln_b))
    return u @ w_pw2 + b_pw2


def conv_ffn(h, w_up, w_dw, b_dw, w_down):
    up = h @ w_up
    gate = causal_dwconv(up[..., :D_FF], w_dw, b_dw)
    return (jax.nn.silu(gate) * up[..., D_FF:]) @ w_down


def _fwd_setup_inputs(seed: int = 0) -> dict:
    key = jax.random.key(seed)
    ks = jax.random.split(key, 24)
    f32 = jnp.float32
    qkv_dim = (N_HEADS + 2 * N_KV_HEADS) * HEAD_DIM

    def nrm(k, shape, scale):
        return jax.random.normal(k, shape, f32) * scale

    def gain(k, shape):
        return 1.0 + 0.02 * jax.random.normal(k, shape, f32)

    return {
        "x": jax.random.normal(ks[0], (BATCH, SEQ, D_MODEL), f32),
        "norm_mix": gain(ks[1], (DEPTH, D_MODEL)),
        "attn_w_qkv": nrm(ks[2], (N_ATTN, D_MODEL, qkv_dim), D_MODEL ** -0.5),
        "attn_b_qkv": nrm(ks[3], (N_ATTN, qkv_dim), 0.02),
        "attn_sinks": nrm(ks[4], (N_ATTN, N_HEADS), 0.5),
        "attn_w_o": nrm(ks[5], (N_ATTN, N_HEADS * HEAD_DIM, D_MODEL), (N_HEADS * HEAD_DIM) ** -0.5),
        "attn_b_o": nrm(ks[6], (N_ATTN, D_MODEL), 0.02),
        "conv_w_pw1": nrm(ks[7], (N_CONV, D_MODEL, 2 * D_MODEL), D_MODEL ** -0.5),
        "conv_b_pw1": nrm(ks[8], (N_CONV, 2 * D_MODEL), 0.02),
        "conv_w_dw": nrm(ks[9], (N_CONV, CONV_WIDTH, D_MODEL), CONV_WIDTH ** -0.5),
        "conv_b_dw": nrm(ks[10], (N_CONV, D_MODEL), 0.02),
        "conv_ln_g": gain(ks[11], (N_CONV, D_MODEL)),
        "conv_ln_b": nrm(ks[12], (N_CONV, D_MODEL), 0.02),
        "conv_w_pw2": nrm(ks[13], (N_CONV, D_MODEL, D_MODEL), D_MODEL ** -0.5),
        "conv_b_pw2": nrm(ks[14], (N_CONV, D_MODEL), 0.02),
        "norm_ffn": gain(ks[15], (DEPTH, D_MODEL)),
        "ffn_w_up": nrm(ks[16], (DEPTH, D_MODEL, 2 * D_FF), D_MODEL ** -0.5),
        "ffn_w_dw": nrm(ks[17], (DEPTH, FFN_CONV_WIDTH, D_FF), FFN_CONV_WIDTH ** -0.5),
        "ffn_b_dw": nrm(ks[18], (DEPTH, D_FF), 0.02),
        "ffn_w_down": nrm(ks[19], (DEPTH, D_FF, D_MODEL), D_FF ** -0.5),
        "final_norm": gain(ks[20], (D_MODEL,)),
    }


def _fwd_reference(x, norm_mix, attn_w_qkv, attn_b_qkv, attn_sinks, attn_w_o, attn_b_o,
              conv_w_pw1, conv_b_pw1, conv_w_dw, conv_b_dw, conv_ln_g, conv_ln_b,
              conv_w_pw2, conv_b_pw2, norm_ffn, ffn_w_up, ffn_w_dw, ffn_b_dw,
              ffn_w_down, final_norm):
    for i in range(DEPTH):
        h = rmsnorm(x, norm_mix[i])
        j = i // N_MIXERS
        if i % N_MIXERS == 0:
            x = x + sliding_window_sink_attention(
                h, attn_w_qkv[j], attn_b_qkv[j], attn_sinks[j], attn_w_o[j], attn_b_o[j])
        else:
            x = x + conformer_conv_module(
                h, conv_w_pw1[j], conv_b_pw1[j], conv_w_dw[j], conv_b_dw[j],
                conv_ln_g[j], conv_ln_b[j], conv_w_pw2[j], conv_b_pw2[j])
        h = rmsnorm(x, norm_ffn[i])
        x = x + conv_ffn(h, ffn_w_up[i], ffn_w_dw[i], ffn_b_dw[i], ffn_w_down[i])
    return rmsnorm(x, final_norm)


import jax as _jax
import jax.numpy as _jnp

TWIN_FORMAT = 'train_step'
FWD_PARAMS = ['x', 'norm_mix', 'attn_w_qkv', 'attn_b_qkv', 'attn_sinks', 'attn_w_o', 'attn_b_o', 'conv_w_pw1', 'conv_b_pw1', 'conv_w_dw', 'conv_b_dw', 'conv_ln_g', 'conv_ln_b', 'conv_w_pw2', 'conv_b_pw2', 'norm_ffn', 'ffn_w_up', 'ffn_w_dw', 'ffn_b_dw', 'ffn_w_down', 'final_norm']
TWIN_WEIGHTS = ['norm_mix', 'attn_w_qkv', 'attn_b_qkv', 'attn_sinks', 'attn_w_o', 'attn_b_o', 'conv_w_pw1', 'conv_b_pw1', 'conv_w_dw', 'conv_b_dw', 'conv_ln_g', 'conv_ln_b', 'conv_w_pw2', 'conv_b_pw2', 'norm_ffn', 'ffn_w_up', 'ffn_w_dw', 'ffn_b_dw', 'ffn_w_down', 'final_norm']
TWIN_DIFF_INPUT = 'x'
TWIN_INPUTS = ['x', 'norm_mix', 'attn_w_qkv', 'attn_b_qkv', 'attn_sinks', 'attn_w_o', 'attn_b_o', 'conv_w_pw1', 'conv_b_pw1', 'conv_w_dw', 'conv_b_dw', 'conv_ln_g', 'conv_ln_b', 'conv_w_pw2', 'conv_b_pw2', 'norm_ffn', 'ffn_w_up', 'ffn_w_dw', 'ffn_b_dw', 'ffn_w_down', 'final_norm', 'loss_target', 'm_norm_mix', 'm_attn_w_qkv', 'm_attn_b_qkv', 'm_attn_sinks', 'm_attn_w_o', 'm_attn_b_o', 'm_conv_w_pw1', 'm_conv_b_pw1', 'm_conv_w_dw', 'm_conv_b_dw', 'm_conv_ln_g', 'm_conv_ln_b', 'm_conv_w_pw2', 'm_conv_b_pw2', 'm_norm_ffn', 'm_ffn_w_up', 'm_ffn_w_dw', 'm_ffn_b_dw', 'm_ffn_w_down', 'm_final_norm', 'v_norm_mix', 'v_attn_w_qkv', 'v_attn_b_qkv', 'v_attn_sinks', 'v_attn_w_o', 'v_attn_b_o', 'v_conv_w_pw1', 'v_conv_b_pw1', 'v_conv_w_dw', 'v_conv_b_dw', 'v_conv_ln_g', 'v_conv_ln_b', 'v_conv_w_pw2', 'v_conv_b_pw2', 'v_norm_ffn', 'v_ffn_w_up', 'v_ffn_w_dw', 'v_ffn_b_dw', 'v_ffn_w_down', 'v_final_norm']
TWIN_OUTPUTS = ['loss', 'grad_x', 'grad_norm_mix', 'grad_attn_w_qkv', 'grad_attn_b_qkv', 'grad_attn_sinks', 'grad_attn_w_o', 'grad_attn_b_o', 'grad_conv_w_pw1', 'grad_conv_b_pw1', 'grad_conv_w_dw', 'grad_conv_b_dw', 'grad_conv_ln_g', 'grad_conv_ln_b', 'grad_conv_w_pw2', 'grad_conv_b_pw2', 'grad_norm_ffn', 'grad_ffn_w_up', 'grad_ffn_w_dw', 'grad_ffn_b_dw', 'grad_ffn_w_down', 'grad_final_norm', 'delta_norm_mix', 'delta_attn_w_qkv', 'delta_attn_b_qkv', 'delta_attn_sinks', 'delta_attn_w_o', 'delta_attn_b_o', 'delta_conv_w_pw1', 'delta_conv_b_pw1', 'delta_conv_w_dw', 'delta_conv_b_dw', 'delta_conv_ln_g', 'delta_conv_ln_b', 'delta_conv_w_pw2', 'delta_conv_b_pw2', 'delta_norm_ffn', 'delta_ffn_w_up', 'delta_ffn_w_dw', 'delta_ffn_b_dw', 'delta_ffn_w_down', 'delta_final_norm', 'new_m_norm_mix', 'new_m_attn_w_qkv', 'new_m_attn_b_qkv', 'new_m_attn_sinks', 'new_m_attn_w_o', 'new_m_attn_b_o', 'new_m_conv_w_pw1', 'new_m_conv_b_pw1', 'new_m_conv_w_dw', 'new_m_conv_b_dw', 'new_m_conv_ln_g', 'new_m_conv_ln_b', 'new_m_conv_w_pw2', 'new_m_conv_b_pw2', 'new_m_norm_ffn', 'new_m_ffn_w_up', 'new_m_ffn_w_dw', 'new_m_ffn_b_dw', 'new_m_ffn_w_down', 'new_m_final_norm', 'new_v_norm_mix', 'new_v_attn_w_qkv', 'new_v_attn_b_qkv', 'new_v_attn_sinks', 'new_v_attn_w_o', 'new_v_attn_b_o', 'new_v_conv_w_pw1', 'new_v_conv_b_pw1', 'new_v_conv_w_dw', 'new_v_conv_b_dw', 'new_v_conv_ln_g', 'new_v_conv_ln_b', 'new_v_conv_w_pw2', 'new_v_conv_b_pw2', 'new_v_norm_ffn', 'new_v_ffn_w_up', 'new_v_ffn_w_dw', 'new_v_ffn_b_dw', 'new_v_ffn_w_down', 'new_v_final_norm']
TWIN_LEAF_KINDS = {'loss': 'loss', 'grad_x': 'grad_x', 'grad_norm_mix': 'grad_w', 'grad_attn_w_qkv': 'grad_w', 'grad_attn_b_qkv': 'grad_w', 'grad_attn_sinks': 'grad_w', 'grad_attn_w_o': 'grad_w', 'grad_attn_b_o': 'grad_w', 'grad_conv_w_pw1': 'grad_w', 'grad_conv_b_pw1': 'grad_w', 'grad_conv_w_dw': 'grad_w', 'grad_conv_b_dw': 'grad_w', 'grad_conv_ln_g': 'grad_w', 'grad_conv_ln_b': 'grad_w', 'grad_conv_w_pw2': 'grad_w', 'grad_conv_b_pw2': 'grad_w', 'grad_norm_ffn': 'grad_w', 'grad_ffn_w_up': 'grad_w', 'grad_ffn_w_dw': 'grad_w', 'grad_ffn_b_dw': 'grad_w', 'grad_ffn_w_down': 'grad_w', 'grad_final_norm': 'grad_w', 'delta_norm_mix': 'delta_w', 'delta_attn_w_qkv': 'delta_w', 'delta_attn_b_qkv': 'delta_w', 'delta_attn_sinks': 'delta_w', 'delta_attn_w_o': 'delta_w', 'delta_attn_b_o': 'delta_w', 'delta_conv_w_pw1': 'delta_w', 'delta_conv_b_pw1': 'delta_w', 'delta_conv_w_dw': 'delta_w', 'delta_conv_b_dw': 'delta_w', 'delta_conv_ln_g': 'delta_w', 'delta_conv_ln_b': 'delta_w', 'delta_conv_w_pw2': 'delta_w', 'delta_conv_b_pw2': 'delta_w', 'delta_norm_ffn': 'delta_w', 'delta_ffn_w_up': 'delta_w', 'delta_ffn_w_dw': 'delta_w', 'delta_ffn_b_dw': 'delta_w', 'delta_ffn_w_down': 'delta_w', 'delta_final_norm': 'delta_w', 'new_m_norm_mix': 'new_m', 'new_m_attn_w_qkv': 'new_m', 'new_m_attn_b_qkv': 'new_m', 'new_m_attn_sinks': 'new_m', 'new_m_attn_w_o': 'new_m', 'new_m_attn_b_o': 'new_m', 'new_m_conv_w_pw1': 'new_m', 'new_m_conv_b_pw1': 'new_m', 'new_m_conv_w_dw': 'new_m', 'new_m_conv_b_dw': 'new_m', 'new_m_conv_ln_g': 'new_m', 'new_m_conv_ln_b': 'new_m', 'new_m_conv_w_pw2': 'new_m', 'new_m_conv_b_pw2': 'new_m', 'new_m_norm_ffn': 'new_m', 'new_m_ffn_w_up': 'new_m', 'new_m_ffn_w_dw': 'new_m', 'new_m_ffn_b_dw': 'new_m', 'new_m_ffn_w_down': 'new_m', 'new_m_final_norm': 'new_m', 'new_v_norm_mix': 'new_v', 'new_v_attn_w_qkv': 'new_v', 'new_v_attn_b_qkv': 'new_v', 'new_v_attn_sinks': 'new_v', 'new_v_attn_w_o': 'new_v', 'new_v_attn_b_o': 'new_v', 'new_v_conv_w_pw1': 'new_v', 'new_v_conv_b_pw1': 'new_v', 'new_v_conv_w_dw': 'new_v', 'new_v_conv_b_dw': 'new_v', 'new_v_conv_ln_g': 'new_v', 'new_v_conv_ln_b': 'new_v', 'new_v_conv_w_pw2': 'new_v', 'new_v_conv_b_pw2': 'new_v', 'new_v_norm_ffn': 'new_v', 'new_v_ffn_w_up': 'new_v', 'new_v_ffn_w_dw': 'new_v', 'new_v_ffn_b_dw': 'new_v', 'new_v_ffn_w_down': 'new_v', 'new_v_final_norm': 'new_v'}


def _forward(args):
    return _fwd_reference(*[args[k] for k in FWD_PARAMS])


def _output_shape():
    def fwd():
        inp = _fwd_setup_inputs(0)
        return _fwd_reference(*[inp[k] for k in FWD_PARAMS])
    out = _jax.eval_shape(fwd)
    return out.shape, out.dtype

N_MICROBATCH = 1
ADAM_LR = 0.001
ADAM_B1 = 0.9
ADAM_B2 = 0.999
ADAM_EPS = 1e-08
ADAM_WD = 0.01
ADAM_STEP = 10
PER_EXAMPLE_BATCH_AXIS = {'x': 0, 'loss_target': 0}
SHARED_INPUTS = []
_WEIGHT_DTYPES = {'norm_mix': _jnp.float32, 'attn_w_qkv': _jnp.float32, 'attn_b_qkv': _jnp.float32, 'attn_sinks': _jnp.float32, 'attn_w_o': _jnp.float32, 'attn_b_o': _jnp.float32, 'conv_w_pw1': _jnp.float32, 'conv_b_pw1': _jnp.float32, 'conv_w_dw': _jnp.float32, 'conv_b_dw': _jnp.float32, 'conv_ln_g': _jnp.float32, 'conv_ln_b': _jnp.float32, 'conv_w_pw2': _jnp.float32, 'conv_b_pw2': _jnp.float32, 'norm_ffn': _jnp.float32, 'ffn_w_up': _jnp.float32, 'ffn_w_dw': _jnp.float32, 'ffn_b_dw': _jnp.float32, 'ffn_w_down': _jnp.float32, 'final_norm': _jnp.float32}
MOMENT_SCALE = {'norm_mix': 1.734247e-01, 'attn_w_qkv': 1.276283e-01, 'attn_b_qkv': 5.513588e-01, 'attn_sinks': 7.558631e-02, 'attn_w_o': 9.083952e-02, 'attn_b_o': 5.638200e-01, 'conv_w_pw1': 1.386437e-01, 'conv_b_pw1': 1.518604e-01, 'conv_w_dw': 1.832311e-01, 'conv_b_dw': 3.859829e-01, 'conv_ln_g': 2.454110e-01, 'conv_ln_b': 2.229266e-01, 'conv_w_pw2': 1.866238e-01, 'conv_b_pw2': 4.251391e-01, 'norm_ffn': 2.735093e-01, 'ffn_w_up': 1.104673e-01, 'ffn_w_dw': 1.137069e-01, 'ffn_b_dw': 1.086146e-01, 'ffn_w_down': 1.805931e-01, 'final_norm': 1.279912e+02}


def _to_microbatches(a, axis):
    t = _jnp.moveaxis(a, axis, 0)
    t = t.reshape((N_MICROBATCH, t.shape[0] // N_MICROBATCH) + t.shape[1:])
    return _jnp.moveaxis(t, 1, axis + 1)


def setup_inputs(seed: int = 0) -> dict:
    inp = _fwd_setup_inputs(seed)
    key = _jax.random.fold_in(_jax.random.key(seed), 7919)
    shape, _ = _output_shape()
    out = dict(inp)
    out["loss_target"] = _jax.random.normal(_jax.random.fold_in(key, 0), shape, _jnp.float32)
    for i, name in enumerate(TWIN_WEIGHTS):
        w = inp[name].astype(_jnp.float32)
        if MOMENT_SCALE is None:
            s = _jnp.sqrt(_jnp.mean(_jnp.square(w)) + 1e-30)
        else:
            s = MOMENT_SCALE[name]
        km, kv = _jax.random.split(_jax.random.fold_in(key, i + 1))
        out[name] = w
        out["m_" + name] = s * _jax.random.normal(km, w.shape, _jnp.float32)
        out["v_" + name] = (s * s) * _jax.random.uniform(kv, w.shape, _jnp.float32, 0.5, 1.5)
    if N_MICROBATCH > 1:
        for name, axis in PER_EXAMPLE_BATCH_AXIS.items():
            out[name] = _to_microbatches(out[name], axis)
    return {'x': out['x'], 'norm_mix': out['norm_mix'], 'attn_w_qkv': out['attn_w_qkv'], 'attn_b_qkv': out['attn_b_qkv'], 'attn_sinks': out['attn_sinks'], 'attn_w_o': out['attn_w_o'], 'attn_b_o': out['attn_b_o'], 'conv_w_pw1': out['conv_w_pw1'], 'conv_b_pw1': out['conv_b_pw1'], 'conv_w_dw': out['conv_w_dw'], 'conv_b_dw': out['conv_b_dw'], 'conv_ln_g': out['conv_ln_g'], 'conv_ln_b': out['conv_ln_b'], 'conv_w_pw2': out['conv_w_pw2'], 'conv_b_pw2': out['conv_b_pw2'], 'norm_ffn': out['norm_ffn'], 'ffn_w_up': out['ffn_w_up'], 'ffn_w_dw': out['ffn_w_dw'], 'ffn_b_dw': out['ffn_b_dw'], 'ffn_w_down': out['ffn_w_down'], 'final_norm': out['final_norm'], 'loss_target': out['loss_target'], 'm_norm_mix': out['m_norm_mix'], 'm_attn_w_qkv': out['m_attn_w_qkv'], 'm_attn_b_qkv': out['m_attn_b_qkv'], 'm_attn_sinks': out['m_attn_sinks'], 'm_attn_w_o': out['m_attn_w_o'], 'm_attn_b_o': out['m_attn_b_o'], 'm_conv_w_pw1': out['m_conv_w_pw1'], 'm_conv_b_pw1': out['m_conv_b_pw1'], 'm_conv_w_dw': out['m_conv_w_dw'], 'm_conv_b_dw': out['m_conv_b_dw'], 'm_conv_ln_g': out['m_conv_ln_g'], 'm_conv_ln_b': out['m_conv_ln_b'], 'm_conv_w_pw2': out['m_conv_w_pw2'], 'm_conv_b_pw2': out['m_conv_b_pw2'], 'm_norm_ffn': out['m_norm_ffn'], 'm_ffn_w_up': out['m_ffn_w_up'], 'm_ffn_w_dw': out['m_ffn_w_dw'], 'm_ffn_b_dw': out['m_ffn_b_dw'], 'm_ffn_w_down': out['m_ffn_w_down'], 'm_final_norm': out['m_final_norm'], 'v_norm_mix': out['v_norm_mix'], 'v_attn_w_qkv': out['v_attn_w_qkv'], 'v_attn_b_qkv': out['v_attn_b_qkv'], 'v_attn_sinks': out['v_attn_sinks'], 'v_attn_w_o': out['v_attn_w_o'], 'v_attn_b_o': out['v_attn_b_o'], 'v_conv_w_pw1': out['v_conv_w_pw1'], 'v_conv_b_pw1': out['v_conv_b_pw1'], 'v_conv_w_dw': out['v_conv_w_dw'], 'v_conv_b_dw': out['v_conv_b_dw'], 'v_conv_ln_g': out['v_conv_ln_g'], 'v_conv_ln_b': out['v_conv_ln_b'], 'v_conv_w_pw2': out['v_conv_w_pw2'], 'v_conv_b_pw2': out['v_conv_b_pw2'], 'v_norm_ffn': out['v_norm_ffn'], 'v_ffn_w_up': out['v_ffn_w_up'], 'v_ffn_w_dw': out['v_ffn_w_dw'], 'v_ffn_b_dw': out['v_ffn_b_dw'], 'v_ffn_w_down': out['v_ffn_w_down'], 'v_final_norm': out['v_final_norm']}


def _loss(weights, diff, rest, loss_target):
    with _jax.named_scope("forward"):
        args = {**rest, TWIN_DIFF_INPUT: diff, **{k: w.astype(_WEIGHT_DTYPES[k]) for k, w in weights.items()}}
        y = _forward(args)
    with _jax.named_scope("loss_head"):
        err = _jnp.square(y.astype(_jnp.float32) - loss_target)
        return 0.5 * _jnp.sum(_jnp.mean(err, axis=-1)) if err.ndim else 0.5 * err


def _adamw(w, g, m, v):
    m = ADAM_B1 * m + (1.0 - ADAM_B1) * g
    v = ADAM_B2 * v + (1.0 - ADAM_B2) * _jnp.square(g)
    m_hat = m / (1.0 - ADAM_B1 ** ADAM_STEP)
    v_hat = v / (1.0 - ADAM_B2 ** ADAM_STEP)
    delta = -ADAM_LR * (m_hat / (_jnp.sqrt(v_hat) + ADAM_EPS) + ADAM_WD * w)
    return delta, m, v


def reference(x, norm_mix, attn_w_qkv, attn_b_qkv, attn_sinks, attn_w_o, attn_b_o, conv_w_pw1, conv_b_pw1, conv_w_dw, conv_b_dw, conv_ln_g, conv_ln_b, conv_w_pw2, conv_b_pw2, norm_ffn, ffn_w_up, ffn_w_dw, ffn_b_dw, ffn_w_down, final_norm, loss_target, m_norm_mix, m_attn_w_qkv, m_attn_b_qkv, m_attn_sinks, m_attn_w_o, m_attn_b_o, m_conv_w_pw1, m_conv_b_pw1, m_conv_w_dw, m_conv_b_dw, m_conv_ln_g, m_conv_ln_b, m_conv_w_pw2, m_conv_b_pw2, m_norm_ffn, m_ffn_w_up, m_ffn_w_dw, m_ffn_b_dw, m_ffn_w_down, m_final_norm, v_norm_mix, v_attn_w_qkv, v_attn_b_qkv, v_attn_sinks, v_attn_w_o, v_attn_b_o, v_conv_w_pw1, v_conv_b_pw1, v_conv_w_dw, v_conv_b_dw, v_conv_ln_g, v_conv_ln_b, v_conv_w_pw2, v_conv_b_pw2, v_norm_ffn, v_ffn_w_up, v_ffn_w_dw, v_ffn_b_dw, v_ffn_w_down, v_final_norm):
    given = dict(x=x, norm_mix=norm_mix, attn_w_qkv=attn_w_qkv, attn_b_qkv=attn_b_qkv, attn_sinks=attn_sinks, attn_w_o=attn_w_o, attn_b_o=attn_b_o, conv_w_pw1=conv_w_pw1, conv_b_pw1=conv_b_pw1, conv_w_dw=conv_w_dw, conv_b_dw=conv_b_dw, conv_ln_g=conv_ln_g, conv_ln_b=conv_ln_b, conv_w_pw2=conv_w_pw2, conv_b_pw2=conv_b_pw2, norm_ffn=norm_ffn, ffn_w_up=ffn_w_up, ffn_w_dw=ffn_w_dw, ffn_b_dw=ffn_b_dw, ffn_w_down=ffn_w_down, final_norm=final_norm, loss_target=loss_target, m_norm_mix=m_norm_mix, m_attn_w_qkv=m_attn_w_qkv, m_attn_b_qkv=m_attn_b_qkv, m_attn_sinks=m_attn_sinks, m_attn_w_o=m_attn_w_o, m_attn_b_o=m_attn_b_o, m_conv_w_pw1=m_conv_w_pw1, m_conv_b_pw1=m_conv_b_pw1, m_conv_w_dw=m_conv_w_dw, m_conv_b_dw=m_conv_b_dw, m_conv_ln_g=m_conv_ln_g, m_conv_ln_b=m_conv_ln_b, m_conv_w_pw2=m_conv_w_pw2, m_conv_b_pw2=m_conv_b_pw2, m_norm_ffn=m_norm_ffn, m_ffn_w_up=m_ffn_w_up, m_ffn_w_dw=m_ffn_w_dw, m_ffn_b_dw=m_ffn_b_dw, m_ffn_w_down=m_ffn_w_down, m_final_norm=m_final_norm, v_norm_mix=v_norm_mix, v_attn_w_qkv=v_attn_w_qkv, v_attn_b_qkv=v_attn_b_qkv, v_attn_sinks=v_attn_sinks, v_attn_w_o=v_attn_w_o, v_attn_b_o=v_attn_b_o, v_conv_w_pw1=v_conv_w_pw1, v_conv_b_pw1=v_conv_b_pw1, v_conv_w_dw=v_conv_w_dw, v_conv_b_dw=v_conv_b_dw, v_conv_ln_g=v_conv_ln_g, v_conv_ln_b=v_conv_ln_b, v_conv_w_pw2=v_conv_w_pw2, v_conv_b_pw2=v_conv_b_pw2, v_norm_ffn=v_norm_ffn, v_ffn_w_up=v_ffn_w_up, v_ffn_w_dw=v_ffn_w_dw, v_ffn_b_dw=v_ffn_b_dw, v_ffn_w_down=v_ffn_w_down, v_final_norm=v_final_norm)
    weights = {n: given[n] for n in TWIN_WEIGHTS}
    shared = {n: given[n] for n in SHARED_INPUTS}
    per_example = {n: given[n] for n in ['x']}
    grad_fn = _jax.value_and_grad(_loss, argnums=(0, 1))

    def one_microbatch(ex, loss_target):
        ex = dict(ex)
        diff = ex.pop(TWIN_DIFF_INPUT)
        return grad_fn(weights, diff, {**shared, **ex}, loss_target)

    if N_MICROBATCH == 1:
        loss, (grad_w, grad_x) = one_microbatch(per_example, given["loss_target"])
    else:
        def body(carry, xs):
            loss_sum, grad_sum = carry
            l_k, (gw_k, gx_k) = one_microbatch(xs[0], xs[1])
            with _jax.named_scope("update"):
                return (loss_sum + l_k, _jax.tree.map(_jnp.add, grad_sum, gw_k)), gx_k

        init = (_jnp.zeros((), _jnp.float32), _jax.tree.map(_jnp.zeros_like, weights))
        (loss, grad_w), grad_x = _jax.lax.scan(body, init, (per_example, given["loss_target"]))
    with _jax.named_scope("update"):
        delta_w, new_m, new_v = {}, {}, {}
        for n in TWIN_WEIGHTS:
            delta_w[n], new_m[n], new_v[n] = _adamw(weights[n], grad_w[n], given["m_" + n], given["v_" + n])
    return (loss, grad_x, *[grad_w[n] for n in TWIN_WEIGHTS], *[delta_w[n] for n in TWIN_WEIGHTS],
            *[new_m[n] for n in TWIN_WEIGHTS], *[new_v[n] for n in TWIN_WEIGHTS])
```

```python
import functools
import math

import jax
import jax.numpy as jnp
from jax import lax
from jax.experimental import pallas as pl
from jax.experimental.pallas import tpu as pltpu

F32 = jnp.float32
BF16 = jnp.bfloat16

D = 1024
F = 2816
HEAD_DIM = 64
N_KV = 2
BLOCK = 128
NQ = 1024
KVW = 256
CONV_K = 31
RMS_EPS = 1e-6
LN_EPS = 1e-5
ADAM_LR, ADAM_B1, ADAM_B2, ADAM_EPS, ADAM_WD, ADAM_STEP = 0.001, 0.9, 0.999, 1e-08, 0.01, 10
N_DEV = 8
MESH = pl.DeviceIdType.MESH
VMEM_LIMIT_BYTES = 56 * 2**20
NEG = float(jnp.finfo(jnp.float32).min)

MAT_ROWS = (160, 128, 256, 128, 704, 352, 704, 352)
MAT_OFF = tuple(sum(MAT_ROWS[:i]) for i in range(len(MAT_ROWS)))
SLAB_ROWS = sum(MAT_ROWS)
N_MAT = len(MAT_ROWS)

NT_DIMS = (((1,), (1,)), ((), ()))
NN_DIMS = (((1,), (0,)), ((), ()))
TN_DIMS = (((0,), (0,)), ((), ()))


def _params(*sem):
    return pltpu.CompilerParams(dimension_semantics=tuple(sem), vmem_limit_bytes=VMEM_LIMIT_BYTES)


def _row_tile(rows, cap, mult=8):
    best = None
    for t in range(mult, min(rows, cap) + 1, mult):
        if rows % t == 0:
            best = t
    return best if best is not None else rows


def _sigmoid(x):
    return 1.0 / (1.0 + jnp.exp(-x))


def _mm(name, a, b, *, nt, tm, tn, out_dtype, rms_g=None, bias=None, res=None):
    m, k = a.shape
    n = b.shape[0] if nt else b.shape[1]
    tm = min(tm, m)
    has_rms = rms_g is not None
    dims = NT_DIMS if nt else NN_DIMS

    def body(*refs):
        a_ref, b_ref = refs[0], refs[1]
        pos = 2
        g_ref = bias_ref = res_ref = h_ref = hs_ref = None
        if has_rms:
            g_ref = refs[pos]; pos += 1
        if bias is not None:
            bias_ref = refs[pos]; pos += 1
        if res is not None:
            res_ref = refs[pos]; pos += 1
        o_ref = refs[pos]; pos += 1
        if has_rms:
            h_ref, hs_ref = refs[pos], refs[pos + 1]

            @pl.when(pl.program_id(1) == 0)
            def _():
                xf = a_ref[...]
                r = lax.rsqrt(jnp.mean(xf * xf, axis=-1, keepdims=True) + RMS_EPS)
                h = ((xf * r) * g_ref[...]).astype(BF16)
                hs_ref[...] = h
                h_ref[...] = h

            lhs = hs_ref[...]
        else:
            lhs = a_ref[...].astype(BF16)
        acc = lax.dot_general(lhs, b_ref[...], dims, preferred_element_type=F32)
        if bias is not None:
            acc = acc + bias_ref[...]
        if res is not None:
            acc = acc + res_ref[...]
        o_ref[...] = acc.astype(out_dtype)

    in_specs = [pl.BlockSpec((tm, k), lambda i, j: (i, 0)),
                pl.BlockSpec((tn, k), lambda i, j: (j, 0)) if nt else pl.BlockSpec((k, tn), lambda i, j: (0, j))]
    args = [a, b]
    if has_rms:
        in_specs.append(pl.BlockSpec((1, k), lambda i, j: (0, 0))); args.append(rms_g)
    if bias is not None:
        in_specs.append(pl.BlockSpec((1, tn), lambda i, j: (0, j))); args.append(bias)
    if res is not None:
        in_specs.append(pl.BlockSpec((tm, tn), lambda i, j: (i, j))); args.append(res)
    out_shape = [jax.ShapeDtypeStruct((m, n), out_dtype)]
    out_specs = [pl.BlockSpec((tm, tn), lambda i, j: (i, j))]
    scratch = []
    if has_rms:
        out_shape.append(jax.ShapeDtypeStruct((m, k), BF16))
        out_specs.append(pl.BlockSpec((tm, k), lambda i, j: (i, 0)))
        scratch.append(pltpu.VMEM((tm, k), BF16))
    outs = pl.pallas_call(body, grid=(m // tm, n // tn), in_specs=in_specs, out_specs=out_specs, out_shape=out_shape,
                          scratch_shapes=scratch, compiler_params=_params("parallel", "arbitrary"), name=name)(*args)
    return outs if has_rms else outs[0]


def _mm_tn(name, l, r, *, tn, tt, colsum_l=False, colsum_r=False):
    s, nl = l.shape
    nr = r.shape[1]
    tt = min(tt, s)
    n_t = s // tt

    def body(*refs):
        l_ref, r_ref, o_ref = refs[0], refs[1], refs[2]
        pos = 3
        cl_ref = cr_ref = None
        if colsum_l:
            cl_ref = refs[pos]; pos += 1
        if colsum_r:
            cr_ref = refs[pos]; pos += 1
        acc_ref = refs[pos]
        j, t = pl.program_id(0), pl.program_id(1)

        @pl.when(t == 0)
        def _():
            acc_ref[...] = jnp.zeros_like(acc_ref)

        lv = l_ref[...]
        rv = r_ref[...]
        acc_ref[...] += lax.dot_general(lv, rv.astype(BF16), TN_DIMS, preferred_element_type=F32)
        if colsum_l:
            @pl.when(t == 0)
            def _():
                cl_ref[...] = jnp.zeros_like(cl_ref)

            cl_ref[...] += jnp.sum(lv.astype(F32), axis=0, keepdims=True)
        if colsum_r:
            @pl.when((t == 0) & (j == 0))
            def _():
                cr_ref[...] = jnp.zeros_like(cr_ref)

            @pl.when(j == 0)
            def _():
                cr_ref[...] += jnp.sum(rv.astype(F32), axis=0, keepdims=True)

        @pl.when(t == n_t - 1)
        def _():
            o_ref[...] = acc_ref[...].astype(BF16)

    out_shape = [jax.ShapeDtypeStruct((nl, nr), BF16)]
    out_specs = [pl.BlockSpec((tn, nr), lambda j, t: (j, 0))]
    if colsum_l:
        out_shape.append(jax.ShapeDtypeStruct((1, nl), F32))
        out_specs.append(pl.BlockSpec((1, tn), lambda j, t: (0, j)))
    if colsum_r:
        out_shape.append(jax.ShapeDtypeStruct((1, nr), F32))
        out_specs.append(pl.BlockSpec((1, nr), lambda j, t: (0, 0)))
    outs = pl.pallas_call(
        body, grid=(nl // tn, n_t),
        in_specs=[pl.BlockSpec((tt, tn), lambda j, t: (t, j)), pl.BlockSpec((tt, nr), lambda j, t: (t, 0))],
        out_specs=out_specs, out_shape=out_shape, scratch_shapes=[pltpu.VMEM((tn, nr), F32)],
        compiler_params=_params("arbitrary", "arbitrary"), name=name)(l, r)
    return outs if (colsum_l or colsum_r) else outs[0]


def _rms_bwd(dh, xf, g):
    r = lax.rsqrt(jnp.mean(xf * xf, axis=-1, keepdims=True) + RMS_EPS)
    gd = dh * g
    dx = r * gd - xf * ((r * r * r) * jnp.mean(xf * gd, axis=-1, keepdims=True))
    return dx, dh * (xf * r)


def _dgrad_rms(name, dy, w, x, g, dres, *, tm, tk):
    m, k = dy.shape
    tm = min(tm, m)
    n_k = k // tk

    def body(dy_ref, w_ref, x_ref, g_ref, dres_ref, o_ref, dg_ref, acc_ref):
        i, kk = pl.program_id(0), pl.program_id(1)

        @pl.when(kk == 0)
        def _():
            acc_ref[...] = jnp.zeros_like(acc_ref)

        acc_ref[...] += lax.dot_general(dy_ref[...], w_ref[...], NN_DIMS, preferred_element_type=F32)

        @pl.when((i == 0) & (kk == n_k - 1))
        def _():
            dg_ref[...] = jnp.zeros_like(dg_ref)

        @pl.when(kk == n_k - 1)
        def _():
            dx, dg_rows = _rms_bwd(acc_ref[...], x_ref[...], g_ref[...])
            o_ref[...] = dres_ref[...] + dx
            dg_ref[...] += jnp.sum(dg_rows, axis=0, keepdims=True)

    return pl.pallas_call(
        body, grid=(m // tm, n_k),
        in_specs=[pl.BlockSpec((tm, tk), lambda i, kk: (i, kk)), pl.BlockSpec((tk, D), lambda i, kk: (kk, 0)),
                  pl.BlockSpec((tm, D), lambda i, kk: (i, 0)), pl.BlockSpec((1, D), lambda i, kk: (0, 0)),
                  pl.BlockSpec((tm, D), lambda i, kk: (i, 0))],
        out_specs=[pl.BlockSpec((tm, D), lambda i, kk: (i, 0)), pl.BlockSpec((1, D), lambda i, kk: (0, 0))],
        out_shape=[jax.ShapeDtypeStruct((m, D), F32), jax.ShapeDtypeStruct((1, D), F32)],
        scratch_shapes=[pltpu.VMEM((tm, D), F32)],
        compiler_params=_params("arbitrary", "arbitrary"), name=name)(dy, w, x, g, dres)


def _loss_head(x4, g, tgt, *, tm):
    m = x4.shape[0]
    tm = min(tm, m)

    def body(x_ref, g_ref, t_ref, dx_ref, loss_ref, dg_ref):
        @pl.when(pl.program_id(0) == 0)
        def _():
            loss_ref[...] = jnp.zeros_like(loss_ref)
            dg_ref[...] = jnp.zeros_like(dg_ref)

        xf = x_ref[...]
        gv = g_ref[...]
        r = lax.rsqrt(jnp.mean(xf * xf, axis=-1, keepdims=True) + RMS_EPS)
        diff = (xf * r) * gv - t_ref[...]
        loss_ref[...] += 0.5 * jnp.sum(jnp.mean(diff * diff, axis=-1, keepdims=True))
        dx, dg_rows = _rms_bwd(diff * (1.0 / D), xf, gv)
        dx_ref[...] = dx
        dg_ref[...] += jnp.sum(dg_rows, axis=0, keepdims=True)

    return pl.pallas_call(
        body, grid=(m // tm,),
        in_specs=[pl.BlockSpec((tm, D), lambda i: (i, 0)), pl.BlockSpec((1, D), lambda i: (0, 0)),
                  pl.BlockSpec((tm, D), lambda i: (i, 0))],
        out_specs=[pl.BlockSpec((tm, D), lambda i: (i, 0)), pl.BlockSpec((1, 128), lambda i: (0, 0)),
                   pl.BlockSpec((1, D), lambda i: (0, 0))],
        out_shape=[jax.ShapeDtypeStruct((m, D), F32), jax.ShapeDtypeStruct((1, 128), F32),
                   jax.ShapeDtypeStruct((1, D), F32)],
        compiler_params=_params("arbitrary"), name="loss_head")(x4, g, tgt)


def _band(kvp, kvc):
    kk = jnp.concatenate([kvp[:, :128], kvc[:, :128]], axis=0)
    vv = jnp.concatenate([kvp[:, 128:], kvc[:, 128:]], axis=0)
    return kk, vv


def _blockdiag(t, h):
    lane = lax.broadcasted_iota(jnp.int32, t.shape, 1)
    z = jnp.zeros_like(t)
    tr = pltpu.roll(t, 64, 1)
    if h == 0:
        top, bot = jnp.where(lane < 64, t, z), jnp.where(lane >= 64, tr, z)
    else:
        top, bot = jnp.where(lane < 64, tr, z), jnp.where(lane >= 64, t, z)
    return jnp.concatenate([top, bot], axis=0)


def _from_blockdiag(t, h):
    lane = lax.broadcasted_iota(jnp.int32, (256, 128), 1)
    top, bot = t[:256], t[256:]
    if h == 0:
        return jnp.where(lane < 64, top + pltpu.roll(bot, 64, 1), 0.0)
    return jnp.where(lane >= 64, pltpu.roll(top, 64, 1) + bot, 0.0)


def _stack_heads(ref, h):
    return jnp.concatenate([ref[:, h * 512 + p * 128: h * 512 + (p + 1) * 128] for p in range(4)], axis=0)


def _valid_mask(n):
    row = lax.broadcasted_iota(jnp.int32, (512, 256), 0) & 127
    col = lax.broadcasted_iota(jnp.int32, (512, 256), 1)
    return (col > row) & (col <= row + BLOCK) & ((col >= BLOCK) | (n > 0))


def _softmax_half(s_half, valid, sink):
    s_half = jnp.where(valid, s_half, NEG)
    mx = jnp.maximum(jnp.max(s_half, axis=-1, keepdims=True), sink)
    p = jnp.exp(s_half - mx)
    e_sink = jnp.exp(sink - mx)
    inv = 1.0 / (jnp.sum(p, axis=-1, keepdims=True) + e_sink)
    return p * inv, e_sink * inv


def _attn_fwd(qkv, sink_cols):
    s = qkv.shape[0]
    nb = s // BLOCK
    scale = 1.0 / math.sqrt(HEAD_DIM)

    def body(q_ref, kvc_ref, kvp_ref, sk_ref, o_ref):
        n = pl.program_id(0)
        kk, vv = _band(kvp_ref[...], kvc_ref[...])
        valid = _valid_mask(n)
        for h in range(N_KV):
            kbd, vbd = _blockdiag(kk, h), _blockdiag(vv, h)
            qp = _stack_heads(q_ref, h)
            sc = lax.dot_general(qp, kbd, NT_DIMS, preferred_element_type=F32) * scale
            probs = []
            for half in range(2):
                sink = sk_ref[h * 2 + half][:, :1]
                pr, _ = _softmax_half(sc[:, half * 256:(half + 1) * 256], valid, sink)
                probs.append(pr.astype(BF16))
            o = lax.dot_general(jnp.concatenate(probs, axis=1), vbd, NN_DIMS, preferred_element_type=F32)
            for p in range(4):
                o_ref[:, h * 512 + p * 128: h * 512 + (p + 1) * 128] = o[p * 128:(p + 1) * 128].astype(BF16)

    return pl.pallas_call(
        body, grid=(nb,),
        in_specs=[pl.BlockSpec((BLOCK, NQ), lambda n: (n, 0)),
                  pl.BlockSpec((BLOCK, KVW), lambda n: (n, NQ // KVW)),
                  pl.BlockSpec((BLOCK, KVW), lambda n: (jnp.maximum(n - 1, 0), NQ // KVW)),
                  pl.BlockSpec((4, 512, 128), lambda n: (0, 0, 0))],
        out_specs=pl.BlockSpec((BLOCK, NQ), lambda n: (n, 0)),
        out_shape=jax.ShapeDtypeStruct((s, NQ), BF16),
        compiler_params=_params("parallel"), name="attn_fwd")(qkv, qkv, qkv, sink_cols)


def _attn_bwd(qkv, do, sink_cols):
    s = qkv.shape[0]
    nb = s // BLOCK
    scale = 1.0 / math.sqrt(HEAD_DIM)

    def body(q_ref, kvc_ref, kvp_ref, do_ref, sk_ref, dq_ref, dkv_ref, dsk_ref, ck_ref, cv_ref):
        n = pl.program_id(0)

        @pl.when(n == 0)
        def _():
            dsk_ref[...] = jnp.zeros_like(dsk_ref)
            ck_ref[...] = jnp.zeros_like(ck_ref)
            cv_ref[...] = jnp.zeros_like(cv_ref)

        @pl.when(n < nb)
        def _():
            kk, vv = _band(kvp_ref[...], kvc_ref[...])
            valid = _valid_mask(n)
            dkk = jnp.zeros((256, 128), F32)
            dvv = jnp.zeros((256, 128), F32)
            for h in range(N_KV):
                kbd, vbd = _blockdiag(kk, h), _blockdiag(vv, h)
                qp = _stack_heads(q_ref, h)
                dop = _stack_heads(do_ref, h)
                sc = lax.dot_general(qp, kbd, NT_DIMS, preferred_element_type=F32) * scale
                dp = lax.dot_general(dop, vbd, NT_DIMS, preferred_element_type=F32)
                probs, dss = [], []
                for half in range(2):
                    sink = sk_ref[h * 2 + half][:, :1]
                    pr, p_sink = _softmax_half(sc[:, half * 256:(half + 1) * 256], valid, sink)
                    dph = dp[:, half * 256:(half + 1) * 256]
                    delta = jnp.sum(pr * dph, axis=-1, keepdims=True)
                    dss.append((pr * (dph - delta) * scale).astype(BF16))
                    probs.append(pr.astype(BF16))
                    dsk_ref[h * 2 + half] += -(p_sink * delta)
                ds = jnp.concatenate(dss, axis=1)
                pb = jnp.concatenate(probs, axis=1)
                dqp = lax.dot_general(ds, kbd, NN_DIMS, preferred_element_type=F32)
                for p in range(4):
                    dq_ref[:, h * 512 + p * 128: h * 512 + (p + 1) * 128] = dqp[p * 128:(p + 1) * 128].astype(BF16)
                dkk = dkk + _from_blockdiag(lax.dot_general(ds, qp, TN_DIMS, preferred_element_type=F32), h)
                dvv = dvv + _from_blockdiag(lax.dot_general(pb, dop, TN_DIMS, preferred_element_type=F32), h)

            @pl.when(n >= 1)
            def _():
                dkv_ref[:, :128] = (ck_ref[...] + dkk[:128]).astype(BF16)
                dkv_ref[:, 128:] = (cv_ref[...] + dvv[:128]).astype(BF16)

            ck_ref[...] = dkk[128:]
            cv_ref[...] = dvv[128:]

        @pl.when(n == nb)
        def _():
            dkv_ref[:, :128] = ck_ref[...].astype(BF16)
            dkv_ref[:, 128:] = cv_ref[...].astype(BF16)

    last = nb - 1
    return pl.pallas_call(
        body, grid=(nb + 1,),
        in_specs=[pl.BlockSpec((BLOCK, NQ), lambda n: (jnp.minimum(n, last), 0)),
                  pl.BlockSpec((BLOCK, KVW), lambda n: (jnp.minimum(n, last), NQ // KVW)),
                  pl.BlockSpec((BLOCK, KVW), lambda n: (jnp.clip(n - 1, 0, last), NQ // KVW)),
                  pl.BlockSpec((BLOCK, NQ), lambda n: (jnp.minimum(n, last), 0)),
                  pl.BlockSpec((4, 512, 128), lambda n: (0, 0, 0))],
        out_specs=[pl.BlockSpec((BLOCK, NQ), lambda n: (jnp.minimum(n, last), 0)),
                   pl.BlockSpec((BLOCK, KVW), lambda n: (jnp.maximum(n - 1, 0), 0)),
                   pl.BlockSpec((4, 512, 1), lambda n: (0, 0, 0))],
        out_shape=[jax.ShapeDtypeStruct((s, NQ), BF16), jax.ShapeDtypeStruct((s, KVW), BF16),
                   jax.ShapeDtypeStruct((4, 512, 1), F32)],
        scratch_shapes=[pltpu.VMEM((BLOCK, 128), F32), pltpu.VMEM((BLOCK, 128), F32)],
        compiler_params=_params("arbitrary"), name="attn_bwd")(qkv, qkv, qkv, do, sink_cols)


FFN_CHUNK = 256
FFN_HALO = 8


def _ffn_act_fwd(name, up, wdw, bdw, *, tt):
    s = up.shape[0]
    tt = min(tt, s)
    hb = tt // FFN_HALO

    def body(up_ref, prev_ref, w_ref, b_ref, o_ref, ext_ref):
        i = pl.program_id(0)
        for c in range(F // FFN_CHUNK):
            cs = slice(c * FFN_CHUNK, (c + 1) * FFN_CHUNK)
            vs = slice(F + c * FFN_CHUNK, F + (c + 1) * FFN_CHUNK)
            g = up_ref[:, cs].astype(F32)
            ext_ref[pl.ds(0, FFN_HALO), :] = jnp.where(i > 0, prev_ref[:, cs].astype(F32), 0.0)
            ext_ref[pl.ds(FFN_HALO, tt), :] = g
            gate = (w_ref[2:3, cs] * g + w_ref[1:2, cs] * ext_ref[pl.ds(FFN_HALO - 1, tt), :]
                    + w_ref[0:1, cs] * ext_ref[pl.ds(FFN_HALO - 2, tt), :] + b_ref[:, cs])
            o_ref[:, cs] = (gate * _sigmoid(gate) * up_ref[:, vs].astype(F32)).astype(BF16)

    return pl.pallas_call(
        body, grid=(s // tt,),
        in_specs=[pl.BlockSpec((tt, 2 * F), lambda i: (i, 0)),
                  pl.BlockSpec((FFN_HALO, 2 * F), lambda i: (jnp.maximum(i * hb - 1, 0), 0)),
                  pl.BlockSpec((3, F), lambda i: (0, 0)), pl.BlockSpec((1, F), lambda i: (0, 0))],
        out_specs=pl.BlockSpec((tt, F), lambda i: (i, 0)),
        out_shape=jax.ShapeDtypeStruct((s, F), BF16),
        scratch_shapes=[pltpu.VMEM((tt + FFN_HALO, FFN_CHUNK), F32)],
        compiler_params=_params("parallel"), name=name)(up, up, wdw, bdw)


def _ffn_act_bwd(name, up, da, wdw, bdw, *, tt):
    s = up.shape[0]
    tt = min(tt, s)
    hb = tt // FFN_HALO
    n_i = s // tt
    te = tt + FFN_HALO

    def body(up_ref, prev_ref, next_ref, da_ref, dan_ref, w_ref, b_ref, dup_ref, dwb_ref, extg_ref, extd_ref):
        i = pl.program_id(0)

        @pl.when(i == 0)
        def _():
            dwb_ref[...] = jnp.zeros_like(dwb_ref)

        row = lax.broadcasted_iota(jnp.int32, (te, FFN_CHUNK), 0)
        live = (row < tt) | (i < n_i - 1)
        for c in range(F // FFN_CHUNK):
            cs = slice(c * FFN_CHUNK, (c + 1) * FFN_CHUNK)
            vs = slice(F + c * FFN_CHUNK, F + (c + 1) * FFN_CHUNK)
            w0, w1, w2 = w_ref[0:1, cs], w_ref[1:2, cs], w_ref[2:3, cs]
            g = up_ref[:, cs].astype(F32)
            extg_ref[pl.ds(0, FFN_HALO), :] = jnp.where(i > 0, prev_ref[:, cs].astype(F32), 0.0)
            extg_ref[pl.ds(FFN_HALO, tt), :] = g
            extg_ref[pl.ds(FFN_HALO + tt, FFN_HALO), :] = next_ref[:, cs].astype(F32)
            gate = (w2 * extg_ref[pl.ds(FFN_HALO, te), :] + w1 * extg_ref[pl.ds(FFN_HALO - 1, te), :]
                    + w0 * extg_ref[pl.ds(FFN_HALO - 2, te), :] + b_ref[:, cs])
            sg = _sigmoid(gate)
            v_e = jnp.concatenate([up_ref[:, vs], next_ref[:, vs]], axis=0).astype(F32)
            da_e = jnp.concatenate([da_ref[:, cs], dan_ref[:, cs]], axis=0).astype(F32)
            dgate = jnp.where(live, da_e * v_e * (sg * (1.0 + gate * (1.0 - sg))), 0.0)
            extd_ref[...] = dgate
            dup_ref[:, vs] = (da_e[:tt] * (gate[:tt] * sg[:tt])).astype(BF16)
            dg0 = dgate[:tt]
            dup_ref[:, cs] = (w2 * dg0 + w1 * extd_ref[pl.ds(1, tt), :] + w0 * extd_ref[pl.ds(2, tt), :]).astype(BF16)
            dwb_ref[2:3, cs] += jnp.sum(dg0 * g, axis=0, keepdims=True)
            dwb_ref[1:2, cs] += jnp.sum(dg0 * extg_ref[pl.ds(FFN_HALO - 1, tt), :], axis=0, keepdims=True)
            dwb_ref[0:1, cs] += jnp.sum(dg0 * extg_ref[pl.ds(FFN_HALO - 2, tt), :], axis=0, keepdims=True)
            dwb_ref[3:4, cs] += jnp.sum(dg0, axis=0, keepdims=True)

    last_h = s // FFN_HALO - 1
    return pl.pallas_call(
        body, grid=(n_i,),
        in_specs=[pl.BlockSpec((tt, 2 * F), lambda i: (i, 0)),
                  pl.BlockSpec((FFN_HALO, 2 * F), lambda i: (jnp.maximum(i * hb - 1, 0), 0)),
                  pl.BlockSpec((FFN_HALO, 2 * F), lambda i: (jnp.minimum((i + 1) * hb, last_h), 0)),
                  pl.BlockSpec((tt, F), lambda i: (i, 0)),
                  pl.BlockSpec((FFN_HALO, F), lambda i: (jnp.minimum((i + 1) * hb, last_h), 0)),
                  pl.BlockSpec((3, F), lambda i: (0, 0)), pl.BlockSpec((1, F), lambda i: (0, 0))],
        out_specs=[pl.BlockSpec((tt, 2 * F), lambda i: (i, 0)), pl.BlockSpec((8, F), lambda i: (0, 0))],
        out_shape=[jax.ShapeDtypeStruct((s, 2 * F), BF16), jax.ShapeDtypeStruct((8, F), F32)],
        scratch_shapes=[pltpu.VMEM((tt + 2 * FFN_HALO, FFN_CHUNK), F32), pltpu.VMEM((te, FFN_CHUNK), F32)],
        compiler_params=_params("arbitrary"), name=name)(up, up, up, da, da, wdw, bdw)


CONV_HALO = 32
CONV_CHUNK = 256


def _ln(u2, g, b):
    mu = jnp.mean(u2, axis=-1, keepdims=True)
    xc = u2 - mu
    rstd = lax.rsqrt(jnp.mean(xc * xc, axis=-1, keepdims=True) + LN_EPS)
    xhat = xc * rstd
    return xhat, rstd, xhat * g + b


def _conv_mid_fwd(a, wdw, ln_g, ln_b, *, tt):
    s = a.shape[0]
    tt = min(tt, s)
    hb = tt // CONV_HALO

    def body(a_ref, prev_ref, w_ref, g_ref, b_ref, s_ref, u2_ref, ext_ref):
        i = pl.program_id(0)
        for c in range(D // CONV_CHUNK):
            cs = slice(c * CONV_CHUNK, (c + 1) * CONV_CHUNK)
            gs = slice(D + c * CONV_CHUNK, D + (c + 1) * CONV_CHUNK)
            uh = prev_ref[:, cs].astype(F32) * _sigmoid(prev_ref[:, gs].astype(F32))
            ext_ref[pl.ds(0, CONV_HALO), :] = jnp.where(i > 0, uh, 0.0)
            ext_ref[pl.ds(CONV_HALO, tt), :] = a_ref[:, cs].astype(F32) * _sigmoid(a_ref[:, gs].astype(F32))
            acc = jnp.zeros((tt, CONV_CHUNK), F32) + w_ref[CONV_K:CONV_K + 1, cs]
            for k in range(CONV_K):
                acc = acc + w_ref[k:k + 1, cs] * ext_ref[pl.ds(CONV_HALO - (CONV_K - 1) + k, tt), :]
            u2_ref[:, cs] = acc.astype(BF16)
        _, _, ln = _ln(u2_ref[...].astype(F32), g_ref[...], b_ref[...])
        s_ref[...] = (ln * _sigmoid(ln)).astype(BF16)

    return pl.pallas_call(
        body, grid=(s // tt,),
        in_specs=[pl.BlockSpec((tt, 2 * D), lambda i: (i, 0)),
                  pl.BlockSpec((CONV_HALO, 2 * D), lambda i: (jnp.maximum(i * hb - 1, 0), 0)),
                  pl.BlockSpec((32, D), lambda i: (0, 0)), pl.BlockSpec((1, D), lambda i: (0, 0)),
                  pl.BlockSpec((1, D), lambda i: (0, 0))],
        out_specs=[pl.BlockSpec((tt, D), lambda i: (i, 0)), pl.BlockSpec((tt, D), lambda i: (i, 0))],
        out_shape=[jax.ShapeDtypeStruct((s, D), BF16), jax.ShapeDtypeStruct((s, D), BF16)],
        scratch_shapes=[pltpu.VMEM((tt + CONV_HALO, CONV_CHUNK), F32)],
        compiler_params=_params("parallel"), name="conv_mid_fwd")(a, a, wdw, ln_g, ln_b)


def _conv_mid_bwd(ds, u2, a, wdw, ln_g, ln_b, *, tt):
    s = a.shape[0]
    tt = min(tt, s)
    hb = tt // CONV_HALO
    n_i = s // tt
    te = tt + CONV_HALO

    def body(ds_ref, dsn_ref, u2_ref, u2n_ref, a_ref, prev_ref, w_ref, g_ref, b_ref, da_ref, acc_ref, e2_ref, ext_ref):
        i = pl.program_id(0)

        @pl.when(i == 0)
        def _():
            acc_ref[...] = jnp.zeros_like(acc_ref)

        u2e = jnp.concatenate([u2_ref[...], u2n_ref[...]], axis=0).astype(F32)
        dse = jnp.concatenate([ds_ref[...], dsn_ref[...]], axis=0).astype(F32)
        gv = g_ref[...]
        xhat, rstd, ln = _ln(u2e, gv, b_ref[...])
        sg = _sigmoid(ln)
        dln = dse * (sg * (1.0 + ln * (1.0 - sg)))
        acc_ref[32:33, :] += jnp.sum(dln[:tt] * xhat[:tt], axis=0, keepdims=True)
        acc_ref[33:34, :] += jnp.sum(dln[:tt], axis=0, keepdims=True)
        dxh = dln * gv
        du2 = rstd * (dxh - jnp.mean(dxh, axis=-1, keepdims=True) - xhat * jnp.mean(dxh * xhat, axis=-1, keepdims=True))
        row = lax.broadcasted_iota(jnp.int32, (te, D), 0)
        e2_ref[...] = jnp.where((row < tt) | (i < n_i - 1), du2, 0.0)
        for c in range(D // CONV_CHUNK):
            cs = slice(c * CONV_CHUNK, (c + 1) * CONV_CHUNK)
            gs = slice(D + c * CONV_CHUNK, D + (c + 1) * CONV_CHUNK)
            uh = prev_ref[:, cs].astype(F32) * _sigmoid(prev_ref[:, gs].astype(F32))
            ext_ref[pl.ds(0, CONV_HALO), :] = jnp.where(i > 0, uh, 0.0)
            a1 = a_ref[:, cs].astype(F32)
            sg2 = _sigmoid(a_ref[:, gs].astype(F32))
            ext_ref[pl.ds(CONV_HALO, tt), :] = a1 * sg2
            d0 = e2_ref[pl.ds(0, tt), cs]
            du = jnp.zeros((tt, CONV_CHUNK), F32)
            for k in range(CONV_K):
                du = du + w_ref[k:k + 1, cs] * e2_ref[pl.ds(CONV_K - 1 - k, tt), cs]
                acc_ref[k:k + 1, cs] += jnp.sum(
                    d0 * ext_ref[pl.ds(CONV_HALO - (CONV_K - 1) + k, tt), :], axis=0, keepdims=True)
            acc_ref[CONV_K:CONV_K + 1, cs] += jnp.sum(d0, axis=0, keepdims=True)
            da_ref[:, cs] = (du * sg2).astype(BF16)
            da_ref[:, gs] = (du * a1 * (sg2 * (1.0 - sg2))).astype(BF16)

    last_h = s // CONV_HALO - 1
    nxt = lambda i: (jnp.minimum((i + 1) * hb, last_h), 0)
    return pl.pallas_call(
        body, grid=(n_i,),
        in_specs=[pl.BlockSpec((tt, D), lambda i: (i, 0)), pl.BlockSpec((CONV_HALO, D), nxt),
                  pl.BlockSpec((tt, D), lambda i: (i, 0)), pl.BlockSpec((CONV_HALO, D), nxt),
                  pl.BlockSpec((tt, 2 * D), lambda i: (i, 0)),
                  pl.BlockSpec((CONV_HALO, 2 * D), lambda i: (jnp.maximum(i * hb - 1, 0), 0)),
                  pl.BlockSpec((32, D), lambda i: (0, 0)), pl.BlockSpec((1, D), lambda i: (0, 0)),
                  pl.BlockSpec((1, D), lambda i: (0, 0))],
        out_specs=[pl.BlockSpec((tt, 2 * D), lambda i: (i, 0)), pl.BlockSpec((40, D), lambda i: (0, 0))],
        out_shape=[jax.ShapeDtypeStruct((s, 2 * D), BF16), jax.ShapeDtypeStruct((40, D), F32)],
        scratch_shapes=[pltpu.VMEM((te, D), F32), pltpu.VMEM((te, CONV_CHUNK), F32)],
        compiler_params=_params("arbitrary"), name="conv_mid_bwd")(ds, ds, u2, u2, a, a, wdw, ln_g, ln_b)


def _adamw(name, w, g, m, v):
    rows, cols = w.shape
    tr = _row_tile(rows, 256)

    def body(w_ref, g_ref, m_ref, v_ref, d_ref, nm_ref, nv_ref):
        gv = g_ref[...]
        m2 = ADAM_B1 * m_ref[...] + (1.0 - ADAM_B1) * gv
        v2 = ADAM_B2 * v_ref[...] + (1.0 - ADAM_B2) * (gv * gv)
        m_hat = m2 / (1.0 - ADAM_B1 ** ADAM_STEP)
        v_hat = v2 / (1.0 - ADAM_B2 ** ADAM_STEP)
        d_ref[...] = -ADAM_LR * (m_hat / (jnp.sqrt(v_hat) + ADAM_EPS) + ADAM_WD * w_ref[...])
        nm_ref[...] = m2
        nv_ref[...] = v2

    spec = pl.BlockSpec((tr, cols), lambda i: (i, 0))
    return pl.pallas_call(
        body, grid=(rows // tr,), in_specs=[spec] * 4, out_specs=[spec] * 3,
        out_shape=[jax.ShapeDtypeStruct((rows, cols), F32)] * 3,
        compiler_params=_params("parallel"), name=name)(w, g, m, v)


def _sum_slabs(buf):
    rows = buf.shape[1]
    tr = _row_tile(rows, 512, 16)

    def body(b_ref, o_ref):
        acc = b_ref[0].astype(F32)
        for k in range(1, N_DEV):
            acc = acc + b_ref[k].astype(F32)
        o_ref[...] = acc

    return pl.pallas_call(
        body, grid=(rows // tr,), in_specs=[pl.BlockSpec((N_DEV, tr, D), lambda i: (0, i, 0))],
        out_specs=pl.BlockSpec((tr, D), lambda i: (i, 0)), out_shape=jax.ShapeDtypeStruct((rows, D), F32),
        compiler_params=_params("parallel"), name="sum_slabs")(buf)


def _me():
    x, y, c = lax.axis_index("x"), lax.axis_index("y"), lax.axis_index("c")
    return x, y, c, 4 * x + 2 * y + c


def _peer(x, y, c, k):
    return (x ^ (k >> 2), y ^ ((k >> 1) & 1), c ^ (k & 1))


def _all_gather_weights(slab, small):
    any_spec = pl.BlockSpec(memory_space=pl.ANY)
    vmem_spec = pl.BlockSpec(memory_space=pltpu.VMEM)

    def body(slab_ref, small_ref, *rest):
        outs = rest[:N_MAT]
        sm_out, send_sems, recv_sems, sm_send, sm_recv, local_sem = rest[N_MAT:]
        x, y, c, me = _me()
        sibling = (x, y, 1 - c)
        chips = [(1 - x, y), (x, 1 - y), (1 - x, 1 - y)]

        def rows_of(i, dev):
            return outs[i].at[pl.ds(dev * MAT_ROWS[i], MAT_ROWS[i])]

        def copies(k, dev, to, from_slab):
            return [pltpu.make_async_remote_copy(
                src_ref=slab_ref.at[pl.ds(MAT_OFF[i], MAT_ROWS[i])] if from_slab else rows_of(i, dev),
                dst_ref=rows_of(i, dev), send_sem=send_sems.at[k], recv_sem=recv_sems.at[k],
                device_id=to, device_id_type=MESH) for i in range(N_MAT)]

        whole = outs[4].at[pl.ds(0, SLAB_ROWS)]

        def waiter(k):
            return pltpu.make_async_remote_copy(src_ref=whole, dst_ref=whole, send_sem=send_sems.at[k],
                                                recv_sem=recv_sems.at[k], device_id=sibling, device_id_type=MESH)

        mine = [pltpu.make_async_copy(slab_ref.at[pl.ds(MAT_OFF[i], MAT_ROWS[i])], rows_of(i, me), local_sem)
                for i in range(N_MAT)]
        for cp in mine:
            cp.start()
        for cp in copies(0, me, sibling, True):
            cp.start()
        for j, chip in enumerate(chips):
            for cp in copies(1 + j, me, (*chip, c), True):
                cp.start()
        sm_out[me] = small_ref[...]
        small_copies = [pltpu.make_async_remote_copy(
            src_ref=small_ref, dst_ref=sm_out.at[me], send_sem=sm_send.at[k - 1], recv_sem=sm_recv.at[k - 1],
            device_id=_peer(x, y, c, k), device_id_type=MESH) for k in range(1, N_DEV)]
        for cp in small_copies:
            cp.start()
        for j, chip in enumerate(chips):
            waiter(1 + j).wait_recv()
            dev = 4 * chip[0] + 2 * chip[1] + c
            for cp in copies(4 + j, dev, sibling, False):
                cp.start()
        waiter(0).wait_recv()
        for j in range(3):
            waiter(4 + j).wait_recv()
        for k in range(7):
            waiter(k).wait_send()
        for cp in small_copies:
            cp.wait()
        pltpu.make_async_copy(whole, whole, local_sem).wait()

    out_shape = [jax.ShapeDtypeStruct((N_DEV * r, D), BF16) for r in MAT_ROWS]
    out_shape.append(jax.ShapeDtypeStruct((N_DEV,) + small.shape, F32))
    return pl.pallas_call(
        body, in_specs=[any_spec, vmem_spec], out_specs=[any_spec] * N_MAT + [vmem_spec], out_shape=out_shape,
        scratch_shapes=[pltpu.SemaphoreType.DMA((7,)), pltpu.SemaphoreType.DMA((7,)),
                        pltpu.SemaphoreType.DMA((7,)), pltpu.SemaphoreType.DMA((7,)), pltpu.SemaphoreType.DMA],
        compiler_params=pltpu.CompilerParams(vmem_limit_bytes=VMEM_LIMIT_BYTES), name="all_gather_weights")(slab, small)


def _reduce_scatter_grads(dws, small):
    any_spec = pl.BlockSpec(memory_space=pl.ANY)
    vmem_spec = pl.BlockSpec(memory_space=pltpu.VMEM)
    n_small = small.shape[0]

    def body(*refs):
        dw_refs = refs[:N_MAT]
        small_ref, buf_ref, ssum_ref, sbuf_ref, send_sems, recv_sems, sm_send, sm_recv, local_sem = refs[N_MAT:]
        x, y, c, me = _me()

        def piece(i, dev):
            return dw_refs[i].at[pl.ds(dev * MAT_ROWS[i], MAT_ROWS[i])]

        def slot(k, i):
            return buf_ref.at[k, pl.ds(MAT_OFF[i], MAT_ROWS[i])]

        mine = [pltpu.make_async_copy(piece(i, me), slot(0, i), local_sem) for i in range(N_MAT)]
        for cp in mine:
            cp.start()
        small_copies = []
        for k in range(1, N_DEV):
            peer = _peer(x, y, c, k)
            for i in range(N_MAT):
                pltpu.make_async_remote_copy(src_ref=piece(i, me ^ k), dst_ref=slot(k, i), send_sem=send_sems.at[k - 1],
                                             recv_sem=recv_sems.at[k - 1], device_id=peer, device_id_type=MESH).start()
            small_copies.append(pltpu.make_async_remote_copy(
                src_ref=small_ref, dst_ref=sbuf_ref.at[me], send_sem=sm_send.at[k - 1], recv_sem=sm_recv.at[k - 1],
                device_id=peer, device_id_type=MESH))
            small_copies[-1].start()
        sbuf_ref[me] = small_ref[...]
        for cp in small_copies:
            cp.wait()
        acc = sbuf_ref[0]
        for d in range(1, N_DEV):
            acc = acc + sbuf_ref[d]
        ssum_ref[...] = acc
        whole = buf_ref.at[0]
        for k in range(1, N_DEV):
            pltpu.make_async_remote_copy(src_ref=whole, dst_ref=whole, send_sem=send_sems.at[k - 1],
                                         recv_sem=recv_sems.at[k - 1], device_id=_peer(x, y, c, k),
                                         device_id_type=MESH).wait()
        pltpu.make_async_copy(whole, whole, local_sem).wait()

    return pl.pallas_call(
        body, in_specs=[any_spec] * N_MAT + [vmem_spec], out_specs=[any_spec, vmem_spec],
        out_shape=[jax.ShapeDtypeStruct((N_DEV, SLAB_ROWS, D), BF16), jax.ShapeDtypeStruct((n_small, D), F32)],
        scratch_shapes=[pltpu.VMEM((N_DEV, n_small, D), F32),
                        pltpu.SemaphoreType.DMA((7,)), pltpu.SemaphoreType.DMA((7,)),
                        pltpu.SemaphoreType.DMA((7,)), pltpu.SemaphoreType.DMA((7,)), pltpu.SemaphoreType.DMA],
        compiler_params=pltpu.CompilerParams(vmem_limit_bytes=VMEM_LIMIT_BYTES), name="reduce_scatter_grads")(*dws, small)


def _pack_rows(arrs):
    rows = []
    for a in arrs:
        flat = a.reshape(-1).astype(F32)
        pad = (-flat.shape[0]) % D
        rows.append(jnp.pad(flat, (0, pad)).reshape(-1, D))
    slab = jnp.concatenate(rows, axis=0)
    return jnp.pad(slab, ((0, (-slab.shape[0]) % 8), (0, 0)))


def _unpack_rows(slab, shapes):
    out, r = [], 0
    for shp in shapes:
        size = math.prod(shp)
        nrows = -(-size // D)
        out.append(slab[r:r + nrows].reshape(-1)[:size].reshape(shp))
        r += nrows
    return out


def _ffn_fwd(tag, x_in, g, w_up_t, w_dw, b_dw, w_down):
    up, h = _mm(f"ffn{tag}_up", x_in, w_up_t, nt=True, tm=512, tn=1408, out_dtype=BF16, rms_g=g)
    act = _ffn_act_fwd(f"ffn{tag}_act", up, w_dw, b_dw, tt=256)
    x_out = _mm(f"ffn{tag}_down", act, w_down, nt=False, tm=512, tn=D, out_dtype=F32, res=x_in)
    return x_out, (h, up, act)


def _ffn_bwd(tag, dx_out, x_in, g, w_up_t, w_dw, b_dw, w_down, saved):
    h, up, act = saved
    d_act = _mm(f"ffn{tag}_dact", dx_out, w_down, nt=True, tm=512, tn=1408, out_dtype=BF16)
    d_up, dwb = _ffn_act_bwd(f"ffn{tag}_dup", up, d_act, w_dw, b_dw, tt=256)
    dw_down = _mm_tn(f"ffn{tag}_dwdown", act, dx_out, tn=1408, tt=512)
    dx_in, dg = _dgrad_rms(f"ffn{tag}_dx", d_up, w_up_t, x_in, g, dx_out, tm=512, tk=1408)
    dw_up_t = _mm_tn(f"ffn{tag}_dwup", d_up, h, tn=1408, tt=512)
    return dx_in, dg, dw_up_t, dw_down, dwb


def kernel(x, norm_mix, attn_w_qkv, attn_b_qkv, attn_sinks, attn_w_o, attn_b_o, conv_w_pw1, conv_b_pw1, conv_w_dw, conv_b_dw, conv_ln_g, conv_ln_b, conv_w_pw2, conv_b_pw2, norm_ffn, ffn_w_up, ffn_w_dw, ffn_b_dw, ffn_w_down, final_norm, loss_target, m_norm_mix, m_attn_w_qkv, m_attn_b_qkv, m_attn_sinks, m_attn_w_o, m_attn_b_o, m_conv_w_pw1, m_conv_b_pw1, m_conv_w_dw, m_conv_b_dw, m_conv_ln_g, m_conv_ln_b, m_conv_w_pw2, m_conv_b_pw2, m_norm_ffn, m_ffn_w_up, m_ffn_w_dw, m_ffn_b_dw, m_ffn_w_down, m_final_norm, v_norm_mix, v_attn_w_qkv, v_attn_b_qkv, v_attn_sinks, v_attn_w_o, v_attn_b_o, v_conv_w_pw1, v_conv_b_pw1, v_conv_w_dw, v_conv_b_dw, v_conv_ln_g, v_conv_ln_b, v_conv_w_pw2, v_conv_b_pw2, v_norm_ffn, v_ffn_w_up, v_ffn_w_dw, v_ffn_b_dw, v_ffn_w_down, v_final_norm):
    s = x.shape[1]
    me = 4 * lax.axis_index("x") + 2 * lax.axis_index("y") + lax.axis_index("c")
    x0 = x.reshape(s, D)
    tgt = loss_target.reshape(s, D)

    slab = jnp.concatenate([attn_w_qkv[0].T, attn_w_o[0], conv_w_pw1[0].T, conv_w_pw2[0],
                            ffn_w_up[0].T, ffn_w_down[0], ffn_w_up[1].T, ffn_w_down[1]], axis=0).astype(BF16)
    sharded_small = [conv_b_pw1, conv_w_dw, conv_b_dw, conv_ln_g, conv_ln_b, conv_b_pw2, ffn_w_dw]
    small_local = _pack_rows(sharded_small)
    *mats, small_all = _all_gather_weights(slab, small_local)
    w_qkv_t, w_o, w_pw1_t, w_pw2, w_up0_t, w_dn0, w_up1_t, w_dn1 = mats
    parts = [_unpack_rows(small_all[d], [a.shape for a in sharded_small]) for d in range(N_DEV)]
    b_pw1, w_cdw, b_cdw, ln_g, ln_b, b_pw2, w_fdw = [jnp.concatenate([parts[d][i] for d in range(N_DEV)], axis=-1)
                                                      for i in range(len(sharded_small))]
    conv_w32 = jnp.concatenate([w_cdw[0], b_cdw], axis=0)
    sr = attn_sinks.reshape(N_KV, 4, 2)
    sink_cols = jnp.broadcast_to(jnp.repeat(jnp.transpose(sr, (0, 2, 1)), BLOCK, axis=-1).reshape(4, 512, 1),
                                 (4, 512, 128))

    qkv, h0 = _mm("qkv", x0, w_qkv_t, nt=True, tm=512, tn=NQ + KVW, out_dtype=BF16, rms_g=norm_mix[0:1], bias=attn_b_qkv)
    o = _attn_fwd(qkv, sink_cols)
    x1 = _mm("attn_out", o, w_o, nt=False, tm=512, tn=D, out_dtype=F32, bias=attn_b_o, res=x0)
    x2, ffn0 = _ffn_fwd(0, x1, norm_ffn[0:1], w_up0_t, w_fdw[0], ffn_b_dw[0:1], w_dn0)
    a, h2 = _mm("pw1", x2, w_pw1_t, nt=True, tm=512, tn=2 * D, out_dtype=BF16, rms_g=norm_mix[1:2], bias=b_pw1)
    s_act, u2 = _conv_mid_fwd(a, conv_w32, ln_g, ln_b, tt=256)
    x3 = _mm("pw2", s_act, w_pw2, nt=False, tm=512, tn=D, out_dtype=F32, bias=b_pw2, res=x2)
    x4, ffn1 = _ffn_fwd(1, x3, norm_ffn[1:2], w_up1_t, w_fdw[1], ffn_b_dw[1:2], w_dn1)
    dx4, loss_acc, dg_final = _loss_head(x4, final_norm.reshape(1, D), tgt, tm=512)
    loss = lax.psum(loss_acc[0, 0], ("x", "y", "c"))

    dx3, dg_f1, dw_up1_t, dw_dn1, dwb1 = _ffn_bwd(1, dx4, x3, norm_ffn[1:2], w_up1_t, w_fdw[1], ffn_b_dw[1:2], w_dn1, ffn1)
    ds_act = _mm("d_pw2", dx3, w_pw2, nt=True, tm=512, tn=D, out_dtype=BF16)
    dw_pw2, db_pw2 = _mm_tn("dw_pw2", s_act, dx3, tn=D, tt=512, colsum_r=True)
    da, conv_acc = _conv_mid_bwd(ds_act, u2, a, conv_w32, ln_g, ln_b, tt=256)
    dx2, dg_m1 = _dgrad_rms("d_pw1", da, w_pw1_t, x2, norm_mix[1:2], dx3, tm=512, tk=2 * D)
    dw_pw1_t, db_pw1 = _mm_tn("dw_pw1", da, h2, tn=D, tt=512, colsum_l=True)
    dx1, dg_f0, dw_up0_t, dw_dn0, dwb0 = _ffn_bwd(0, dx2, x1, norm_ffn[0:1], w_up0_t, w_fdw[0], ffn_b_dw[0:1], w_dn0, ffn0)
    do = _mm("d_attn_out", dx1, w_o, nt=True, tm=512, tn=D, out_dtype=BF16)
    dw_o, db_o = _mm_tn("dw_o", o, dx1, tn=D, tt=512, colsum_r=True)
    dq, dkv, dsk = _attn_bwd(qkv, do, sink_cols)
    dqkv = jnp.concatenate([dq, dkv], axis=1)
    grad_x, dg_m0 = _dgrad_rms("d_qkv", dqkv, w_qkv_t, x0, norm_mix[0:1], dx1, tm=512, tk=NQ + KVW)
    dw_qkv_t, db_qkv = _mm_tn("dw_qkv", dqkv, h0, tn=NQ + KVW, tt=512, colsum_l=True)

    d_sinks = jnp.transpose(jnp.sum(dsk.reshape(N_KV, 2, 4, BLOCK), axis=-1), (0, 2, 1)).reshape(1, 16)
    small_grads = [jnp.concatenate([dg_m0, dg_m1], axis=0), db_qkv, d_sinks, db_o, jnp.concatenate([dg_f0, dg_f1], axis=0),
                   jnp.stack([dwb0[3], dwb1[3]]), dg_final, db_pw1, conv_acc[0:CONV_K], conv_acc[31:32], conv_acc[32:33],
                   conv_acc[33:34], db_pw2, jnp.stack([dwb0[0:3], dwb1[0:3]])]
    small_shapes = [(2, D), (1, NQ + KVW), (1, 16), (1, D), (2, D), (2, F), (D,), (1, 2 * D), (1, CONV_K, D), (1, D),
                    (1, D), (1, D), (1, D), (2, 3, F)]
    buf, small_sum = _reduce_scatter_grads([dw_qkv_t, dw_o, dw_pw1_t, dw_pw2, dw_up0_t, dw_dn0, dw_up1_t, dw_dn1],
                                           _pack_rows(small_grads))
    gslab = _sum_slabs(buf)
    gm = [gslab[MAT_OFF[i]:MAT_OFF[i] + MAT_ROWS[i]] for i in range(N_MAT)]
    (g_norm_mix, g_b_qkv, g_sinks, g_b_o, g_norm_ffn, g_ffn_b_dw, g_final, g_b_pw1_full, g_w_cdw_full, g_b_cdw_full,
     g_ln_g_full, g_ln_b_full, g_b_pw2_full, g_w_fdw_full) = _unpack_rows(small_sum, small_shapes)

    def shard(a, width):
        return lax.dynamic_slice_in_dim(a, me * width, width, axis=a.ndim - 1)

    grads = {
        "norm_mix": g_norm_mix, "attn_w_qkv": gm[0].T[None], "attn_b_qkv": g_b_qkv, "attn_sinks": g_sinks,
        "attn_w_o": gm[1][None], "attn_b_o": g_b_o, "conv_w_pw1": gm[2].T[None], "conv_b_pw1": shard(g_b_pw1_full, 256),
        "conv_w_dw": shard(g_w_cdw_full, 128), "conv_b_dw": shard(g_b_cdw_full, 128), "conv_ln_g": shard(g_ln_g_full, 128),
        "conv_ln_b": shard(g_ln_b_full, 128), "conv_w_pw2": gm[3][None], "conv_b_pw2": shard(g_b_pw2_full, 128),
        "norm_ffn": g_norm_ffn, "ffn_w_up": jnp.stack([gm[4].T, gm[6].T]), "ffn_w_dw": shard(g_w_fdw_full, 352),
        "ffn_b_dw": g_ffn_b_dw, "ffn_w_down": jnp.stack([gm[5], gm[7]]), "final_norm": g_final,
    }
    weights = dict(norm_mix=norm_mix, attn_w_qkv=attn_w_qkv, attn_b_qkv=attn_b_qkv, attn_sinks=attn_sinks, attn_w_o=attn_w_o, attn_b_o=attn_b_o, conv_w_pw1=conv_w_pw1, conv_b_pw1=conv_b_pw1, conv_w_dw=conv_w_dw, conv_b_dw=conv_b_dw, conv_ln_g=conv_ln_g, conv_ln_b=conv_ln_b, conv_w_pw2=conv_w_pw2, conv_b_pw2=conv_b_pw2, norm_ffn=norm_ffn, ffn_w_up=ffn_w_up, ffn_w_dw=ffn_w_dw, ffn_b_dw=ffn_b_dw, ffn_w_down=ffn_w_down, final_norm=final_norm)
    moms = dict(norm_mix=m_norm_mix, attn_w_qkv=m_attn_w_qkv, attn_b_qkv=m_attn_b_qkv, attn_sinks=m_attn_sinks, attn_w_o=m_attn_w_o, attn_b_o=m_attn_b_o, conv_w_pw1=m_conv_w_pw1, conv_b_pw1=m_conv_b_pw1, conv_w_dw=m_conv_w_dw, conv_b_dw=m_conv_b_dw, conv_ln_g=m_conv_ln_g, conv_ln_b=m_conv_ln_b, conv_w_pw2=m_conv_w_pw2, conv_b_pw2=m_conv_b_pw2, norm_ffn=m_norm_ffn, ffn_w_up=m_ffn_w_up, ffn_w_dw=m_ffn_w_dw, ffn_b_dw=m_ffn_b_dw, ffn_w_down=m_ffn_w_down, final_norm=m_final_norm)
    vars_ = dict(norm_mix=v_norm_mix, attn_w_qkv=v_attn_w_qkv, attn_b_qkv=v_attn_b_qkv, attn_sinks=v_attn_sinks, attn_w_o=v_attn_w_o, attn_b_o=v_attn_b_o, conv_w_pw1=v_conv_w_pw1, conv_b_pw1=v_conv_b_pw1, conv_w_dw=v_conv_w_dw, conv_b_dw=v_conv_b_dw, conv_ln_g=v_conv_ln_g, conv_ln_b=v_conv_ln_b, conv_w_pw2=v_conv_w_pw2, conv_b_pw2=v_conv_b_pw2, norm_ffn=v_norm_ffn, ffn_w_up=v_ffn_w_up, ffn_w_dw=v_ffn_w_dw, ffn_b_dw=v_ffn_b_dw, ffn_w_down=v_ffn_w_down, final_norm=v_final_norm)
    names = list(weights)
    big = ("attn_w_qkv", "attn_w_o", "conv_w_pw1", "conv_w_pw2", "ffn_w_up", "ffn_w_down")
    delta, new_m, new_v = {}, {}, {}
    for nm in big:
        shp = weights[nm].shape
        two_d = (shp[0] * shp[1], shp[2])
        res = _adamw("adamw_" + nm, *(t[nm].reshape(two_d) for t in (weights, grads, moms, vars_)))
        delta[nm], new_m[nm], new_v[nm] = (r.reshape(shp) for r in res)
    small_names = [nm for nm in names if nm not in big]
    small_shapes_local = [weights[nm].shape for nm in small_names]
    res = _adamw("adamw_small", *(_pack_rows([t[nm] for nm in small_names]) for t in (weights, grads, moms, vars_)))
    for tgt_dict, slab_out in zip((delta, new_m, new_v), res):
        for nm, val in zip(small_names, _unpack_rows(slab_out, small_shapes_local)):
            tgt_dict[nm] = val
    return (loss, grad_x.reshape(1, s, D), *[grads[nm] for nm in names], *[delta[nm] for nm in names],
            *[new_m[nm] for nm in names], *[new_v[nm] for nm in names])
```

```python
import functools
import math

import jax
import jax.numpy as jnp
from jax import lax
from jax.experimental import pallas as pl
from jax.experimental.pallas import tpu as pltpu

F32 = jnp.float32
BF16 = jnp.bfloat16

D = 1024
F = 2816
HEAD_DIM = 64
N_KV = 2
BLOCK = 128
NQ = 1024
KVW = 256
CONV_K = 31
RMS_EPS = 1e-6
LN_EPS = 1e-5
ADAM_LR, ADAM_B1, ADAM_B2, ADAM_EPS, ADAM_WD, ADAM_STEP = 0.001, 0.9, 0.999, 1e-08, 0.01, 10
N_DEV = 8
MESH = pl.DeviceIdType.MESH
VMEM_LIMIT_BYTES = 56 * 2**20
NEG = float(jnp.finfo(jnp.float32).min)

MAT_ROWS = (160, 128, 256, 128, 704, 352, 704, 352)
MAT_OFF = tuple(sum(MAT_ROWS[:i]) for i in range(len(MAT_ROWS)))
SLAB_ROWS = sum(MAT_ROWS)
N_MAT = len(MAT_ROWS)

NT_DIMS = (((1,), (1,)), ((), ()))
NN_DIMS = (((1,), (0,)), ((), ()))
TN_DIMS = (((0,), (0,)), ((), ()))


def _params(*sem):
    return pltpu.CompilerParams(dimension_semantics=tuple(sem), vmem_limit_bytes=VMEM_LIMIT_BYTES)


def _row_tile(rows, cap, mult=8):
    best = None
    for t in range(mult, min(rows, cap) + 1, mult):
        if rows % t == 0:
            best = t
    return best if best is not None else rows


def _sigmoid(x):
    return 1.0 / (1.0 + jnp.exp(-x))


def _mm(name, a, b, *, nt, tm, tn, out_dtype, rms_g=None, bias=None, res=None):
    m, k = a.shape
    n = b.shape[0] if nt else b.shape[1]
    tm = min(tm, m)
    has_rms = rms_g is not None
    dims = NT_DIMS if nt else NN_DIMS

    def body(*refs):
        a_ref, b_ref = refs[0], refs[1]
        pos = 2
        g_ref = bias_ref = res_ref = h_ref = hs_ref = None
        if has_rms:
            g_ref = refs[pos]; pos += 1
        if bias is not None:
            bias_ref = refs[pos]; pos += 1
        if res is not None:
            res_ref = refs[pos]; pos += 1
        o_ref = refs[pos]; pos += 1
        if has_rms:
            h_ref, hs_ref = refs[pos], refs[pos + 1]

            @pl.when(pl.program_id(1) == 0)
            def _():
                xf = a_ref[...]
                r = lax.rsqrt(jnp.mean(xf * xf, axis=-1, keepdims=True) + RMS_EPS)
                h = ((xf * r) * g_ref[...]).astype(BF16)
                hs_ref[...] = h
                h_ref[...] = h

            lhs = hs_ref[...]
        else:
            lhs = a_ref[...].astype(BF16)
        acc = lax.dot_general(lhs, b_ref[...], dims, preferred_element_type=F32)
        if bias is not None:
            acc = acc + bias_ref[...]
        if res is not None:
            acc = acc + res_ref[...]
        o_ref[...] = acc.astype(out_dtype)

    in_specs = [pl.BlockSpec((tm, k), lambda i, j: (i, 0)),
                pl.BlockSpec((tn, k), lambda i, j: (j, 0)) if nt else pl.BlockSpec((k, tn), lambda i, j: (0, j))]
    args = [a, b]
    if has_rms:
        in_specs.append(pl.BlockSpec((1, k), lambda i, j: (0, 0))); args.append(rms_g)
    if bias is not None:
        in_specs.append(pl.BlockSpec((1, tn), lambda i, j: (0, j))); args.append(bias)
    if res is not None:
        in_specs.append(pl.BlockSpec((tm, tn), lambda i, j: (i, j))); args.append(res)
    out_shape = [jax.ShapeDtypeStruct((m, n), out_dtype)]
    out_specs = [pl.BlockSpec((tm, tn), lambda i, j: (i, j))]
    scratch = []
    if has_rms:
        out_shape.append(jax.ShapeDtypeStruct((m, k), BF16))
        out_specs.append(pl.BlockSpec((tm, k), lambda i, j: (i, 0)))
        scratch.append(pltpu.VMEM((tm, k), BF16))
    outs = pl.pallas_call(body, grid=(m // tm, n // tn), in_specs=in_specs, out_specs=out_specs, out_shape=out_shape,
                          scratch_shapes=scratch, compiler_params=_params("parallel", "arbitrary"), name=name)(*args)
    return outs if has_rms else outs[0]


def _mm_tn(name, l, r, *, tn, tt, colsum_l=False, colsum_r=False):
    s, nl = l.shape
    nr = r.shape[1]
    tt = min(tt, s)
    n_t = s // tt

    def body(*refs):
        l_ref, r_ref, o_ref = refs[0], refs[1], refs[2]
        pos = 3
        cl_ref = cr_ref = None
        if colsum_l:
            cl_ref = refs[pos]; pos += 1
        if colsum_r:
            cr_ref = refs[pos]; pos += 1
        acc_ref = refs[pos]
        j, t = pl.program_id(0), pl.program_id(1)

        @pl.when(t == 0)
        def _():
            acc_ref[...] = jnp.zeros_like(acc_ref)

        lv = l_ref[...]
        rv = r_ref[...]
        acc_ref[...] += lax.dot_general(lv, rv.astype(BF16), TN_DIMS, preferred_element_type=F32)
        if colsum_l:
            @pl.when(t == 0)
            def _():
                cl_ref[...] = jnp.zeros_like(cl_ref)

            cl_ref[...] += jnp.sum(lv.astype(F32), axis=0, keepdims=True)
        if colsum_r:
            @pl.when((t == 0) & (j == 0))
            def _():
                cr_ref[...] = jnp.zeros_like(cr_ref)

            @pl.when(j == 0)
            def _():
                cr_ref[...] += jnp.sum(rv.astype(F32), axis=0, keepdims=True)

        @pl.when(t == n_t - 1)
        def _():
            o_ref[...] = acc_ref[...].astype(BF16)

    out_shape = [jax.ShapeDtypeStruct((nl, nr), BF16)]
    out_specs = [pl.BlockSpec((tn, nr), lambda j, t: (j, 0))]
    if colsum_l:
        out_shape.append(jax.ShapeDtypeStruct((1, nl), F32))
        out_specs.append(pl.BlockSpec((1, tn), lambda j, t: (0, j)))
    if colsum_r:
        out_shape.append(jax.ShapeDtypeStruct((1, nr), F32))
        out_specs.append(pl.BlockSpec((1, nr), lambda j, t: (0, 0)))
    outs = pl.pallas_call(
        body, grid=(nl // tn, n_t),
        in_specs=[pl.BlockSpec((tt, tn), lambda j, t: (t, j)), pl.BlockSpec((tt, nr), lambda j, t: (t, 0))],
        out_specs=out_specs, out_shape=out_shape, scratch_shapes=[pltpu.VMEM((tn, nr), F32)],
        compiler_params=_params("arbitrary", "arbitrary"), name=name)(l, r)
    return outs if (colsum_l or colsum_r) else outs[0]


def _rms_bwd(dh, xf, g):
    r = lax.rsqrt(jnp.mean(xf * xf, axis=-1, keepdims=True) + RMS_EPS)
    gd = dh * g
    dx = r * gd - xf * ((r * r * r) * jnp.mean(xf * gd, axis=-1, keepdims=True))
    return dx, dh * (xf * r)


def _dgrad_rms(name, dy, w, x, g, dres, *, tm, tk):
    m, k = dy.shape
    tm = min(tm, m)
    n_k = k // tk

    def body(dy_ref, w_ref, x_ref, g_ref, dres_ref, o_ref, dg_ref, acc_ref):
        i, kk = pl.program_id(0), pl.program_id(1)

        @pl.when(kk == 0)
        def _():
            acc_ref[...] = jnp.zeros_like(acc_ref)

        acc_ref[...] += lax.dot_general(dy_ref[...], w_ref[...], NN_DIMS, preferred_element_type=F32)

        @pl.when((i == 0) & (kk == n_k - 1))
        def _():
            dg_ref[...] = jnp.zeros_like(dg_ref)

        @pl.when(kk == n_k - 1)
        def _():
            dx, dg_rows = _rms_bwd(acc_ref[...], x_ref[...], g_ref[...])
            o_ref[...] = dres_ref[...] + dx
            dg_ref[...] += jnp.sum(dg_rows, axis=0, keepdims=True)

    return pl.pallas_call(
        body, grid=(m // tm, n_k),
        in_specs=[pl.BlockSpec((tm, tk), lambda i, kk: (i, kk)), pl.BlockSpec((tk, D), lambda i, kk: (kk, 0)),
                  pl.BlockSpec((tm, D), lambda i, kk: (i, 0)), pl.BlockSpec((1, D), lambda i, kk: (0, 0)),
                  pl.BlockSpec((tm, D), lambda i, kk: (i, 0))],
        out_specs=[pl.BlockSpec((tm, D), lambda i, kk: (i, 0)), pl.BlockSpec((1, D), lambda i, kk: (0, 0))],
        out_shape=[jax.ShapeDtypeStruct((m, D), F32), jax.ShapeDtypeStruct((1, D), F32)],
        scratch_shapes=[pltpu.VMEM((tm, D), F32)],
        compiler_params=_params("arbitrary", "arbitrary"), name=name)(dy, w, x, g, dres)


def _loss_head(x4, g, tgt, *, tm):
    m = x4.shape[0]
    tm = min(tm, m)

    def body(x_ref, g_ref, t_ref, dx_ref, loss_ref, dg_ref):
        @pl.when(pl.program_id(0) == 0)
        def _():
            loss_ref[...] = jnp.zeros_like(loss_ref)
            dg_ref[...] = jnp.zeros_like(dg_ref)

        xf = x_ref[...]
        gv = g_ref[...]
        r = lax.rsqrt(jnp.mean(xf * xf, axis=-1, keepdims=True) + RMS_EPS)
        diff = (xf * r) * gv - t_ref[...]
        loss_ref[...] += 0.5 * jnp.sum(jnp.mean(diff * diff, axis=-1, keepdims=True))
        dx, dg_rows = _rms_bwd(diff * (1.0 / D), xf, gv)
        dx_ref[...] = dx
        dg_ref[...] += jnp.sum(dg_rows, axis=0, keepdims=True)

    return pl.pallas_call(
        body, grid=(m // tm,),
        in_specs=[pl.BlockSpec((tm, D), lambda i: (i, 0)), pl.BlockSpec((1, D), lambda i: (0, 0)),
                  pl.BlockSpec((tm, D), lambda i: (i, 0))],
        out_specs=[pl.BlockSpec((tm, D), lambda i: (i, 0)), pl.BlockSpec((1, 128), lambda i: (0, 0)),
                   pl.BlockSpec((1, D), lambda i: (0, 0))],
        out_shape=[jax.ShapeDtypeStruct((m, D), F32), jax.ShapeDtypeStruct((1, 128), F32),
                   jax.ShapeDtypeStruct((1, D), F32)],
        compiler_params=_params("arbitrary"), name="loss_head")(x4, g, tgt)


def _band(kvp, kvc):
    kk = jnp.concatenate([kvp[:, :128], kvc[:, :128]], axis=0)
    vv = jnp.concatenate([kvp[:, 128:], kvc[:, 128:]], axis=0)
    return kk, vv


def _blockdiag(t, h):
    lane = lax.broadcasted_iota(jnp.int32, t.shape, 1)
    z = jnp.zeros_like(t)
    tr = pltpu.roll(t, 64, 1)
    if h == 0:
        top, bot = jnp.where(lane < 64, t, z), jnp.where(lane >= 64, tr, z)
    else:
        top, bot = jnp.where(lane < 64, tr, z), jnp.where(lane >= 64, t, z)
    return jnp.concatenate([top, bot], axis=0)


def _from_blockdiag(t, h):
    lane = lax.broadcasted_iota(jnp.int32, (256, 128), 1)
    top, bot = t[:256], t[256:]
    if h == 0:
        return jnp.where(lane < 64, top + pltpu.roll(bot, 64, 1), 0.0)
    return jnp.where(lane >= 64, pltpu.roll(top, 64, 1) + bot, 0.0)


def _stack_heads(ref, h):
    return jnp.concatenate([ref[:, h * 512 + p * 128: h * 512 + (p + 1) * 128] for p in range(4)], axis=0)


def _valid_mask(n):
    row = lax.broadcasted_iota(jnp.int32, (512, 256), 0) & 127
    col = lax.broadcasted_iota(jnp.int32, (512, 256), 1)
    return (col > row) & (col <= row + BLOCK) & ((col >= BLOCK) | (n > 0))


def _softmax_half(s_half, valid, sink):
    s_half = jnp.where(valid, s_half, NEG)
    mx = jnp.maximum(jnp.max(s_half, axis=-1, keepdims=True), sink)
    p = jnp.exp(s_half - mx)
    e_sink = jnp.exp(sink - mx)
    inv = 1.0 / (jnp.sum(p, axis=-1, keepdims=True) + e_sink)
    return p * inv, e_sink * inv


def _attn_fwd(qkv, sink_cols):
    s = qkv.shape[0]
    nb = s // BLOCK
    scale = 1.0 / math.sqrt(HEAD_DIM)

    def body(q_ref, kvc_ref, kvp_ref, sk_ref, o_ref):
        n = pl.program_id(0)
        kk, vv = _band(kvp_ref[...], kvc_ref[...])
        valid = _valid_mask(n)
        for h in range(N_KV):
            kbd, vbd = _blockdiag(kk, h), _blockdiag(vv, h)
            qp = _stack_heads(q_ref, h)
            sc = lax.dot_general(qp, kbd, NT_DIMS, preferred_element_type=F32) * scale
            probs = []
            for half in range(2):
                sink = sk_ref[h * 2 + half][:, :1]
                pr, _ = _softmax_half(sc[:, half * 256:(half + 1) * 256], valid, sink)
                probs.append(pr.astype(BF16))
            o = lax.dot_general(jnp.concatenate(probs, axis=1), vbd, NN_DIMS, preferred_element_type=F32)
            for p in range(4):
                o_ref[:, h * 512 + p * 128: h * 512 + (p + 1) * 128] = o[p * 128:(p + 1) * 128].astype(BF16)

    return pl.pallas_call(
        body, grid=(nb,),
        in_specs=[pl.BlockSpec((BLOCK, NQ), lambda n: (n, 0)),
                  pl.BlockSpec((BLOCK, KVW), lambda n: (n, NQ // KVW)),
                  pl.BlockSpec((BLOCK, KVW), lambda n: (jnp.maximum(n - 1, 0), NQ // KVW)),
                  pl.BlockSpec((4, 512, 128), lambda n: (0, 0, 0))],
        out_specs=pl.BlockSpec((BLOCK, NQ), lambda n: (n, 0)),
        out_shape=jax.ShapeDtypeStruct((s, NQ), BF16),
        compiler_params=_params("parallel"), name="attn_fwd")(qkv, qkv, qkv, sink_cols)


def _attn_bwd(qkv, do, sink_cols):
    s = qkv.shape[0]
    nb = s // BLOCK
    scale = 1.0 / math.sqrt(HEAD_DIM)

    def body(q_ref, kvc_ref, kvp_ref, do_ref, sk_ref, dq_ref, dkv_ref, dsk_ref, ck_ref, cv_ref):
        n = pl.program_id(0)

        @pl.when(n == 0)
        def _():
            dsk_ref[...] = jnp.zeros_like(dsk_ref)
            ck_ref[...] = jnp.zeros_like(ck_ref)
            cv_ref[...] = jnp.zeros_like(cv_ref)

        @pl.when(n < nb)
        def _():
            kk, vv = _band(kvp_ref[...], kvc_ref[...])
            valid = _valid_mask(n)
            dkk = jnp.zeros((256, 128), F32)
            dvv = jnp.zeros((256, 128), F32)
            for h in range(N_KV):
                kbd, vbd = _blockdiag(kk, h), _blockdiag(vv, h)
                qp = _stack_heads(q_ref, h)
                dop = _stack_heads(do_ref, h)
                sc = lax.dot_general(qp, kbd, NT_DIMS, preferred_element_type=F32) * scale
                dp = lax.dot_general(dop, vbd, NT_DIMS, preferred_element_type=F32)
                probs, dss = [], []
                for half in range(2):
                    sink = sk_ref[h * 2 + half][:, :1]
                    pr, p_sink = _softmax_half(sc[:, half * 256:(half + 1) * 256], valid, sink)
                    dph = dp[:, half * 256:(half + 1) * 256]
                    delta = jnp.sum(pr * dph, axis=-1, keepdims=True)
                    dss.append((pr * (dph - delta) * scale).astype(BF16))
                    probs.append(pr.astype(BF16))
                    dsk_ref[h * 2 + half] += -(p_sink * delta)
                ds = jnp.concatenate(dss, axis=1)
                pb = jnp.concatenate(probs, axis=1)
                dqp = lax.dot_general(ds, kbd, NN_DIMS, preferred_element_type=F32)
                for p in range(4):
                    dq_ref[:, h * 512 + p * 128: h * 512 + (p + 1) * 128] = dqp[p * 128:(p + 1) * 128].astype(BF16)
                dkk = dkk + _from_blockdiag(lax.dot_general(ds, qp, TN_DIMS, preferred_element_type=F32), h)
                dvv = dvv + _from_blockdiag(lax.dot_general(pb, dop, TN_DIMS, preferred_element_type=F32), h)

            @pl.when(n >= 1)
            def _():
                dkv_ref[:, :128] = (ck_ref[...] + dkk[:128]).astype(BF16)
                dkv_ref[:, 128:] = (cv_ref[...] + dvv[:128]).astype(BF16)

            ck_ref[...] = dkk[128:]
            cv_ref[...] = dvv[128:]

        @pl.when(n == nb)
        def _():
            dkv_ref[:, :128] = ck_ref[...].astype(BF16)
            dkv_ref[:, 128:] = cv_ref[...].astype(BF16)

    last = nb - 1
    return pl.pallas_call(
        body, grid=(nb + 1,),
        in_specs=[pl.BlockSpec((BLOCK, NQ), lambda n: (jnp.minimum(n, last), 0)),
                  pl.BlockSpec((BLOCK, KVW), lambda n: (jnp.minimum(n, last), NQ // KVW)),
                  pl.BlockSpec((BLOCK, KVW), lambda n: (jnp.clip(n - 1, 0, last), NQ // KVW)),
                  pl.BlockSpec((BLOCK, NQ), lambda n: (jnp.minimum(n, last), 0)),
                  pl.BlockSpec((4, 512, 128), lambda n: (0, 0, 0))],
        out_specs=[pl.BlockSpec((BLOCK, NQ), lambda n: (jnp.minimum(n, last), 0)),
                   pl.BlockSpec((BLOCK, KVW), lambda n: (jnp.maximum(n - 1, 0), 0)),
                   pl.BlockSpec((4, 512, 1), lambda n: (0, 0, 0))],
        out_shape=[jax.ShapeDtypeStruct((s, NQ), BF16), jax.ShapeDtypeStruct((s, KVW), BF16),
                   jax.ShapeDtypeStruct((4, 512, 1), F32)],
        scratch_shapes=[pltpu.VMEM((BLOCK, 128), F32), pltpu.VMEM((BLOCK, 128), F32)],
        compiler_params=_params("arbitrary"), name="attn_bwd")(qkv, qkv, qkv, do, sink_cols)


LANE = 128
FFN_HALF = F // 2
FFN_HALO = 8


def _resident(shape):
    return pl.BlockSpec(shape, lambda i: (0,) * len(shape), pipeline_mode=pl.Buffered(1))


def _ffn_fwd(tag, x_in, g, w_up_t, w_dw, b_dw, w_down, *, tm):
    s = x_in.shape[0]
    tm = min(tm, s)

    def body(x_ref, g_ref, wup_ref, wdw_ref, bdw_ref, wd_ref, o_ref, h_ref, up_ref, act_ref, carry_ref, ext_ref):
        @pl.when(pl.program_id(0) == 0)
        def _():
            carry_ref[...] = jnp.zeros_like(carry_ref)

        xf = x_ref[...]
        r = lax.rsqrt(jnp.mean(xf * xf, axis=-1, keepdims=True) + RMS_EPS)
        h = ((xf * r) * g_ref[...]).astype(BF16)
        h_ref[...] = h
        acc = xf
        for half in range(2):
            for off in (half * FFN_HALF, F + half * FFN_HALF):
                rs = slice(off, off + FFN_HALF)
                up_ref[:, rs] = lax.dot_general(h, wup_ref[rs, :], NT_DIMS, preferred_element_type=F32).astype(BF16)
            for c in range(half * (FFN_HALF // LANE), (half + 1) * (FFN_HALF // LANE)):
                cs = slice(c * LANE, (c + 1) * LANE)
                vs = slice(F + c * LANE, F + (c + 1) * LANE)
                gcol = up_ref[:, cs].astype(F32)
                ext_ref[pl.ds(0, FFN_HALO), :] = carry_ref[:, cs]
                ext_ref[pl.ds(FFN_HALO, tm), :] = gcol
                gate = (wdw_ref[2:3, cs] * gcol + wdw_ref[1:2, cs] * ext_ref[pl.ds(FFN_HALO - 1, tm), :]
                        + wdw_ref[0:1, cs] * ext_ref[pl.ds(FFN_HALO - 2, tm), :] + bdw_ref[:, cs])
                act_ref[:, cs] = (gate * _sigmoid(gate) * up_ref[:, vs].astype(F32)).astype(BF16)
                carry_ref[:, cs] = gcol[tm - FFN_HALO:]
            hs = slice(half * FFN_HALF, (half + 1) * FFN_HALF)
            acc = acc + lax.dot_general(act_ref[:, hs], wd_ref[hs, :], NN_DIMS, preferred_element_type=F32)
        o_ref[...] = acc

    row = lambda i: (i, 0)
    return pl.pallas_call(
        body, grid=(s // tm,),
        in_specs=[pl.BlockSpec((tm, D), row), _resident((1, D)), _resident((2 * F, D)), _resident((3, F)),
                  _resident((1, F)), _resident((F, D))],
        out_specs=[pl.BlockSpec((tm, D), row), pl.BlockSpec((tm, D), row), pl.BlockSpec((tm, 2 * F), row),
                   pl.BlockSpec((tm, F), row)],
        out_shape=[jax.ShapeDtypeStruct((s, D), F32), jax.ShapeDtypeStruct((s, D), BF16),
                   jax.ShapeDtypeStruct((s, 2 * F), BF16), jax.ShapeDtypeStruct((s, F), BF16)],
        scratch_shapes=[pltpu.VMEM((FFN_HALO, F), F32), pltpu.VMEM((tm + FFN_HALO, LANE), F32)],
        compiler_params=_params("arbitrary"), name=f"ffn{tag}_fwd")(x_in, g, w_up_t, w_dw, b_dw, w_down)


def _ffn_bwd(tag, dx_out, x_in, g, w_up_t, w_dw, b_dw, w_down, up, *, tm):
    s = x_in.shape[0]
    tm = min(tm, s)
    n_i = s // tm
    hb = tm // FFN_HALO

    def body(dx_ref, x_ref, g_ref, up_ref, prev_ref, wup_ref, wdw_ref, bdw_ref, wd_ref,
             dxin_ref, dup_ref, dg_ref, dwb_ref, carry_ref, da_ref, extg_ref, extd_ref):
        i = pl.program_id(0)

        @pl.when(i == 0)
        def _():
            carry_ref[...] = jnp.zeros_like(carry_ref)
            dg_ref[...] = jnp.zeros_like(dg_ref)
            dwb_ref[...] = jnp.zeros_like(dwb_ref)

        first_tile = i == n_i - 1
        dxf = dx_ref[...]
        dxb = dxf.astype(BF16)
        acc = None
        for half in range(2):
            hs = slice(half * FFN_HALF, (half + 1) * FFN_HALF)
            vhs = slice(F + half * FFN_HALF, F + (half + 1) * FFN_HALF)
            da_ref[...] = lax.dot_general(dxb, wd_ref[hs, :], NT_DIMS, preferred_element_type=F32)
            for c in range(FFN_HALF // LANE):
                cc = half * (FFN_HALF // LANE) + c
                cs = slice(cc * LANE, (cc + 1) * LANE)
                vs = slice(F + cc * LANE, F + (cc + 1) * LANE)
                w0, w1, w2 = wdw_ref[0:1, cs], wdw_ref[1:2, cs], wdw_ref[2:3, cs]
                gcol = up_ref[:, cs].astype(F32)
                extg_ref[pl.ds(0, FFN_HALO), :] = jnp.where(first_tile, 0.0, prev_ref[:, cs].astype(F32))
                extg_ref[pl.ds(FFN_HALO, tm), :] = gcol
                gm1 = extg_ref[pl.ds(FFN_HALO - 1, tm), :]
                gm2 = extg_ref[pl.ds(FFN_HALO - 2, tm), :]
                gate = w2 * gcol + w1 * gm1 + w0 * gm2 + bdw_ref[:, cs]
                sg = _sigmoid(gate)
                dac = da_ref[:, c * LANE:(c + 1) * LANE]
                dup_ref[:, vs] = (dac * (gate * sg)).astype(BF16)
                dgate = dac * up_ref[:, vs].astype(F32) * (sg * (1.0 + gate * (1.0 - sg)))
                extd_ref[pl.ds(0, tm), :] = dgate
                extd_ref[pl.ds(tm, FFN_HALO), :] = carry_ref[:, cs]
                dup_ref[:, cs] = (w2 * dgate + w1 * extd_ref[pl.ds(1, tm), :] + w0 * extd_ref[pl.ds(2, tm), :]).astype(BF16)
                carry_ref[:, cs] = dgate[:FFN_HALO]
                dwb_ref[2:3, cs] += jnp.sum(dgate * gcol, axis=0, keepdims=True)
                dwb_ref[1:2, cs] += jnp.sum(dgate * gm1, axis=0, keepdims=True)
                dwb_ref[0:1, cs] += jnp.sum(dgate * gm2, axis=0, keepdims=True)
                dwb_ref[3:4, cs] += jnp.sum(dgate, axis=0, keepdims=True)
            part = (lax.dot_general(dup_ref[:, hs], wup_ref[hs, :], NN_DIMS, preferred_element_type=F32)
                    + lax.dot_general(dup_ref[:, vhs], wup_ref[vhs, :], NN_DIMS, preferred_element_type=F32))
            acc = part if acc is None else acc + part
        dx, dg_rows = _rms_bwd(acc, x_ref[...], g_ref[...])
        dxin_ref[...] = dxf + dx
        dg_ref[...] += jnp.sum(dg_rows, axis=0, keepdims=True)

    rev = lambda i: (n_i - 1 - i, 0)
    return pl.pallas_call(
        body, grid=(n_i,),
        in_specs=[pl.BlockSpec((tm, D), rev), pl.BlockSpec((tm, D), rev), _resident((1, D)),
                  pl.BlockSpec((tm, 2 * F), rev),
                  pl.BlockSpec((FFN_HALO, 2 * F), lambda i: (jnp.maximum((n_i - 1 - i) * hb - 1, 0), 0)),
                  _resident((2 * F, D)), _resident((3, F)), _resident((1, F)), _resident((F, D))],
        out_specs=[pl.BlockSpec((tm, D), rev), pl.BlockSpec((tm, 2 * F), rev),
                   pl.BlockSpec((1, D), lambda i: (0, 0)), pl.BlockSpec((8, F), lambda i: (0, 0))],
        out_shape=[jax.ShapeDtypeStruct((s, D), F32), jax.ShapeDtypeStruct((s, 2 * F), BF16),
                   jax.ShapeDtypeStruct((1, D), F32), jax.ShapeDtypeStruct((8, F), F32)],
        scratch_shapes=[pltpu.VMEM((FFN_HALO, F), F32), pltpu.VMEM((tm, FFN_HALF), F32),
                        pltpu.VMEM((tm + FFN_HALO, LANE), F32), pltpu.VMEM((tm + FFN_HALO, LANE), F32)],
        compiler_params=_params("arbitrary"), name=f"ffn{tag}_bwd")(dx_out, x_in, g, up, up, w_up_t, w_dw, b_dw, w_down)


CONV_HALO = 32
CONV_CHUNK = 256


def _ln(u2, g, b):
    mu = jnp.mean(u2, axis=-1, keepdims=True)
    xc = u2 - mu
    rstd = lax.rsqrt(jnp.mean(xc * xc, axis=-1, keepdims=True) + LN_EPS)
    xhat = xc * rstd
    return xhat, rstd, xhat * g + b


def _phase_taps(ph):
    first = CONV_HALO - (CONV_K - 1)
    return [(k, first + k - ph) for k in range(CONV_K) if (first + k) % 8 == ph]


def _store_phases(src_ref, sh_ref, tt, cols=slice(None)):
    sh_ref[0] = src_ref[:, cols]
    for ph in range(1, 8):
        sh_ref[ph, pl.ds(0, tt + CONV_HALO - 8), :] = src_ref[pl.ds(ph, tt + CONV_HALO - 8), cols]


def _conv_mid_fwd(a, wdw, ln_g, ln_b, *, tt):
    s = a.shape[0]
    tt = min(tt, s)
    hb = tt // CONV_HALO

    def body(a_ref, prev_ref, w_ref, g_ref, b_ref, s_ref, u2_ref, ext_ref, sh_ref):
        i = pl.program_id(0)
        for c in range(D // CONV_CHUNK):
            cs = slice(c * CONV_CHUNK, (c + 1) * CONV_CHUNK)
            gs = slice(D + c * CONV_CHUNK, D + (c + 1) * CONV_CHUNK)
            uh = prev_ref[:, cs].astype(F32) * _sigmoid(prev_ref[:, gs].astype(F32))
            ext_ref[pl.ds(0, CONV_HALO), :] = jnp.where(i > 0, uh, 0.0)
            ext_ref[pl.ds(CONV_HALO, tt), :] = a_ref[:, cs].astype(F32) * _sigmoid(a_ref[:, gs].astype(F32))
            acc = jnp.zeros((tt, CONV_CHUNK), F32) + w_ref[CONV_K:CONV_K + 1, cs]
            _store_phases(ext_ref, sh_ref, tt)
            for ph in range(8):
                for k, off in _phase_taps(ph):
                    acc = acc + w_ref[k:k + 1, cs] * sh_ref[ph, pl.ds(off, tt), :]
            u2_ref[:, cs] = acc.astype(BF16)
        _, _, ln = _ln(u2_ref[...].astype(F32), g_ref[...], b_ref[...])
        s_ref[...] = (ln * _sigmoid(ln)).astype(BF16)

    return pl.pallas_call(
        body, grid=(s // tt,),
        in_specs=[pl.BlockSpec((tt, 2 * D), lambda i: (i, 0)),
                  pl.BlockSpec((CONV_HALO, 2 * D), lambda i: (jnp.maximum(i * hb - 1, 0), 0)),
                  pl.BlockSpec((32, D), lambda i: (0, 0)), pl.BlockSpec((1, D), lambda i: (0, 0)),
                  pl.BlockSpec((1, D), lambda i: (0, 0))],
        out_specs=[pl.BlockSpec((tt, D), lambda i: (i, 0)), pl.BlockSpec((tt, D), lambda i: (i, 0))],
        out_shape=[jax.ShapeDtypeStruct((s, D), BF16), jax.ShapeDtypeStruct((s, D), BF16)],
        scratch_shapes=[pltpu.VMEM((tt + CONV_HALO, CONV_CHUNK), F32), pltpu.VMEM((8, tt + CONV_HALO, CONV_CHUNK), F32)],
        compiler_params=_params("parallel"), name="conv_mid_fwd")(a, a, wdw, ln_g, ln_b)


def _conv_mid_bwd(ds, u2, a, wdw, ln_g, ln_b, *, tt):
    s = a.shape[0]
    tt = min(tt, s)
    hb = tt // CONV_HALO
    n_i = s // tt
    te = tt + CONV_HALO

    def body(ds_ref, dsn_ref, u2_ref, u2n_ref, a_ref, prev_ref, w_ref, g_ref, b_ref, da_ref, acc_ref, e2_ref, ext_ref,
             ush_ref, dsh_ref):
        i = pl.program_id(0)

        @pl.when(i == 0)
        def _():
            acc_ref[...] = jnp.zeros_like(acc_ref)

        u2e = jnp.concatenate([u2_ref[...], u2n_ref[...]], axis=0).astype(F32)
        dse = jnp.concatenate([ds_ref[...], dsn_ref[...]], axis=0).astype(F32)
        gv = g_ref[...]
        xhat, rstd, ln = _ln(u2e, gv, b_ref[...])
        sg = _sigmoid(ln)
        dln = dse * (sg * (1.0 + ln * (1.0 - sg)))
        acc_ref[32:33, :] += jnp.sum(dln[:tt] * xhat[:tt], axis=0, keepdims=True)
        acc_ref[33:34, :] += jnp.sum(dln[:tt], axis=0, keepdims=True)
        dxh = dln * gv
        du2 = rstd * (dxh - jnp.mean(dxh, axis=-1, keepdims=True) - xhat * jnp.mean(dxh * xhat, axis=-1, keepdims=True))
        row = lax.broadcasted_iota(jnp.int32, (te, D), 0)
        e2_ref[...] = jnp.where((row < tt) | (i < n_i - 1), du2, 0.0)
        for c in range(D // CONV_CHUNK):
            cs = slice(c * CONV_CHUNK, (c + 1) * CONV_CHUNK)
            gs = slice(D + c * CONV_CHUNK, D + (c + 1) * CONV_CHUNK)
            uh = prev_ref[:, cs].astype(F32) * _sigmoid(prev_ref[:, gs].astype(F32))
            ext_ref[pl.ds(0, CONV_HALO), :] = jnp.where(i > 0, uh, 0.0)
            a1 = a_ref[:, cs].astype(F32)
            sg2 = _sigmoid(a_ref[:, gs].astype(F32))
            ext_ref[pl.ds(CONV_HALO, tt), :] = a1 * sg2
            d0 = e2_ref[pl.ds(0, tt), cs]
            du = jnp.zeros((tt, CONV_CHUNK), F32)
            _store_phases(ext_ref, ush_ref, tt)
            _store_phases(e2_ref, dsh_ref, tt, cs)
            for ph in range(8):
                for k, off in _phase_taps(ph):
                    acc_ref[k:k + 1, cs] += jnp.sum(d0 * ush_ref[ph, pl.ds(off, tt), :], axis=0, keepdims=True)
                for k in range(CONV_K):
                    if (CONV_K - 1 - k) % 8 == ph:
                        du = du + w_ref[k:k + 1, cs] * dsh_ref[ph, pl.ds(CONV_K - 1 - k - ph, tt), :]
            acc_ref[CONV_K:CONV_K + 1, cs] += jnp.sum(d0, axis=0, keepdims=True)
            da_ref[:, cs] = (du * sg2).astype(BF16)
            da_ref[:, gs] = (du * a1 * (sg2 * (1.0 - sg2))).astype(BF16)

    last_h = s // CONV_HALO - 1
    nxt = lambda i: (jnp.minimum((i + 1) * hb, last_h), 0)
    return pl.pallas_call(
        body, grid=(n_i,),
        in_specs=[pl.BlockSpec((tt, D), lambda i: (i, 0)), pl.BlockSpec((CONV_HALO, D), nxt),
                  pl.BlockSpec((tt, D), lambda i: (i, 0)), pl.BlockSpec((CONV_HALO, D), nxt),
                  pl.BlockSpec((tt, 2 * D), lambda i: (i, 0)),
                  pl.BlockSpec((CONV_HALO, 2 * D), lambda i: (jnp.maximum(i * hb - 1, 0), 0)),
                  pl.BlockSpec((32, D), lambda i: (0, 0)), pl.BlockSpec((1, D), lambda i: (0, 0)),
                  pl.BlockSpec((1, D), lambda i: (0, 0))],
        out_specs=[pl.BlockSpec((tt, 2 * D), lambda i: (i, 0)), pl.BlockSpec((40, D), lambda i: (0, 0))],
        out_shape=[jax.ShapeDtypeStruct((s, 2 * D), BF16), jax.ShapeDtypeStruct((40, D), F32)],
        scratch_shapes=[pltpu.VMEM((te, D), F32), pltpu.VMEM((te, CONV_CHUNK), F32),
                        pltpu.VMEM((8, te, CONV_CHUNK), F32), pltpu.VMEM((8, te, CONV_CHUNK), F32)],
        compiler_params=_params("arbitrary"), name="conv_mid_bwd")(ds, ds, u2, u2, a, a, wdw, ln_g, ln_b)


def _adamw(name, w, g, m, v):
    rows, cols = w.shape
    tr = _row_tile(rows, 256)

    def body(w_ref, g_ref, m_ref, v_ref, d_ref, nm_ref, nv_ref):
        gv = g_ref[...]
        m2 = ADAM_B1 * m_ref[...] + (1.0 - ADAM_B1) * gv
        v2 = ADAM_B2 * v_ref[...] + (1.0 - ADAM_B2) * (gv * gv)
        m_hat = m2 / (1.0 - ADAM_B1 ** ADAM_STEP)
        v_hat = v2 / (1.0 - ADAM_B2 ** ADAM_STEP)
        d_ref[...] = -ADAM_LR * (m_hat / (jnp.sqrt(v_hat) + ADAM_EPS) + ADAM_WD * w_ref[...])
        nm_ref[...] = m2
        nv_ref[...] = v2

    spec = pl.BlockSpec((tr, cols), lambda i: (i, 0))
    return pl.pallas_call(
        body, grid=(rows // tr,), in_specs=[spec] * 4, out_specs=[spec] * 3,
        out_shape=[jax.ShapeDtypeStruct((rows, cols), F32)] * 3,
        compiler_params=_params("parallel"), name=name)(w, g, m, v)


def _sum_slabs(buf):
    rows = buf.shape[1]
    tr = _row_tile(rows, 512, 16)

    def body(b_ref, o_ref):
        acc = b_ref[0].astype(F32)
        for k in range(1, N_DEV):
            acc = acc + b_ref[k].astype(F32)
        o_ref[...] = acc

    return pl.pallas_call(
        body, grid=(rows // tr,), in_specs=[pl.BlockSpec((N_DEV, tr, D), lambda i: (0, i, 0))],
        out_specs=pl.BlockSpec((tr, D), lambda i: (i, 0)), out_shape=jax.ShapeDtypeStruct((rows, D), F32),
        compiler_params=_params("parallel"), name="sum_slabs")(buf)


def _me():
    x, y, c = lax.axis_index("x"), lax.axis_index("y"), lax.axis_index("c")
    return x, y, c, 4 * x + 2 * y + c


def _peer(x, y, c, k):
    return (x ^ (k >> 2), y ^ ((k >> 1) & 1), c ^ (k & 1))


def _all_gather_weights(slab, small):
    any_spec = pl.BlockSpec(memory_space=pl.ANY)
    vmem_spec = pl.BlockSpec(memory_space=pltpu.VMEM)

    def body(slab_ref, small_ref, *rest):
        outs = rest[:N_MAT]
        sm_out, send_sems, recv_sems, sm_send, sm_recv, local_sem = rest[N_MAT:]
        x, y, c, me = _me()
        sibling = (x, y, 1 - c)
        chips = [(1 - x, y), (x, 1 - y), (1 - x, 1 - y)]

        def rows_of(i, dev):
            return outs[i].at[pl.ds(dev * MAT_ROWS[i], MAT_ROWS[i])]

        def copies(k, dev, to, from_slab):
            return [pltpu.make_async_remote_copy(
                src_ref=slab_ref.at[pl.ds(MAT_OFF[i], MAT_ROWS[i])] if from_slab else rows_of(i, dev),
                dst_ref=rows_of(i, dev), send_sem=send_sems.at[k], recv_sem=recv_sems.at[k],
                device_id=to, device_id_type=MESH) for i in range(N_MAT)]

        whole = outs[4].at[pl.ds(0, SLAB_ROWS)]

        def waiter(k):
            return pltpu.make_async_remote_copy(src_ref=whole, dst_ref=whole, send_sem=send_sems.at[k],
                                                recv_sem=recv_sems.at[k], device_id=sibling, device_id_type=MESH)

        mine = [pltpu.make_async_copy(slab_ref.at[pl.ds(MAT_OFF[i], MAT_ROWS[i])], rows_of(i, me), local_sem)
                for i in range(N_MAT)]
        for cp in mine:
            cp.start()
        for cp in copies(0, me, sibling, True):
            cp.start()
        for j, chip in enumerate(chips):
            for cp in copies(1 + j, me, (*chip, c), True):
                cp.start()
        sm_out[me] = small_ref[...]
        small_copies = [pltpu.make_async_remote_copy(
            src_ref=small_ref, dst_ref=sm_out.at[me], send_sem=sm_send.at[k - 1], recv_sem=sm_recv.at[k - 1],
            device_id=_peer(x, y, c, k), device_id_type=MESH) for k in range(1, N_DEV)]
        for cp in small_copies:
            cp.start()
        for j, chip in enumerate(chips):
            waiter(1 + j).wait_recv()
            dev = 4 * chip[0] + 2 * chip[1] + c
            for cp in copies(4 + j, dev, sibling, False):
                cp.start()
        waiter(0).wait_recv()
        for j in range(3):
            waiter(4 + j).wait_recv()
        for k in range(7):
            waiter(k).wait_send()
        for cp in small_copies:
            cp.wait()
        pltpu.make_async_copy(whole, whole, local_sem).wait()

    out_shape = [jax.ShapeDtypeStruct((N_DEV * r, D), BF16) for r in MAT_ROWS]
    out_shape.append(jax.ShapeDtypeStruct((N_DEV,) + small.shape, F32))
    return pl.pallas_call(
        body, in_specs=[any_spec, vmem_spec], out_specs=[any_spec] * N_MAT + [vmem_spec], out_shape=out_shape,
        scratch_shapes=[pltpu.SemaphoreType.DMA((7,)), pltpu.SemaphoreType.DMA((7,)),
                        pltpu.SemaphoreType.DMA((7,)), pltpu.SemaphoreType.DMA((7,)), pltpu.SemaphoreType.DMA],
        compiler_params=pltpu.CompilerParams(vmem_limit_bytes=VMEM_LIMIT_BYTES), name="all_gather_weights")(slab, small)


def _reduce_scatter_grads(dws, small):
    any_spec = pl.BlockSpec(memory_space=pl.ANY)
    vmem_spec = pl.BlockSpec(memory_space=pltpu.VMEM)
    n_small = small.shape[0]

    def body(*refs):
        dw_refs = refs[:N_MAT]
        small_ref, buf_ref, ssum_ref, sbuf_ref, send_sems, recv_sems, sm_send, sm_recv, local_sem = refs[N_MAT:]
        x, y, c, me = _me()

        def piece(i, dev):
            return dw_refs[i].at[pl.ds(dev * MAT_ROWS[i], MAT_ROWS[i])]

        def slot(k, i):
            return buf_ref.at[k, pl.ds(MAT_OFF[i], MAT_ROWS[i])]

        mine = [pltpu.make_async_copy(piece(i, me), slot(0, i), local_sem) for i in range(N_MAT)]
        for cp in mine:
            cp.start()
        small_copies = []
        for k in range(1, N_DEV):
            peer = _peer(x, y, c, k)
            for i in range(N_MAT):
                pltpu.make_async_remote_copy(src_ref=piece(i, me ^ k), dst_ref=slot(k, i), send_sem=send_sems.at[k - 1],
                                             recv_sem=recv_sems.at[k - 1], device_id=peer, device_id_type=MESH).start()
            small_copies.append(pltpu.make_async_remote_copy(
                src_ref=small_ref, dst_ref=sbuf_ref.at[me], send_sem=sm_send.at[k - 1], recv_sem=sm_recv.at[k - 1],
                device_id=peer, device_id_type=MESH))
            small_copies[-1].start()
        sbuf_ref[me] = small_ref[...]
        for cp in small_copies:
            cp.wait()
        acc = sbuf_ref[0]
        for d in range(1, N_DEV):
            acc = acc + sbuf_ref[d]
        ssum_ref[...] = acc
        whole = buf_ref.at[0]
        for k in range(1, N_DEV):
            pltpu.make_async_remote_copy(src_ref=whole, dst_ref=whole, send_sem=send_sems.at[k - 1],
                                         recv_sem=recv_sems.at[k - 1], device_id=_peer(x, y, c, k),
                                         device_id_type=MESH).wait()
        pltpu.make_async_copy(whole, whole, local_sem).wait()

    return pl.pallas_call(
        body, in_specs=[any_spec] * N_MAT + [vmem_spec], out_specs=[any_spec, vmem_spec],
        out_shape=[jax.ShapeDtypeStruct((N_DEV, SLAB_ROWS, D), BF16), jax.ShapeDtypeStruct((n_small, D), F32)],
        scratch_shapes=[pltpu.VMEM((N_DEV, n_small, D), F32),
                        pltpu.SemaphoreType.DMA((7,)), pltpu.SemaphoreType.DMA((7,)),
                        pltpu.SemaphoreType.DMA((7,)), pltpu.SemaphoreType.DMA((7,)), pltpu.SemaphoreType.DMA],
        compiler_params=pltpu.CompilerParams(vmem_limit_bytes=VMEM_LIMIT_BYTES), name="reduce_scatter_grads")(*dws, small)


def _pack_rows(arrs):
    rows = []
    for a in arrs:
        flat = a.reshape(-1).astype(F32)
        pad = (-flat.shape[0]) % D
        rows.append(jnp.pad(flat, (0, pad)).reshape(-1, D))
    slab = jnp.concatenate(rows, axis=0)
    return jnp.pad(slab, ((0, (-slab.shape[0]) % 8), (0, 0)))


def _unpack_rows(slab, shapes):
    out, r = [], 0
    for shp in shapes:
        size = math.prod(shp)
        nrows = -(-size // D)
        out.append(slab[r:r + nrows].reshape(-1)[:size].reshape(shp))
        r += nrows
    return out


def kernel(x, norm_mix, attn_w_qkv, attn_b_qkv, attn_sinks, attn_w_o, attn_b_o, conv_w_pw1, conv_b_pw1, conv_w_dw, conv_b_dw, conv_ln_g, conv_ln_b, conv_w_pw2, conv_b_pw2, norm_ffn, ffn_w_up, ffn_w_dw, ffn_b_dw, ffn_w_down, final_norm, loss_target, m_norm_mix, m_attn_w_qkv, m_attn_b_qkv, m_attn_sinks, m_attn_w_o, m_attn_b_o, m_conv_w_pw1, m_conv_b_pw1, m_conv_w_dw, m_conv_b_dw, m_conv_ln_g, m_conv_ln_b, m_conv_w_pw2, m_conv_b_pw2, m_norm_ffn, m_ffn_w_up, m_ffn_w_dw, m_ffn_b_dw, m_ffn_w_down, m_final_norm, v_norm_mix, v_attn_w_qkv, v_attn_b_qkv, v_attn_sinks, v_attn_w_o, v_attn_b_o, v_conv_w_pw1, v_conv_b_pw1, v_conv_w_dw, v_conv_b_dw, v_conv_ln_g, v_conv_ln_b, v_conv_w_pw2, v_conv_b_pw2, v_norm_ffn, v_ffn_w_up, v_ffn_w_dw, v_ffn_b_dw, v_ffn_w_down, v_final_norm):
    s = x.shape[1]
    me = 4 * lax.axis_index("x") + 2 * lax.axis_index("y") + lax.axis_index("c")
    x0 = x.reshape(s, D)
    tgt = loss_target.reshape(s, D)

    slab = jnp.concatenate([attn_w_qkv[0].T, attn_w_o[0], conv_w_pw1[0].T, conv_w_pw2[0],
                            ffn_w_up[0].T, ffn_w_down[0], ffn_w_up[1].T, ffn_w_down[1]], axis=0).astype(BF16)
    sharded_small = [conv_b_pw1, conv_w_dw, conv_b_dw, conv_ln_g, conv_ln_b, conv_b_pw2, ffn_w_dw]
    small_local = _pack_rows(sharded_small)
    *mats, small_all = _all_gather_weights(slab, small_local)
    w_qkv_t, w_o, w_pw1_t, w_pw2, w_up0_t, w_dn0, w_up1_t, w_dn1 = mats
    parts = [_unpack_rows(small_all[d], [a.shape for a in sharded_small]) for d in range(N_DEV)]
    b_pw1, w_cdw, b_cdw, ln_g, ln_b, b_pw2, w_fdw = [jnp.concatenate([parts[d][i] for d in range(N_DEV)], axis=-1)
                                                      for i in range(len(sharded_small))]
    conv_w32 = jnp.concatenate([w_cdw[0], b_cdw], axis=0)
    sr = attn_sinks.reshape(N_KV, 4, 2)
    sink_cols = jnp.broadcast_to(jnp.repeat(jnp.transpose(sr, (0, 2, 1)), BLOCK, axis=-1).reshape(4, 512, 1),
                                 (4, 512, 128))

    qkv, h0 = _mm("qkv", x0, w_qkv_t, nt=True, tm=512, tn=NQ + KVW, out_dtype=BF16, rms_g=norm_mix[0:1], bias=attn_b_qkv)
    o = _attn_fwd(qkv, sink_cols)
    x1 = _mm("attn_out", o, w_o, nt=False, tm=512, tn=D, out_dtype=F32, bias=attn_b_o, res=x0)
    x2, h1, up0, act0 = _ffn_fwd(0, x1, norm_ffn[0:1], w_up0_t, w_fdw[0], ffn_b_dw[0:1], w_dn0, tm=256)
    a, h2 = _mm("pw1", x2, w_pw1_t, nt=True, tm=512, tn=2 * D, out_dtype=BF16, rms_g=norm_mix[1:2], bias=b_pw1)
    s_act, u2 = _conv_mid_fwd(a, conv_w32, ln_g, ln_b, tt=256)
    x3 = _mm("pw2", s_act, w_pw2, nt=False, tm=512, tn=D, out_dtype=F32, bias=b_pw2, res=x2)
    x4, h3, up1, act1 = _ffn_fwd(1, x3, norm_ffn[1:2], w_up1_t, w_fdw[1], ffn_b_dw[1:2], w_dn1, tm=256)
    dx4, loss_acc, dg_final = _loss_head(x4, final_norm.reshape(1, D), tgt, tm=512)
    loss = lax.psum(loss_acc[0, 0], ("x", "y", "c"))

    dx3, d_up1, dg_f1, dwb1 = _ffn_bwd(1, dx4, x3, norm_ffn[1:2], w_up1_t, w_fdw[1], ffn_b_dw[1:2], w_dn1, up1, tm=256)
    dw_dn1 = _mm_tn("ffn1_dwdown", act1, dx4, tn=FFN_HALF, tt=512)
    dw_up1_t = _mm_tn("ffn1_dwup", d_up1, h3, tn=FFN_HALF, tt=512)
    ds_act = _mm("d_pw2", dx3, w_pw2, nt=True, tm=512, tn=D, out_dtype=BF16)
    dw_pw2, db_pw2 = _mm_tn("dw_pw2", s_act, dx3, tn=D, tt=512, colsum_r=True)
    da, conv_acc = _conv_mid_bwd(ds_act, u2, a, conv_w32, ln_g, ln_b, tt=256)
    dx2, dg_m1 = _dgrad_rms("d_pw1", da, w_pw1_t, x2, norm_mix[1:2], dx3, tm=512, tk=2 * D)
    dw_pw1_t, db_pw1 = _mm_tn("dw_pw1", da, h2, tn=D, tt=512, colsum_l=True)
    dx1, d_up0, dg_f0, dwb0 = _ffn_bwd(0, dx2, x1, norm_ffn[0:1], w_up0_t, w_fdw[0], ffn_b_dw[0:1], w_dn0, up0, tm=256)
    dw_dn0 = _mm_tn("ffn0_dwdown", act0, dx2, tn=FFN_HALF, tt=512)
    dw_up0_t = _mm_tn("ffn0_dwup", d_up0, h1, tn=FFN_HALF, tt=512)
    do = _mm("d_attn_out", dx1, w_o, nt=True, tm=512, tn=D, out_dtype=BF16)
    dw_o, db_o = _mm_tn("dw_o", o, dx1, tn=D, tt=512, colsum_r=True)
    dq, dkv, dsk = _attn_bwd(qkv, do, sink_cols)
    dqkv = jnp.concatenate([dq, dkv], axis=1)
    grad_x, dg_m0 = _dgrad_rms("d_qkv", dqkv, w_qkv_t, x0, norm_mix[0:1], dx1, tm=512, tk=NQ + KVW)
    dw_qkv_t, db_qkv = _mm_tn("dw_qkv", dqkv, h0, tn=NQ + KVW, tt=512, colsum_l=True)

    d_sinks = jnp.transpose(jnp.sum(dsk.reshape(N_KV, 2, 4, BLOCK), axis=-1), (0, 2, 1)).reshape(1, 16)
    small_grads = [jnp.concatenate([dg_m0, dg_m1], axis=0), db_qkv, d_sinks, db_o, jnp.concatenate([dg_f0, dg_f1], axis=0),
                   jnp.stack([dwb0[3], dwb1[3]]), dg_final, db_pw1, conv_acc[0:CONV_K], conv_acc[31:32], conv_acc[32:33],
                   conv_acc[33:34], db_pw2, jnp.stack([dwb0[0:3], dwb1[0:3]])]
    small_shapes = [(2, D), (1, NQ + KVW), (1, 16), (1, D), (2, D), (2, F), (D,), (1, 2 * D), (1, CONV_K, D), (1, D),
                    (1, D), (1, D), (1, D), (2, 3, F)]
    buf, small_sum = _reduce_scatter_grads([dw_qkv_t, dw_o, dw_pw1_t, dw_pw2, dw_up0_t, dw_dn0, dw_up1_t, dw_dn1],
                                           _pack_rows(small_grads))
    gslab = _sum_slabs(buf)
    gm = [gslab[MAT_OFF[i]:MAT_OFF[i] + MAT_ROWS[i]] for i in range(N_MAT)]
    (g_norm_mix, g_b_qkv, g_sinks, g_b_o, g_norm_ffn, g_ffn_b_dw, g_final, g_b_pw1_full, g_w_cdw_full, g_b_cdw_full,
     g_ln_g_full, g_ln_b_full, g_b_pw2_full, g_w_fdw_full) = _unpack_rows(small_sum, small_shapes)

    def shard(a, width):
        return lax.dynamic_slice_in_dim(a, me * width, width, axis=a.ndim - 1)

    grads = {
        "norm_mix": g_norm_mix, "attn_w_qkv": gm[0].T[None], "attn_b_qkv": g_b_qkv, "attn_sinks": g_sinks,
        "attn_w_o": gm[1][None], "attn_b_o": g_b_o, "conv_w_pw1": gm[2].T[None], "conv_b_pw1": shard(g_b_pw1_full, 256),
        "conv_w_dw": shard(g_w_cdw_full, 128), "conv_b_dw": shard(g_b_cdw_full, 128), "conv_ln_g": shard(g_ln_g_full, 128),
        "conv_ln_b": shard(g_ln_b_full, 128), "conv_w_pw2": gm[3][None], "conv_b_pw2": shard(g_b_pw2_full, 128),
        "norm_ffn": g_norm_ffn, "ffn_w_up": jnp.stack([gm[4].T, gm[6].T]), "ffn_w_dw": shard(g_w_fdw_full, 352),
        "ffn_b_dw": g_ffn_b_dw, "ffn_w_down": jnp.stack([gm[5], gm[7]]), "final_norm": g_final,
    }
    weights = dict(norm_mix=norm_mix, attn_w_qkv=attn_w_qkv, attn_b_qkv=attn_b_qkv, attn_sinks=attn_sinks, attn_w_o=attn_w_o, attn_b_o=attn_b_o, conv_w_pw1=conv_w_pw1, conv_b_pw1=conv_b_pw1, conv_w_dw=conv_w_dw, conv_b_dw=conv_b_dw, conv_ln_g=conv_ln_g, conv_ln_b=conv_ln_b, conv_w_pw2=conv_w_pw2, conv_b_pw2=conv_b_pw2, norm_ffn=norm_ffn, ffn_w_up=ffn_w_up, ffn_w_dw=ffn_w_dw, ffn_b_dw=ffn_b_dw, ffn_w_down=ffn_w_down, final_norm=final_norm)
    moms = dict(norm_mix=m_norm_mix, attn_w_qkv=m_attn_w_qkv, attn_b_qkv=m_attn_b_qkv, attn_sinks=m_attn_sinks, attn_w_o=m_attn_w_o, attn_b_o=m_attn_b_o, conv_w_pw1=m_conv_w_pw1, conv_b_pw1=m_conv_b_pw1, conv_w_dw=m_conv_w_dw, conv_b_dw=m_conv_b_dw, conv_ln_g=m_conv_ln_g, conv_ln_b=m_conv_ln_b, conv_w_pw2=m_conv_w_pw2, conv_b_pw2=m_conv_b_pw2, norm_ffn=m_norm_ffn, ffn_w_up=m_ffn_w_up, ffn_w_dw=m_ffn_w_dw, ffn_b_dw=m_ffn_b_dw, ffn_w_down=m_ffn_w_down, final_norm=m_final_norm)
    vars_ = dict(norm_mix=v_norm_mix, attn_w_qkv=v_attn_w_qkv, attn_b_qkv=v_attn_b_qkv, attn_sinks=v_attn_sinks, attn_w_o=v_attn_w_o, attn_b_o=v_attn_b_o, conv_w_pw1=v_conv_w_pw1, conv_b_pw1=v_conv_b_pw1, conv_w_dw=v_conv_w_dw, conv_b_dw=v_conv_b_dw, conv_ln_g=v_conv_ln_g, conv_ln_b=v_conv_ln_b, conv_w_pw2=v_conv_w_pw2, conv_b_pw2=v_conv_b_pw2, norm_ffn=v_norm_ffn, ffn_w_up=v_ffn_w_up, ffn_w_dw=v_ffn_w_dw, ffn_b_dw=v_ffn_b_dw, ffn_w_down=v_ffn_w_down, final_norm=v_final_norm)
    names = list(weights)
    big = ("attn_w_qkv", "attn_w_o", "conv_w_pw1", "conv_w_pw2", "ffn_w_up", "ffn_w_down")
    delta, new_m, new_v = {}, {}, {}
    for nm in big:
        shp = weights[nm].shape
        two_d = (shp[0] * shp[1], shp[2])
        res = _adamw("adamw_" + nm, *(t[nm].reshape(two_d) for t in (weights, grads, moms, vars_)))
        delta[nm], new_m[nm], new_v[nm] = (r.reshape(shp) for r in res)
    small_names = [nm for nm in names if nm not in big]
    small_shapes_local = [weights[nm].shape for nm in small_names]
    res = _adamw("adamw_small", *(_pack_rows([t[nm] for nm in small_names]) for t in (weights, grads, moms, vars_)))
    for tgt_dict, slab_out in zip((delta, new_m, new_v), res):
        for nm, val in zip(small_names, _unpack_rows(slab_out, small_shapes_local)):
            tgt_dict[nm] = val
    return (loss, grad_x.reshape(1, s, D), *[grads[nm] for nm in names], *[delta[nm] for nm in names],
            *[new_m[nm] for nm in names], *[new_v[nm] for nm in names])
```

```python
import functools
import math

import jax
import jax.numpy as jnp
from jax import lax
from jax.experimental import pallas as pl
from jax.experimental.pallas import tpu as pltpu

F32 = jnp.float32
BF16 = jnp.bfloat16

D = 1024
F = 2816
HEAD_DIM = 64
N_KV = 2
BLOCK = 128
NQ = 1024
KVW = 256
CONV_K = 31
RMS_EPS = 1e-6
LN_EPS = 1e-5
ADAM_LR, ADAM_B1, ADAM_B2, ADAM_EPS, ADAM_WD, ADAM_STEP = 0.001, 0.9, 0.999, 1e-08, 0.01, 10
N_DEV = 8
MESH = pl.DeviceIdType.MESH
VMEM_LIMIT_BYTES = 56 * 2**20
NEG = float(jnp.finfo(jnp.float32).min)

MAT_ROWS = (160, 128, 256, 128, 704, 352, 704, 352)
MAT_OFF = tuple(sum(MAT_ROWS[:i]) for i in range(len(MAT_ROWS)))
SLAB_ROWS = sum(MAT_ROWS)
N_MAT = len(MAT_ROWS)

NT_DIMS = (((1,), (1,)), ((), ()))
NN_DIMS = (((1,), (0,)), ((), ()))
TN_DIMS = (((0,), (0,)), ((), ()))


def _params(*sem):
    return pltpu.CompilerParams(dimension_semantics=tuple(sem), vmem_limit_bytes=VMEM_LIMIT_BYTES)


def _row_tile(rows, cap, mult=8):
    best = None
    for t in range(mult, min(rows, cap) + 1, mult):
        if rows % t == 0:
            best = t
    return best if best is not None else rows


def _sigmoid(x):
    return 1.0 / (1.0 + jnp.exp(-x))


def _mm(name, a, b, *, nt, tm, tn, out_dtype, rms_g=None, bias=None, res=None):
    m, k = a.shape
    n = b.shape[0] if nt else b.shape[1]
    tm = min(tm, m)
    has_rms = rms_g is not None
    dims = NT_DIMS if nt else NN_DIMS

    def body(*refs):
        a_ref, b_ref = refs[0], refs[1]
        pos = 2
        g_ref = bias_ref = res_ref = h_ref = hs_ref = None
        if has_rms:
            g_ref = refs[pos]; pos += 1
        if bias is not None:
            bias_ref = refs[pos]; pos += 1
        if res is not None:
            res_ref = refs[pos]; pos += 1
        o_ref = refs[pos]; pos += 1
        if has_rms:
            h_ref, hs_ref = refs[pos], refs[pos + 1]

            @pl.when(pl.program_id(1) == 0)
            def _():
                xf = a_ref[...]
                r = lax.rsqrt(jnp.mean(xf * xf, axis=-1, keepdims=True) + RMS_EPS)
                h = ((xf * r) * g_ref[...]).astype(BF16)
                hs_ref[...] = h
                h_ref[...] = h

            lhs = hs_ref[...]
        else:
            lhs = a_ref[...].astype(BF16)
        acc = lax.dot_general(lhs, b_ref[...], dims, preferred_element_type=F32)
        if bias is not None:
            acc = acc + bias_ref[...]
        if res is not None:
            acc = acc + res_ref[...]
        o_ref[...] = acc.astype(out_dtype)

    in_specs = [pl.BlockSpec((tm, k), lambda i, j: (i, 0)),
                pl.BlockSpec((tn, k), lambda i, j: (j, 0)) if nt else pl.BlockSpec((k, tn), lambda i, j: (0, j))]
    args = [a, b]
    if has_rms:
        in_specs.append(pl.BlockSpec((1, k), lambda i, j: (0, 0))); args.append(rms_g)
    if bias is not None:
        in_specs.append(pl.BlockSpec((1, tn), lambda i, j: (0, j))); args.append(bias)
    if res is not None:
        in_specs.append(pl.BlockSpec((tm, tn), lambda i, j: (i, j))); args.append(res)
    out_shape = [jax.ShapeDtypeStruct((m, n), out_dtype)]
    out_specs = [pl.BlockSpec((tm, tn), lambda i, j: (i, j))]
    scratch = []
    if has_rms:
        out_shape.append(jax.ShapeDtypeStruct((m, k), BF16))
        out_specs.append(pl.BlockSpec((tm, k), lambda i, j: (i, 0)))
        scratch.append(pltpu.VMEM((tm, k), BF16))
    outs = pl.pallas_call(body, grid=(m // tm, n // tn), in_specs=in_specs, out_specs=out_specs, out_shape=out_shape,
                          scratch_shapes=scratch, compiler_params=_params("parallel", "arbitrary"), name=name)(*args)
    return outs if has_rms else outs[0]


def _mm_tn(name, l, r, *, tn, tt, colsum_l=False, colsum_r=False):
    s, nl = l.shape
    nr = r.shape[1]
    tt = min(tt, s)
    n_t = s // tt

    def body(*refs):
        l_ref, r_ref, o_ref = refs[0], refs[1], refs[2]
        pos = 3
        cl_ref = cr_ref = None
        if colsum_l:
            cl_ref = refs[pos]; pos += 1
        if colsum_r:
            cr_ref = refs[pos]; pos += 1
        acc_ref = refs[pos]
        j, t = pl.program_id(0), pl.program_id(1)

        @pl.when(t == 0)
        def _():
            acc_ref[...] = jnp.zeros_like(acc_ref)

        lv = l_ref[...]
        rv = r_ref[...]
        acc_ref[...] += lax.dot_general(lv, rv.astype(BF16), TN_DIMS, preferred_element_type=F32)
        if colsum_l:
            @pl.when(t == 0)
            def _():
                cl_ref[...] = jnp.zeros_like(cl_ref)

            cl_ref[...] += jnp.sum(lv.astype(F32), axis=0, keepdims=True)
        if colsum_r:
            @pl.when((t == 0) & (j == 0))
            def _():
                cr_ref[...] = jnp.zeros_like(cr_ref)

            @pl.when(j == 0)
            def _():
                cr_ref[...] += jnp.sum(rv.astype(F32), axis=0, keepdims=True)

        @pl.when(t == n_t - 1)
        def _():
            o_ref[...] = acc_ref[...].astype(BF16)

    out_shape = [jax.ShapeDtypeStruct((nl, nr), BF16)]
    out_specs = [pl.BlockSpec((tn, nr), lambda j, t: (j, 0))]
    if colsum_l:
        out_shape.append(jax.ShapeDtypeStruct((1, nl), F32))
        out_specs.append(pl.BlockSpec((1, tn), lambda j, t: (0, j)))
    if colsum_r:
        out_shape.append(jax.ShapeDtypeStruct((1, nr), F32))
        out_specs.append(pl.BlockSpec((1, nr), lambda j, t: (0, 0)))
    outs = pl.pallas_call(
        body, grid=(nl // tn, n_t),
        in_specs=[pl.BlockSpec((tt, tn), lambda j, t: (t, j)), pl.BlockSpec((tt, nr), lambda j, t: (t, 0))],
        out_specs=out_specs, out_shape=out_shape, scratch_shapes=[pltpu.VMEM((tn, nr), F32)],
        compiler_params=_params("arbitrary", "arbitrary"), name=name)(l, r)
    return outs if (colsum_l or colsum_r) else outs[0]


def _rms_bwd(dh, xf, g):
    r = lax.rsqrt(jnp.mean(xf * xf, axis=-1, keepdims=True) + RMS_EPS)
    gd = dh * g
    dx = r * gd - xf * ((r * r * r) * jnp.mean(xf * gd, axis=-1, keepdims=True))
    return dx, dh * (xf * r)


def _dgrad_rms(name, dy, w, x, g, dres, *, tm, tk):
    m, k = dy.shape
    tm = min(tm, m)
    n_k = k // tk

    def body(dy_ref, w_ref, x_ref, g_ref, dres_ref, o_ref, dg_ref, acc_ref):
        i, kk = pl.program_id(0), pl.program_id(1)

        @pl.when(kk == 0)
        def _():
            acc_ref[...] = jnp.zeros_like(acc_ref)

        acc_ref[...] += lax.dot_general(dy_ref[...], w_ref[...], NN_DIMS, preferred_element_type=F32)

        @pl.when((i == 0) & (kk == n_k - 1))
        def _():
            dg_ref[...] = jnp.zeros_like(dg_ref)

        @pl.when(kk == n_k - 1)
        def _():
            dx, dg_rows = _rms_bwd(acc_ref[...], x_ref[...], g_ref[...])
            o_ref[...] = dres_ref[...] + dx
            dg_ref[...] += jnp.sum(dg_rows, axis=0, keepdims=True)

    return pl.pallas_call(
        body, grid=(m // tm, n_k),
        in_specs=[pl.BlockSpec((tm, tk), lambda i, kk: (i, kk)), pl.BlockSpec((tk, D), lambda i, kk: (kk, 0)),
                  pl.BlockSpec((tm, D), lambda i, kk: (i, 0)), pl.BlockSpec((1, D), lambda i, kk: (0, 0)),
                  pl.BlockSpec((tm, D), lambda i, kk: (i, 0))],
        out_specs=[pl.BlockSpec((tm, D), lambda i, kk: (i, 0)), pl.BlockSpec((1, D), lambda i, kk: (0, 0))],
        out_shape=[jax.ShapeDtypeStruct((m, D), F32), jax.ShapeDtypeStruct((1, D), F32)],
        scratch_shapes=[pltpu.VMEM((tm, D), F32)],
        compiler_params=_params("arbitrary", "arbitrary"), name=name)(dy, w, x, g, dres)


def _loss_head(x4, g, tgt, *, tm):
    m = x4.shape[0]
    tm = min(tm, m)

    def body(x_ref, g_ref, t_ref, dx_ref, loss_ref, dg_ref):
        @pl.when(pl.program_id(0) == 0)
        def _():
            loss_ref[...] = jnp.zeros_like(loss_ref)
            dg_ref[...] = jnp.zeros_like(dg_ref)

        xf = x_ref[...]
        gv = g_ref[...]
        r = lax.rsqrt(jnp.mean(xf * xf, axis=-1, keepdims=True) + RMS_EPS)
        diff = (xf * r) * gv - t_ref[...]
        loss_ref[...] += 0.5 * jnp.sum(jnp.mean(diff * diff, axis=-1, keepdims=True))
        dx, dg_rows = _rms_bwd(diff * (1.0 / D), xf, gv)
        dx_ref[...] = dx
        dg_ref[...] += jnp.sum(dg_rows, axis=0, keepdims=True)

    return pl.pallas_call(
        body, grid=(m // tm,),
        in_specs=[pl.BlockSpec((tm, D), lambda i: (i, 0)), pl.BlockSpec((1, D), lambda i: (0, 0)),
                  pl.BlockSpec((tm, D), lambda i: (i, 0))],
        out_specs=[pl.BlockSpec((tm, D), lambda i: (i, 0)), pl.BlockSpec((1, 128), lambda i: (0, 0)),
                   pl.BlockSpec((1, D), lambda i: (0, 0))],
        out_shape=[jax.ShapeDtypeStruct((m, D), F32), jax.ShapeDtypeStruct((1, 128), F32),
                   jax.ShapeDtypeStruct((1, D), F32)],
        compiler_params=_params("arbitrary"), name="loss_head")(x4, g, tgt)


def _band(kvp, kvc):
    kk = jnp.concatenate([kvp[:, :128], kvc[:, :128]], axis=0)
    vv = jnp.concatenate([kvp[:, 128:], kvc[:, 128:]], axis=0)
    return kk, vv


def _blockdiag(t, h):
    lane = lax.broadcasted_iota(jnp.int32, t.shape, 1)
    z = jnp.zeros_like(t)
    tr = pltpu.roll(t, 64, 1)
    if h == 0:
        top, bot = jnp.where(lane < 64, t, z), jnp.where(lane >= 64, tr, z)
    else:
        top, bot = jnp.where(lane < 64, tr, z), jnp.where(lane >= 64, t, z)
    return jnp.concatenate([top, bot], axis=0)


def _from_blockdiag(t, h):
    lane = lax.broadcasted_iota(jnp.int32, (256, 128), 1)
    top, bot = t[:256], t[256:]
    if h == 0:
        return jnp.where(lane < 64, top + pltpu.roll(bot, 64, 1), 0.0)
    return jnp.where(lane >= 64, pltpu.roll(top, 64, 1) + bot, 0.0)


def _stack_heads(ref, h):
    return jnp.concatenate([ref[:, h * 512 + p * 128: h * 512 + (p + 1) * 128] for p in range(4)], axis=0)


def _valid_mask(n):
    row = lax.broadcasted_iota(jnp.int32, (512, 256), 0) & 127
    col = lax.broadcasted_iota(jnp.int32, (512, 256), 1)
    return (col > row) & (col <= row + BLOCK) & ((col >= BLOCK) | (n > 0))


def _softmax_half(s_half, valid, sink):
    s_half = jnp.where(valid, s_half, NEG)
    mx = jnp.maximum(jnp.max(s_half, axis=-1, keepdims=True), sink)
    p = jnp.exp(s_half - mx)
    e_sink = jnp.exp(sink - mx)
    inv = 1.0 / (jnp.sum(p, axis=-1, keepdims=True) + e_sink)
    return p * inv, e_sink * inv


def _attn_fwd(qkv, sink_cols, ex=None, ex_args=()):
    s = qkv.shape[0]
    nb = s // BLOCK
    scale = 1.0 / math.sqrt(HEAD_DIM)

    def body(q_ref, kvc_ref, kvp_ref, sk_ref, o_ref):
        n = pl.program_id(0)
        kk, vv = _band(kvp_ref[...], kvc_ref[...])
        valid = _valid_mask(n)
        for h in range(N_KV):
            kbd, vbd = _blockdiag(kk, h), _blockdiag(vv, h)
            qp = _stack_heads(q_ref, h)
            sc = lax.dot_general(qp, kbd, NT_DIMS, preferred_element_type=F32) * scale
            probs = []
            for half in range(2):
                sink = sk_ref[h * 2 + half][:, :1]
                pr, _ = _softmax_half(sc[:, half * 256:(half + 1) * 256], valid, sink)
                probs.append(pr.astype(BF16))
            o = lax.dot_general(jnp.concatenate(probs, axis=1), vbd, NN_DIMS, preferred_element_type=F32)
            for p in range(4):
                o_ref[:, h * 512 + p * 128: h * 512 + (p + 1) * 128] = o[p * 128:(p + 1) * 128].astype(BF16)

    return pl.pallas_call(
        _carried(body, 4, 1, ex, nb), grid=(nb,),
        in_specs=[pl.BlockSpec((BLOCK, NQ), lambda n: (n, 0)),
                  pl.BlockSpec((BLOCK, KVW), lambda n: (n, NQ // KVW)),
                  pl.BlockSpec((BLOCK, KVW), lambda n: (jnp.maximum(n - 1, 0), NQ // KVW)),
                  pl.BlockSpec((4, 512, 128), lambda n: (0, 0, 0))] + _ex(ex, "in_specs"),
        out_specs=[pl.BlockSpec((BLOCK, NQ), lambda n: (n, 0))] + _ex(ex, "out_specs"),
        out_shape=[jax.ShapeDtypeStruct((s, NQ), BF16)] + _ex(ex, "out_shape"), scratch_shapes=_ex(ex, "scratch"),
        compiler_params=_params("arbitrary"), name="attn_fwd")(qkv, qkv, qkv, sink_cols, *ex_args)


def _attn_bwd(qkv, do, sink_cols, ex=None, ex_args=()):
    s = qkv.shape[0]
    nb = s // BLOCK
    scale = 1.0 / math.sqrt(HEAD_DIM)

    def body(q_ref, kvc_ref, kvp_ref, do_ref, sk_ref, dq_ref, dkv_ref, dsk_ref, ck_ref, cv_ref):
        n = pl.program_id(0)

        @pl.when(n == 0)
        def _():
            dsk_ref[...] = jnp.zeros_like(dsk_ref)
            ck_ref[...] = jnp.zeros_like(ck_ref)
            cv_ref[...] = jnp.zeros_like(cv_ref)

        @pl.when(n < nb)
        def _():
            kk, vv = _band(kvp_ref[...], kvc_ref[...])
            valid = _valid_mask(n)
            dkk = jnp.zeros((256, 128), F32)
            dvv = jnp.zeros((256, 128), F32)
            for h in range(N_KV):
                kbd, vbd = _blockdiag(kk, h), _blockdiag(vv, h)
                qp = _stack_heads(q_ref, h)
                dop = _stack_heads(do_ref, h)
                sc = lax.dot_general(qp, kbd, NT_DIMS, preferred_element_type=F32) * scale
                dp = lax.dot_general(dop, vbd, NT_DIMS, preferred_element_type=F32)
                probs, dss = [], []
                for half in range(2):
                    sink = sk_ref[h * 2 + half][:, :1]
                    pr, p_sink = _softmax_half(sc[:, half * 256:(half + 1) * 256], valid, sink)
                    dph = dp[:, half * 256:(half + 1) * 256]
                    delta = jnp.sum(pr * dph, axis=-1, keepdims=True)
                    dss.append((pr * (dph - delta) * scale).astype(BF16))
                    probs.append(pr.astype(BF16))
                    dsk_ref[h * 2 + half] += -(p_sink * delta)
                ds = jnp.concatenate(dss, axis=1)
                pb = jnp.concatenate(probs, axis=1)
                dqp = lax.dot_general(ds, kbd, NN_DIMS, preferred_element_type=F32)
                for p in range(4):
                    dq_ref[:, h * 512 + p * 128: h * 512 + (p + 1) * 128] = dqp[p * 128:(p + 1) * 128].astype(BF16)
                dkk = dkk + _from_blockdiag(lax.dot_general(ds, qp, TN_DIMS, preferred_element_type=F32), h)
                dvv = dvv + _from_blockdiag(lax.dot_general(pb, dop, TN_DIMS, preferred_element_type=F32), h)

            @pl.when(n >= 1)
            def _():
                dkv_ref[:, :128] = (ck_ref[...] + dkk[:128]).astype(BF16)
                dkv_ref[:, 128:] = (cv_ref[...] + dvv[:128]).astype(BF16)

            ck_ref[...] = dkk[128:]
            cv_ref[...] = dvv[128:]

        @pl.when(n == nb)
        def _():
            dkv_ref[:, :128] = ck_ref[...].astype(BF16)
            dkv_ref[:, 128:] = cv_ref[...].astype(BF16)

    last = nb - 1
    return pl.pallas_call(
        _carried(body, 5, 3, ex, nb + 1), grid=(nb + 1,),
        in_specs=[pl.BlockSpec((BLOCK, NQ), lambda n: (jnp.minimum(n, last), 0)),
                  pl.BlockSpec((BLOCK, KVW), lambda n: (jnp.minimum(n, last), NQ // KVW)),
                  pl.BlockSpec((BLOCK, KVW), lambda n: (jnp.clip(n - 1, 0, last), NQ // KVW)),
                  pl.BlockSpec((BLOCK, NQ), lambda n: (jnp.minimum(n, last), 0)),
                  pl.BlockSpec((4, 512, 128), lambda n: (0, 0, 0))] + _ex(ex, "in_specs"),
        out_specs=[pl.BlockSpec((BLOCK, NQ), lambda n: (jnp.minimum(n, last), 0)),
                   pl.BlockSpec((BLOCK, KVW), lambda n: (jnp.maximum(n - 1, 0), 0)),
                   pl.BlockSpec((4, 512, 1), lambda n: (0, 0, 0))] + _ex(ex, "out_specs"),
        out_shape=[jax.ShapeDtypeStruct((s, NQ), BF16), jax.ShapeDtypeStruct((s, KVW), BF16),
                   jax.ShapeDtypeStruct((4, 512, 1), F32)] + _ex(ex, "out_shape"),
        scratch_shapes=[pltpu.VMEM((BLOCK, 128), F32), pltpu.VMEM((BLOCK, 128), F32)] + _ex(ex, "scratch"),
        compiler_params=_params("arbitrary"), name="attn_bwd")(qkv, qkv, qkv, do, sink_cols, *ex_args)


LANE = 128
FFN_HALF = F // 2
FFN_HALO = 8


def _resident(shape):
    return pl.BlockSpec(shape, lambda i: (0,) * len(shape), pipeline_mode=pl.Buffered(1))


def _ffn_fwd(tag, x_in, g, w_up_t, w_dw, b_dw, w_down, *, tm, ex=None, ex_args=()):
    s = x_in.shape[0]
    tm = min(tm, s)

    def body(x_ref, g_ref, wup_ref, wdw_ref, bdw_ref, wd_ref, o_ref, h_ref, up_ref, act_ref, carry_ref, ext_ref):
        @pl.when(pl.program_id(0) == 0)
        def _():
            carry_ref[...] = jnp.zeros_like(carry_ref)

        xf = x_ref[...]
        r = lax.rsqrt(jnp.mean(xf * xf, axis=-1, keepdims=True) + RMS_EPS)
        h = ((xf * r) * g_ref[...]).astype(BF16)
        h_ref[...] = h
        acc = xf
        for half in range(2):
            for off in (half * FFN_HALF, F + half * FFN_HALF):
                rs = slice(off, off + FFN_HALF)
                up_ref[:, rs] = lax.dot_general(h, wup_ref[rs, :], NT_DIMS, preferred_element_type=F32).astype(BF16)
            for c in range(half * (FFN_HALF // LANE), (half + 1) * (FFN_HALF // LANE)):
                cs = slice(c * LANE, (c + 1) * LANE)
                vs = slice(F + c * LANE, F + (c + 1) * LANE)
                gcol = up_ref[:, cs].astype(F32)
                ext_ref[pl.ds(0, FFN_HALO), :] = carry_ref[:, cs]
                ext_ref[pl.ds(FFN_HALO, tm), :] = gcol
                gate = (wdw_ref[2:3, cs] * gcol + wdw_ref[1:2, cs] * ext_ref[pl.ds(FFN_HALO - 1, tm), :]
                        + wdw_ref[0:1, cs] * ext_ref[pl.ds(FFN_HALO - 2, tm), :] + bdw_ref[:, cs])
                act_ref[:, cs] = (gate * _sigmoid(gate) * up_ref[:, vs].astype(F32)).astype(BF16)
                carry_ref[:, cs] = gcol[tm - FFN_HALO:]
            hs = slice(half * FFN_HALF, (half + 1) * FFN_HALF)
            acc = acc + lax.dot_general(act_ref[:, hs], wd_ref[hs, :], NN_DIMS, preferred_element_type=F32)
        o_ref[...] = acc

    row = lambda i: (i, 0)
    return pl.pallas_call(
        _carried(body, 6, 4, ex, s // tm), grid=(s // tm,),
        in_specs=[pl.BlockSpec((tm, D), row), _resident((1, D)), _resident((2 * F, D)), _resident((3, F)),
                  _resident((1, F)), _resident((F, D))] + _ex(ex, "in_specs"),
        out_specs=[pl.BlockSpec((tm, D), row), pl.BlockSpec((tm, D), row), pl.BlockSpec((tm, 2 * F), row),
                   pl.BlockSpec((tm, F), row)] + _ex(ex, "out_specs"),
        out_shape=[jax.ShapeDtypeStruct((s, D), F32), jax.ShapeDtypeStruct((s, D), BF16),
                   jax.ShapeDtypeStruct((s, 2 * F), BF16), jax.ShapeDtypeStruct((s, F), BF16)] + _ex(ex, "out_shape"),
        scratch_shapes=[pltpu.VMEM((FFN_HALO, F), F32), pltpu.VMEM((tm + FFN_HALO, LANE), F32)] + _ex(ex, "scratch"),
        compiler_params=_params("arbitrary"), name=f"ffn{tag}_fwd")(x_in, g, w_up_t, w_dw, b_dw, w_down, *ex_args)


def _ffn_bwd(tag, dx_out, x_in, g, w_up_t, w_dw, b_dw, w_down, up, *, tm):
    s = x_in.shape[0]
    tm = min(tm, s)
    n_i = s // tm
    hb = tm // FFN_HALO

    def body(dx_ref, x_ref, g_ref, up_ref, prev_ref, wup_ref, wdw_ref, bdw_ref, wd_ref,
             dxin_ref, dup_ref, dg_ref, dwb_ref, carry_ref, da_ref, extg_ref, extd_ref):
        i = pl.program_id(0)

        @pl.when(i == 0)
        def _():
            carry_ref[...] = jnp.zeros_like(carry_ref)
            dg_ref[...] = jnp.zeros_like(dg_ref)
            dwb_ref[...] = jnp.zeros_like(dwb_ref)

        first_tile = i == n_i - 1
        dxf = dx_ref[...]
        dxb = dxf.astype(BF16)
        acc = None
        for half in range(2):
            hs = slice(half * FFN_HALF, (half + 1) * FFN_HALF)
            vhs = slice(F + half * FFN_HALF, F + (half + 1) * FFN_HALF)
            da_ref[...] = lax.dot_general(dxb, wd_ref[hs, :], NT_DIMS, preferred_element_type=F32)
            for c in range(FFN_HALF // LANE):
                cc = half * (FFN_HALF // LANE) + c
                cs = slice(cc * LANE, (cc + 1) * LANE)
                vs = slice(F + cc * LANE, F + (cc + 1) * LANE)
                w0, w1, w2 = wdw_ref[0:1, cs], wdw_ref[1:2, cs], wdw_ref[2:3, cs]
                gcol = up_ref[:, cs].astype(F32)
                extg_ref[pl.ds(0, FFN_HALO), :] = jnp.where(first_tile, 0.0, prev_ref[:, cs].astype(F32))
                extg_ref[pl.ds(FFN_HALO, tm), :] = gcol
                gm1 = extg_ref[pl.ds(FFN_HALO - 1, tm), :]
                gm2 = extg_ref[pl.ds(FFN_HALO - 2, tm), :]
                gate = w2 * gcol + w1 * gm1 + w0 * gm2 + bdw_ref[:, cs]
                sg = _sigmoid(gate)
                dac = da_ref[:, c * LANE:(c + 1) * LANE]
                dup_ref[:, vs] = (dac * (gate * sg)).astype(BF16)
                dgate = dac * up_ref[:, vs].astype(F32) * (sg * (1.0 + gate * (1.0 - sg)))
                extd_ref[pl.ds(0, tm), :] = dgate
                extd_ref[pl.ds(tm, FFN_HALO), :] = carry_ref[:, cs]
                dup_ref[:, cs] = (w2 * dgate + w1 * extd_ref[pl.ds(1, tm), :] + w0 * extd_ref[pl.ds(2, tm), :]).astype(BF16)
                carry_ref[:, cs] = dgate[:FFN_HALO]
                dwb_ref[2:3, cs] += jnp.sum(dgate * gcol, axis=0, keepdims=True)
                dwb_ref[1:2, cs] += jnp.sum(dgate * gm1, axis=0, keepdims=True)
                dwb_ref[0:1, cs] += jnp.sum(dgate * gm2, axis=0, keepdims=True)
                dwb_ref[3:4, cs] += jnp.sum(dgate, axis=0, keepdims=True)
            part = (lax.dot_general(dup_ref[:, hs], wup_ref[hs, :], NN_DIMS, preferred_element_type=F32)
                    + lax.dot_general(dup_ref[:, vhs], wup_ref[vhs, :], NN_DIMS, preferred_element_type=F32))
            acc = part if acc is None else acc + part
        dx, dg_rows = _rms_bwd(acc, x_ref[...], g_ref[...])
        dxin_ref[...] = dxf + dx
        dg_ref[...] += jnp.sum(dg_rows, axis=0, keepdims=True)

    rev = lambda i: (n_i - 1 - i, 0)
    return pl.pallas_call(
        body, grid=(n_i,),
        in_specs=[pl.BlockSpec((tm, D), rev), pl.BlockSpec((tm, D), rev), _resident((1, D)),
                  pl.BlockSpec((tm, 2 * F), rev),
                  pl.BlockSpec((FFN_HALO, 2 * F), lambda i: (jnp.maximum((n_i - 1 - i) * hb - 1, 0), 0)),
                  _resident((2 * F, D)), _resident((3, F)), _resident((1, F)), _resident((F, D))],
        out_specs=[pl.BlockSpec((tm, D), rev), pl.BlockSpec((tm, 2 * F), rev),
                   pl.BlockSpec((1, D), lambda i: (0, 0)), pl.BlockSpec((8, F), lambda i: (0, 0))],
        out_shape=[jax.ShapeDtypeStruct((s, D), F32), jax.ShapeDtypeStruct((s, 2 * F), BF16),
                   jax.ShapeDtypeStruct((1, D), F32), jax.ShapeDtypeStruct((8, F), F32)],
        scratch_shapes=[pltpu.VMEM((FFN_HALO, F), F32), pltpu.VMEM((tm, FFN_HALF), F32),
                        pltpu.VMEM((tm + FFN_HALO, LANE), F32), pltpu.VMEM((tm + FFN_HALO, LANE), F32)],
        compiler_params=_params("arbitrary"), name=f"ffn{tag}_bwd")(dx_out, x_in, g, up, up, w_up_t, w_dw, b_dw, w_down)


CONV_HALO = 32
CONV_CHUNK = 256
CONV_ROWS = 64


def _ln(u2, g, b):
    mu = jnp.mean(u2, axis=-1, keepdims=True)
    xc = u2 - mu
    rstd = lax.rsqrt(jnp.mean(xc * xc, axis=-1, keepdims=True) + LN_EPS)
    xhat = xc * rstd
    return xhat, rstd, xhat * g + b


def _phase_taps(ph):
    first = CONV_HALO - (CONV_K - 1)
    return [(k, first + k - ph) for k in range(CONV_K) if (first + k) % 8 == ph]


def _sum8(v):
    out = v[0:8]
    for j in range(8, v.shape[0], 8):
        out = out + v[j:j + 8]
    return out


def _store_phases(src_ref, sh_ref, tt, cols=slice(None)):
    sh_ref[0] = src_ref[:, cols]
    for ph in range(1, 8):
        sh_ref[ph, pl.ds(0, tt + CONV_HALO - 8), :] = src_ref[pl.ds(ph, tt + CONV_HALO - 8), cols]


def _conv_mid_fwd(a, wdw, ln_g, ln_b, *, tt):
    s = a.shape[0]
    tt = min(tt, s)
    hb = tt // CONV_HALO

    def body(a_ref, prev_ref, w_ref, g_ref, b_ref, s_ref, u2_ref, ext_ref, sh_ref):
        i = pl.program_id(0)
        for c in range(D // CONV_CHUNK):
            cs = slice(c * CONV_CHUNK, (c + 1) * CONV_CHUNK)
            gs = slice(D + c * CONV_CHUNK, D + (c + 1) * CONV_CHUNK)
            uh = prev_ref[:, cs].astype(F32) * _sigmoid(prev_ref[:, gs].astype(F32))
            ext_ref[pl.ds(0, CONV_HALO), :] = jnp.where(i > 0, uh, 0.0)
            ext_ref[pl.ds(CONV_HALO, tt), :] = a_ref[:, cs].astype(F32) * _sigmoid(a_ref[:, gs].astype(F32))
            _store_phases(ext_ref, sh_ref, tt)
            for base in range(0, tt, CONV_ROWS):
                acc = jnp.zeros((CONV_ROWS, CONV_CHUNK), F32) + w_ref[CONV_K:CONV_K + 1, cs]
                for ph in range(8):
                    for k, off in _phase_taps(ph):
                        acc = acc + w_ref[k:k + 1, cs] * sh_ref[ph, pl.ds(base + off, CONV_ROWS), :]
                u2_ref[pl.ds(base, CONV_ROWS), cs] = acc.astype(BF16)
        _, _, ln = _ln(u2_ref[...].astype(F32), g_ref[...], b_ref[...])
        s_ref[...] = (ln * _sigmoid(ln)).astype(BF16)

    return pl.pallas_call(
        body, grid=(s // tt,),
        in_specs=[pl.BlockSpec((tt, 2 * D), lambda i: (i, 0)),
                  pl.BlockSpec((CONV_HALO, 2 * D), lambda i: (jnp.maximum(i * hb - 1, 0), 0)),
                  pl.BlockSpec((32, D), lambda i: (0, 0)), pl.BlockSpec((1, D), lambda i: (0, 0)),
                  pl.BlockSpec((1, D), lambda i: (0, 0))],
        out_specs=[pl.BlockSpec((tt, D), lambda i: (i, 0)), pl.BlockSpec((tt, D), lambda i: (i, 0))],
        out_shape=[jax.ShapeDtypeStruct((s, D), BF16), jax.ShapeDtypeStruct((s, D), BF16)],
        scratch_shapes=[pltpu.VMEM((tt + CONV_HALO, CONV_CHUNK), F32), pltpu.VMEM((8, tt + CONV_HALO, CONV_CHUNK), F32)],
        compiler_params=_params("parallel"), name="conv_mid_fwd")(a, a, wdw, ln_g, ln_b)


def _conv_mid_bwd(ds, u2, a, wdw, ln_g, ln_b, *, tt, ex=None, ex_args=()):
    s = a.shape[0]
    tt = min(tt, s)
    hb = tt // CONV_HALO
    n_i = s // tt
    te = tt + CONV_HALO

    def body(ds_ref, dsn_ref, u2_ref, u2n_ref, a_ref, prev_ref, w_ref, g_ref, b_ref, da_ref, acc_ref, e2_ref, ext_ref,
             ush_ref, dsh_ref, du_ref):
        i = pl.program_id(0)

        @pl.when(i == 0)
        def _():
            acc_ref[...] = jnp.zeros_like(acc_ref)

        u2e = jnp.concatenate([u2_ref[...], u2n_ref[...]], axis=0).astype(F32)
        dse = jnp.concatenate([ds_ref[...], dsn_ref[...]], axis=0).astype(F32)
        gv = g_ref[...]
        xhat, rstd, ln = _ln(u2e, gv, b_ref[...])
        sg = _sigmoid(ln)
        dln = dse * (sg * (1.0 + ln * (1.0 - sg)))
        acc_ref[32] += _sum8(dln[:tt] * xhat[:tt])
        acc_ref[33] += _sum8(dln[:tt])
        dxh = dln * gv
        du2 = rstd * (dxh - jnp.mean(dxh, axis=-1, keepdims=True) - xhat * jnp.mean(dxh * xhat, axis=-1, keepdims=True))
        row = lax.broadcasted_iota(jnp.int32, (te, D), 0)
        e2_ref[...] = jnp.where((row < tt) | (i < n_i - 1), du2, 0.0)
        for c in range(D // CONV_CHUNK):
            cs = slice(c * CONV_CHUNK, (c + 1) * CONV_CHUNK)
            gs = slice(D + c * CONV_CHUNK, D + (c + 1) * CONV_CHUNK)
            uh = prev_ref[:, cs].astype(F32) * _sigmoid(prev_ref[:, gs].astype(F32))
            ext_ref[pl.ds(0, CONV_HALO), :] = jnp.where(i > 0, uh, 0.0)
            a1 = a_ref[:, cs].astype(F32)
            sg2 = _sigmoid(a_ref[:, gs].astype(F32))
            ext_ref[pl.ds(CONV_HALO, tt), :] = a1 * sg2
            _store_phases(ext_ref, ush_ref, tt)
            _store_phases(e2_ref, dsh_ref, tt, cs)
            for base in range(0, tt, CONV_ROWS):
                d0 = e2_ref[pl.ds(base, CONV_ROWS), cs]
                du = jnp.zeros((CONV_ROWS, CONV_CHUNK), F32)
                for ph in range(8):
                    for k, off in _phase_taps(ph):
                        acc_ref[k, :, cs] += _sum8(d0 * ush_ref[ph, pl.ds(base + off, CONV_ROWS), :])
                    for k in range(CONV_K):
                        if (CONV_K - 1 - k) % 8 == ph:
                            du = du + w_ref[k:k + 1, cs] * dsh_ref[ph, pl.ds(base + CONV_K - 1 - k - ph, CONV_ROWS), :]
                acc_ref[CONV_K, :, cs] += _sum8(d0)
                du_ref[pl.ds(base, CONV_ROWS), :] = du
            du = du_ref[...]
            da_ref[:, cs] = (du * sg2).astype(BF16)
            da_ref[:, gs] = (du * a1 * (sg2 * (1.0 - sg2))).astype(BF16)

    last_h = s // CONV_HALO - 1
    nxt = lambda i: (jnp.minimum((i + 1) * hb, last_h), 0)
    return pl.pallas_call(
        _carried(body, 9, 2, ex, n_i), grid=(n_i,),
        in_specs=[pl.BlockSpec((tt, D), lambda i: (i, 0)), pl.BlockSpec((CONV_HALO, D), nxt),
                  pl.BlockSpec((tt, D), lambda i: (i, 0)), pl.BlockSpec((CONV_HALO, D), nxt),
                  pl.BlockSpec((tt, 2 * D), lambda i: (i, 0)),
                  pl.BlockSpec((CONV_HALO, 2 * D), lambda i: (jnp.maximum(i * hb - 1, 0), 0)),
                  pl.BlockSpec((32, D), lambda i: (0, 0)), pl.BlockSpec((1, D), lambda i: (0, 0)),
                  pl.BlockSpec((1, D), lambda i: (0, 0))] + _ex(ex, "in_specs"),
        out_specs=[pl.BlockSpec((tt, 2 * D), lambda i: (i, 0)), pl.BlockSpec((40, 8, D), lambda i: (0, 0, 0))] + _ex(ex, "out_specs"),
        out_shape=[jax.ShapeDtypeStruct((s, 2 * D), BF16), jax.ShapeDtypeStruct((40, 8, D), F32)] + _ex(ex, "out_shape"),
        scratch_shapes=[pltpu.VMEM((te, D), F32), pltpu.VMEM((te, CONV_CHUNK), F32),
                        pltpu.VMEM((8, te, CONV_CHUNK), F32), pltpu.VMEM((8, te, CONV_CHUNK), F32),
                        pltpu.VMEM((tt, CONV_CHUNK), F32)] + _ex(ex, "scratch"),
        compiler_params=_params("arbitrary"), name="conv_mid_bwd")(ds, ds, u2, u2, a, a, wdw, ln_g, ln_b, *ex_args)


def _adamw(name, w, g, m, v):
    rows, cols = w.shape
    tr = _row_tile(rows, 256)

    def body(w_ref, g_ref, m_ref, v_ref, d_ref, nm_ref, nv_ref):
        gv = g_ref[...]
        m2 = ADAM_B1 * m_ref[...] + (1.0 - ADAM_B1) * gv
        v2 = ADAM_B2 * v_ref[...] + (1.0 - ADAM_B2) * (gv * gv)
        m_hat = m2 / (1.0 - ADAM_B1 ** ADAM_STEP)
        v_hat = v2 / (1.0 - ADAM_B2 ** ADAM_STEP)
        d_ref[...] = -ADAM_LR * (m_hat / (jnp.sqrt(v_hat) + ADAM_EPS) + ADAM_WD * w_ref[...])
        nm_ref[...] = m2
        nv_ref[...] = v2

    spec = pl.BlockSpec((tr, cols), lambda i: (i, 0))
    return pl.pallas_call(
        body, grid=(rows // tr,), in_specs=[spec] * 4, out_specs=[spec] * 3,
        out_shape=[jax.ShapeDtypeStruct((rows, cols), F32)] * 3,
        compiler_params=_params("parallel"), name=name)(w, g, m, v)


def _sum_slabs(name, buf):
    rows = buf.shape[1]
    tr = _row_tile(rows, 512, 16)

    def body(b_ref, o_ref):
        acc = b_ref[0].astype(F32)
        for k in range(1, N_DEV):
            acc = acc + b_ref[k].astype(F32)
        o_ref[...] = acc

    return pl.pallas_call(
        body, grid=(rows // tr,), in_specs=[pl.BlockSpec((N_DEV, tr, D), lambda i: (0, i, 0))],
        out_specs=pl.BlockSpec((tr, D), lambda i: (i, 0)), out_shape=jax.ShapeDtypeStruct((rows, D), F32),
        compiler_params=_params("parallel"), name=name)(buf)


def _ex(ex, field):
    return list(getattr(ex, field)) if ex is not None else []


def _me():
    x, y, c = lax.axis_index("x"), lax.axis_index("y"), lax.axis_index("c")
    return x, y, c, 4 * x + 2 * y + c


def _peer(x, y, c, k):
    return (x ^ (k >> 2), y ^ ((k >> 1) & 1), c ^ (k & 1))


class _Gather:
    def __init__(self, mats):
        self.mats = tuple(mats)
        self.rows = sum(MAT_ROWS[i] for i in self.mats)
        any_spec = pl.BlockSpec(memory_space=pl.ANY)
        self.n_in, self.n_out = 1, len(self.mats)
        self.in_specs = [any_spec]
        self.out_specs = [any_spec] * self.n_out
        self.out_shape = [jax.ShapeDtypeStruct((N_DEV * MAT_ROWS[i], D), BF16) for i in self.mats]
        self.scratch = [pltpu.SemaphoreType.DMA((N_DEV - 1,)), pltpu.SemaphoreType.DMA((N_DEV - 1,)),
                        pltpu.SemaphoreType.DMA]
        assert max(s.shape[0] for s in self.out_shape) >= self.rows

    def start(self, ins, outs, scr):
        (slab_ref,), (send_sems, recv_sems, local_sem) = ins, scr
        x, y, c, me = _me()
        for n, i in enumerate(self.mats):
            src = slab_ref.at[pl.ds(MAT_OFF[i], MAT_ROWS[i])]
            dst = outs[n].at[pl.ds(me * MAT_ROWS[i], MAT_ROWS[i])]
            pltpu.make_async_copy(src, dst, local_sem).start()
            for k in range(1, N_DEV):
                pltpu.make_async_remote_copy(src_ref=src, dst_ref=dst, send_sem=send_sems.at[k - 1],
                                             recv_sem=recv_sems.at[k - 1], device_id=_peer(x, y, c, k),
                                             device_id_type=MESH).start()

    def wait(self, ins, outs, scr):
        send_sems, recv_sems, local_sem = scr
        x, y, c, _ = _me()
        big = max(range(self.n_out), key=lambda n: self.out_shape[n].shape[0])
        whole = outs[big].at[pl.ds(0, self.rows)]
        for k in range(1, N_DEV):
            pltpu.make_async_remote_copy(src_ref=whole, dst_ref=whole, send_sem=send_sems.at[k - 1],
                                         recv_sem=recv_sems.at[k - 1], device_id=_peer(x, y, c, k),
                                         device_id_type=MESH).wait()
        pltpu.make_async_copy(whole, whole, local_sem).wait()


class _Scatter:
    def __init__(self, mats):
        self.mats = tuple(mats)
        self.rows = sum(MAT_ROWS[i] for i in self.mats)
        any_spec = pl.BlockSpec(memory_space=pl.ANY)
        self.n_in, self.n_out = len(self.mats), 1
        self.in_specs = [any_spec] * self.n_in
        self.out_specs = [any_spec]
        self.out_shape = [jax.ShapeDtypeStruct((N_DEV, self.rows, D), BF16)]
        self.scratch = [pltpu.SemaphoreType.DMA((N_DEV - 1,)), pltpu.SemaphoreType.DMA((N_DEV - 1,)),
                        pltpu.SemaphoreType.DMA]

    def offsets(self):
        return [sum(MAT_ROWS[i] for i in self.mats[:n]) for n in range(len(self.mats))]

    def start(self, ins, outs, scr):
        (buf_ref,), (send_sems, recv_sems, local_sem) = outs, scr
        x, y, c, me = _me()
        for n, (i, off) in enumerate(zip(self.mats, self.offsets())):
            r = MAT_ROWS[i]
            pltpu.make_async_copy(ins[n].at[pl.ds(me * r, r)], buf_ref.at[0, pl.ds(off, r)], local_sem).start()
            for k in range(1, N_DEV):
                pltpu.make_async_remote_copy(src_ref=ins[n].at[pl.ds((me ^ k) * r, r)], dst_ref=buf_ref.at[k, pl.ds(off, r)],
                                             send_sem=send_sems.at[k - 1], recv_sem=recv_sems.at[k - 1],
                                             device_id=_peer(x, y, c, k), device_id_type=MESH).start()

    def wait(self, ins, outs, scr):
        (buf_ref,), (send_sems, recv_sems, local_sem) = outs, scr
        x, y, c, _ = _me()
        whole = buf_ref.at[0]
        for k in range(1, N_DEV):
            pltpu.make_async_remote_copy(src_ref=whole, dst_ref=whole, send_sem=send_sems.at[k - 1],
                                         recv_sem=recv_sems.at[k - 1], device_id=_peer(x, y, c, k),
                                         device_id_type=MESH).wait()
        pltpu.make_async_copy(whole, whole, local_sem).wait()


def _carried(body, n_in, n_out, ex, n_steps):
    if ex is None:
        return body

    def wrapped(*refs):
        ins, ex_ins = refs[:n_in], refs[n_in:n_in + ex.n_in]
        p = n_in + ex.n_in
        outs, ex_outs = refs[p:p + n_out], refs[p + n_out:p + n_out + ex.n_out]
        p += n_out + ex.n_out
        n_scr = len(refs) - p - len(ex.scratch)
        scr, ex_scr = refs[p:p + n_scr], refs[p + n_scr:]

        @pl.when(pl.program_id(0) == 0)
        def _():
            ex.start(ex_ins, ex_outs, ex_scr)

        body(*ins, *outs, *scr)

        @pl.when(pl.program_id(0) == n_steps - 1)
        def _():
            ex.wait(ex_ins, ex_outs, ex_scr)

    return wrapped


def _exchange_small(name, ex, ex_args, small, reduce):
    vmem_spec = pl.BlockSpec(memory_space=pltpu.VMEM)
    n_small = small.shape[0]

    def body(*refs):
        ex_ins, small_ref = refs[:ex.n_in], refs[ex.n_in]
        p = ex.n_in + 1
        ex_outs, res_ref = refs[p:p + ex.n_out], refs[p + ex.n_out]
        p += ex.n_out + 1
        if reduce:
            all_ref, p = refs[p], p + 1
        else:
            all_ref = res_ref
        sm_send, sm_recv = refs[p], refs[p + 1]
        ex_scr = refs[p + 2:]
        x, y, c, me = _me()
        ex.start(ex_ins, ex_outs, ex_scr)
        copies = [pltpu.make_async_remote_copy(
            src_ref=small_ref, dst_ref=all_ref.at[me], send_sem=sm_send.at[k - 1], recv_sem=sm_recv.at[k - 1],
            device_id=_peer(x, y, c, k), device_id_type=MESH) for k in range(1, N_DEV)]
        for cp in copies:
            cp.start()
        all_ref[me] = small_ref[...]
        for cp in copies:
            cp.wait()
        if reduce:
            acc = all_ref[0]
            for d in range(1, N_DEV):
                acc = acc + all_ref[d]
            res_ref[...] = acc
        ex.wait(ex_ins, ex_outs, ex_scr)

    res_shape = (n_small, D) if reduce else (N_DEV, n_small, D)
    scratch = ([pltpu.VMEM((N_DEV, n_small, D), F32)] if reduce else []) + [
        pltpu.SemaphoreType.DMA((N_DEV - 1,)), pltpu.SemaphoreType.DMA((N_DEV - 1,))] + ex.scratch
    outs = pl.pallas_call(
        body, in_specs=ex.in_specs + [vmem_spec], out_specs=ex.out_specs + [vmem_spec],
        out_shape=ex.out_shape + [jax.ShapeDtypeStruct(res_shape, F32)], scratch_shapes=scratch,
        compiler_params=pltpu.CompilerParams(vmem_limit_bytes=VMEM_LIMIT_BYTES), name=name)(*ex_args, small)
    return outs


def _pack_rows(arrs):
    rows = []
    for a in arrs:
        flat = a.reshape(-1).astype(F32)
        pad = (-flat.shape[0]) % D
        rows.append(jnp.pad(flat, (0, pad)).reshape(-1, D))
    slab = jnp.concatenate(rows, axis=0)
    return jnp.pad(slab, ((0, (-slab.shape[0]) % 8), (0, 0)))


def _unpack_rows(slab, shapes):
    out, r = [], 0
    for shp in shapes:
        size = math.prod(shp)
        nrows = -(-size // D)
        out.append(slab[r:r + nrows].reshape(-1)[:size].reshape(shp))
        r += nrows
    return out


def kernel(x, norm_mix, attn_w_qkv, attn_b_qkv, attn_sinks, attn_w_o, attn_b_o, conv_w_pw1, conv_b_pw1, conv_w_dw, conv_b_dw, conv_ln_g, conv_ln_b, conv_w_pw2, conv_b_pw2, norm_ffn, ffn_w_up, ffn_w_dw, ffn_b_dw, ffn_w_down, final_norm, loss_target, m_norm_mix, m_attn_w_qkv, m_attn_b_qkv, m_attn_sinks, m_attn_w_o, m_attn_b_o, m_conv_w_pw1, m_conv_b_pw1, m_conv_w_dw, m_conv_b_dw, m_conv_ln_g, m_conv_ln_b, m_conv_w_pw2, m_conv_b_pw2, m_norm_ffn, m_ffn_w_up, m_ffn_w_dw, m_ffn_b_dw, m_ffn_w_down, m_final_norm, v_norm_mix, v_attn_w_qkv, v_attn_b_qkv, v_attn_sinks, v_attn_w_o, v_attn_b_o, v_conv_w_pw1, v_conv_b_pw1, v_conv_w_dw, v_conv_b_dw, v_conv_ln_g, v_conv_ln_b, v_conv_w_pw2, v_conv_b_pw2, v_norm_ffn, v_ffn_w_up, v_ffn_w_dw, v_ffn_b_dw, v_ffn_w_down, v_final_norm):
    s = x.shape[1]
    me = 4 * lax.axis_index("x") + 2 * lax.axis_index("y") + lax.axis_index("c")
    x0 = x.reshape(s, D)
    tgt = loss_target.reshape(s, D)

    slab = jnp.concatenate([attn_w_qkv[0].T, attn_w_o[0], conv_w_pw1[0].T, conv_w_pw2[0],
                            ffn_w_up[0].T, ffn_w_down[0], ffn_w_up[1].T, ffn_w_down[1]], axis=0).astype(BF16)
    sharded_small = [conv_b_pw1, conv_w_dw, conv_b_dw, conv_ln_g, conv_ln_b, conv_b_pw2, ffn_w_dw]
    small_local = _pack_rows(sharded_small)
    w_qkv_t, small_all = _exchange_small("gather_first", _Gather((0,)), (slab,), small_local, reduce=False)
    parts = [_unpack_rows(small_all[d], [a.shape for a in sharded_small]) for d in range(N_DEV)]
    b_pw1, w_cdw, b_cdw, ln_g, ln_b, b_pw2, w_fdw = [jnp.concatenate([parts[d][i] for d in range(N_DEV)], axis=-1)
                                                      for i in range(len(sharded_small))]
    conv_w32 = jnp.concatenate([w_cdw[0], b_cdw], axis=0)
    sr = attn_sinks.reshape(N_KV, 4, 2)
    sink_cols = jnp.broadcast_to(jnp.repeat(jnp.transpose(sr, (0, 2, 1)), BLOCK, axis=-1).reshape(4, 512, 1),
                                 (4, 512, 128))

    qkv, h0 = _mm("qkv", x0, w_qkv_t, nt=True, tm=512, tn=NQ + KVW, out_dtype=BF16, rms_g=norm_mix[0:1], bias=attn_b_qkv)
    o, w_o, w_up0_t, w_dn0 = _attn_fwd(qkv, sink_cols, _Gather((1, 4, 5)), (slab,))
    x1 = _mm("attn_out", o, w_o, nt=False, tm=512, tn=D, out_dtype=F32, bias=attn_b_o, res=x0)
    x2, h1, up0, act0, w_pw1_t, w_pw2, w_up1_t, w_dn1 = _ffn_fwd(0, x1, norm_ffn[0:1], w_up0_t, w_fdw[0], ffn_b_dw[0:1], w_dn0,
                                                             tm=256, ex=_Gather((2, 3, 6, 7)), ex_args=(slab,))
    a, h2 = _mm("pw1", x2, w_pw1_t, nt=True, tm=512, tn=2 * D, out_dtype=BF16, rms_g=norm_mix[1:2], bias=b_pw1)
    s_act, u2 = _conv_mid_fwd(a, conv_w32, ln_g, ln_b, tt=256)
    x3 = _mm("pw2", s_act, w_pw2, nt=False, tm=512, tn=D, out_dtype=F32, bias=b_pw2, res=x2)
    x4, h3, up1, act1 = _ffn_fwd(1, x3, norm_ffn[1:2], w_up1_t, w_fdw[1], ffn_b_dw[1:2], w_dn1, tm=256)
    dx4, loss_acc, dg_final = _loss_head(x4, final_norm.reshape(1, D), tgt, tm=512)
    loss = lax.psum(loss_acc[0, 0], ("x", "y", "c"))

    dx3, d_up1, dg_f1, dwb1 = _ffn_bwd(1, dx4, x3, norm_ffn[1:2], w_up1_t, w_fdw[1], ffn_b_dw[1:2], w_dn1, up1, tm=256)
    dw_dn1 = _mm_tn("ffn1_dwdown", act1, dx4, tn=FFN_HALF, tt=512)
    dw_up1_t = _mm_tn("ffn1_dwup", d_up1, h3, tn=FFN_HALF, tt=512)
    ds_act = _mm("d_pw2", dx3, w_pw2, nt=True, tm=512, tn=D, out_dtype=BF16)
    dw_pw2, db_pw2 = _mm_tn("dw_pw2", s_act, dx3, tn=D, tt=512, colsum_r=True)
    scatter_a = _Scatter((6, 7))
    da, conv_acc8, buf_a = _conv_mid_bwd(ds_act, u2, a, conv_w32, ln_g, ln_b, tt=256, ex=scatter_a, ex_args=(dw_up1_t, dw_dn1))
    conv_acc = jnp.sum(conv_acc8, axis=1)
    dx2, dg_m1 = _dgrad_rms("d_pw1", da, w_pw1_t, x2, norm_mix[1:2], dx3, tm=512, tk=2 * D)
    dw_pw1_t, db_pw1 = _mm_tn("dw_pw1", da, h2, tn=D, tt=512, colsum_l=True)
    dx1, d_up0, dg_f0, dwb0 = _ffn_bwd(0, dx2, x1, norm_ffn[0:1], w_up0_t, w_fdw[0], ffn_b_dw[0:1], w_dn0, up0, tm=256)
    dw_dn0 = _mm_tn("ffn0_dwdown", act0, dx2, tn=FFN_HALF, tt=512)
    dw_up0_t = _mm_tn("ffn0_dwup", d_up0, h1, tn=FFN_HALF, tt=512)
    do = _mm("d_attn_out", dx1, w_o, nt=True, tm=512, tn=D, out_dtype=BF16)
    dw_o, db_o = _mm_tn("dw_o", o, dx1, tn=D, tt=512, colsum_r=True)
    scatter_b = _Scatter((1, 2, 3, 4, 5))
    dq, dkv, dsk, buf_b = _attn_bwd(qkv, do, sink_cols, scatter_b, (dw_o, dw_pw1_t, dw_pw2, dw_up0_t, dw_dn0))
    dqkv = jnp.concatenate([dq, dkv], axis=1)
    grad_x, dg_m0 = _dgrad_rms("d_qkv", dqkv, w_qkv_t, x0, norm_mix[0:1], dx1, tm=512, tk=NQ + KVW)
    dw_qkv_t, db_qkv = _mm_tn("dw_qkv", dqkv, h0, tn=NQ + KVW, tt=512, colsum_l=True)

    d_sinks = jnp.transpose(jnp.sum(dsk.reshape(N_KV, 2, 4, BLOCK), axis=-1), (0, 2, 1)).reshape(1, 16)
    small_grads = [jnp.concatenate([dg_m0, dg_m1], axis=0), db_qkv, d_sinks, db_o, jnp.concatenate([dg_f0, dg_f1], axis=0),
                   jnp.stack([dwb0[3], dwb1[3]]), dg_final, db_pw1, conv_acc[0:CONV_K], conv_acc[31:32], conv_acc[32:33],
                   conv_acc[33:34], db_pw2, jnp.stack([dwb0[0:3], dwb1[0:3]])]
    small_shapes = [(2, D), (1, NQ + KVW), (1, 16), (1, D), (2, D), (2, F), (D,), (1, 2 * D), (1, CONV_K, D), (1, D),
                    (1, D), (1, D), (1, D), (2, 3, F)]
    scatter_c = _Scatter((0,))
    buf_c, small_sum = _exchange_small("scatter_last", scatter_c, (dw_qkv_t,), _pack_rows(small_grads), reduce=True)
    gm = [None] * N_MAT
    for tag, sc, buf in (("a", scatter_a, buf_a), ("b", scatter_b, buf_b), ("c", scatter_c, buf_c)):
        gslab = _sum_slabs("sum_slabs_" + tag, buf)
        for i, off in zip(sc.mats, sc.offsets()):
            gm[i] = gslab[off:off + MAT_ROWS[i]]
    (g_norm_mix, g_b_qkv, g_sinks, g_b_o, g_norm_ffn, g_ffn_b_dw, g_final, g_b_pw1_full, g_w_cdw_full, g_b_cdw_full,
     g_ln_g_full, g_ln_b_full, g_b_pw2_full, g_w_fdw_full) = _unpack_rows(small_sum, small_shapes)

    def shard(a, width):
        return lax.dynamic_slice_in_dim(a, me * width, width, axis=a.ndim - 1)

    grads = {
        "norm_mix": g_norm_mix, "attn_w_qkv": gm[0].T[None], "attn_b_qkv": g_b_qkv, "attn_sinks": g_sinks,
        "attn_w_o": gm[1][None], "attn_b_o": g_b_o, "conv_w_pw1": gm[2].T[None], "conv_b_pw1": shard(g_b_pw1_full, 256),
        "conv_w_dw": shard(g_w_cdw_full, 128), "conv_b_dw": shard(g_b_cdw_full, 128), "conv_ln_g": shard(g_ln_g_full, 128),
        "conv_ln_b": shard(g_ln_b_full, 128), "conv_w_pw2": gm[3][None], "conv_b_pw2": shard(g_b_pw2_full, 128),
        "norm_ffn": g_norm_ffn, "ffn_w_up": jnp.stack([gm[4].T, gm[6].T]), "ffn_w_dw": shard(g_w_fdw_full, 352),
        "ffn_b_dw": g_ffn_b_dw, "ffn_w_down": jnp.stack([gm[5], gm[7]]), "final_norm": g_final,
    }
    weights = dict(norm_mix=norm_mix, attn_w_qkv=attn_w_qkv, attn_b_qkv=attn_b_qkv, attn_sinks=attn_sinks, attn_w_o=attn_w_o, attn_b_o=attn_b_o, conv_w_pw1=conv_w_pw1, conv_b_pw1=conv_b_pw1, conv_w_dw=conv_w_dw, conv_b_dw=conv_b_dw, conv_ln_g=conv_ln_g, conv_ln_b=conv_ln_b, conv_w_pw2=conv_w_pw2, conv_b_pw2=conv_b_pw2, norm_ffn=norm_ffn, ffn_w_up=ffn_w_up, ffn_w_dw=ffn_w_dw, ffn_b_dw=ffn_b_dw, ffn_w_down=ffn_w_down, final_norm=final_norm)
    moms = dict(norm_mix=m_norm_mix, attn_w_qkv=m_attn_w_qkv, attn_b_qkv=m_attn_b_qkv, attn_sinks=m_attn_sinks, attn_w_o=m_attn_w_o, attn_b_o=m_attn_b_o, conv_w_pw1=m_conv_w_pw1, conv_b_pw1=m_conv_b_pw1, conv_w_dw=m_conv_w_dw, conv_b_dw=m_conv_b_dw, conv_ln_g=m_conv_ln_g, conv_ln_b=m_conv_ln_b, conv_w_pw2=m_conv_w_pw2, conv_b_pw2=m_conv_b_pw2, norm_ffn=m_norm_ffn, ffn_w_up=m_ffn_w_up, ffn_w_dw=m_ffn_w_dw, ffn_b_dw=m_ffn_b_dw, ffn_w_down=m_ffn_w_down, final_norm=m_final_norm)
    vars_ = dict(norm_mix=v_norm_mix, attn_w_qkv=v_attn_w_qkv, attn_b_qkv=v_attn_b_qkv, attn_sinks=v_attn_sinks, attn_w_o=v_attn_w_o, attn_b_o=v_attn_b_o, conv_w_pw1=v_conv_w_pw1, conv_b_pw1=v_conv_b_pw1, conv_w_dw=v_conv_w_dw, conv_b_dw=v_conv_b_dw, conv_ln_g=v_conv_ln_g, conv_ln_b=v_conv_ln_b, conv_w_pw2=v_conv_w_pw2, conv_b_pw2=v_conv_b_pw2, norm_ffn=v_norm_ffn, ffn_w_up=v_ffn_w_up, ffn_w_dw=v_ffn_w_dw, ffn_b_dw=v_ffn_b_dw, ffn_w_down=v_ffn_w_down, final_norm=v_final_norm)
    names = list(weights)
    big = ("attn_w_qkv", "attn_w_o", "conv_w_pw1", "conv_w_pw2", "ffn_w_up", "ffn_w_down")
    delta, new_m, new_v = {}, {}, {}
    for nm in big:
        shp = weights[nm].shape
        two_d = (shp[0] * shp[1], shp[2])
        res = _adamw("adamw_" + nm, *(t[nm].reshape(two_d) for t in (weights, grads, moms, vars_)))
        delta[nm], new_m[nm], new_v[nm] = (r.reshape(shp) for r in res)
    small_names = [nm for nm in names if nm not in big]
    small_shapes_local = [weights[nm].shape for nm in small_names]
    res = _adamw("adamw_small", *(_pack_rows([t[nm] for nm in small_names]) for t in (weights, grads, moms, vars_)))
    for tgt_dict, slab_out in zip((delta, new_m, new_v), res):
        for nm, val in zip(small_names, _unpack_rows(slab_out, small_shapes_local)):
            tgt_dict[nm] = val
    return (loss, grad_x.reshape(1, s, D), *[grads[nm] for nm in names], *[delta[nm] for nm in names],
            *[new_m[nm] for nm in names], *[new_v[nm] for nm in names])
```

```python
import functools
import math

import jax
import jax.numpy as jnp
from jax import lax
from jax.experimental import pallas as pl
from jax.experimental.pallas import tpu as pltpu

F32 = jnp.float32
BF16 = jnp.bfloat16

D = 1024
F = 2816
HEAD_DIM = 64
N_KV = 2
BLOCK = 128
NQ = 1024
KVW = 256
CONV_K = 31
RMS_EPS = 1e-6
LN_EPS = 1e-5
ADAM_LR, ADAM_B1, ADAM_B2, ADAM_EPS, ADAM_WD, ADAM_STEP = 0.001, 0.9, 0.999, 1e-08, 0.01, 10
N_DEV = 8
MESH = pl.DeviceIdType.MESH
VMEM_LIMIT_BYTES = 56 * 2**20
NEG = float(jnp.finfo(jnp.float32).min)

MAT_ROWS = (160, 128, 256, 128, 704, 352, 704, 352)
MAT_OFF = tuple(sum(MAT_ROWS[:i]) for i in range(len(MAT_ROWS)))
SLAB_ROWS = sum(MAT_ROWS)
N_MAT = len(MAT_ROWS)

NT_DIMS = (((1,), (1,)), ((), ()))
NN_DIMS = (((1,), (0,)), ((), ()))
TN_DIMS = (((0,), (0,)), ((), ()))


def _params(*sem):
    return pltpu.CompilerParams(dimension_semantics=tuple(sem), vmem_limit_bytes=VMEM_LIMIT_BYTES)


def _row_tile(rows, cap, mult=8):
    best = None
    for t in range(mult, min(rows, cap) + 1, mult):
        if rows % t == 0:
            best = t
    return best if best is not None else rows


def _sigmoid(x):
    return 1.0 / (1.0 + jnp.exp(-x))


def _mm(name, a, b, *, nt, tm, tn, out_dtype, rms_g=None, bias=None, res=None):
    m, k = a.shape
    n = b.shape[0] if nt else b.shape[1]
    tm = min(tm, m)
    has_rms = rms_g is not None
    dims = NT_DIMS if nt else NN_DIMS

    def body(*refs):
        a_ref, b_ref = refs[0], refs[1]
        pos = 2
        g_ref = bias_ref = res_ref = h_ref = hs_ref = None
        if has_rms:
            g_ref = refs[pos]; pos += 1
        if bias is not None:
            bias_ref = refs[pos]; pos += 1
        if res is not None:
            res_ref = refs[pos]; pos += 1
        o_ref = refs[pos]; pos += 1
        if has_rms:
            h_ref, hs_ref = refs[pos], refs[pos + 1]

            @pl.when(pl.program_id(1) == 0)
            def _():
                xf = a_ref[...]
                r = lax.rsqrt(jnp.mean(xf * xf, axis=-1, keepdims=True) + RMS_EPS)
                h = ((xf * r) * g_ref[...]).astype(BF16)
                hs_ref[...] = h
                h_ref[...] = h

            lhs = hs_ref[...]
        else:
            lhs = a_ref[...].astype(BF16)
        acc = lax.dot_general(lhs, b_ref[...], dims, preferred_element_type=F32)
        if bias is not None:
            acc = acc + bias_ref[...]
        if res is not None:
            acc = acc + res_ref[...]
        o_ref[...] = acc.astype(out_dtype)

    in_specs = [pl.BlockSpec((tm, k), lambda i, j: (i, 0)),
                pl.BlockSpec((tn, k), lambda i, j: (j, 0)) if nt else pl.BlockSpec((k, tn), lambda i, j: (0, j))]
    args = [a, b]
    if has_rms:
        in_specs.append(pl.BlockSpec((1, k), lambda i, j: (0, 0))); args.append(rms_g)
    if bias is not None:
        in_specs.append(pl.BlockSpec((1, tn), lambda i, j: (0, j))); args.append(bias)
    if res is not None:
        in_specs.append(pl.BlockSpec((tm, tn), lambda i, j: (i, j))); args.append(res)
    out_shape = [jax.ShapeDtypeStruct((m, n), out_dtype)]
    out_specs = [pl.BlockSpec((tm, tn), lambda i, j: (i, j))]
    scratch = []
    if has_rms:
        out_shape.append(jax.ShapeDtypeStruct((m, k), BF16))
        out_specs.append(pl.BlockSpec((tm, k), lambda i, j: (i, 0)))
        scratch.append(pltpu.VMEM((tm, k), BF16))
    outs = pl.pallas_call(body, grid=(m // tm, n // tn), in_specs=in_specs, out_specs=out_specs, out_shape=out_shape,
                          scratch_shapes=scratch, compiler_params=_params("parallel", "arbitrary"), name=name)(*args)
    return outs if has_rms else outs[0]


def _mm_tn(name, l, r, *, tn, tt, colsum_l=False, colsum_r=False):
    s, nl = l.shape
    nr = r.shape[1]
    tt = min(tt, s)
    n_t = s // tt

    def body(*refs):
        l_ref, r_ref, o_ref = refs[0], refs[1], refs[2]
        pos = 3
        cl_ref = cr_ref = None
        if colsum_l:
            cl_ref = refs[pos]; pos += 1
        if colsum_r:
            cr_ref = refs[pos]; pos += 1
        acc_ref = refs[pos]
        j, t = pl.program_id(0), pl.program_id(1)

        @pl.when(t == 0)
        def _():
            acc_ref[...] = jnp.zeros_like(acc_ref)

        lv = l_ref[...]
        rv = r_ref[...]
        acc_ref[...] += lax.dot_general(lv, rv.astype(BF16), TN_DIMS, preferred_element_type=F32)
        if colsum_l:
            @pl.when(t == 0)
            def _():
                cl_ref[...] = jnp.zeros_like(cl_ref)

            cl_ref[...] += jnp.sum(lv.astype(F32), axis=0, keepdims=True)
        if colsum_r:
            @pl.when((t == 0) & (j == 0))
            def _():
                cr_ref[...] = jnp.zeros_like(cr_ref)

            @pl.when(j == 0)
            def _():
                cr_ref[...] += jnp.sum(rv.astype(F32), axis=0, keepdims=True)

        @pl.when(t == n_t - 1)
        def _():
            o_ref[...] = acc_ref[...].astype(BF16)

    out_shape = [jax.ShapeDtypeStruct((nl, nr), BF16)]
    out_specs = [pl.BlockSpec((tn, nr), lambda j, t: (j, 0))]
    if colsum_l:
        out_shape.append(jax.ShapeDtypeStruct((1, nl), F32))
        out_specs.append(pl.BlockSpec((1, tn), lambda j, t: (0, j)))
    if colsum_r:
        out_shape.append(jax.ShapeDtypeStruct((1, nr), F32))
        out_specs.append(pl.BlockSpec((1, nr), lambda j, t: (0, 0)))
    outs = pl.pallas_call(
        body, grid=(nl // tn, n_t),
        in_specs=[pl.BlockSpec((tt, tn), lambda j, t: (t, j)), pl.BlockSpec((tt, nr), lambda j, t: (t, 0))],
        out_specs=out_specs, out_shape=out_shape, scratch_shapes=[pltpu.VMEM((tn, nr), F32)],
        compiler_params=_params("arbitrary", "arbitrary"), name=name)(l, r)
    return outs if (colsum_l or colsum_r) else outs[0]


def _rms_bwd(dh, xf, g):
    r = lax.rsqrt(jnp.mean(xf * xf, axis=-1, keepdims=True) + RMS_EPS)
    gd = dh * g
    dx = r * gd - xf * ((r * r * r) * jnp.mean(xf * gd, axis=-1, keepdims=True))
    return dx, dh * (xf * r)


def _dgrad_rms(name, dy, w, x, g, dres, *, tm, tk):
    m, k = dy.shape
    tm = min(tm, m)
    n_k = k // tk

    def body(dy_ref, w_ref, x_ref, g_ref, dres_ref, o_ref, dg_ref, acc_ref):
        i, kk = pl.program_id(0), pl.program_id(1)

        @pl.when(kk == 0)
        def _():
            acc_ref[...] = jnp.zeros_like(acc_ref)

        acc_ref[...] += lax.dot_general(dy_ref[...], w_ref[...], NN_DIMS, preferred_element_type=F32)

        @pl.when((i == 0) & (kk == n_k - 1))
        def _():
            dg_ref[...] = jnp.zeros_like(dg_ref)

        @pl.when(kk == n_k - 1)
        def _():
            dx, dg_rows = _rms_bwd(acc_ref[...], x_ref[...], g_ref[...])
            o_ref[...] = dres_ref[...] + dx
            dg_ref[...] += jnp.sum(dg_rows, axis=0, keepdims=True)

    return pl.pallas_call(
        body, grid=(m // tm, n_k),
        in_specs=[pl.BlockSpec((tm, tk), lambda i, kk: (i, kk)), pl.BlockSpec((tk, D), lambda i, kk: (kk, 0)),
                  pl.BlockSpec((tm, D), lambda i, kk: (i, 0)), pl.BlockSpec((1, D), lambda i, kk: (0, 0)),
                  pl.BlockSpec((tm, D), lambda i, kk: (i, 0))],
        out_specs=[pl.BlockSpec((tm, D), lambda i, kk: (i, 0)), pl.BlockSpec((1, D), lambda i, kk: (0, 0))],
        out_shape=[jax.ShapeDtypeStruct((m, D), F32), jax.ShapeDtypeStruct((1, D), F32)],
        scratch_shapes=[pltpu.VMEM((tm, D), F32)],
        compiler_params=_params("arbitrary", "arbitrary"), name=name)(dy, w, x, g, dres)


def _loss_head(x4, g, tgt, *, tm):
    m = x4.shape[0]
    tm = min(tm, m)

    def body(x_ref, g_ref, t_ref, dx_ref, loss_ref, dg_ref):
        @pl.when(pl.program_id(0) == 0)
        def _():
            loss_ref[...] = jnp.zeros_like(loss_ref)
            dg_ref[...] = jnp.zeros_like(dg_ref)

        xf = x_ref[...]
        gv = g_ref[...]
        r = lax.rsqrt(jnp.mean(xf * xf, axis=-1, keepdims=True) + RMS_EPS)
        diff = (xf * r) * gv - t_ref[...]
        loss_ref[...] += 0.5 * jnp.sum(jnp.mean(diff * diff, axis=-1, keepdims=True))
        dx, dg_rows = _rms_bwd(diff * (1.0 / D), xf, gv)
        dx_ref[...] = dx
        dg_ref[...] += jnp.sum(dg_rows, axis=0, keepdims=True)

    return pl.pallas_call(
        body, grid=(m // tm,),
        in_specs=[pl.BlockSpec((tm, D), lambda i: (i, 0)), pl.BlockSpec((1, D), lambda i: (0, 0)),
                  pl.BlockSpec((tm, D), lambda i: (i, 0))],
        out_specs=[pl.BlockSpec((tm, D), lambda i: (i, 0)), pl.BlockSpec((1, 128), lambda i: (0, 0)),
                   pl.BlockSpec((1, D), lambda i: (0, 0))],
        out_shape=[jax.ShapeDtypeStruct((m, D), F32), jax.ShapeDtypeStruct((1, 128), F32),
                   jax.ShapeDtypeStruct((1, D), F32)],
        compiler_params=_params("arbitrary"), name="loss_head")(x4, g, tgt)


def _band(kvp, kvc):
    kk = jnp.concatenate([kvp[:, :128], kvc[:, :128]], axis=0)
    vv = jnp.concatenate([kvp[:, 128:], kvc[:, 128:]], axis=0)
    return kk, vv


def _blockdiag(t, h):
    lane = lax.broadcasted_iota(jnp.int32, t.shape, 1)
    z = jnp.zeros_like(t)
    tr = pltpu.roll(t, 64, 1)
    if h == 0:
        top, bot = jnp.where(lane < 64, t, z), jnp.where(lane >= 64, tr, z)
    else:
        top, bot = jnp.where(lane < 64, tr, z), jnp.where(lane >= 64, t, z)
    return jnp.concatenate([top, bot], axis=0)


def _from_blockdiag(t, h):
    lane = lax.broadcasted_iota(jnp.int32, (256, 128), 1)
    top, bot = t[:256], t[256:]
    if h == 0:
        return jnp.where(lane < 64, top + pltpu.roll(bot, 64, 1), 0.0)
    return jnp.where(lane >= 64, pltpu.roll(top, 64, 1) + bot, 0.0)


def _stack_heads(ref, h):
    return jnp.concatenate([ref[:, h * 512 + p * 128: h * 512 + (p + 1) * 128] for p in range(4)], axis=0)


def _valid_mask(n):
    row = lax.broadcasted_iota(jnp.int32, (512, 256), 0) & 127
    col = lax.broadcasted_iota(jnp.int32, (512, 256), 1)
    return (col > row) & (col <= row + BLOCK) & ((col >= BLOCK) | (n > 0))


def _softmax_half(s_half, valid, sink):
    s_half = jnp.where(valid, s_half, NEG)
    mx = jnp.maximum(jnp.max(s_half, axis=-1, keepdims=True), sink)
    p = jnp.exp(s_half - mx)
    e_sink = jnp.exp(sink - mx)
    inv = 1.0 / (jnp.sum(p, axis=-1, keepdims=True) + e_sink)
    return p * inv, e_sink * inv


def _attn_fwd(qkv, sink_cols, ex=None, ex_args=()):
    s = qkv.shape[0]
    nb = s // BLOCK
    scale = 1.0 / math.sqrt(HEAD_DIM)

    def body(q_ref, kvc_ref, kvp_ref, sk_ref, o_ref):
        n = pl.program_id(0)
        kk, vv = _band(kvp_ref[...], kvc_ref[...])
        valid = _valid_mask(n)
        for h in range(N_KV):
            kbd, vbd = _blockdiag(kk, h), _blockdiag(vv, h)
            qp = _stack_heads(q_ref, h)
            sc = lax.dot_general(qp, kbd, NT_DIMS, preferred_element_type=F32) * scale
            probs = []
            for half in range(2):
                sink = sk_ref[h * 2 + half][:, :1]
                pr, _ = _softmax_half(sc[:, half * 256:(half + 1) * 256], valid, sink)
                probs.append(pr.astype(BF16))
            o = lax.dot_general(jnp.concatenate(probs, axis=1), vbd, NN_DIMS, preferred_element_type=F32)
            for p in range(4):
                o_ref[:, h * 512 + p * 128: h * 512 + (p + 1) * 128] = o[p * 128:(p + 1) * 128].astype(BF16)

    return pl.pallas_call(
        _carried(body, 4, 1, ex, nb), grid=(nb,),
        in_specs=[pl.BlockSpec((BLOCK, NQ), lambda n: (n, 0)),
                  pl.BlockSpec((BLOCK, KVW), lambda n: (n, NQ // KVW)),
                  pl.BlockSpec((BLOCK, KVW), lambda n: (jnp.maximum(n - 1, 0), NQ // KVW)),
                  pl.BlockSpec((4, 512, 128), lambda n: (0, 0, 0))] + _ex(ex, "in_specs"),
        out_specs=[pl.BlockSpec((BLOCK, NQ), lambda n: (n, 0))] + _ex(ex, "out_specs"),
        out_shape=[jax.ShapeDtypeStruct((s, NQ), BF16)] + _ex(ex, "out_shape"), scratch_shapes=_ex(ex, "scratch"),
        compiler_params=_params("arbitrary"), name="attn_fwd")(qkv, qkv, qkv, sink_cols, *ex_args)


def _attn_bwd(qkv, do, sink_cols, ex=None, ex_args=()):
    s = qkv.shape[0]
    nb = s // BLOCK
    scale = 1.0 / math.sqrt(HEAD_DIM)

    def body(q_ref, kvc_ref, kvp_ref, do_ref, sk_ref, dq_ref, dkv_ref, dsk_ref, ck_ref, cv_ref):
        n = pl.program_id(0)

        @pl.when(n == 0)
        def _():
            dsk_ref[...] = jnp.zeros_like(dsk_ref)
            ck_ref[...] = jnp.zeros_like(ck_ref)
            cv_ref[...] = jnp.zeros_like(cv_ref)

        @pl.when(n < nb)
        def _():
            kk, vv = _band(kvp_ref[...], kvc_ref[...])
            valid = _valid_mask(n)
            dkk = jnp.zeros((256, 128), F32)
            dvv = jnp.zeros((256, 128), F32)
            for h in range(N_KV):
                kbd, vbd = _blockdiag(kk, h), _blockdiag(vv, h)
                qp = _stack_heads(q_ref, h)
                dop = _stack_heads(do_ref, h)
                sc = lax.dot_general(qp, kbd, NT_DIMS, preferred_element_type=F32) * scale
                dp = lax.dot_general(dop, vbd, NT_DIMS, preferred_element_type=F32)
                probs, dss = [], []
                for half in range(2):
                    sink = sk_ref[h * 2 + half][:, :1]
                    pr, p_sink = _softmax_half(sc[:, half * 256:(half + 1) * 256], valid, sink)
                    dph = dp[:, half * 256:(half + 1) * 256]
                    delta = jnp.sum(pr * dph, axis=-1, keepdims=True)
                    dss.append((pr * (dph - delta) * scale).astype(BF16))
                    probs.append(pr.astype(BF16))
                    dsk_ref[h * 2 + half] += -(p_sink * delta)
                ds = jnp.concatenate(dss, axis=1)
                pb = jnp.concatenate(probs, axis=1)
                dqp = lax.dot_general(ds, kbd, NN_DIMS, preferred_element_type=F32)
                for p in range(4):
                    dq_ref[:, h * 512 + p * 128: h * 512 + (p + 1) * 128] = dqp[p * 128:(p + 1) * 128].astype(BF16)
                dkk = dkk + _from_blockdiag(lax.dot_general(ds, qp, TN_DIMS, preferred_element_type=F32), h)
                dvv = dvv + _from_blockdiag(lax.dot_general(pb, dop, TN_DIMS, preferred_element_type=F32), h)

            @pl.when(n >= 1)
            def _():
                dkv_ref[:, :128] = (ck_ref[...] + dkk[:128]).astype(BF16)
                dkv_ref[:, 128:] = (cv_ref[...] + dvv[:128]).astype(BF16)

            ck_ref[...] = dkk[128:]
            cv_ref[...] = dvv[128:]

        @pl.when(n == nb)
        def _():
            dkv_ref[:, :128] = ck_ref[...].astype(BF16)
            dkv_ref[:, 128:] = cv_ref[...].astype(BF16)

    last = nb - 1
    return pl.pallas_call(
        _carried(body, 5, 3, ex, nb + 1), grid=(nb + 1,),
        in_specs=[pl.BlockSpec((BLOCK, NQ), lambda n: (jnp.minimum(n, last), 0)),
                  pl.BlockSpec((BLOCK, KVW), lambda n: (jnp.minimum(n, last), NQ // KVW)),
                  pl.BlockSpec((BLOCK, KVW), lambda n: (jnp.clip(n - 1, 0, last), NQ // KVW)),
                  pl.BlockSpec((BLOCK, NQ), lambda n: (jnp.minimum(n, last), 0)),
                  pl.BlockSpec((4, 512, 128), lambda n: (0, 0, 0))] + _ex(ex, "in_specs"),
        out_specs=[pl.BlockSpec((BLOCK, NQ), lambda n: (jnp.minimum(n, last), 0)),
                   pl.BlockSpec((BLOCK, KVW), lambda n: (jnp.maximum(n - 1, 0), 0)),
                   pl.BlockSpec((4, 512, 1), lambda n: (0, 0, 0))] + _ex(ex, "out_specs"),
        out_shape=[jax.ShapeDtypeStruct((s, NQ), BF16), jax.ShapeDtypeStruct((s, KVW), BF16),
                   jax.ShapeDtypeStruct((4, 512, 1), F32)] + _ex(ex, "out_shape"),
        scratch_shapes=[pltpu.VMEM((BLOCK, 128), F32), pltpu.VMEM((BLOCK, 128), F32)] + _ex(ex, "scratch"),
        compiler_params=_params("arbitrary"), name="attn_bwd")(qkv, qkv, qkv, do, sink_cols, *ex_args)


LANE = 128
FFN_HALF = F // 2
FFN_PAIR = 2 * LANE
FFN_PAIRS = F // FFN_PAIR
FFN_GROUPS = ((0, 3), (3, 6), (6, 9), (9, 11))
FFN_HALO = 8


def _resident(shape):
    return pl.BlockSpec(shape, lambda i: (0,) * len(shape), pipeline_mode=pl.Buffered(1))


def _ffn_fwd(tag, x_in, g, w_up_t, w_dw, b_dw, w_down, *, tm, ex=None, ex_args=()):
    s = x_in.shape[0]
    tm = min(tm, s)

    def body(x_ref, g_ref, wup_ref, wdw_ref, bdw_ref, wd_ref, o_ref, h_ref, up_ref, act_ref, gate_ref, carry_ref, ext_ref):
        @pl.when(pl.program_id(0) == 0)
        def _():
            carry_ref[...] = jnp.zeros_like(carry_ref)

        xf = x_ref[...]
        r = lax.rsqrt(jnp.mean(xf * xf, axis=-1, keepdims=True) + RMS_EPS)
        h = ((xf * r) * g_ref[...]).astype(BF16)
        h_ref[...] = h

        def up_mm(p):
            for off in (p * FFN_PAIR, F + p * FFN_PAIR):
                rs = slice(off, off + FFN_PAIR)
                up_ref[:, rs] = lax.dot_general(h, wup_ref[rs, :], NT_DIMS, preferred_element_type=F32).astype(BF16)

        def elementwise(p):
            for c in (2 * p, 2 * p + 1):
                cs = slice(c * LANE, (c + 1) * LANE)
                vs = slice(F + c * LANE, F + (c + 1) * LANE)
                gcol = up_ref[:, cs].astype(F32)
                ext_ref[pl.ds(0, FFN_HALO), :] = carry_ref[:, cs]
                ext_ref[pl.ds(FFN_HALO, tm), :] = gcol
                gate_b = (wdw_ref[2:3, cs] * gcol + wdw_ref[1:2, cs] * ext_ref[pl.ds(FFN_HALO - 1, tm), :]
                          + wdw_ref[0:1, cs] * ext_ref[pl.ds(FFN_HALO - 2, tm), :] + bdw_ref[:, cs]).astype(BF16)
                gate_ref[:, cs] = gate_b
                gate = gate_b.astype(F32)
                act_ref[:, cs] = (gate * _sigmoid(gate) * up_ref[:, vs].astype(F32)).astype(BF16)
                carry_ref[:, cs] = gcol[tm - FFN_HALO:]

        def down_mm(lo, hi):
            ks = slice(lo * FFN_PAIR, hi * FFN_PAIR)
            return lax.dot_general(act_ref[:, ks], wd_ref[ks, :], NN_DIMS, preferred_element_type=F32)

        acc = xf
        up_mm(0)
        pending = None
        for lo, hi in FFN_GROUPS:
            for p in range(lo, hi):
                if p + 1 < FFN_PAIRS:
                    up_mm(p + 1)
                if pending is not None:
                    acc = acc + down_mm(*pending)
                    pending = None
                elementwise(p)
            pending = (lo, hi)
        o_ref[...] = acc + down_mm(*pending)

    row = lambda i: (i, 0)
    return pl.pallas_call(
        _carried(body, 6, 5, ex, s // tm), grid=(s // tm,),
        in_specs=[pl.BlockSpec((tm, D), row), _resident((1, D)), _resident((2 * F, D)), _resident((3, F)),
                  _resident((1, F)), _resident((F, D))] + _ex(ex, "in_specs"),
        out_specs=[pl.BlockSpec((tm, D), row), pl.BlockSpec((tm, D), row), pl.BlockSpec((tm, 2 * F), row),
                   pl.BlockSpec((tm, F), row), pl.BlockSpec((tm, F), row)] + _ex(ex, "out_specs"),
        out_shape=[jax.ShapeDtypeStruct((s, D), F32), jax.ShapeDtypeStruct((s, D), BF16),
                   jax.ShapeDtypeStruct((s, 2 * F), BF16), jax.ShapeDtypeStruct((s, F), BF16),
                   jax.ShapeDtypeStruct((s, F), BF16)] + _ex(ex, "out_shape"),
        scratch_shapes=[pltpu.VMEM((FFN_HALO, F), F32), pltpu.VMEM((tm + FFN_HALO, LANE), F32)] + _ex(ex, "scratch"),
        compiler_params=_params("arbitrary"), name=f"ffn{tag}_fwd")(x_in, g, w_up_t, w_dw, b_dw, w_down, *ex_args)


def _ffn_bwd(tag, dx_out, x_in, g, w_up_t, w_dw, w_down, up, gate, *, tm):
    s = x_in.shape[0]
    tm = min(tm, s)
    n_i = s // tm

    def body(dx_ref, x_ref, g_ref, up_ref, gate_ref, wup_ref, wdw_ref, wd_ref,
             dxin_ref, dup_ref, dg_ref, dwb_ref, carry_ref, da0_ref, da1_ref, extd_ref):
        i = pl.program_id(0)

        @pl.when(i == 0)
        def _():
            carry_ref[...] = jnp.zeros_like(carry_ref)
            dg_ref[...] = jnp.zeros_like(dg_ref)
            dwb_ref[...] = jnp.zeros_like(dwb_ref)

        dxf = dx_ref[...]
        dxb = dxf.astype(BF16)
        da_refs = (da0_ref, da1_ref)

        def da_mm(p):
            ks = slice(p * FFN_PAIR, (p + 1) * FFN_PAIR)
            da_refs[p % 2][...] = lax.dot_general(dxb, wd_ref[ks, :], NT_DIMS, preferred_element_type=F32)

        def elementwise(p):
            da_ref = da_refs[p % 2]
            for c in range(2):
                cc = 2 * p + c
                cs = slice(cc * LANE, (cc + 1) * LANE)
                vs = slice(F + cc * LANE, F + (cc + 1) * LANE)
                gate = gate_ref[:, cs].astype(F32)
                sg = _sigmoid(gate)
                silu = gate * sg
                dac = da_ref[:, c * LANE:(c + 1) * LANE]
                dup_ref[:, vs] = (dac * silu).astype(BF16)
                dgate = dac * up_ref[:, vs].astype(F32) * (sg + silu * (1.0 - sg))
                extd_ref[pl.ds(0, tm), :] = dgate
                extd_ref[pl.ds(tm, FFN_HALO), :] = carry_ref[:, cs]
                d1 = extd_ref[pl.ds(1, tm), :]
                d2 = extd_ref[pl.ds(2, tm), :]
                dup_ref[:, cs] = (wdw_ref[2:3, cs] * dgate + wdw_ref[1:2, cs] * d1 + wdw_ref[0:1, cs] * d2).astype(BF16)
                carry_ref[:, cs] = dgate[:FFN_HALO]
                gcol = up_ref[:, cs].astype(F32)
                dwb_ref[2, :, cs] += _sum8(dgate * gcol)
                dwb_ref[1, :, cs] += _sum8(d1 * gcol)
                dwb_ref[0, :, cs] += _sum8(d2 * gcol)
                dwb_ref[3, :, cs] += _sum8(dgate)

        def dh_mm(lo, hi):
            ks = slice(lo * FFN_PAIR, hi * FFN_PAIR)
            vks = slice(F + lo * FFN_PAIR, F + hi * FFN_PAIR)
            return (lax.dot_general(dup_ref[:, ks], wup_ref[ks, :], NN_DIMS, preferred_element_type=F32)
                    + lax.dot_general(dup_ref[:, vks], wup_ref[vks, :], NN_DIMS, preferred_element_type=F32))

        acc = None
        da_mm(0)
        pending = None
        for lo, hi in FFN_GROUPS:
            for p in range(lo, hi):
                if p + 1 < FFN_PAIRS:
                    da_mm(p + 1)
                if pending is not None:
                    part = dh_mm(*pending)
                    acc = part if acc is None else acc + part
                    pending = None
                elementwise(p)
            pending = (lo, hi)
        acc = acc + dh_mm(*pending)
        dx, dg_rows = _rms_bwd(acc, x_ref[...], g_ref[...])
        dxin_ref[...] = dxf + dx
        dg_ref[...] += jnp.sum(dg_rows, axis=0, keepdims=True)

    rev = lambda i: (n_i - 1 - i, 0)
    return pl.pallas_call(
        body, grid=(n_i,),
        in_specs=[pl.BlockSpec((tm, D), rev), pl.BlockSpec((tm, D), rev), _resident((1, D)),
                  pl.BlockSpec((tm, 2 * F), rev), pl.BlockSpec((tm, F), rev),
                  _resident((2 * F, D)), _resident((3, F)), _resident((F, D))],
        out_specs=[pl.BlockSpec((tm, D), rev), pl.BlockSpec((tm, 2 * F), rev),
                   pl.BlockSpec((1, D), lambda i: (0, 0)), pl.BlockSpec((4, 8, F), lambda i: (0, 0, 0))],
        out_shape=[jax.ShapeDtypeStruct((s, D), F32), jax.ShapeDtypeStruct((s, 2 * F), BF16),
                   jax.ShapeDtypeStruct((1, D), F32), jax.ShapeDtypeStruct((4, 8, F), F32)],
        scratch_shapes=[pltpu.VMEM((FFN_HALO, F), F32), pltpu.VMEM((tm, FFN_PAIR), F32), pltpu.VMEM((tm, FFN_PAIR), F32),
                        pltpu.VMEM((tm + FFN_HALO, LANE), F32)],
        compiler_params=_params("arbitrary"), name=f"ffn{tag}_bwd")(dx_out, x_in, g, up, gate, w_up_t, w_dw, w_down)


CONV_HALO = 32
CONV_CHUNK = 256
CONV_ROWS = 64


def _ln(u2, g, b):
    mu = jnp.mean(u2, axis=-1, keepdims=True)
    xc = u2 - mu
    rstd = lax.rsqrt(jnp.mean(xc * xc, axis=-1, keepdims=True) + LN_EPS)
    xhat = xc * rstd
    return xhat, rstd, xhat * g + b


def _phase_taps(ph):
    first = CONV_HALO - (CONV_K - 1)
    return [(k, first + k - ph) for k in range(CONV_K) if (first + k) % 8 == ph]


def _sum8(v):
    out = v[0:8]
    for j in range(8, v.shape[0], 8):
        out = out + v[j:j + 8]
    return out


def _store_phases(src_ref, sh_ref, tt, cols=slice(None)):
    sh_ref[0] = src_ref[:, cols]
    for ph in range(1, 8):
        sh_ref[ph, pl.ds(0, tt + CONV_HALO - 8), :] = src_ref[pl.ds(ph, tt + CONV_HALO - 8), cols]


def _conv_mid_fwd(a, wdw, ln_g, ln_b, *, tt):
    s = a.shape[0]
    tt = min(tt, s)
    hb = tt // CONV_HALO

    def body(a_ref, prev_ref, w_ref, g_ref, b_ref, s_ref, u2_ref, ext_ref, sh_ref):
        i = pl.program_id(0)
        for c in range(D // CONV_CHUNK):
            cs = slice(c * CONV_CHUNK, (c + 1) * CONV_CHUNK)
            gs = slice(D + c * CONV_CHUNK, D + (c + 1) * CONV_CHUNK)
            uh = prev_ref[:, cs].astype(F32) * _sigmoid(prev_ref[:, gs].astype(F32))
            ext_ref[pl.ds(0, CONV_HALO), :] = jnp.where(i > 0, uh, 0.0)
            ext_ref[pl.ds(CONV_HALO, tt), :] = a_ref[:, cs].astype(F32) * _sigmoid(a_ref[:, gs].astype(F32))
            _store_phases(ext_ref, sh_ref, tt)
            for base in range(0, tt, CONV_ROWS):
                acc = jnp.zeros((CONV_ROWS, CONV_CHUNK), F32) + w_ref[CONV_K:CONV_K + 1, cs]
                for ph in range(8):
                    for k, off in _phase_taps(ph):
                        acc = acc + w_ref[k:k + 1, cs] * sh_ref[ph, pl.ds(base + off, CONV_ROWS), :]
                u2_ref[pl.ds(base, CONV_ROWS), cs] = acc.astype(BF16)
        _, _, ln = _ln(u2_ref[...].astype(F32), g_ref[...], b_ref[...])
        s_ref[...] = (ln * _sigmoid(ln)).astype(BF16)

    return pl.pallas_call(
        body, grid=(s // tt,),
        in_specs=[pl.BlockSpec((tt, 2 * D), lambda i: (i, 0)),
                  pl.BlockSpec((CONV_HALO, 2 * D), lambda i: (jnp.maximum(i * hb - 1, 0), 0)),
                  pl.BlockSpec((32, D), lambda i: (0, 0)), pl.BlockSpec((1, D), lambda i: (0, 0)),
                  pl.BlockSpec((1, D), lambda i: (0, 0))],
        out_specs=[pl.BlockSpec((tt, D), lambda i: (i, 0)), pl.BlockSpec((tt, D), lambda i: (i, 0))],
        out_shape=[jax.ShapeDtypeStruct((s, D), BF16), jax.ShapeDtypeStruct((s, D), BF16)],
        scratch_shapes=[pltpu.VMEM((tt + CONV_HALO, CONV_CHUNK), F32), pltpu.VMEM((8, tt + CONV_HALO, CONV_CHUNK), F32)],
        compiler_params=_params("parallel"), name="conv_mid_fwd")(a, a, wdw, ln_g, ln_b)


def _conv_mid_bwd(ds, u2, a, wdw, ln_g, ln_b, *, tt, ex=None, ex_args=()):
    s = a.shape[0]
    tt = min(tt, s)
    hb = tt // CONV_HALO
    n_i = s // tt
    te = tt + CONV_HALO

    def body(ds_ref, dsn_ref, u2_ref, u2n_ref, a_ref, prev_ref, w_ref, g_ref, b_ref, da_ref, acc_ref, e2_ref, ext_ref,
             ush_ref, dsh_ref, du_ref):
        i = pl.program_id(0)

        @pl.when(i == 0)
        def _():
            acc_ref[...] = jnp.zeros_like(acc_ref)

        u2e = jnp.concatenate([u2_ref[...], u2n_ref[...]], axis=0).astype(F32)
        dse = jnp.concatenate([ds_ref[...], dsn_ref[...]], axis=0).astype(F32)
        gv = g_ref[...]
        xhat, rstd, ln = _ln(u2e, gv, b_ref[...])
        sg = _sigmoid(ln)
        dln = dse * (sg * (1.0 + ln * (1.0 - sg)))
        acc_ref[32] += _sum8(dln[:tt] * xhat[:tt])
        acc_ref[33] += _sum8(dln[:tt])
        dxh = dln * gv
        du2 = rstd * (dxh - jnp.mean(dxh, axis=-1, keepdims=True) - xhat * jnp.mean(dxh * xhat, axis=-1, keepdims=True))
        row = lax.broadcasted_iota(jnp.int32, (te, D), 0)
        e2_ref[...] = jnp.where((row < tt) | (i < n_i - 1), du2, 0.0)
        for c in range(D // CONV_CHUNK):
            cs = slice(c * CONV_CHUNK, (c + 1) * CONV_CHUNK)
            gs = slice(D + c * CONV_CHUNK, D + (c + 1) * CONV_CHUNK)
            uh = prev_ref[:, cs].astype(F32) * _sigmoid(prev_ref[:, gs].astype(F32))
            ext_ref[pl.ds(0, CONV_HALO), :] = jnp.where(i > 0, uh, 0.0)
            a1 = a_ref[:, cs].astype(F32)
            sg2 = _sigmoid(a_ref[:, gs].astype(F32))
            ext_ref[pl.ds(CONV_HALO, tt), :] = a1 * sg2
            _store_phases(ext_ref, ush_ref, tt)
            _store_phases(e2_ref, dsh_ref, tt, cs)
            for base in range(0, tt, CONV_ROWS):
                d0 = e2_ref[pl.ds(base, CONV_ROWS), cs]
                du = jnp.zeros((CONV_ROWS, CONV_CHUNK), F32)
                for ph in range(8):
                    for k, off in _phase_taps(ph):
                        acc_ref[k, :, cs] += _sum8(d0 * ush_ref[ph, pl.ds(base + off, CONV_ROWS), :])
                    for k in range(CONV_K):
                        if (CONV_K - 1 - k) % 8 == ph:
                            du = du + w_ref[k:k + 1, cs] * dsh_ref[ph, pl.ds(base + CONV_K - 1 - k - ph, CONV_ROWS), :]
                acc_ref[CONV_K, :, cs] += _sum8(d0)
                du_ref[pl.ds(base, CONV_ROWS), :] = du
            du = du_ref[...]
            da_ref[:, cs] = (du * sg2).astype(BF16)
            da_ref[:, gs] = (du * a1 * (sg2 * (1.0 - sg2))).astype(BF16)

    last_h = s // CONV_HALO - 1
    nxt = lambda i: (jnp.minimum((i + 1) * hb, last_h), 0)
    return pl.pallas_call(
        _carried(body, 9, 2, ex, n_i), grid=(n_i,),
        in_specs=[pl.BlockSpec((tt, D), lambda i: (i, 0)), pl.BlockSpec((CONV_HALO, D), nxt),
                  pl.BlockSpec((tt, D), lambda i: (i, 0)), pl.BlockSpec((CONV_HALO, D), nxt),
                  pl.BlockSpec((tt, 2 * D), lambda i: (i, 0)),
                  pl.BlockSpec((CONV_HALO, 2 * D), lambda i: (jnp.maximum(i * hb - 1, 0), 0)),
                  pl.BlockSpec((32, D), lambda i: (0, 0)), pl.BlockSpec((1, D), lambda i: (0, 0)),
                  pl.BlockSpec((1, D), lambda i: (0, 0))] + _ex(ex, "in_specs"),
        out_specs=[pl.BlockSpec((tt, 2 * D), lambda i: (i, 0)), pl.BlockSpec((40, 8, D), lambda i: (0, 0, 0))] + _ex(ex, "out_specs"),
        out_shape=[jax.ShapeDtypeStruct((s, 2 * D), BF16), jax.ShapeDtypeStruct((40, 8, D), F32)] + _ex(ex, "out_shape"),
        scratch_shapes=[pltpu.VMEM((te, D), F32), pltpu.VMEM((te, CONV_CHUNK), F32),
                        pltpu.VMEM((8, te, CONV_CHUNK), F32), pltpu.VMEM((8, te, CONV_CHUNK), F32),
                        pltpu.VMEM((tt, CONV_CHUNK), F32)] + _ex(ex, "scratch"),
        compiler_params=_params("arbitrary"), name="conv_mid_bwd")(ds, ds, u2, u2, a, a, wdw, ln_g, ln_b, *ex_args)


def _adamw(name, w, g, m, v):
    rows, cols = w.shape
    tr = _row_tile(rows, 256)

    def body(w_ref, g_ref, m_ref, v_ref, d_ref, nm_ref, nv_ref):
        gv = g_ref[...]
        m2 = ADAM_B1 * m_ref[...] + (1.0 - ADAM_B1) * gv
        v2 = ADAM_B2 * v_ref[...] + (1.0 - ADAM_B2) * (gv * gv)
        m_hat = m2 / (1.0 - ADAM_B1 ** ADAM_STEP)
        v_hat = v2 / (1.0 - ADAM_B2 ** ADAM_STEP)
        d_ref[...] = -ADAM_LR * (m_hat / (jnp.sqrt(v_hat) + ADAM_EPS) + ADAM_WD * w_ref[...])
        nm_ref[...] = m2
        nv_ref[...] = v2

    spec = pl.BlockSpec((tr, cols), lambda i: (i, 0))
    return pl.pallas_call(
        body, grid=(rows // tr,), in_specs=[spec] * 4, out_specs=[spec] * 3,
        out_shape=[jax.ShapeDtypeStruct((rows, cols), F32)] * 3,
        compiler_params=_params("parallel"), name=name)(w, g, m, v)


def _sum_slabs(name, buf):
    rows = buf.shape[1]
    tr = _row_tile(rows, 512, 16)

    def body(b_ref, o_ref):
        acc = b_ref[0].astype(F32)
        for k in range(1, N_DEV):
            acc = acc + b_ref[k].astype(F32)
        o_ref[...] = acc

    return pl.pallas_call(
        body, grid=(rows // tr,), in_specs=[pl.BlockSpec((N_DEV, tr, D), lambda i: (0, i, 0))],
        out_specs=pl.BlockSpec((tr, D), lambda i: (i, 0)), out_shape=jax.ShapeDtypeStruct((rows, D), F32),
        compiler_params=_params("parallel"), name=name)(buf)


def _ex(ex, field):
    return list(getattr(ex, field)) if ex is not None else []


def _me():
    x, y, c = lax.axis_index("x"), lax.axis_index("y"), lax.axis_index("c")
    return x, y, c, 4 * x + 2 * y + c


def _peer(x, y, c, k):
    return (x ^ (k >> 2), y ^ ((k >> 1) & 1), c ^ (k & 1))


class _Gather:
    def __init__(self, mats):
        self.mats = tuple(mats)
        self.rows = sum(MAT_ROWS[i] for i in self.mats)
        any_spec = pl.BlockSpec(memory_space=pl.ANY)
        self.n_in, self.n_out = 1, len(self.mats)
        self.in_specs = [any_spec]
        self.out_specs = [any_spec] * self.n_out
        self.out_shape = [jax.ShapeDtypeStruct((N_DEV * MAT_ROWS[i], D), BF16) for i in self.mats]
        self.scratch = [pltpu.SemaphoreType.DMA((N_DEV - 1,)), pltpu.SemaphoreType.DMA((N_DEV - 1,)),
                        pltpu.SemaphoreType.DMA]
        assert max(s.shape[0] for s in self.out_shape) >= self.rows

    def start(self, ins, outs, scr):
        (slab_ref,), (send_sems, recv_sems, local_sem) = ins, scr
        x, y, c, me = _me()
        for n, i in enumerate(self.mats):
            src = slab_ref.at[pl.ds(MAT_OFF[i], MAT_ROWS[i])]
            dst = outs[n].at[pl.ds(me * MAT_ROWS[i], MAT_ROWS[i])]
            pltpu.make_async_copy(src, dst, local_sem).start()
            for k in range(1, N_DEV):
                pltpu.make_async_remote_copy(src_ref=src, dst_ref=dst, send_sem=send_sems.at[k - 1],
                                             recv_sem=recv_sems.at[k - 1], device_id=_peer(x, y, c, k),
                                             device_id_type=MESH).start()

    def wait(self, ins, outs, scr):
        send_sems, recv_sems, local_sem = scr
        x, y, c, _ = _me()
        big = max(range(self.n_out), key=lambda n: self.out_shape[n].shape[0])
        whole = outs[big].at[pl.ds(0, self.rows)]
        for k in range(1, N_DEV):
            pltpu.make_async_remote_copy(src_ref=whole, dst_ref=whole, send_sem=send_sems.at[k - 1],
                                         recv_sem=recv_sems.at[k - 1], device_id=_peer(x, y, c, k),
                                         device_id_type=MESH).wait()
        pltpu.make_async_copy(whole, whole, local_sem).wait()


class _Scatter:
    def __init__(self, mats):
        self.mats = tuple(mats)
        self.rows = sum(MAT_ROWS[i] for i in self.mats)
        any_spec = pl.BlockSpec(memory_space=pl.ANY)
        self.n_in, self.n_out = len(self.mats), 1
        self.in_specs = [any_spec] * self.n_in
        self.out_specs = [any_spec]
        self.out_shape = [jax.ShapeDtypeStruct((N_DEV, self.rows, D), BF16)]
        self.scratch = [pltpu.SemaphoreType.DMA((N_DEV - 1,)), pltpu.SemaphoreType.DMA((N_DEV - 1,)),
                        pltpu.SemaphoreType.DMA]

    def offsets(self):
        return [sum(MAT_ROWS[i] for i in self.mats[:n]) for n in range(len(self.mats))]

    def start(self, ins, outs, scr):
        (buf_ref,), (send_sems, recv_sems, local_sem) = outs, scr
        x, y, c, me = _me()
        for n, (i, off) in enumerate(zip(self.mats, self.offsets())):
            r = MAT_ROWS[i]
            pltpu.make_async_copy(ins[n].at[pl.ds(me * r, r)], buf_ref.at[0, pl.ds(off, r)], local_sem).start()
            for k in range(1, N_DEV):
                pltpu.make_async_remote_copy(src_ref=ins[n].at[pl.ds((me ^ k) * r, r)], dst_ref=buf_ref.at[k, pl.ds(off, r)],
                                             send_sem=send_sems.at[k - 1], recv_sem=recv_sems.at[k - 1],
                                             device_id=_peer(x, y, c, k), device_id_type=MESH).start()

    def wait(self, ins, outs, scr):
        (buf_ref,), (send_sems, recv_sems, local_sem) = outs, scr
        x, y, c, _ = _me()
        whole = buf_ref.at[0]
        for k in range(1, N_DEV):
            pltpu.make_async_remote_copy(src_ref=whole, dst_ref=whole, send_sem=send_sems.at[k - 1],
                                         recv_sem=recv_sems.at[k - 1], device_id=_peer(x, y, c, k),
                                         device_id_type=MESH).wait()
        pltpu.make_async_copy(whole, whole, local_sem).wait()


def _carried(body, n_in, n_out, ex, n_steps):
    if ex is None:
        return body

    def wrapped(*refs):
        ins, ex_ins = refs[:n_in], refs[n_in:n_in + ex.n_in]
        p = n_in + ex.n_in
        outs, ex_outs = refs[p:p + n_out], refs[p + n_out:p + n_out + ex.n_out]
        p += n_out + ex.n_out
        n_scr = len(refs) - p - len(ex.scratch)
        scr, ex_scr = refs[p:p + n_scr], refs[p + n_scr:]

        @pl.when(pl.program_id(0) == 0)
        def _():
            ex.start(ex_ins, ex_outs, ex_scr)

        body(*ins, *outs, *scr)

        @pl.when(pl.program_id(0) == n_steps - 1)
        def _():
            ex.wait(ex_ins, ex_outs, ex_scr)

    return wrapped


def _exchange_small(name, ex, ex_args, small, reduce):
    vmem_spec = pl.BlockSpec(memory_space=pltpu.VMEM)
    n_small = small.shape[0]

    def body(*refs):
        ex_ins, small_ref = refs[:ex.n_in], refs[ex.n_in]
        p = ex.n_in + 1
        ex_outs, res_ref = refs[p:p + ex.n_out], refs[p + ex.n_out]
        p += ex.n_out + 1
        if reduce:
            all_ref, p = refs[p], p + 1
        else:
            all_ref = res_ref
        sm_send, sm_recv = refs[p], refs[p + 1]
        ex_scr = refs[p + 2:]
        x, y, c, me = _me()
        ex.start(ex_ins, ex_outs, ex_scr)
        copies = [pltpu.make_async_remote_copy(
            src_ref=small_ref, dst_ref=all_ref.at[me], send_sem=sm_send.at[k - 1], recv_sem=sm_recv.at[k - 1],
            device_id=_peer(x, y, c, k), device_id_type=MESH) for k in range(1, N_DEV)]
        for cp in copies:
            cp.start()
        all_ref[me] = small_ref[...]
        for cp in copies:
            cp.wait()
        if reduce:
            acc = all_ref[0]
            for d in range(1, N_DEV):
                acc = acc + all_ref[d]
            res_ref[...] = acc
        ex.wait(ex_ins, ex_outs, ex_scr)

    res_shape = (n_small, D) if reduce else (N_DEV, n_small, D)
    scratch = ([pltpu.VMEM((N_DEV, n_small, D), F32)] if reduce else []) + [
        pltpu.SemaphoreType.DMA((N_DEV - 1,)), pltpu.SemaphoreType.DMA((N_DEV - 1,))] + ex.scratch
    outs = pl.pallas_call(
        body, in_specs=ex.in_specs + [vmem_spec], out_specs=ex.out_specs + [vmem_spec],
        out_shape=ex.out_shape + [jax.ShapeDtypeStruct(res_shape, F32)], scratch_shapes=scratch,
        compiler_params=pltpu.CompilerParams(vmem_limit_bytes=VMEM_LIMIT_BYTES), name=name)(*ex_args, small)
    return outs


def _pack_rows(arrs):
    rows = []
    for a in arrs:
        flat = a.reshape(-1).astype(F32)
        pad = (-flat.shape[0]) % D
        rows.append(jnp.pad(flat, (0, pad)).reshape(-1, D))
    slab = jnp.concatenate(rows, axis=0)
    return jnp.pad(slab, ((0, (-slab.shape[0]) % 8), (0, 0)))


def _unpack_rows(slab, shapes):
    out, r = [], 0
    for shp in shapes:
        size = math.prod(shp)
        nrows = -(-size // D)
        out.append(slab[r:r + nrows].reshape(-1)[:size].reshape(shp))
        r += nrows
    return out


def kernel(x, norm_mix, attn_w_qkv, attn_b_qkv, attn_sinks, attn_w_o, attn_b_o, conv_w_pw1, conv_b_pw1, conv_w_dw, conv_b_dw, conv_ln_g, conv_ln_b, conv_w_pw2, conv_b_pw2, norm_ffn, ffn_w_up, ffn_w_dw, ffn_b_dw, ffn_w_down, final_norm, loss_target, m_norm_mix, m_attn_w_qkv, m_attn_b_qkv, m_attn_sinks, m_attn_w_o, m_attn_b_o, m_conv_w_pw1, m_conv_b_pw1, m_conv_w_dw, m_conv_b_dw, m_conv_ln_g, m_conv_ln_b, m_conv_w_pw2, m_conv_b_pw2, m_norm_ffn, m_ffn_w_up, m_ffn_w_dw, m_ffn_b_dw, m_ffn_w_down, m_final_norm, v_norm_mix, v_attn_w_qkv, v_attn_b_qkv, v_attn_sinks, v_attn_w_o, v_attn_b_o, v_conv_w_pw1, v_conv_b_pw1, v_conv_w_dw, v_conv_b_dw, v_conv_ln_g, v_conv_ln_b, v_conv_w_pw2, v_conv_b_pw2, v_norm_ffn, v_ffn_w_up, v_ffn_w_dw, v_ffn_b_dw, v_ffn_w_down, v_final_norm):
    s = x.shape[1]
    me = 4 * lax.axis_index("x") + 2 * lax.axis_index("y") + lax.axis_index("c")
    x0 = x.reshape(s, D)
    tgt = loss_target.reshape(s, D)

    slab = jnp.concatenate([attn_w_qkv[0].T, attn_w_o[0], conv_w_pw1[0].T, conv_w_pw2[0],
                            ffn_w_up[0].T, ffn_w_down[0], ffn_w_up[1].T, ffn_w_down[1]], axis=0).astype(BF16)
    sharded_small = [conv_b_pw1, conv_w_dw, conv_b_dw, conv_ln_g, conv_ln_b, conv_b_pw2, ffn_w_dw]
    small_local = _pack_rows(sharded_small)
    w_qkv_t, small_all = _exchange_small("gather_first", _Gather((0,)), (slab,), small_local, reduce=False)
    parts = [_unpack_rows(small_all[d], [a.shape for a in sharded_small]) for d in range(N_DEV)]
    b_pw1, w_cdw, b_cdw, ln_g, ln_b, b_pw2, w_fdw = [jnp.concatenate([parts[d][i] for d in range(N_DEV)], axis=-1)
                                                      for i in range(len(sharded_small))]
    conv_w32 = jnp.concatenate([w_cdw[0], b_cdw], axis=0)
    sr = attn_sinks.reshape(N_KV, 4, 2)
    sink_cols = jnp.broadcast_to(jnp.repeat(jnp.transpose(sr, (0, 2, 1)), BLOCK, axis=-1).reshape(4, 512, 1),
                                 (4, 512, 128))

    qkv, h0 = _mm("qkv", x0, w_qkv_t, nt=True, tm=512, tn=NQ + KVW, out_dtype=BF16, rms_g=norm_mix[0:1], bias=attn_b_qkv)
    o, w_o, w_up0_t, w_dn0 = _attn_fwd(qkv, sink_cols, _Gather((1, 4, 5)), (slab,))
    x1 = _mm("attn_out", o, w_o, nt=False, tm=512, tn=D, out_dtype=F32, bias=attn_b_o, res=x0)
    x2, h1, up0, act0, gate0, w_pw1_t, w_pw2, w_up1_t, w_dn1 = _ffn_fwd(0, x1, norm_ffn[0:1], w_up0_t, w_fdw[0], ffn_b_dw[0:1], w_dn0,
                                                             tm=256, ex=_Gather((2, 3, 6, 7)), ex_args=(slab,))
    a, h2 = _mm("pw1", x2, w_pw1_t, nt=True, tm=512, tn=2 * D, out_dtype=BF16, rms_g=norm_mix[1:2], bias=b_pw1)
    s_act, u2 = _conv_mid_fwd(a, conv_w32, ln_g, ln_b, tt=256)
    x3 = _mm("pw2", s_act, w_pw2, nt=False, tm=512, tn=D, out_dtype=F32, bias=b_pw2, res=x2)
    x4, h3, up1, act1, gate1 = _ffn_fwd(1, x3, norm_ffn[1:2], w_up1_t, w_fdw[1], ffn_b_dw[1:2], w_dn1, tm=256)
    dx4, loss_acc, dg_final = _loss_head(x4, final_norm.reshape(1, D), tgt, tm=512)
    loss = lax.psum(loss_acc[0, 0], ("x", "y", "c"))

    dx3, d_up1, dg_f1, dwb1 = _ffn_bwd(1, dx4, x3, norm_ffn[1:2], w_up1_t, w_fdw[1], w_dn1, up1, gate1, tm=256)
    dw_dn1 = _mm_tn("ffn1_dwdown", act1, dx4, tn=FFN_HALF, tt=512)
    dw_up1_t = _mm_tn("ffn1_dwup", d_up1, h3, tn=FFN_HALF, tt=512)
    ds_act = _mm("d_pw2", dx3, w_pw2, nt=True, tm=512, tn=D, out_dtype=BF16)
    dw_pw2, db_pw2 = _mm_tn("dw_pw2", s_act, dx3, tn=D, tt=512, colsum_r=True)
    scatter_a = _Scatter((6, 7))
    da, conv_acc8, buf_a = _conv_mid_bwd(ds_act, u2, a, conv_w32, ln_g, ln_b, tt=256, ex=scatter_a, ex_args=(dw_up1_t, dw_dn1))
    conv_acc = jnp.sum(conv_acc8, axis=1)
    dx2, dg_m1 = _dgrad_rms("d_pw1", da, w_pw1_t, x2, norm_mix[1:2], dx3, tm=512, tk=2 * D)
    dw_pw1_t, db_pw1 = _mm_tn("dw_pw1", da, h2, tn=D, tt=512, colsum_l=True)
    dx1, d_up0, dg_f0, dwb0 = _ffn_bwd(0, dx2, x1, norm_ffn[0:1], w_up0_t, w_fdw[0], w_dn0, up0, gate0, tm=256)
    dw_dn0 = _mm_tn("ffn0_dwdown", act0, dx2, tn=FFN_HALF, tt=512)
    dw_up0_t = _mm_tn("ffn0_dwup", d_up0, h1, tn=FFN_HALF, tt=512)
    do = _mm("d_attn_out", dx1, w_o, nt=True, tm=512, tn=D, out_dtype=BF16)
    dw_o, db_o = _mm_tn("dw_o", o, dx1, tn=D, tt=512, colsum_r=True)
    scatter_b = _Scatter((1, 2, 3, 4, 5))
    dq, dkv, dsk, buf_b = _attn_bwd(qkv, do, sink_cols, scatter_b, (dw_o, dw_pw1_t, dw_pw2, dw_up0_t, dw_dn0))
    dqkv = jnp.concatenate([dq, dkv], axis=1)
    grad_x, dg_m0 = _dgrad_rms("d_qkv", dqkv, w_qkv_t, x0, norm_mix[0:1], dx1, tm=512, tk=NQ + KVW)
    dw_qkv_t, db_qkv = _mm_tn("dw_qkv", dqkv, h0, tn=NQ + KVW, tt=512, colsum_l=True)

    dwb0, dwb1 = jnp.sum(dwb0, axis=1), jnp.sum(dwb1, axis=1)
    d_sinks = jnp.transpose(jnp.sum(dsk.reshape(N_KV, 2, 4, BLOCK), axis=-1), (0, 2, 1)).reshape(1, 16)
    small_grads = [jnp.concatenate([dg_m0, dg_m1], axis=0), db_qkv, d_sinks, db_o, jnp.concatenate([dg_f0, dg_f1], axis=0),
                   jnp.stack([dwb0[3], dwb1[3]]), dg_final, db_pw1, conv_acc[0:CONV_K], conv_acc[31:32], conv_acc[32:33],
                   conv_acc[33:34], db_pw2, jnp.stack([dwb0[0:3], dwb1[0:3]])]
    small_shapes = [(2, D), (1, NQ + KVW), (1, 16), (1, D), (2, D), (2, F), (D,), (1, 2 * D), (1, CONV_K, D), (1, D),
                    (1, D), (1, D), (1, D), (2, 3, F)]
    scatter_c = _Scatter((0,))
    buf_c, small_sum = _exchange_small("scatter_last", scatter_c, (dw_qkv_t,), _pack_rows(small_grads), reduce=True)
    gm = [None] * N_MAT
    for tag, sc, buf in (("a", scatter_a, buf_a), ("b", scatter_b, buf_b), ("c", scatter_c, buf_c)):
        gslab = _sum_slabs("sum_slabs_" + tag, buf)
        for i, off in zip(sc.mats, sc.offsets()):
            gm[i] = gslab[off:off + MAT_ROWS[i]]
    (g_norm_mix, g_b_qkv, g_sinks, g_b_o, g_norm_ffn, g_ffn_b_dw, g_final, g_b_pw1_full, g_w_cdw_full, g_b_cdw_full,
     g_ln_g_full, g_ln_b_full, g_b_pw2_full, g_w_fdw_full) = _unpack_rows(small_sum, small_shapes)

    def shard(a, width):
        return lax.dynamic_slice_in_dim(a, me * width, width, axis=a.ndim - 1)

    grads = {
        "norm_mix": g_norm_mix, "attn_w_qkv": gm[0].T[None], "attn_b_qkv": g_b_qkv, "attn_sinks": g_sinks,
        "attn_w_o": gm[1][None], "attn_b_o": g_b_o, "conv_w_pw1": gm[2].T[None], "conv_b_pw1": shard(g_b_pw1_full, 256),
        "conv_w_dw": shard(g_w_cdw_full, 128), "conv_b_dw": shard(g_b_cdw_full, 128), "conv_ln_g": shard(g_ln_g_full, 128),
        "conv_ln_b": shard(g_ln_b_full, 128), "conv_w_pw2": gm[3][None], "conv_b_pw2": shard(g_b_pw2_full, 128),
        "norm_ffn": g_norm_ffn, "ffn_w_up": jnp.stack([gm[4].T, gm[6].T]), "ffn_w_dw": shard(g_w_fdw_full, 352),
        "ffn_b_dw": g_ffn_b_dw, "ffn_w_down": jnp.stack([gm[5], gm[7]]), "final_norm": g_final,
    }
    weights = dict(norm_mix=norm_mix, attn_w_qkv=attn_w_qkv, attn_b_qkv=attn_b_qkv, attn_sinks=attn_sinks, attn_w_o=attn_w_o, attn_b_o=attn_b_o, conv_w_pw1=conv_w_pw1, conv_b_pw1=conv_b_pw1, conv_w_dw=conv_w_dw, conv_b_dw=conv_b_dw, conv_ln_g=conv_ln_g, conv_ln_b=conv_ln_b, conv_w_pw2=conv_w_pw2, conv_b_pw2=conv_b_pw2, norm_ffn=norm_ffn, ffn_w_up=ffn_w_up, ffn_w_dw=ffn_w_dw, ffn_b_dw=ffn_b_dw, ffn_w_down=ffn_w_down, final_norm=final_norm)
    moms = dict(norm_mix=m_norm_mix, attn_w_qkv=m_attn_w_qkv, attn_b_qkv=m_attn_b_qkv, attn_sinks=m_attn_sinks, attn_w_o=m_attn_w_o, attn_b_o=m_attn_b_o, conv_w_pw1=m_conv_w_pw1, conv_b_pw1=m_conv_b_pw1, conv_w_dw=m_conv_w_dw, conv_b_dw=m_conv_b_dw, conv_ln_g=m_conv_ln_g, conv_ln_b=m_conv_ln_b, conv_w_pw2=m_conv_w_pw2, conv_b_pw2=m_conv_b_pw2, norm_ffn=m_norm_ffn, ffn_w_up=m_ffn_w_up, ffn_w_dw=m_ffn_w_dw, ffn_b_dw=m_ffn_b_dw, ffn_w_down=m_ffn_w_down, final_norm=m_final_norm)
    vars_ = dict(norm_mix=v_norm_mix, attn_w_qkv=v_attn_w_qkv, attn_b_qkv=v_attn_b_qkv, attn_sinks=v_attn_sinks, attn_w_o=v_attn_w_o, attn_b_o=v_attn_b_o, conv_w_pw1=v_conv_w_pw1, conv_b_pw1=v_conv_b_pw1, conv_w_dw=v_conv_w_dw, conv_b_dw=v_conv_b_dw, conv_ln_g=v_conv_ln_g, conv_ln_b=v_conv_ln_b, conv_w_pw2=v_conv_w_pw2, conv_b_pw2=v_conv_b_pw2, norm_ffn=v_norm_ffn, ffn_w_up=v_ffn_w_up, ffn_w_dw=v_ffn_w_dw, ffn_b_dw=v_ffn_b_dw, ffn_w_down=v_ffn_w_down, final_norm=v_final_norm)
    names = list(weights)
    big = ("attn_w_qkv", "attn_w_o", "conv_w_pw1", "conv_w_pw2", "ffn_w_up", "ffn_w_down")
    delta, new_m, new_v = {}, {}, {}
    for nm in big:
        shp = weights[nm].shape
        two_d = (shp[0] * shp[1], shp[2])
        res = _adamw("adamw_" + nm, *(t[nm].reshape(two_d) for t in (weights, grads, moms, vars_)))
        delta[nm], new_m[nm], new_v[nm] = (r.reshape(shp) for r in res)
    small_names = [nm for nm in names if nm not in big]
    small_shapes_local = [weights[nm].shape for nm in small_names]
    res = _adamw("adamw_small", *(_pack_rows([t[nm] for nm in small_names]) for t in (weights, grads, moms, vars_)))
    for tgt_dict, slab_out in zip((delta, new_m, new_v), res):
        for nm, val in zip(small_names, _unpack_rows(slab_out, small_shapes_local)):
            tgt_dict[nm] = val
    return (loss, grad_x.reshape(1, s, D), *[grads[nm] for nm in names], *[delta[nm] for nm in names],
            *[new_m[nm] for nm in names], *[new_v[nm] for nm in names])
```

```python
import functools
import math

import jax
import jax.numpy as jnp
from jax import lax
from jax.experimental import pallas as pl
from jax.experimental.pallas import tpu as pltpu

F32 = jnp.float32
BF16 = jnp.bfloat16

D = 1024
F = 2816
HEAD_DIM = 64
N_KV = 2
BLOCK = 128
NQ = 1024
KVW = 256
CONV_K = 31
RMS_EPS = 1e-6
LN_EPS = 1e-5
ADAM_LR, ADAM_B1, ADAM_B2, ADAM_EPS, ADAM_WD, ADAM_STEP = 0.001, 0.9, 0.999, 1e-08, 0.01, 10
N_DEV = 8
MESH = pl.DeviceIdType.MESH
VMEM_LIMIT_BYTES = 56 * 2**20
NEG = float(jnp.finfo(jnp.float32).min)

MAT_ROWS = (160, 128, 256, 128, 704, 352, 704, 352)
MAT_OFF = tuple(sum(MAT_ROWS[:i]) for i in range(len(MAT_ROWS)))
SLAB_ROWS = sum(MAT_ROWS)
N_MAT = len(MAT_ROWS)

NT_DIMS = (((1,), (1,)), ((), ()))
NN_DIMS = (((1,), (0,)), ((), ()))
TN_DIMS = (((0,), (0,)), ((), ()))


def _params(*sem):
    return pltpu.CompilerParams(dimension_semantics=tuple(sem), vmem_limit_bytes=VMEM_LIMIT_BYTES)


def _row_tile(rows, cap, mult=8):
    best = None
    for t in range(mult, min(rows, cap) + 1, mult):
        if rows % t == 0:
            best = t
    return best if best is not None else rows


def _sigmoid(x):
    return 1.0 / (1.0 + jnp.exp(-x))


def _mm(name, a, b, *, nt, tm, tn, out_dtype, rms_g=None, bias=None, res=None):
    m, k = a.shape
    n = b.shape[0] if nt else b.shape[1]
    tm = min(tm, m)
    has_rms = rms_g is not None
    dims = NT_DIMS if nt else NN_DIMS

    def body(*refs):
        a_ref, b_ref = refs[0], refs[1]
        pos = 2
        g_ref = bias_ref = res_ref = h_ref = hs_ref = None
        if has_rms:
            g_ref = refs[pos]; pos += 1
        if bias is not None:
            bias_ref = refs[pos]; pos += 1
        if res is not None:
            res_ref = refs[pos]; pos += 1
        o_ref = refs[pos]; pos += 1
        if has_rms:
            h_ref, hs_ref = refs[pos], refs[pos + 1]

            @pl.when(pl.program_id(1) == 0)
            def _():
                xf = a_ref[...]
                r = lax.rsqrt(jnp.mean(xf * xf, axis=-1, keepdims=True) + RMS_EPS)
                h = ((xf * r) * g_ref[...]).astype(BF16)
                hs_ref[...] = h
                h_ref[...] = h

            lhs = hs_ref[...]
        else:
            lhs = a_ref[...].astype(BF16)
        acc = lax.dot_general(lhs, b_ref[...], dims, preferred_element_type=F32)
        if bias is not None:
            acc = acc + bias_ref[...]
        if res is not None:
            acc = acc + res_ref[...]
        o_ref[...] = acc.astype(out_dtype)

    in_specs = [pl.BlockSpec((tm, k), lambda i, j: (i, 0)),
                pl.BlockSpec((tn, k), lambda i, j: (j, 0)) if nt else pl.BlockSpec((k, tn), lambda i, j: (0, j))]
    args = [a, b]
    if has_rms:
        in_specs.append(pl.BlockSpec((1, k), lambda i, j: (0, 0))); args.append(rms_g)
    if bias is not None:
        in_specs.append(pl.BlockSpec((1, tn), lambda i, j: (0, j))); args.append(bias)
    if res is not None:
        in_specs.append(pl.BlockSpec((tm, tn), lambda i, j: (i, j))); args.append(res)
    out_shape = [jax.ShapeDtypeStruct((m, n), out_dtype)]
    out_specs = [pl.BlockSpec((tm, tn), lambda i, j: (i, j))]
    scratch = []
    if has_rms:
        out_shape.append(jax.ShapeDtypeStruct((m, k), BF16))
        out_specs.append(pl.BlockSpec((tm, k), lambda i, j: (i, 0)))
        scratch.append(pltpu.VMEM((tm, k), BF16))
    outs = pl.pallas_call(body, grid=(m // tm, n // tn), in_specs=in_specs, out_specs=out_specs, out_shape=out_shape,
                          scratch_shapes=scratch, compiler_params=_params("parallel", "arbitrary"), name=name)(*args)
    return outs if has_rms else outs[0]


def _mm_tn(name, l, r, *, tn, tt, colsum_l=False, colsum_r=False):
    s, nl = l.shape
    nr = r.shape[1]
    tt = min(tt, s)
    n_t = s // tt

    def body(*refs):
        l_ref, r_ref, o_ref = refs[0], refs[1], refs[2]
        pos = 3
        cl_ref = cr_ref = None
        if colsum_l:
            cl_ref = refs[pos]; pos += 1
        if colsum_r:
            cr_ref = refs[pos]; pos += 1
        acc_ref = refs[pos]
        j, t = pl.program_id(0), pl.program_id(1)

        @pl.when(t == 0)
        def _():
            acc_ref[...] = jnp.zeros_like(acc_ref)

        lv = l_ref[...]
        rv = r_ref[...]
        acc_ref[...] += lax.dot_general(lv, rv.astype(BF16), TN_DIMS, preferred_element_type=F32)
        if colsum_l:
            @pl.when(t == 0)
            def _():
                cl_ref[...] = jnp.zeros_like(cl_ref)

            cl_ref[...] += jnp.sum(lv.astype(F32), axis=0, keepdims=True)
        if colsum_r:
            @pl.when((t == 0) & (j == 0))
            def _():
                cr_ref[...] = jnp.zeros_like(cr_ref)

            @pl.when(j == 0)
            def _():
                cr_ref[...] += jnp.sum(rv.astype(F32), axis=0, keepdims=True)

        @pl.when(t == n_t - 1)
        def _():
            o_ref[...] = acc_ref[...].astype(BF16)

    out_shape = [jax.ShapeDtypeStruct((nl, nr), BF16)]
    out_specs = [pl.BlockSpec((tn, nr), lambda j, t: (j, 0))]
    if colsum_l:
        out_shape.append(jax.ShapeDtypeStruct((1, nl), F32))
        out_specs.append(pl.BlockSpec((1, tn), lambda j, t: (0, j)))
    if colsum_r:
        out_shape.append(jax.ShapeDtypeStruct((1, nr), F32))
        out_specs.append(pl.BlockSpec((1, nr), lambda j, t: (0, 0)))
    outs = pl.pallas_call(
        body, grid=(nl // tn, n_t),
        in_specs=[pl.BlockSpec((tt, tn), lambda j, t: (t, j)), pl.BlockSpec((tt, nr), lambda j, t: (t, 0))],
        out_specs=out_specs, out_shape=out_shape, scratch_shapes=[pltpu.VMEM((tn, nr), F32)],
        compiler_params=_params("arbitrary", "arbitrary"), name=name)(l, r)
    return outs if (colsum_l or colsum_r) else outs[0]


def _rms_bwd(dh, xf, g):
    r = lax.rsqrt(jnp.mean(xf * xf, axis=-1, keepdims=True) + RMS_EPS)
    gd = dh * g
    dx = r * gd - xf * ((r * r * r) * jnp.mean(xf * gd, axis=-1, keepdims=True))
    return dx, dh * (xf * r)


def _dgrad_rms(name, dy, w, x, g, dres, *, tm, tk):
    m, k = dy.shape
    tm = min(tm, m)
    n_k = k // tk

    def body(dy_ref, w_ref, x_ref, g_ref, dres_ref, o_ref, dg_ref, acc_ref):
        i, kk = pl.program_id(0), pl.program_id(1)

        @pl.when(kk == 0)
        def _():
            acc_ref[...] = jnp.zeros_like(acc_ref)

        acc_ref[...] += lax.dot_general(dy_ref[...], w_ref[...], NN_DIMS, preferred_element_type=F32)

        @pl.when((i == 0) & (kk == n_k - 1))
        def _():
            dg_ref[...] = jnp.zeros_like(dg_ref)

        @pl.when(kk == n_k - 1)
        def _():
            dx, dg_rows = _rms_bwd(acc_ref[...], x_ref[...], g_ref[...])
            o_ref[...] = dres_ref[...] + dx
            dg_ref[...] += jnp.sum(dg_rows, axis=0, keepdims=True)

    return pl.pallas_call(
        body, grid=(m // tm, n_k),
        in_specs=[pl.BlockSpec((tm, tk), lambda i, kk: (i, kk)), pl.BlockSpec((tk, D), lambda i, kk: (kk, 0)),
                  pl.BlockSpec((tm, D), lambda i, kk: (i, 0)), pl.BlockSpec((1, D), lambda i, kk: (0, 0)),
                  pl.BlockSpec((tm, D), lambda i, kk: (i, 0))],
        out_specs=[pl.BlockSpec((tm, D), lambda i, kk: (i, 0)), pl.BlockSpec((1, D), lambda i, kk: (0, 0))],
        out_shape=[jax.ShapeDtypeStruct((m, D), F32), jax.ShapeDtypeStruct((1, D), F32)],
        scratch_shapes=[pltpu.VMEM((tm, D), F32)],
        compiler_params=_params("arbitrary", "arbitrary"), name=name)(dy, w, x, g, dres)


def _loss_head(x4, g, tgt, *, tm):
    m = x4.shape[0]
    tm = min(tm, m)

    def body(x_ref, g_ref, t_ref, dx_ref, loss_ref, dg_ref):
        @pl.when(pl.program_id(0) == 0)
        def _():
            loss_ref[...] = jnp.zeros_like(loss_ref)
            dg_ref[...] = jnp.zeros_like(dg_ref)

        xf = x_ref[...]
        gv = g_ref[...]
        r = lax.rsqrt(jnp.mean(xf * xf, axis=-1, keepdims=True) + RMS_EPS)
        diff = (xf * r) * gv - t_ref[...]
        loss_ref[...] += 0.5 * jnp.sum(jnp.mean(diff * diff, axis=-1, keepdims=True))
        dx, dg_rows = _rms_bwd(diff * (1.0 / D), xf, gv)
        dx_ref[...] = dx
        dg_ref[...] += jnp.sum(dg_rows, axis=0, keepdims=True)

    return pl.pallas_call(
        body, grid=(m // tm,),
        in_specs=[pl.BlockSpec((tm, D), lambda i: (i, 0)), pl.BlockSpec((1, D), lambda i: (0, 0)),
                  pl.BlockSpec((tm, D), lambda i: (i, 0))],
        out_specs=[pl.BlockSpec((tm, D), lambda i: (i, 0)), pl.BlockSpec((1, 128), lambda i: (0, 0)),
                   pl.BlockSpec((1, D), lambda i: (0, 0))],
        out_shape=[jax.ShapeDtypeStruct((m, D), F32), jax.ShapeDtypeStruct((1, 128), F32),
                   jax.ShapeDtypeStruct((1, D), F32)],
        compiler_params=_params("arbitrary"), name="loss_head")(x4, g, tgt)


def _band(kvp, kvc):
    kk = jnp.concatenate([kvp[:, :128], kvc[:, :128]], axis=0)
    vv = jnp.concatenate([kvp[:, 128:], kvc[:, 128:]], axis=0)
    return kk, vv


def _blockdiag(t, h):
    lane = lax.broadcasted_iota(jnp.int32, t.shape, 1)
    z = jnp.zeros_like(t)
    tr = pltpu.roll(t, 64, 1)
    if h == 0:
        top, bot = jnp.where(lane < 64, t, z), jnp.where(lane >= 64, tr, z)
    else:
        top, bot = jnp.where(lane < 64, tr, z), jnp.where(lane >= 64, t, z)
    return jnp.concatenate([top, bot], axis=0)


def _from_blockdiag(t, h):
    lane = lax.broadcasted_iota(jnp.int32, (256, 128), 1)
    top, bot = t[:256], t[256:]
    if h == 0:
        return jnp.where(lane < 64, top + pltpu.roll(bot, 64, 1), 0.0)
    return jnp.where(lane >= 64, pltpu.roll(top, 64, 1) + bot, 0.0)


def _stack_heads(ref, h):
    return jnp.concatenate([ref[:, h * 512 + p * 128: h * 512 + (p + 1) * 128] for p in range(4)], axis=0)


def _valid_mask(n):
    row = lax.broadcasted_iota(jnp.int32, (512, 256), 0) & 127
    col = lax.broadcasted_iota(jnp.int32, (512, 256), 1)
    return (col > row) & (col <= row + BLOCK) & ((col >= BLOCK) | (n > 0))


def _softmax_half(s_half, valid, sink):
    s_half = jnp.where(valid, s_half, NEG)
    mx = jnp.maximum(jnp.max(s_half, axis=-1, keepdims=True), sink)
    p = jnp.exp(s_half - mx)
    e_sink = jnp.exp(sink - mx)
    inv = 1.0 / (jnp.sum(p, axis=-1, keepdims=True) + e_sink)
    return p * inv, e_sink * inv


def _attn_fwd(qkv, sink_cols, ex=None, ex_args=()):
    s = qkv.shape[0]
    nb = s // BLOCK
    scale = 1.0 / math.sqrt(HEAD_DIM)

    def body(q_ref, kvc_ref, kvp_ref, sk_ref, o_ref):
        n = pl.program_id(0)
        kk, vv = _band(kvp_ref[...], kvc_ref[...])
        valid = _valid_mask(n)
        for h in range(N_KV):
            kbd, vbd = _blockdiag(kk, h), _blockdiag(vv, h)
            qp = _stack_heads(q_ref, h)
            sc = lax.dot_general(qp, kbd, NT_DIMS, preferred_element_type=F32) * scale
            probs = []
            for half in range(2):
                sink = sk_ref[h * 2 + half][:, :1]
                pr, _ = _softmax_half(sc[:, half * 256:(half + 1) * 256], valid, sink)
                probs.append(pr.astype(BF16))
            o = lax.dot_general(jnp.concatenate(probs, axis=1), vbd, NN_DIMS, preferred_element_type=F32)
            for p in range(4):
                o_ref[:, h * 512 + p * 128: h * 512 + (p + 1) * 128] = o[p * 128:(p + 1) * 128].astype(BF16)

    return pl.pallas_call(
        _carried(body, 4, 1, ex, nb), grid=(nb,),
        in_specs=[pl.BlockSpec((BLOCK, NQ), lambda n: (n, 0)),
                  pl.BlockSpec((BLOCK, KVW), lambda n: (n, NQ // KVW)),
                  pl.BlockSpec((BLOCK, KVW), lambda n: (jnp.maximum(n - 1, 0), NQ // KVW)),
                  pl.BlockSpec((4, 512, 128), lambda n: (0, 0, 0))] + _ex(ex, "in_specs"),
        out_specs=[pl.BlockSpec((BLOCK, NQ), lambda n: (n, 0))] + _ex(ex, "out_specs"),
        out_shape=[jax.ShapeDtypeStruct((s, NQ), BF16)] + _ex(ex, "out_shape"), scratch_shapes=_ex(ex, "scratch"),
        compiler_params=_params("arbitrary"), name="attn_fwd")(qkv, qkv, qkv, sink_cols, *ex_args)


def _attn_bwd(qkv, do, sink_cols, ex=None, ex_args=()):
    s = qkv.shape[0]
    nb = s // BLOCK
    scale = 1.0 / math.sqrt(HEAD_DIM)

    def body(q_ref, kvc_ref, kvp_ref, do_ref, sk_ref, dq_ref, dkv_ref, dsk_ref, ck_ref, cv_ref):
        n = pl.program_id(0)

        @pl.when(n == 0)
        def _():
            dsk_ref[...] = jnp.zeros_like(dsk_ref)
            ck_ref[...] = jnp.zeros_like(ck_ref)
            cv_ref[...] = jnp.zeros_like(cv_ref)

        @pl.when(n < nb)
        def _():
            kk, vv = _band(kvp_ref[...], kvc_ref[...])
            valid = _valid_mask(n)
            dkk = jnp.zeros((256, 128), F32)
            dvv = jnp.zeros((256, 128), F32)
            for h in range(N_KV):
                kbd, vbd = _blockdiag(kk, h), _blockdiag(vv, h)
                qp = _stack_heads(q_ref, h)
                dop = _stack_heads(do_ref, h)
                sc = lax.dot_general(qp, kbd, NT_DIMS, preferred_element_type=F32) * scale
                dp = lax.dot_general(dop, vbd, NT_DIMS, preferred_element_type=F32)
                probs, dss = [], []
                for half in range(2):
                    sink = sk_ref[h * 2 + half][:, :1]
                    pr, p_sink = _softmax_half(sc[:, half * 256:(half + 1) * 256], valid, sink)
                    dph = dp[:, half * 256:(half + 1) * 256]
                    delta = jnp.sum(pr * dph, axis=-1, keepdims=True)
                    dss.append((pr * (dph - delta) * scale).astype(BF16))
                    probs.append(pr.astype(BF16))
                    dsk_ref[h * 2 + half] += -(p_sink * delta)
                ds = jnp.concatenate(dss, axis=1)
                pb = jnp.concatenate(probs, axis=1)
                dqp = lax.dot_general(ds, kbd, NN_DIMS, preferred_element_type=F32)
                for p in range(4):
                    dq_ref[:, h * 512 + p * 128: h * 512 + (p + 1) * 128] = dqp[p * 128:(p + 1) * 128].astype(BF16)
                dkk = dkk + _from_blockdiag(lax.dot_general(ds, qp, TN_DIMS, preferred_element_type=F32), h)
                dvv = dvv + _from_blockdiag(lax.dot_general(pb, dop, TN_DIMS, preferred_element_type=F32), h)

            @pl.when(n >= 1)
            def _():
                dkv_ref[:, :128] = (ck_ref[...] + dkk[:128]).astype(BF16)
                dkv_ref[:, 128:] = (cv_ref[...] + dvv[:128]).astype(BF16)

            ck_ref[...] = dkk[128:]
            cv_ref[...] = dvv[128:]

        @pl.when(n == nb)
        def _():
            dkv_ref[:, :128] = ck_ref[...].astype(BF16)
            dkv_ref[:, 128:] = cv_ref[...].astype(BF16)

    last = nb - 1
    return pl.pallas_call(
        _carried(body, 5, 3, ex, nb + 1), grid=(nb + 1,),
        in_specs=[pl.BlockSpec((BLOCK, NQ), lambda n: (jnp.minimum(n, last), 0)),
                  pl.BlockSpec((BLOCK, KVW), lambda n: (jnp.minimum(n, last), NQ // KVW)),
                  pl.BlockSpec((BLOCK, KVW), lambda n: (jnp.clip(n - 1, 0, last), NQ // KVW)),
                  pl.BlockSpec((BLOCK, NQ), lambda n: (jnp.minimum(n, last), 0)),
                  pl.BlockSpec((4, 512, 128), lambda n: (0, 0, 0))] + _ex(ex, "in_specs"),
        out_specs=[pl.BlockSpec((BLOCK, NQ), lambda n: (jnp.minimum(n, last), 0)),
                   pl.BlockSpec((BLOCK, KVW), lambda n: (jnp.maximum(n - 1, 0), 0)),
                   pl.BlockSpec((4, 512, 1), lambda n: (0, 0, 0))] + _ex(ex, "out_specs"),
        out_shape=[jax.ShapeDtypeStruct((s, NQ), BF16), jax.ShapeDtypeStruct((s, KVW), BF16),
                   jax.ShapeDtypeStruct((4, 512, 1), F32)] + _ex(ex, "out_shape"),
        scratch_shapes=[pltpu.VMEM((BLOCK, 128), F32), pltpu.VMEM((BLOCK, 128), F32)] + _ex(ex, "scratch"),
        compiler_params=_params("arbitrary"), name="attn_bwd")(qkv, qkv, qkv, do, sink_cols, *ex_args)


LANE = 128
FFN_HALF = F // 2
FFN_PAIR = 2 * LANE
FFN_PAIRS = F // FFN_PAIR
FFN_GROUPS = ((0, 3), (3, 6), (6, 9), (9, 11))
FFN_HALO = 8


def _resident(shape):
    return pl.BlockSpec(shape, lambda i: (0,) * len(shape), pipeline_mode=pl.Buffered(1))


def _ffn_fwd(tag, x_in, g, w_up_t, w_dw, b_dw, w_down, *, tm, ex=None, ex_args=()):
    s = x_in.shape[0]
    tm = min(tm, s)

    def body(x_ref, g_ref, wup_ref, wdw_ref, bdw_ref, wd_ref, o_ref, h_ref, up_ref, act_ref, gate_ref, carry_ref, ext_ref):
        @pl.when(pl.program_id(0) == 0)
        def _():
            carry_ref[...] = jnp.zeros_like(carry_ref)

        xf = x_ref[...]
        r = lax.rsqrt(jnp.mean(xf * xf, axis=-1, keepdims=True) + RMS_EPS)
        h = ((xf * r) * g_ref[...]).astype(BF16)
        h_ref[...] = h

        def up_mm(p):
            for off in (p * FFN_PAIR, F + p * FFN_PAIR):
                rs = slice(off, off + FFN_PAIR)
                up_ref[:, rs] = lax.dot_general(h, wup_ref[rs, :], NT_DIMS, preferred_element_type=F32).astype(BF16)

        def elementwise(p):
            for c in (2 * p, 2 * p + 1):
                cs = slice(c * LANE, (c + 1) * LANE)
                vs = slice(F + c * LANE, F + (c + 1) * LANE)
                gcol = up_ref[:, cs].astype(F32)
                ext_ref[pl.ds(0, FFN_HALO), :] = carry_ref[:, cs]
                ext_ref[pl.ds(FFN_HALO, tm), :] = gcol
                gate_b = (wdw_ref[2:3, cs] * gcol + wdw_ref[1:2, cs] * ext_ref[pl.ds(FFN_HALO - 1, tm), :]
                          + wdw_ref[0:1, cs] * ext_ref[pl.ds(FFN_HALO - 2, tm), :] + bdw_ref[:, cs]).astype(BF16)
                gate_ref[:, cs] = gate_b
                gate = gate_b.astype(F32)
                act_ref[:, cs] = (gate * _sigmoid(gate) * up_ref[:, vs].astype(F32)).astype(BF16)
                carry_ref[:, cs] = gcol[tm - FFN_HALO:]

        def down_mm(lo, hi):
            ks = slice(lo * FFN_PAIR, hi * FFN_PAIR)
            return lax.dot_general(act_ref[:, ks], wd_ref[ks, :], NN_DIMS, preferred_element_type=F32)

        acc = xf
        up_mm(0)
        pending = None
        for lo, hi in FFN_GROUPS:
            for p in range(lo, hi):
                if p + 1 < FFN_PAIRS:
                    up_mm(p + 1)
                if pending is not None:
                    acc = acc + down_mm(*pending)
                    pending = None
                elementwise(p)
            pending = (lo, hi)
        o_ref[...] = acc + down_mm(*pending)

    row = lambda i: (i, 0)
    return pl.pallas_call(
        _carried(body, 6, 5, ex, s // tm), grid=(s // tm,),
        in_specs=[pl.BlockSpec((tm, D), row), _resident((1, D)), _resident((2 * F, D)), _resident((3, F)),
                  _resident((1, F)), _resident((F, D))] + _ex(ex, "in_specs"),
        out_specs=[pl.BlockSpec((tm, D), row), pl.BlockSpec((tm, D), row), pl.BlockSpec((tm, 2 * F), row),
                   pl.BlockSpec((tm, F), row), pl.BlockSpec((tm, F), row)] + _ex(ex, "out_specs"),
        out_shape=[jax.ShapeDtypeStruct((s, D), F32), jax.ShapeDtypeStruct((s, D), BF16),
                   jax.ShapeDtypeStruct((s, 2 * F), BF16), jax.ShapeDtypeStruct((s, F), BF16),
                   jax.ShapeDtypeStruct((s, F), BF16)] + _ex(ex, "out_shape"),
        scratch_shapes=[pltpu.VMEM((FFN_HALO, F), F32), pltpu.VMEM((tm + FFN_HALO, LANE), F32)] + _ex(ex, "scratch"),
        compiler_params=_params("arbitrary"), name=f"ffn{tag}_fwd")(x_in, g, w_up_t, w_dw, b_dw, w_down, *ex_args)


def _ffn_bwd(tag, dx_out, x_in, g, w_up_t, w_dw, w_down, up, gate, *, tm):
    s = x_in.shape[0]
    tm = min(tm, s)
    n_i = s // tm

    def body(dx_ref, x_ref, g_ref, up_ref, gate_ref, wup_ref, wdw_ref, wd_ref,
             dxin_ref, dup_ref, dg_ref, dwb_ref, carry_ref, da0_ref, da1_ref, extd_ref):
        i = pl.program_id(0)

        @pl.when(i == 0)
        def _():
            carry_ref[...] = jnp.zeros_like(carry_ref)
            dg_ref[...] = jnp.zeros_like(dg_ref)
            dwb_ref[...] = jnp.zeros_like(dwb_ref)

        dxf = dx_ref[...]
        dxb = dxf.astype(BF16)
        da_refs = (da0_ref, da1_ref)

        def da_mm(p):
            ks = slice(p * FFN_PAIR, (p + 1) * FFN_PAIR)
            da_refs[p % 2][...] = lax.dot_general(dxb, wd_ref[ks, :], NT_DIMS, preferred_element_type=F32)

        def elementwise(p):
            da_ref = da_refs[p % 2]
            for c in range(2):
                cc = 2 * p + c
                cs = slice(cc * LANE, (cc + 1) * LANE)
                vs = slice(F + cc * LANE, F + (cc + 1) * LANE)
                gate = gate_ref[:, cs].astype(F32)
                sg = _sigmoid(gate)
                silu = gate * sg
                dac = da_ref[:, c * LANE:(c + 1) * LANE]
                dup_ref[:, vs] = (dac * silu).astype(BF16)
                dgate = dac * up_ref[:, vs].astype(F32) * (sg + silu * (1.0 - sg))
                extd_ref[pl.ds(0, tm), :] = dgate
                extd_ref[pl.ds(tm, FFN_HALO), :] = carry_ref[:, cs]
                d1 = extd_ref[pl.ds(1, tm), :]
                d2 = extd_ref[pl.ds(2, tm), :]
                dup_ref[:, cs] = (wdw_ref[2:3, cs] * dgate + wdw_ref[1:2, cs] * d1 + wdw_ref[0:1, cs] * d2).astype(BF16)
                carry_ref[:, cs] = dgate[:FFN_HALO]
                gcol = up_ref[:, cs].astype(F32)
                dwb_ref[2, :, cs] += _sum8(dgate * gcol)
                dwb_ref[1, :, cs] += _sum8(d1 * gcol)
                dwb_ref[0, :, cs] += _sum8(d2 * gcol)
                dwb_ref[3, :, cs] += _sum8(dgate)

        def dh_mm(lo, hi):
            ks = slice(lo * FFN_PAIR, hi * FFN_PAIR)
            vks = slice(F + lo * FFN_PAIR, F + hi * FFN_PAIR)
            return (lax.dot_general(dup_ref[:, ks], wup_ref[ks, :], NN_DIMS, preferred_element_type=F32)
                    + lax.dot_general(dup_ref[:, vks], wup_ref[vks, :], NN_DIMS, preferred_element_type=F32))

        acc = None
        da_mm(0)
        pending = None
        for lo, hi in FFN_GROUPS:
            for p in range(lo, hi):
                if p + 1 < FFN_PAIRS:
                    da_mm(p + 1)
                if pending is not None:
                    part = dh_mm(*pending)
                    acc = part if acc is None else acc + part
                    pending = None
                elementwise(p)
            pending = (lo, hi)
        acc = acc + dh_mm(*pending)
        dx, dg_rows = _rms_bwd(acc, x_ref[...], g_ref[...])
        dxin_ref[...] = dxf + dx
        dg_ref[...] += jnp.sum(dg_rows, axis=0, keepdims=True)

    rev = lambda i: (n_i - 1 - i, 0)
    return pl.pallas_call(
        body, grid=(n_i,),
        in_specs=[pl.BlockSpec((tm, D), rev), pl.BlockSpec((tm, D), rev), _resident((1, D)),
                  pl.BlockSpec((tm, 2 * F), rev), pl.BlockSpec((tm, F), rev),
                  _resident((2 * F, D)), _resident((3, F)), _resident((F, D))],
        out_specs=[pl.BlockSpec((tm, D), rev), pl.BlockSpec((tm, 2 * F), rev),
                   pl.BlockSpec((1, D), lambda i: (0, 0)), pl.BlockSpec((4, 8, F), lambda i: (0, 0, 0))],
        out_shape=[jax.ShapeDtypeStruct((s, D), F32), jax.ShapeDtypeStruct((s, 2 * F), BF16),
                   jax.ShapeDtypeStruct((1, D), F32), jax.ShapeDtypeStruct((4, 8, F), F32)],
        scratch_shapes=[pltpu.VMEM((FFN_HALO, F), F32), pltpu.VMEM((tm, FFN_PAIR), F32), pltpu.VMEM((tm, FFN_PAIR), F32),
                        pltpu.VMEM((tm + FFN_HALO, LANE), F32)],
        compiler_params=_params("arbitrary"), name=f"ffn{tag}_bwd")(dx_out, x_in, g, up, gate, w_up_t, w_dw, w_down)


CONV_HALO = 32
CONV_CHUNK = 256
CONV_ROWS = 64


def _ln(u2, g, b):
    mu = jnp.mean(u2, axis=-1, keepdims=True)
    xc = u2 - mu
    rstd = lax.rsqrt(jnp.mean(xc * xc, axis=-1, keepdims=True) + LN_EPS)
    xhat = xc * rstd
    return xhat, rstd, xhat * g + b


def _phase_taps(ph):
    first = CONV_HALO - (CONV_K - 1)
    return [(k, first + k - ph) for k in range(CONV_K) if (first + k) % 8 == ph]


def _sum8(v):
    out = v[0:8]
    for j in range(8, v.shape[0], 8):
        out = out + v[j:j + 8]
    return out


def _store_phases(src_ref, sh_ref, tt, cols=slice(None)):
    sh_ref[0] = src_ref[:, cols]
    for ph in range(1, 8):
        sh_ref[ph, pl.ds(0, tt + CONV_HALO - 8), :] = src_ref[pl.ds(ph, tt + CONV_HALO - 8), cols]


def _conv_taps_fwd(w_ref, sh_ref, u2_ref, cs, tt):
    for base in range(0, tt, CONV_ROWS):
        acc = jnp.zeros((CONV_ROWS, CONV_CHUNK), F32) + w_ref[CONV_K:CONV_K + 1, cs]
        for ph in range(8):
            for k, off in _phase_taps(ph):
                acc = acc + w_ref[k:k + 1, cs] * sh_ref[ph, pl.ds(base + off, CONV_ROWS), :]
        u2_ref[pl.ds(base, CONV_ROWS), cs] = acc.astype(BF16)


def _conv_fwd(x_in, g, w1t, b1, wdw, ln_g, ln_b, w2, b2, *, tt):
    s = x_in.shape[0]
    tt = min(tt, s)
    n_c = D // CONV_CHUNK

    def body(x_ref, g_ref, w1_ref, b1_ref, w_ref, lg_ref, lb_ref, w2_ref, b2_ref,
             o_ref, h_ref, a_ref, u2_ref, s_ref, carry_ref, ext_ref, sh_ref):
        @pl.when(pl.program_id(0) == 0)
        def _():
            carry_ref[...] = jnp.zeros_like(carry_ref)

        xf = x_ref[...]
        r = lax.rsqrt(jnp.mean(xf * xf, axis=-1, keepdims=True) + RMS_EPS)
        h = ((xf * r) * g_ref[...]).astype(BF16)
        h_ref[...] = h

        def pw1_mm(c):
            for off in (c * CONV_CHUNK, D + c * CONV_CHUNK):
                rs = slice(off, off + CONV_CHUNK)
                a_ref[:, rs] = (lax.dot_general(h, w1_ref[rs, :], NT_DIMS, preferred_element_type=F32)
                                + b1_ref[:, rs]).astype(BF16)

        pw1_mm(0)
        for c in range(n_c):
            if c + 1 < n_c:
                pw1_mm(c + 1)
            cs = slice(c * CONV_CHUNK, (c + 1) * CONV_CHUNK)
            gs = slice(D + c * CONV_CHUNK, D + (c + 1) * CONV_CHUNK)
            u = a_ref[:, cs].astype(F32) * _sigmoid(a_ref[:, gs].astype(F32))
            ext_ref[pl.ds(0, CONV_HALO), :] = carry_ref[:, cs]
            ext_ref[pl.ds(CONV_HALO, tt), :] = u
            carry_ref[:, cs] = u[tt - CONV_HALO:]
            _store_phases(ext_ref, sh_ref, tt)
            _conv_taps_fwd(w_ref, sh_ref, u2_ref, cs, tt)
        _, _, ln = _ln(u2_ref[...].astype(F32), lg_ref[...], lb_ref[...])
        sv = (ln * _sigmoid(ln)).astype(BF16)
        s_ref[...] = sv
        o_ref[...] = xf + lax.dot_general(sv, w2_ref[...], NN_DIMS, preferred_element_type=F32) + b2_ref[...]

    row = lambda i: (i, 0)
    return pl.pallas_call(
        body, grid=(s // tt,),
        in_specs=[pl.BlockSpec((tt, D), row), _resident((1, D)), _resident((2 * D, D)), _resident((1, 2 * D)),
                  _resident((32, D)), _resident((1, D)), _resident((1, D)), _resident((D, D)), _resident((1, D))],
        out_specs=[pl.BlockSpec((tt, D), row), pl.BlockSpec((tt, D), row), pl.BlockSpec((tt, 2 * D), row),
                   pl.BlockSpec((tt, D), row), pl.BlockSpec((tt, D), row)],
        out_shape=[jax.ShapeDtypeStruct((s, D), F32), jax.ShapeDtypeStruct((s, D), BF16),
                   jax.ShapeDtypeStruct((s, 2 * D), BF16), jax.ShapeDtypeStruct((s, D), BF16),
                   jax.ShapeDtypeStruct((s, D), BF16)],
        scratch_shapes=[pltpu.VMEM((CONV_HALO, D), F32), pltpu.VMEM((tt + CONV_HALO, CONV_CHUNK), F32),
                        pltpu.VMEM((8, tt + CONV_HALO, CONV_CHUNK), F32)],
        compiler_params=_params("arbitrary"), name="conv_fwd")(x_in, g, w1t, b1, wdw, ln_g, ln_b, w2, b2)


def _conv_bwd(dx_out, x_in, g, a, u2, w1t, wdw, ln_g, ln_b, w2, *, tt, ex=None, ex_args=()):
    s = x_in.shape[0]
    tt = min(tt, s)
    hb = tt // CONV_HALO
    n_i = s // tt
    te = tt + CONV_HALO
    n_c = D // CONV_CHUNK

    def body(dx_ref, x_ref, g_ref, a_ref, prev_ref, u2_ref, w1_ref, w_ref, lg_ref, lb_ref, w2_ref,
             dxin_ref, da_ref, dg_ref, acc_ref, db1_ref, db2_ref,
             carry_ref, e2_ref, ext_ref, ush_ref, dsh_ref, du_ref):
        i = pl.program_id(0)

        @pl.when(i == 0)
        def _():
            for ref in (dg_ref, acc_ref, db1_ref, db2_ref, carry_ref):
                ref[...] = jnp.zeros_like(ref)

        dxf = dx_ref[...]
        db2_ref[...] += _sum8(dxf)
        dse = lax.dot_general(dxf.astype(BF16), w2_ref[...], NT_DIMS, preferred_element_type=F32)
        gv = lg_ref[...]
        xhat, rstd, ln = _ln(u2_ref[...].astype(F32), gv, lb_ref[...])
        sg = _sigmoid(ln)
        dln = dse * (sg * (1.0 + ln * (1.0 - sg)))
        acc_ref[32] += _sum8(dln * xhat)
        acc_ref[33] += _sum8(dln)
        dxh = dln * gv
        du2 = rstd * (dxh - jnp.mean(dxh, axis=-1, keepdims=True) - xhat * jnp.mean(dxh * xhat, axis=-1, keepdims=True))
        e2_ref[pl.ds(0, tt), :] = du2
        e2_ref[pl.ds(tt, CONV_HALO), :] = carry_ref[...]
        carry_ref[...] = du2[:CONV_HALO]
        first_tile = i == n_i - 1

        def dh_mm(c):
            cs = slice(c * CONV_CHUNK, (c + 1) * CONV_CHUNK)
            gs = slice(D + c * CONV_CHUNK, D + (c + 1) * CONV_CHUNK)
            return (lax.dot_general(da_ref[:, cs], w1_ref[cs, :], NN_DIMS, preferred_element_type=F32)
                    + lax.dot_general(da_ref[:, gs], w1_ref[gs, :], NN_DIMS, preferred_element_type=F32))

        dh = None
        for c in range(n_c):
            if c >= 1:
                part = dh_mm(c - 1)
                dh = part if dh is None else dh + part
            cs = slice(c * CONV_CHUNK, (c + 1) * CONV_CHUNK)
            gs = slice(D + c * CONV_CHUNK, D + (c + 1) * CONV_CHUNK)
            uh = prev_ref[:, cs].astype(F32) * _sigmoid(prev_ref[:, gs].astype(F32))
            ext_ref[pl.ds(0, CONV_HALO), :] = jnp.where(first_tile, 0.0, uh)
            a1 = a_ref[:, cs].astype(F32)
            sg2 = _sigmoid(a_ref[:, gs].astype(F32))
            ext_ref[pl.ds(CONV_HALO, tt), :] = a1 * sg2
            _store_phases(ext_ref, ush_ref, tt)
            _store_phases(e2_ref, dsh_ref, tt, cs)
            for base in range(0, tt, CONV_ROWS):
                d0 = e2_ref[pl.ds(base, CONV_ROWS), cs]
                du = jnp.zeros((CONV_ROWS, CONV_CHUNK), F32)
                for ph in range(8):
                    for k, off in _phase_taps(ph):
                        acc_ref[k, :, cs] += _sum8(d0 * ush_ref[ph, pl.ds(base + off, CONV_ROWS), :])
                    for k in range(CONV_K):
                        if (CONV_K - 1 - k) % 8 == ph:
                            du = du + w_ref[k:k + 1, cs] * dsh_ref[ph, pl.ds(base + CONV_K - 1 - k - ph, CONV_ROWS), :]
                acc_ref[CONV_K, :, cs] += _sum8(d0)
                du_ref[pl.ds(base, CONV_ROWS), :] = du
            du = du_ref[...]
            da1 = du * sg2
            da2 = du * a1 * (sg2 * (1.0 - sg2))
            da_ref[:, cs] = da1.astype(BF16)
            da_ref[:, gs] = da2.astype(BF16)
            db1_ref[:, cs] += _sum8(da1)
            db1_ref[:, gs] += _sum8(da2)
        dh = dh + dh_mm(n_c - 1)
        dx, dg_rows = _rms_bwd(dh, x_ref[...], g_ref[...])
        dxin_ref[...] = dxf + dx
        dg_ref[...] += jnp.sum(dg_rows, axis=0, keepdims=True)

    rev = lambda i: (n_i - 1 - i, 0)
    return pl.pallas_call(
        _carried(body, 11, 6, ex, n_i), grid=(n_i,),
        in_specs=[pl.BlockSpec((tt, D), rev), pl.BlockSpec((tt, D), rev), _resident((1, D)),
                  pl.BlockSpec((tt, 2 * D), rev),
                  pl.BlockSpec((CONV_HALO, 2 * D), lambda i: (jnp.maximum((n_i - 1 - i) * hb - 1, 0), 0)),
                  pl.BlockSpec((tt, D), rev), _resident((2 * D, D)), _resident((32, D)), _resident((1, D)),
                  _resident((1, D)), _resident((D, D))] + _ex(ex, "in_specs"),
        out_specs=[pl.BlockSpec((tt, D), rev), pl.BlockSpec((tt, 2 * D), rev), pl.BlockSpec((1, D), lambda i: (0, 0)),
                   pl.BlockSpec((40, 8, D), lambda i: (0, 0, 0)), pl.BlockSpec((8, 2 * D), lambda i: (0, 0)),
                   pl.BlockSpec((8, D), lambda i: (0, 0))] + _ex(ex, "out_specs"),
        out_shape=[jax.ShapeDtypeStruct((s, D), F32), jax.ShapeDtypeStruct((s, 2 * D), BF16),
                   jax.ShapeDtypeStruct((1, D), F32), jax.ShapeDtypeStruct((40, 8, D), F32),
                   jax.ShapeDtypeStruct((8, 2 * D), F32), jax.ShapeDtypeStruct((8, D), F32)] + _ex(ex, "out_shape"),
        scratch_shapes=[pltpu.VMEM((CONV_HALO, D), F32), pltpu.VMEM((te, D), F32), pltpu.VMEM((te, CONV_CHUNK), F32),
                        pltpu.VMEM((8, te, CONV_CHUNK), F32), pltpu.VMEM((8, te, CONV_CHUNK), F32),
                        pltpu.VMEM((tt, CONV_CHUNK), F32)] + _ex(ex, "scratch"),
        compiler_params=_params("arbitrary"), name="conv_bwd")(dx_out, x_in, g, a, a, u2, w1t, wdw, ln_g, ln_b, w2, *ex_args)


def _adamw(name, w, g, m, v):
    rows, cols = w.shape
    tr = _row_tile(rows, 256)

    def body(w_ref, g_ref, m_ref, v_ref, d_ref, nm_ref, nv_ref):
        gv = g_ref[...]
        m2 = ADAM_B1 * m_ref[...] + (1.0 - ADAM_B1) * gv
        v2 = ADAM_B2 * v_ref[...] + (1.0 - ADAM_B2) * (gv * gv)
        m_hat = m2 / (1.0 - ADAM_B1 ** ADAM_STEP)
        v_hat = v2 / (1.0 - ADAM_B2 ** ADAM_STEP)
        d_ref[...] = -ADAM_LR * (m_hat / (jnp.sqrt(v_hat) + ADAM_EPS) + ADAM_WD * w_ref[...])
        nm_ref[...] = m2
        nv_ref[...] = v2

    spec = pl.BlockSpec((tr, cols), lambda i: (i, 0))
    return pl.pallas_call(
        body, grid=(rows // tr,), in_specs=[spec] * 4, out_specs=[spec] * 3,
        out_shape=[jax.ShapeDtypeStruct((rows, cols), F32)] * 3,
        compiler_params=_params("parallel"), name=name)(w, g, m, v)


def _sum_slabs(name, buf):
    rows = buf.shape[1]
    tr = _row_tile(rows, 512, 16)

    def body(b_ref, o_ref):
        acc = b_ref[0].astype(F32)
        for k in range(1, N_DEV):
            acc = acc + b_ref[k].astype(F32)
        o_ref[...] = acc

    return pl.pallas_call(
        body, grid=(rows // tr,), in_specs=[pl.BlockSpec((N_DEV, tr, D), lambda i: (0, i, 0))],
        out_specs=pl.BlockSpec((tr, D), lambda i: (i, 0)), out_shape=jax.ShapeDtypeStruct((rows, D), F32),
        compiler_params=_params("parallel"), name=name)(buf)


def _ex(ex, field):
    return list(getattr(ex, field)) if ex is not None else []


def _me():
    x, y, c = lax.axis_index("x"), lax.axis_index("y"), lax.axis_index("c")
    return x, y, c, 4 * x + 2 * y + c


def _peer(x, y, c, k):
    return (x ^ (k >> 2), y ^ ((k >> 1) & 1), c ^ (k & 1))


class _Gather:
    def __init__(self, mats):
        self.mats = tuple(mats)
        self.rows = sum(MAT_ROWS[i] for i in self.mats)
        any_spec = pl.BlockSpec(memory_space=pl.ANY)
        self.n_in, self.n_out = 1, len(self.mats)
        self.in_specs = [any_spec]
        self.out_specs = [any_spec] * self.n_out
        self.out_shape = [jax.ShapeDtypeStruct((N_DEV * MAT_ROWS[i], D), BF16) for i in self.mats]
        self.scratch = [pltpu.SemaphoreType.DMA((N_DEV - 1,)), pltpu.SemaphoreType.DMA((N_DEV - 1,)),
                        pltpu.SemaphoreType.DMA]
        assert max(s.shape[0] for s in self.out_shape) >= self.rows

    def start(self, ins, outs, scr):
        (slab_ref,), (send_sems, recv_sems, local_sem) = ins, scr
        x, y, c, me = _me()
        for n, i in enumerate(self.mats):
            src = slab_ref.at[pl.ds(MAT_OFF[i], MAT_ROWS[i])]
            dst = outs[n].at[pl.ds(me * MAT_ROWS[i], MAT_ROWS[i])]
            pltpu.make_async_copy(src, dst, local_sem).start()
            for k in range(1, N_DEV):
                pltpu.make_async_remote_copy(src_ref=src, dst_ref=dst, send_sem=send_sems.at[k - 1],
                                             recv_sem=recv_sems.at[k - 1], device_id=_peer(x, y, c, k),
                                             device_id_type=MESH).start()

    def wait(self, ins, outs, scr):
        send_sems, recv_sems, local_sem = scr
        x, y, c, _ = _me()
        big = max(range(self.n_out), key=lambda n: self.out_shape[n].shape[0])
        whole = outs[big].at[pl.ds(0, self.rows)]
        for k in range(1, N_DEV):
            pltpu.make_async_remote_copy(src_ref=whole, dst_ref=whole, send_sem=send_sems.at[k - 1],
                                         recv_sem=recv_sems.at[k - 1], device_id=_peer(x, y, c, k),
                                         device_id_type=MESH).wait()
        pltpu.make_async_copy(whole, whole, local_sem).wait()


class _Scatter:
    def __init__(self, mats):
        self.mats = tuple(mats)
        self.rows = sum(MAT_ROWS[i] for i in self.mats)
        any_spec = pl.BlockSpec(memory_space=pl.ANY)
        self.n_in, self.n_out = len(self.mats), 1
        self.in_specs = [any_spec] * self.n_in
        self.out_specs = [any_spec]
        self.out_shape = [jax.ShapeDtypeStruct((N_DEV, self.rows, D), BF16)]
        self.scratch = [pltpu.SemaphoreType.DMA((N_DEV - 1,)), pltpu.SemaphoreType.DMA((N_DEV - 1,)),
                        pltpu.SemaphoreType.DMA]

    def offsets(self):
        return [sum(MAT_ROWS[i] for i in self.mats[:n]) for n in range(len(self.mats))]

    def start(self, ins, outs, scr):
        (buf_ref,), (send_sems, recv_sems, local_sem) = outs, scr
        x, y, c, me = _me()
        for n, (i, off) in enumerate(zip(self.mats, self.offsets())):
            r = MAT_ROWS[i]
            pltpu.make_async_copy(ins[n].at[pl.ds(me * r, r)], buf_ref.at[0, pl.ds(off, r)], local_sem).start()
            for k in range(1, N_DEV):
                pltpu.make_async_remote_copy(src_ref=ins[n].at[pl.ds((me ^ k) * r, r)], dst_ref=buf_ref.at[k, pl.ds(off, r)],
                                             send_sem=send_sems.at[k - 1], recv_sem=recv_sems.at[k - 1],
                                             device_id=_peer(x, y, c, k), device_id_type=MESH).start()

    def wait(self, ins, outs, scr):
        (buf_ref,), (send_sems, recv_sems, local_sem) = outs, scr
        x, y, c, _ = _me()
        whole = buf_ref.at[0]
        for k in range(1, N_DEV):
            pltpu.make_async_remote_copy(src_ref=whole, dst_ref=whole, send_sem=send_sems.at[k - 1],
                                         recv_sem=recv_sems.at[k - 1], device_id=_peer(x, y, c, k),
                                         device_id_type=MESH).wait()
        pltpu.make_async_copy(whole, whole, local_sem).wait()


def _carried(body, n_in, n_out, ex, n_steps):
    if ex is None:
        return body

    def wrapped(*refs):
        ins, ex_ins = refs[:n_in], refs[n_in:n_in + ex.n_in]
        p = n_in + ex.n_in
        outs, ex_outs = refs[p:p + n_out], refs[p + n_out:p + n_out + ex.n_out]
        p += n_out + ex.n_out
        n_scr = len(refs) - p - len(ex.scratch)
        scr, ex_scr = refs[p:p + n_scr], refs[p + n_scr:]

        @pl.when(pl.program_id(0) == 0)
        def _():
            ex.start(ex_ins, ex_outs, ex_scr)

        body(*ins, *outs, *scr)

        @pl.when(pl.program_id(0) == n_steps - 1)
        def _():
            ex.wait(ex_ins, ex_outs, ex_scr)

    return wrapped


def _exchange_small(name, ex, ex_args, small, reduce):
    vmem_spec = pl.BlockSpec(memory_space=pltpu.VMEM)
    n_small = small.shape[0]

    def body(*refs):
        ex_ins, small_ref = refs[:ex.n_in], refs[ex.n_in]
        p = ex.n_in + 1
        ex_outs, res_ref = refs[p:p + ex.n_out], refs[p + ex.n_out]
        p += ex.n_out + 1
        if reduce:
            all_ref, p = refs[p], p + 1
        else:
            all_ref = res_ref
        sm_send, sm_recv = refs[p], refs[p + 1]
        ex_scr = refs[p + 2:]
        x, y, c, me = _me()
        ex.start(ex_ins, ex_outs, ex_scr)
        copies = [pltpu.make_async_remote_copy(
            src_ref=small_ref, dst_ref=all_ref.at[me], send_sem=sm_send.at[k - 1], recv_sem=sm_recv.at[k - 1],
            device_id=_peer(x, y, c, k), device_id_type=MESH) for k in range(1, N_DEV)]
        for cp in copies:
            cp.start()
        all_ref[me] = small_ref[...]
        for cp in copies:
            cp.wait()
        if reduce:
            acc = all_ref[0]
            for d in range(1, N_DEV):
                acc = acc + all_ref[d]
            res_ref[...] = acc
        ex.wait(ex_ins, ex_outs, ex_scr)

    res_shape = (n_small, D) if reduce else (N_DEV, n_small, D)
    scratch = ([pltpu.VMEM((N_DEV, n_small, D), F32)] if reduce else []) + [
        pltpu.SemaphoreType.DMA((N_DEV - 1,)), pltpu.SemaphoreType.DMA((N_DEV - 1,))] + ex.scratch
    outs = pl.pallas_call(
        body, in_specs=ex.in_specs + [vmem_spec], out_specs=ex.out_specs + [vmem_spec],
        out_shape=ex.out_shape + [jax.ShapeDtypeStruct(res_shape, F32)], scratch_shapes=scratch,
        compiler_params=pltpu.CompilerParams(vmem_limit_bytes=VMEM_LIMIT_BYTES), name=name)(*ex_args, small)
    return outs


def _pack_rows(arrs):
    rows = []
    for a in arrs:
        flat = a.reshape(-1).astype(F32)
        pad = (-flat.shape[0]) % D
        rows.append(jnp.pad(flat, (0, pad)).reshape(-1, D))
    slab = jnp.concatenate(rows, axis=0)
    return jnp.pad(slab, ((0, (-slab.shape[0]) % 8), (0, 0)))


def _unpack_rows(slab, shapes):
    out, r = [], 0
    for shp in shapes:
        size = math.prod(shp)
        nrows = -(-size // D)
        out.append(slab[r:r + nrows].reshape(-1)[:size].reshape(shp))
        r += nrows
    return out


def kernel(x, norm_mix, attn_w_qkv, attn_b_qkv, attn_sinks, attn_w_o, attn_b_o, conv_w_pw1, conv_b_pw1, conv_w_dw, conv_b_dw, conv_ln_g, conv_ln_b, conv_w_pw2, conv_b_pw2, norm_ffn, ffn_w_up, ffn_w_dw, ffn_b_dw, ffn_w_down, final_norm, loss_target, m_norm_mix, m_attn_w_qkv, m_attn_b_qkv, m_attn_sinks, m_attn_w_o, m_attn_b_o, m_conv_w_pw1, m_conv_b_pw1, m_conv_w_dw, m_conv_b_dw, m_conv_ln_g, m_conv_ln_b, m_conv_w_pw2, m_conv_b_pw2, m_norm_ffn, m_ffn_w_up, m_ffn_w_dw, m_ffn_b_dw, m_ffn_w_down, m_final_norm, v_norm_mix, v_attn_w_qkv, v_attn_b_qkv, v_attn_sinks, v_attn_w_o, v_attn_b_o, v_conv_w_pw1, v_conv_b_pw1, v_conv_w_dw, v_conv_b_dw, v_conv_ln_g, v_conv_ln_b, v_conv_w_pw2, v_conv_b_pw2, v_norm_ffn, v_ffn_w_up, v_ffn_w_dw, v_ffn_b_dw, v_ffn_w_down, v_final_norm):
    s = x.shape[1]
    me = 4 * lax.axis_index("x") + 2 * lax.axis_index("y") + lax.axis_index("c")
    x0 = x.reshape(s, D)
    tgt = loss_target.reshape(s, D)

    slab = jnp.concatenate([attn_w_qkv[0].T, attn_w_o[0], conv_w_pw1[0].T, conv_w_pw2[0],
                            ffn_w_up[0].T, ffn_w_down[0], ffn_w_up[1].T, ffn_w_down[1]], axis=0).astype(BF16)
    sharded_small = [conv_b_pw1, conv_w_dw, conv_b_dw, conv_ln_g, conv_ln_b, conv_b_pw2, ffn_w_dw]
    small_local = _pack_rows(sharded_small)
    w_qkv_t, small_all = _exchange_small("gather_first", _Gather((0,)), (slab,), small_local, reduce=False)
    parts = [_unpack_rows(small_all[d], [a.shape for a in sharded_small]) for d in range(N_DEV)]
    b_pw1, w_cdw, b_cdw, ln_g, ln_b, b_pw2, w_fdw = [jnp.concatenate([parts[d][i] for d in range(N_DEV)], axis=-1)
                                                      for i in range(len(sharded_small))]
    conv_w32 = jnp.concatenate([w_cdw[0], b_cdw], axis=0)
    sr = attn_sinks.reshape(N_KV, 4, 2)
    sink_cols = jnp.broadcast_to(jnp.repeat(jnp.transpose(sr, (0, 2, 1)), BLOCK, axis=-1).reshape(4, 512, 1),
                                 (4, 512, 128))

    qkv, h0 = _mm("qkv", x0, w_qkv_t, nt=True, tm=512, tn=NQ + KVW, out_dtype=BF16, rms_g=norm_mix[0:1], bias=attn_b_qkv)
    o, w_o, w_up0_t, w_dn0 = _attn_fwd(qkv, sink_cols, _Gather((1, 4, 5)), (slab,))
    x1 = _mm("attn_out", o, w_o, nt=False, tm=512, tn=D, out_dtype=F32, bias=attn_b_o, res=x0)
    x2, h1, up0, act0, gate0, w_pw1_t, w_pw2, w_up1_t, w_dn1 = _ffn_fwd(0, x1, norm_ffn[0:1], w_up0_t, w_fdw[0], ffn_b_dw[0:1], w_dn0,
                                                             tm=256, ex=_Gather((2, 3, 6, 7)), ex_args=(slab,))
    x3, h2, a, u2, s_act = _conv_fwd(x2, norm_mix[1:2], w_pw1_t, b_pw1, conv_w32, ln_g, ln_b, w_pw2, b_pw2, tt=256)
    x4, h3, up1, act1, gate1 = _ffn_fwd(1, x3, norm_ffn[1:2], w_up1_t, w_fdw[1], ffn_b_dw[1:2], w_dn1, tm=256)
    dx4, loss_acc, dg_final = _loss_head(x4, final_norm.reshape(1, D), tgt, tm=512)
    loss = lax.psum(loss_acc[0, 0], ("x", "y", "c"))

    dx3, d_up1, dg_f1, dwb1 = _ffn_bwd(1, dx4, x3, norm_ffn[1:2], w_up1_t, w_fdw[1], w_dn1, up1, gate1, tm=256)
    dw_dn1 = _mm_tn("ffn1_dwdown", act1, dx4, tn=FFN_HALF, tt=512)
    dw_up1_t = _mm_tn("ffn1_dwup", d_up1, h3, tn=FFN_HALF, tt=512)
    dw_pw2 = _mm_tn("dw_pw2", s_act, dx3, tn=D, tt=512)
    scatter_a = _Scatter((6, 7))
    dx2, da, dg_m1, conv_acc8, db_pw1_8, db_pw2_8, buf_a = _conv_bwd(
        dx3, x2, norm_mix[1:2], a, u2, w_pw1_t, conv_w32, ln_g, ln_b, w_pw2, tt=256, ex=scatter_a, ex_args=(dw_up1_t, dw_dn1))
    conv_acc = jnp.sum(conv_acc8, axis=1)
    db_pw1 = jnp.sum(db_pw1_8, axis=0, keepdims=True)
    db_pw2 = jnp.sum(db_pw2_8, axis=0, keepdims=True)
    dw_pw1_t = _mm_tn("dw_pw1", da, h2, tn=D, tt=512)
    dx1, d_up0, dg_f0, dwb0 = _ffn_bwd(0, dx2, x1, norm_ffn[0:1], w_up0_t, w_fdw[0], w_dn0, up0, gate0, tm=256)
    dw_dn0 = _mm_tn("ffn0_dwdown", act0, dx2, tn=FFN_HALF, tt=512)
    dw_up0_t = _mm_tn("ffn0_dwup", d_up0, h1, tn=FFN_HALF, tt=512)
    do = _mm("d_attn_out", dx1, w_o, nt=True, tm=512, tn=D, out_dtype=BF16)
    dw_o, db_o = _mm_tn("dw_o", o, dx1, tn=D, tt=512, colsum_r=True)
    scatter_b = _Scatter((1, 2, 3, 4, 5))
    dq, dkv, dsk, buf_b = _attn_bwd(qkv, do, sink_cols, scatter_b, (dw_o, dw_pw1_t, dw_pw2, dw_up0_t, dw_dn0))
    dqkv = jnp.concatenate([dq, dkv], axis=1)
    grad_x, dg_m0 = _dgrad_rms("d_qkv", dqkv, w_qkv_t, x0, norm_mix[0:1], dx1, tm=512, tk=NQ + KVW)
    dw_qkv_t, db_qkv = _mm_tn("dw_qkv", dqkv, h0, tn=NQ + KVW, tt=512, colsum_l=True)

    dwb0, dwb1 = jnp.sum(dwb0, axis=1), jnp.sum(dwb1, axis=1)
    d_sinks = jnp.transpose(jnp.sum(dsk.reshape(N_KV, 2, 4, BLOCK), axis=-1), (0, 2, 1)).reshape(1, 16)
    small_grads = [jnp.concatenate([dg_m0, dg_m1], axis=0), db_qkv, d_sinks, db_o, jnp.concatenate([dg_f0, dg_f1], axis=0),
                   jnp.stack([dwb0[3], dwb1[3]]), dg_final, db_pw1, conv_acc[0:CONV_K], conv_acc[31:32], conv_acc[32:33],
                   conv_acc[33:34], db_pw2, jnp.stack([dwb0[0:3], dwb1[0:3]])]
    small_shapes = [(2, D), (1, NQ + KVW), (1, 16), (1, D), (2, D), (2, F), (D,), (1, 2 * D), (1, CONV_K, D), (1, D),
                    (1, D), (1, D), (1, D), (2, 3, F)]
    scatter_c = _Scatter((0,))
    buf_c, small_sum = _exchange_small("scatter_last", scatter_c, (dw_qkv_t,), _pack_rows(small_grads), reduce=True)
    gm = [None] * N_MAT
    for tag, sc, buf in (("a", scatter_a, buf_a), ("b", scatter_b, buf_b), ("c", scatter_c, buf_c)):
        gslab = _sum_slabs("sum_slabs_" + tag, buf)
        for i, off in zip(sc.mats, sc.offsets()):
            gm[i] = gslab[off:off + MAT_ROWS[i]]
    (g_norm_mix, g_b_qkv, g_sinks, g_b_o, g_norm_ffn, g_ffn_b_dw, g_final, g_b_pw1_full, g_w_cdw_full, g_b_cdw_full,
     g_ln_g_full, g_ln_b_full, g_b_pw2_full, g_w_fdw_full) = _unpack_rows(small_sum, small_shapes)

    def shard(a, width):
        return lax.dynamic_slice_in_dim(a, me * width, width, axis=a.ndim - 1)

    grads = {
        "norm_mix": g_norm_mix, "attn_w_qkv": gm[0].T[None], "attn_b_qkv": g_b_qkv, "attn_sinks": g_sinks,
        "attn_w_o": gm[1][None], "attn_b_o": g_b_o, "conv_w_pw1": gm[2].T[None], "conv_b_pw1": shard(g_b_pw1_full, 256),
        "conv_w_dw": shard(g_w_cdw_full, 128), "conv_b_dw": shard(g_b_cdw_full, 128), "conv_ln_g": shard(g_ln_g_full, 128),
        "conv_ln_b": shard(g_ln_b_full, 128), "conv_w_pw2": gm[3][None], "conv_b_pw2": shard(g_b_pw2_full, 128),
        "norm_ffn": g_norm_ffn, "ffn_w_up": jnp.stack([gm[4].T, gm[6].T]), "ffn_w_dw": shard(g_w_fdw_full, 352),
        "ffn_b_dw": g_ffn_b_dw, "ffn_w_down": jnp.stack([gm[5], gm[7]]), "final_norm": g_final,
    }
    weights = dict(norm_mix=norm_mix, attn_w_qkv=attn_w_qkv, attn_b_qkv=attn_b_qkv, attn_sinks=attn_sinks, attn_w_o=attn_w_o, attn_b_o=attn_b_o, conv_w_pw1=conv_w_pw1, conv_b_pw1=conv_b_pw1, conv_w_dw=conv_w_dw, conv_b_dw=conv_b_dw, conv_ln_g=conv_ln_g, conv_ln_b=conv_ln_b, conv_w_pw2=conv_w_pw2, conv_b_pw2=conv_b_pw2, norm_ffn=norm_ffn, ffn_w_up=ffn_w_up, ffn_w_dw=ffn_w_dw, ffn_b_dw=ffn_b_dw, ffn_w_down=ffn_w_down, final_norm=final_norm)
    moms = dict(norm_mix=m_norm_mix, attn_w_qkv=m_attn_w_qkv, attn_b_qkv=m_attn_b_qkv, attn_sinks=m_attn_sinks, attn_w_o=m_attn_w_o, attn_b_o=m_attn_b_o, conv_w_pw1=m_conv_w_pw1, conv_b_pw1=m_conv_b_pw1, conv_w_dw=m_conv_w_dw, conv_b_dw=m_conv_b_dw, conv_ln_g=m_conv_ln_g, conv_ln_b=m_conv_ln_b, conv_w_pw2=m_conv_w_pw2, conv_b_pw2=m_conv_b_pw2, norm_ffn=m_norm_ffn, ffn_w_up=m_ffn_w_up, ffn_w_dw=m_ffn_w_dw, ffn_b_dw=m_ffn_b_dw, ffn_w_down=m_ffn_w_down, final_norm=m_final_norm)
    vars_ = dict(norm_mix=v_norm_mix, attn_w_qkv=v_attn_w_qkv, attn_b_qkv=v_attn_b_qkv, attn_sinks=v_attn_sinks, attn_w_o=v_attn_w_o, attn_b_o=v_attn_b_o, conv_w_pw1=v_conv_w_pw1, conv_b_pw1=v_conv_b_pw1, conv_w_dw=v_conv_w_dw, conv_b_dw=v_conv_b_dw, conv_ln_g=v_conv_ln_g, conv_ln_b=v_conv_ln_b, conv_w_pw2=v_conv_w_pw2, conv_b_pw2=v_conv_b_pw2, norm_ffn=v_norm_ffn, ffn_w_up=v_ffn_w_up, ffn_w_dw=v_ffn_w_dw, ffn_b_dw=v_ffn_b_dw, ffn_w_down=v_ffn_w_down, final_norm=v_final_norm)
    names = list(weights)
    big = ("attn_w_qkv", "attn_w_o", "conv_w_pw1", "conv_w_pw2", "ffn_w_up", "ffn_w_down")
    delta, new_m, new_v = {}, {}, {}
    for nm in big:
        shp = weights[nm].shape
        two_d = (shp[0] * shp[1], shp[2])
        res = _adamw("adamw_" + nm, *(t[nm].reshape(two_d) for t in (weights, grads, moms, vars_)))
        delta[nm], new_m[nm], new_v[nm] = (r.reshape(shp) for r in res)
    small_names = [nm for nm in names if nm not in big]
    small_shapes_local = [weights[nm].shape for nm in small_names]
    res = _adamw("adamw_small", *(_pack_rows([t[nm] for nm in small_names]) for t in (weights, grads, moms, vars_)))
    for tgt_dict, slab_out in zip((delta, new_m, new_v), res):
        for nm, val in zip(small_names, _unpack_rows(slab_out, small_shapes_local)):
            tgt_dict[nm] = val
    return (loss, grad_x.reshape(1, s, D), *[grads[nm] for nm in names], *[delta[nm] for nm in names],
            *[new_m[nm] for nm in names], *[new_v[nm] for nm in names])
```

```python
import functools
import math

import jax
import jax.numpy as jnp
from jax import lax
from jax.experimental import pallas as pl
from jax.experimental.pallas import tpu as pltpu

F32 = jnp.float32
BF16 = jnp.bfloat16

D = 1024
F = 2816
HEAD_DIM = 64
N_KV = 2
BLOCK = 128
NQ = 1024
KVW = 256
CONV_K = 31
RMS_EPS = 1e-6
LN_EPS = 1e-5
ADAM_LR, ADAM_B1, ADAM_B2, ADAM_EPS, ADAM_WD, ADAM_STEP = 0.001, 0.9, 0.999, 1e-08, 0.01, 10
N_DEV = 8
MESH = pl.DeviceIdType.MESH
VMEM_LIMIT_BYTES = 56 * 2**20
NEG = float(jnp.finfo(jnp.float32).min)

MAT_ROWS = (160, 128, 256, 128, 704, 352, 704, 352)
MAT_OFF = tuple(sum(MAT_ROWS[:i]) for i in range(len(MAT_ROWS)))
SLAB_ROWS = sum(MAT_ROWS)
N_MAT = len(MAT_ROWS)

NT_DIMS = (((1,), (1,)), ((), ()))
NN_DIMS = (((1,), (0,)), ((), ()))
TN_DIMS = (((0,), (0,)), ((), ()))


def _params(*sem):
    return pltpu.CompilerParams(dimension_semantics=tuple(sem), vmem_limit_bytes=VMEM_LIMIT_BYTES)


def _row_tile(rows, cap, mult=8):
    best = None
    for t in range(mult, min(rows, cap) + 1, mult):
        if rows % t == 0:
            best = t
    return best if best is not None else rows


def _sigmoid(x):
    return 1.0 / (1.0 + jnp.exp(-x))


def _mm(name, a, b, *, nt, tm, tn, out_dtype, rms_g=None, bias=None, res=None):
    m, k = a.shape
    n = b.shape[0] if nt else b.shape[1]
    tm = min(tm, m)
    has_rms = rms_g is not None
    dims = NT_DIMS if nt else NN_DIMS

    def body(*refs):
        a_ref, b_ref = refs[0], refs[1]
        pos = 2
        g_ref = bias_ref = res_ref = h_ref = hs_ref = None
        if has_rms:
            g_ref = refs[pos]; pos += 1
        if bias is not None:
            bias_ref = refs[pos]; pos += 1
        if res is not None:
            res_ref = refs[pos]; pos += 1
        o_ref = refs[pos]; pos += 1
        if has_rms:
            h_ref, hs_ref = refs[pos], refs[pos + 1]

            @pl.when(pl.program_id(1) == 0)
            def _():
                xf = a_ref[...]
                r = lax.rsqrt(jnp.mean(xf * xf, axis=-1, keepdims=True) + RMS_EPS)
                h = ((xf * r) * g_ref[...]).astype(BF16)
                hs_ref[...] = h
                h_ref[...] = h

            lhs = hs_ref[...]
        else:
            lhs = a_ref[...].astype(BF16)
        acc = lax.dot_general(lhs, b_ref[...], dims, preferred_element_type=F32)
        if bias is not None:
            acc = acc + bias_ref[...]
        if res is not None:
            acc = acc + res_ref[...]
        o_ref[...] = acc.astype(out_dtype)

    in_specs = [pl.BlockSpec((tm, k), lambda i, j: (i, 0)),
                pl.BlockSpec((tn, k), lambda i, j: (j, 0)) if nt else pl.BlockSpec((k, tn), lambda i, j: (0, j))]
    args = [a, b]
    if has_rms:
        in_specs.append(pl.BlockSpec((1, k), lambda i, j: (0, 0))); args.append(rms_g)
    if bias is not None:
        in_specs.append(pl.BlockSpec((1, tn), lambda i, j: (0, j))); args.append(bias)
    if res is not None:
        in_specs.append(pl.BlockSpec((tm, tn), lambda i, j: (i, j))); args.append(res)
    out_shape = [jax.ShapeDtypeStruct((m, n), out_dtype)]
    out_specs = [pl.BlockSpec((tm, tn), lambda i, j: (i, j))]
    scratch = []
    if has_rms:
        out_shape.append(jax.ShapeDtypeStruct((m, k), BF16))
        out_specs.append(pl.BlockSpec((tm, k), lambda i, j: (i, 0)))
        scratch.append(pltpu.VMEM((tm, k), BF16))
    outs = pl.pallas_call(body, grid=(m // tm, n // tn), in_specs=in_specs, out_specs=out_specs, out_shape=out_shape,
                          scratch_shapes=scratch, compiler_params=_params("parallel", "arbitrary"), name=name)(*args)
    return outs if has_rms else outs[0]


def _mm_tn(name, l, r, *, tn, tt, colsum_l=False, colsum_r=False):
    s, nl = l.shape
    nr = r.shape[1]
    tt = min(tt, s)
    n_t = s // tt

    def body(*refs):
        l_ref, r_ref, o_ref = refs[0], refs[1], refs[2]
        pos = 3
        cl_ref = cr_ref = None
        if colsum_l:
            cl_ref = refs[pos]; pos += 1
        if colsum_r:
            cr_ref = refs[pos]; pos += 1
        acc_ref = refs[pos]
        j, t = pl.program_id(0), pl.program_id(1)

        @pl.when(t == 0)
        def _():
            acc_ref[...] = jnp.zeros_like(acc_ref)

        lv = l_ref[...]
        rv = r_ref[...]
        acc_ref[...] += lax.dot_general(lv, rv.astype(BF16), TN_DIMS, preferred_element_type=F32)
        if colsum_l:
            @pl.when(t == 0)
            def _():
                cl_ref[...] = jnp.zeros_like(cl_ref)

            cl_ref[...] += jnp.sum(lv.astype(F32), axis=0, keepdims=True)
        if colsum_r:
            @pl.when((t == 0) & (j == 0))
            def _():
                cr_ref[...] = jnp.zeros_like(cr_ref)

            @pl.when(j == 0)
            def _():
                cr_ref[...] += jnp.sum(rv.astype(F32), axis=0, keepdims=True)

        @pl.when(t == n_t - 1)
        def _():
            o_ref[...] = acc_ref[...].astype(BF16)

    out_shape = [jax.ShapeDtypeStruct((nl, nr), BF16)]
    out_specs = [pl.BlockSpec((tn, nr), lambda j, t: (j, 0))]
    if colsum_l:
        out_shape.append(jax.ShapeDtypeStruct((1, nl), F32))
        out_specs.append(pl.BlockSpec((1, tn), lambda j, t: (0, j)))
    if colsum_r:
        out_shape.append(jax.ShapeDtypeStruct((1, nr), F32))
        out_specs.append(pl.BlockSpec((1, nr), lambda j, t: (0, 0)))
    outs = pl.pallas_call(
        body, grid=(nl // tn, n_t),
        in_specs=[pl.BlockSpec((tt, tn), lambda j, t: (t, j)), pl.BlockSpec((tt, nr), lambda j, t: (t, 0))],
        out_specs=out_specs, out_shape=out_shape, scratch_shapes=[pltpu.VMEM((tn, nr), F32)],
        compiler_params=_params("arbitrary", "arbitrary"), name=name)(l, r)
    return outs if (colsum_l or colsum_r) else outs[0]


def _rms_bwd(dh, xf, g):
    r = lax.rsqrt(jnp.mean(xf * xf, axis=-1, keepdims=True) + RMS_EPS)
    gd = dh * g
    dx = r * gd - xf * ((r * r * r) * jnp.mean(xf * gd, axis=-1, keepdims=True))
    return dx, dh * (xf * r)


def _dgrad_rms(name, dy, w, x, g, dres, *, tm, tk):
    m, k = dy.shape
    tm = min(tm, m)
    n_k = k // tk

    def body(dy_ref, w_ref, x_ref, g_ref, dres_ref, o_ref, dg_ref, acc_ref):
        i, kk = pl.program_id(0), pl.program_id(1)

        @pl.when(kk == 0)
        def _():
            acc_ref[...] = jnp.zeros_like(acc_ref)

        acc_ref[...] += lax.dot_general(dy_ref[...], w_ref[...], NN_DIMS, preferred_element_type=F32)

        @pl.when((i == 0) & (kk == n_k - 1))
        def _():
            dg_ref[...] = jnp.zeros_like(dg_ref)

        @pl.when(kk == n_k - 1)
        def _():
            dx, dg_rows = _rms_bwd(acc_ref[...], x_ref[...], g_ref[...])
            o_ref[...] = dres_ref[...] + dx
            dg_ref[...] += jnp.sum(dg_rows, axis=0, keepdims=True)

    return pl.pallas_call(
        body, grid=(m // tm, n_k),
        in_specs=[pl.BlockSpec((tm, tk), lambda i, kk: (i, kk)), pl.BlockSpec((tk, D), lambda i, kk: (kk, 0)),
                  pl.BlockSpec((tm, D), lambda i, kk: (i, 0)), pl.BlockSpec((1, D), lambda i, kk: (0, 0)),
                  pl.BlockSpec((tm, D), lambda i, kk: (i, 0))],
        out_specs=[pl.BlockSpec((tm, D), lambda i, kk: (i, 0)), pl.BlockSpec((1, D), lambda i, kk: (0, 0))],
        out_shape=[jax.ShapeDtypeStruct((m, D), F32), jax.ShapeDtypeStruct((1, D), F32)],
        scratch_shapes=[pltpu.VMEM((tm, D), F32)],
        compiler_params=_params("arbitrary", "arbitrary"), name=name)(dy, w, x, g, dres)


def _loss_head(x4, g, tgt, *, tm):
    m = x4.shape[0]
    tm = min(tm, m)

    def body(x_ref, g_ref, t_ref, dx_ref, loss_ref, dg_ref):
        @pl.when(pl.program_id(0) == 0)
        def _():
            loss_ref[...] = jnp.zeros_like(loss_ref)
            dg_ref[...] = jnp.zeros_like(dg_ref)

        xf = x_ref[...]
        gv = g_ref[...]
        r = lax.rsqrt(jnp.mean(xf * xf, axis=-1, keepdims=True) + RMS_EPS)
        diff = (xf * r) * gv - t_ref[...]
        loss_ref[...] += 0.5 * jnp.sum(jnp.mean(diff * diff, axis=-1, keepdims=True))
        dx, dg_rows = _rms_bwd(diff * (1.0 / D), xf, gv)
        dx_ref[...] = dx
        dg_ref[...] += jnp.sum(dg_rows, axis=0, keepdims=True)

    return pl.pallas_call(
        body, grid=(m // tm,),
        in_specs=[pl.BlockSpec((tm, D), lambda i: (i, 0)), pl.BlockSpec((1, D), lambda i: (0, 0)),
                  pl.BlockSpec((tm, D), lambda i: (i, 0))],
        out_specs=[pl.BlockSpec((tm, D), lambda i: (i, 0)), pl.BlockSpec((1, 128), lambda i: (0, 0)),
                   pl.BlockSpec((1, D), lambda i: (0, 0))],
        out_shape=[jax.ShapeDtypeStruct((m, D), F32), jax.ShapeDtypeStruct((1, 128), F32),
                   jax.ShapeDtypeStruct((1, D), F32)],
        compiler_params=_params("arbitrary"), name="loss_head")(x4, g, tgt)


def _band(kvp, kvc):
    kk = jnp.concatenate([kvp[:, :128], kvc[:, :128]], axis=0)
    vv = jnp.concatenate([kvp[:, 128:], kvc[:, 128:]], axis=0)
    row = lax.broadcasted_iota(jnp.int32, kk.shape, 0)
    return jnp.where(row == 0, jnp.zeros_like(kk), kk), jnp.where(row == 0, jnp.zeros_like(vv), vv)


def _blockdiag(t, h):
    lane = lax.broadcasted_iota(jnp.int32, t.shape, 1)
    z = jnp.zeros_like(t)
    tr = pltpu.roll(t, 64, 1)
    if h == 0:
        top, bot = jnp.where(lane < 64, t, z), jnp.where(lane >= 64, tr, z)
    else:
        top, bot = jnp.where(lane < 64, tr, z), jnp.where(lane >= 64, t, z)
    return jnp.concatenate([top, bot], axis=0)


def _from_blockdiag(t, h):
    lane = lax.broadcasted_iota(jnp.int32, (256, 128), 1)
    top, bot = t[:256], t[256:]
    if h == 0:
        return jnp.where(lane < 64, top + pltpu.roll(bot, 64, 1), 0.0)
    return jnp.where(lane >= 64, pltpu.roll(top, 64, 1) + bot, 0.0)


def _stack_heads(ref, h):
    return jnp.concatenate([ref[:, h * 512 + p * 128: h * 512 + (p + 1) * 128] for p in range(4)], axis=0)


def _masked_scores(sc_half, n, sink):
    row = lax.broadcasted_iota(jnp.int32, (512, 256), 0) & 127
    col = lax.broadcasted_iota(jnp.int32, (512, 256), 1)
    valid = (col > row) & (col <= row + BLOCK) & ((col >= BLOCK) | (n > 0))
    return jnp.where(col == 0, sink, jnp.where(valid, sc_half, NEG))


def _half_selector(shape, split):
    row = lax.broadcasted_iota(jnp.int32, shape, 0)
    one, zero = jnp.ones(shape, BF16), jnp.zeros(shape, BF16)
    return jnp.where(row < split, one, zero), jnp.where(row >= split, one, zero)


def _attn_fwd(qkv, sink_full, ex=None, ex_args=()):
    s = qkv.shape[0]
    nb = s // BLOCK
    scale = 1.0 / math.sqrt(HEAD_DIM)

    def body(q_ref, kvc_ref, kvp_ref, sk_ref, o_ref, lse_ref):
        n = pl.program_id(0)
        kk, vv = _band(kvp_ref[...], kvc_ref[...])
        sel0, sel1 = _half_selector((512, 128), 256)
        lane = lax.broadcasted_iota(jnp.int32, (512, 128), 1)
        for h in range(N_KV):
            kbd, vbd = _blockdiag(kk, h), _blockdiag(vv, h)
            sc = lax.dot_general(_stack_heads(q_ref, h), kbd, NT_DIMS, preferred_element_type=F32) * scale
            pes, mxs = [], []
            for half in range(2):
                sh = _masked_scores(sc[:, half * 256:(half + 1) * 256], n, sk_ref[h * 2 + half])
                mx = jnp.max(sh, axis=-1, keepdims=True)
                pes.append(jnp.exp(sh - mx).astype(BF16))
                mxs.append(mx)
            pb = jnp.concatenate(pes, axis=1)
            o_un = lax.dot_general(pb, vbd, NN_DIMS, preferred_element_type=F32)
            sums = [lax.dot_general(pb, sel, NN_DIMS, preferred_element_type=F32) for sel in (sel0, sel1)]
            o = o_un * jnp.where(lane < 64, 1.0 / sums[0], 1.0 / sums[1])
            for half in range(2):
                lse_ref[h * 2 + half] = mxs[half] + jnp.log(sums[half])
            for p in range(4):
                o_ref[:, h * 512 + p * 128: h * 512 + (p + 1) * 128] = o[p * 128:(p + 1) * 128].astype(BF16)

    return pl.pallas_call(
        _carried(body, 4, 2, ex, nb), grid=(nb,),
        in_specs=[pl.BlockSpec((BLOCK, NQ), lambda n: (n, 0)),
                  pl.BlockSpec((BLOCK, KVW), lambda n: (n, NQ // KVW)),
                  pl.BlockSpec((BLOCK, KVW), lambda n: (jnp.maximum(n - 1, 0), NQ // KVW)),
                  pl.BlockSpec((4, 512, 256), lambda n: (0, 0, 0))] + _ex(ex, "in_specs"),
        out_specs=[pl.BlockSpec((BLOCK, NQ), lambda n: (n, 0)),
                   pl.BlockSpec((None, 4, 512, 128), lambda n: (n, 0, 0, 0))] + _ex(ex, "out_specs"),
        out_shape=[jax.ShapeDtypeStruct((s, NQ), BF16), jax.ShapeDtypeStruct((nb, 4, 512, 128), F32)] + _ex(ex, "out_shape"),
        scratch_shapes=_ex(ex, "scratch"),
        compiler_params=_params("arbitrary"), name="attn_fwd")(qkv, qkv, qkv, sink_full, *ex_args)


def _attn_bwd(qkv, do, o, lse, sink_full, ex=None, ex_args=()):
    s = qkv.shape[0]
    nb = s // BLOCK
    scale = 1.0 / math.sqrt(HEAD_DIM)

    def body(q_ref, kvc_ref, kvp_ref, do_ref, o_ref, lse_ref, sk_ref, dq_ref, dkv_ref, dsk_ref, ck_ref, cv_ref):
        n = pl.program_id(0)

        @pl.when(n == 0)
        def _():
            dsk_ref[...] = jnp.zeros_like(dsk_ref)
            ck_ref[...] = jnp.zeros_like(ck_ref)
            cv_ref[...] = jnp.zeros_like(cv_ref)

        @pl.when(n < nb)
        def _():
            kk, vv = _band(kvp_ref[...], kvc_ref[...])
            sel0, sel1 = _half_selector((128, 128), 64)
            dkk = jnp.zeros((256, 128), F32)
            dvv = jnp.zeros((256, 128), F32)
            for h in range(N_KV):
                kbd, vbd = _blockdiag(kk, h), _blockdiag(vv, h)
                qp = _stack_heads(q_ref, h)
                dop = _stack_heads(do_ref, h)
                sc = lax.dot_general(qp, kbd, NT_DIMS, preferred_element_type=F32) * scale
                dp = lax.dot_general(dop, vbd, NT_DIMS, preferred_element_type=F32)
                dd = (dop.astype(F32) * _stack_heads(o_ref, h).astype(F32)).astype(BF16)
                probs, dss = [], []
                for half, sel in enumerate((sel0, sel1)):
                    delta = lax.dot_general(dd, sel, NN_DIMS, preferred_element_type=F32)
                    lse_h = lse_ref[h * 2 + half]
                    sh = _masked_scores(sc[:, half * 256:(half + 1) * 256], n, sk_ref[h * 2 + half])
                    pr = jnp.exp(sh - jnp.concatenate([lse_h, lse_h], axis=1))
                    dsf = pr * (dp[:, half * 256:(half + 1) * 256] - jnp.concatenate([delta, delta], axis=1))
                    dsk_ref[h * 2 + half] += dsf[:, 0:1]
                    dss.append((dsf * scale).astype(BF16))
                    probs.append(pr.astype(BF16))
                ds = jnp.concatenate(dss, axis=1)
                pb = jnp.concatenate(probs, axis=1)
                dqp = lax.dot_general(ds, kbd, NN_DIMS, preferred_element_type=F32)
                for p in range(4):
                    dq_ref[:, h * 512 + p * 128: h * 512 + (p + 1) * 128] = dqp[p * 128:(p + 1) * 128].astype(BF16)
                dkk = dkk + _from_blockdiag(lax.dot_general(ds, qp, TN_DIMS, preferred_element_type=F32), h)
                dvv = dvv + _from_blockdiag(lax.dot_general(pb, dop, TN_DIMS, preferred_element_type=F32), h)
            row = lax.broadcasted_iota(jnp.int32, (256, 128), 0)
            dkk = jnp.where(row == 0, 0.0, dkk)
            dvv = jnp.where(row == 0, 0.0, dvv)

            @pl.when(n >= 1)
            def _():
                dkv_ref[:, :128] = (ck_ref[...] + dkk[:128]).astype(BF16)
                dkv_ref[:, 128:] = (cv_ref[...] + dvv[:128]).astype(BF16)

            ck_ref[...] = dkk[128:]
            cv_ref[...] = dvv[128:]

        @pl.when(n == nb)
        def _():
            dkv_ref[:, :128] = ck_ref[...].astype(BF16)
            dkv_ref[:, 128:] = cv_ref[...].astype(BF16)

    last = nb - 1
    blk = lambda n: (jnp.minimum(n, last), 0)
    return pl.pallas_call(
        _carried(body, 7, 3, ex, nb + 1), grid=(nb + 1,),
        in_specs=[pl.BlockSpec((BLOCK, NQ), blk),
                  pl.BlockSpec((BLOCK, KVW), lambda n: (jnp.minimum(n, last), NQ // KVW)),
                  pl.BlockSpec((BLOCK, KVW), lambda n: (jnp.clip(n - 1, 0, last), NQ // KVW)),
                  pl.BlockSpec((BLOCK, NQ), blk), pl.BlockSpec((BLOCK, NQ), blk),
                  pl.BlockSpec((None, 4, 512, 128), lambda n: (jnp.minimum(n, last), 0, 0, 0)),
                  pl.BlockSpec((4, 512, 256), lambda n: (0, 0, 0))] + _ex(ex, "in_specs"),
        out_specs=[pl.BlockSpec((BLOCK, NQ), blk),
                   pl.BlockSpec((BLOCK, KVW), lambda n: (jnp.maximum(n - 1, 0), 0)),
                   pl.BlockSpec((4, 512, 1), lambda n: (0, 0, 0))] + _ex(ex, "out_specs"),
        out_shape=[jax.ShapeDtypeStruct((s, NQ), BF16), jax.ShapeDtypeStruct((s, KVW), BF16),
                   jax.ShapeDtypeStruct((4, 512, 1), F32)] + _ex(ex, "out_shape"),
        scratch_shapes=[pltpu.VMEM((BLOCK, 128), F32), pltpu.VMEM((BLOCK, 128), F32)] + _ex(ex, "scratch"),
        compiler_params=_params("arbitrary"), name="attn_bwd")(qkv, qkv, qkv, do, o, lse, sink_full, *ex_args)


LANE = 128
FFN_HALF = F // 2
FFN_PAIR = 2 * LANE
FFN_PAIRS = F // FFN_PAIR
FFN_GROUPS = ((0, 3), (3, 6), (6, 9), (9, 11))
FFN_HALO = 8


def _resident(shape):
    return pl.BlockSpec(shape, lambda i: (0,) * len(shape), pipeline_mode=pl.Buffered(1))


def _ffn_fwd(tag, x_in, g, w_up_t, w_dw, b_dw, w_down, *, tm, ex=None, ex_args=()):
    s = x_in.shape[0]
    tm = min(tm, s)

    def body(x_ref, g_ref, wup_ref, wdw_ref, bdw_ref, wd_ref, o_ref, h_ref, up_ref, act_ref, gate_ref, carry_ref, ext_ref):
        @pl.when(pl.program_id(0) == 0)
        def _():
            carry_ref[...] = jnp.zeros_like(carry_ref)

        xf = x_ref[...]
        r = lax.rsqrt(jnp.mean(xf * xf, axis=-1, keepdims=True) + RMS_EPS)
        h = ((xf * r) * g_ref[...]).astype(BF16)
        h_ref[...] = h

        def up_mm(p):
            for off in (p * FFN_PAIR, F + p * FFN_PAIR):
                rs = slice(off, off + FFN_PAIR)
                up_ref[:, rs] = lax.dot_general(h, wup_ref[rs, :], NT_DIMS, preferred_element_type=F32).astype(BF16)

        def elementwise(p):
            for c in (2 * p, 2 * p + 1):
                cs = slice(c * LANE, (c + 1) * LANE)
                vs = slice(F + c * LANE, F + (c + 1) * LANE)
                gcol = up_ref[:, cs].astype(F32)
                ext_ref[pl.ds(0, FFN_HALO), :] = carry_ref[:, cs]
                ext_ref[pl.ds(FFN_HALO, tm), :] = gcol
                gate_b = (wdw_ref[2:3, cs] * gcol + wdw_ref[1:2, cs] * ext_ref[pl.ds(FFN_HALO - 1, tm), :]
                          + wdw_ref[0:1, cs] * ext_ref[pl.ds(FFN_HALO - 2, tm), :] + bdw_ref[:, cs]).astype(BF16)
                gate_ref[:, cs] = gate_b
                gate = gate_b.astype(F32)
                act_ref[:, cs] = (gate * _sigmoid(gate) * up_ref[:, vs].astype(F32)).astype(BF16)
                carry_ref[:, cs] = gcol[tm - FFN_HALO:]

        def down_mm(lo, hi):
            ks = slice(lo * FFN_PAIR, hi * FFN_PAIR)
            return lax.dot_general(act_ref[:, ks], wd_ref[ks, :], NN_DIMS, preferred_element_type=F32)

        acc = xf
        up_mm(0)
        pending = None
        for lo, hi in FFN_GROUPS:
            for p in range(lo, hi):
                if p + 1 < FFN_PAIRS:
                    up_mm(p + 1)
                if pending is not None:
                    acc = acc + down_mm(*pending)
                    pending = None
                elementwise(p)
            pending = (lo, hi)
        o_ref[...] = acc + down_mm(*pending)

    row = lambda i: (i, 0)
    return pl.pallas_call(
        _carried(body, 6, 5, ex, s // tm), grid=(s // tm,),
        in_specs=[pl.BlockSpec((tm, D), row), _resident((1, D)), _resident((2 * F, D)), _resident((3, F)),
                  _resident((1, F)), _resident((F, D))] + _ex(ex, "in_specs"),
        out_specs=[pl.BlockSpec((tm, D), row), pl.BlockSpec((tm, D), row), pl.BlockSpec((tm, 2 * F), row),
                   pl.BlockSpec((tm, F), row), pl.BlockSpec((tm, F), row)] + _ex(ex, "out_specs"),
        out_shape=[jax.ShapeDtypeStruct((s, D), F32), jax.ShapeDtypeStruct((s, D), BF16),
                   jax.ShapeDtypeStruct((s, 2 * F), BF16), jax.ShapeDtypeStruct((s, F), BF16),
                   jax.ShapeDtypeStruct((s, F), BF16)] + _ex(ex, "out_shape"),
        scratch_shapes=[pltpu.VMEM((FFN_HALO, F), F32), pltpu.VMEM((tm + FFN_HALO, LANE), F32)] + _ex(ex, "scratch"),
        compiler_params=_params("arbitrary"), name=f"ffn{tag}_fwd")(x_in, g, w_up_t, w_dw, b_dw, w_down, *ex_args)


def _ffn_bwd(tag, dx_out, x_in, g, w_up_t, w_dw, w_down, up, gate, *, tm):
    s = x_in.shape[0]
    tm = min(tm, s)
    n_i = s // tm

    def body(dx_ref, x_ref, g_ref, up_ref, gate_ref, wup_ref, wdw_ref, wd_ref,
             dxin_ref, dup_ref, dg_ref, dwb_ref, carry_ref, da0_ref, da1_ref, extd_ref):
        i = pl.program_id(0)

        @pl.when(i == 0)
        def _():
            carry_ref[...] = jnp.zeros_like(carry_ref)
            dg_ref[...] = jnp.zeros_like(dg_ref)
            dwb_ref[...] = jnp.zeros_like(dwb_ref)

        dxf = dx_ref[...]
        dxb = dxf.astype(BF16)
        da_refs = (da0_ref, da1_ref)

        def da_mm(p):
            ks = slice(p * FFN_PAIR, (p + 1) * FFN_PAIR)
            da_refs[p % 2][...] = lax.dot_general(dxb, wd_ref[ks, :], NT_DIMS, preferred_element_type=F32)

        def elementwise(p):
            da_ref = da_refs[p % 2]
            for c in range(2):
                cc = 2 * p + c
                cs = slice(cc * LANE, (cc + 1) * LANE)
                vs = slice(F + cc * LANE, F + (cc + 1) * LANE)
                gate = gate_ref[:, cs].astype(F32)
                sg = _sigmoid(gate)
                silu = gate * sg
                dac = da_ref[:, c * LANE:(c + 1) * LANE]
                dup_ref[:, vs] = (dac * silu).astype(BF16)
                dgate = dac * up_ref[:, vs].astype(F32) * (sg + silu * (1.0 - sg))
                extd_ref[pl.ds(0, tm), :] = dgate
                extd_ref[pl.ds(tm, FFN_HALO), :] = carry_ref[:, cs]
                d1 = extd_ref[pl.ds(1, tm), :]
                d2 = extd_ref[pl.ds(2, tm), :]
                dup_ref[:, cs] = (wdw_ref[2:3, cs] * dgate + wdw_ref[1:2, cs] * d1 + wdw_ref[0:1, cs] * d2).astype(BF16)
                carry_ref[:, cs] = dgate[:FFN_HALO]
                gcol = up_ref[:, cs].astype(F32)
                dwb_ref[2, :, cs] += _sum8(dgate * gcol)
                dwb_ref[1, :, cs] += _sum8(d1 * gcol)
                dwb_ref[0, :, cs] += _sum8(d2 * gcol)
                dwb_ref[3, :, cs] += _sum8(dgate)

        def dh_mm(lo, hi):
            ks = slice(lo * FFN_PAIR, hi * FFN_PAIR)
            vks = slice(F + lo * FFN_PAIR, F + hi * FFN_PAIR)
            return (lax.dot_general(dup_ref[:, ks], wup_ref[ks, :], NN_DIMS, preferred_element_type=F32)
                    + lax.dot_general(dup_ref[:, vks], wup_ref[vks, :], NN_DIMS, preferred_element_type=F32))

        acc = None
        da_mm(0)
        pending = None
        for lo, hi in FFN_GROUPS:
            for p in range(lo, hi):
                if p + 1 < FFN_PAIRS:
                    da_mm(p + 1)
                if pending is not None:
                    part = dh_mm(*pending)
                    acc = part if acc is None else acc + part
                    pending = None
                elementwise(p)
            pending = (lo, hi)
        acc = acc + dh_mm(*pending)
        dx, dg_rows = _rms_bwd(acc, x_ref[...], g_ref[...])
        dxin_ref[...] = dxf + dx
        dg_ref[...] += jnp.sum(dg_rows, axis=0, keepdims=True)

    rev = lambda i: (n_i - 1 - i, 0)
    return pl.pallas_call(
        body, grid=(n_i,),
        in_specs=[pl.BlockSpec((tm, D), rev), pl.BlockSpec((tm, D), rev), _resident((1, D)),
                  pl.BlockSpec((tm, 2 * F), rev), pl.BlockSpec((tm, F), rev),
                  _resident((2 * F, D)), _resident((3, F)), _resident((F, D))],
        out_specs=[pl.BlockSpec((tm, D), rev), pl.BlockSpec((tm, 2 * F), rev),
                   pl.BlockSpec((1, D), lambda i: (0, 0)), pl.BlockSpec((4, 8, F), lambda i: (0, 0, 0))],
        out_shape=[jax.ShapeDtypeStruct((s, D), F32), jax.ShapeDtypeStruct((s, 2 * F), BF16),
                   jax.ShapeDtypeStruct((1, D), F32), jax.ShapeDtypeStruct((4, 8, F), F32)],
        scratch_shapes=[pltpu.VMEM((FFN_HALO, F), F32), pltpu.VMEM((tm, FFN_PAIR), F32), pltpu.VMEM((tm, FFN_PAIR), F32),
                        pltpu.VMEM((tm + FFN_HALO, LANE), F32)],
        compiler_params=_params("arbitrary"), name=f"ffn{tag}_bwd")(dx_out, x_in, g, up, gate, w_up_t, w_dw, w_down)


CONV_HALO = 32
CONV_CHUNK = 256
CONV_ROWS = 64


def _ln(u2, g, b):
    mu = jnp.mean(u2, axis=-1, keepdims=True)
    xc = u2 - mu
    rstd = lax.rsqrt(jnp.mean(xc * xc, axis=-1, keepdims=True) + LN_EPS)
    xhat = xc * rstd
    return xhat, rstd, xhat * g + b


def _phase_taps(ph):
    first = CONV_HALO - (CONV_K - 1)
    return [(k, first + k - ph) for k in range(CONV_K) if (first + k) % 8 == ph]


def _sum8(v):
    out = v[0:8]
    for j in range(8, v.shape[0], 8):
        out = out + v[j:j + 8]
    return out


def _store_phases(src_ref, sh_ref, tt, cols=slice(None)):
    sh_ref[0] = src_ref[:, cols]
    for ph in range(1, 8):
        sh_ref[ph, pl.ds(0, tt + CONV_HALO - 8), :] = src_ref[pl.ds(ph, tt + CONV_HALO - 8), cols]


def _conv_taps_fwd(w_ref, sh_ref, u2_ref, cs, tt):
    for base in range(0, tt, CONV_ROWS):
        acc = jnp.zeros((CONV_ROWS, CONV_CHUNK), F32) + w_ref[CONV_K:CONV_K + 1, cs]
        for ph in range(8):
            for k, off in _phase_taps(ph):
                acc = acc + w_ref[k:k + 1, cs] * sh_ref[ph, pl.ds(base + off, CONV_ROWS), :]
        u2_ref[pl.ds(base, CONV_ROWS), cs] = acc.astype(BF16)


def _conv_fwd(x_in, g, w1t, b1, wdw, ln_g, ln_b, w2, b2, *, tt):
    s = x_in.shape[0]
    tt = min(tt, s)
    n_c = D // CONV_CHUNK

    def body(x_ref, g_ref, w1_ref, b1_ref, w_ref, lg_ref, lb_ref, w2_ref, b2_ref,
             o_ref, h_ref, a_ref, u2_ref, s_ref, carry_ref, ext_ref, sh_ref):
        @pl.when(pl.program_id(0) == 0)
        def _():
            carry_ref[...] = jnp.zeros_like(carry_ref)

        xf = x_ref[...]
        r = lax.rsqrt(jnp.mean(xf * xf, axis=-1, keepdims=True) + RMS_EPS)
        h = ((xf * r) * g_ref[...]).astype(BF16)
        h_ref[...] = h

        def pw1_mm(c):
            for off in (c * CONV_CHUNK, D + c * CONV_CHUNK):
                rs = slice(off, off + CONV_CHUNK)
                a_ref[:, rs] = (lax.dot_general(h, w1_ref[rs, :], NT_DIMS, preferred_element_type=F32)
                                + b1_ref[:, rs]).astype(BF16)

        pw1_mm(0)
        for c in range(n_c):
            if c + 1 < n_c:
                pw1_mm(c + 1)
            cs = slice(c * CONV_CHUNK, (c + 1) * CONV_CHUNK)
            gs = slice(D + c * CONV_CHUNK, D + (c + 1) * CONV_CHUNK)
            u = a_ref[:, cs].astype(F32) * _sigmoid(a_ref[:, gs].astype(F32))
            ext_ref[pl.ds(0, CONV_HALO), :] = carry_ref[:, cs]
            ext_ref[pl.ds(CONV_HALO, tt), :] = u
            carry_ref[:, cs] = u[tt - CONV_HALO:]
            _store_phases(ext_ref, sh_ref, tt)
            _conv_taps_fwd(w_ref, sh_ref, u2_ref, cs, tt)
        _, _, ln = _ln(u2_ref[...].astype(F32), lg_ref[...], lb_ref[...])
        sv = (ln * _sigmoid(ln)).astype(BF16)
        s_ref[...] = sv
        o_ref[...] = xf + lax.dot_general(sv, w2_ref[...], NN_DIMS, preferred_element_type=F32) + b2_ref[...]

    row = lambda i: (i, 0)
    return pl.pallas_call(
        body, grid=(s // tt,),
        in_specs=[pl.BlockSpec((tt, D), row), _resident((1, D)), _resident((2 * D, D)), _resident((1, 2 * D)),
                  _resident((32, D)), _resident((1, D)), _resident((1, D)), _resident((D, D)), _resident((1, D))],
        out_specs=[pl.BlockSpec((tt, D), row), pl.BlockSpec((tt, D), row), pl.BlockSpec((tt, 2 * D), row),
                   pl.BlockSpec((tt, D), row), pl.BlockSpec((tt, D), row)],
        out_shape=[jax.ShapeDtypeStruct((s, D), F32), jax.ShapeDtypeStruct((s, D), BF16),
                   jax.ShapeDtypeStruct((s, 2 * D), BF16), jax.ShapeDtypeStruct((s, D), BF16),
                   jax.ShapeDtypeStruct((s, D), BF16)],
        scratch_shapes=[pltpu.VMEM((CONV_HALO, D), F32), pltpu.VMEM((tt + CONV_HALO, CONV_CHUNK), F32),
                        pltpu.VMEM((8, tt + CONV_HALO, CONV_CHUNK), F32)],
        compiler_params=_params("arbitrary"), name="conv_fwd")(x_in, g, w1t, b1, wdw, ln_g, ln_b, w2, b2)


def _conv_bwd(dx_out, x_in, g, a, u2, w1t, wdw, ln_g, ln_b, w2, *, tt, ex=None, ex_args=()):
    s = x_in.shape[0]
    tt = min(tt, s)
    hb = tt // CONV_HALO
    n_i = s // tt
    te = tt + CONV_HALO
    n_c = D // CONV_CHUNK

    def body(dx_ref, x_ref, g_ref, a_ref, prev_ref, u2_ref, w1_ref, w_ref, lg_ref, lb_ref, w2_ref,
             dxin_ref, da_ref, dg_ref, acc_ref, db1_ref, db2_ref,
             carry_ref, e2_ref, ext_ref, ush_ref, dsh_ref, du_ref):
        i = pl.program_id(0)

        @pl.when(i == 0)
        def _():
            for ref in (dg_ref, acc_ref, db1_ref, db2_ref, carry_ref):
                ref[...] = jnp.zeros_like(ref)

        dxf = dx_ref[...]
        db2_ref[...] += _sum8(dxf)
        dse = lax.dot_general(dxf.astype(BF16), w2_ref[...], NT_DIMS, preferred_element_type=F32)
        gv = lg_ref[...]
        xhat, rstd, ln = _ln(u2_ref[...].astype(F32), gv, lb_ref[...])
        sg = _sigmoid(ln)
        dln = dse * (sg * (1.0 + ln * (1.0 - sg)))
        acc_ref[32] += _sum8(dln * xhat)
        acc_ref[33] += _sum8(dln)
        dxh = dln * gv
        du2 = rstd * (dxh - jnp.mean(dxh, axis=-1, keepdims=True) - xhat * jnp.mean(dxh * xhat, axis=-1, keepdims=True))
        e2_ref[pl.ds(0, tt), :] = du2
        e2_ref[pl.ds(tt, CONV_HALO), :] = carry_ref[...]
        carry_ref[...] = du2[:CONV_HALO]
        first_tile = i == n_i - 1

        def dh_mm(c):
            cs = slice(c * CONV_CHUNK, (c + 1) * CONV_CHUNK)
            gs = slice(D + c * CONV_CHUNK, D + (c + 1) * CONV_CHUNK)
            return (lax.dot_general(da_ref[:, cs], w1_ref[cs, :], NN_DIMS, preferred_element_type=F32)
                    + lax.dot_general(da_ref[:, gs], w1_ref[gs, :], NN_DIMS, preferred_element_type=F32))

        dh = None
        for c in range(n_c):
            if c >= 1:
                part = dh_mm(c - 1)
                dh = part if dh is None else dh + part
            cs = slice(c * CONV_CHUNK, (c + 1) * CONV_CHUNK)
            gs = slice(D + c * CONV_CHUNK, D + (c + 1) * CONV_CHUNK)
            uh = prev_ref[:, cs].astype(F32) * _sigmoid(prev_ref[:, gs].astype(F32))
            ext_ref[pl.ds(0, CONV_HALO), :] = jnp.where(first_tile, 0.0, uh)
            a1 = a_ref[:, cs].astype(F32)
            sg2 = _sigmoid(a_ref[:, gs].astype(F32))
            ext_ref[pl.ds(CONV_HALO, tt), :] = a1 * sg2
            _store_phases(ext_ref, ush_ref, tt)
            _store_phases(e2_ref, dsh_ref, tt, cs)
            for base in range(0, tt, CONV_ROWS):
                d0 = e2_ref[pl.ds(base, CONV_ROWS), cs]
                du = jnp.zeros((CONV_ROWS, CONV_CHUNK), F32)
                for ph in range(8):
                    for k, off in _phase_taps(ph):
                        acc_ref[k, :, cs] += _sum8(d0 * ush_ref[ph, pl.ds(base + off, CONV_ROWS), :])
                    for k in range(CONV_K):
                        if (CONV_K - 1 - k) % 8 == ph:
                            du = du + w_ref[k:k + 1, cs] * dsh_ref[ph, pl.ds(base + CONV_K - 1 - k - ph, CONV_ROWS), :]
                acc_ref[CONV_K, :, cs] += _sum8(d0)
                du_ref[pl.ds(base, CONV_ROWS), :] = du
            du = du_ref[...]
            da1 = du * sg2
            da2 = du * a1 * (sg2 * (1.0 - sg2))
            da_ref[:, cs] = da1.astype(BF16)
            da_ref[:, gs] = da2.astype(BF16)
            db1_ref[:, cs] += _sum8(da1)
            db1_ref[:, gs] += _sum8(da2)
        dh = dh + dh_mm(n_c - 1)
        dx, dg_rows = _rms_bwd(dh, x_ref[...], g_ref[...])
        dxin_ref[...] = dxf + dx
        dg_ref[...] += jnp.sum(dg_rows, axis=0, keepdims=True)

    rev = lambda i: (n_i - 1 - i, 0)
    return pl.pallas_call(
        _carried(body, 11, 6, ex, n_i), grid=(n_i,),
        in_specs=[pl.BlockSpec((tt, D), rev), pl.BlockSpec((tt, D), rev), _resident((1, D)),
                  pl.BlockSpec((tt, 2 * D), rev),
                  pl.BlockSpec((CONV_HALO, 2 * D), lambda i: (jnp.maximum((n_i - 1 - i) * hb - 1, 0), 0)),
                  pl.BlockSpec((tt, D), rev), _resident((2 * D, D)), _resident((32, D)), _resident((1, D)),
                  _resident((1, D)), _resident((D, D))] + _ex(ex, "in_specs"),
        out_specs=[pl.BlockSpec((tt, D), rev), pl.BlockSpec((tt, 2 * D), rev), pl.BlockSpec((1, D), lambda i: (0, 0)),
                   pl.BlockSpec((40, 8, D), lambda i: (0, 0, 0)), pl.BlockSpec((8, 2 * D), lambda i: (0, 0)),
                   pl.BlockSpec((8, D), lambda i: (0, 0))] + _ex(ex, "out_specs"),
        out_shape=[jax.ShapeDtypeStruct((s, D), F32), jax.ShapeDtypeStruct((s, 2 * D), BF16),
                   jax.ShapeDtypeStruct((1, D), F32), jax.ShapeDtypeStruct((40, 8, D), F32),
                   jax.ShapeDtypeStruct((8, 2 * D), F32), jax.ShapeDtypeStruct((8, D), F32)] + _ex(ex, "out_shape"),
        scratch_shapes=[pltpu.VMEM((CONV_HALO, D), F32), pltpu.VMEM((te, D), F32), pltpu.VMEM((te, CONV_CHUNK), F32),
                        pltpu.VMEM((8, te, CONV_CHUNK), F32), pltpu.VMEM((8, te, CONV_CHUNK), F32),
                        pltpu.VMEM((tt, CONV_CHUNK), F32)] + _ex(ex, "scratch"),
        compiler_params=_params("arbitrary"), name="conv_bwd")(dx_out, x_in, g, a, a, u2, w1t, wdw, ln_g, ln_b, w2, *ex_args)


def _adamw(name, w, g, m, v):
    rows, cols = w.shape
    tr = _row_tile(rows, 256)

    def body(w_ref, g_ref, m_ref, v_ref, d_ref, nm_ref, nv_ref):
        gv = g_ref[...]
        m2 = ADAM_B1 * m_ref[...] + (1.0 - ADAM_B1) * gv
        v2 = ADAM_B2 * v_ref[...] + (1.0 - ADAM_B2) * (gv * gv)
        m_hat = m2 / (1.0 - ADAM_B1 ** ADAM_STEP)
        v_hat = v2 / (1.0 - ADAM_B2 ** ADAM_STEP)
        d_ref[...] = -ADAM_LR * (m_hat / (jnp.sqrt(v_hat) + ADAM_EPS) + ADAM_WD * w_ref[...])
        nm_ref[...] = m2
        nv_ref[...] = v2

    spec = pl.BlockSpec((tr, cols), lambda i: (i, 0))
    return pl.pallas_call(
        body, grid=(rows // tr,), in_specs=[spec] * 4, out_specs=[spec] * 3,
        out_shape=[jax.ShapeDtypeStruct((rows, cols), F32)] * 3,
        compiler_params=_params("parallel"), name=name)(w, g, m, v)


def _sum_slabs(name, buf):
    rows = buf.shape[1]
    tr = _row_tile(rows, 512, 16)

    def body(b_ref, o_ref):
        acc = b_ref[0].astype(F32)
        for k in range(1, N_DEV):
            acc = acc + b_ref[k].astype(F32)
        o_ref[...] = acc

    return pl.pallas_call(
        body, grid=(rows // tr,), in_specs=[pl.BlockSpec((N_DEV, tr, D), lambda i: (0, i, 0))],
        out_specs=pl.BlockSpec((tr, D), lambda i: (i, 0)), out_shape=jax.ShapeDtypeStruct((rows, D), F32),
        compiler_params=_params("parallel"), name=name)(buf)


def _ex(ex, field):
    return list(getattr(ex, field)) if ex is not None else []


def _me():
    x, y, c = lax.axis_index("x"), lax.axis_index("y"), lax.axis_index("c")
    return x, y, c, 4 * x + 2 * y + c


def _peer(x, y, c, k):
    return (x ^ (k >> 2), y ^ ((k >> 1) & 1), c ^ (k & 1))


class _Gather:
    def __init__(self, mats):
        self.mats = tuple(mats)
        self.rows = sum(MAT_ROWS[i] for i in self.mats)
        any_spec = pl.BlockSpec(memory_space=pl.ANY)
        self.n_in, self.n_out = 1, len(self.mats)
        self.in_specs = [any_spec]
        self.out_specs = [any_spec] * self.n_out
        self.out_shape = [jax.ShapeDtypeStruct((N_DEV * MAT_ROWS[i], D), BF16) for i in self.mats]
        self.scratch = [pltpu.SemaphoreType.DMA((N_DEV - 1,)), pltpu.SemaphoreType.DMA((N_DEV - 1,)),
                        pltpu.SemaphoreType.DMA]
        assert max(s.shape[0] for s in self.out_shape) >= self.rows

    def start(self, ins, outs, scr):
        (slab_ref,), (send_sems, recv_sems, local_sem) = ins, scr
        x, y, c, me = _me()
        for n, i in enumerate(self.mats):
            src = slab_ref.at[pl.ds(MAT_OFF[i], MAT_ROWS[i])]
            dst = outs[n].at[pl.ds(me * MAT_ROWS[i], MAT_ROWS[i])]
            pltpu.make_async_copy(src, dst, local_sem).start()
            for k in range(1, N_DEV):
                pltpu.make_async_remote_copy(src_ref=src, dst_ref=dst, send_sem=send_sems.at[k - 1],
                                             recv_sem=recv_sems.at[k - 1], device_id=_peer(x, y, c, k),
                                             device_id_type=MESH).start()

    def wait(self, ins, outs, scr):
        send_sems, recv_sems, local_sem = scr
        x, y, c, _ = _me()
        big = max(range(self.n_out), key=lambda n: self.out_shape[n].shape[0])
        whole = outs[big].at[pl.ds(0, self.rows)]
        for k in range(1, N_DEV):
            pltpu.make_async_remote_copy(src_ref=whole, dst_ref=whole, send_sem=send_sems.at[k - 1],
                                         recv_sem=recv_sems.at[k - 1], device_id=_peer(x, y, c, k),
                                         device_id_type=MESH).wait()
        pltpu.make_async_copy(whole, whole, local_sem).wait()


class _Scatter:
    def __init__(self, mats):
        self.mats = tuple(mats)
        self.rows = sum(MAT_ROWS[i] for i in self.mats)
        any_spec = pl.BlockSpec(memory_space=pl.ANY)
        self.n_in, self.n_out = len(self.mats), 1
        self.in_specs = [any_spec] * self.n_in
        self.out_specs = [any_spec]
        self.out_shape = [jax.ShapeDtypeStruct((N_DEV, self.rows, D), BF16)]
        self.scratch = [pltpu.SemaphoreType.DMA((N_DEV - 1,)), pltpu.SemaphoreType.DMA((N_DEV - 1,)),
                        pltpu.SemaphoreType.DMA]

    def offsets(self):
        return [sum(MAT_ROWS[i] for i in self.mats[:n]) for n in range(len(self.mats))]

    def start(self, ins, outs, scr):
        (buf_ref,), (send_sems, recv_sems, local_sem) = outs, scr
        x, y, c, me = _me()
        for n, (i, off) in enumerate(zip(self.mats, self.offsets())):
            r = MAT_ROWS[i]
            pltpu.make_async_copy(ins[n].at[pl.ds(me * r, r)], buf_ref.at[0, pl.ds(off, r)], local_sem).start()
            for k in range(1, N_DEV):
                pltpu.make_async_remote_copy(src_ref=ins[n].at[pl.ds((me ^ k) * r, r)], dst_ref=buf_ref.at[k, pl.ds(off, r)],
                                             send_sem=send_sems.at[k - 1], recv_sem=recv_sems.at[k - 1],
                                             device_id=_peer(x, y, c, k), device_id_type=MESH).start()

    def wait(self, ins, outs, scr):
        (buf_ref,), (send_sems, recv_sems, local_sem) = outs, scr
        x, y, c, _ = _me()
        whole = buf_ref.at[0]
        for k in range(1, N_DEV):
            pltpu.make_async_remote_copy(src_ref=whole, dst_ref=whole, send_sem=send_sems.at[k - 1],
                                         recv_sem=recv_sems.at[k - 1], device_id=_peer(x, y, c, k),
                                         device_id_type=MESH).wait()
        pltpu.make_async_copy(whole, whole, local_sem).wait()


def _carried(body, n_in, n_out, ex, n_steps):
    if ex is None:
        return body

    def wrapped(*refs):
        ins, ex_ins = refs[:n_in], refs[n_in:n_in + ex.n_in]
        p = n_in + ex.n_in
        outs, ex_outs = refs[p:p + n_out], refs[p + n_out:p + n_out + ex.n_out]
        p += n_out + ex.n_out
        n_scr = len(refs) - p - len(ex.scratch)
        scr, ex_scr = refs[p:p + n_scr], refs[p + n_scr:]

        @pl.when(pl.program_id(0) == 0)
        def _():
            ex.start(ex_ins, ex_outs, ex_scr)

        body(*ins, *outs, *scr)

        @pl.when(pl.program_id(0) == n_steps - 1)
        def _():
            ex.wait(ex_ins, ex_outs, ex_scr)

    return wrapped


def _exchange_small(name, ex, ex_args, small, reduce):
    vmem_spec = pl.BlockSpec(memory_space=pltpu.VMEM)
    n_small = small.shape[0]

    def body(*refs):
        ex_ins, small_ref = refs[:ex.n_in], refs[ex.n_in]
        p = ex.n_in + 1
        ex_outs, res_ref = refs[p:p + ex.n_out], refs[p + ex.n_out]
        p += ex.n_out + 1
        if reduce:
            all_ref, p = refs[p], p + 1
        else:
            all_ref = res_ref
        sm_send, sm_recv = refs[p], refs[p + 1]
        ex_scr = refs[p + 2:]
        x, y, c, me = _me()
        ex.start(ex_ins, ex_outs, ex_scr)
        copies = [pltpu.make_async_remote_copy(
            src_ref=small_ref, dst_ref=all_ref.at[me], send_sem=sm_send.at[k - 1], recv_sem=sm_recv.at[k - 1],
            device_id=_peer(x, y, c, k), device_id_type=MESH) for k in range(1, N_DEV)]
        for cp in copies:
            cp.start()
        all_ref[me] = small_ref[...]
        for cp in copies:
            cp.wait()
        if reduce:
            acc = all_ref[0]
            for d in range(1, N_DEV):
                acc = acc + all_ref[d]
            res_ref[...] = acc
        ex.wait(ex_ins, ex_outs, ex_scr)

    res_shape = (n_small, D) if reduce else (N_DEV, n_small, D)
    scratch = ([pltpu.VMEM((N_DEV, n_small, D), F32)] if reduce else []) + [
        pltpu.SemaphoreType.DMA((N_DEV - 1,)), pltpu.SemaphoreType.DMA((N_DEV - 1,))] + ex.scratch
    outs = pl.pallas_call(
        body, in_specs=ex.in_specs + [vmem_spec], out_specs=ex.out_specs + [vmem_spec],
        out_shape=ex.out_shape + [jax.ShapeDtypeStruct(res_shape, F32)], scratch_shapes=scratch,
        compiler_params=pltpu.CompilerParams(vmem_limit_bytes=VMEM_LIMIT_BYTES), name=name)(*ex_args, small)
    return outs


def _pack_rows(arrs):
    rows = []
    for a in arrs:
        flat = a.reshape(-1).astype(F32)
        pad = (-flat.shape[0]) % D
        rows.append(jnp.pad(flat, (0, pad)).reshape(-1, D))
    slab = jnp.concatenate(rows, axis=0)
    return jnp.pad(slab, ((0, (-slab.shape[0]) % 8), (0, 0)))


def _unpack_rows(slab, shapes):
    out, r = [], 0
    for shp in shapes:
        size = math.prod(shp)
        nrows = -(-size // D)
        out.append(slab[r:r + nrows].reshape(-1)[:size].reshape(shp))
        r += nrows
    return out


def kernel(x, norm_mix, attn_w_qkv, attn_b_qkv, attn_sinks, attn_w_o, attn_b_o, conv_w_pw1, conv_b_pw1, conv_w_dw, conv_b_dw, conv_ln_g, conv_ln_b, conv_w_pw2, conv_b_pw2, norm_ffn, ffn_w_up, ffn_w_dw, ffn_b_dw, ffn_w_down, final_norm, loss_target, m_norm_mix, m_attn_w_qkv, m_attn_b_qkv, m_attn_sinks, m_attn_w_o, m_attn_b_o, m_conv_w_pw1, m_conv_b_pw1, m_conv_w_dw, m_conv_b_dw, m_conv_ln_g, m_conv_ln_b, m_conv_w_pw2, m_conv_b_pw2, m_norm_ffn, m_ffn_w_up, m_ffn_w_dw, m_ffn_b_dw, m_ffn_w_down, m_final_norm, v_norm_mix, v_attn_w_qkv, v_attn_b_qkv, v_attn_sinks, v_attn_w_o, v_attn_b_o, v_conv_w_pw1, v_conv_b_pw1, v_conv_w_dw, v_conv_b_dw, v_conv_ln_g, v_conv_ln_b, v_conv_w_pw2, v_conv_b_pw2, v_norm_ffn, v_ffn_w_up, v_ffn_w_dw, v_ffn_b_dw, v_ffn_w_down, v_final_norm):
    s = x.shape[1]
    me = 4 * lax.axis_index("x") + 2 * lax.axis_index("y") + lax.axis_index("c")
    x0 = x.reshape(s, D)
    tgt = loss_target.reshape(s, D)

    slab = jnp.concatenate([attn_w_qkv[0].T, attn_w_o[0], conv_w_pw1[0].T, conv_w_pw2[0],
                            ffn_w_up[0].T, ffn_w_down[0], ffn_w_up[1].T, ffn_w_down[1]], axis=0).astype(BF16)
    sharded_small = [conv_b_pw1, conv_w_dw, conv_b_dw, conv_ln_g, conv_ln_b, conv_b_pw2, ffn_w_dw]
    small_local = _pack_rows(sharded_small)
    w_qkv_t, small_all = _exchange_small("gather_first", _Gather((0,)), (slab,), small_local, reduce=False)
    parts = [_unpack_rows(small_all[d], [a.shape for a in sharded_small]) for d in range(N_DEV)]
    b_pw1, w_cdw, b_cdw, ln_g, ln_b, b_pw2, w_fdw = [jnp.concatenate([parts[d][i] for d in range(N_DEV)], axis=-1)
                                                      for i in range(len(sharded_small))]
    conv_w32 = jnp.concatenate([w_cdw[0], b_cdw], axis=0)
    sr = attn_sinks.reshape(N_KV, 4, 2)
    sink_full = jnp.broadcast_to(jnp.repeat(jnp.transpose(sr, (0, 2, 1)), BLOCK, axis=-1).reshape(4, 512, 1),
                                 (4, 512, 256))

    qkv, h0 = _mm("qkv", x0, w_qkv_t, nt=True, tm=512, tn=NQ + KVW, out_dtype=BF16, rms_g=norm_mix[0:1], bias=attn_b_qkv)
    o, lse, w_o, w_up0_t, w_dn0 = _attn_fwd(qkv, sink_full, _Gather((1, 4, 5)), (slab,))
    x1 = _mm("attn_out", o, w_o, nt=False, tm=512, tn=D, out_dtype=F32, bias=attn_b_o, res=x0)
    x2, h1, up0, act0, gate0, w_pw1_t, w_pw2, w_up1_t, w_dn1 = _ffn_fwd(0, x1, norm_ffn[0:1], w_up0_t, w_fdw[0], ffn_b_dw[0:1], w_dn0,
                                                             tm=256, ex=_Gather((2, 3, 6, 7)), ex_args=(slab,))
    x3, h2, a, u2, s_act = _conv_fwd(x2, norm_mix[1:2], w_pw1_t, b_pw1, conv_w32, ln_g, ln_b, w_pw2, b_pw2, tt=256)
    x4, h3, up1, act1, gate1 = _ffn_fwd(1, x3, norm_ffn[1:2], w_up1_t, w_fdw[1], ffn_b_dw[1:2], w_dn1, tm=256)
    dx4, loss_acc, dg_final = _loss_head(x4, final_norm.reshape(1, D), tgt, tm=512)
    loss = lax.psum(loss_acc[0, 0], ("x", "y", "c"))

    dx3, d_up1, dg_f1, dwb1 = _ffn_bwd(1, dx4, x3, norm_ffn[1:2], w_up1_t, w_fdw[1], w_dn1, up1, gate1, tm=256)
    dw_dn1 = _mm_tn("ffn1_dwdown", act1, dx4, tn=FFN_HALF, tt=512)
    dw_up1_t = _mm_tn("ffn1_dwup", d_up1, h3, tn=FFN_HALF, tt=512)
    dw_pw2 = _mm_tn("dw_pw2", s_act, dx3, tn=D, tt=512)
    scatter_a = _Scatter((6, 7))
    dx2, da, dg_m1, conv_acc8, db_pw1_8, db_pw2_8, buf_a = _conv_bwd(
        dx3, x2, norm_mix[1:2], a, u2, w_pw1_t, conv_w32, ln_g, ln_b, w_pw2, tt=256, ex=scatter_a, ex_args=(dw_up1_t, dw_dn1))
    conv_acc = jnp.sum(conv_acc8, axis=1)
    db_pw1 = jnp.sum(db_pw1_8, axis=0, keepdims=True)
    db_pw2 = jnp.sum(db_pw2_8, axis=0, keepdims=True)
    dw_pw1_t = _mm_tn("dw_pw1", da, h2, tn=D, tt=512)
    dx1, d_up0, dg_f0, dwb0 = _ffn_bwd(0, dx2, x1, norm_ffn[0:1], w_up0_t, w_fdw[0], w_dn0, up0, gate0, tm=256)
    dw_dn0 = _mm_tn("ffn0_dwdown", act0, dx2, tn=FFN_HALF, tt=512)
    dw_up0_t = _mm_tn("ffn0_dwup", d_up0, h1, tn=FFN_HALF, tt=512)
    do = _mm("d_attn_out", dx1, w_o, nt=True, tm=512, tn=D, out_dtype=BF16)
    dw_o, db_o = _mm_tn("dw_o", o, dx1, tn=D, tt=512, colsum_r=True)
    scatter_b = _Scatter((1, 2, 3, 4, 5))
    dq, dkv, dsk, buf_b = _attn_bwd(qkv, do, o, lse, sink_full, scatter_b, (dw_o, dw_pw1_t, dw_pw2, dw_up0_t, dw_dn0))
    dqkv = jnp.concatenate([dq, dkv], axis=1)
    grad_x, dg_m0 = _dgrad_rms("d_qkv", dqkv, w_qkv_t, x0, norm_mix[0:1], dx1, tm=512, tk=NQ + KVW)
    dw_qkv_t, db_qkv = _mm_tn("dw_qkv", dqkv, h0, tn=NQ + KVW, tt=512, colsum_l=True)

    dwb0, dwb1 = jnp.sum(dwb0, axis=1), jnp.sum(dwb1, axis=1)
    d_sinks = jnp.transpose(jnp.sum(dsk.reshape(N_KV, 2, 4, BLOCK), axis=-1), (0, 2, 1)).reshape(1, 16)
    small_grads = [jnp.concatenate([dg_m0, dg_m1], axis=0), db_qkv, d_sinks, db_o, jnp.concatenate([dg_f0, dg_f1], axis=0),
                   jnp.stack([dwb0[3], dwb1[3]]), dg_final, db_pw1, conv_acc[0:CONV_K], conv_acc[31:32], conv_acc[32:33],
                   conv_acc[33:34], db_pw2, jnp.stack([dwb0[0:3], dwb1[0:3]])]
    small_shapes = [(2, D), (1, NQ + KVW), (1, 16), (1, D), (2, D), (2, F), (D,), (1, 2 * D), (1, CONV_K, D), (1, D),
                    (1, D), (1, D), (1, D), (2, 3, F)]
    scatter_c = _Scatter((0,))
    buf_c, small_sum = _exchange_small("scatter_last", scatter_c, (dw_qkv_t,), _pack_rows(small_grads), reduce=True)
    gm = [None] * N_MAT
    for tag, sc, buf in (("a", scatter_a, buf_a), ("b", scatter_b, buf_b), ("c", scatter_c, buf_c)):
        gslab = _sum_slabs("sum_slabs_" + tag, buf)
        for i, off in zip(sc.mats, sc.offsets()):
            gm[i] = gslab[off:off + MAT_ROWS[i]]
    (g_norm_mix, g_b_qkv, g_sinks, g_b_o, g_norm_ffn, g_ffn_b_dw, g_final, g_b_pw1_full, g_w_cdw_full, g_b_cdw_full,
     g_ln_g_full, g_ln_b_full, g_b_pw2_full, g_w_fdw_full) = _unpack_rows(small_sum, small_shapes)

    def shard(a, width):
        return lax.dynamic_slice_in_dim(a, me * width, width, axis=a.ndim - 1)

    grads = {
        "norm_mix": g_norm_mix, "attn_w_qkv": gm[0].T[None], "attn_b_qkv": g_b_qkv, "attn_sinks": g_sinks,
        "attn_w_o": gm[1][None], "attn_b_o": g_b_o, "conv_w_pw1": gm[2].T[None], "conv_b_pw1": shard(g_b_pw1_full, 256),
        "conv_w_dw": shard(g_w_cdw_full, 128), "conv_b_dw": shard(g_b_cdw_full, 128), "conv_ln_g": shard(g_ln_g_full, 128),
        "conv_ln_b": shard(g_ln_b_full, 128), "conv_w_pw2": gm[3][None], "conv_b_pw2": shard(g_b_pw2_full, 128),
        "norm_ffn": g_norm_ffn, "ffn_w_up": jnp.stack([gm[4].T, gm[6].T]), "ffn_w_dw": shard(g_w_fdw_full, 352),
        "ffn_b_dw": g_ffn_b_dw, "ffn_w_down": jnp.stack([gm[5], gm[7]]), "final_norm": g_final,
    }
    weights = dict(norm_mix=norm_mix, attn_w_qkv=attn_w_qkv, attn_b_qkv=attn_b_qkv, attn_sinks=attn_sinks, attn_w_o=attn_w_o, attn_b_o=attn_b_o, conv_w_pw1=conv_w_pw1, conv_b_pw1=conv_b_pw1, conv_w_dw=conv_w_dw, conv_b_dw=conv_b_dw, conv_ln_g=conv_ln_g, conv_ln_b=conv_ln_b, conv_w_pw2=conv_w_pw2, conv_b_pw2=conv_b_pw2, norm_ffn=norm_ffn, ffn_w_up=ffn_w_up, ffn_w_dw=ffn_w_dw, ffn_b_dw=ffn_b_dw, ffn_w_down=ffn_w_down, final_norm=final_norm)
    moms = dict(norm_mix=m_norm_mix, attn_w_qkv=m_attn_w_qkv, attn_b_qkv=m_attn_b_qkv, attn_sinks=m_attn_sinks, attn_w_o=m_attn_w_o, attn_b_o=m_attn_b_o, conv_w_pw1=m_conv_w_pw1, conv_b_pw1=m_conv_b_pw1, conv_w_dw=m_conv_w_dw, conv_b_dw=m_conv_b_dw, conv_ln_g=m_conv_ln_g, conv_ln_b=m_conv_ln_b, conv_w_pw2=m_conv_w_pw2, conv_b_pw2=m_conv_b_pw2, norm_ffn=m_norm_ffn, ffn_w_up=m_ffn_w_up, ffn_w_dw=m_ffn_w_dw, ffn_b_dw=m_ffn_b_dw, ffn_w_down=m_ffn_w_down, final_norm=m_final_norm)
    vars_ = dict(norm_mix=v_norm_mix, attn_w_qkv=v_attn_w_qkv, attn_b_qkv=v_attn_b_qkv, attn_sinks=v_attn_sinks, attn_w_o=v_attn_w_o, attn_b_o=v_attn_b_o, conv_w_pw1=v_conv_w_pw1, conv_b_pw1=v_conv_b_pw1, conv_w_dw=v_conv_w_dw, conv_b_dw=v_conv_b_dw, conv_ln_g=v_conv_ln_g, conv_ln_b=v_conv_ln_b, conv_w_pw2=v_conv_w_pw2, conv_b_pw2=v_conv_b_pw2, norm_ffn=v_norm_ffn, ffn_w_up=v_ffn_w_up, ffn_w_dw=v_ffn_w_dw, ffn_b_dw=v_ffn_b_dw, ffn_w_down=v_ffn_w_down, final_norm=v_final_norm)
    names = list(weights)
    big = ("attn_w_qkv", "attn_w_o", "conv_w_pw1", "conv_w_pw2", "ffn_w_up", "ffn_w_down")
    delta, new_m, new_v = {}, {}, {}
    for nm in big:
        shp = weights[nm].shape
        two_d = (shp[0] * shp[1], shp[2])
        res = _adamw("adamw_" + nm, *(t[nm].reshape(two_d) for t in (weights, grads, moms, vars_)))
        delta[nm], new_m[nm], new_v[nm] = (r.reshape(shp) for r in res)
    small_names = [nm for nm in names if nm not in big]
    small_shapes_local = [weights[nm].shape for nm in small_names]
    res = _adamw("adamw_small", *(_pack_rows([t[nm] for nm in small_names]) for t in (weights, grads, moms, vars_)))
    for tgt_dict, slab_out in zip((delta, new_m, new_v), res):
        for nm, val in zip(small_names, _unpack_rows(slab_out, small_shapes_local)):
            tgt_dict[nm] = val
    return (loss, grad_x.reshape(1, s, D), *[grads[nm] for nm in names], *[delta[nm] for nm in names],
            *[new_m[nm] for nm in names], *[new_v[nm] for nm in names])
```

```python
import functools
import math

import jax
import jax.numpy as jnp
from jax import lax
from jax.experimental import pallas as pl
from jax.experimental.pallas import tpu as pltpu

F32 = jnp.float32
BF16 = jnp.bfloat16

D = 1024
F = 2816
HEAD_DIM = 64
N_KV = 2
BLOCK = 128
NQ = 1024
KVW = 256
CONV_K = 31
RMS_EPS = 1e-6
LN_EPS = 1e-5
ADAM_LR, ADAM_B1, ADAM_B2, ADAM_EPS, ADAM_WD, ADAM_STEP = 0.001, 0.9, 0.999, 1e-08, 0.01, 10
N_DEV = 8
MESH = pl.DeviceIdType.MESH
VMEM_LIMIT_BYTES = 56 * 2**20
NEG = float(jnp.finfo(jnp.float32).min)

MAT_ROWS = (160, 128, 256, 128, 704, 352, 704, 352)
MAT_OFF = tuple(sum(MAT_ROWS[:i]) for i in range(len(MAT_ROWS)))
SLAB_ROWS = sum(MAT_ROWS)
N_MAT = len(MAT_ROWS)

NT_DIMS = (((1,), (1,)), ((), ()))
NN_DIMS = (((1,), (0,)), ((), ()))
TN_DIMS = (((0,), (0,)), ((), ()))


def _params(*sem):
    return pltpu.CompilerParams(dimension_semantics=tuple(sem), vmem_limit_bytes=VMEM_LIMIT_BYTES)


def _row_tile(rows, cap, mult=8):
    best = None
    for t in range(mult, min(rows, cap) + 1, mult):
        if rows % t == 0:
            best = t
    return best if best is not None else rows


def _sigmoid(x):
    return 1.0 / (1.0 + jnp.exp(-x))


def _mm(name, a, b, *, nt, tm, tn, out_dtype, rms_g=None, bias=None, res=None):
    m, k = a.shape
    n = b.shape[0] if nt else b.shape[1]
    tm = min(tm, m)
    has_rms = rms_g is not None
    dims = NT_DIMS if nt else NN_DIMS

    def body(*refs):
        a_ref, b_ref = refs[0], refs[1]
        pos = 2
        g_ref = bias_ref = res_ref = h_ref = hs_ref = None
        if has_rms:
            g_ref = refs[pos]; pos += 1
        if bias is not None:
            bias_ref = refs[pos]; pos += 1
        if res is not None:
            res_ref = refs[pos]; pos += 1
        o_ref = refs[pos]; pos += 1
        if has_rms:
            h_ref, hs_ref = refs[pos], refs[pos + 1]

            @pl.when(pl.program_id(1) == 0)
            def _():
                xf = a_ref[...]
                r = lax.rsqrt(jnp.mean(xf * xf, axis=-1, keepdims=True) + RMS_EPS)
                h = ((xf * r) * g_ref[...]).astype(BF16)
                hs_ref[...] = h
                h_ref[...] = h

            lhs = hs_ref[...]
        else:
            lhs = a_ref[...].astype(BF16)
        acc = lax.dot_general(lhs, b_ref[...], dims, preferred_element_type=F32)
        if bias is not None:
            acc = acc + bias_ref[...]
        if res is not None:
            acc = acc + res_ref[...]
        o_ref[...] = acc.astype(out_dtype)

    in_specs = [pl.BlockSpec((tm, k), lambda i, j: (i, 0)),
                pl.BlockSpec((tn, k), lambda i, j: (j, 0)) if nt else pl.BlockSpec((k, tn), lambda i, j: (0, j))]
    args = [a, b]
    if has_rms:
        in_specs.append(pl.BlockSpec((1, k), lambda i, j: (0, 0))); args.append(rms_g)
    if bias is not None:
        in_specs.append(pl.BlockSpec((1, tn), lambda i, j: (0, j))); args.append(bias)
    if res is not None:
        in_specs.append(pl.BlockSpec((tm, tn), lambda i, j: (i, j))); args.append(res)
    out_shape = [jax.ShapeDtypeStruct((m, n), out_dtype)]
    out_specs = [pl.BlockSpec((tm, tn), lambda i, j: (i, j))]
    scratch = []
    if has_rms:
        out_shape.append(jax.ShapeDtypeStruct((m, k), BF16))
        out_specs.append(pl.BlockSpec((tm, k), lambda i, j: (i, 0)))
        scratch.append(pltpu.VMEM((tm, k), BF16))
    outs = pl.pallas_call(body, grid=(m // tm, n // tn), in_specs=in_specs, out_specs=out_specs, out_shape=out_shape,
                          scratch_shapes=scratch, compiler_params=_params("parallel", "arbitrary"), name=name)(*args)
    return outs if has_rms else outs[0]


def _mm_tn(name, l, r, *, tn, tt, colsum_l=False, colsum_r=False):
    s, nl = l.shape
    nr = r.shape[1]
    tt = min(tt, s)
    n_t = s // tt

    def body(*refs):
        l_ref, r_ref, o_ref = refs[0], refs[1], refs[2]
        pos = 3
        cl_ref = cr_ref = None
        if colsum_l:
            cl_ref = refs[pos]; pos += 1
        if colsum_r:
            cr_ref = refs[pos]; pos += 1
        acc_ref = refs[pos]
        j, t = pl.program_id(0), pl.program_id(1)

        @pl.when(t == 0)
        def _():
            acc_ref[...] = jnp.zeros_like(acc_ref)

        lv = l_ref[...]
        rv = r_ref[...]
        acc_ref[...] += lax.dot_general(lv, rv.astype(BF16), TN_DIMS, preferred_element_type=F32)
        if colsum_l:
            @pl.when(t == 0)
            def _():
                cl_ref[...] = jnp.zeros_like(cl_ref)

            cl_ref[...] += jnp.sum(lv.astype(F32), axis=0, keepdims=True)
        if colsum_r:
            @pl.when((t == 0) & (j == 0))
            def _():
                cr_ref[...] = jnp.zeros_like(cr_ref)

            @pl.when(j == 0)
            def _():
                cr_ref[...] += jnp.sum(rv.astype(F32), axis=0, keepdims=True)

        @pl.when(t == n_t - 1)
        def _():
            o_ref[...] = acc_ref[...].astype(BF16)

    out_shape = [jax.ShapeDtypeStruct((nl, nr), BF16)]
    out_specs = [pl.BlockSpec((tn, nr), lambda j, t: (j, 0))]
    if colsum_l:
        out_shape.append(jax.ShapeDtypeStruct((1, nl), F32))
        out_specs.append(pl.BlockSpec((1, tn), lambda j, t: (0, j)))
    if colsum_r:
        out_shape.append(jax.ShapeDtypeStruct((1, nr), F32))
        out_specs.append(pl.BlockSpec((1, nr), lambda j, t: (0, 0)))
    outs = pl.pallas_call(
        body, grid=(nl // tn, n_t),
        in_specs=[pl.BlockSpec((tt, tn), lambda j, t: (t, j)), pl.BlockSpec((tt, nr), lambda j, t: (t, 0))],
        out_specs=out_specs, out_shape=out_shape, scratch_shapes=[pltpu.VMEM((tn, nr), F32)],
        compiler_params=_params("arbitrary", "arbitrary"), name=name)(l, r)
    return outs if (colsum_l or colsum_r) else outs[0]


def _rms_bwd(dh, xf, g):
    r = lax.rsqrt(jnp.mean(xf * xf, axis=-1, keepdims=True) + RMS_EPS)
    gd = dh * g
    dx = r * gd - xf * ((r * r * r) * jnp.mean(xf * gd, axis=-1, keepdims=True))
    return dx, dh * (xf * r)


def _dgrad_rms(name, dy, w, x, g, dres, *, tm, tk):
    m, k = dy.shape
    tm = min(tm, m)
    n_k = k // tk

    def body(dy_ref, w_ref, x_ref, g_ref, dres_ref, o_ref, dg_ref, acc_ref):
        i, kk = pl.program_id(0), pl.program_id(1)

        @pl.when(kk == 0)
        def _():
            acc_ref[...] = jnp.zeros_like(acc_ref)

        acc_ref[...] += lax.dot_general(dy_ref[...], w_ref[...], NN_DIMS, preferred_element_type=F32)

        @pl.when((i == 0) & (kk == n_k - 1))
        def _():
            dg_ref[...] = jnp.zeros_like(dg_ref)

        @pl.when(kk == n_k - 1)
        def _():
            dx, dg_rows = _rms_bwd(acc_ref[...], x_ref[...], g_ref[...])
            o_ref[...] = dres_ref[...] + dx
            dg_ref[...] += jnp.sum(dg_rows, axis=0, keepdims=True)

    return pl.pallas_call(
        body, grid=(m // tm, n_k),
        in_specs=[pl.BlockSpec((tm, tk), lambda i, kk: (i, kk)), pl.BlockSpec((tk, D), lambda i, kk: (kk, 0)),
                  pl.BlockSpec((tm, D), lambda i, kk: (i, 0)), pl.BlockSpec((1, D), lambda i, kk: (0, 0)),
                  pl.BlockSpec((tm, D), lambda i, kk: (i, 0))],
        out_specs=[pl.BlockSpec((tm, D), lambda i, kk: (i, 0)), pl.BlockSpec((1, D), lambda i, kk: (0, 0))],
        out_shape=[jax.ShapeDtypeStruct((m, D), F32), jax.ShapeDtypeStruct((1, D), F32)],
        scratch_shapes=[pltpu.VMEM((tm, D), F32)],
        compiler_params=_params("arbitrary", "arbitrary"), name=name)(dy, w, x, g, dres)


def _loss_head(x4, g, tgt, *, tm):
    m = x4.shape[0]
    tm = min(tm, m)

    def body(x_ref, g_ref, t_ref, dx_ref, loss_ref, dg_ref):
        @pl.when(pl.program_id(0) == 0)
        def _():
            loss_ref[...] = jnp.zeros_like(loss_ref)
            dg_ref[...] = jnp.zeros_like(dg_ref)

        xf = x_ref[...]
        gv = g_ref[...]
        r = lax.rsqrt(jnp.mean(xf * xf, axis=-1, keepdims=True) + RMS_EPS)
        diff = (xf * r) * gv - t_ref[...]
        loss_ref[...] += 0.5 * jnp.sum(jnp.mean(diff * diff, axis=-1, keepdims=True))
        dx, dg_rows = _rms_bwd(diff * (1.0 / D), xf, gv)
        dx_ref[...] = dx
        dg_ref[...] += jnp.sum(dg_rows, axis=0, keepdims=True)

    return pl.pallas_call(
        body, grid=(m // tm,),
        in_specs=[pl.BlockSpec((tm, D), lambda i: (i, 0)), pl.BlockSpec((1, D), lambda i: (0, 0)),
                  pl.BlockSpec((tm, D), lambda i: (i, 0))],
        out_specs=[pl.BlockSpec((tm, D), lambda i: (i, 0)), pl.BlockSpec((1, 128), lambda i: (0, 0)),
                   pl.BlockSpec((1, D), lambda i: (0, 0))],
        out_shape=[jax.ShapeDtypeStruct((m, D), F32), jax.ShapeDtypeStruct((1, 128), F32),
                   jax.ShapeDtypeStruct((1, D), F32)],
        compiler_params=_params("arbitrary"), name="loss_head")(x4, g, tgt)


def _band(kvp, kvc):
    kk = jnp.concatenate([kvp[:, :128], kvc[:, :128]], axis=0)
    vv = jnp.concatenate([kvp[:, 128:], kvc[:, 128:]], axis=0)
    row = lax.broadcasted_iota(jnp.int32, kk.shape, 0)
    return jnp.where(row == 0, jnp.zeros_like(kk), kk), jnp.where(row == 0, jnp.zeros_like(vv), vv)


def _blockdiag(t, h):
    lane = lax.broadcasted_iota(jnp.int32, t.shape, 1)
    z = jnp.zeros_like(t)
    tr = pltpu.roll(t, 64, 1)
    if h == 0:
        top, bot = jnp.where(lane < 64, t, z), jnp.where(lane >= 64, tr, z)
    else:
        top, bot = jnp.where(lane < 64, tr, z), jnp.where(lane >= 64, t, z)
    return jnp.concatenate([top, bot], axis=0)


def _from_blockdiag(t, h):
    lane = lax.broadcasted_iota(jnp.int32, (256, 128), 1)
    top, bot = t[:256], t[256:]
    if h == 0:
        return jnp.where(lane < 64, top + pltpu.roll(bot, 64, 1), 0.0)
    return jnp.where(lane >= 64, pltpu.roll(top, 64, 1) + bot, 0.0)


def _stack_heads(ref, h):
    return jnp.concatenate([ref[:, h * 512 + p * 128: h * 512 + (p + 1) * 128] for p in range(4)], axis=0)


def _masked_scores(sc_half, n, sink):
    row = lax.broadcasted_iota(jnp.int32, (512, 256), 0) & 127
    col = lax.broadcasted_iota(jnp.int32, (512, 256), 1)
    valid = (col > row) & (col <= row + BLOCK) & ((col >= BLOCK) | (n > 0))
    return jnp.where(col == 0, sink, jnp.where(valid, sc_half, NEG))


def _half_selector(shape, split):
    row = lax.broadcasted_iota(jnp.int32, shape, 0)
    one, zero = jnp.ones(shape, BF16), jnp.zeros(shape, BF16)
    return jnp.where(row < split, one, zero), jnp.where(row >= split, one, zero)


def _attn_fwd(qkv, sink_full, ex=None, ex_args=()):
    s = qkv.shape[0]
    nb = s // BLOCK
    scale = 1.0 / math.sqrt(HEAD_DIM)

    def body(q_ref, kvc_ref, kvp_ref, sk_ref, o_ref, lse_ref):
        n = pl.program_id(0)
        kk, vv = _band(kvp_ref[...], kvc_ref[...])
        sel0, sel1 = _half_selector((512, 128), 256)
        lane = lax.broadcasted_iota(jnp.int32, (512, 128), 1)
        for h in range(N_KV):
            kbd, vbd = _blockdiag(kk, h), _blockdiag(vv, h)
            sc = lax.dot_general(_stack_heads(q_ref, h), kbd, NT_DIMS, preferred_element_type=F32) * scale
            pes, mxs = [], []
            for half in range(2):
                sh = _masked_scores(sc[:, half * 256:(half + 1) * 256], n, sk_ref[h * 2 + half])
                mx = jnp.max(sh, axis=-1, keepdims=True)
                pes.append(jnp.exp(sh - mx).astype(BF16))
                mxs.append(mx)
            pb = jnp.concatenate(pes, axis=1)
            o_un = lax.dot_general(pb, vbd, NN_DIMS, preferred_element_type=F32)
            sums = [lax.dot_general(pb, sel, NN_DIMS, preferred_element_type=F32) for sel in (sel0, sel1)]
            o = o_un * jnp.where(lane < 64, 1.0 / sums[0], 1.0 / sums[1])
            for half in range(2):
                lse_ref[h * 2 + half] = mxs[half] + jnp.log(sums[half])
            for p in range(4):
                o_ref[:, h * 512 + p * 128: h * 512 + (p + 1) * 128] = o[p * 128:(p + 1) * 128].astype(BF16)

    return pl.pallas_call(
        _carried(body, 4, 2, ex, nb), grid=(nb,),
        in_specs=[pl.BlockSpec((BLOCK, NQ), lambda n: (n, 0)),
                  pl.BlockSpec((BLOCK, KVW), lambda n: (n, NQ // KVW)),
                  pl.BlockSpec((BLOCK, KVW), lambda n: (jnp.maximum(n - 1, 0), NQ // KVW)),
                  pl.BlockSpec((4, 512, 256), lambda n: (0, 0, 0))] + _ex(ex, "in_specs"),
        out_specs=[pl.BlockSpec((BLOCK, NQ), lambda n: (n, 0)),
                   pl.BlockSpec((None, 4, 512, 128), lambda n: (n, 0, 0, 0))] + _ex(ex, "out_specs"),
        out_shape=[jax.ShapeDtypeStruct((s, NQ), BF16), jax.ShapeDtypeStruct((nb, 4, 512, 128), F32)] + _ex(ex, "out_shape"),
        scratch_shapes=_ex(ex, "scratch"),
        compiler_params=_params("arbitrary"), name="attn_fwd")(qkv, qkv, qkv, sink_full, *ex_args)


def _attn_bwd(qkv, do, o, lse, sink_full, ex=None, ex_args=()):
    s = qkv.shape[0]
    nb = s // BLOCK
    scale = 1.0 / math.sqrt(HEAD_DIM)

    def body(q_ref, kvc_ref, kvp_ref, do_ref, o_ref, lse_ref, sk_ref, dq_ref, dkv_ref, dsk_ref, ck_ref, cv_ref):
        n = pl.program_id(0)

        @pl.when(n == 0)
        def _():
            dsk_ref[...] = jnp.zeros_like(dsk_ref)
            ck_ref[...] = jnp.zeros_like(ck_ref)
            cv_ref[...] = jnp.zeros_like(cv_ref)

        @pl.when(n < nb)
        def _():
            kk, vv = _band(kvp_ref[...], kvc_ref[...])
            sel0, sel1 = _half_selector((128, 128), 64)
            dkk = jnp.zeros((256, 128), F32)
            dvv = jnp.zeros((256, 128), F32)
            for h in range(N_KV):
                kbd, vbd = _blockdiag(kk, h), _blockdiag(vv, h)
                qp = _stack_heads(q_ref, h)
                dop = _stack_heads(do_ref, h)
                sc = lax.dot_general(qp, kbd, NT_DIMS, preferred_element_type=F32) * scale
                dp = lax.dot_general(dop, vbd, NT_DIMS, preferred_element_type=F32)
                dd = (dop.astype(F32) * _stack_heads(o_ref, h).astype(F32)).astype(BF16)
                probs, dss = [], []
                for half, sel in enumerate((sel0, sel1)):
                    delta = lax.dot_general(dd, sel, NN_DIMS, preferred_element_type=F32)
                    lse_h = lse_ref[h * 2 + half]
                    sh = _masked_scores(sc[:, half * 256:(half + 1) * 256], n, sk_ref[h * 2 + half])
                    pr = jnp.exp(sh - jnp.concatenate([lse_h, lse_h], axis=1))
                    dsf = pr * (dp[:, half * 256:(half + 1) * 256] - jnp.concatenate([delta, delta], axis=1))
                    dsk_ref[h * 2 + half] += dsf[:, 0:1]
                    dss.append((dsf * scale).astype(BF16))
                    probs.append(pr.astype(BF16))
                ds = jnp.concatenate(dss, axis=1)
                pb = jnp.concatenate(probs, axis=1)
                dqp = lax.dot_general(ds, kbd, NN_DIMS, preferred_element_type=F32)
                for p in range(4):
                    dq_ref[:, h * 512 + p * 128: h * 512 + (p + 1) * 128] = dqp[p * 128:(p + 1) * 128].astype(BF16)
                dkk = dkk + _from_blockdiag(lax.dot_general(ds, qp, TN_DIMS, preferred_element_type=F32), h)
                dvv = dvv + _from_blockdiag(lax.dot_general(pb, dop, TN_DIMS, preferred_element_type=F32), h)
            row = lax.broadcasted_iota(jnp.int32, (256, 128), 0)
            dkk = jnp.where(row == 0, 0.0, dkk)
            dvv = jnp.where(row == 0, 0.0, dvv)

            @pl.when(n >= 1)
            def _():
                dkv_ref[:, :128] = (ck_ref[...] + dkk[:128]).astype(BF16)
                dkv_ref[:, 128:] = (cv_ref[...] + dvv[:128]).astype(BF16)

            ck_ref[...] = dkk[128:]
            cv_ref[...] = dvv[128:]

        @pl.when(n == nb)
        def _():
            dkv_ref[:, :128] = ck_ref[...].astype(BF16)
            dkv_ref[:, 128:] = cv_ref[...].astype(BF16)

    last = nb - 1
    blk = lambda n: (jnp.minimum(n, last), 0)
    return pl.pallas_call(
        _carried(body, 7, 3, ex, nb + 1), grid=(nb + 1,),
        in_specs=[pl.BlockSpec((BLOCK, NQ), blk),
                  pl.BlockSpec((BLOCK, KVW), lambda n: (jnp.minimum(n, last), NQ // KVW)),
                  pl.BlockSpec((BLOCK, KVW), lambda n: (jnp.clip(n - 1, 0, last), NQ // KVW)),
                  pl.BlockSpec((BLOCK, NQ), blk), pl.BlockSpec((BLOCK, NQ), blk),
                  pl.BlockSpec((None, 4, 512, 128), lambda n: (jnp.minimum(n, last), 0, 0, 0)),
                  pl.BlockSpec((4, 512, 256), lambda n: (0, 0, 0))] + _ex(ex, "in_specs"),
        out_specs=[pl.BlockSpec((BLOCK, NQ), blk),
                   pl.BlockSpec((BLOCK, KVW), lambda n: (jnp.maximum(n - 1, 0), 0)),
                   pl.BlockSpec((4, 512, 1), lambda n: (0, 0, 0))] + _ex(ex, "out_specs"),
        out_shape=[jax.ShapeDtypeStruct((s, NQ), BF16), jax.ShapeDtypeStruct((s, KVW), BF16),
                   jax.ShapeDtypeStruct((4, 512, 1), F32)] + _ex(ex, "out_shape"),
        scratch_shapes=[pltpu.VMEM((BLOCK, 128), F32), pltpu.VMEM((BLOCK, 128), F32)] + _ex(ex, "scratch"),
        compiler_params=_params("arbitrary"), name="attn_bwd")(qkv, qkv, qkv, do, o, lse, sink_full, *ex_args)


LANE = 128
FFN_HALF = F // 2
FFN_PAIR = 2 * LANE
FFN_PAIRS = F // FFN_PAIR
FFN_GROUPS = ((0, 3), (3, 6), (6, 9), (9, 11))
FFN_HALO = 8


def _resident(shape):
    return pl.BlockSpec(shape, lambda i: (0,) * len(shape), pipeline_mode=pl.Buffered(1))


def _ffn_fwd(tag, x_in, g, w_up_t, w_dw, b_dw, w_down, *, tm, ex=None, ex_args=()):
    s = x_in.shape[0]
    tm = min(tm, s)

    def body(x_ref, g_ref, wup_ref, wdw_ref, bdw_ref, wd_ref, o_ref, h_ref, up_ref, act_ref, gate_ref, carry_ref, ext_ref):
        @pl.when(pl.program_id(0) == 0)
        def _():
            carry_ref[...] = jnp.zeros_like(carry_ref)

        xf = x_ref[...]
        r = lax.rsqrt(jnp.mean(xf * xf, axis=-1, keepdims=True) + RMS_EPS)
        h = ((xf * r) * g_ref[...]).astype(BF16)
        h_ref[...] = h

        def up_mm(p):
            for off in (p * FFN_PAIR, F + p * FFN_PAIR):
                rs = slice(off, off + FFN_PAIR)
                up_ref[:, rs] = lax.dot_general(h, wup_ref[rs, :], NT_DIMS, preferred_element_type=F32).astype(BF16)

        def elementwise(p):
            for c in (2 * p, 2 * p + 1):
                cs = slice(c * LANE, (c + 1) * LANE)
                vs = slice(F + c * LANE, F + (c + 1) * LANE)
                gcol = up_ref[:, cs].astype(F32)
                ext_ref[pl.ds(0, FFN_HALO), :] = carry_ref[:, cs]
                ext_ref[pl.ds(FFN_HALO, tm), :] = gcol
                gate_b = (wdw_ref[2:3, cs] * gcol + wdw_ref[1:2, cs] * ext_ref[pl.ds(FFN_HALO - 1, tm), :]
                          + wdw_ref[0:1, cs] * ext_ref[pl.ds(FFN_HALO - 2, tm), :] + bdw_ref[:, cs]).astype(BF16)
                gate_ref[:, cs] = gate_b
                gate = gate_b.astype(F32)
                act_ref[:, cs] = (gate * _sigmoid(gate) * up_ref[:, vs].astype(F32)).astype(BF16)
                carry_ref[:, cs] = gcol[tm - FFN_HALO:]

        def down_mm(lo, hi):
            ks = slice(lo * FFN_PAIR, hi * FFN_PAIR)
            return lax.dot_general(act_ref[:, ks], wd_ref[ks, :], NN_DIMS, preferred_element_type=F32)

        acc = xf
        up_mm(0)
        pending = None
        for lo, hi in FFN_GROUPS:
            for p in range(lo, hi):
                if p + 1 < FFN_PAIRS:
                    up_mm(p + 1)
                if pending is not None:
                    acc = acc + down_mm(*pending)
                    pending = None
                elementwise(p)
            pending = (lo, hi)
        o_ref[...] = acc + down_mm(*pending)

    row = lambda i: (i, 0)
    return pl.pallas_call(
        _carried(body, 6, 5, ex, s // tm), grid=(s // tm,),
        in_specs=[pl.BlockSpec((tm, D), row), _resident((1, D)), _resident((2 * F, D)), _resident((3, F)),
                  _resident((1, F)), _resident((F, D))] + _ex(ex, "in_specs"),
        out_specs=[pl.BlockSpec((tm, D), row), pl.BlockSpec((tm, D), row), pl.BlockSpec((tm, 2 * F), row),
                   pl.BlockSpec((tm, F), row), pl.BlockSpec((tm, F), row)] + _ex(ex, "out_specs"),
        out_shape=[jax.ShapeDtypeStruct((s, D), F32), jax.ShapeDtypeStruct((s, D), BF16),
                   jax.ShapeDtypeStruct((s, 2 * F), BF16), jax.ShapeDtypeStruct((s, F), BF16),
                   jax.ShapeDtypeStruct((s, F), BF16)] + _ex(ex, "out_shape"),
        scratch_shapes=[pltpu.VMEM((FFN_HALO, F), F32), pltpu.VMEM((tm + FFN_HALO, LANE), F32)] + _ex(ex, "scratch"),
        compiler_params=_params("arbitrary"), name=f"ffn{tag}_fwd")(x_in, g, w_up_t, w_dw, b_dw, w_down, *ex_args)


def _ffn_bwd(tag, dx_out, x_in, g, w_up_t, w_dw, w_down, up, gate, *, tm):
    s = x_in.shape[0]
    tm = min(tm, s)
    n_i = s // tm

    def body(dx_ref, x_ref, g_ref, up_ref, gate_ref, wup_ref, wdw_ref, wd_ref,
             dxin_ref, dup_ref, dg_ref, dwb_ref, carry_ref, da0_ref, da1_ref, extd_ref):
        i = pl.program_id(0)

        @pl.when(i == 0)
        def _():
            carry_ref[...] = jnp.zeros_like(carry_ref)
            dg_ref[...] = jnp.zeros_like(dg_ref)
            dwb_ref[...] = jnp.zeros_like(dwb_ref)

        dxf = dx_ref[...]
        dxb = dxf.astype(BF16)
        da_refs = (da0_ref, da1_ref)

        def da_mm(p):
            ks = slice(p * FFN_PAIR, (p + 1) * FFN_PAIR)
            da_refs[p % 2][...] = lax.dot_general(dxb, wd_ref[ks, :], NT_DIMS, preferred_element_type=F32)

        def elementwise(p):
            da_ref = da_refs[p % 2]
            for c in range(2):
                cc = 2 * p + c
                cs = slice(cc * LANE, (cc + 1) * LANE)
                vs = slice(F + cc * LANE, F + (cc + 1) * LANE)
                gate = gate_ref[:, cs].astype(F32)
                sg = _sigmoid(gate)
                silu = gate * sg
                dac = da_ref[:, c * LANE:(c + 1) * LANE]
                dup_ref[:, vs] = (dac * silu).astype(BF16)
                dgate = dac * up_ref[:, vs].astype(F32) * (sg + silu * (1.0 - sg))
                extd_ref[pl.ds(0, tm), :] = dgate
                extd_ref[pl.ds(tm, FFN_HALO), :] = carry_ref[:, cs]
                d1 = extd_ref[pl.ds(1, tm), :]
                d2 = extd_ref[pl.ds(2, tm), :]
                dup_ref[:, cs] = (wdw_ref[2:3, cs] * dgate + wdw_ref[1:2, cs] * d1 + wdw_ref[0:1, cs] * d2).astype(BF16)
                carry_ref[:, cs] = dgate[:FFN_HALO]
                gcol = up_ref[:, cs].astype(F32)
                dwb_ref[2, :, cs] += _sum8(dgate * gcol)
                dwb_ref[1, :, cs] += _sum8(d1 * gcol)
                dwb_ref[0, :, cs] += _sum8(d2 * gcol)
                dwb_ref[3, :, cs] += _sum8(dgate)

        def dh_mm(lo, hi):
            ks = slice(lo * FFN_PAIR, hi * FFN_PAIR)
            vks = slice(F + lo * FFN_PAIR, F + hi * FFN_PAIR)
            return (lax.dot_general(dup_ref[:, ks], wup_ref[ks, :], NN_DIMS, preferred_element_type=F32)
                    + lax.dot_general(dup_ref[:, vks], wup_ref[vks, :], NN_DIMS, preferred_element_type=F32))

        acc = None
        da_mm(0)
        pending = None
        for lo, hi in FFN_GROUPS:
            for p in range(lo, hi):
                if p + 1 < FFN_PAIRS:
                    da_mm(p + 1)
                if pending is not None:
                    part = dh_mm(*pending)
                    acc = part if acc is None else acc + part
                    pending = None
                elementwise(p)
            pending = (lo, hi)
        acc = acc + dh_mm(*pending)
        dx, dg_rows = _rms_bwd(acc, x_ref[...], g_ref[...])
        dxin_ref[...] = dxf + dx
        dg_ref[...] += jnp.sum(dg_rows, axis=0, keepdims=True)

    rev = lambda i: (n_i - 1 - i, 0)
    return pl.pallas_call(
        body, grid=(n_i,),
        in_specs=[pl.BlockSpec((tm, D), rev), pl.BlockSpec((tm, D), rev), _resident((1, D)),
                  pl.BlockSpec((tm, 2 * F), rev), pl.BlockSpec((tm, F), rev),
                  _resident((2 * F, D)), _resident((3, F)), _resident((F, D))],
        out_specs=[pl.BlockSpec((tm, D), rev), pl.BlockSpec((tm, 2 * F), rev),
                   pl.BlockSpec((1, D), lambda i: (0, 0)), pl.BlockSpec((4, 8, F), lambda i: (0, 0, 0))],
        out_shape=[jax.ShapeDtypeStruct((s, D), F32), jax.ShapeDtypeStruct((s, 2 * F), BF16),
                   jax.ShapeDtypeStruct((1, D), F32), jax.ShapeDtypeStruct((4, 8, F), F32)],
        scratch_shapes=[pltpu.VMEM((FFN_HALO, F), F32), pltpu.VMEM((tm, FFN_PAIR), F32), pltpu.VMEM((tm, FFN_PAIR), F32),
                        pltpu.VMEM((tm + FFN_HALO, LANE), F32)],
        compiler_params=_params("arbitrary"), name=f"ffn{tag}_bwd")(dx_out, x_in, g, up, gate, w_up_t, w_dw, w_down)


CONV_HALO = 32
CONV_CHUNK = 256
CONV_ROWS = 64


def _ln(u2, g, b):
    mu = jnp.mean(u2, axis=-1, keepdims=True)
    xc = u2 - mu
    rstd = lax.rsqrt(jnp.mean(xc * xc, axis=-1, keepdims=True) + LN_EPS)
    xhat = xc * rstd
    return xhat, rstd, xhat * g + b


def _phase_taps(ph):
    first = CONV_HALO - (CONV_K - 1)
    return [(k, first + k - ph) for k in range(CONV_K) if (first + k) % 8 == ph]


def _sum8(v):
    out = v[0:8]
    for j in range(8, v.shape[0], 8):
        out = out + v[j:j + 8]
    return out


def _store_phases(src_ref, sh_ref, tt, cols=slice(None)):
    sh_ref[0] = src_ref[:, cols]
    for ph in range(1, 8):
        sh_ref[ph, pl.ds(0, tt + CONV_HALO - 8), :] = src_ref[pl.ds(ph, tt + CONV_HALO - 8), cols]


def _conv_taps_fwd(w_ref, sh_ref, u2_ref, cs, tt):
    for base in range(0, tt, CONV_ROWS):
        acc = jnp.zeros((CONV_ROWS, CONV_CHUNK), F32) + w_ref[CONV_K:CONV_K + 1, cs]
        for ph in range(8):
            for k, off in _phase_taps(ph):
                acc = acc + w_ref[k:k + 1, cs] * sh_ref[ph, pl.ds(base + off, CONV_ROWS), :]
        u2_ref[pl.ds(base, CONV_ROWS), cs] = acc.astype(BF16)


def _conv_fwd(x_in, g, w1t, b1, wdw, ln_g, ln_b, w2, b2, *, tt):
    s = x_in.shape[0]
    tt = min(tt, s)
    n_c = D // CONV_CHUNK

    def body(x_ref, g_ref, w1_ref, b1_ref, w_ref, lg_ref, lb_ref, w2_ref, b2_ref,
             o_ref, h_ref, a_ref, u2_ref, s_ref, carry_ref, ext_ref, sh_ref):
        @pl.when(pl.program_id(0) == 0)
        def _():
            carry_ref[...] = jnp.zeros_like(carry_ref)

        xf = x_ref[...]
        r = lax.rsqrt(jnp.mean(xf * xf, axis=-1, keepdims=True) + RMS_EPS)
        h = ((xf * r) * g_ref[...]).astype(BF16)
        h_ref[...] = h

        def pw1_mm(c):
            for off in (c * CONV_CHUNK, D + c * CONV_CHUNK):
                rs = slice(off, off + CONV_CHUNK)
                a_ref[:, rs] = (lax.dot_general(h, w1_ref[rs, :], NT_DIMS, preferred_element_type=F32)
                                + b1_ref[:, rs]).astype(BF16)

        pw1_mm(0)
        for c in range(n_c):
            if c + 1 < n_c:
                pw1_mm(c + 1)
            cs = slice(c * CONV_CHUNK, (c + 1) * CONV_CHUNK)
            gs = slice(D + c * CONV_CHUNK, D + (c + 1) * CONV_CHUNK)
            u = a_ref[:, cs].astype(F32) * _sigmoid(a_ref[:, gs].astype(F32))
            ext_ref[pl.ds(0, CONV_HALO), :] = carry_ref[:, cs]
            ext_ref[pl.ds(CONV_HALO, tt), :] = u
            carry_ref[:, cs] = u[tt - CONV_HALO:]
            _store_phases(ext_ref, sh_ref, tt)
            _conv_taps_fwd(w_ref, sh_ref, u2_ref, cs, tt)
        _, _, ln = _ln(u2_ref[...].astype(F32), lg_ref[...], lb_ref[...])
        sv = (ln * _sigmoid(ln)).astype(BF16)
        s_ref[...] = sv
        o_ref[...] = xf + lax.dot_general(sv, w2_ref[...], NN_DIMS, preferred_element_type=F32) + b2_ref[...]

    row = lambda i: (i, 0)
    return pl.pallas_call(
        body, grid=(s // tt,),
        in_specs=[pl.BlockSpec((tt, D), row), _resident((1, D)), _resident((2 * D, D)), _resident((1, 2 * D)),
                  _resident((32, D)), _resident((1, D)), _resident((1, D)), _resident((D, D)), _resident((1, D))],
        out_specs=[pl.BlockSpec((tt, D), row), pl.BlockSpec((tt, D), row), pl.BlockSpec((tt, 2 * D), row),
                   pl.BlockSpec((tt, D), row), pl.BlockSpec((tt, D), row)],
        out_shape=[jax.ShapeDtypeStruct((s, D), F32), jax.ShapeDtypeStruct((s, D), BF16),
                   jax.ShapeDtypeStruct((s, 2 * D), BF16), jax.ShapeDtypeStruct((s, D), BF16),
                   jax.ShapeDtypeStruct((s, D), BF16)],
        scratch_shapes=[pltpu.VMEM((CONV_HALO, D), F32), pltpu.VMEM((tt + CONV_HALO, CONV_CHUNK), F32),
                        pltpu.VMEM((8, tt + CONV_HALO, CONV_CHUNK), F32)],
        compiler_params=_params("arbitrary"), name="conv_fwd")(x_in, g, w1t, b1, wdw, ln_g, ln_b, w2, b2)


def _conv_bwd(dx_out, x_in, g, a, u2, w1t, wdw, ln_g, ln_b, w2, *, tt, ex=None, ex_args=()):
    s = x_in.shape[0]
    tt = min(tt, s)
    hb = tt // CONV_HALO
    n_i = s // tt
    te = tt + CONV_HALO
    n_c = D // CONV_CHUNK

    def body(dx_ref, x_ref, g_ref, a_ref, prev_ref, u2_ref, w1_ref, w_ref, lg_ref, lb_ref, w2_ref,
             dxin_ref, da_ref, dg_ref, acc_ref, db1_ref, db2_ref,
             carry_ref, e2_ref, ext_ref, ush_ref, dsh_ref, du_ref):
        i = pl.program_id(0)

        @pl.when(i == 0)
        def _():
            for ref in (dg_ref, acc_ref, db1_ref, db2_ref, carry_ref):
                ref[...] = jnp.zeros_like(ref)

        dxf = dx_ref[...]
        db2_ref[...] += _sum8(dxf)
        dse = lax.dot_general(dxf.astype(BF16), w2_ref[...], NT_DIMS, preferred_element_type=F32)
        gv = lg_ref[...]
        xhat, rstd, ln = _ln(u2_ref[...].astype(F32), gv, lb_ref[...])
        sg = _sigmoid(ln)
        dln = dse * (sg * (1.0 + ln * (1.0 - sg)))
        acc_ref[32] += _sum8(dln * xhat)
        acc_ref[33] += _sum8(dln)
        dxh = dln * gv
        du2 = rstd * (dxh - jnp.mean(dxh, axis=-1, keepdims=True) - xhat * jnp.mean(dxh * xhat, axis=-1, keepdims=True))
        e2_ref[pl.ds(0, tt), :] = du2
        e2_ref[pl.ds(tt, CONV_HALO), :] = carry_ref[...]
        carry_ref[...] = du2[:CONV_HALO]
        first_tile = i == n_i - 1

        def dh_mm(c):
            cs = slice(c * CONV_CHUNK, (c + 1) * CONV_CHUNK)
            gs = slice(D + c * CONV_CHUNK, D + (c + 1) * CONV_CHUNK)
            return (lax.dot_general(da_ref[:, cs], w1_ref[cs, :], NN_DIMS, preferred_element_type=F32)
                    + lax.dot_general(da_ref[:, gs], w1_ref[gs, :], NN_DIMS, preferred_element_type=F32))

        dh = None
        for c in range(n_c):
            if c >= 1:
                part = dh_mm(c - 1)
                dh = part if dh is None else dh + part
            cs = slice(c * CONV_CHUNK, (c + 1) * CONV_CHUNK)
            gs = slice(D + c * CONV_CHUNK, D + (c + 1) * CONV_CHUNK)
            uh = prev_ref[:, cs].astype(F32) * _sigmoid(prev_ref[:, gs].astype(F32))
            ext_ref[pl.ds(0, CONV_HALO), :] = jnp.where(first_tile, 0.0, uh)
            a1 = a_ref[:, cs].astype(F32)
            sg2 = _sigmoid(a_ref[:, gs].astype(F32))
            ext_ref[pl.ds(CONV_HALO, tt), :] = a1 * sg2
            _store_phases(ext_ref, ush_ref, tt)
            _store_phases(e2_ref, dsh_ref, tt, cs)
            for base in range(0, tt, CONV_ROWS):
                d0 = e2_ref[pl.ds(base, CONV_ROWS), cs]
                du = jnp.zeros((CONV_ROWS, CONV_CHUNK), F32)
                for ph in range(8):
                    for k, off in _phase_taps(ph):
                        acc_ref[k, :, cs] += _sum8(d0 * ush_ref[ph, pl.ds(base + off, CONV_ROWS), :])
                    for k in range(CONV_K):
                        if (CONV_K - 1 - k) % 8 == ph:
                            du = du + w_ref[k:k + 1, cs] * dsh_ref[ph, pl.ds(base + CONV_K - 1 - k - ph, CONV_ROWS), :]
                acc_ref[CONV_K, :, cs] += _sum8(d0)
                du_ref[pl.ds(base, CONV_ROWS), :] = du
            du = du_ref[...]
            da1 = du * sg2
            da2 = du * a1 * (sg2 * (1.0 - sg2))
            da_ref[:, cs] = da1.astype(BF16)
            da_ref[:, gs] = da2.astype(BF16)
            db1_ref[:, cs] += _sum8(da1)
            db1_ref[:, gs] += _sum8(da2)
        dh = dh + dh_mm(n_c - 1)
        dx, dg_rows = _rms_bwd(dh, x_ref[...], g_ref[...])
        dxin_ref[...] = dxf + dx
        dg_ref[...] += jnp.sum(dg_rows, axis=0, keepdims=True)

    rev = lambda i: (n_i - 1 - i, 0)
    return pl.pallas_call(
        _carried(body, 11, 6, ex, n_i), grid=(n_i,),
        in_specs=[pl.BlockSpec((tt, D), rev), pl.BlockSpec((tt, D), rev), _resident((1, D)),
                  pl.BlockSpec((tt, 2 * D), rev),
                  pl.BlockSpec((CONV_HALO, 2 * D), lambda i: (jnp.maximum((n_i - 1 - i) * hb - 1, 0), 0)),
                  pl.BlockSpec((tt, D), rev), _resident((2 * D, D)), _resident((32, D)), _resident((1, D)),
                  _resident((1, D)), _resident((D, D))] + _ex(ex, "in_specs"),
        out_specs=[pl.BlockSpec((tt, D), rev), pl.BlockSpec((tt, 2 * D), rev), pl.BlockSpec((1, D), lambda i: (0, 0)),
                   pl.BlockSpec((40, 8, D), lambda i: (0, 0, 0)), pl.BlockSpec((8, 2 * D), lambda i: (0, 0)),
                   pl.BlockSpec((8, D), lambda i: (0, 0))] + _ex(ex, "out_specs"),
        out_shape=[jax.ShapeDtypeStruct((s, D), F32), jax.ShapeDtypeStruct((s, 2 * D), BF16),
                   jax.ShapeDtypeStruct((1, D), F32), jax.ShapeDtypeStruct((40, 8, D), F32),
                   jax.ShapeDtypeStruct((8, 2 * D), F32), jax.ShapeDtypeStruct((8, D), F32)] + _ex(ex, "out_shape"),
        scratch_shapes=[pltpu.VMEM((CONV_HALO, D), F32), pltpu.VMEM((te, D), F32), pltpu.VMEM((te, CONV_CHUNK), F32),
                        pltpu.VMEM((8, te, CONV_CHUNK), F32), pltpu.VMEM((8, te, CONV_CHUNK), F32),
                        pltpu.VMEM((tt, CONV_CHUNK), F32)] + _ex(ex, "scratch"),
        compiler_params=_params("arbitrary"), name="conv_bwd")(dx_out, x_in, g, a, a, u2, w1t, wdw, ln_g, ln_b, w2, *ex_args)


def _adamw(name, w, g, m, v):
    rows, cols = w.shape
    tr = _row_tile(rows, 256)

    def body(w_ref, g_ref, m_ref, v_ref, d_ref, nm_ref, nv_ref):
        gv = g_ref[...]
        m2 = ADAM_B1 * m_ref[...] + (1.0 - ADAM_B1) * gv
        v2 = ADAM_B2 * v_ref[...] + (1.0 - ADAM_B2) * (gv * gv)
        m_hat = m2 / (1.0 - ADAM_B1 ** ADAM_STEP)
        v_hat = v2 / (1.0 - ADAM_B2 ** ADAM_STEP)
        d_ref[...] = -ADAM_LR * (m_hat / (jnp.sqrt(v_hat) + ADAM_EPS) + ADAM_WD * w_ref[...])
        nm_ref[...] = m2
        nv_ref[...] = v2

    spec = pl.BlockSpec((tr, cols), lambda i: (i, 0))
    return pl.pallas_call(
        body, grid=(rows // tr,), in_specs=[spec] * 4, out_specs=[spec] * 3,
        out_shape=[jax.ShapeDtypeStruct((rows, cols), F32)] * 3,
        compiler_params=_params("parallel"), name=name)(w, g, m, v)


def _sum_slabs(name, buf):
    rows = buf.shape[1]
    tr = _row_tile(rows, 512, 16)

    def body(b_ref, o_ref):
        acc = b_ref[0].astype(F32)
        for k in range(1, N_DEV):
            acc = acc + b_ref[k].astype(F32)
        o_ref[...] = acc

    return pl.pallas_call(
        body, grid=(rows // tr,), in_specs=[pl.BlockSpec((N_DEV, tr, D), lambda i: (0, i, 0))],
        out_specs=pl.BlockSpec((tr, D), lambda i: (i, 0)), out_shape=jax.ShapeDtypeStruct((rows, D), F32),
        compiler_params=_params("parallel"), name=name)(buf)


def _ex(ex, field):
    return list(getattr(ex, field)) if ex is not None else []


def _me():
    x, y, c = lax.axis_index("x"), lax.axis_index("y"), lax.axis_index("c")
    return x, y, c, 4 * x + 2 * y + c


def _peer(x, y, c, k):
    return (x ^ (k >> 2), y ^ ((k >> 1) & 1), c ^ (k & 1))


class _Gather:
    def __init__(self, mats):
        self.mats = tuple(mats)
        self.rows = sum(MAT_ROWS[i] for i in self.mats)
        any_spec = pl.BlockSpec(memory_space=pl.ANY)
        self.n_in, self.n_out = 1, len(self.mats)
        self.in_specs = [any_spec]
        self.out_specs = [any_spec] * self.n_out
        self.out_shape = [jax.ShapeDtypeStruct((N_DEV * MAT_ROWS[i], D), BF16) for i in self.mats]
        self.scratch = [pltpu.SemaphoreType.DMA((N_DEV - 1,)), pltpu.SemaphoreType.DMA((N_DEV - 1,)),
                        pltpu.SemaphoreType.DMA]
        assert max(s.shape[0] for s in self.out_shape) >= self.rows

    def start(self, ins, outs, scr):
        (slab_ref,), (send_sems, recv_sems, local_sem) = ins, scr
        x, y, c, me = _me()
        for n, i in enumerate(self.mats):
            src = slab_ref.at[pl.ds(MAT_OFF[i], MAT_ROWS[i])]
            dst = outs[n].at[pl.ds(me * MAT_ROWS[i], MAT_ROWS[i])]
            pltpu.make_async_copy(src, dst, local_sem).start()
            for k in range(1, N_DEV):
                pltpu.make_async_remote_copy(src_ref=src, dst_ref=dst, send_sem=send_sems.at[k - 1],
                                             recv_sem=recv_sems.at[k - 1], device_id=_peer(x, y, c, k),
                                             device_id_type=MESH).start()

    def wait(self, ins, outs, scr):
        send_sems, recv_sems, local_sem = scr
        x, y, c, _ = _me()
        big = max(range(self.n_out), key=lambda n: self.out_shape[n].shape[0])
        whole = outs[big].at[pl.ds(0, self.rows)]
        for k in range(1, N_DEV):
            pltpu.make_async_remote_copy(src_ref=whole, dst_ref=whole, send_sem=send_sems.at[k - 1],
                                         recv_sem=recv_sems.at[k - 1], device_id=_peer(x, y, c, k),
                                         device_id_type=MESH).wait()
        pltpu.make_async_copy(whole, whole, local_sem).wait()


class _Scatter:
    def __init__(self, mats):
        self.mats = tuple(mats)
        self.rows = sum(MAT_ROWS[i] for i in self.mats)
        any_spec = pl.BlockSpec(memory_space=pl.ANY)
        self.n_in, self.n_out = len(self.mats), 1
        self.in_specs = [any_spec] * self.n_in
        self.out_specs = [any_spec]
        self.out_shape = [jax.ShapeDtypeStruct((N_DEV, self.rows, D), BF16)]
        self.scratch = [pltpu.SemaphoreType.DMA((N_DEV - 1,)), pltpu.SemaphoreType.DMA((N_DEV - 1,)),
                        pltpu.SemaphoreType.DMA]

    def offsets(self):
        return [sum(MAT_ROWS[i] for i in self.mats[:n]) for n in range(len(self.mats))]

    def start(self, ins, outs, scr):
        (buf_ref,), (send_sems, recv_sems, local_sem) = outs, scr
        x, y, c, me = _me()
        for n, (i, off) in enumerate(zip(self.mats, self.offsets())):
            r = MAT_ROWS[i]
            pltpu.make_async_copy(ins[n].at[pl.ds(me * r, r)], buf_ref.at[0, pl.ds(off, r)], local_sem).start()
            for k in range(1, N_DEV):
                pltpu.make_async_remote_copy(src_ref=ins[n].at[pl.ds((me ^ k) * r, r)], dst_ref=buf_ref.at[k, pl.ds(off, r)],
                                             send_sem=send_sems.at[k - 1], recv_sem=recv_sems.at[k - 1],
                                             device_id=_peer(x, y, c, k), device_id_type=MESH).start()

    def wait(self, ins, outs, scr):
        (buf_ref,), (send_sems, recv_sems, local_sem) = outs, scr
        x, y, c, _ = _me()
        whole = buf_ref.at[0]
        for k in range(1, N_DEV):
            pltpu.make_async_remote_copy(src_ref=whole, dst_ref=whole, send_sem=send_sems.at[k - 1],
                                         recv_sem=recv_sems.at[k - 1], device_id=_peer(x, y, c, k),
                                         device_id_type=MESH).wait()
        pltpu.make_async_copy(whole, whole, local_sem).wait()


def _carried(body, n_in, n_out, ex, n_steps):
    if ex is None:
        return body

    def wrapped(*refs):
        ins, ex_ins = refs[:n_in], refs[n_in:n_in + ex.n_in]
        p = n_in + ex.n_in
        outs, ex_outs = refs[p:p + n_out], refs[p + n_out:p + n_out + ex.n_out]
        p += n_out + ex.n_out
        n_scr = len(refs) - p - len(ex.scratch)
        scr, ex_scr = refs[p:p + n_scr], refs[p + n_scr:]

        @pl.when(pl.program_id(0) == 0)
        def _():
            ex.start(ex_ins, ex_outs, ex_scr)

        body(*ins, *outs, *scr)

        @pl.when(pl.program_id(0) == n_steps - 1)
        def _():
            ex.wait(ex_ins, ex_outs, ex_scr)

    return wrapped


def _exchange_small(name, ex, ex_args, small, reduce):
    vmem_spec = pl.BlockSpec(memory_space=pltpu.VMEM)
    n_small = small.shape[0]

    def body(*refs):
        ex_ins, small_ref = refs[:ex.n_in], refs[ex.n_in]
        p = ex.n_in + 1
        ex_outs, res_ref = refs[p:p + ex.n_out], refs[p + ex.n_out]
        p += ex.n_out + 1
        if reduce:
            all_ref, p = refs[p], p + 1
        else:
            all_ref = res_ref
        sm_send, sm_recv = refs[p], refs[p + 1]
        ex_scr = refs[p + 2:]
        x, y, c, me = _me()
        ex.start(ex_ins, ex_outs, ex_scr)
        copies = [pltpu.make_async_remote_copy(
            src_ref=small_ref, dst_ref=all_ref.at[me], send_sem=sm_send.at[k - 1], recv_sem=sm_recv.at[k - 1],
            device_id=_peer(x, y, c, k), device_id_type=MESH) for k in range(1, N_DEV)]
        for cp in copies:
            cp.start()
        all_ref[me] = small_ref[...]
        for cp in copies:
            cp.wait()
        if reduce:
            acc = all_ref[0]
            for d in range(1, N_DEV):
                acc = acc + all_ref[d]
            res_ref[...] = acc
        ex.wait(ex_ins, ex_outs, ex_scr)

    res_shape = (n_small, D) if reduce else (N_DEV, n_small, D)
    scratch = ([pltpu.VMEM((N_DEV, n_small, D), F32)] if reduce else []) + [
        pltpu.SemaphoreType.DMA((N_DEV - 1,)), pltpu.SemaphoreType.DMA((N_DEV - 1,))] + ex.scratch
    outs = pl.pallas_call(
        body, in_specs=ex.in_specs + [vmem_spec], out_specs=ex.out_specs + [vmem_spec],
        out_shape=ex.out_shape + [jax.ShapeDtypeStruct(res_shape, F32)], scratch_shapes=scratch,
        compiler_params=pltpu.CompilerParams(vmem_limit_bytes=VMEM_LIMIT_BYTES), name=name)(*ex_args, small)
    return outs


def _pack_rows(arrs):
    rows = []
    for a in arrs:
        flat = a.reshape(-1).astype(F32)
        pad = (-flat.shape[0]) % D
        rows.append(jnp.pad(flat, (0, pad)).reshape(-1, D))
    slab = jnp.concatenate(rows, axis=0)
    return jnp.pad(slab, ((0, (-slab.shape[0]) % 8), (0, 0)))


def _unpack_rows(slab, shapes):
    out, r = [], 0
    for shp in shapes:
        size = math.prod(shp)
        nrows = -(-size // D)
        out.append(slab[r:r + nrows].reshape(-1)[:size].reshape(shp))
        r += nrows
    return out


def kernel(x, norm_mix, attn_w_qkv, attn_b_qkv, attn_sinks, attn_w_o, attn_b_o, conv_w_pw1, conv_b_pw1, conv_w_dw, conv_b_dw, conv_ln_g, conv_ln_b, conv_w_pw2, conv_b_pw2, norm_ffn, ffn_w_up, ffn_w_dw, ffn_b_dw, ffn_w_down, final_norm, loss_target, m_norm_mix, m_attn_w_qkv, m_attn_b_qkv, m_attn_sinks, m_attn_w_o, m_attn_b_o, m_conv_w_pw1, m_conv_b_pw1, m_conv_w_dw, m_conv_b_dw, m_conv_ln_g, m_conv_ln_b, m_conv_w_pw2, m_conv_b_pw2, m_norm_ffn, m_ffn_w_up, m_ffn_w_dw, m_ffn_b_dw, m_ffn_w_down, m_final_norm, v_norm_mix, v_attn_w_qkv, v_attn_b_qkv, v_attn_sinks, v_attn_w_o, v_attn_b_o, v_conv_w_pw1, v_conv_b_pw1, v_conv_w_dw, v_conv_b_dw, v_conv_ln_g, v_conv_ln_b, v_conv_w_pw2, v_conv_b_pw2, v_norm_ffn, v_ffn_w_up, v_ffn_w_dw, v_ffn_b_dw, v_ffn_w_down, v_final_norm):
    s = x.shape[1]
    me = 4 * lax.axis_index("x") + 2 * lax.axis_index("y") + lax.axis_index("c")
    x0 = x.reshape(s, D)
    tgt = loss_target.reshape(s, D)

    slab = jnp.concatenate([attn_w_qkv[0].T, attn_w_o[0], conv_w_pw1[0].T, conv_w_pw2[0],
                            ffn_w_up[0].T, ffn_w_down[0], ffn_w_up[1].T, ffn_w_down[1]], axis=0).astype(BF16)
    sharded_small = [conv_b_pw1, conv_w_dw, conv_b_dw, conv_ln_g, conv_ln_b, conv_b_pw2, ffn_w_dw]
    small_local = _pack_rows(sharded_small)
    w_qkv_t, small_all = _exchange_small("gather_first", _Gather((0,)), (slab,), small_local, reduce=False)
    parts = [_unpack_rows(small_all[d], [a.shape for a in sharded_small]) for d in range(N_DEV)]
    b_pw1, w_cdw, b_cdw, ln_g, ln_b, b_pw2, w_fdw = [jnp.concatenate([parts[d][i] for d in range(N_DEV)], axis=-1)
                                                      for i in range(len(sharded_small))]
    conv_w32 = jnp.concatenate([w_cdw[0], b_cdw], axis=0)
    sr = attn_sinks.reshape(N_KV, 4, 2)
    sink_full = jnp.broadcast_to(jnp.repeat(jnp.transpose(sr, (0, 2, 1)), BLOCK, axis=-1).reshape(4, 512, 1),
                                 (4, 512, 256))

    qkv, h0 = _mm("qkv", x0, w_qkv_t, nt=True, tm=512, tn=NQ + KVW, out_dtype=BF16, rms_g=norm_mix[0:1], bias=attn_b_qkv)
    o, lse, w_o, w_up0_t, w_dn0 = _attn_fwd(qkv, sink_full, _Gather((1, 4, 5)), (slab,))
    x1 = _mm("attn_out", o, w_o, nt=False, tm=512, tn=D, out_dtype=F32, bias=attn_b_o, res=x0)
    x2, h1, up0, act0, gate0, w_pw1_t, w_pw2, w_up1_t, w_dn1 = _ffn_fwd(0, x1, norm_ffn[0:1], w_up0_t, w_fdw[0], ffn_b_dw[0:1], w_dn0,
                                                             tm=256, ex=_Gather((2, 3, 6, 7)), ex_args=(slab,))
    x3, h2, a, u2, s_act = _conv_fwd(x2, norm_mix[1:2], w_pw1_t, b_pw1, conv_w32, ln_g, ln_b, w_pw2, b_pw2, tt=256)
    x4, h3, up1, act1, gate1 = _ffn_fwd(1, x3, norm_ffn[1:2], w_up1_t, w_fdw[1], ffn_b_dw[1:2], w_dn1, tm=256)
    dx4, loss_acc, dg_final = _loss_head(x4, final_norm.reshape(1, D), tgt, tm=512)
    loss = lax.psum(loss_acc[0, 0], ("x", "y", "c"))

    dx3, d_up1, dg_f1, dwb1 = _ffn_bwd(1, dx4, x3, norm_ffn[1:2], w_up1_t, w_fdw[1], w_dn1, up1, gate1, tm=256)
    dw_dn1 = _mm_tn("ffn1_dwdown", act1, dx4, tn=FFN_HALF, tt=2048)
    dw_up1_t = _mm_tn("ffn1_dwup", d_up1, h3, tn=FFN_HALF, tt=2048)
    dw_pw2 = _mm_tn("dw_pw2", s_act, dx3, tn=D, tt=2048)
    scatter_a = _Scatter((6, 7))
    dx2, da, dg_m1, conv_acc8, db_pw1_8, db_pw2_8, buf_a = _conv_bwd(
        dx3, x2, norm_mix[1:2], a, u2, w_pw1_t, conv_w32, ln_g, ln_b, w_pw2, tt=256, ex=scatter_a, ex_args=(dw_up1_t, dw_dn1))
    conv_acc = jnp.sum(conv_acc8, axis=1)
    db_pw1 = jnp.sum(db_pw1_8, axis=0, keepdims=True)
    db_pw2 = jnp.sum(db_pw2_8, axis=0, keepdims=True)
    dw_pw1_t = _mm_tn("dw_pw1", da, h2, tn=D, tt=2048)
    dx1, d_up0, dg_f0, dwb0 = _ffn_bwd(0, dx2, x1, norm_ffn[0:1], w_up0_t, w_fdw[0], w_dn0, up0, gate0, tm=256)
    dw_dn0 = _mm_tn("ffn0_dwdown", act0, dx2, tn=FFN_HALF, tt=2048)
    dw_up0_t = _mm_tn("ffn0_dwup", d_up0, h1, tn=FFN_HALF, tt=2048)
    do = _mm("d_attn_out", dx1, w_o, nt=True, tm=512, tn=D, out_dtype=BF16)
    dw_o, db_o = _mm_tn("dw_o", o, dx1, tn=D, tt=2048, colsum_r=True)
    scatter_b = _Scatter((1, 2, 3, 4, 5))
    dq, dkv, dsk, buf_b = _attn_bwd(qkv, do, o, lse, sink_full, scatter_b, (dw_o, dw_pw1_t, dw_pw2, dw_up0_t, dw_dn0))
    dqkv = jnp.concatenate([dq, dkv], axis=1)
    grad_x, dg_m0 = _dgrad_rms("d_qkv", dqkv, w_qkv_t, x0, norm_mix[0:1], dx1, tm=512, tk=NQ + KVW)
    dw_qkv_t, db_qkv = _mm_tn("dw_qkv", dqkv, h0, tn=NQ + KVW, tt=2048, colsum_l=True)

    dwb0, dwb1 = jnp.sum(dwb0, axis=1), jnp.sum(dwb1, axis=1)
    d_sinks = jnp.transpose(jnp.sum(dsk.reshape(N_KV, 2, 4, BLOCK), axis=-1), (0, 2, 1)).reshape(1, 16)
    small_grads = [jnp.concatenate([dg_m0, dg_m1], axis=0), db_qkv, d_sinks, db_o, jnp.concatenate([dg_f0, dg_f1], axis=0),
                   jnp.stack([dwb0[3], dwb1[3]]), dg_final, db_pw1, conv_acc[0:CONV_K], conv_acc[31:32], conv_acc[32:33],
                   conv_acc[33:34], db_pw2, jnp.stack([dwb0[0:3], dwb1[0:3]])]
    small_shapes = [(2, D), (1, NQ + KVW), (1, 16), (1, D), (2, D), (2, F), (D,), (1, 2 * D), (1, CONV_K, D), (1, D),
                    (1, D), (1, D), (1, D), (2, 3, F)]
    scatter_c = _Scatter((0,))
    buf_c, small_sum = _exchange_small("scatter_last", scatter_c, (dw_qkv_t,), _pack_rows(small_grads), reduce=True)
    gm = [None] * N_MAT
    for tag, sc, buf in (("a", scatter_a, buf_a), ("b", scatter_b, buf_b), ("c", scatter_c, buf_c)):
        gslab = _sum_slabs("sum_slabs_" + tag, buf)
        for i, off in zip(sc.mats, sc.offsets()):
            gm[i] = gslab[off:off + MAT_ROWS[i]]
    (g_norm_mix, g_b_qkv, g_sinks, g_b_o, g_norm_ffn, g_ffn_b_dw, g_final, g_b_pw1_full, g_w_cdw_full, g_b_cdw_full,
     g_ln_g_full, g_ln_b_full, g_b_pw2_full, g_w_fdw_full) = _unpack_rows(small_sum, small_shapes)

    def shard(a, width):
        return lax.dynamic_slice_in_dim(a, me * width, width, axis=a.ndim - 1)

    grads = {
        "norm_mix": g_norm_mix, "attn_w_qkv": gm[0].T[None], "attn_b_qkv": g_b_qkv, "attn_sinks": g_sinks,
        "attn_w_o": gm[1][None], "attn_b_o": g_b_o, "conv_w_pw1": gm[2].T[None], "conv_b_pw1": shard(g_b_pw1_full, 256),
        "conv_w_dw": shard(g_w_cdw_full, 128), "conv_b_dw": shard(g_b_cdw_full, 128), "conv_ln_g": shard(g_ln_g_full, 128),
        "conv_ln_b": shard(g_ln_b_full, 128), "conv_w_pw2": gm[3][None], "conv_b_pw2": shard(g_b_pw2_full, 128),
        "norm_ffn": g_norm_ffn, "ffn_w_up": jnp.stack([gm[4].T, gm[6].T]), "ffn_w_dw": shard(g_w_fdw_full, 352),
        "ffn_b_dw": g_ffn_b_dw, "ffn_w_down": jnp.stack([gm[5], gm[7]]), "final_norm": g_final,
    }
    weights = dict(norm_mix=norm_mix, attn_w_qkv=attn_w_qkv, attn_b_qkv=attn_b_qkv, attn_sinks=attn_sinks, attn_w_o=attn_w_o, attn_b_o=attn_b_o, conv_w_pw1=conv_w_pw1, conv_b_pw1=conv_b_pw1, conv_w_dw=conv_w_dw, conv_b_dw=conv_b_dw, conv_ln_g=conv_ln_g, conv_ln_b=conv_ln_b, conv_w_pw2=conv_w_pw2, conv_b_pw2=conv_b_pw2, norm_ffn=norm_ffn, ffn_w_up=ffn_w_up, ffn_w_dw=ffn_w_dw, ffn_b_dw=ffn_b_dw, ffn_w_down=ffn_w_down, final_norm=final_norm)
    moms = dict(norm_mix=m_norm_mix, attn_w_qkv=m_attn_w_qkv, attn_b_qkv=m_attn_b_qkv, attn_sinks=m_attn_sinks, attn_w_o=m_attn_w_o, attn_b_o=m_attn_b_o, conv_w_pw1=m_conv_w_pw1, conv_b_pw1=m_conv_b_pw1, conv_w_dw=m_conv_w_dw, conv_b_dw=m_conv_b_dw, conv_ln_g=m_conv_ln_g, conv_ln_b=m_conv_ln_b, conv_w_pw2=m_conv_w_pw2, conv_b_pw2=m_conv_b_pw2, norm_ffn=m_norm_ffn, ffn_w_up=m_ffn_w_up, ffn_w_dw=m_ffn_w_dw, ffn_b_dw=m_ffn_b_dw, ffn_w_down=m_ffn_w_down, final_norm=m_final_norm)
    vars_ = dict(norm_mix=v_norm_mix, attn_w_qkv=v_attn_w_qkv, attn_b_qkv=v_attn_b_qkv, attn_sinks=v_attn_sinks, attn_w_o=v_attn_w_o, attn_b_o=v_attn_b_o, conv_w_pw1=v_conv_w_pw1, conv_b_pw1=v_conv_b_pw1, conv_w_dw=v_conv_w_dw, conv_b_dw=v_conv_b_dw, conv_ln_g=v_conv_ln_g, conv_ln_b=v_conv_ln_b, conv_w_pw2=v_conv_w_pw2, conv_b_pw2=v_conv_b_pw2, norm_ffn=v_norm_ffn, ffn_w_up=v_ffn_w_up, ffn_w_dw=v_ffn_w_dw, ffn_b_dw=v_ffn_b_dw, ffn_w_down=v_ffn_w_down, final_norm=v_final_norm)
    names = list(weights)
    big = ("attn_w_qkv", "attn_w_o", "conv_w_pw1", "conv_w_pw2", "ffn_w_up", "ffn_w_down")
    delta, new_m, new_v = {}, {}, {}
    for nm in big:
        shp = weights[nm].shape
        two_d = (shp[0] * shp[1], shp[2])
        res = _adamw("adamw_" + nm, *(t[nm].reshape(two_d) for t in (weights, grads, moms, vars_)))
        delta[nm], new_m[nm], new_v[nm] = (r.reshape(shp) for r in res)
    small_names = [nm for nm in names if nm not in big]
    small_shapes_local = [weights[nm].shape for nm in small_names]
    res = _adamw("adamw_small", *(_pack_rows([t[nm] for nm in small_names]) for t in (weights, grads, moms, vars_)))
    for tgt_dict, slab_out in zip((delta, new_m, new_v), res):
        for nm, val in zip(small_names, _unpack_rows(slab_out, small_shapes_local)):
            tgt_dict[nm] = val
    return (loss, grad_x.reshape(1, s, D), *[grads[nm] for nm in names], *[delta[nm] for nm in names],
            *[new_m[nm] for nm in names], *[new_v[nm] for nm in names])
```

```python
import functools
import math

import jax
import jax.numpy as jnp
from jax import lax
from jax.experimental import pallas as pl
from jax.experimental.pallas import tpu as pltpu

F32 = jnp.float32
BF16 = jnp.bfloat16

D = 1024
F = 2816
HEAD_DIM = 64
N_KV = 2
BLOCK = 128
NQ = 1024
KVW = 256
CONV_K = 31
RMS_EPS = 1e-6
LN_EPS = 1e-5
ADAM_LR, ADAM_B1, ADAM_B2, ADAM_EPS, ADAM_WD, ADAM_STEP = 0.001, 0.9, 0.999, 1e-08, 0.01, 10
N_DEV = 8
MESH = pl.DeviceIdType.MESH
VMEM_LIMIT_BYTES = 56 * 2**20
NEG = float(jnp.finfo(jnp.float32).min)

MAT_ROWS = (160, 128, 256, 128, 704, 352, 704, 352)
MAT_OFF = tuple(sum(MAT_ROWS[:i]) for i in range(len(MAT_ROWS)))
SLAB_ROWS = sum(MAT_ROWS)
N_MAT = len(MAT_ROWS)

NT_DIMS = (((1,), (1,)), ((), ()))
NN_DIMS = (((1,), (0,)), ((), ()))
TN_DIMS = (((0,), (0,)), ((), ()))


def _params(*sem):
    return pltpu.CompilerParams(dimension_semantics=tuple(sem), vmem_limit_bytes=VMEM_LIMIT_BYTES)


def _row_tile(rows, cap, mult=8):
    best = None
    for t in range(mult, min(rows, cap) + 1, mult):
        if rows % t == 0:
            best = t
    return best if best is not None else rows


def _sigmoid(x):
    return 1.0 / (1.0 + jnp.exp(-x))


def _mm(name, a, b, *, nt, tm, tn, out_dtype, rms_g=None, bias=None, res=None):
    m, k = a.shape
    n = b.shape[0] if nt else b.shape[1]
    tm = min(tm, m)
    has_rms = rms_g is not None
    dims = NT_DIMS if nt else NN_DIMS

    def body(*refs):
        a_ref, b_ref = refs[0], refs[1]
        pos = 2
        g_ref = bias_ref = res_ref = h_ref = hs_ref = None
        if has_rms:
            g_ref = refs[pos]; pos += 1
        if bias is not None:
            bias_ref = refs[pos]; pos += 1
        if res is not None:
            res_ref = refs[pos]; pos += 1
        o_ref = refs[pos]; pos += 1
        if has_rms:
            h_ref, hs_ref = refs[pos], refs[pos + 1]

            @pl.when(pl.program_id(1) == 0)
            def _():
                xf = a_ref[...]
                r = lax.rsqrt(jnp.mean(xf * xf, axis=-1, keepdims=True) + RMS_EPS)
                h = ((xf * r) * g_ref[...]).astype(BF16)
                hs_ref[...] = h
                h_ref[...] = h

            lhs = hs_ref[...]
        else:
            lhs = a_ref[...].astype(BF16)
        acc = lax.dot_general(lhs, b_ref[...], dims, preferred_element_type=F32)
        if bias is not None:
            acc = acc + bias_ref[...]
        if res is not None:
            acc = acc + res_ref[...]
        o_ref[...] = acc.astype(out_dtype)

    in_specs = [pl.BlockSpec((tm, k), lambda i, j: (i, 0)),
                pl.BlockSpec((tn, k), lambda i, j: (j, 0)) if nt else pl.BlockSpec((k, tn), lambda i, j: (0, j))]
    args = [a, b]
    if has_rms:
        in_specs.append(pl.BlockSpec((1, k), lambda i, j: (0, 0))); args.append(rms_g)
    if bias is not None:
        in_specs.append(pl.BlockSpec((1, tn), lambda i, j: (0, j))); args.append(bias)
    if res is not None:
        in_specs.append(pl.BlockSpec((tm, tn), lambda i, j: (i, j))); args.append(res)
    out_shape = [jax.ShapeDtypeStruct((m, n), out_dtype)]
    out_specs = [pl.BlockSpec((tm, tn), lambda i, j: (i, j))]
    scratch = []
    if has_rms:
        out_shape.append(jax.ShapeDtypeStruct((m, k), BF16))
        out_specs.append(pl.BlockSpec((tm, k), lambda i, j: (i, 0)))
        scratch.append(pltpu.VMEM((tm, k), BF16))
    outs = pl.pallas_call(body, grid=(m // tm, n // tn), in_specs=in_specs, out_specs=out_specs, out_shape=out_shape,
                          scratch_shapes=scratch, compiler_params=_params("parallel", "arbitrary"), name=name)(*args)
    return outs if has_rms else outs[0]


def _mm_tn(name, l, r, *, tn, tt, colsum_l=False, colsum_r=False):
    s, nl = l.shape
    nr = r.shape[1]
    tt = min(tt, s)
    n_t = s // tt

    def body(*refs):
        l_ref, r_ref, o_ref = refs[0], refs[1], refs[2]
        pos = 3
        cl_ref = cr_ref = None
        if colsum_l:
            cl_ref = refs[pos]; pos += 1
        if colsum_r:
            cr_ref = refs[pos]; pos += 1
        acc_ref = refs[pos]
        j, t = pl.program_id(0), pl.program_id(1)

        @pl.when(t == 0)
        def _():
            acc_ref[...] = jnp.zeros_like(acc_ref)

        lv = l_ref[...]
        rv = r_ref[...]
        acc_ref[...] += lax.dot_general(lv, rv.astype(BF16), TN_DIMS, preferred_element_type=F32)
        if colsum_l:
            @pl.when(t == 0)
            def _():
                cl_ref[...] = jnp.zeros_like(cl_ref)

            cl_ref[...] += jnp.sum(lv.astype(F32), axis=0, keepdims=True)
        if colsum_r:
            @pl.when((t == 0) & (j == 0))
            def _():
                cr_ref[...] = jnp.zeros_like(cr_ref)

            @pl.when(j == 0)
            def _():
                cr_ref[...] += jnp.sum(rv.astype(F32), axis=0, keepdims=True)

        @pl.when(t == n_t - 1)
        def _():
            o_ref[...] = acc_ref[...].astype(BF16)

    out_shape = [jax.ShapeDtypeStruct((nl, nr), BF16)]
    out_specs = [pl.BlockSpec((tn, nr), lambda j, t: (j, 0))]
    if colsum_l:
        out_shape.append(jax.ShapeDtypeStruct((1, nl), F32))
        out_specs.append(pl.BlockSpec((1, tn), lambda j, t: (0, j)))
    if colsum_r:
        out_shape.append(jax.ShapeDtypeStruct((1, nr), F32))
        out_specs.append(pl.BlockSpec((1, nr), lambda j, t: (0, 0)))
    outs = pl.pallas_call(
        body, grid=(nl // tn, n_t),
        in_specs=[pl.BlockSpec((tt, tn), lambda j, t: (t, j)), pl.BlockSpec((tt, nr), lambda j, t: (t, 0))],
        out_specs=out_specs, out_shape=out_shape, scratch_shapes=[pltpu.VMEM((tn, nr), F32)],
        compiler_params=_params("arbitrary", "arbitrary"), name=name)(l, r)
    return outs if (colsum_l or colsum_r) else outs[0]


def _rms_bwd(dh, xf, g):
    r = lax.rsqrt(jnp.mean(xf * xf, axis=-1, keepdims=True) + RMS_EPS)
    gd = dh * g
    dx = r * gd - xf * ((r * r * r) * jnp.mean(xf * gd, axis=-1, keepdims=True))
    return dx, dh * (xf * r)


def _dgrad_rms(name, dy, w, x, g, dres, *, tm, tk):
    m, k = dy.shape
    tm = min(tm, m)
    n_k = k // tk

    def body(dy_ref, w_ref, x_ref, g_ref, dres_ref, o_ref, dg_ref, acc_ref):
        i, kk = pl.program_id(0), pl.program_id(1)

        @pl.when(kk == 0)
        def _():
            acc_ref[...] = jnp.zeros_like(acc_ref)

        acc_ref[...] += lax.dot_general(dy_ref[...], w_ref[...], NN_DIMS, preferred_element_type=F32)

        @pl.when((i == 0) & (kk == n_k - 1))
        def _():
            dg_ref[...] = jnp.zeros_like(dg_ref)

        @pl.when(kk == n_k - 1)
        def _():
            dx, dg_rows = _rms_bwd(acc_ref[...], x_ref[...], g_ref[...])
            o_ref[...] = dres_ref[...] + dx
            dg_ref[...] += jnp.sum(dg_rows, axis=0, keepdims=True)

    return pl.pallas_call(
        body, grid=(m // tm, n_k),
        in_specs=[pl.BlockSpec((tm, tk), lambda i, kk: (i, kk)), pl.BlockSpec((tk, D), lambda i, kk: (kk, 0)),
                  pl.BlockSpec((tm, D), lambda i, kk: (i, 0)), pl.BlockSpec((1, D), lambda i, kk: (0, 0)),
                  pl.BlockSpec((tm, D), lambda i, kk: (i, 0))],
        out_specs=[pl.BlockSpec((tm, D), lambda i, kk: (i, 0)), pl.BlockSpec((1, D), lambda i, kk: (0, 0))],
        out_shape=[jax.ShapeDtypeStruct((m, D), F32), jax.ShapeDtypeStruct((1, D), F32)],
        scratch_shapes=[pltpu.VMEM((tm, D), F32)],
        compiler_params=_params("arbitrary", "arbitrary"), name=name)(dy, w, x, g, dres)


def _band(kvp, kvc):
    kk = jnp.concatenate([kvp[:, :128], kvc[:, :128]], axis=0)
    vv = jnp.concatenate([kvp[:, 128:], kvc[:, 128:]], axis=0)
    row = lax.broadcasted_iota(jnp.int32, kk.shape, 0)
    return jnp.where(row == 0, jnp.zeros_like(kk), kk), jnp.where(row == 0, jnp.zeros_like(vv), vv)


def _blockdiag(t, h):
    lane = lax.broadcasted_iota(jnp.int32, t.shape, 1)
    z = jnp.zeros_like(t)
    tr = pltpu.roll(t, 64, 1)
    if h == 0:
        top, bot = jnp.where(lane < 64, t, z), jnp.where(lane >= 64, tr, z)
    else:
        top, bot = jnp.where(lane < 64, tr, z), jnp.where(lane >= 64, t, z)
    return jnp.concatenate([top, bot], axis=0)


def _from_blockdiag(t, h):
    lane = lax.broadcasted_iota(jnp.int32, (256, 128), 1)
    top, bot = t[:256], t[256:]
    if h == 0:
        return jnp.where(lane < 64, top + pltpu.roll(bot, 64, 1), 0.0)
    return jnp.where(lane >= 64, pltpu.roll(top, 64, 1) + bot, 0.0)


def _stack_heads(ref, h):
    return jnp.concatenate([ref[:, h * 512 + p * 128: h * 512 + (p + 1) * 128] for p in range(4)], axis=0)


def _masked_scores(sc_half, n, sink):
    row = lax.broadcasted_iota(jnp.int32, (512, 256), 0) & 127
    col = lax.broadcasted_iota(jnp.int32, (512, 256), 1)
    valid = (col > row) & (col <= row + BLOCK) & ((col >= BLOCK) | (n > 0))
    return jnp.where(col == 0, sink, jnp.where(valid, sc_half, NEG))


def _half_selector(shape, split):
    row = lax.broadcasted_iota(jnp.int32, shape, 0)
    one, zero = jnp.ones(shape, BF16), jnp.zeros(shape, BF16)
    return jnp.where(row < split, one, zero), jnp.where(row >= split, one, zero)


def _attn_fwd(qkv, sink_full, ex=None, ex_args=()):
    s = qkv.shape[0]
    nb = s // BLOCK
    scale = 1.0 / math.sqrt(HEAD_DIM)

    def body(q_ref, kvc_ref, kvp_ref, sk_ref, o_ref, lse_ref):
        n = pl.program_id(0)
        kk, vv = _band(kvp_ref[...], kvc_ref[...])
        sel0, sel1 = _half_selector((512, 128), 256)
        lane = lax.broadcasted_iota(jnp.int32, (512, 128), 1)
        for h in range(N_KV):
            kbd, vbd = _blockdiag(kk, h), _blockdiag(vv, h)
            sc = lax.dot_general(_stack_heads(q_ref, h), kbd, NT_DIMS, preferred_element_type=F32) * scale
            pes, mxs = [], []
            for half in range(2):
                sh = _masked_scores(sc[:, half * 256:(half + 1) * 256], n, sk_ref[h * 2 + half])
                mx = jnp.max(sh, axis=-1, keepdims=True)
                pes.append(jnp.exp(sh - mx).astype(BF16))
                mxs.append(mx)
            pb = jnp.concatenate(pes, axis=1)
            o_un = lax.dot_general(pb, vbd, NN_DIMS, preferred_element_type=F32)
            sums = [lax.dot_general(pb, sel, NN_DIMS, preferred_element_type=F32) for sel in (sel0, sel1)]
            o = o_un * jnp.where(lane < 64, 1.0 / sums[0], 1.0 / sums[1])
            for half in range(2):
                lse_ref[h * 2 + half] = mxs[half] + jnp.log(sums[half])
            for p in range(4):
                o_ref[:, h * 512 + p * 128: h * 512 + (p + 1) * 128] = o[p * 128:(p + 1) * 128].astype(BF16)

    return pl.pallas_call(
        _carried(body, 4, 2, ex, nb), grid=(nb,),
        in_specs=[pl.BlockSpec((BLOCK, NQ), lambda n: (n, 0)),
                  pl.BlockSpec((BLOCK, KVW), lambda n: (n, NQ // KVW)),
                  pl.BlockSpec((BLOCK, KVW), lambda n: (jnp.maximum(n - 1, 0), NQ // KVW)),
                  pl.BlockSpec((4, 512, 256), lambda n: (0, 0, 0))] + _ex(ex, "in_specs"),
        out_specs=[pl.BlockSpec((BLOCK, NQ), lambda n: (n, 0)),
                   pl.BlockSpec((None, 4, 512, 128), lambda n: (n, 0, 0, 0))] + _ex(ex, "out_specs"),
        out_shape=[jax.ShapeDtypeStruct((s, NQ), BF16), jax.ShapeDtypeStruct((nb, 4, 512, 128), F32)] + _ex(ex, "out_shape"),
        scratch_shapes=_ex(ex, "scratch"),
        compiler_params=_params("arbitrary"), name="attn_fwd")(qkv, qkv, qkv, sink_full, *ex_args)


def _attn_bwd(qkv, do, o, lse, sink_full, ex=None, ex_args=()):
    s = qkv.shape[0]
    nb = s // BLOCK
    scale = 1.0 / math.sqrt(HEAD_DIM)

    def body(q_ref, kvc_ref, kvp_ref, do_ref, o_ref, lse_ref, sk_ref, dq_ref, dkv_ref, dsk_ref, ck_ref, cv_ref):
        n = pl.program_id(0)

        @pl.when(n == 0)
        def _():
            dsk_ref[...] = jnp.zeros_like(dsk_ref)
            ck_ref[...] = jnp.zeros_like(ck_ref)
            cv_ref[...] = jnp.zeros_like(cv_ref)

        @pl.when(n < nb)
        def _():
            kk, vv = _band(kvp_ref[...], kvc_ref[...])
            sel0, sel1 = _half_selector((128, 128), 64)
            dkk = jnp.zeros((256, 128), F32)
            dvv = jnp.zeros((256, 128), F32)
            for h in range(N_KV):
                kbd, vbd = _blockdiag(kk, h), _blockdiag(vv, h)
                qp = _stack_heads(q_ref, h)
                dop = _stack_heads(do_ref, h)
                sc = lax.dot_general(qp, kbd, NT_DIMS, preferred_element_type=F32) * scale
                dp = lax.dot_general(dop, vbd, NT_DIMS, preferred_element_type=F32)
                dd = (dop.astype(F32) * _stack_heads(o_ref, h).astype(F32)).astype(BF16)
                probs, dss = [], []
                for half, sel in enumerate((sel0, sel1)):
                    delta = lax.dot_general(dd, sel, NN_DIMS, preferred_element_type=F32)
                    lse_h = lse_ref[h * 2 + half]
                    sh = _masked_scores(sc[:, half * 256:(half + 1) * 256], n, sk_ref[h * 2 + half])
                    pr = jnp.exp(sh - jnp.concatenate([lse_h, lse_h], axis=1))
                    dsf = pr * (dp[:, half * 256:(half + 1) * 256] - jnp.concatenate([delta, delta], axis=1))
                    dsk_ref[h * 2 + half] += dsf[:, 0:1]
                    dss.append((dsf * scale).astype(BF16))
                    probs.append(pr.astype(BF16))
                ds = jnp.concatenate(dss, axis=1)
                pb = jnp.concatenate(probs, axis=1)
                dqp = lax.dot_general(ds, kbd, NN_DIMS, preferred_element_type=F32)
                for p in range(4):
                    dq_ref[:, h * 512 + p * 128: h * 512 + (p + 1) * 128] = dqp[p * 128:(p + 1) * 128].astype(BF16)
                dkk = dkk + _from_blockdiag(lax.dot_general(ds, qp, TN_DIMS, preferred_element_type=F32), h)
                dvv = dvv + _from_blockdiag(lax.dot_general(pb, dop, TN_DIMS, preferred_element_type=F32), h)
            row = lax.broadcasted_iota(jnp.int32, (256, 128), 0)
            dkk = jnp.where(row == 0, 0.0, dkk)
            dvv = jnp.where(row == 0, 0.0, dvv)

            @pl.when(n >= 1)
            def _():
                dkv_ref[:, :128] = (ck_ref[...] + dkk[:128]).astype(BF16)
                dkv_ref[:, 128:] = (cv_ref[...] + dvv[:128]).astype(BF16)

            ck_ref[...] = dkk[128:]
            cv_ref[...] = dvv[128:]

        @pl.when(n == nb)
        def _():
            dkv_ref[:, :128] = ck_ref[...].astype(BF16)
            dkv_ref[:, 128:] = cv_ref[...].astype(BF16)

    last = nb - 1
    blk = lambda n: (jnp.minimum(n, last), 0)
    return pl.pallas_call(
        _carried(body, 7, 3, ex, nb + 1), grid=(nb + 1,),
        in_specs=[pl.BlockSpec((BLOCK, NQ), blk),
                  pl.BlockSpec((BLOCK, KVW), lambda n: (jnp.minimum(n, last), NQ // KVW)),
                  pl.BlockSpec((BLOCK, KVW), lambda n: (jnp.clip(n - 1, 0, last), NQ // KVW)),
                  pl.BlockSpec((BLOCK, NQ), blk), pl.BlockSpec((BLOCK, NQ), blk),
                  pl.BlockSpec((None, 4, 512, 128), lambda n: (jnp.minimum(n, last), 0, 0, 0)),
                  pl.BlockSpec((4, 512, 256), lambda n: (0, 0, 0))] + _ex(ex, "in_specs"),
        out_specs=[pl.BlockSpec((BLOCK, NQ), blk),
                   pl.BlockSpec((BLOCK, KVW), lambda n: (jnp.maximum(n - 1, 0), 0)),
                   pl.BlockSpec((4, 512, 1), lambda n: (0, 0, 0))] + _ex(ex, "out_specs"),
        out_shape=[jax.ShapeDtypeStruct((s, NQ), BF16), jax.ShapeDtypeStruct((s, KVW), BF16),
                   jax.ShapeDtypeStruct((4, 512, 1), F32)] + _ex(ex, "out_shape"),
        scratch_shapes=[pltpu.VMEM((BLOCK, 128), F32), pltpu.VMEM((BLOCK, 128), F32)] + _ex(ex, "scratch"),
        compiler_params=_params("arbitrary"), name="attn_bwd")(qkv, qkv, qkv, do, o, lse, sink_full, *ex_args)


LANE = 128
FFN_HALF = F // 2
FFN_PAIR = 2 * LANE
FFN_PAIRS = F // FFN_PAIR
FFN_GROUPS = ((0, 3), (3, 6), (6, 9), (9, 11))
FFN_HALO = 8


def _resident(shape):
    return pl.BlockSpec(shape, lambda i: (0,) * len(shape), pipeline_mode=pl.Buffered(1))


def _ffn_fwd(tag, x_in, g, w_up_t, w_dw, b_dw, w_down, *, tm, pre=None, head=None, ex=None, ex_args=()):
    s = x_in.shape[0]
    tm = min(tm, s)
    n_pre = 3 if pre is not None else 0
    n_head = 2 if head is not None else 0

    def body(*refs):
        x_ref, g_ref, wup_ref, wdw_ref, bdw_ref, wd_ref = refs[:6]
        pre_refs = refs[6:6 + n_pre]
        head_refs = refs[6 + n_pre:6 + n_pre + n_head]
        p = 6 + n_pre + n_head
        o_ref, h_ref, up_ref, act_ref, gate_ref = refs[p:p + 5]
        p += 5
        if pre is not None:
            xmid_ref, p = refs[p], p + 1
        if head is not None:
            loss_ref, dgf_ref, p = refs[p], refs[p + 1], p + 2
        carry_ref, ext_ref = refs[p], refs[p + 1]

        @pl.when(pl.program_id(0) == 0)
        def _():
            carry_ref[...] = jnp.zeros_like(carry_ref)
            if head is not None:
                loss_ref[...] = jnp.zeros_like(loss_ref)
                dgf_ref[...] = jnp.zeros_like(dgf_ref)

        xf = x_ref[...]
        if pre is not None:
            oin_ref, wo_ref, bo_ref = pre_refs
            xf = xf + lax.dot_general(oin_ref[...], wo_ref[...], NN_DIMS, preferred_element_type=F32) + bo_ref[...]
            xmid_ref[...] = xf
        r = lax.rsqrt(jnp.mean(xf * xf, axis=-1, keepdims=True) + RMS_EPS)
        h = ((xf * r) * g_ref[...]).astype(BF16)
        h_ref[...] = h

        def up_mm(p):
            for off in (p * FFN_PAIR, F + p * FFN_PAIR):
                rs = slice(off, off + FFN_PAIR)
                up_ref[:, rs] = lax.dot_general(h, wup_ref[rs, :], NT_DIMS, preferred_element_type=F32).astype(BF16)

        def elementwise(p):
            for c in (2 * p, 2 * p + 1):
                cs = slice(c * LANE, (c + 1) * LANE)
                vs = slice(F + c * LANE, F + (c + 1) * LANE)
                gcol = up_ref[:, cs].astype(F32)
                ext_ref[pl.ds(0, FFN_HALO), :] = carry_ref[:, cs]
                ext_ref[pl.ds(FFN_HALO, tm), :] = gcol
                gate_b = (wdw_ref[2:3, cs] * gcol + wdw_ref[1:2, cs] * ext_ref[pl.ds(FFN_HALO - 1, tm), :]
                          + wdw_ref[0:1, cs] * ext_ref[pl.ds(FFN_HALO - 2, tm), :] + bdw_ref[:, cs]).astype(BF16)
                gate_ref[:, cs] = gate_b
                gate = gate_b.astype(F32)
                act_ref[:, cs] = (gate * _sigmoid(gate) * up_ref[:, vs].astype(F32)).astype(BF16)
                carry_ref[:, cs] = gcol[tm - FFN_HALO:]

        def down_mm(lo, hi):
            ks = slice(lo * FFN_PAIR, hi * FFN_PAIR)
            return lax.dot_general(act_ref[:, ks], wd_ref[ks, :], NN_DIMS, preferred_element_type=F32)

        acc = xf
        up_mm(0)
        pending = None
        for lo, hi in FFN_GROUPS:
            for p in range(lo, hi):
                if p + 1 < FFN_PAIRS:
                    up_mm(p + 1)
                if pending is not None:
                    acc = acc + down_mm(*pending)
                    pending = None
                elementwise(p)
            pending = (lo, hi)
        x_out = acc + down_mm(*pending)
        if head is None:
            o_ref[...] = x_out
        else:
            gf_ref, tgt_ref = head_refs
            gf = gf_ref[...]
            rf = lax.rsqrt(jnp.mean(x_out * x_out, axis=-1, keepdims=True) + RMS_EPS)
            diff = (x_out * rf) * gf - tgt_ref[...]
            loss_ref[...] += 0.5 * jnp.sum(jnp.mean(diff * diff, axis=-1, keepdims=True))
            dx, dg_rows = _rms_bwd(diff * (1.0 / D), x_out, gf)
            o_ref[...] = dx
            dgf_ref[...] += jnp.sum(dg_rows, axis=0, keepdims=True)

    row = lambda i: (i, 0)
    const = lambda i: (0, 0)
    in_specs = [pl.BlockSpec((tm, D), row), _resident((1, D)), _resident((2 * F, D)), _resident((3, F)),
                _resident((1, F)), _resident((F, D))]
    out_specs = [pl.BlockSpec((tm, D), row), pl.BlockSpec((tm, D), row), pl.BlockSpec((tm, 2 * F), row),
                 pl.BlockSpec((tm, F), row), pl.BlockSpec((tm, F), row)]
    out_shape = [jax.ShapeDtypeStruct((s, D), F32), jax.ShapeDtypeStruct((s, D), BF16),
                 jax.ShapeDtypeStruct((s, 2 * F), BF16), jax.ShapeDtypeStruct((s, F), BF16),
                 jax.ShapeDtypeStruct((s, F), BF16)]
    args = [x_in, g, w_up_t, w_dw, b_dw, w_down]
    if pre is not None:
        in_specs += [pl.BlockSpec((tm, NQ), row), _resident((NQ, D)), _resident((1, D))]
        out_specs.append(pl.BlockSpec((tm, D), row))
        out_shape.append(jax.ShapeDtypeStruct((s, D), F32))
        args += list(pre)
    if head is not None:
        in_specs += [_resident((1, D)), pl.BlockSpec((tm, D), row)]
        out_specs += [pl.BlockSpec((1, 128), const), pl.BlockSpec((1, D), const)]
        out_shape += [jax.ShapeDtypeStruct((1, 128), F32), jax.ShapeDtypeStruct((1, D), F32)]
        args += list(head)
    return pl.pallas_call(
        _carried(body, len(args), len(out_shape), ex, s // tm), grid=(s // tm,),
        in_specs=in_specs + _ex(ex, "in_specs"), out_specs=out_specs + _ex(ex, "out_specs"),
        out_shape=out_shape + _ex(ex, "out_shape"),
        scratch_shapes=[pltpu.VMEM((FFN_HALO, F), F32), pltpu.VMEM((tm + FFN_HALO, LANE), F32)] + _ex(ex, "scratch"),
        compiler_params=_params("arbitrary"), name=f"ffn{tag}_fwd")(*args, *ex_args)


def _ffn_bwd(tag, dx_out, x_in, g, w_up_t, w_dw, w_down, up, gate, *, tm):
    s = x_in.shape[0]
    tm = min(tm, s)
    n_i = s // tm

    def body(dx_ref, x_ref, g_ref, up_ref, gate_ref, wup_ref, wdw_ref, wd_ref,
             dxin_ref, dup_ref, dg_ref, dwb_ref, carry_ref, da0_ref, da1_ref, extd_ref):
        i = pl.program_id(0)

        @pl.when(i == 0)
        def _():
            carry_ref[...] = jnp.zeros_like(carry_ref)
            dg_ref[...] = jnp.zeros_like(dg_ref)
            dwb_ref[...] = jnp.zeros_like(dwb_ref)

        dxf = dx_ref[...]
        dxb = dxf.astype(BF16)
        da_refs = (da0_ref, da1_ref)

        def da_mm(p):
            ks = slice(p * FFN_PAIR, (p + 1) * FFN_PAIR)
            da_refs[p % 2][...] = lax.dot_general(dxb, wd_ref[ks, :], NT_DIMS, preferred_element_type=F32)

        def elementwise(p):
            da_ref = da_refs[p % 2]
            for c in range(2):
                cc = 2 * p + c
                cs = slice(cc * LANE, (cc + 1) * LANE)
                vs = slice(F + cc * LANE, F + (cc + 1) * LANE)
                gate = gate_ref[:, cs].astype(F32)
                sg = _sigmoid(gate)
                silu = gate * sg
                dac = da_ref[:, c * LANE:(c + 1) * LANE]
                dup_ref[:, vs] = (dac * silu).astype(BF16)
                dgate = dac * up_ref[:, vs].astype(F32) * (sg + silu * (1.0 - sg))
                extd_ref[pl.ds(0, tm), :] = dgate
                extd_ref[pl.ds(tm, FFN_HALO), :] = carry_ref[:, cs]
                d1 = extd_ref[pl.ds(1, tm), :]
                d2 = extd_ref[pl.ds(2, tm), :]
                dup_ref[:, cs] = (wdw_ref[2:3, cs] * dgate + wdw_ref[1:2, cs] * d1 + wdw_ref[0:1, cs] * d2).astype(BF16)
                carry_ref[:, cs] = dgate[:FFN_HALO]
                gcol = up_ref[:, cs].astype(F32)
                dwb_ref[2, :, cs] += _sum8(dgate * gcol)
                dwb_ref[1, :, cs] += _sum8(d1 * gcol)
                dwb_ref[0, :, cs] += _sum8(d2 * gcol)
                dwb_ref[3, :, cs] += _sum8(dgate)

        def dh_mm(lo, hi):
            ks = slice(lo * FFN_PAIR, hi * FFN_PAIR)
            vks = slice(F + lo * FFN_PAIR, F + hi * FFN_PAIR)
            return (lax.dot_general(dup_ref[:, ks], wup_ref[ks, :], NN_DIMS, preferred_element_type=F32)
                    + lax.dot_general(dup_ref[:, vks], wup_ref[vks, :], NN_DIMS, preferred_element_type=F32))

        acc = None
        da_mm(0)
        pending = None
        for lo, hi in FFN_GROUPS:
            for p in range(lo, hi):
                if p + 1 < FFN_PAIRS:
                    da_mm(p + 1)
                if pending is not None:
                    part = dh_mm(*pending)
                    acc = part if acc is None else acc + part
                    pending = None
                elementwise(p)
            pending = (lo, hi)
        acc = acc + dh_mm(*pending)
        dx, dg_rows = _rms_bwd(acc, x_ref[...], g_ref[...])
        dxin_ref[...] = dxf + dx
        dg_ref[...] += jnp.sum(dg_rows, axis=0, keepdims=True)

    rev = lambda i: (n_i - 1 - i, 0)
    return pl.pallas_call(
        body, grid=(n_i,),
        in_specs=[pl.BlockSpec((tm, D), rev), pl.BlockSpec((tm, D), rev), _resident((1, D)),
                  pl.BlockSpec((tm, 2 * F), rev), pl.BlockSpec((tm, F), rev),
                  _resident((2 * F, D)), _resident((3, F)), _resident((F, D))],
        out_specs=[pl.BlockSpec((tm, D), rev), pl.BlockSpec((tm, 2 * F), rev),
                   pl.BlockSpec((1, D), lambda i: (0, 0)), pl.BlockSpec((4, 8, F), lambda i: (0, 0, 0))],
        out_shape=[jax.ShapeDtypeStruct((s, D), F32), jax.ShapeDtypeStruct((s, 2 * F), BF16),
                   jax.ShapeDtypeStruct((1, D), F32), jax.ShapeDtypeStruct((4, 8, F), F32)],
        scratch_shapes=[pltpu.VMEM((FFN_HALO, F), F32), pltpu.VMEM((tm, FFN_PAIR), F32), pltpu.VMEM((tm, FFN_PAIR), F32),
                        pltpu.VMEM((tm + FFN_HALO, LANE), F32)],
        compiler_params=_params("arbitrary"), name=f"ffn{tag}_bwd")(dx_out, x_in, g, up, gate, w_up_t, w_dw, w_down)


CONV_HALO = 32
CONV_CHUNK = 256
CONV_ROWS = 64


def _ln(u2, g, b):
    mu = jnp.mean(u2, axis=-1, keepdims=True)
    xc = u2 - mu
    rstd = lax.rsqrt(jnp.mean(xc * xc, axis=-1, keepdims=True) + LN_EPS)
    xhat = xc * rstd
    return xhat, rstd, xhat * g + b


def _phase_taps(ph):
    first = CONV_HALO - (CONV_K - 1)
    return [(k, first + k - ph) for k in range(CONV_K) if (first + k) % 8 == ph]


def _sum8(v):
    out = v[0:8]
    for j in range(8, v.shape[0], 8):
        out = out + v[j:j + 8]
    return out


def _store_phases(src_ref, sh_ref, tt, cols=slice(None)):
    sh_ref[0] = src_ref[:, cols]
    for ph in range(1, 8):
        sh_ref[ph, pl.ds(0, tt + CONV_HALO - 8), :] = src_ref[pl.ds(ph, tt + CONV_HALO - 8), cols]


def _conv_taps_fwd(w_ref, sh_ref, u2_ref, cs, tt):
    for base in range(0, tt, CONV_ROWS):
        acc = jnp.zeros((CONV_ROWS, CONV_CHUNK), F32) + w_ref[CONV_K:CONV_K + 1, cs]
        for ph in range(8):
            for k, off in _phase_taps(ph):
                acc = acc + w_ref[k:k + 1, cs] * sh_ref[ph, pl.ds(base + off, CONV_ROWS), :]
        u2_ref[pl.ds(base, CONV_ROWS), cs] = acc.astype(BF16)


def _conv_fwd(x_in, g, w1t, b1, wdw, ln_g, ln_b, w2, b2, *, tt):
    s = x_in.shape[0]
    tt = min(tt, s)
    n_c = D // CONV_CHUNK

    def body(x_ref, g_ref, w1_ref, b1_ref, w_ref, lg_ref, lb_ref, w2_ref, b2_ref,
             o_ref, h_ref, a_ref, u2_ref, s_ref, carry_ref, ext_ref, sh_ref):
        @pl.when(pl.program_id(0) == 0)
        def _():
            carry_ref[...] = jnp.zeros_like(carry_ref)

        xf = x_ref[...]
        r = lax.rsqrt(jnp.mean(xf * xf, axis=-1, keepdims=True) + RMS_EPS)
        h = ((xf * r) * g_ref[...]).astype(BF16)
        h_ref[...] = h

        def pw1_mm(c):
            for off in (c * CONV_CHUNK, D + c * CONV_CHUNK):
                rs = slice(off, off + CONV_CHUNK)
                a_ref[:, rs] = (lax.dot_general(h, w1_ref[rs, :], NT_DIMS, preferred_element_type=F32)
                                + b1_ref[:, rs]).astype(BF16)

        pw1_mm(0)
        for c in range(n_c):
            if c + 1 < n_c:
                pw1_mm(c + 1)
            cs = slice(c * CONV_CHUNK, (c + 1) * CONV_CHUNK)
            gs = slice(D + c * CONV_CHUNK, D + (c + 1) * CONV_CHUNK)
            u = a_ref[:, cs].astype(F32) * _sigmoid(a_ref[:, gs].astype(F32))
            ext_ref[pl.ds(0, CONV_HALO), :] = carry_ref[:, cs]
            ext_ref[pl.ds(CONV_HALO, tt), :] = u
            carry_ref[:, cs] = u[tt - CONV_HALO:]
            _store_phases(ext_ref, sh_ref, tt)
            _conv_taps_fwd(w_ref, sh_ref, u2_ref, cs, tt)
        _, _, ln = _ln(u2_ref[...].astype(F32), lg_ref[...], lb_ref[...])
        sv = (ln * _sigmoid(ln)).astype(BF16)
        s_ref[...] = sv
        o_ref[...] = xf + lax.dot_general(sv, w2_ref[...], NN_DIMS, preferred_element_type=F32) + b2_ref[...]

    row = lambda i: (i, 0)
    return pl.pallas_call(
        body, grid=(s // tt,),
        in_specs=[pl.BlockSpec((tt, D), row), _resident((1, D)), _resident((2 * D, D)), _resident((1, 2 * D)),
                  _resident((32, D)), _resident((1, D)), _resident((1, D)), _resident((D, D)), _resident((1, D))],
        out_specs=[pl.BlockSpec((tt, D), row), pl.BlockSpec((tt, D), row), pl.BlockSpec((tt, 2 * D), row),
                   pl.BlockSpec((tt, D), row), pl.BlockSpec((tt, D), row)],
        out_shape=[jax.ShapeDtypeStruct((s, D), F32), jax.ShapeDtypeStruct((s, D), BF16),
                   jax.ShapeDtypeStruct((s, 2 * D), BF16), jax.ShapeDtypeStruct((s, D), BF16),
                   jax.ShapeDtypeStruct((s, D), BF16)],
        scratch_shapes=[pltpu.VMEM((CONV_HALO, D), F32), pltpu.VMEM((tt + CONV_HALO, CONV_CHUNK), F32),
                        pltpu.VMEM((8, tt + CONV_HALO, CONV_CHUNK), F32)],
        compiler_params=_params("arbitrary"), name="conv_fwd")(x_in, g, w1t, b1, wdw, ln_g, ln_b, w2, b2)


def _conv_bwd(dx_out, x_in, g, a, u2, w1t, wdw, ln_g, ln_b, w2, *, tt, ex=None, ex_args=()):
    s = x_in.shape[0]
    tt = min(tt, s)
    hb = tt // CONV_HALO
    n_i = s // tt
    te = tt + CONV_HALO
    n_c = D // CONV_CHUNK

    def body(dx_ref, x_ref, g_ref, a_ref, prev_ref, u2_ref, w1_ref, w_ref, lg_ref, lb_ref, w2_ref,
             dxin_ref, da_ref, dg_ref, acc_ref, db1_ref, db2_ref,
             carry_ref, e2_ref, ext_ref, ush_ref, dsh_ref, du_ref):
        i = pl.program_id(0)

        @pl.when(i == 0)
        def _():
            for ref in (dg_ref, acc_ref, db1_ref, db2_ref, carry_ref):
                ref[...] = jnp.zeros_like(ref)

        dxf = dx_ref[...]
        db2_ref[...] += _sum8(dxf)
        dse = lax.dot_general(dxf.astype(BF16), w2_ref[...], NT_DIMS, preferred_element_type=F32)
        gv = lg_ref[...]
        xhat, rstd, ln = _ln(u2_ref[...].astype(F32), gv, lb_ref[...])
        sg = _sigmoid(ln)
        dln = dse * (sg * (1.0 + ln * (1.0 - sg)))
        acc_ref[32] += _sum8(dln * xhat)
        acc_ref[33] += _sum8(dln)
        dxh = dln * gv
        du2 = rstd * (dxh - jnp.mean(dxh, axis=-1, keepdims=True) - xhat * jnp.mean(dxh * xhat, axis=-1, keepdims=True))
        e2_ref[pl.ds(0, tt), :] = du2
        e2_ref[pl.ds(tt, CONV_HALO), :] = carry_ref[...]
        carry_ref[...] = du2[:CONV_HALO]
        first_tile = i == n_i - 1

        def dh_mm(c):
            cs = slice(c * CONV_CHUNK, (c + 1) * CONV_CHUNK)
            gs = slice(D + c * CONV_CHUNK, D + (c + 1) * CONV_CHUNK)
            return (lax.dot_general(da_ref[:, cs], w1_ref[cs, :], NN_DIMS, preferred_element_type=F32)
                    + lax.dot_general(da_ref[:, gs], w1_ref[gs, :], NN_DIMS, preferred_element_type=F32))

        dh = None
        for c in range(n_c):
            if c >= 1:
                part = dh_mm(c - 1)
                dh = part if dh is None else dh + part
            cs = slice(c * CONV_CHUNK, (c + 1) * CONV_CHUNK)
            gs = slice(D + c * CONV_CHUNK, D + (c + 1) * CONV_CHUNK)
            uh = prev_ref[:, cs].astype(F32) * _sigmoid(prev_ref[:, gs].astype(F32))
            ext_ref[pl.ds(0, CONV_HALO), :] = jnp.where(first_tile, 0.0, uh)
            a1 = a_ref[:, cs].astype(F32)
            sg2 = _sigmoid(a_ref[:, gs].astype(F32))
            ext_ref[pl.ds(CONV_HALO, tt), :] = a1 * sg2
            _store_phases(ext_ref, ush_ref, tt)
            _store_phases(e2_ref, dsh_ref, tt, cs)
            for base in range(0, tt, CONV_ROWS):
                d0 = e2_ref[pl.ds(base, CONV_ROWS), cs]
                du = jnp.zeros((CONV_ROWS, CONV_CHUNK), F32)
                for ph in range(8):
                    for k, off in _phase_taps(ph):
                        acc_ref[k, :, cs] += _sum8(d0 * ush_ref[ph, pl.ds(base + off, CONV_ROWS), :])
                    for k in range(CONV_K):
                        if (CONV_K - 1 - k) % 8 == ph:
                            du = du + w_ref[k:k + 1, cs] * dsh_ref[ph, pl.ds(base + CONV_K - 1 - k - ph, CONV_ROWS), :]
                acc_ref[CONV_K, :, cs] += _sum8(d0)
                du_ref[pl.ds(base, CONV_ROWS), :] = du
            du = du_ref[...]
            da1 = du * sg2
            da2 = du * a1 * (sg2 * (1.0 - sg2))
            da_ref[:, cs] = da1.astype(BF16)
            da_ref[:, gs] = da2.astype(BF16)
            db1_ref[:, cs] += _sum8(da1)
            db1_ref[:, gs] += _sum8(da2)
        dh = dh + dh_mm(n_c - 1)
        dx, dg_rows = _rms_bwd(dh, x_ref[...], g_ref[...])
        dxin_ref[...] = dxf + dx
        dg_ref[...] += jnp.sum(dg_rows, axis=0, keepdims=True)

    rev = lambda i: (n_i - 1 - i, 0)
    return pl.pallas_call(
        _carried(body, 11, 6, ex, n_i), grid=(n_i,),
        in_specs=[pl.BlockSpec((tt, D), rev), pl.BlockSpec((tt, D), rev), _resident((1, D)),
                  pl.BlockSpec((tt, 2 * D), rev),
                  pl.BlockSpec((CONV_HALO, 2 * D), lambda i: (jnp.maximum((n_i - 1 - i) * hb - 1, 0), 0)),
                  pl.BlockSpec((tt, D), rev), _resident((2 * D, D)), _resident((32, D)), _resident((1, D)),
                  _resident((1, D)), _resident((D, D))] + _ex(ex, "in_specs"),
        out_specs=[pl.BlockSpec((tt, D), rev), pl.BlockSpec((tt, 2 * D), rev), pl.BlockSpec((1, D), lambda i: (0, 0)),
                   pl.BlockSpec((40, 8, D), lambda i: (0, 0, 0)), pl.BlockSpec((8, 2 * D), lambda i: (0, 0)),
                   pl.BlockSpec((8, D), lambda i: (0, 0))] + _ex(ex, "out_specs"),
        out_shape=[jax.ShapeDtypeStruct((s, D), F32), jax.ShapeDtypeStruct((s, 2 * D), BF16),
                   jax.ShapeDtypeStruct((1, D), F32), jax.ShapeDtypeStruct((40, 8, D), F32),
                   jax.ShapeDtypeStruct((8, 2 * D), F32), jax.ShapeDtypeStruct((8, D), F32)] + _ex(ex, "out_shape"),
        scratch_shapes=[pltpu.VMEM((CONV_HALO, D), F32), pltpu.VMEM((te, D), F32), pltpu.VMEM((te, CONV_CHUNK), F32),
                        pltpu.VMEM((8, te, CONV_CHUNK), F32), pltpu.VMEM((8, te, CONV_CHUNK), F32),
                        pltpu.VMEM((tt, CONV_CHUNK), F32)] + _ex(ex, "scratch"),
        compiler_params=_params("arbitrary"), name="conv_bwd")(dx_out, x_in, g, a, a, u2, w1t, wdw, ln_g, ln_b, w2, *ex_args)


def _adamw(name, w, g, m, v):
    rows, cols = w.shape
    tr = _row_tile(rows, 256)

    def body(w_ref, g_ref, m_ref, v_ref, d_ref, nm_ref, nv_ref):
        gv = g_ref[...]
        m2 = ADAM_B1 * m_ref[...] + (1.0 - ADAM_B1) * gv
        v2 = ADAM_B2 * v_ref[...] + (1.0 - ADAM_B2) * (gv * gv)
        m_hat = m2 / (1.0 - ADAM_B1 ** ADAM_STEP)
        v_hat = v2 / (1.0 - ADAM_B2 ** ADAM_STEP)
        d_ref[...] = -ADAM_LR * (m_hat / (jnp.sqrt(v_hat) + ADAM_EPS) + ADAM_WD * w_ref[...])
        nm_ref[...] = m2
        nv_ref[...] = v2

    spec = pl.BlockSpec((tr, cols), lambda i: (i, 0))
    return pl.pallas_call(
        body, grid=(rows // tr,), in_specs=[spec] * 4, out_specs=[spec] * 3,
        out_shape=[jax.ShapeDtypeStruct((rows, cols), F32)] * 3,
        compiler_params=_params("parallel"), name=name)(w, g, m, v)


def _sum_slabs(name, buf):
    rows = buf.shape[1]
    tr = _row_tile(rows, 512, 16)

    def body(b_ref, o_ref):
        acc = b_ref[0].astype(F32)
        for k in range(1, N_DEV):
            acc = acc + b_ref[k].astype(F32)
        o_ref[...] = acc

    return pl.pallas_call(
        body, grid=(rows // tr,), in_specs=[pl.BlockSpec((N_DEV, tr, D), lambda i: (0, i, 0))],
        out_specs=pl.BlockSpec((tr, D), lambda i: (i, 0)), out_shape=jax.ShapeDtypeStruct((rows, D), F32),
        compiler_params=_params("parallel"), name=name)(buf)


def _ex(ex, field):
    return list(getattr(ex, field)) if ex is not None else []


def _me():
    x, y, c = lax.axis_index("x"), lax.axis_index("y"), lax.axis_index("c")
    return x, y, c, 4 * x + 2 * y + c


def _peer(x, y, c, k):
    return (x ^ (k >> 2), y ^ ((k >> 1) & 1), c ^ (k & 1))


class _Gather:
    def __init__(self, mats):
        self.mats = tuple(mats)
        self.rows = sum(MAT_ROWS[i] for i in self.mats)
        any_spec = pl.BlockSpec(memory_space=pl.ANY)
        self.n_in, self.n_out = 1, len(self.mats)
        self.in_specs = [any_spec]
        self.out_specs = [any_spec] * self.n_out
        self.out_shape = [jax.ShapeDtypeStruct((N_DEV * MAT_ROWS[i], D), BF16) for i in self.mats]
        self.scratch = [pltpu.SemaphoreType.DMA((N_DEV - 1,)), pltpu.SemaphoreType.DMA((N_DEV - 1,)),
                        pltpu.SemaphoreType.DMA]
        assert max(s.shape[0] for s in self.out_shape) >= self.rows

    def start(self, ins, outs, scr):
        (slab_ref,), (send_sems, recv_sems, local_sem) = ins, scr
        x, y, c, me = _me()
        for n, i in enumerate(self.mats):
            src = slab_ref.at[pl.ds(MAT_OFF[i], MAT_ROWS[i])]
            dst = outs[n].at[pl.ds(me * MAT_ROWS[i], MAT_ROWS[i])]
            pltpu.make_async_copy(src, dst, local_sem).start()
            for k in range(1, N_DEV):
                pltpu.make_async_remote_copy(src_ref=src, dst_ref=dst, send_sem=send_sems.at[k - 1],
                                             recv_sem=recv_sems.at[k - 1], device_id=_peer(x, y, c, k),
                                             device_id_type=MESH).start()

    def wait(self, ins, outs, scr):
        send_sems, recv_sems, local_sem = scr
        x, y, c, _ = _me()
        big = max(range(self.n_out), key=lambda n: self.out_shape[n].shape[0])
        whole = outs[big].at[pl.ds(0, self.rows)]
        for k in range(1, N_DEV):
            pltpu.make_async_remote_copy(src_ref=whole, dst_ref=whole, send_sem=send_sems.at[k - 1],
                                         recv_sem=recv_sems.at[k - 1], device_id=_peer(x, y, c, k),
                                         device_id_type=MESH).wait()
        pltpu.make_async_copy(whole, whole, local_sem).wait()


class _Scatter:
    def __init__(self, mats):
        self.mats = tuple(mats)
        self.rows = sum(MAT_ROWS[i] for i in self.mats)
        any_spec = pl.BlockSpec(memory_space=pl.ANY)
        self.n_in, self.n_out = len(self.mats), 1
        self.in_specs = [any_spec] * self.n_in
        self.out_specs = [any_spec]
        self.out_shape = [jax.ShapeDtypeStruct((N_DEV, self.rows, D), BF16)]
        self.scratch = [pltpu.SemaphoreType.DMA((N_DEV - 1,)), pltpu.SemaphoreType.DMA((N_DEV - 1,)),
                        pltpu.SemaphoreType.DMA]

    def offsets(self):
        return [sum(MAT_ROWS[i] for i in self.mats[:n]) for n in range(len(self.mats))]

    def start(self, ins, outs, scr):
        (buf_ref,), (send_sems, recv_sems, local_sem) = outs, scr
        x, y, c, me = _me()
        for n, (i, off) in enumerate(zip(self.mats, self.offsets())):
            r = MAT_ROWS[i]
            pltpu.make_async_copy(ins[n].at[pl.ds(me * r, r)], buf_ref.at[0, pl.ds(off, r)], local_sem).start()
            for k in range(1, N_DEV):
                pltpu.make_async_remote_copy(src_ref=ins[n].at[pl.ds((me ^ k) * r, r)], dst_ref=buf_ref.at[k, pl.ds(off, r)],
                                             send_sem=send_sems.at[k - 1], recv_sem=recv_sems.at[k - 1],
                                             device_id=_peer(x, y, c, k), device_id_type=MESH).start()

    def wait(self, ins, outs, scr):
        (buf_ref,), (send_sems, recv_sems, local_sem) = outs, scr
        x, y, c, _ = _me()
        whole = buf_ref.at[0]
        for k in range(1, N_DEV):
            pltpu.make_async_remote_copy(src_ref=whole, dst_ref=whole, send_sem=send_sems.at[k - 1],
                                         recv_sem=recv_sems.at[k - 1], device_id=_peer(x, y, c, k),
                                         device_id_type=MESH).wait()
        pltpu.make_async_copy(whole, whole, local_sem).wait()


def _carried(body, n_in, n_out, ex, n_steps):
    if ex is None:
        return body

    def wrapped(*refs):
        ins, ex_ins = refs[:n_in], refs[n_in:n_in + ex.n_in]
        p = n_in + ex.n_in
        outs, ex_outs = refs[p:p + n_out], refs[p + n_out:p + n_out + ex.n_out]
        p += n_out + ex.n_out
        n_scr = len(refs) - p - len(ex.scratch)
        scr, ex_scr = refs[p:p + n_scr], refs[p + n_scr:]

        @pl.when(pl.program_id(0) == 0)
        def _():
            ex.start(ex_ins, ex_outs, ex_scr)

        body(*ins, *outs, *scr)

        @pl.when(pl.program_id(0) == n_steps - 1)
        def _():
            ex.wait(ex_ins, ex_outs, ex_scr)

    return wrapped


def _exchange_small(name, ex, ex_args, small, reduce):
    vmem_spec = pl.BlockSpec(memory_space=pltpu.VMEM)
    n_small = small.shape[0]

    def body(*refs):
        ex_ins, small_ref = refs[:ex.n_in], refs[ex.n_in]
        p = ex.n_in + 1
        ex_outs, res_ref = refs[p:p + ex.n_out], refs[p + ex.n_out]
        p += ex.n_out + 1
        if reduce:
            all_ref, p = refs[p], p + 1
        else:
            all_ref = res_ref
        sm_send, sm_recv = refs[p], refs[p + 1]
        ex_scr = refs[p + 2:]
        x, y, c, me = _me()
        ex.start(ex_ins, ex_outs, ex_scr)
        copies = [pltpu.make_async_remote_copy(
            src_ref=small_ref, dst_ref=all_ref.at[me], send_sem=sm_send.at[k - 1], recv_sem=sm_recv.at[k - 1],
            device_id=_peer(x, y, c, k), device_id_type=MESH) for k in range(1, N_DEV)]
        for cp in copies:
            cp.start()
        all_ref[me] = small_ref[...]
        for cp in copies:
            cp.wait()
        if reduce:
            acc = all_ref[0]
            for d in range(1, N_DEV):
                acc = acc + all_ref[d]
            res_ref[...] = acc
        ex.wait(ex_ins, ex_outs, ex_scr)

    res_shape = (n_small, D) if reduce else (N_DEV, n_small, D)
    scratch = ([pltpu.VMEM((N_DEV, n_small, D), F32)] if reduce else []) + [
        pltpu.SemaphoreType.DMA((N_DEV - 1,)), pltpu.SemaphoreType.DMA((N_DEV - 1,))] + ex.scratch
    outs = pl.pallas_call(
        body, in_specs=ex.in_specs + [vmem_spec], out_specs=ex.out_specs + [vmem_spec],
        out_shape=ex.out_shape + [jax.ShapeDtypeStruct(res_shape, F32)], scratch_shapes=scratch,
        compiler_params=pltpu.CompilerParams(vmem_limit_bytes=VMEM_LIMIT_BYTES), name=name)(*ex_args, small)
    return outs


def _pack_rows(arrs):
    rows = []
    for a in arrs:
        flat = a.reshape(-1).astype(F32)
        pad = (-flat.shape[0]) % D
        rows.append(jnp.pad(flat, (0, pad)).reshape(-1, D))
    slab = jnp.concatenate(rows, axis=0)
    return jnp.pad(slab, ((0, (-slab.shape[0]) % 8), (0, 0)))


def _unpack_rows(slab, shapes):
    out, r = [], 0
    for shp in shapes:
        size = math.prod(shp)
        nrows = -(-size // D)
        out.append(slab[r:r + nrows].reshape(-1)[:size].reshape(shp))
        r += nrows
    return out


def kernel(x, norm_mix, attn_w_qkv, attn_b_qkv, attn_sinks, attn_w_o, attn_b_o, conv_w_pw1, conv_b_pw1, conv_w_dw, conv_b_dw, conv_ln_g, conv_ln_b, conv_w_pw2, conv_b_pw2, norm_ffn, ffn_w_up, ffn_w_dw, ffn_b_dw, ffn_w_down, final_norm, loss_target, m_norm_mix, m_attn_w_qkv, m_attn_b_qkv, m_attn_sinks, m_attn_w_o, m_attn_b_o, m_conv_w_pw1, m_conv_b_pw1, m_conv_w_dw, m_conv_b_dw, m_conv_ln_g, m_conv_ln_b, m_conv_w_pw2, m_conv_b_pw2, m_norm_ffn, m_ffn_w_up, m_ffn_w_dw, m_ffn_b_dw, m_ffn_w_down, m_final_norm, v_norm_mix, v_attn_w_qkv, v_attn_b_qkv, v_attn_sinks, v_attn_w_o, v_attn_b_o, v_conv_w_pw1, v_conv_b_pw1, v_conv_w_dw, v_conv_b_dw, v_conv_ln_g, v_conv_ln_b, v_conv_w_pw2, v_conv_b_pw2, v_norm_ffn, v_ffn_w_up, v_ffn_w_dw, v_ffn_b_dw, v_ffn_w_down, v_final_norm):
    s = x.shape[1]
    me = 4 * lax.axis_index("x") + 2 * lax.axis_index("y") + lax.axis_index("c")
    x0 = x.reshape(s, D)
    tgt = loss_target.reshape(s, D)

    slab = jnp.concatenate([attn_w_qkv[0].T, attn_w_o[0], conv_w_pw1[0].T, conv_w_pw2[0],
                            ffn_w_up[0].T, ffn_w_down[0], ffn_w_up[1].T, ffn_w_down[1]], axis=0).astype(BF16)
    sharded_small = [conv_b_pw1, conv_w_dw, conv_b_dw, conv_ln_g, conv_ln_b, conv_b_pw2, ffn_w_dw]
    small_local = _pack_rows(sharded_small)
    w_qkv_t, small_all = _exchange_small("gather_first", _Gather((0,)), (slab,), small_local, reduce=False)
    parts = [_unpack_rows(small_all[d], [a.shape for a in sharded_small]) for d in range(N_DEV)]
    b_pw1, w_cdw, b_cdw, ln_g, ln_b, b_pw2, w_fdw = [jnp.concatenate([parts[d][i] for d in range(N_DEV)], axis=-1)
                                                      for i in range(len(sharded_small))]
    conv_w32 = jnp.concatenate([w_cdw[0], b_cdw], axis=0)
    sr = attn_sinks.reshape(N_KV, 4, 2)
    sink_full = jnp.broadcast_to(jnp.repeat(jnp.transpose(sr, (0, 2, 1)), BLOCK, axis=-1).reshape(4, 512, 1),
                                 (4, 512, 256))

    qkv, h0 = _mm("qkv", x0, w_qkv_t, nt=True, tm=512, tn=NQ + KVW, out_dtype=BF16, rms_g=norm_mix[0:1], bias=attn_b_qkv)
    o, lse, w_o, w_up0_t, w_dn0 = _attn_fwd(qkv, sink_full, _Gather((1, 4, 5)), (slab,))
    x2, h1, up0, act0, gate0, x1, w_pw1_t, w_pw2, w_up1_t, w_dn1 = _ffn_fwd(
        0, x0, norm_ffn[0:1], w_up0_t, w_fdw[0], ffn_b_dw[0:1], w_dn0, tm=256, pre=(o, w_o, attn_b_o),
        ex=_Gather((2, 3, 6, 7)), ex_args=(slab,))
    x3, h2, a, u2, s_act = _conv_fwd(x2, norm_mix[1:2], w_pw1_t, b_pw1, conv_w32, ln_g, ln_b, w_pw2, b_pw2, tt=256)
    dx4, h3, up1, act1, gate1, loss_acc, dg_final = _ffn_fwd(1, x3, norm_ffn[1:2], w_up1_t, w_fdw[1], ffn_b_dw[1:2], w_dn1,
                                                             tm=256, head=(final_norm.reshape(1, D), tgt))
    loss = lax.psum(loss_acc[0, 0], ("x", "y", "c"))

    dx3, d_up1, dg_f1, dwb1 = _ffn_bwd(1, dx4, x3, norm_ffn[1:2], w_up1_t, w_fdw[1], w_dn1, up1, gate1, tm=256)
    dw_dn1 = _mm_tn("ffn1_dwdown", act1, dx4, tn=FFN_HALF, tt=2048)
    dw_up1_t = _mm_tn("ffn1_dwup", d_up1, h3, tn=FFN_HALF, tt=2048)
    dw_pw2 = _mm_tn("dw_pw2", s_act, dx3, tn=D, tt=2048)
    scatter_a = _Scatter((6, 7))
    dx2, da, dg_m1, conv_acc8, db_pw1_8, db_pw2_8, buf_a = _conv_bwd(
        dx3, x2, norm_mix[1:2], a, u2, w_pw1_t, conv_w32, ln_g, ln_b, w_pw2, tt=256, ex=scatter_a, ex_args=(dw_up1_t, dw_dn1))
    conv_acc = jnp.sum(conv_acc8, axis=1)
    db_pw1 = jnp.sum(db_pw1_8, axis=0, keepdims=True)
    db_pw2 = jnp.sum(db_pw2_8, axis=0, keepdims=True)
    dw_pw1_t = _mm_tn("dw_pw1", da, h2, tn=D, tt=2048)
    dx1, d_up0, dg_f0, dwb0 = _ffn_bwd(0, dx2, x1, norm_ffn[0:1], w_up0_t, w_fdw[0], w_dn0, up0, gate0, tm=256)
    dw_dn0 = _mm_tn("ffn0_dwdown", act0, dx2, tn=FFN_HALF, tt=2048)
    dw_up0_t = _mm_tn("ffn0_dwup", d_up0, h1, tn=FFN_HALF, tt=2048)
    do = _mm("d_attn_out", dx1, w_o, nt=True, tm=512, tn=D, out_dtype=BF16)
    dw_o, db_o = _mm_tn("dw_o", o, dx1, tn=D, tt=2048, colsum_r=True)
    scatter_b = _Scatter((1, 2, 3, 4, 5))
    dq, dkv, dsk, buf_b = _attn_bwd(qkv, do, o, lse, sink_full, scatter_b, (dw_o, dw_pw1_t, dw_pw2, dw_up0_t, dw_dn0))
    dqkv = jnp.concatenate([dq, dkv], axis=1)
    grad_x, dg_m0 = _dgrad_rms("d_qkv", dqkv, w_qkv_t, x0, norm_mix[0:1], dx1, tm=512, tk=NQ + KVW)
    dw_qkv_t, db_qkv = _mm_tn("dw_qkv", dqkv, h0, tn=NQ + KVW, tt=2048, colsum_l=True)

    dwb0, dwb1 = jnp.sum(dwb0, axis=1), jnp.sum(dwb1, axis=1)
    d_sinks = jnp.transpose(jnp.sum(dsk.reshape(N_KV, 2, 4, BLOCK), axis=-1), (0, 2, 1)).reshape(1, 16)
    small_grads = [jnp.concatenate([dg_m0, dg_m1], axis=0), db_qkv, d_sinks, db_o, jnp.concatenate([dg_f0, dg_f1], axis=0),
                   jnp.stack([dwb0[3], dwb1[3]]), dg_final, db_pw1, conv_acc[0:CONV_K], conv_acc[31:32], conv_acc[32:33],
                   conv_acc[33:34], db_pw2, jnp.stack([dwb0[0:3], dwb1[0:3]])]
    small_shapes = [(2, D), (1, NQ + KVW), (1, 16), (1, D), (2, D), (2, F), (D,), (1, 2 * D), (1, CONV_K, D), (1, D),
                    (1, D), (1, D), (1, D), (2, 3, F)]
    scatter_c = _Scatter((0,))
    buf_c, small_sum = _exchange_small("scatter_last", scatter_c, (dw_qkv_t,), _pack_rows(small_grads), reduce=True)
    gm = [None] * N_MAT
    for tag, sc, buf in (("a", scatter_a, buf_a), ("b", scatter_b, buf_b), ("c", scatter_c, buf_c)):
        gslab = _sum_slabs("sum_slabs_" + tag, buf)
        for i, off in zip(sc.mats, sc.offsets()):
            gm[i] = gslab[off:off + MAT_ROWS[i]]
    (g_norm_mix, g_b_qkv, g_sinks, g_b_o, g_norm_ffn, g_ffn_b_dw, g_final, g_b_pw1_full, g_w_cdw_full, g_b_cdw_full,
     g_ln_g_full, g_ln_b_full, g_b_pw2_full, g_w_fdw_full) = _unpack_rows(small_sum, small_shapes)

    def shard(a, width):
        return lax.dynamic_slice_in_dim(a, me * width, width, axis=a.ndim - 1)

    grads = {
        "norm_mix": g_norm_mix, "attn_w_qkv": gm[0].T[None], "attn_b_qkv": g_b_qkv, "attn_sinks": g_sinks,
        "attn_w_o": gm[1][None], "attn_b_o": g_b_o, "conv_w_pw1": gm[2].T[None], "conv_b_pw1": shard(g_b_pw1_full, 256),
        "conv_w_dw": shard(g_w_cdw_full, 128), "conv_b_dw": shard(g_b_cdw_full, 128), "conv_ln_g": shard(g_ln_g_full, 128),
        "conv_ln_b": shard(g_ln_b_full, 128), "conv_w_pw2": gm[3][None], "conv_b_pw2": shard(g_b_pw2_full, 128),
        "norm_ffn": g_norm_ffn, "ffn_w_up": jnp.stack([gm[4].T, gm[6].T]), "ffn_w_dw": shard(g_w_fdw_full, 352),
        "ffn_b_dw": g_ffn_b_dw, "ffn_w_down": jnp.stack([gm[5], gm[7]]), "final_norm": g_final,
    }
    weights = dict(norm_mix=norm_mix, attn_w_qkv=attn_w_qkv, attn_b_qkv=attn_b_qkv, attn_sinks=attn_sinks, attn_w_o=attn_w_o, attn_b_o=attn_b_o, conv_w_pw1=conv_w_pw1, conv_b_pw1=conv_b_pw1, conv_w_dw=conv_w_dw, conv_b_dw=conv_b_dw, conv_ln_g=conv_ln_g, conv_ln_b=conv_ln_b, conv_w_pw2=conv_w_pw2, conv_b_pw2=conv_b_pw2, norm_ffn=norm_ffn, ffn_w_up=ffn_w_up, ffn_w_dw=ffn_w_dw, ffn_b_dw=ffn_b_dw, ffn_w_down=ffn_w_down, final_norm=final_norm)
    moms = dict(norm_mix=m_norm_mix, attn_w_qkv=m_attn_w_qkv, attn_b_qkv=m_attn_b_qkv, attn_sinks=m_attn_sinks, attn_w_o=m_attn_w_o, attn_b_o=m_attn_b_o, conv_w_pw1=m_conv_w_pw1, conv_b_pw1=m_conv_b_pw1, conv_w_dw=m_conv_w_dw, conv_b_dw=m_conv_b_dw, conv_ln_g=m_conv_ln_g, conv_ln_b=m_conv_ln_b, conv_w_pw2=m_conv_w_pw2, conv_b_pw2=m_conv_b_pw2, norm_ffn=m_norm_ffn, ffn_w_up=m_ffn_w_up, ffn_w_dw=m_ffn_w_dw, ffn_b_dw=m_ffn_b_dw, ffn_w_down=m_ffn_w_down, final_norm=m_final_norm)
    vars_ = dict(norm_mix=v_norm_mix, attn_w_qkv=v_attn_w_qkv, attn_b_qkv=v_attn_b_qkv, attn_sinks=v_attn_sinks, attn_w_o=v_attn_w_o, attn_b_o=v_attn_b_o, conv_w_pw1=v_conv_w_pw1, conv_b_pw1=v_conv_b_pw1, conv_w_dw=v_conv_w_dw, conv_b_dw=v_conv_b_dw, conv_ln_g=v_conv_ln_g, conv_ln_b=v_conv_ln_b, conv_w_pw2=v_conv_w_pw2, conv_b_pw2=v_conv_b_pw2, norm_ffn=v_norm_ffn, ffn_w_up=v_ffn_w_up, ffn_w_dw=v_ffn_w_dw, ffn_b_dw=v_ffn_b_dw, ffn_w_down=v_ffn_w_down, final_norm=v_final_norm)
    names = list(weights)
    big = ("attn_w_qkv", "attn_w_o", "conv_w_pw1", "conv_w_pw2", "ffn_w_up", "ffn_w_down")
    delta, new_m, new_v = {}, {}, {}
    for nm in big:
        shp = weights[nm].shape
        two_d = (shp[0] * shp[1], shp[2])
        res = _adamw("adamw_" + nm, *(t[nm].reshape(two_d) for t in (weights, grads, moms, vars_)))
        delta[nm], new_m[nm], new_v[nm] = (r.reshape(shp) for r in res)
    small_names = [nm for nm in names if nm not in big]
    small_shapes_local = [weights[nm].shape for nm in small_names]
    res = _adamw("adamw_small", *(_pack_rows([t[nm] for nm in small_names]) for t in (weights, grads, moms, vars_)))
    for tgt_dict, slab_out in zip((delta, new_m, new_v), res):
        for nm, val in zip(small_names, _unpack_rows(slab_out, small_shapes_local)):
            tgt_dict[nm] = val
    return (loss, grad_x.reshape(1, s, D), *[grads[nm] for nm in names], *[delta[nm] for nm in names],
            *[new_m[nm] for nm in names], *[new_v[nm] for nm in names])
```

```python
import functools
import math

import jax
import jax.numpy as jnp
from jax import lax
from jax.experimental import pallas as pl
from jax.experimental.pallas import tpu as pltpu

F32 = jnp.float32
BF16 = jnp.bfloat16

D = 1024
F = 2816
HEAD_DIM = 64
N_KV = 2
BLOCK = 128
NQ = 1024
KVW = 256
CONV_K = 31
RMS_EPS = 1e-6
LN_EPS = 1e-5
ADAM_LR, ADAM_B1, ADAM_B2, ADAM_EPS, ADAM_WD, ADAM_STEP = 0.001, 0.9, 0.999, 1e-08, 0.01, 10
N_DEV = 8
MESH = pl.DeviceIdType.MESH
VMEM_LIMIT_BYTES = 56 * 2**20
NEG = float(jnp.finfo(jnp.float32).min)

MAT_ROWS = (160, 128, 256, 128, 704, 352, 704, 352)
MAT_OFF = tuple(sum(MAT_ROWS[:i]) for i in range(len(MAT_ROWS)))
SLAB_ROWS = sum(MAT_ROWS)
N_MAT = len(MAT_ROWS)

NT_DIMS = (((1,), (1,)), ((), ()))
NN_DIMS = (((1,), (0,)), ((), ()))
TN_DIMS = (((0,), (0,)), ((), ()))


def _params(*sem):
    return pltpu.CompilerParams(dimension_semantics=tuple(sem), vmem_limit_bytes=VMEM_LIMIT_BYTES)


def _row_tile(rows, cap, mult=8):
    best = None
    for t in range(mult, min(rows, cap) + 1, mult):
        if rows % t == 0:
            best = t
    return best if best is not None else rows


def _sigmoid(x):
    return 1.0 / (1.0 + jnp.exp(-x))


def _mm(name, a, b, *, nt, tm, tn, out_dtype, rms_g=None, bias=None, res=None):
    m, k = a.shape
    n = b.shape[0] if nt else b.shape[1]
    tm = min(tm, m)
    has_rms = rms_g is not None
    dims = NT_DIMS if nt else NN_DIMS

    def body(*refs):
        a_ref, b_ref = refs[0], refs[1]
        pos = 2
        g_ref = bias_ref = res_ref = h_ref = hs_ref = None
        if has_rms:
            g_ref = refs[pos]; pos += 1
        if bias is not None:
            bias_ref = refs[pos]; pos += 1
        if res is not None:
            res_ref = refs[pos]; pos += 1
        o_ref = refs[pos]; pos += 1
        if has_rms:
            h_ref, hs_ref = refs[pos], refs[pos + 1]

            @pl.when(pl.program_id(1) == 0)
            def _():
                xf = a_ref[...]
                r = lax.rsqrt(jnp.mean(xf * xf, axis=-1, keepdims=True) + RMS_EPS)
                h = ((xf * r) * g_ref[...]).astype(BF16)
                hs_ref[...] = h
                h_ref[...] = h

            lhs = hs_ref[...]
        else:
            lhs = a_ref[...].astype(BF16)
        acc = lax.dot_general(lhs, b_ref[...], dims, preferred_element_type=F32)
        if bias is not None:
            acc = acc + bias_ref[...]
        if res is not None:
            acc = acc + res_ref[...]
        o_ref[...] = acc.astype(out_dtype)

    in_specs = [pl.BlockSpec((tm, k), lambda i, j: (i, 0)),
                pl.BlockSpec((tn, k), lambda i, j: (j, 0)) if nt else pl.BlockSpec((k, tn), lambda i, j: (0, j))]
    args = [a, b]
    if has_rms:
        in_specs.append(pl.BlockSpec((1, k), lambda i, j: (0, 0))); args.append(rms_g)
    if bias is not None:
        in_specs.append(pl.BlockSpec((1, tn), lambda i, j: (0, j))); args.append(bias)
    if res is not None:
        in_specs.append(pl.BlockSpec((tm, tn), lambda i, j: (i, j))); args.append(res)
    out_shape = [jax.ShapeDtypeStruct((m, n), out_dtype)]
    out_specs = [pl.BlockSpec((tm, tn), lambda i, j: (i, j))]
    scratch = []
    if has_rms:
        out_shape.append(jax.ShapeDtypeStruct((m, k), BF16))
        out_specs.append(pl.BlockSpec((tm, k), lambda i, j: (i, 0)))
        scratch.append(pltpu.VMEM((tm, k), BF16))
    outs = pl.pallas_call(body, grid=(m // tm, n // tn), in_specs=in_specs, out_specs=out_specs, out_shape=out_shape,
                          scratch_shapes=scratch, compiler_params=_params("parallel", "arbitrary"), name=name)(*args)
    return outs if has_rms else outs[0]


def _mm_tn(name, l, r, *, tn, tt, colsum_l=False, colsum_r=False):
    s, nl = l.shape
    nr = r.shape[1]
    tt = min(tt, s)
    n_t = s // tt

    def body(*refs):
        l_ref, r_ref, o_ref = refs[0], refs[1], refs[2]
        pos = 3
        cl_ref = cr_ref = None
        if colsum_l:
            cl_ref = refs[pos]; pos += 1
        if colsum_r:
            cr_ref = refs[pos]; pos += 1
        acc_ref = refs[pos]
        j, t = pl.program_id(0), pl.program_id(1)

        @pl.when(t == 0)
        def _():
            acc_ref[...] = jnp.zeros_like(acc_ref)

        lv = l_ref[...]
        rv = r_ref[...]
        acc_ref[...] += lax.dot_general(lv, rv.astype(BF16), TN_DIMS, preferred_element_type=F32)
        if colsum_l:
            @pl.when(t == 0)
            def _():
                cl_ref[...] = jnp.zeros_like(cl_ref)

            cl_ref[...] += jnp.sum(lv.astype(F32), axis=0, keepdims=True)
        if colsum_r:
            @pl.when((t == 0) & (j == 0))
            def _():
                cr_ref[...] = jnp.zeros_like(cr_ref)

            @pl.when(j == 0)
            def _():
                cr_ref[...] += jnp.sum(rv.astype(F32), axis=0, keepdims=True)

        @pl.when(t == n_t - 1)
        def _():
            o_ref[...] = acc_ref[...].astype(BF16)

    out_shape = [jax.ShapeDtypeStruct((nl, nr), BF16)]
    out_specs = [pl.BlockSpec((tn, nr), lambda j, t: (j, 0))]
    if colsum_l:
        out_shape.append(jax.ShapeDtypeStruct((1, nl), F32))
        out_specs.append(pl.BlockSpec((1, tn), lambda j, t: (0, j)))
    if colsum_r:
        out_shape.append(jax.ShapeDtypeStruct((1, nr), F32))
        out_specs.append(pl.BlockSpec((1, nr), lambda j, t: (0, 0)))
    outs = pl.pallas_call(
        body, grid=(nl // tn, n_t),
        in_specs=[pl.BlockSpec((tt, tn), lambda j, t: (t, j)), pl.BlockSpec((tt, nr), lambda j, t: (t, 0))],
        out_specs=out_specs, out_shape=out_shape, scratch_shapes=[pltpu.VMEM((tn, nr), F32)],
        compiler_params=_params("arbitrary", "arbitrary"), name=name)(l, r)
    return outs if (colsum_l or colsum_r) else outs[0]


def _rms_bwd(dh, xf, g):
    r = lax.rsqrt(jnp.mean(xf * xf, axis=-1, keepdims=True) + RMS_EPS)
    gd = dh * g
    dx = r * gd - xf * ((r * r * r) * jnp.mean(xf * gd, axis=-1, keepdims=True))
    return dx, dh * (xf * r)


def _dgrad_rms(name, dy, w, x, g, dres, *, tm, tk):
    m, k = dy.shape
    tm = min(tm, m)
    n_k = k // tk

    def body(dy_ref, w_ref, x_ref, g_ref, dres_ref, o_ref, dg_ref, acc_ref):
        i, kk = pl.program_id(0), pl.program_id(1)

        @pl.when(kk == 0)
        def _():
            acc_ref[...] = jnp.zeros_like(acc_ref)

        acc_ref[...] += lax.dot_general(dy_ref[...], w_ref[...], NN_DIMS, preferred_element_type=F32)

        @pl.when((i == 0) & (kk == n_k - 1))
        def _():
            dg_ref[...] = jnp.zeros_like(dg_ref)

        @pl.when(kk == n_k - 1)
        def _():
            dx, dg_rows = _rms_bwd(acc_ref[...], x_ref[...], g_ref[...])
            o_ref[...] = dres_ref[...] + dx
            dg_ref[...] += jnp.sum(dg_rows, axis=0, keepdims=True)

    return pl.pallas_call(
        body, grid=(m // tm, n_k),
        in_specs=[pl.BlockSpec((tm, tk), lambda i, kk: (i, kk)), pl.BlockSpec((tk, D), lambda i, kk: (kk, 0)),
                  pl.BlockSpec((tm, D), lambda i, kk: (i, 0)), pl.BlockSpec((1, D), lambda i, kk: (0, 0)),
                  pl.BlockSpec((tm, D), lambda i, kk: (i, 0))],
        out_specs=[pl.BlockSpec((tm, D), lambda i, kk: (i, 0)), pl.BlockSpec((1, D), lambda i, kk: (0, 0))],
        out_shape=[jax.ShapeDtypeStruct((m, D), F32), jax.ShapeDtypeStruct((1, D), F32)],
        scratch_shapes=[pltpu.VMEM((tm, D), F32)],
        compiler_params=_params("arbitrary", "arbitrary"), name=name)(dy, w, x, g, dres)


def _band(kvp, kvc):
    kk = jnp.concatenate([kvp[:, :128], kvc[:, :128]], axis=0)
    vv = jnp.concatenate([kvp[:, 128:], kvc[:, 128:]], axis=0)
    row = lax.broadcasted_iota(jnp.int32, kk.shape, 0)
    return jnp.where(row == 0, jnp.zeros_like(kk), kk), jnp.where(row == 0, jnp.zeros_like(vv), vv)


def _blockdiag(t, h):
    lane = lax.broadcasted_iota(jnp.int32, t.shape, 1)
    z = jnp.zeros_like(t)
    tr = pltpu.roll(t, 64, 1)
    if h == 0:
        top, bot = jnp.where(lane < 64, t, z), jnp.where(lane >= 64, tr, z)
    else:
        top, bot = jnp.where(lane < 64, tr, z), jnp.where(lane >= 64, t, z)
    return jnp.concatenate([top, bot], axis=0)


def _from_blockdiag(t, h):
    lane = lax.broadcasted_iota(jnp.int32, (256, 128), 1)
    top, bot = t[:256], t[256:]
    if h == 0:
        return jnp.where(lane < 64, top + pltpu.roll(bot, 64, 1), 0.0)
    return jnp.where(lane >= 64, pltpu.roll(top, 64, 1) + bot, 0.0)


def _stack_heads(ref, h):
    return jnp.concatenate([ref[:, h * 512 + p * 128: h * 512 + (p + 1) * 128] for p in range(4)], axis=0)


def _masked_scores(sc_half, n, sink):
    row = lax.broadcasted_iota(jnp.int32, (512, 256), 0) & 127
    col = lax.broadcasted_iota(jnp.int32, (512, 256), 1)
    valid = (col > row) & (col <= row + BLOCK) & ((col >= BLOCK) | (n > 0))
    return jnp.where(col == 0, sink, jnp.where(valid, sc_half, NEG))


def _half_selector(shape, split):
    row = lax.broadcasted_iota(jnp.int32, shape, 0)
    one, zero = jnp.ones(shape, BF16), jnp.zeros(shape, BF16)
    return jnp.where(row < split, one, zero), jnp.where(row >= split, one, zero)


def _attn_fwd(qkv, sink_full, ex=None, ex_args=()):
    s = qkv.shape[0]
    nb = s // BLOCK
    scale = 1.0 / math.sqrt(HEAD_DIM)

    def body(q_ref, kvc_ref, kvp_ref, sk_ref, o_ref, lse_ref):
        n = pl.program_id(0)
        kk, vv = _band(kvp_ref[...], kvc_ref[...])
        sel0, sel1 = _half_selector((512, 128), 256)
        lane = lax.broadcasted_iota(jnp.int32, (512, 128), 1)
        for h in range(N_KV):
            kbd, vbd = _blockdiag(kk, h), _blockdiag(vv, h)
            sc = lax.dot_general(_stack_heads(q_ref, h), kbd, NT_DIMS, preferred_element_type=F32) * scale
            pes, mxs = [], []
            for half in range(2):
                sh = _masked_scores(sc[:, half * 256:(half + 1) * 256], n, sk_ref[h * 2 + half])
                mx = jnp.max(sh, axis=-1, keepdims=True)
                pes.append(jnp.exp(sh - mx).astype(BF16))
                mxs.append(mx)
            pb = jnp.concatenate(pes, axis=1)
            o_un = lax.dot_general(pb, vbd, NN_DIMS, preferred_element_type=F32)
            sums = [lax.dot_general(pb, sel, NN_DIMS, preferred_element_type=F32) for sel in (sel0, sel1)]
            o = o_un * jnp.where(lane < 64, 1.0 / sums[0], 1.0 / sums[1])
            for half in range(2):
                lse_ref[h * 2 + half] = mxs[half] + jnp.log(sums[half])
            for p in range(4):
                o_ref[:, h * 512 + p * 128: h * 512 + (p + 1) * 128] = o[p * 128:(p + 1) * 128].astype(BF16)

    return pl.pallas_call(
        _carried(body, 4, 2, ex, nb), grid=(nb,),
        in_specs=[pl.BlockSpec((BLOCK, NQ), lambda n: (n, 0)),
                  pl.BlockSpec((BLOCK, KVW), lambda n: (n, NQ // KVW)),
                  pl.BlockSpec((BLOCK, KVW), lambda n: (jnp.maximum(n - 1, 0), NQ // KVW)),
                  pl.BlockSpec((4, 512, 256), lambda n: (0, 0, 0))] + _ex(ex, "in_specs"),
        out_specs=[pl.BlockSpec((BLOCK, NQ), lambda n: (n, 0)),
                   pl.BlockSpec((None, 4, 512, 128), lambda n: (n, 0, 0, 0))] + _ex(ex, "out_specs"),
        out_shape=[jax.ShapeDtypeStruct((s, NQ), BF16), jax.ShapeDtypeStruct((nb, 4, 512, 128), F32)] + _ex(ex, "out_shape"),
        scratch_shapes=_ex(ex, "scratch"),
        compiler_params=_params("arbitrary"), name="attn_fwd")(qkv, qkv, qkv, sink_full, *ex_args)


def _attn_bwd(qkv, do, o, lse, sink_full, ex=None, ex_args=()):
    s = qkv.shape[0]
    nb = s // BLOCK
    scale = 1.0 / math.sqrt(HEAD_DIM)

    def body(q_ref, kvc_ref, kvp_ref, do_ref, o_ref, lse_ref, sk_ref, dq_ref, dkv_ref, dsk_ref, ck_ref, cv_ref):
        n = pl.program_id(0)

        @pl.when(n == 0)
        def _():
            dsk_ref[...] = jnp.zeros_like(dsk_ref)
            ck_ref[...] = jnp.zeros_like(ck_ref)
            cv_ref[...] = jnp.zeros_like(cv_ref)

        @pl.when(n < nb)
        def _():
            kk, vv = _band(kvp_ref[...], kvc_ref[...])
            sel0, sel1 = _half_selector((128, 128), 64)
            dkk = jnp.zeros((256, 128), F32)
            dvv = jnp.zeros((256, 128), F32)
            for h in range(N_KV):
                kbd, vbd = _blockdiag(kk, h), _blockdiag(vv, h)
                qp = _stack_heads(q_ref, h)
                dop = _stack_heads(do_ref, h)
                sc = lax.dot_general(qp, kbd, NT_DIMS, preferred_element_type=F32) * scale
                dp = lax.dot_general(dop, vbd, NT_DIMS, preferred_element_type=F32)
                dd = (dop.astype(F32) * _stack_heads(o_ref, h).astype(F32)).astype(BF16)
                probs, dss = [], []
                for half, sel in enumerate((sel0, sel1)):
                    delta = lax.dot_general(dd, sel, NN_DIMS, preferred_element_type=F32)
                    lse_h = lse_ref[h * 2 + half]
                    sh = _masked_scores(sc[:, half * 256:(half + 1) * 256], n, sk_ref[h * 2 + half])
                    pr = jnp.exp(sh - jnp.concatenate([lse_h, lse_h], axis=1))
                    dsf = pr * (dp[:, half * 256:(half + 1) * 256] - jnp.concatenate([delta, delta], axis=1))
                    dsk_ref[h * 2 + half] += dsf[:, 0:1]
                    dss.append((dsf * scale).astype(BF16))
                    probs.append(pr.astype(BF16))
                ds = jnp.concatenate(dss, axis=1)
                pb = jnp.concatenate(probs, axis=1)
                dqp = lax.dot_general(ds, kbd, NN_DIMS, preferred_element_type=F32)
                for p in range(4):
                    dq_ref[:, h * 512 + p * 128: h * 512 + (p + 1) * 128] = dqp[p * 128:(p + 1) * 128].astype(BF16)
                dkk = dkk + _from_blockdiag(lax.dot_general(ds, qp, TN_DIMS, preferred_element_type=F32), h)
                dvv = dvv + _from_blockdiag(lax.dot_general(pb, dop, TN_DIMS, preferred_element_type=F32), h)
            row = lax.broadcasted_iota(jnp.int32, (256, 128), 0)
            dkk = jnp.where(row == 0, 0.0, dkk)
            dvv = jnp.where(row == 0, 0.0, dvv)

            @pl.when(n >= 1)
            def _():
                dkv_ref[:, :128] = (ck_ref[...] + dkk[:128]).astype(BF16)
                dkv_ref[:, 128:] = (cv_ref[...] + dvv[:128]).astype(BF16)

            ck_ref[...] = dkk[128:]
            cv_ref[...] = dvv[128:]

        @pl.when(n == nb)
        def _():
            dkv_ref[:, :128] = ck_ref[...].astype(BF16)
            dkv_ref[:, 128:] = cv_ref[...].astype(BF16)

    last = nb - 1
    blk = lambda n: (jnp.minimum(n, last), 0)
    return pl.pallas_call(
        _carried(body, 7, 3, ex, nb + 1), grid=(nb + 1,),
        in_specs=[pl.BlockSpec((BLOCK, NQ), blk),
                  pl.BlockSpec((BLOCK, KVW), lambda n: (jnp.minimum(n, last), NQ // KVW)),
                  pl.BlockSpec((BLOCK, KVW), lambda n: (jnp.clip(n - 1, 0, last), NQ // KVW)),
                  pl.BlockSpec((BLOCK, NQ), blk), pl.BlockSpec((BLOCK, NQ), blk),
                  pl.BlockSpec((None, 4, 512, 128), lambda n: (jnp.minimum(n, last), 0, 0, 0)),
                  pl.BlockSpec((4, 512, 256), lambda n: (0, 0, 0))] + _ex(ex, "in_specs"),
        out_specs=[pl.BlockSpec((BLOCK, NQ), blk),
                   pl.BlockSpec((BLOCK, KVW), lambda n: (jnp.maximum(n - 1, 0), 0)),
                   pl.BlockSpec((4, 512, 1), lambda n: (0, 0, 0))] + _ex(ex, "out_specs"),
        out_shape=[jax.ShapeDtypeStruct((s, NQ), BF16), jax.ShapeDtypeStruct((s, KVW), BF16),
                   jax.ShapeDtypeStruct((4, 512, 1), F32)] + _ex(ex, "out_shape"),
        scratch_shapes=[pltpu.VMEM((BLOCK, 128), F32), pltpu.VMEM((BLOCK, 128), F32)] + _ex(ex, "scratch"),
        compiler_params=_params("arbitrary"), name="attn_bwd")(qkv, qkv, qkv, do, o, lse, sink_full, *ex_args)


LANE = 128
FFN_HALF = F // 2
FFN_PAIR = 2 * LANE
FFN_PAIRS = F // FFN_PAIR
FFN_GROUPS = ((0, 3), (3, 6), (6, 9), (9, 11))
FFN_HALO = 8


def _resident(shape):
    return pl.BlockSpec(shape, lambda i: (0,) * len(shape), pipeline_mode=pl.Buffered(1))


def _ffn_fwd(tag, x_in, g, w_up_t, w_dw, b_dw, w_down, *, tm, pre=None, head=None, ex=None, ex_args=()):
    s = x_in.shape[0]
    tm = min(tm, s)
    n_pre = 3 if pre is not None else 0
    n_head = 2 if head is not None else 0

    def body(*refs):
        x_ref, g_ref, wup_ref, wdw_ref, bdw_ref, wd_ref = refs[:6]
        pre_refs = refs[6:6 + n_pre]
        head_refs = refs[6 + n_pre:6 + n_pre + n_head]
        p = 6 + n_pre + n_head
        o_ref, h_ref, up_ref, act_ref, gate_ref = refs[p:p + 5]
        p += 5
        if pre is not None:
            xmid_ref, p = refs[p], p + 1
        if head is not None:
            loss_ref, dgf_ref, p = refs[p], refs[p + 1], p + 2
        carry_ref, ext_ref = refs[p], refs[p + 1]

        @pl.when(pl.program_id(0) == 0)
        def _():
            carry_ref[...] = jnp.zeros_like(carry_ref)
            if head is not None:
                loss_ref[...] = jnp.zeros_like(loss_ref)
                dgf_ref[...] = jnp.zeros_like(dgf_ref)

        xf = x_ref[...]
        if pre is not None:
            oin_ref, wo_ref, bo_ref = pre_refs
            xf = xf + lax.dot_general(oin_ref[...], wo_ref[...], NN_DIMS, preferred_element_type=F32) + bo_ref[...]
            xmid_ref[...] = xf
        r = lax.rsqrt(jnp.mean(xf * xf, axis=-1, keepdims=True) + RMS_EPS)
        h = ((xf * r) * g_ref[...]).astype(BF16)
        h_ref[...] = h

        def up_mm(p):
            for off in (p * FFN_PAIR, F + p * FFN_PAIR):
                rs = slice(off, off + FFN_PAIR)
                up_ref[:, rs] = lax.dot_general(h, wup_ref[rs, :], NT_DIMS, preferred_element_type=F32).astype(BF16)

        def elementwise(p):
            for c in (2 * p, 2 * p + 1):
                cs = slice(c * LANE, (c + 1) * LANE)
                vs = slice(F + c * LANE, F + (c + 1) * LANE)
                gcol = up_ref[:, cs].astype(F32)
                ext_ref[pl.ds(0, FFN_HALO), :] = carry_ref[:, cs]
                ext_ref[pl.ds(FFN_HALO, tm), :] = gcol
                gate_b = (wdw_ref[2:3, cs] * gcol + wdw_ref[1:2, cs] * ext_ref[pl.ds(FFN_HALO - 1, tm), :]
                          + wdw_ref[0:1, cs] * ext_ref[pl.ds(FFN_HALO - 2, tm), :] + bdw_ref[:, cs]).astype(BF16)
                gate_ref[:, cs] = gate_b
                gate = gate_b.astype(F32)
                act_ref[:, cs] = (gate * _sigmoid(gate) * up_ref[:, vs].astype(F32)).astype(BF16)
                carry_ref[:, cs] = gcol[tm - FFN_HALO:]

        def down_mm(lo, hi):
            ks = slice(lo * FFN_PAIR, hi * FFN_PAIR)
            return lax.dot_general(act_ref[:, ks], wd_ref[ks, :], NN_DIMS, preferred_element_type=F32)

        acc = xf
        up_mm(0)
        pending = None
        for lo, hi in FFN_GROUPS:
            for p in range(lo, hi):
                if p + 1 < FFN_PAIRS:
                    up_mm(p + 1)
                if pending is not None:
                    acc = acc + down_mm(*pending)
                    pending = None
                elementwise(p)
            pending = (lo, hi)
        x_out = acc + down_mm(*pending)
        if head is None:
            o_ref[...] = x_out
        else:
            gf_ref, tgt_ref = head_refs
            gf = gf_ref[...]
            rf = lax.rsqrt(jnp.mean(x_out * x_out, axis=-1, keepdims=True) + RMS_EPS)
            diff = (x_out * rf) * gf - tgt_ref[...]
            loss_ref[...] += 0.5 * jnp.sum(jnp.mean(diff * diff, axis=-1, keepdims=True))
            dx, dg_rows = _rms_bwd(diff * (1.0 / D), x_out, gf)
            o_ref[...] = dx
            dgf_ref[...] += jnp.sum(dg_rows, axis=0, keepdims=True)

    row = lambda i: (i, 0)
    const = lambda i: (0, 0)
    in_specs = [pl.BlockSpec((tm, D), row), _resident((1, D)), _resident((2 * F, D)), _resident((3, F)),
                _resident((1, F)), _resident((F, D))]
    out_specs = [pl.BlockSpec((tm, D), row), pl.BlockSpec((tm, D), row), pl.BlockSpec((tm, 2 * F), row),
                 pl.BlockSpec((tm, F), row), pl.BlockSpec((tm, F), row)]
    out_shape = [jax.ShapeDtypeStruct((s, D), F32), jax.ShapeDtypeStruct((s, D), BF16),
                 jax.ShapeDtypeStruct((s, 2 * F), BF16), jax.ShapeDtypeStruct((s, F), BF16),
                 jax.ShapeDtypeStruct((s, F), BF16)]
    args = [x_in, g, w_up_t, w_dw, b_dw, w_down]
    if pre is not None:
        in_specs += [pl.BlockSpec((tm, NQ), row), _resident((NQ, D)), _resident((1, D))]
        out_specs.append(pl.BlockSpec((tm, D), row))
        out_shape.append(jax.ShapeDtypeStruct((s, D), F32))
        args += list(pre)
    if head is not None:
        in_specs += [_resident((1, D)), pl.BlockSpec((tm, D), row)]
        out_specs += [pl.BlockSpec((1, 128), const), pl.BlockSpec((1, D), const)]
        out_shape += [jax.ShapeDtypeStruct((1, 128), F32), jax.ShapeDtypeStruct((1, D), F32)]
        args += list(head)
    return pl.pallas_call(
        _carried(body, len(args), len(out_shape), ex, s // tm), grid=(s // tm,),
        in_specs=in_specs + _ex(ex, "in_specs"), out_specs=out_specs + _ex(ex, "out_specs"),
        out_shape=out_shape + _ex(ex, "out_shape"),
        scratch_shapes=[pltpu.VMEM((FFN_HALO, F), F32), pltpu.VMEM((tm + FFN_HALO, LANE), F32)] + _ex(ex, "scratch"),
        compiler_params=_params("arbitrary"), name=f"ffn{tag}_fwd")(*args, *ex_args)


def _ffn_bwd(tag, dx_out, x_in, g, w_up_t, w_dw, w_down, up, gate, *, tm):
    s = x_in.shape[0]
    tm = min(tm, s)
    n_i = s // tm

    def body(dx_ref, x_ref, g_ref, up_ref, gate_ref, wup_ref, wdw_ref, wd_ref,
             dxin_ref, dup_ref, dg_ref, dwb_ref, carry_ref, da0_ref, da1_ref, extd_ref):
        i = pl.program_id(0)

        @pl.when(i == 0)
        def _():
            carry_ref[...] = jnp.zeros_like(carry_ref)
            dg_ref[...] = jnp.zeros_like(dg_ref)
            dwb_ref[...] = jnp.zeros_like(dwb_ref)

        dxf = dx_ref[...]
        dxb = dxf.astype(BF16)
        da_refs = (da0_ref, da1_ref)

        def da_mm(p):
            ks = slice(p * FFN_PAIR, (p + 1) * FFN_PAIR)
            da_refs[p % 2][...] = lax.dot_general(dxb, wd_ref[ks, :], NT_DIMS, preferred_element_type=F32)

        def elementwise(p):
            da_ref = da_refs[p % 2]
            for c in range(2):
                cc = 2 * p + c
                cs = slice(cc * LANE, (cc + 1) * LANE)
                vs = slice(F + cc * LANE, F + (cc + 1) * LANE)
                gate = gate_ref[:, cs].astype(F32)
                sg = _sigmoid(gate)
                silu = gate * sg
                dac = da_ref[:, c * LANE:(c + 1) * LANE]
                dup_ref[:, vs] = (dac * silu).astype(BF16)
                dgate = dac * up_ref[:, vs].astype(F32) * (sg + silu * (1.0 - sg))
                extd_ref[pl.ds(0, tm), :] = dgate
                extd_ref[pl.ds(tm, FFN_HALO), :] = carry_ref[:, cs]
                d1 = extd_ref[pl.ds(1, tm), :]
                d2 = extd_ref[pl.ds(2, tm), :]
                dup_ref[:, cs] = (wdw_ref[2:3, cs] * dgate + wdw_ref[1:2, cs] * d1 + wdw_ref[0:1, cs] * d2).astype(BF16)
                carry_ref[:, cs] = dgate[:FFN_HALO]
                gcol = up_ref[:, cs].astype(F32)
                dwb_ref[2, :, cs] += _sum8(dgate * gcol)
                dwb_ref[1, :, cs] += _sum8(d1 * gcol)
                dwb_ref[0, :, cs] += _sum8(d2 * gcol)
                dwb_ref[3, :, cs] += _sum8(dgate)

        def dh_mm(lo, hi):
            ks = slice(lo * FFN_PAIR, hi * FFN_PAIR)
            vks = slice(F + lo * FFN_PAIR, F + hi * FFN_PAIR)
            return (lax.dot_general(dup_ref[:, ks], wup_ref[ks, :], NN_DIMS, preferred_element_type=F32)
                    + lax.dot_general(dup_ref[:, vks], wup_ref[vks, :], NN_DIMS, preferred_element_type=F32))

        acc = None
        da_mm(0)
        pending = None
        for lo, hi in FFN_GROUPS:
            for p in range(lo, hi):
                if p + 1 < FFN_PAIRS:
                    da_mm(p + 1)
                if pending is not None:
                    part = dh_mm(*pending)
                    acc = part if acc is None else acc + part
                    pending = None
                elementwise(p)
            pending = (lo, hi)
        acc = acc + dh_mm(*pending)
        dx, dg_rows = _rms_bwd(acc, x_ref[...], g_ref[...])
        dxin_ref[...] = dxf + dx
        dg_ref[...] += jnp.sum(dg_rows, axis=0, keepdims=True)

    rev = lambda i: (n_i - 1 - i, 0)
    return pl.pallas_call(
        body, grid=(n_i,),
        in_specs=[pl.BlockSpec((tm, D), rev), pl.BlockSpec((tm, D), rev), _resident((1, D)),
                  pl.BlockSpec((tm, 2 * F), rev), pl.BlockSpec((tm, F), rev),
                  _resident((2 * F, D)), _resident((3, F)), _resident((F, D))],
        out_specs=[pl.BlockSpec((tm, D), rev), pl.BlockSpec((tm, 2 * F), rev),
                   pl.BlockSpec((1, D), lambda i: (0, 0)), pl.BlockSpec((4, 8, F), lambda i: (0, 0, 0))],
        out_shape=[jax.ShapeDtypeStruct((s, D), F32), jax.ShapeDtypeStruct((s, 2 * F), BF16),
                   jax.ShapeDtypeStruct((1, D), F32), jax.ShapeDtypeStruct((4, 8, F), F32)],
        scratch_shapes=[pltpu.VMEM((FFN_HALO, F), F32), pltpu.VMEM((tm, FFN_PAIR), F32), pltpu.VMEM((tm, FFN_PAIR), F32),
                        pltpu.VMEM((tm + FFN_HALO, LANE), F32)],
        compiler_params=_params("arbitrary"), name=f"ffn{tag}_bwd")(dx_out, x_in, g, up, gate, w_up_t, w_dw, w_down)


CONV_HALO = 32
CONV_CHUNK = 256
CONV_ROWS = 64


def _ln(u2, g, b):
    mu = jnp.mean(u2, axis=-1, keepdims=True)
    xc = u2 - mu
    rstd = lax.rsqrt(jnp.mean(xc * xc, axis=-1, keepdims=True) + LN_EPS)
    xhat = xc * rstd
    return xhat, rstd, xhat * g + b


def _phase_taps(ph):
    first = CONV_HALO - (CONV_K - 1)
    return [(k, first + k - ph) for k in range(CONV_K) if (first + k) % 8 == ph]


def _sum8(v):
    out = v[0:8]
    for j in range(8, v.shape[0], 8):
        out = out + v[j:j + 8]
    return out


def _store_phases(src_ref, sh_ref, tt, cols=slice(None)):
    sh_ref[0] = src_ref[:, cols]
    for ph in range(1, 8):
        sh_ref[ph, pl.ds(0, tt + CONV_HALO - 8), :] = src_ref[pl.ds(ph, tt + CONV_HALO - 8), cols]


def _conv_taps_fwd(w_ref, sh_ref, u2_ref, cs, tt):
    for base in range(0, tt, CONV_ROWS):
        acc = jnp.zeros((CONV_ROWS, CONV_CHUNK), F32) + w_ref[CONV_K:CONV_K + 1, cs]
        for ph in range(8):
            for k, off in _phase_taps(ph):
                acc = acc + w_ref[k:k + 1, cs] * sh_ref[ph, pl.ds(base + off, CONV_ROWS), :]
        u2_ref[pl.ds(base, CONV_ROWS), cs] = acc.astype(BF16)


def _conv_fwd(x_in, g, w1t, b1, wdw, ln_g, ln_b, w2, b2, *, tt):
    s = x_in.shape[0]
    tt = min(tt, s)
    n_c = D // CONV_CHUNK

    def body(x_ref, g_ref, w1_ref, b1_ref, w_ref, lg_ref, lb_ref, w2_ref, b2_ref,
             o_ref, h_ref, a_ref, u2_ref, s_ref, carry_ref, ext_ref, sh_ref):
        @pl.when(pl.program_id(0) == 0)
        def _():
            carry_ref[...] = jnp.zeros_like(carry_ref)

        xf = x_ref[...]
        r = lax.rsqrt(jnp.mean(xf * xf, axis=-1, keepdims=True) + RMS_EPS)
        h = ((xf * r) * g_ref[...]).astype(BF16)
        h_ref[...] = h

        def pw1_mm(c):
            for off in (c * CONV_CHUNK, D + c * CONV_CHUNK):
                rs = slice(off, off + CONV_CHUNK)
                a_ref[:, rs] = (lax.dot_general(h, w1_ref[rs, :], NT_DIMS, preferred_element_type=F32)
                                + b1_ref[:, rs]).astype(BF16)

        pw1_mm(0)
        for c in range(n_c):
            if c + 1 < n_c:
                pw1_mm(c + 1)
            cs = slice(c * CONV_CHUNK, (c + 1) * CONV_CHUNK)
            gs = slice(D + c * CONV_CHUNK, D + (c + 1) * CONV_CHUNK)
            u = a_ref[:, cs].astype(F32) * _sigmoid(a_ref[:, gs].astype(F32))
            ext_ref[pl.ds(0, CONV_HALO), :] = carry_ref[:, cs]
            ext_ref[pl.ds(CONV_HALO, tt), :] = u
            carry_ref[:, cs] = u[tt - CONV_HALO:]
            _store_phases(ext_ref, sh_ref, tt)
            _conv_taps_fwd(w_ref, sh_ref, u2_ref, cs, tt)
        _, _, ln = _ln(u2_ref[...].astype(F32), lg_ref[...], lb_ref[...])
        sv = (ln * _sigmoid(ln)).astype(BF16)
        s_ref[...] = sv
        o_ref[...] = xf + lax.dot_general(sv, w2_ref[...], NN_DIMS, preferred_element_type=F32) + b2_ref[...]

    row = lambda i: (i, 0)
    return pl.pallas_call(
        body, grid=(s // tt,),
        in_specs=[pl.BlockSpec((tt, D), row), _resident((1, D)), _resident((2 * D, D)), _resident((1, 2 * D)),
                  _resident((32, D)), _resident((1, D)), _resident((1, D)), _resident((D, D)), _resident((1, D))],
        out_specs=[pl.BlockSpec((tt, D), row), pl.BlockSpec((tt, D), row), pl.BlockSpec((tt, 2 * D), row),
                   pl.BlockSpec((tt, D), row), pl.BlockSpec((tt, D), row)],
        out_shape=[jax.ShapeDtypeStruct((s, D), F32), jax.ShapeDtypeStruct((s, D), BF16),
                   jax.ShapeDtypeStruct((s, 2 * D), BF16), jax.ShapeDtypeStruct((s, D), BF16),
                   jax.ShapeDtypeStruct((s, D), BF16)],
        scratch_shapes=[pltpu.VMEM((CONV_HALO, D), F32), pltpu.VMEM((tt + CONV_HALO, CONV_CHUNK), F32),
                        pltpu.VMEM((8, tt + CONV_HALO, CONV_CHUNK), F32)],
        compiler_params=_params("arbitrary"), name="conv_fwd")(x_in, g, w1t, b1, wdw, ln_g, ln_b, w2, b2)


def _conv_bwd(dx_out, x_in, g, a, u2, w1t, wdw, ln_g, ln_b, w2, *, tt, ex=None, ex_args=()):
    s = x_in.shape[0]
    tt = min(tt, s)
    hb = tt // CONV_HALO
    n_i = s // tt
    te = tt + CONV_HALO
    n_c = D // CONV_CHUNK

    def body(dx_ref, x_ref, g_ref, a_ref, prev_ref, u2_ref, w1_ref, w_ref, lg_ref, lb_ref, w2_ref,
             dxin_ref, da_ref, dg_ref, acc_ref, db1_ref, db2_ref,
             carry_ref, e2_ref, ext_ref, ush_ref, dsh_ref, du_ref):
        i = pl.program_id(0)

        @pl.when(i == 0)
        def _():
            for ref in (dg_ref, acc_ref, db1_ref, db2_ref, carry_ref):
                ref[...] = jnp.zeros_like(ref)

        dxf = dx_ref[...]
        db2_ref[...] += _sum8(dxf)
        dse = lax.dot_general(dxf.astype(BF16), w2_ref[...], NT_DIMS, preferred_element_type=F32)
        gv = lg_ref[...]
        xhat, rstd, ln = _ln(u2_ref[...].astype(F32), gv, lb_ref[...])
        sg = _sigmoid(ln)
        dln = dse * (sg * (1.0 + ln * (1.0 - sg)))
        acc_ref[32] += _sum8(dln * xhat)
        acc_ref[33] += _sum8(dln)
        dxh = dln * gv
        du2 = rstd * (dxh - jnp.mean(dxh, axis=-1, keepdims=True) - xhat * jnp.mean(dxh * xhat, axis=-1, keepdims=True))
        e2_ref[pl.ds(0, tt), :] = du2
        e2_ref[pl.ds(tt, CONV_HALO), :] = carry_ref[...]
        carry_ref[...] = du2[:CONV_HALO]
        first_tile = i == n_i - 1

        def dh_mm(c):
            cs = slice(c * CONV_CHUNK, (c + 1) * CONV_CHUNK)
            gs = slice(D + c * CONV_CHUNK, D + (c + 1) * CONV_CHUNK)
            return (lax.dot_general(da_ref[:, cs], w1_ref[cs, :], NN_DIMS, preferred_element_type=F32)
                    + lax.dot_general(da_ref[:, gs], w1_ref[gs, :], NN_DIMS, preferred_element_type=F32))

        dh = None
        for c in range(n_c):
            if c >= 1:
                part = dh_mm(c - 1)
                dh = part if dh is None else dh + part
            cs = slice(c * CONV_CHUNK, (c + 1) * CONV_CHUNK)
            gs = slice(D + c * CONV_CHUNK, D + (c + 1) * CONV_CHUNK)
            uh = prev_ref[:, cs].astype(F32) * _sigmoid(prev_ref[:, gs].astype(F32))
            ext_ref[pl.ds(0, CONV_HALO), :] = jnp.where(first_tile, 0.0, uh)
            a1 = a_ref[:, cs].astype(F32)
            sg2 = _sigmoid(a_ref[:, gs].astype(F32))
            ext_ref[pl.ds(CONV_HALO, tt), :] = a1 * sg2
            _store_phases(ext_ref, ush_ref, tt)
            _store_phases(e2_ref, dsh_ref, tt, cs)
            for base in range(0, tt, CONV_ROWS):
                d0 = e2_ref[pl.ds(base, CONV_ROWS), cs]
                du = jnp.zeros((CONV_ROWS, CONV_CHUNK), F32)
                for ph in range(8):
                    for k, off in _phase_taps(ph):
                        acc_ref[k, :, cs] += _sum8(d0 * ush_ref[ph, pl.ds(base + off, CONV_ROWS), :])
                    for k in range(CONV_K):
                        if (CONV_K - 1 - k) % 8 == ph:
                            du = du + w_ref[k:k + 1, cs] * dsh_ref[ph, pl.ds(base + CONV_K - 1 - k - ph, CONV_ROWS), :]
                acc_ref[CONV_K, :, cs] += _sum8(d0)
                du_ref[pl.ds(base, CONV_ROWS), :] = du
            du = du_ref[...]
            da1 = du * sg2
            da2 = du * a1 * (sg2 * (1.0 - sg2))
            da_ref[:, cs] = da1.astype(BF16)
            da_ref[:, gs] = da2.astype(BF16)
            db1_ref[:, cs] += _sum8(da1)
            db1_ref[:, gs] += _sum8(da2)
        dh = dh + dh_mm(n_c - 1)
        dx, dg_rows = _rms_bwd(dh, x_ref[...], g_ref[...])
        dxin_ref[...] = dxf + dx
        dg_ref[...] += jnp.sum(dg_rows, axis=0, keepdims=True)

    rev = lambda i: (n_i - 1 - i, 0)
    return pl.pallas_call(
        _carried(body, 11, 6, ex, n_i), grid=(n_i,),
        in_specs=[pl.BlockSpec((tt, D), rev), pl.BlockSpec((tt, D), rev), _resident((1, D)),
                  pl.BlockSpec((tt, 2 * D), rev),
                  pl.BlockSpec((CONV_HALO, 2 * D), lambda i: (jnp.maximum((n_i - 1 - i) * hb - 1, 0), 0)),
                  pl.BlockSpec((tt, D), rev), _resident((2 * D, D)), _resident((32, D)), _resident((1, D)),
                  _resident((1, D)), _resident((D, D))] + _ex(ex, "in_specs"),
        out_specs=[pl.BlockSpec((tt, D), rev), pl.BlockSpec((tt, 2 * D), rev), pl.BlockSpec((1, D), lambda i: (0, 0)),
                   pl.BlockSpec((40, 8, D), lambda i: (0, 0, 0)), pl.BlockSpec((8, 2 * D), lambda i: (0, 0)),
                   pl.BlockSpec((8, D), lambda i: (0, 0))] + _ex(ex, "out_specs"),
        out_shape=[jax.ShapeDtypeStruct((s, D), F32), jax.ShapeDtypeStruct((s, 2 * D), BF16),
                   jax.ShapeDtypeStruct((1, D), F32), jax.ShapeDtypeStruct((40, 8, D), F32),
                   jax.ShapeDtypeStruct((8, 2 * D), F32), jax.ShapeDtypeStruct((8, D), F32)] + _ex(ex, "out_shape"),
        scratch_shapes=[pltpu.VMEM((CONV_HALO, D), F32), pltpu.VMEM((te, D), F32), pltpu.VMEM((te, CONV_CHUNK), F32),
                        pltpu.VMEM((8, te, CONV_CHUNK), F32), pltpu.VMEM((8, te, CONV_CHUNK), F32),
                        pltpu.VMEM((tt, CONV_CHUNK), F32)] + _ex(ex, "scratch"),
        compiler_params=_params("arbitrary"), name="conv_bwd")(dx_out, x_in, g, a, a, u2, w1t, wdw, ln_g, ln_b, w2, *ex_args)


def _adamw(name, w, g, m, v):
    rows, cols = w.shape
    tr = _row_tile(rows, 256)

    def body(w_ref, g_ref, m_ref, v_ref, d_ref, nm_ref, nv_ref):
        gv = g_ref[...]
        m2 = ADAM_B1 * m_ref[...] + (1.0 - ADAM_B1) * gv
        v2 = ADAM_B2 * v_ref[...] + (1.0 - ADAM_B2) * (gv * gv)
        m_hat = m2 / (1.0 - ADAM_B1 ** ADAM_STEP)
        v_hat = v2 / (1.0 - ADAM_B2 ** ADAM_STEP)
        d_ref[...] = -ADAM_LR * (m_hat / (jnp.sqrt(v_hat) + ADAM_EPS) + ADAM_WD * w_ref[...])
        nm_ref[...] = m2
        nv_ref[...] = v2

    spec = pl.BlockSpec((tr, cols), lambda i: (i, 0))
    return pl.pallas_call(
        body, grid=(rows // tr,), in_specs=[spec] * 4, out_specs=[spec] * 3,
        out_shape=[jax.ShapeDtypeStruct((rows, cols), F32)] * 3,
        compiler_params=_params("parallel"), name=name)(w, g, m, v)


def _sum_slabs(name, buf):
    rows = buf.shape[1]
    tr = _row_tile(rows, 512, 16)

    def body(b_ref, o_ref):
        acc = b_ref[0].astype(F32)
        for k in range(1, N_DEV):
            acc = acc + b_ref[k].astype(F32)
        o_ref[...] = acc

    return pl.pallas_call(
        body, grid=(rows // tr,), in_specs=[pl.BlockSpec((N_DEV, tr, D), lambda i: (0, i, 0))],
        out_specs=pl.BlockSpec((tr, D), lambda i: (i, 0)), out_shape=jax.ShapeDtypeStruct((rows, D), F32),
        compiler_params=_params("parallel"), name=name)(buf)


def _ex(ex, field):
    return list(getattr(ex, field)) if ex is not None else []


def _me():
    x, y, c = lax.axis_index("x"), lax.axis_index("y"), lax.axis_index("c")
    return x, y, c, 4 * x + 2 * y + c


def _peer(x, y, c, k):
    return (x ^ (k >> 2), y ^ ((k >> 1) & 1), c ^ (k & 1))


class _Gather:
    def __init__(self, mats):
        self.mats = tuple(mats)
        self.rows = sum(MAT_ROWS[i] for i in self.mats)
        any_spec = pl.BlockSpec(memory_space=pl.ANY)
        self.n_in, self.n_out = 1, len(self.mats)
        self.in_specs = [any_spec]
        self.out_specs = [any_spec] * self.n_out
        self.out_shape = [jax.ShapeDtypeStruct((N_DEV * MAT_ROWS[i], D), BF16) for i in self.mats]
        self.scratch = [pltpu.SemaphoreType.DMA((N_DEV - 1,)), pltpu.SemaphoreType.DMA((N_DEV - 1,)),
                        pltpu.SemaphoreType.DMA]
        assert max(s.shape[0] for s in self.out_shape) >= self.rows

    def start(self, ins, outs, scr):
        (slab_ref,), (send_sems, recv_sems, local_sem) = ins, scr
        x, y, c, me = _me()
        for n, i in enumerate(self.mats):
            src = slab_ref.at[pl.ds(MAT_OFF[i], MAT_ROWS[i])]
            dst = outs[n].at[pl.ds(me * MAT_ROWS[i], MAT_ROWS[i])]
            pltpu.make_async_copy(src, dst, local_sem).start()
            for k in range(1, N_DEV):
                pltpu.make_async_remote_copy(src_ref=src, dst_ref=dst, send_sem=send_sems.at[k - 1],
                                             recv_sem=recv_sems.at[k - 1], device_id=_peer(x, y, c, k),
                                             device_id_type=MESH).start()

    def wait(self, ins, outs, scr):
        send_sems, recv_sems, local_sem = scr
        x, y, c, _ = _me()
        big = max(range(self.n_out), key=lambda n: self.out_shape[n].shape[0])
        whole = outs[big].at[pl.ds(0, self.rows)]
        for k in range(1, N_DEV):
            pltpu.make_async_remote_copy(src_ref=whole, dst_ref=whole, send_sem=send_sems.at[k - 1],
                                         recv_sem=recv_sems.at[k - 1], device_id=_peer(x, y, c, k),
                                         device_id_type=MESH).wait()
        pltpu.make_async_copy(whole, whole, local_sem).wait()


class _Scatter:
    def __init__(self, mats):
        self.mats = tuple(mats)
        self.rows = sum(MAT_ROWS[i] for i in self.mats)
        any_spec = pl.BlockSpec(memory_space=pl.ANY)
        self.n_in, self.n_out = len(self.mats), 1
        self.in_specs = [any_spec] * self.n_in
        self.out_specs = [any_spec]
        self.out_shape = [jax.ShapeDtypeStruct((N_DEV, self.rows, D), BF16)]
        self.scratch = [pltpu.SemaphoreType.DMA((N_DEV - 1,)), pltpu.SemaphoreType.DMA((N_DEV - 1,)),
                        pltpu.SemaphoreType.DMA]

    def offsets(self):
        return [sum(MAT_ROWS[i] for i in self.mats[:n]) for n in range(len(self.mats))]

    def start(self, ins, outs, scr):
        (buf_ref,), (send_sems, recv_sems, local_sem) = outs, scr
        x, y, c, me = _me()
        for n, (i, off) in enumerate(zip(self.mats, self.offsets())):
            r = MAT_ROWS[i]
            pltpu.make_async_copy(ins[n].at[pl.ds(me * r, r)], buf_ref.at[0, pl.ds(off, r)], local_sem).start()
            for k in range(1, N_DEV):
                pltpu.make_async_remote_copy(src_ref=ins[n].at[pl.ds((me ^ k) * r, r)], dst_ref=buf_ref.at[k, pl.ds(off, r)],
                                             send_sem=send_sems.at[k - 1], recv_sem=recv_sems.at[k - 1],
                                             device_id=_peer(x, y, c, k), device_id_type=MESH).start()

    def wait(self, ins, outs, scr):
        (buf_ref,), (send_sems, recv_sems, local_sem) = outs, scr
        x, y, c, _ = _me()
        whole = buf_ref.at[0]
        for k in range(1, N_DEV):
            pltpu.make_async_remote_copy(src_ref=whole, dst_ref=whole, send_sem=send_sems.at[k - 1],
                                         recv_sem=recv_sems.at[k - 1], device_id=_peer(x, y, c, k),
                                         device_id_type=MESH).wait()
        pltpu.make_async_copy(whole, whole, local_sem).wait()


def _carried(body, n_in, n_out, ex, n_steps):
    if ex is None:
        return body

    def wrapped(*refs):
        ins, ex_ins = refs[:n_in], refs[n_in:n_in + ex.n_in]
        p = n_in + ex.n_in
        outs, ex_outs = refs[p:p + n_out], refs[p + n_out:p + n_out + ex.n_out]
        p += n_out + ex.n_out
        n_scr = len(refs) - p - len(ex.scratch)
        scr, ex_scr = refs[p:p + n_scr], refs[p + n_scr:]

        @pl.when(pl.program_id(0) == 0)
        def _():
            ex.start(ex_ins, ex_outs, ex_scr)

        body(*ins, *outs, *scr)

        @pl.when(pl.program_id(0) == n_steps - 1)
        def _():
            ex.wait(ex_ins, ex_outs, ex_scr)

    return wrapped


def _exchange_small(name, ex, ex_args, small, reduce):
    vmem_spec = pl.BlockSpec(memory_space=pltpu.VMEM)
    n_small = small.shape[0]

    def body(*refs):
        ex_ins, small_ref = refs[:ex.n_in], refs[ex.n_in]
        p = ex.n_in + 1
        ex_outs, res_ref = refs[p:p + ex.n_out], refs[p + ex.n_out]
        p += ex.n_out + 1
        if reduce:
            all_ref, p = refs[p], p + 1
        else:
            all_ref = res_ref
        sm_send, sm_recv = refs[p], refs[p + 1]
        ex_scr = refs[p + 2:]
        x, y, c, me = _me()
        ex.start(ex_ins, ex_outs, ex_scr)
        copies = [pltpu.make_async_remote_copy(
            src_ref=small_ref, dst_ref=all_ref.at[me], send_sem=sm_send.at[k - 1], recv_sem=sm_recv.at[k - 1],
            device_id=_peer(x, y, c, k), device_id_type=MESH) for k in range(1, N_DEV)]
        for cp in copies:
            cp.start()
        all_ref[me] = small_ref[...]
        for cp in copies:
            cp.wait()
        if reduce:
            acc = all_ref[0]
            for d in range(1, N_DEV):
                acc = acc + all_ref[d]
            res_ref[...] = acc
        ex.wait(ex_ins, ex_outs, ex_scr)

    res_shape = (n_small, D) if reduce else (N_DEV, n_small, D)
    scratch = ([pltpu.VMEM((N_DEV, n_small, D), F32)] if reduce else []) + [
        pltpu.SemaphoreType.DMA((N_DEV - 1,)), pltpu.SemaphoreType.DMA((N_DEV - 1,))] + ex.scratch
    outs = pl.pallas_call(
        body, in_specs=ex.in_specs + [vmem_spec], out_specs=ex.out_specs + [vmem_spec],
        out_shape=ex.out_shape + [jax.ShapeDtypeStruct(res_shape, F32)], scratch_shapes=scratch,
        compiler_params=pltpu.CompilerParams(vmem_limit_bytes=VMEM_LIMIT_BYTES), name=name)(*ex_args, small)
    return outs


def _pack_rows(arrs):
    rows = []
    for a in arrs:
        flat = a.reshape(-1).astype(F32)
        pad = (-flat.shape[0]) % D
        rows.append(jnp.pad(flat, (0, pad)).reshape(-1, D))
    slab = jnp.concatenate(rows, axis=0)
    return jnp.pad(slab, ((0, (-slab.shape[0]) % 8), (0, 0)))


def _unpack_rows(slab, shapes):
    out, r = [], 0
    for shp in shapes:
        size = math.prod(shp)
        nrows = -(-size // D)
        out.append(slab[r:r + nrows].reshape(-1)[:size].reshape(shp))
        r += nrows
    return out


def kernel(x, norm_mix, attn_w_qkv, attn_b_qkv, attn_sinks, attn_w_o, attn_b_o, conv_w_pw1, conv_b_pw1, conv_w_dw, conv_b_dw, conv_ln_g, conv_ln_b, conv_w_pw2, conv_b_pw2, norm_ffn, ffn_w_up, ffn_w_dw, ffn_b_dw, ffn_w_down, final_norm, loss_target, m_norm_mix, m_attn_w_qkv, m_attn_b_qkv, m_attn_sinks, m_attn_w_o, m_attn_b_o, m_conv_w_pw1, m_conv_b_pw1, m_conv_w_dw, m_conv_b_dw, m_conv_ln_g, m_conv_ln_b, m_conv_w_pw2, m_conv_b_pw2, m_norm_ffn, m_ffn_w_up, m_ffn_w_dw, m_ffn_b_dw, m_ffn_w_down, m_final_norm, v_norm_mix, v_attn_w_qkv, v_attn_b_qkv, v_attn_sinks, v_attn_w_o, v_attn_b_o, v_conv_w_pw1, v_conv_b_pw1, v_conv_w_dw, v_conv_b_dw, v_conv_ln_g, v_conv_ln_b, v_conv_w_pw2, v_conv_b_pw2, v_norm_ffn, v_ffn_w_up, v_ffn_w_dw, v_ffn_b_dw, v_ffn_w_down, v_final_norm):
    s = x.shape[1]
    me = 4 * lax.axis_index("x") + 2 * lax.axis_index("y") + lax.axis_index("c")
    x0 = x.reshape(s, D)
    tgt = loss_target.reshape(s, D)

    slab = jnp.concatenate([attn_w_qkv[0].T, attn_w_o[0], conv_w_pw1[0].T, conv_w_pw2[0],
                            ffn_w_up[0].T, ffn_w_down[0], ffn_w_up[1].T, ffn_w_down[1]], axis=0).astype(BF16)
    sharded_small = [conv_b_pw1, conv_w_dw, conv_b_dw, conv_ln_g, conv_ln_b, conv_b_pw2, ffn_w_dw]
    small_local = _pack_rows(sharded_small)
    w_qkv_t, small_all = _exchange_small("gather_first", _Gather((0,)), (slab,), small_local, reduce=False)
    parts = [_unpack_rows(small_all[d], [a.shape for a in sharded_small]) for d in range(N_DEV)]
    b_pw1, w_cdw, b_cdw, ln_g, ln_b, b_pw2, w_fdw = [jnp.concatenate([parts[d][i] for d in range(N_DEV)], axis=-1)
                                                      for i in range(len(sharded_small))]
    conv_w32 = jnp.concatenate([w_cdw[0], b_cdw], axis=0)
    sr = attn_sinks.reshape(N_KV, 4, 2)
    sink_full = jnp.broadcast_to(jnp.repeat(jnp.transpose(sr, (0, 2, 1)), BLOCK, axis=-1).reshape(4, 512, 1),
                                 (4, 512, 256))

    qkv, h0 = _mm("qkv", x0, w_qkv_t, nt=True, tm=1024, tn=NQ + KVW, out_dtype=BF16, rms_g=norm_mix[0:1], bias=attn_b_qkv)
    o, lse, w_o, w_up0_t, w_dn0 = _attn_fwd(qkv, sink_full, _Gather((1, 4, 5)), (slab,))
    x2, h1, up0, act0, gate0, x1, w_pw1_t, w_pw2, w_up1_t, w_dn1 = _ffn_fwd(
        0, x0, norm_ffn[0:1], w_up0_t, w_fdw[0], ffn_b_dw[0:1], w_dn0, tm=256, pre=(o, w_o, attn_b_o),
        ex=_Gather((2, 3, 6, 7)), ex_args=(slab,))
    x3, h2, a, u2, s_act = _conv_fwd(x2, norm_mix[1:2], w_pw1_t, b_pw1, conv_w32, ln_g, ln_b, w_pw2, b_pw2, tt=256)
    dx4, h3, up1, act1, gate1, loss_acc, dg_final = _ffn_fwd(1, x3, norm_ffn[1:2], w_up1_t, w_fdw[1], ffn_b_dw[1:2], w_dn1,
                                                             tm=256, head=(final_norm.reshape(1, D), tgt))
    loss = lax.psum(loss_acc[0, 0], ("x", "y", "c"))

    dx3, d_up1, dg_f1, dwb1 = _ffn_bwd(1, dx4, x3, norm_ffn[1:2], w_up1_t, w_fdw[1], w_dn1, up1, gate1, tm=256)
    dw_dn1 = _mm_tn("ffn1_dwdown", act1, dx4, tn=FFN_HALF, tt=2048)
    dw_up1_t = _mm_tn("ffn1_dwup", d_up1, h3, tn=FFN_HALF, tt=2048)
    dw_pw2 = _mm_tn("dw_pw2", s_act, dx3, tn=D, tt=2048)
    scatter_a = _Scatter((6, 7))
    dx2, da, dg_m1, conv_acc8, db_pw1_8, db_pw2_8, buf_a = _conv_bwd(
        dx3, x2, norm_mix[1:2], a, u2, w_pw1_t, conv_w32, ln_g, ln_b, w_pw2, tt=256, ex=scatter_a, ex_args=(dw_up1_t, dw_dn1))
    conv_acc = jnp.sum(conv_acc8, axis=1)
    db_pw1 = jnp.sum(db_pw1_8, axis=0, keepdims=True)
    db_pw2 = jnp.sum(db_pw2_8, axis=0, keepdims=True)
    dw_pw1_t = _mm_tn("dw_pw1", da, h2, tn=D, tt=2048)
    dx1, d_up0, dg_f0, dwb0 = _ffn_bwd(0, dx2, x1, norm_ffn[0:1], w_up0_t, w_fdw[0], w_dn0, up0, gate0, tm=256)
    dw_dn0 = _mm_tn("ffn0_dwdown", act0, dx2, tn=FFN_HALF, tt=2048)
    dw_up0_t = _mm_tn("ffn0_dwup", d_up0, h1, tn=FFN_HALF, tt=2048)
    do = _mm("d_attn_out", dx1, w_o, nt=True, tm=1024, tn=D, out_dtype=BF16)
    dw_o, db_o = _mm_tn("dw_o", o, dx1, tn=D, tt=2048, colsum_r=True)
    scatter_b = _Scatter((1, 2, 3, 4, 5))
    dq, dkv, dsk, buf_b = _attn_bwd(qkv, do, o, lse, sink_full, scatter_b, (dw_o, dw_pw1_t, dw_pw2, dw_up0_t, dw_dn0))
    dqkv = jnp.concatenate([dq, dkv], axis=1)
    grad_x, dg_m0 = _dgrad_rms("d_qkv", dqkv, w_qkv_t, x0, norm_mix[0:1], dx1, tm=1024, tk=NQ + KVW)
    dw_qkv_t, db_qkv = _mm_tn("dw_qkv", dqkv, h0, tn=NQ + KVW, tt=2048, colsum_l=True)

    dwb0, dwb1 = jnp.sum(dwb0, axis=1), jnp.sum(dwb1, axis=1)
    d_sinks = jnp.transpose(jnp.sum(dsk.reshape(N_KV, 2, 4, BLOCK), axis=-1), (0, 2, 1)).reshape(1, 16)
    small_grads = [jnp.concatenate([dg_m0, dg_m1], axis=0), db_qkv, d_sinks, db_o, jnp.concatenate([dg_f0, dg_f1], axis=0),
                   jnp.stack([dwb0[3], dwb1[3]]), dg_final, db_pw1, conv_acc[0:CONV_K], conv_acc[31:32], conv_acc[32:33],
                   conv_acc[33:34], db_pw2, jnp.stack([dwb0[0:3], dwb1[0:3]])]
    small_shapes = [(2, D), (1, NQ + KVW), (1, 16), (1, D), (2, D), (2, F), (D,), (1, 2 * D), (1, CONV_K, D), (1, D),
                    (1, D), (1, D), (1, D), (2, 3, F)]
    scatter_c = _Scatter((0,))
    buf_c, small_sum = _exchange_small("scatter_last", scatter_c, (dw_qkv_t,), _pack_rows(small_grads), reduce=True)
    gm = [None] * N_MAT
    for tag, sc, buf in (("a", scatter_a, buf_a), ("b", scatter_b, buf_b), ("c", scatter_c, buf_c)):
        gslab = _sum_slabs("sum_slabs_" + tag, buf)
        for i, off in zip(sc.mats, sc.offsets()):
            gm[i] = gslab[off:off + MAT_ROWS[i]]
    (g_norm_mix, g_b_qkv, g_sinks, g_b_o, g_norm_ffn, g_ffn_b_dw, g_final, g_b_pw1_full, g_w_cdw_full, g_b_cdw_full,
     g_ln_g_full, g_ln_b_full, g_b_pw2_full, g_w_fdw_full) = _unpack_rows(small_sum, small_shapes)

    def shard(a, width):
        return lax.dynamic_slice_in_dim(a, me * width, width, axis=a.ndim - 1)

    grads = {
        "norm_mix": g_norm_mix, "attn_w_qkv": gm[0].T[None], "attn_b_qkv": g_b_qkv, "attn_sinks": g_sinks,
        "attn_w_o": gm[1][None], "attn_b_o": g_b_o, "conv_w_pw1": gm[2].T[None], "conv_b_pw1": shard(g_b_pw1_full, 256),
        "conv_w_dw": shard(g_w_cdw_full, 128), "conv_b_dw": shard(g_b_cdw_full, 128), "conv_ln_g": shard(g_ln_g_full, 128),
        "conv_ln_b": shard(g_ln_b_full, 128), "conv_w_pw2": gm[3][None], "conv_b_pw2": shard(g_b_pw2_full, 128),
        "norm_ffn": g_norm_ffn, "ffn_w_up": jnp.stack([gm[4].T, gm[6].T]), "ffn_w_dw": shard(g_w_fdw_full, 352),
        "ffn_b_dw": g_ffn_b_dw, "ffn_w_down": jnp.stack([gm[5], gm[7]]), "final_norm": g_final,
    }
    weights = dict(norm_mix=norm_mix, attn_w_qkv=attn_w_qkv, attn_b_qkv=attn_b_qkv, attn_sinks=attn_sinks, attn_w_o=attn_w_o, attn_b_o=attn_b_o, conv_w_pw1=conv_w_pw1, conv_b_pw1=conv_b_pw1, conv_w_dw=conv_w_dw, conv_b_dw=conv_b_dw, conv_ln_g=conv_ln_g, conv_ln_b=conv_ln_b, conv_w_pw2=conv_w_pw2, conv_b_pw2=conv_b_pw2, norm_ffn=norm_ffn, ffn_w_up=ffn_w_up, ffn_w_dw=ffn_w_dw, ffn_b_dw=ffn_b_dw, ffn_w_down=ffn_w_down, final_norm=final_norm)
    moms = dict(norm_mix=m_norm_mix, attn_w_qkv=m_attn_w_qkv, attn_b_qkv=m_attn_b_qkv, attn_sinks=m_attn_sinks, attn_w_o=m_attn_w_o, attn_b_o=m_attn_b_o, conv_w_pw1=m_conv_w_pw1, conv_b_pw1=m_conv_b_pw1, conv_w_dw=m_conv_w_dw, conv_b_dw=m_conv_b_dw, conv_ln_g=m_conv_ln_g, conv_ln_b=m_conv_ln_b, conv_w_pw2=m_conv_w_pw2, conv_b_pw2=m_conv_b_pw2, norm_ffn=m_norm_ffn, ffn_w_up=m_ffn_w_up, ffn_w_dw=m_ffn_w_dw, ffn_b_dw=m_ffn_b_dw, ffn_w_down=m_ffn_w_down, final_norm=m_final_norm)
    vars_ = dict(norm_mix=v_norm_mix, attn_w_qkv=v_attn_w_qkv, attn_b_qkv=v_attn_b_qkv, attn_sinks=v_attn_sinks, attn_w_o=v_attn_w_o, attn_b_o=v_attn_b_o, conv_w_pw1=v_conv_w_pw1, conv_b_pw1=v_conv_b_pw1, conv_w_dw=v_conv_w_dw, conv_b_dw=v_conv_b_dw, conv_ln_g=v_conv_ln_g, conv_ln_b=v_conv_ln_b, conv_w_pw2=v_conv_w_pw2, conv_b_pw2=v_conv_b_pw2, norm_ffn=v_norm_ffn, ffn_w_up=v_ffn_w_up, ffn_w_dw=v_ffn_w_dw, ffn_b_dw=v_ffn_b_dw, ffn_w_down=v_ffn_w_down, final_norm=v_final_norm)
    names = list(weights)
    big = ("attn_w_qkv", "attn_w_o", "conv_w_pw1", "conv_w_pw2", "ffn_w_up", "ffn_w_down")
    delta, new_m, new_v = {}, {}, {}
    for nm in big:
        shp = weights[nm].shape
        two_d = (shp[0] * shp[1], shp[2])
        res = _adamw("adamw_" + nm, *(t[nm].reshape(two_d) for t in (weights, grads, moms, vars_)))
        delta[nm], new_m[nm], new_v[nm] = (r.reshape(shp) for r in res)
    small_names = [nm for nm in names if nm not in big]
    small_shapes_local = [weights[nm].shape for nm in small_names]
    res = _adamw("adamw_small", *(_pack_rows([t[nm] for nm in small_names]) for t in (weights, grads, moms, vars_)))
    for tgt_dict, slab_out in zip((delta, new_m, new_v), res):
        for nm, val in zip(small_names, _unpack_rows(slab_out, small_shapes_local)):
            tgt_dict[nm] = val
    return (loss, grad_x.reshape(1, s, D), *[grads[nm] for nm in names], *[delta[nm] for nm in names],
            *[new_m[nm] for nm in names], *[new_v[nm] for nm in names])
```

```python
import functools
import math

import jax
import jax.numpy as jnp
from jax import lax
from jax.experimental import pallas as pl
from jax.experimental.pallas import tpu as pltpu

F32 = jnp.float32
BF16 = jnp.bfloat16

D = 1024
F = 2816
HEAD_DIM = 64
N_KV = 2
BLOCK = 128
NQ = 1024
KVW = 256
CONV_K = 31
RMS_EPS = 1e-6
LN_EPS = 1e-5
ADAM_LR, ADAM_B1, ADAM_B2, ADAM_EPS, ADAM_WD, ADAM_STEP = 0.001, 0.9, 0.999, 1e-08, 0.01, 10
N_DEV = 8
MESH = pl.DeviceIdType.MESH
VMEM_LIMIT_BYTES = 56 * 2**20
NEG = float(jnp.finfo(jnp.float32).min)

MAT_ROWS = (160, 128, 256, 128, 704, 352, 704, 352)
MAT_OFF = tuple(sum(MAT_ROWS[:i]) for i in range(len(MAT_ROWS)))
SLAB_ROWS = sum(MAT_ROWS)
N_MAT = len(MAT_ROWS)

NT_DIMS = (((1,), (1,)), ((), ()))
NN_DIMS = (((1,), (0,)), ((), ()))
TN_DIMS = (((0,), (0,)), ((), ()))


def _params(*sem):
    return pltpu.CompilerParams(dimension_semantics=tuple(sem), vmem_limit_bytes=VMEM_LIMIT_BYTES)


def _row_tile(rows, cap, mult=8):
    best = None
    for t in range(mult, min(rows, cap) + 1, mult):
        if rows % t == 0:
            best = t
    return best if best is not None else rows


def _sigmoid(x):
    return 1.0 / (1.0 + jnp.exp(-x))


def _mm(name, a, b, *, nt, tm, tn, out_dtype, rms_g=None, bias=None, res=None):
    m, k = a.shape
    n = b.shape[0] if nt else b.shape[1]
    tm = min(tm, m)
    has_rms = rms_g is not None
    dims = NT_DIMS if nt else NN_DIMS

    def body(*refs):
        a_ref, b_ref = refs[0], refs[1]
        pos = 2
        g_ref = bias_ref = res_ref = h_ref = hs_ref = None
        if has_rms:
            g_ref = refs[pos]; pos += 1
        if bias is not None:
            bias_ref = refs[pos]; pos += 1
        if res is not None:
            res_ref = refs[pos]; pos += 1
        o_ref = refs[pos]; pos += 1
        if has_rms:
            h_ref, hs_ref = refs[pos], refs[pos + 1]

            @pl.when(pl.program_id(1) == 0)
            def _():
                xf = a_ref[...]
                r = lax.rsqrt(jnp.mean(xf * xf, axis=-1, keepdims=True) + RMS_EPS)
                h = ((xf * r) * g_ref[...]).astype(BF16)
                hs_ref[...] = h
                h_ref[...] = h

            lhs = hs_ref[...]
        else:
            lhs = a_ref[...].astype(BF16)
        acc = lax.dot_general(lhs, b_ref[...], dims, preferred_element_type=F32)
        if bias is not None:
            acc = acc + bias_ref[...]
        if res is not None:
            acc = acc + res_ref[...]
        o_ref[...] = acc.astype(out_dtype)

    in_specs = [pl.BlockSpec((tm, k), lambda i, j: (i, 0)),
                pl.BlockSpec((tn, k), lambda i, j: (j, 0)) if nt else pl.BlockSpec((k, tn), lambda i, j: (0, j))]
    args = [a, b]
    if has_rms:
        in_specs.append(pl.BlockSpec((1, k), lambda i, j: (0, 0))); args.append(rms_g)
    if bias is not None:
        in_specs.append(pl.BlockSpec((1, tn), lambda i, j: (0, j))); args.append(bias)
    if res is not None:
        in_specs.append(pl.BlockSpec((tm, tn), lambda i, j: (i, j))); args.append(res)
    out_shape = [jax.ShapeDtypeStruct((m, n), out_dtype)]
    out_specs = [pl.BlockSpec((tm, tn), lambda i, j: (i, j))]
    scratch = []
    if has_rms:
        out_shape.append(jax.ShapeDtypeStruct((m, k), BF16))
        out_specs.append(pl.BlockSpec((tm, k), lambda i, j: (i, 0)))
        scratch.append(pltpu.VMEM((tm, k), BF16))
    outs = pl.pallas_call(body, grid=(m // tm, n // tn), in_specs=in_specs, out_specs=out_specs, out_shape=out_shape,
                          scratch_shapes=scratch, compiler_params=_params("parallel", "arbitrary"), name=name)(*args)
    return outs if has_rms else outs[0]


def _mm_tn(name, l, r, *, tn, tt, colsum_l=False, colsum_r=False):
    s, nl = l.shape
    nr = r.shape[1]
    tt = min(tt, s)
    n_t = s // tt

    def body(*refs):
        l_ref, r_ref, o_ref = refs[0], refs[1], refs[2]
        pos = 3
        cl_ref = cr_ref = None
        if colsum_l:
            cl_ref = refs[pos]; pos += 1
        if colsum_r:
            cr_ref = refs[pos]; pos += 1
        acc_ref = refs[pos]
        j, t = pl.program_id(0), pl.program_id(1)

        @pl.when(t == 0)
        def _():
            acc_ref[...] = jnp.zeros_like(acc_ref)

        lv = l_ref[...]
        rv = r_ref[...]
        acc_ref[...] += lax.dot_general(lv, rv.astype(BF16), TN_DIMS, preferred_element_type=F32)
        if colsum_l:
            @pl.when(t == 0)
            def _():
                cl_ref[...] = jnp.zeros_like(cl_ref)

            cl_ref[...] += jnp.sum(lv.astype(F32), axis=0, keepdims=True)
        if colsum_r:
            @pl.when((t == 0) & (j == 0))
            def _():
                cr_ref[...] = jnp.zeros_like(cr_ref)

            @pl.when(j == 0)
            def _():
                cr_ref[...] += jnp.sum(rv.astype(F32), axis=0, keepdims=True)

        @pl.when(t == n_t - 1)
        def _():
            o_ref[...] = acc_ref[...].astype(BF16)

    out_shape = [jax.ShapeDtypeStruct((nl, nr), BF16)]
    out_specs = [pl.BlockSpec((tn, nr), lambda j, t: (j, 0))]
    if colsum_l:
        out_shape.append(jax.ShapeDtypeStruct((1, nl), F32))
        out_specs.append(pl.BlockSpec((1, tn), lambda j, t: (0, j)))
    if colsum_r:
        out_shape.append(jax.ShapeDtypeStruct((1, nr), F32))
        out_specs.append(pl.BlockSpec((1, nr), lambda j, t: (0, 0)))
    outs = pl.pallas_call(
        body, grid=(nl // tn, n_t),
        in_specs=[pl.BlockSpec((tt, tn), lambda j, t: (t, j)), pl.BlockSpec((tt, nr), lambda j, t: (t, 0))],
        out_specs=out_specs, out_shape=out_shape, scratch_shapes=[pltpu.VMEM((tn, nr), F32)],
        compiler_params=_params("arbitrary", "arbitrary"), name=name)(l, r)
    return outs if (colsum_l or colsum_r) else outs[0]


def _rms_bwd(dh, xf, g):
    r = lax.rsqrt(jnp.mean(xf * xf, axis=-1, keepdims=True) + RMS_EPS)
    gd = dh * g
    dx = r * gd - xf * ((r * r * r) * jnp.mean(xf * gd, axis=-1, keepdims=True))
    return dx, dh * (xf * r)


def _dgrad_rms(name, dy, w, x, g, dres, *, tm, tk):
    m, k = dy.shape
    tm = min(tm, m)
    n_k = k // tk

    def body(dy_ref, w_ref, x_ref, g_ref, dres_ref, o_ref, dg_ref, acc_ref):
        i, kk = pl.program_id(0), pl.program_id(1)

        @pl.when(kk == 0)
        def _():
            acc_ref[...] = jnp.zeros_like(acc_ref)

        acc_ref[...] += lax.dot_general(dy_ref[...], w_ref[...], NN_DIMS, preferred_element_type=F32)

        @pl.when((i == 0) & (kk == n_k - 1))
        def _():
            dg_ref[...] = jnp.zeros_like(dg_ref)

        @pl.when(kk == n_k - 1)
        def _():
            dx, dg_rows = _rms_bwd(acc_ref[...], x_ref[...], g_ref[...])
            o_ref[...] = dres_ref[...] + dx
            dg_ref[...] += jnp.sum(dg_rows, axis=0, keepdims=True)

    return pl.pallas_call(
        body, grid=(m // tm, n_k),
        in_specs=[pl.BlockSpec((tm, tk), lambda i, kk: (i, kk)), pl.BlockSpec((tk, D), lambda i, kk: (kk, 0)),
                  pl.BlockSpec((tm, D), lambda i, kk: (i, 0)), pl.BlockSpec((1, D), lambda i, kk: (0, 0)),
                  pl.BlockSpec((tm, D), lambda i, kk: (i, 0))],
        out_specs=[pl.BlockSpec((tm, D), lambda i, kk: (i, 0)), pl.BlockSpec((1, D), lambda i, kk: (0, 0))],
        out_shape=[jax.ShapeDtypeStruct((m, D), F32), jax.ShapeDtypeStruct((1, D), F32)],
        scratch_shapes=[pltpu.VMEM((tm, D), F32)],
        compiler_params=_params("arbitrary", "arbitrary"), name=name)(dy, w, x, g, dres)


def _band(kvp, kvc):
    kk = jnp.concatenate([kvp[:, :128], kvc[:, :128]], axis=0)
    vv = jnp.concatenate([kvp[:, 128:], kvc[:, 128:]], axis=0)
    row = lax.broadcasted_iota(jnp.int32, kk.shape, 0)
    return jnp.where(row == 0, jnp.zeros_like(kk), kk), jnp.where(row == 0, jnp.zeros_like(vv), vv)


def _blockdiag(t, h):
    lane = lax.broadcasted_iota(jnp.int32, t.shape, 1)
    z = jnp.zeros_like(t)
    tr = pltpu.roll(t, 64, 1)
    if h == 0:
        top, bot = jnp.where(lane < 64, t, z), jnp.where(lane >= 64, tr, z)
    else:
        top, bot = jnp.where(lane < 64, tr, z), jnp.where(lane >= 64, t, z)
    return jnp.concatenate([top, bot], axis=0)


def _from_blockdiag(t, h):
    lane = lax.broadcasted_iota(jnp.int32, (256, 128), 1)
    top, bot = t[:256], t[256:]
    if h == 0:
        return jnp.where(lane < 64, top + pltpu.roll(bot, 64, 1), 0.0)
    return jnp.where(lane >= 64, pltpu.roll(top, 64, 1) + bot, 0.0)


def _stack_heads(ref, h):
    return jnp.concatenate([ref[:, h * 512 + p * 128: h * 512 + (p + 1) * 128] for p in range(4)], axis=0)


def _masked_scores(sc_half, n, sink):
    row = lax.broadcasted_iota(jnp.int32, (512, 256), 0) & 127
    col = lax.broadcasted_iota(jnp.int32, (512, 256), 1)
    valid = (col > row) & (col <= row + BLOCK) & ((col >= BLOCK) | (n > 0))
    return jnp.where(col == 0, sink, jnp.where(valid, sc_half, NEG))


def _half_selector(shape, split):
    row = lax.broadcasted_iota(jnp.int32, shape, 0)
    one, zero = jnp.ones(shape, BF16), jnp.zeros(shape, BF16)
    return jnp.where(row < split, one, zero), jnp.where(row >= split, one, zero)


def _attn_fwd(qkv, sink_full, ex=None, ex_args=()):
    s = qkv.shape[0]
    nb = s // BLOCK
    scale = 1.0 / math.sqrt(HEAD_DIM)

    def body(q_ref, kvc_ref, kvp_ref, sk_ref, o_ref, lse_ref):
        n = pl.program_id(0)
        kk, vv = _band(kvp_ref[...], kvc_ref[...])
        sel0, sel1 = _half_selector((512, 128), 256)
        lane = lax.broadcasted_iota(jnp.int32, (512, 128), 1)
        for h in range(N_KV):
            kbd, vbd = _blockdiag(kk, h), _blockdiag(vv, h)
            sc = lax.dot_general(_stack_heads(q_ref, h), kbd, NT_DIMS, preferred_element_type=F32) * scale
            pes, mxs = [], []
            for half in range(2):
                sh = _masked_scores(sc[:, half * 256:(half + 1) * 256], n, sk_ref[h * 2 + half])
                mx = jnp.max(sh, axis=-1, keepdims=True)
                pes.append(jnp.exp(sh - mx).astype(BF16))
                mxs.append(mx)
            pb = jnp.concatenate(pes, axis=1)
            o_un = lax.dot_general(pb, vbd, NN_DIMS, preferred_element_type=F32)
            sums = [lax.dot_general(pb, sel, NN_DIMS, preferred_element_type=F32) for sel in (sel0, sel1)]
            o = o_un * jnp.where(lane < 64, 1.0 / sums[0], 1.0 / sums[1])
            for half in range(2):
                lse_ref[h * 2 + half] = mxs[half] + jnp.log(sums[half])
            for p in range(4):
                o_ref[:, h * 512 + p * 128: h * 512 + (p + 1) * 128] = o[p * 128:(p + 1) * 128].astype(BF16)

    return pl.pallas_call(
        _carried(body, 4, 2, ex, nb), grid=(nb,),
        in_specs=[pl.BlockSpec((BLOCK, NQ), lambda n: (n, 0)),
                  pl.BlockSpec((BLOCK, KVW), lambda n: (n, NQ // KVW)),
                  pl.BlockSpec((BLOCK, KVW), lambda n: (jnp.maximum(n - 1, 0), NQ // KVW)),
                  pl.BlockSpec((4, 512, 256), lambda n: (0, 0, 0))] + _ex(ex, "in_specs"),
        out_specs=[pl.BlockSpec((BLOCK, NQ), lambda n: (n, 0)),
                   pl.BlockSpec((None, 4, 512, 128), lambda n: (n, 0, 0, 0))] + _ex(ex, "out_specs"),
        out_shape=[jax.ShapeDtypeStruct((s, NQ), BF16), jax.ShapeDtypeStruct((nb, 4, 512, 128), F32)] + _ex(ex, "out_shape"),
        scratch_shapes=_ex(ex, "scratch"),
        compiler_params=_params("arbitrary"), name="attn_fwd")(qkv, qkv, qkv, sink_full, *ex_args)


def _attn_bwd(qkv, do, o, lse, sink_full, ex=None, ex_args=()):
    s = qkv.shape[0]
    nb = s // BLOCK
    scale = 1.0 / math.sqrt(HEAD_DIM)

    def body(q_ref, kvc_ref, kvp_ref, do_ref, o_ref, lse_ref, sk_ref, dq_ref, dkv_ref, dsk_ref, ck_ref, cv_ref):
        n = pl.program_id(0)

        @pl.when(n == 0)
        def _():
            dsk_ref[...] = jnp.zeros_like(dsk_ref)
            ck_ref[...] = jnp.zeros_like(ck_ref)
            cv_ref[...] = jnp.zeros_like(cv_ref)

        @pl.when(n < nb)
        def _():
            kk, vv = _band(kvp_ref[...], kvc_ref[...])
            sel0, sel1 = _half_selector((128, 128), 64)
            dkk = jnp.zeros((256, 128), F32)
            dvv = jnp.zeros((256, 128), F32)
            for h in range(N_KV):
                kbd, vbd = _blockdiag(kk, h), _blockdiag(vv, h)
                qp = _stack_heads(q_ref, h)
                dop = _stack_heads(do_ref, h)
                sc = lax.dot_general(qp, kbd, NT_DIMS, preferred_element_type=F32) * scale
                dp = lax.dot_general(dop, vbd, NT_DIMS, preferred_element_type=F32)
                dd = (dop.astype(F32) * _stack_heads(o_ref, h).astype(F32)).astype(BF16)
                probs, dss = [], []
                for half, sel in enumerate((sel0, sel1)):
                    delta = lax.dot_general(dd, sel, NN_DIMS, preferred_element_type=F32)
                    lse_h = lse_ref[h * 2 + half]
                    sh = _masked_scores(sc[:, half * 256:(half + 1) * 256], n, sk_ref[h * 2 + half])
                    pr = jnp.exp(sh - jnp.concatenate([lse_h, lse_h], axis=1))
                    dsf = pr * (dp[:, half * 256:(half + 1) * 256] - jnp.concatenate([delta, delta], axis=1))
                    dsk_ref[h * 2 + half] += dsf[:, 0:1]
                    dss.append((dsf * scale).astype(BF16))
                    probs.append(pr.astype(BF16))
                ds = jnp.concatenate(dss, axis=1)
                pb = jnp.concatenate(probs, axis=1)
                dqp = lax.dot_general(ds, kbd, NN_DIMS, preferred_element_type=F32)
                for p in range(4):
                    dq_ref[:, h * 512 + p * 128: h * 512 + (p + 1) * 128] = dqp[p * 128:(p + 1) * 128].astype(BF16)
                dkk = dkk + _from_blockdiag(lax.dot_general(ds, qp, TN_DIMS, preferred_element_type=F32), h)
                dvv = dvv + _from_blockdiag(lax.dot_general(pb, dop, TN_DIMS, preferred_element_type=F32), h)
            row = lax.broadcasted_iota(jnp.int32, (256, 128), 0)
            dkk = jnp.where(row == 0, 0.0, dkk)
            dvv = jnp.where(row == 0, 0.0, dvv)

            @pl.when(n >= 1)
            def _():
                dkv_ref[:, :128] = (ck_ref[...] + dkk[:128]).astype(BF16)
                dkv_ref[:, 128:] = (cv_ref[...] + dvv[:128]).astype(BF16)

            ck_ref[...] = dkk[128:]
            cv_ref[...] = dvv[128:]

        @pl.when(n == nb)
        def _():
            dkv_ref[:, :128] = ck_ref[...].astype(BF16)
            dkv_ref[:, 128:] = cv_ref[...].astype(BF16)

    last = nb - 1
    blk = lambda n: (jnp.minimum(n, last), 0)
    return pl.pallas_call(
        _carried(body, 7, 3, ex, nb + 1), grid=(nb + 1,),
        in_specs=[pl.BlockSpec((BLOCK, NQ), blk),
                  pl.BlockSpec((BLOCK, KVW), lambda n: (jnp.minimum(n, last), NQ // KVW)),
                  pl.BlockSpec((BLOCK, KVW), lambda n: (jnp.clip(n - 1, 0, last), NQ // KVW)),
                  pl.BlockSpec((BLOCK, NQ), blk), pl.BlockSpec((BLOCK, NQ), blk),
                  pl.BlockSpec((None, 4, 512, 128), lambda n: (jnp.minimum(n, last), 0, 0, 0)),
                  pl.BlockSpec((4, 512, 256), lambda n: (0, 0, 0))] + _ex(ex, "in_specs"),
        out_specs=[pl.BlockSpec((BLOCK, NQ), blk),
                   pl.BlockSpec((BLOCK, KVW), lambda n: (jnp.maximum(n - 1, 0), 0)),
                   pl.BlockSpec((4, 512, 1), lambda n: (0, 0, 0))] + _ex(ex, "out_specs"),
        out_shape=[jax.ShapeDtypeStruct((s, NQ), BF16), jax.ShapeDtypeStruct((s, KVW), BF16),
                   jax.ShapeDtypeStruct((4, 512, 1), F32)] + _ex(ex, "out_shape"),
        scratch_shapes=[pltpu.VMEM((BLOCK, 128), F32), pltpu.VMEM((BLOCK, 128), F32)] + _ex(ex, "scratch"),
        compiler_params=_params("arbitrary"), name="attn_bwd")(qkv, qkv, qkv, do, o, lse, sink_full, *ex_args)


LANE = 128
FFN_HALF = F // 2
FFN_PAIR = 2 * LANE
FFN_PAIRS = F // FFN_PAIR
FFN_GROUPS = ((0, 2), (2, 4), (4, 6), (6, 8), (8, 10), (10, 11))
FFN_HALO = 8


def _resident(shape):
    return pl.BlockSpec(shape, lambda i: (0,) * len(shape), pipeline_mode=pl.Buffered(1))


def _ffn_fwd(tag, x_in, g, w_up_t, w_dw, b_dw, w_down, *, tm, pre=None, head=None, ex=None, ex_args=()):
    s = x_in.shape[0]
    tm = min(tm, s)
    n_pre = 3 if pre is not None else 0
    n_head = 2 if head is not None else 0

    def body(*refs):
        x_ref, g_ref, wup_ref, wdw_ref, bdw_ref, wd_ref = refs[:6]
        pre_refs = refs[6:6 + n_pre]
        head_refs = refs[6 + n_pre:6 + n_pre + n_head]
        p = 6 + n_pre + n_head
        o_ref, h_ref, up_ref, act_ref, gate_ref = refs[p:p + 5]
        p += 5
        if pre is not None:
            xmid_ref, p = refs[p], p + 1
        if head is not None:
            loss_ref, dgf_ref, p = refs[p], refs[p + 1], p + 2
        carry_ref, ext_ref = refs[p], refs[p + 1]

        @pl.when(pl.program_id(0) == 0)
        def _():
            carry_ref[...] = jnp.zeros_like(carry_ref)
            if head is not None:
                loss_ref[...] = jnp.zeros_like(loss_ref)
                dgf_ref[...] = jnp.zeros_like(dgf_ref)

        xf = x_ref[...]
        if pre is not None:
            oin_ref, wo_ref, bo_ref = pre_refs
            xf = xf + lax.dot_general(oin_ref[...], wo_ref[...], NN_DIMS, preferred_element_type=F32) + bo_ref[...]
            xmid_ref[...] = xf
        r = lax.rsqrt(jnp.mean(xf * xf, axis=-1, keepdims=True) + RMS_EPS)
        h = ((xf * r) * g_ref[...]).astype(BF16)
        h_ref[...] = h

        def up_mm(p):
            for off in (p * FFN_PAIR, F + p * FFN_PAIR):
                rs = slice(off, off + FFN_PAIR)
                up_ref[:, rs] = lax.dot_general(h, wup_ref[rs, :], NT_DIMS, preferred_element_type=F32).astype(BF16)

        def elementwise(p):
            for c in (2 * p, 2 * p + 1):
                cs = slice(c * LANE, (c + 1) * LANE)
                vs = slice(F + c * LANE, F + (c + 1) * LANE)
                gcol = up_ref[:, cs].astype(F32)
                ext_ref[pl.ds(0, FFN_HALO), :] = carry_ref[:, cs]
                ext_ref[pl.ds(FFN_HALO, tm), :] = gcol
                gate_b = (wdw_ref[2:3, cs] * gcol + wdw_ref[1:2, cs] * ext_ref[pl.ds(FFN_HALO - 1, tm), :]
                          + wdw_ref[0:1, cs] * ext_ref[pl.ds(FFN_HALO - 2, tm), :] + bdw_ref[:, cs]).astype(BF16)
                gate_ref[:, cs] = gate_b
                gate = gate_b.astype(F32)
                act_ref[:, cs] = (gate * _sigmoid(gate) * up_ref[:, vs].astype(F32)).astype(BF16)
                carry_ref[:, cs] = gcol[tm - FFN_HALO:]

        def down_mm(lo, hi):
            ks = slice(lo * FFN_PAIR, hi * FFN_PAIR)
            return lax.dot_general(act_ref[:, ks], wd_ref[ks, :], NN_DIMS, preferred_element_type=F32)

        acc = xf
        up_mm(0)
        pending = None
        for lo, hi in FFN_GROUPS:
            for p in range(lo, hi):
                if p + 1 < FFN_PAIRS:
                    up_mm(p + 1)
                if pending is not None:
                    acc = acc + down_mm(*pending)
                    pending = None
                elementwise(p)
            pending = (lo, hi)
        x_out = acc + down_mm(*pending)
        if head is None:
            o_ref[...] = x_out
        else:
            gf_ref, tgt_ref = head_refs
            gf = gf_ref[...]
            rf = lax.rsqrt(jnp.mean(x_out * x_out, axis=-1, keepdims=True) + RMS_EPS)
            diff = (x_out * rf) * gf - tgt_ref[...]
            loss_ref[...] += 0.5 * jnp.sum(jnp.mean(diff * diff, axis=-1, keepdims=True))
            dx, dg_rows = _rms_bwd(diff * (1.0 / D), x_out, gf)
            o_ref[...] = dx
            dgf_ref[...] += jnp.sum(dg_rows, axis=0, keepdims=True)

    row = lambda i: (i, 0)
    const = lambda i: (0, 0)
    in_specs = [pl.BlockSpec((tm, D), row), _resident((1, D)), _resident((2 * F, D)), _resident((3, F)),
                _resident((1, F)), _resident((F, D))]
    out_specs = [pl.BlockSpec((tm, D), row), pl.BlockSpec((tm, D), row), pl.BlockSpec((tm, 2 * F), row),
                 pl.BlockSpec((tm, F), row), pl.BlockSpec((tm, F), row)]
    out_shape = [jax.ShapeDtypeStruct((s, D), F32), jax.ShapeDtypeStruct((s, D), BF16),
                 jax.ShapeDtypeStruct((s, 2 * F), BF16), jax.ShapeDtypeStruct((s, F), BF16),
                 jax.ShapeDtypeStruct((s, F), BF16)]
    args = [x_in, g, w_up_t, w_dw, b_dw, w_down]
    if pre is not None:
        in_specs += [pl.BlockSpec((tm, NQ), row), _resident((NQ, D)), _resident((1, D))]
        out_specs.append(pl.BlockSpec((tm, D), row))
        out_shape.append(jax.ShapeDtypeStruct((s, D), F32))
        args += list(pre)
    if head is not None:
        in_specs += [_resident((1, D)), pl.BlockSpec((tm, D), row)]
        out_specs += [pl.BlockSpec((1, 128), const), pl.BlockSpec((1, D), const)]
        out_shape += [jax.ShapeDtypeStruct((1, 128), F32), jax.ShapeDtypeStruct((1, D), F32)]
        args += list(head)
    return pl.pallas_call(
        _carried(body, len(args), len(out_shape), ex, s // tm), grid=(s // tm,),
        in_specs=in_specs + _ex(ex, "in_specs"), out_specs=out_specs + _ex(ex, "out_specs"),
        out_shape=out_shape + _ex(ex, "out_shape"),
        scratch_shapes=[pltpu.VMEM((FFN_HALO, F), F32), pltpu.VMEM((tm + FFN_HALO, LANE), F32)] + _ex(ex, "scratch"),
        compiler_params=_params("arbitrary"), name=f"ffn{tag}_fwd")(*args, *ex_args)


def _ffn_bwd(tag, dx_out, x_in, g, w_up_t, w_dw, w_down, up, gate, *, tm):
    s = x_in.shape[0]
    tm = min(tm, s)
    n_i = s // tm

    def body(dx_ref, x_ref, g_ref, up_ref, gate_ref, wup_ref, wdw_ref, wd_ref,
             dxin_ref, dup_ref, dg_ref, dwb_ref, carry_ref, da0_ref, da1_ref, extd_ref):
        i = pl.program_id(0)

        @pl.when(i == 0)
        def _():
            carry_ref[...] = jnp.zeros_like(carry_ref)
            dg_ref[...] = jnp.zeros_like(dg_ref)
            dwb_ref[...] = jnp.zeros_like(dwb_ref)

        dxf = dx_ref[...]
        dxb = dxf.astype(BF16)
        da_refs = (da0_ref, da1_ref)

        def da_mm(p):
            ks = slice(p * FFN_PAIR, (p + 1) * FFN_PAIR)
            da_refs[p % 2][...] = lax.dot_general(dxb, wd_ref[ks, :], NT_DIMS, preferred_element_type=F32)

        def elementwise(p):
            da_ref = da_refs[p % 2]
            for c in range(2):
                cc = 2 * p + c
                cs = slice(cc * LANE, (cc + 1) * LANE)
                vs = slice(F + cc * LANE, F + (cc + 1) * LANE)
                gate = gate_ref[:, cs].astype(F32)
                sg = _sigmoid(gate)
                silu = gate * sg
                dac = da_ref[:, c * LANE:(c + 1) * LANE]
                dup_ref[:, vs] = (dac * silu).astype(BF16)
                dgate = dac * up_ref[:, vs].astype(F32) * (sg + silu * (1.0 - sg))
                extd_ref[pl.ds(0, tm), :] = dgate
                extd_ref[pl.ds(tm, FFN_HALO), :] = carry_ref[:, cs]
                d1 = extd_ref[pl.ds(1, tm), :]
                d2 = extd_ref[pl.ds(2, tm), :]
                dup_ref[:, cs] = (wdw_ref[2:3, cs] * dgate + wdw_ref[1:2, cs] * d1 + wdw_ref[0:1, cs] * d2).astype(BF16)
                carry_ref[:, cs] = dgate[:FFN_HALO]
                gcol = up_ref[:, cs].astype(F32)
                dwb_ref[2, :, cs] += _sum8(dgate * gcol)
                dwb_ref[1, :, cs] += _sum8(d1 * gcol)
                dwb_ref[0, :, cs] += _sum8(d2 * gcol)
                dwb_ref[3, :, cs] += _sum8(dgate)

        def dh_mm(lo, hi):
            ks = slice(lo * FFN_PAIR, hi * FFN_PAIR)
            vks = slice(F + lo * FFN_PAIR, F + hi * FFN_PAIR)
            return (lax.dot_general(dup_ref[:, ks], wup_ref[ks, :], NN_DIMS, preferred_element_type=F32)
                    + lax.dot_general(dup_ref[:, vks], wup_ref[vks, :], NN_DIMS, preferred_element_type=F32))

        acc = None
        da_mm(0)
        pending = None
        for lo, hi in FFN_GROUPS:
            for p in range(lo, hi):
                if p + 1 < FFN_PAIRS:
                    da_mm(p + 1)
                if pending is not None:
                    part = dh_mm(*pending)
                    acc = part if acc is None else acc + part
                    pending = None
                elementwise(p)
            pending = (lo, hi)
        acc = acc + dh_mm(*pending)
        dx, dg_rows = _rms_bwd(acc, x_ref[...], g_ref[...])
        dxin_ref[...] = dxf + dx
        dg_ref[...] += jnp.sum(dg_rows, axis=0, keepdims=True)

    rev = lambda i: (n_i - 1 - i, 0)
    return pl.pallas_call(
        body, grid=(n_i,),
        in_specs=[pl.BlockSpec((tm, D), rev), pl.BlockSpec((tm, D), rev), _resident((1, D)),
                  pl.BlockSpec((tm, 2 * F), rev), pl.BlockSpec((tm, F), rev),
                  _resident((2 * F, D)), _resident((3, F)), _resident((F, D))],
        out_specs=[pl.BlockSpec((tm, D), rev), pl.BlockSpec((tm, 2 * F), rev),
                   pl.BlockSpec((1, D), lambda i: (0, 0)), pl.BlockSpec((4, 8, F), lambda i: (0, 0, 0))],
        out_shape=[jax.ShapeDtypeStruct((s, D), F32), jax.ShapeDtypeStruct((s, 2 * F), BF16),
                   jax.ShapeDtypeStruct((1, D), F32), jax.ShapeDtypeStruct((4, 8, F), F32)],
        scratch_shapes=[pltpu.VMEM((FFN_HALO, F), F32), pltpu.VMEM((tm, FFN_PAIR), F32), pltpu.VMEM((tm, FFN_PAIR), F32),
                        pltpu.VMEM((tm + FFN_HALO, LANE), F32)],
        compiler_params=_params("arbitrary"), name=f"ffn{tag}_bwd")(dx_out, x_in, g, up, gate, w_up_t, w_dw, w_down)


CONV_HALO = 32
CONV_CHUNK = 256
CONV_ROWS = 64


def _ln(u2, g, b):
    mu = jnp.mean(u2, axis=-1, keepdims=True)
    xc = u2 - mu
    rstd = lax.rsqrt(jnp.mean(xc * xc, axis=-1, keepdims=True) + LN_EPS)
    xhat = xc * rstd
    return xhat, rstd, xhat * g + b


def _phase_taps(ph):
    first = CONV_HALO - (CONV_K - 1)
    return [(k, first + k - ph) for k in range(CONV_K) if (first + k) % 8 == ph]


def _sum8(v):
    out = v[0:8]
    for j in range(8, v.shape[0], 8):
        out = out + v[j:j + 8]
    return out


def _store_phases(src_ref, sh_ref, tt, cols=slice(None)):
    sh_ref[0] = src_ref[:, cols]
    for ph in range(1, 8):
        sh_ref[ph, pl.ds(0, tt + CONV_HALO - 8), :] = src_ref[pl.ds(ph, tt + CONV_HALO - 8), cols]


def _conv_taps_fwd(w_ref, sh_ref, u2_ref, cs, tt):
    for base in range(0, tt, CONV_ROWS):
        acc = jnp.zeros((CONV_ROWS, CONV_CHUNK), F32) + w_ref[CONV_K:CONV_K + 1, cs]
        for ph in range(8):
            for k, off in _phase_taps(ph):
                acc = acc + w_ref[k:k + 1, cs] * sh_ref[ph, pl.ds(base + off, CONV_ROWS), :]
        u2_ref[pl.ds(base, CONV_ROWS), cs] = acc.astype(BF16)


def _conv_fwd(x_in, g, w1t, b1, wdw, ln_g, ln_b, w2, b2, *, tt):
    s = x_in.shape[0]
    tt = min(tt, s)
    n_c = D // CONV_CHUNK

    def body(x_ref, g_ref, w1_ref, b1_ref, w_ref, lg_ref, lb_ref, w2_ref, b2_ref,
             o_ref, h_ref, a_ref, u2_ref, s_ref, carry_ref, ext_ref, sh_ref):
        @pl.when(pl.program_id(0) == 0)
        def _():
            carry_ref[...] = jnp.zeros_like(carry_ref)

        xf = x_ref[...]
        r = lax.rsqrt(jnp.mean(xf * xf, axis=-1, keepdims=True) + RMS_EPS)
        h = ((xf * r) * g_ref[...]).astype(BF16)
        h_ref[...] = h

        def pw1_mm(c):
            for off in (c * CONV_CHUNK, D + c * CONV_CHUNK):
                rs = slice(off, off + CONV_CHUNK)
                a_ref[:, rs] = (lax.dot_general(h, w1_ref[rs, :], NT_DIMS, preferred_element_type=F32)
                                + b1_ref[:, rs]).astype(BF16)

        pw1_mm(0)
        for c in range(n_c):
            if c + 1 < n_c:
                pw1_mm(c + 1)
            cs = slice(c * CONV_CHUNK, (c + 1) * CONV_CHUNK)
            gs = slice(D + c * CONV_CHUNK, D + (c + 1) * CONV_CHUNK)
            u = a_ref[:, cs].astype(F32) * _sigmoid(a_ref[:, gs].astype(F32))
            ext_ref[pl.ds(0, CONV_HALO), :] = carry_ref[:, cs]
            ext_ref[pl.ds(CONV_HALO, tt), :] = u
            carry_ref[:, cs] = u[tt - CONV_HALO:]
            _store_phases(ext_ref, sh_ref, tt)
            _conv_taps_fwd(w_ref, sh_ref, u2_ref, cs, tt)
        _, _, ln = _ln(u2_ref[...].astype(F32), lg_ref[...], lb_ref[...])
        sv = (ln * _sigmoid(ln)).astype(BF16)
        s_ref[...] = sv
        o_ref[...] = xf + lax.dot_general(sv, w2_ref[...], NN_DIMS, preferred_element_type=F32) + b2_ref[...]

    row = lambda i: (i, 0)
    return pl.pallas_call(
        body, grid=(s // tt,),
        in_specs=[pl.BlockSpec((tt, D), row), _resident((1, D)), _resident((2 * D, D)), _resident((1, 2 * D)),
                  _resident((32, D)), _resident((1, D)), _resident((1, D)), _resident((D, D)), _resident((1, D))],
        out_specs=[pl.BlockSpec((tt, D), row), pl.BlockSpec((tt, D), row), pl.BlockSpec((tt, 2 * D), row),
                   pl.BlockSpec((tt, D), row), pl.BlockSpec((tt, D), row)],
        out_shape=[jax.ShapeDtypeStruct((s, D), F32), jax.ShapeDtypeStruct((s, D), BF16),
                   jax.ShapeDtypeStruct((s, 2 * D), BF16), jax.ShapeDtypeStruct((s, D), BF16),
                   jax.ShapeDtypeStruct((s, D), BF16)],
        scratch_shapes=[pltpu.VMEM((CONV_HALO, D), F32), pltpu.VMEM((tt + CONV_HALO, CONV_CHUNK), F32),
                        pltpu.VMEM((8, tt + CONV_HALO, CONV_CHUNK), F32)],
        compiler_params=_params("arbitrary"), name="conv_fwd")(x_in, g, w1t, b1, wdw, ln_g, ln_b, w2, b2)


def _conv_bwd(dx_out, x_in, g, a, u2, w1t, wdw, ln_g, ln_b, w2, *, tt, ex=None, ex_args=()):
    s = x_in.shape[0]
    tt = min(tt, s)
    hb = tt // CONV_HALO
    n_i = s // tt
    te = tt + CONV_HALO
    n_c = D // CONV_CHUNK

    def body(dx_ref, x_ref, g_ref, a_ref, prev_ref, u2_ref, w1_ref, w_ref, lg_ref, lb_ref, w2_ref,
             dxin_ref, da_ref, dg_ref, acc_ref, db1_ref, db2_ref,
             carry_ref, e2_ref, ext_ref, ush_ref, dsh_ref, du_ref):
        i = pl.program_id(0)

        @pl.when(i == 0)
        def _():
            for ref in (dg_ref, acc_ref, db1_ref, db2_ref, carry_ref):
                ref[...] = jnp.zeros_like(ref)

        dxf = dx_ref[...]
        db2_ref[...] += _sum8(dxf)
        dse = lax.dot_general(dxf.astype(BF16), w2_ref[...], NT_DIMS, preferred_element_type=F32)
        gv = lg_ref[...]
        xhat, rstd, ln = _ln(u2_ref[...].astype(F32), gv, lb_ref[...])
        sg = _sigmoid(ln)
        dln = dse * (sg * (1.0 + ln * (1.0 - sg)))
        acc_ref[32] += _sum8(dln * xhat)
        acc_ref[33] += _sum8(dln)
        dxh = dln * gv
        du2 = rstd * (dxh - jnp.mean(dxh, axis=-1, keepdims=True) - xhat * jnp.mean(dxh * xhat, axis=-1, keepdims=True))
        e2_ref[pl.ds(0, tt), :] = du2
        e2_ref[pl.ds(tt, CONV_HALO), :] = carry_ref[...]
        carry_ref[...] = du2[:CONV_HALO]
        first_tile = i == n_i - 1

        def dh_mm(c):
            cs = slice(c * CONV_CHUNK, (c + 1) * CONV_CHUNK)
            gs = slice(D + c * CONV_CHUNK, D + (c + 1) * CONV_CHUNK)
            return (lax.dot_general(da_ref[:, cs], w1_ref[cs, :], NN_DIMS, preferred_element_type=F32)
                    + lax.dot_general(da_ref[:, gs], w1_ref[gs, :], NN_DIMS, preferred_element_type=F32))

        dh = None
        for c in range(n_c):
            if c >= 1:
                part = dh_mm(c - 1)
                dh = part if dh is None else dh + part
            cs = slice(c * CONV_CHUNK, (c + 1) * CONV_CHUNK)
            gs = slice(D + c * CONV_CHUNK, D + (c + 1) * CONV_CHUNK)
            uh = prev_ref[:, cs].astype(F32) * _sigmoid(prev_ref[:, gs].astype(F32))
            ext_ref[pl.ds(0, CONV_HALO), :] = jnp.where(first_tile, 0.0, uh)
            a1 = a_ref[:, cs].astype(F32)
            sg2 = _sigmoid(a_ref[:, gs].astype(F32))
            ext_ref[pl.ds(CONV_HALO, tt), :] = a1 * sg2
            _store_phases(ext_ref, ush_ref, tt)
            _store_phases(e2_ref, dsh_ref, tt, cs)
            for base in range(0, tt, CONV_ROWS):
                d0 = e2_ref[pl.ds(base, CONV_ROWS), cs]
                du = jnp.zeros((CONV_ROWS, CONV_CHUNK), F32)
                for ph in range(8):
                    for k, off in _phase_taps(ph):
                        acc_ref[k, :, cs] += _sum8(d0 * ush_ref[ph, pl.ds(base + off, CONV_ROWS), :])
                    for k in range(CONV_K):
                        if (CONV_K - 1 - k) % 8 == ph:
                            du = du + w_ref[k:k + 1, cs] * dsh_ref[ph, pl.ds(base + CONV_K - 1 - k - ph, CONV_ROWS), :]
                acc_ref[CONV_K, :, cs] += _sum8(d0)
                du_ref[pl.ds(base, CONV_ROWS), :] = du
            du = du_ref[...]
            da1 = du * sg2
            da2 = du * a1 * (sg2 * (1.0 - sg2))
            da_ref[:, cs] = da1.astype(BF16)
            da_ref[:, gs] = da2.astype(BF16)
            db1_ref[:, cs] += _sum8(da1)
            db1_ref[:, gs] += _sum8(da2)
        dh = dh + dh_mm(n_c - 1)
        dx, dg_rows = _rms_bwd(dh, x_ref[...], g_ref[...])
        dxin_ref[...] = dxf + dx
        dg_ref[...] += jnp.sum(dg_rows, axis=0, keepdims=True)

    rev = lambda i: (n_i - 1 - i, 0)
    return pl.pallas_call(
        _carried(body, 11, 6, ex, n_i), grid=(n_i,),
        in_specs=[pl.BlockSpec((tt, D), rev), pl.BlockSpec((tt, D), rev), _resident((1, D)),
                  pl.BlockSpec((tt, 2 * D), rev),
                  pl.BlockSpec((CONV_HALO, 2 * D), lambda i: (jnp.maximum((n_i - 1 - i) * hb - 1, 0), 0)),
                  pl.BlockSpec((tt, D), rev), _resident((2 * D, D)), _resident((32, D)), _resident((1, D)),
                  _resident((1, D)), _resident((D, D))] + _ex(ex, "in_specs"),
        out_specs=[pl.BlockSpec((tt, D), rev), pl.BlockSpec((tt, 2 * D), rev), pl.BlockSpec((1, D), lambda i: (0, 0)),
                   pl.BlockSpec((40, 8, D), lambda i: (0, 0, 0)), pl.BlockSpec((8, 2 * D), lambda i: (0, 0)),
                   pl.BlockSpec((8, D), lambda i: (0, 0))] + _ex(ex, "out_specs"),
        out_shape=[jax.ShapeDtypeStruct((s, D), F32), jax.ShapeDtypeStruct((s, 2 * D), BF16),
                   jax.ShapeDtypeStruct((1, D), F32), jax.ShapeDtypeStruct((40, 8, D), F32),
                   jax.ShapeDtypeStruct((8, 2 * D), F32), jax.ShapeDtypeStruct((8, D), F32)] + _ex(ex, "out_shape"),
        scratch_shapes=[pltpu.VMEM((CONV_HALO, D), F32), pltpu.VMEM((te, D), F32), pltpu.VMEM((te, CONV_CHUNK), F32),
                        pltpu.VMEM((8, te, CONV_CHUNK), F32), pltpu.VMEM((8, te, CONV_CHUNK), F32),
                        pltpu.VMEM((tt, CONV_CHUNK), F32)] + _ex(ex, "scratch"),
        compiler_params=_params("arbitrary"), name="conv_bwd")(dx_out, x_in, g, a, a, u2, w1t, wdw, ln_g, ln_b, w2, *ex_args)


def _adamw(name, w, g, m, v):
    rows, cols = w.shape
    tr = _row_tile(rows, 256)

    def body(w_ref, g_ref, m_ref, v_ref, d_ref, nm_ref, nv_ref):
        gv = g_ref[...]
        m2 = ADAM_B1 * m_ref[...] + (1.0 - ADAM_B1) * gv
        v2 = ADAM_B2 * v_ref[...] + (1.0 - ADAM_B2) * (gv * gv)
        m_hat = m2 / (1.0 - ADAM_B1 ** ADAM_STEP)
        v_hat = v2 / (1.0 - ADAM_B2 ** ADAM_STEP)
        d_ref[...] = -ADAM_LR * (m_hat / (jnp.sqrt(v_hat) + ADAM_EPS) + ADAM_WD * w_ref[...])
        nm_ref[...] = m2
        nv_ref[...] = v2

    spec = pl.BlockSpec((tr, cols), lambda i: (i, 0))
    return pl.pallas_call(
        body, grid=(rows // tr,), in_specs=[spec] * 4, out_specs=[spec] * 3,
        out_shape=[jax.ShapeDtypeStruct((rows, cols), F32)] * 3,
        compiler_params=_params("parallel"), name=name)(w, g, m, v)


def _sum_slabs(name, buf):
    rows = buf.shape[1]
    tr = _row_tile(rows, 512, 16)

    def body(b_ref, o_ref):
        acc = b_ref[0].astype(F32)
        for k in range(1, N_DEV):
            acc = acc + b_ref[k].astype(F32)
        o_ref[...] = acc

    return pl.pallas_call(
        body, grid=(rows // tr,), in_specs=[pl.BlockSpec((N_DEV, tr, D), lambda i: (0, i, 0))],
        out_specs=pl.BlockSpec((tr, D), lambda i: (i, 0)), out_shape=jax.ShapeDtypeStruct((rows, D), F32),
        compiler_params=_params("parallel"), name=name)(buf)


def _ex(ex, field):
    return list(getattr(ex, field)) if ex is not None else []


def _me():
    x, y, c = lax.axis_index("x"), lax.axis_index("y"), lax.axis_index("c")
    return x, y, c, 4 * x + 2 * y + c


def _peer(x, y, c, k):
    return (x ^ (k >> 2), y ^ ((k >> 1) & 1), c ^ (k & 1))


class _Gather:
    def __init__(self, mats):
        self.mats = tuple(mats)
        self.rows = sum(MAT_ROWS[i] for i in self.mats)
        any_spec = pl.BlockSpec(memory_space=pl.ANY)
        self.n_in, self.n_out = 1, len(self.mats)
        self.in_specs = [any_spec]
        self.out_specs = [any_spec] * self.n_out
        self.out_shape = [jax.ShapeDtypeStruct((N_DEV * MAT_ROWS[i], D), BF16) for i in self.mats]
        self.scratch = [pltpu.SemaphoreType.DMA((N_DEV - 1,)), pltpu.SemaphoreType.DMA((N_DEV - 1,)),
                        pltpu.SemaphoreType.DMA]
        assert max(s.shape[0] for s in self.out_shape) >= self.rows

    def start(self, ins, outs, scr):
        (slab_ref,), (send_sems, recv_sems, local_sem) = ins, scr
        x, y, c, me = _me()
        for n, i in enumerate(self.mats):
            src = slab_ref.at[pl.ds(MAT_OFF[i], MAT_ROWS[i])]
            dst = outs[n].at[pl.ds(me * MAT_ROWS[i], MAT_ROWS[i])]
            pltpu.make_async_copy(src, dst, local_sem).start()
            for k in range(1, N_DEV):
                pltpu.make_async_remote_copy(src_ref=src, dst_ref=dst, send_sem=send_sems.at[k - 1],
                                             recv_sem=recv_sems.at[k - 1], device_id=_peer(x, y, c, k),
                                             device_id_type=MESH).start()

    def wait(self, ins, outs, scr):
        send_sems, recv_sems, local_sem = scr
        x, y, c, _ = _me()
        big = max(range(self.n_out), key=lambda n: self.out_shape[n].shape[0])
        whole = outs[big].at[pl.ds(0, self.rows)]
        for k in range(1, N_DEV):
            pltpu.make_async_remote_copy(src_ref=whole, dst_ref=whole, send_sem=send_sems.at[k - 1],
                                         recv_sem=recv_sems.at[k - 1], device_id=_peer(x, y, c, k),
                                         device_id_type=MESH).wait()
        pltpu.make_async_copy(whole, whole, local_sem).wait()


class _Scatter:
    def __init__(self, mats):
        self.mats = tuple(mats)
        self.rows = sum(MAT_ROWS[i] for i in self.mats)
        any_spec = pl.BlockSpec(memory_space=pl.ANY)
        self.n_in, self.n_out = len(self.mats), 1
        self.in_specs = [any_spec] * self.n_in
        self.out_specs = [any_spec]
        self.out_shape = [jax.ShapeDtypeStruct((N_DEV, self.rows, D), BF16)]
        self.scratch = [pltpu.SemaphoreType.DMA((N_DEV - 1,)), pltpu.SemaphoreType.DMA((N_DEV - 1,)),
                        pltpu.SemaphoreType.DMA]

    def offsets(self):
        return [sum(MAT_ROWS[i] for i in self.mats[:n]) for n in range(len(self.mats))]

    def start(self, ins, outs, scr):
        (buf_ref,), (send_sems, recv_sems, local_sem) = outs, scr
        x, y, c, me = _me()
        for n, (i, off) in enumerate(zip(self.mats, self.offsets())):
            r = MAT_ROWS[i]
            pltpu.make_async_copy(ins[n].at[pl.ds(me * r, r)], buf_ref.at[0, pl.ds(off, r)], local_sem).start()
            for k in range(1, N_DEV):
                pltpu.make_async_remote_copy(src_ref=ins[n].at[pl.ds((me ^ k) * r, r)], dst_ref=buf_ref.at[k, pl.ds(off, r)],
                                             send_sem=send_sems.at[k - 1], recv_sem=recv_sems.at[k - 1],
                                             device_id=_peer(x, y, c, k), device_id_type=MESH).start()

    def wait(self, ins, outs, scr):
        (buf_ref,), (send_sems, recv_sems, local_sem) = outs, scr
        x, y, c, _ = _me()
        whole = buf_ref.at[0]
        for k in range(1, N_DEV):
            pltpu.make_async_remote_copy(src_ref=whole, dst_ref=whole, send_sem=send_sems.at[k - 1],
                                         recv_sem=recv_sems.at[k - 1], device_id=_peer(x, y, c, k),
                                         device_id_type=MESH).wait()
        pltpu.make_async_copy(whole, whole, local_sem).wait()


def _carried(body, n_in, n_out, ex, n_steps):
    if ex is None:
        return body

    def wrapped(*refs):
        ins, ex_ins = refs[:n_in], refs[n_in:n_in + ex.n_in]
        p = n_in + ex.n_in
        outs, ex_outs = refs[p:p + n_out], refs[p + n_out:p + n_out + ex.n_out]
        p += n_out + ex.n_out
        n_scr = len(refs) - p - len(ex.scratch)
        scr, ex_scr = refs[p:p + n_scr], refs[p + n_scr:]

        @pl.when(pl.program_id(0) == 0)
        def _():
            ex.start(ex_ins, ex_outs, ex_scr)

        body(*ins, *outs, *scr)

        @pl.when(pl.program_id(0) == n_steps - 1)
        def _():
            ex.wait(ex_ins, ex_outs, ex_scr)

    return wrapped


def _exchange_small(name, ex, ex_args, small, reduce):
    vmem_spec = pl.BlockSpec(memory_space=pltpu.VMEM)
    n_small = small.shape[0]

    def body(*refs):
        ex_ins, small_ref = refs[:ex.n_in], refs[ex.n_in]
        p = ex.n_in + 1
        ex_outs, res_ref = refs[p:p + ex.n_out], refs[p + ex.n_out]
        p += ex.n_out + 1
        if reduce:
            all_ref, p = refs[p], p + 1
        else:
            all_ref = res_ref
        sm_send, sm_recv = refs[p], refs[p + 1]
        ex_scr = refs[p + 2:]
        x, y, c, me = _me()
        ex.start(ex_ins, ex_outs, ex_scr)
        copies = [pltpu.make_async_remote_copy(
            src_ref=small_ref, dst_ref=all_ref.at[me], send_sem=sm_send.at[k - 1], recv_sem=sm_recv.at[k - 1],
            device_id=_peer(x, y, c, k), device_id_type=MESH) for k in range(1, N_DEV)]
        for cp in copies:
            cp.start()
        all_ref[me] = small_ref[...]
        for cp in copies:
            cp.wait()
        if reduce:
            acc = all_ref[0]
            for d in range(1, N_DEV):
                acc = acc + all_ref[d]
            res_ref[...] = acc
        ex.wait(ex_ins, ex_outs, ex_scr)

    res_shape = (n_small, D) if reduce else (N_DEV, n_small, D)
    scratch = ([pltpu.VMEM((N_DEV, n_small, D), F32)] if reduce else []) + [
        pltpu.SemaphoreType.DMA((N_DEV - 1,)), pltpu.SemaphoreType.DMA((N_DEV - 1,))] + ex.scratch
    outs = pl.pallas_call(
        body, in_specs=ex.in_specs + [vmem_spec], out_specs=ex.out_specs + [vmem_spec],
        out_shape=ex.out_shape + [jax.ShapeDtypeStruct(res_shape, F32)], scratch_shapes=scratch,
        compiler_params=pltpu.CompilerParams(vmem_limit_bytes=VMEM_LIMIT_BYTES), name=name)(*ex_args, small)
    return outs


def _pack_rows(arrs):
    rows = []
    for a in arrs:
        flat = a.reshape(-1).astype(F32)
        pad = (-flat.shape[0]) % D
        rows.append(jnp.pad(flat, (0, pad)).reshape(-1, D))
    slab = jnp.concatenate(rows, axis=0)
    return jnp.pad(slab, ((0, (-slab.shape[0]) % 8), (0, 0)))


def _unpack_rows(slab, shapes):
    out, r = [], 0
    for shp in shapes:
        size = math.prod(shp)
        nrows = -(-size // D)
        out.append(slab[r:r + nrows].reshape(-1)[:size].reshape(shp))
        r += nrows
    return out


def kernel(x, norm_mix, attn_w_qkv, attn_b_qkv, attn_sinks, attn_w_o, attn_b_o, conv_w_pw1, conv_b_pw1, conv_w_dw, conv_b_dw, conv_ln_g, conv_ln_b, conv_w_pw2, conv_b_pw2, norm_ffn, ffn_w_up, ffn_w_dw, ffn_b_dw, ffn_w_down, final_norm, loss_target, m_norm_mix, m_attn_w_qkv, m_attn_b_qkv, m_attn_sinks, m_attn_w_o, m_attn_b_o, m_conv_w_pw1, m_conv_b_pw1, m_conv_w_dw, m_conv_b_dw, m_conv_ln_g, m_conv_ln_b, m_conv_w_pw2, m_conv_b_pw2, m_norm_ffn, m_ffn_w_up, m_ffn_w_dw, m_ffn_b_dw, m_ffn_w_down, m_final_norm, v_norm_mix, v_attn_w_qkv, v_attn_b_qkv, v_attn_sinks, v_attn_w_o, v_attn_b_o, v_conv_w_pw1, v_conv_b_pw1, v_conv_w_dw, v_conv_b_dw, v_conv_ln_g, v_conv_ln_b, v_conv_w_pw2, v_conv_b_pw2, v_norm_ffn, v_ffn_w_up, v_ffn_w_dw, v_ffn_b_dw, v_ffn_w_down, v_final_norm):
    s = x.shape[1]
    me = 4 * lax.axis_index("x") + 2 * lax.axis_index("y") + lax.axis_index("c")
    x0 = x.reshape(s, D)
    tgt = loss_target.reshape(s, D)

    slab = jnp.concatenate([attn_w_qkv[0].T, attn_w_o[0], conv_w_pw1[0].T, conv_w_pw2[0],
                            ffn_w_up[0].T, ffn_w_down[0], ffn_w_up[1].T, ffn_w_down[1]], axis=0).astype(BF16)
    sharded_small = [conv_b_pw1, conv_w_dw, conv_b_dw, conv_ln_g, conv_ln_b, conv_b_pw2, ffn_w_dw]
    small_local = _pack_rows(sharded_small)
    w_qkv_t, small_all = _exchange_small("gather_first", _Gather((0,)), (slab,), small_local, reduce=False)
    parts = [_unpack_rows(small_all[d], [a.shape for a in sharded_small]) for d in range(N_DEV)]
    b_pw1, w_cdw, b_cdw, ln_g, ln_b, b_pw2, w_fdw = [jnp.concatenate([parts[d][i] for d in range(N_DEV)], axis=-1)
                                                      for i in range(len(sharded_small))]
    conv_w32 = jnp.concatenate([w_cdw[0], b_cdw], axis=0)
    sr = attn_sinks.reshape(N_KV, 4, 2)
    sink_full = jnp.broadcast_to(jnp.repeat(jnp.transpose(sr, (0, 2, 1)), BLOCK, axis=-1).reshape(4, 512, 1),
                                 (4, 512, 256))

    qkv, h0 = _mm("qkv", x0, w_qkv_t, nt=True, tm=1024, tn=NQ + KVW, out_dtype=BF16, rms_g=norm_mix[0:1], bias=attn_b_qkv)
    o, lse, w_o, w_up0_t, w_dn0 = _attn_fwd(qkv, sink_full, _Gather((1, 4, 5)), (slab,))
    x2, h1, up0, act0, gate0, x1, w_pw1_t, w_pw2, w_up1_t, w_dn1 = _ffn_fwd(
        0, x0, norm_ffn[0:1], w_up0_t, w_fdw[0], ffn_b_dw[0:1], w_dn0, tm=256, pre=(o, w_o, attn_b_o),
        ex=_Gather((2, 3, 6, 7)), ex_args=(slab,))
    x3, h2, a, u2, s_act = _conv_fwd(x2, norm_mix[1:2], w_pw1_t, b_pw1, conv_w32, ln_g, ln_b, w_pw2, b_pw2, tt=256)
    dx4, h3, up1, act1, gate1, loss_acc, dg_final = _ffn_fwd(1, x3, norm_ffn[1:2], w_up1_t, w_fdw[1], ffn_b_dw[1:2], w_dn1,
                                                             tm=256, head=(final_norm.reshape(1, D), tgt))
    loss = lax.psum(loss_acc[0, 0], ("x", "y", "c"))

    dx3, d_up1, dg_f1, dwb1 = _ffn_bwd(1, dx4, x3, norm_ffn[1:2], w_up1_t, w_fdw[1], w_dn1, up1, gate1, tm=256)
    dw_dn1 = _mm_tn("ffn1_dwdown", act1, dx4, tn=FFN_HALF, tt=2048)
    dw_up1_t = _mm_tn("ffn1_dwup", d_up1, h3, tn=FFN_HALF, tt=2048)
    dw_pw2 = _mm_tn("dw_pw2", s_act, dx3, tn=D, tt=2048)
    scatter_a = _Scatter((6, 7))
    dx2, da, dg_m1, conv_acc8, db_pw1_8, db_pw2_8, buf_a = _conv_bwd(
        dx3, x2, norm_mix[1:2], a, u2, w_pw1_t, conv_w32, ln_g, ln_b, w_pw2, tt=256, ex=scatter_a, ex_args=(dw_up1_t, dw_dn1))
    conv_acc = jnp.sum(conv_acc8, axis=1)
    db_pw1 = jnp.sum(db_pw1_8, axis=0, keepdims=True)
    db_pw2 = jnp.sum(db_pw2_8, axis=0, keepdims=True)
    dw_pw1_t = _mm_tn("dw_pw1", da, h2, tn=D, tt=2048)
    dx1, d_up0, dg_f0, dwb0 = _ffn_bwd(0, dx2, x1, norm_ffn[0:1], w_up0_t, w_fdw[0], w_dn0, up0, gate0, tm=256)
    dw_dn0 = _mm_tn("ffn0_dwdown", act0, dx2, tn=FFN_HALF, tt=2048)
    dw_up0_t = _mm_tn("ffn0_dwup", d_up0, h1, tn=FFN_HALF, tt=2048)
    do = _mm("d_attn_out", dx1, w_o, nt=True, tm=1024, tn=D, out_dtype=BF16)
    dw_o, db_o = _mm_tn("dw_o", o, dx1, tn=D, tt=2048, colsum_r=True)
    scatter_b = _Scatter((1, 2, 3, 4, 5))
    dq, dkv, dsk, buf_b = _attn_bwd(qkv, do, o, lse, sink_full, scatter_b, (dw_o, dw_pw1_t, dw_pw2, dw_up0_t, dw_dn0))
    dqkv = jnp.concatenate([dq, dkv], axis=1)
    grad_x, dg_m0 = _dgrad_rms("d_qkv", dqkv, w_qkv_t, x0, norm_mix[0:1], dx1, tm=1024, tk=NQ + KVW)
    dw_qkv_t, db_qkv = _mm_tn("dw_qkv", dqkv, h0, tn=NQ + KVW, tt=2048, colsum_l=True)

    dwb0, dwb1 = jnp.sum(dwb0, axis=1), jnp.sum(dwb1, axis=1)
    d_sinks = jnp.transpose(jnp.sum(dsk.reshape(N_KV, 2, 4, BLOCK), axis=-1), (0, 2, 1)).reshape(1, 16)
    small_grads = [jnp.concatenate([dg_m0, dg_m1], axis=0), db_qkv, d_sinks, db_o, jnp.concatenate([dg_f0, dg_f1], axis=0),
                   jnp.stack([dwb0[3], dwb1[3]]), dg_final, db_pw1, conv_acc[0:CONV_K], conv_acc[31:32], conv_acc[32:33],
                   conv_acc[33:34], db_pw2, jnp.stack([dwb0[0:3], dwb1[0:3]])]
    small_shapes = [(2, D), (1, NQ + KVW), (1, 16), (1, D), (2, D), (2, F), (D,), (1, 2 * D), (1, CONV_K, D), (1, D),
                    (1, D), (1, D), (1, D), (2, 3, F)]
    scatter_c = _Scatter((0,))
    buf_c, small_sum = _exchange_small("scatter_last", scatter_c, (dw_qkv_t,), _pack_rows(small_grads), reduce=True)
    gm = [None] * N_MAT
    for tag, sc, buf in (("a", scatter_a, buf_a), ("b", scatter_b, buf_b), ("c", scatter_c, buf_c)):
        gslab = _sum_slabs("sum_slabs_" + tag, buf)
        for i, off in zip(sc.mats, sc.offsets()):
            gm[i] = gslab[off:off + MAT_ROWS[i]]
    (g_norm_mix, g_b_qkv, g_sinks, g_b_o, g_norm_ffn, g_ffn_b_dw, g_final, g_b_pw1_full, g_w_cdw_full, g_b_cdw_full,
     g_ln_g_full, g_ln_b_full, g_b_pw2_full, g_w_fdw_full) = _unpack_rows(small_sum, small_shapes)

    def shard(a, width):
        return lax.dynamic_slice_in_dim(a, me * width, width, axis=a.ndim - 1)

    grads = {
        "norm_mix": g_norm_mix, "attn_w_qkv": gm[0].T[None], "attn_b_qkv": g_b_qkv, "attn_sinks": g_sinks,
        "attn_w_o": gm[1][None], "attn_b_o": g_b_o, "conv_w_pw1": gm[2].T[None], "conv_b_pw1": shard(g_b_pw1_full, 256),
        "conv_w_dw": shard(g_w_cdw_full, 128), "conv_b_dw": shard(g_b_cdw_full, 128), "conv_ln_g": shard(g_ln_g_full, 128),
        "conv_ln_b": shard(g_ln_b_full, 128), "conv_w_pw2": gm[3][None], "conv_b_pw2": shard(g_b_pw2_full, 128),
        "norm_ffn": g_norm_ffn, "ffn_w_up": jnp.stack([gm[4].T, gm[6].T]), "ffn_w_dw": shard(g_w_fdw_full, 352),
        "ffn_b_dw": g_ffn_b_dw, "ffn_w_down": jnp.stack([gm[5], gm[7]]), "final_norm": g_final,
    }
    weights = dict(norm_mix=norm_mix, attn_w_qkv=attn_w_qkv, attn_b_qkv=attn_b_qkv, attn_sinks=attn_sinks, attn_w_o=attn_w_o, attn_b_o=attn_b_o, conv_w_pw1=conv_w_pw1, conv_b_pw1=conv_b_pw1, conv_w_dw=conv_w_dw, conv_b_dw=conv_b_dw, conv_ln_g=conv_ln_g, conv_ln_b=conv_ln_b, conv_w_pw2=conv_w_pw2, conv_b_pw2=conv_b_pw2, norm_ffn=norm_ffn, ffn_w_up=ffn_w_up, ffn_w_dw=ffn_w_dw, ffn_b_dw=ffn_b_dw, ffn_w_down=ffn_w_down, final_norm=final_norm)
    moms = dict(norm_mix=m_norm_mix, attn_w_qkv=m_attn_w_qkv, attn_b_qkv=m_attn_b_qkv, attn_sinks=m_attn_sinks, attn_w_o=m_attn_w_o, attn_b_o=m_attn_b_o, conv_w_pw1=m_conv_w_pw1, conv_b_pw1=m_conv_b_pw1, conv_w_dw=m_conv_w_dw, conv_b_dw=m_conv_b_dw, conv_ln_g=m_conv_ln_g, conv_ln_b=m_conv_ln_b, conv_w_pw2=m_conv_w_pw2, conv_b_pw2=m_conv_b_pw2, norm_ffn=m_norm_ffn, ffn_w_up=m_ffn_w_up, ffn_w_dw=m_ffn_w_dw, ffn_b_dw=m_ffn_b_dw, ffn_w_down=m_ffn_w_down, final_norm=m_final_norm)
    vars_ = dict(norm_mix=v_norm_mix, attn_w_qkv=v_attn_w_qkv, attn_b_qkv=v_attn_b_qkv, attn_sinks=v_attn_sinks, attn_w_o=v_attn_w_o, attn_b_o=v_attn_b_o, conv_w_pw1=v_conv_w_pw1, conv_b_pw1=v_conv_b_pw1, conv_w_dw=v_conv_w_dw, conv_b_dw=v_conv_b_dw, conv_ln_g=v_conv_ln_g, conv_ln_b=v_conv_ln_b, conv_w_pw2=v_conv_w_pw2, conv_b_pw2=v_conv_b_pw2, norm_ffn=v_norm_ffn, ffn_w_up=v_ffn_w_up, ffn_w_dw=v_ffn_w_dw, ffn_b_dw=v_ffn_b_dw, ffn_w_down=v_ffn_w_down, final_norm=v_final_norm)
    names = list(weights)
    big = ("attn_w_qkv", "attn_w_o", "conv_w_pw1", "conv_w_pw2", "ffn_w_up", "ffn_w_down")
    delta, new_m, new_v = {}, {}, {}
    for nm in big:
        shp = weights[nm].shape
        two_d = (shp[0] * shp[1], shp[2])
        res = _adamw("adamw_" + nm, *(t[nm].reshape(two_d) for t in (weights, grads, moms, vars_)))
        delta[nm], new_m[nm], new_v[nm] = (r.reshape(shp) for r in res)
    small_names = [nm for nm in names if nm not in big]
    small_shapes_local = [weights[nm].shape for nm in small_names]
    res = _adamw("adamw_small", *(_pack_rows([t[nm] for nm in small_names]) for t in (weights, grads, moms, vars_)))
    for tgt_dict, slab_out in zip((delta, new_m, new_v), res):
        for nm, val in zip(small_names, _unpack_rows(slab_out, small_shapes_local)):
            tgt_dict[nm] = val
    return (loss, grad_x.reshape(1, s, D), *[grads[nm] for nm in names], *[delta[nm] for nm in names],
            *[new_m[nm] for nm in names], *[new_v[nm] for nm in names])
```

```python
import functools
import math

import jax
import jax.numpy as jnp
from jax import lax
from jax.experimental import pallas as pl
from jax.experimental.pallas import tpu as pltpu

F32 = jnp.float32
BF16 = jnp.bfloat16

D = 1024
F = 2816
HEAD_DIM = 64
N_KV = 2
BLOCK = 128
NQ = 1024
KVW = 256
CONV_K = 31
RMS_EPS = 1e-6
LN_EPS = 1e-5
ADAM_LR, ADAM_B1, ADAM_B2, ADAM_EPS, ADAM_WD, ADAM_STEP = 0.001, 0.9, 0.999, 1e-08, 0.01, 10
N_DEV = 8
MESH = pl.DeviceIdType.MESH
VMEM_LIMIT_BYTES = 56 * 2**20
MASKED = -1e30

MAT_ROWS = (160, 128, 256, 128, 704, 352, 704, 352)
MAT_OFF = tuple(sum(MAT_ROWS[:i]) for i in range(len(MAT_ROWS)))
SLAB_ROWS = sum(MAT_ROWS)
N_MAT = len(MAT_ROWS)

NT_DIMS = (((1,), (1,)), ((), ()))
NN_DIMS = (((1,), (0,)), ((), ()))
TN_DIMS = (((0,), (0,)), ((), ()))


def _params(*sem):
    return pltpu.CompilerParams(dimension_semantics=tuple(sem), vmem_limit_bytes=VMEM_LIMIT_BYTES)


def _row_tile(rows, cap, mult=8):
    best = None
    for t in range(mult, min(rows, cap) + 1, mult):
        if rows % t == 0:
            best = t
    return best if best is not None else rows


def _sigmoid(x):
    return 1.0 / (1.0 + jnp.exp(-x))


def _mm(name, a, b, *, nt, tm, tn, out_dtype, rms_g=None, bias=None, res=None):
    m, k = a.shape
    n = b.shape[0] if nt else b.shape[1]
    tm = min(tm, m)
    has_rms = rms_g is not None
    dims = NT_DIMS if nt else NN_DIMS

    def body(*refs):
        a_ref, b_ref = refs[0], refs[1]
        pos = 2
        g_ref = bias_ref = res_ref = h_ref = hs_ref = None
        if has_rms:
            g_ref = refs[pos]; pos += 1
        if bias is not None:
            bias_ref = refs[pos]; pos += 1
        if res is not None:
            res_ref = refs[pos]; pos += 1
        o_ref = refs[pos]; pos += 1
        if has_rms:
            h_ref, hs_ref = refs[pos], refs[pos + 1]

            @pl.when(pl.program_id(1) == 0)
            def _():
                xf = a_ref[...]
                r = lax.rsqrt(jnp.mean(xf * xf, axis=-1, keepdims=True) + RMS_EPS)
                h = ((xf * r) * g_ref[...]).astype(BF16)
                hs_ref[...] = h
                h_ref[...] = h

            lhs = hs_ref[...]
        else:
            lhs = a_ref[...].astype(BF16)
        acc = lax.dot_general(lhs, b_ref[...], dims, preferred_element_type=F32)
        if bias is not None:
            acc = acc + bias_ref[...]
        if res is not None:
            acc = acc + res_ref[...]
        o_ref[...] = acc.astype(out_dtype)

    in_specs = [pl.BlockSpec((tm, k), lambda i, j: (i, 0)),
                pl.BlockSpec((tn, k), lambda i, j: (j, 0)) if nt else pl.BlockSpec((k, tn), lambda i, j: (0, j))]
    args = [a, b]
    if has_rms:
        in_specs.append(pl.BlockSpec((1, k), lambda i, j: (0, 0))); args.append(rms_g)
    if bias is not None:
        in_specs.append(pl.BlockSpec((1, tn), lambda i, j: (0, j))); args.append(bias)
    if res is not None:
        in_specs.append(pl.BlockSpec((tm, tn), lambda i, j: (i, j))); args.append(res)
    out_shape = [jax.ShapeDtypeStruct((m, n), out_dtype)]
    out_specs = [pl.BlockSpec((tm, tn), lambda i, j: (i, j))]
    scratch = []
    if has_rms:
        out_shape.append(jax.ShapeDtypeStruct((m, k), BF16))
        out_specs.append(pl.BlockSpec((tm, k), lambda i, j: (i, 0)))
        scratch.append(pltpu.VMEM((tm, k), BF16))
    outs = pl.pallas_call(body, grid=(m // tm, n // tn), in_specs=in_specs, out_specs=out_specs, out_shape=out_shape,
                          scratch_shapes=scratch, compiler_params=_params("parallel", "arbitrary"), name=name)(*args)
    return outs if has_rms else outs[0]


def _mm_tn(name, l, r, *, tn, tt, colsum_l=False, colsum_r=False):
    s, nl = l.shape
    nr = r.shape[1]
    tt = min(tt, s)
    n_t = s // tt

    def body(*refs):
        l_ref, r_ref, o_ref = refs[0], refs[1], refs[2]
        pos = 3
        cl_ref = cr_ref = None
        if colsum_l:
            cl_ref = refs[pos]; pos += 1
        if colsum_r:
            cr_ref = refs[pos]; pos += 1
        acc_ref = refs[pos]
        j, t = pl.program_id(0), pl.program_id(1)

        @pl.when(t == 0)
        def _():
            acc_ref[...] = jnp.zeros_like(acc_ref)

        lv = l_ref[...]
        rv = r_ref[...]
        acc_ref[...] += lax.dot_general(lv, rv.astype(BF16), TN_DIMS, preferred_element_type=F32)
        if colsum_l:
            @pl.when(t == 0)
            def _():
                cl_ref[...] = jnp.zeros_like(cl_ref)

            cl_ref[...] += jnp.sum(lv.astype(F32), axis=0, keepdims=True)
        if colsum_r:
            @pl.when((t == 0) & (j == 0))
            def _():
                cr_ref[...] = jnp.zeros_like(cr_ref)

            @pl.when(j == 0)
            def _():
                cr_ref[...] += jnp.sum(rv.astype(F32), axis=0, keepdims=True)

        @pl.when(t == n_t - 1)
        def _():
            o_ref[...] = acc_ref[...].astype(BF16)

    out_shape = [jax.ShapeDtypeStruct((nl, nr), BF16)]
    out_specs = [pl.BlockSpec((tn, nr), lambda j, t: (j, 0))]
    if colsum_l:
        out_shape.append(jax.ShapeDtypeStruct((1, nl), F32))
        out_specs.append(pl.BlockSpec((1, tn), lambda j, t: (0, j)))
    if colsum_r:
        out_shape.append(jax.ShapeDtypeStruct((1, nr), F32))
        out_specs.append(pl.BlockSpec((1, nr), lambda j, t: (0, 0)))
    outs = pl.pallas_call(
        body, grid=(nl // tn, n_t),
        in_specs=[pl.BlockSpec((tt, tn), lambda j, t: (t, j)), pl.BlockSpec((tt, nr), lambda j, t: (t, 0))],
        out_specs=out_specs, out_shape=out_shape, scratch_shapes=[pltpu.VMEM((tn, nr), F32)],
        compiler_params=_params("arbitrary", "arbitrary"), name=name)(l, r)
    return outs if (colsum_l or colsum_r) else outs[0]


def _rms_bwd(dh, xf, g):
    r = lax.rsqrt(jnp.mean(xf * xf, axis=-1, keepdims=True) + RMS_EPS)
    gd = dh * g
    dx = r * gd - xf * ((r * r * r) * jnp.mean(xf * gd, axis=-1, keepdims=True))
    return dx, dh * (xf * r)


def _dgrad_rms(name, dy, w, x, g, dres, *, tm, tk):
    m, k = dy.shape
    tm = min(tm, m)
    n_k = k // tk

    def body(dy_ref, w_ref, x_ref, g_ref, dres_ref, o_ref, dg_ref, acc_ref):
        i, kk = pl.program_id(0), pl.program_id(1)

        @pl.when(kk == 0)
        def _():
            acc_ref[...] = jnp.zeros_like(acc_ref)

        acc_ref[...] += lax.dot_general(dy_ref[...], w_ref[...], NN_DIMS, preferred_element_type=F32)

        @pl.when((i == 0) & (kk == n_k - 1))
        def _():
            dg_ref[...] = jnp.zeros_like(dg_ref)

        @pl.when(kk == n_k - 1)
        def _():
            dx, dg_rows = _rms_bwd(acc_ref[...], x_ref[...], g_ref[...])
            o_ref[...] = dres_ref[...] + dx
            dg_ref[...] += jnp.sum(dg_rows, axis=0, keepdims=True)

    return pl.pallas_call(
        body, grid=(m // tm, n_k),
        in_specs=[pl.BlockSpec((tm, tk), lambda i, kk: (i, kk)), pl.BlockSpec((tk, D), lambda i, kk: (kk, 0)),
                  pl.BlockSpec((tm, D), lambda i, kk: (i, 0)), pl.BlockSpec((1, D), lambda i, kk: (0, 0)),
                  pl.BlockSpec((tm, D), lambda i, kk: (i, 0))],
        out_specs=[pl.BlockSpec((tm, D), lambda i, kk: (i, 0)), pl.BlockSpec((1, D), lambda i, kk: (0, 0))],
        out_shape=[jax.ShapeDtypeStruct((m, D), F32), jax.ShapeDtypeStruct((1, D), F32)],
        scratch_shapes=[pltpu.VMEM((tm, D), F32)],
        compiler_params=_params("arbitrary", "arbitrary"), name=name)(dy, w, x, g, dres)


def _band(kvp, kvc):
    kk = jnp.concatenate([kvp[:, :128], kvc[:, :128]], axis=0)
    vv = jnp.concatenate([kvp[:, 128:], kvc[:, 128:]], axis=0)
    row = lax.broadcasted_iota(jnp.int32, kk.shape, 0)
    return jnp.where(row == 0, jnp.zeros_like(kk), kk), jnp.where(row == 0, jnp.zeros_like(vv), vv)


def _blockdiag(t, h):
    lane = lax.broadcasted_iota(jnp.int32, t.shape, 1)
    z = jnp.zeros_like(t)
    tr = pltpu.roll(t, 64, 1)
    if h == 0:
        top, bot = jnp.where(lane < 64, t, z), jnp.where(lane >= 64, tr, z)
    else:
        top, bot = jnp.where(lane < 64, tr, z), jnp.where(lane >= 64, t, z)
    return jnp.concatenate([top, bot], axis=0)


def _from_blockdiag(t, h):
    lane = lax.broadcasted_iota(jnp.int32, (256, 128), 1)
    top, bot = t[:256], t[256:]
    if h == 0:
        return jnp.where(lane < 64, top + pltpu.roll(bot, 64, 1), 0.0)
    return jnp.where(lane >= 64, pltpu.roll(top, 64, 1) + bot, 0.0)


def _stack_heads(ref, h):
    return jnp.concatenate([ref[:, h * 512 + p * 128: h * 512 + (p + 1) * 128] for p in range(4)], axis=0)


def _score_bias(sinks):
    row = jnp.arange(512)[:, None] & 127
    col = jnp.arange(256)[None, :]
    band = (col > row) & (col <= row + BLOCK)
    valid = jnp.stack([band & (col >= BLOCK), band])
    sink_rows = jnp.repeat(jnp.transpose(sinks.reshape(N_KV, 4, 2), (0, 2, 1)), BLOCK, axis=-1).reshape(4, 512, 1)
    bias = jnp.where(valid, 0.0, MASKED)[:, None] + jnp.zeros((1, 4, 1, 1), F32)
    return jnp.where(col == 0, sink_rows[None], bias).astype(F32)


def _half_selector(shape, split):
    row = lax.broadcasted_iota(jnp.int32, shape, 0)
    one, zero = jnp.ones(shape, BF16), jnp.zeros(shape, BF16)
    return jnp.where(row < split, one, zero), jnp.where(row >= split, one, zero)


def _attn_fwd(qkv, bias, ex=None, ex_args=()):
    s = qkv.shape[0]
    nb = s // BLOCK
    scale = 1.0 / math.sqrt(HEAD_DIM)

    def body(q_ref, kvc_ref, kvp_ref, sk_ref, o_ref, lse_ref):
        kk, vv = _band(kvp_ref[...], kvc_ref[...])
        sel0, sel1 = _half_selector((512, 128), 256)
        lane = lax.broadcasted_iota(jnp.int32, (512, 128), 1)
        for h in range(N_KV):
            kbd, vbd = _blockdiag(kk, h), _blockdiag(vv, h)
            sc = lax.dot_general(_stack_heads(q_ref, h), kbd, NT_DIMS, preferred_element_type=F32) * scale
            pes, mxs = [], []
            for half in range(2):
                sh = sc[:, half * 256:(half + 1) * 256] + sk_ref[h * 2 + half]
                mx = jnp.max(sh, axis=-1, keepdims=True)
                pes.append(jnp.exp(sh - mx).astype(BF16))
                mxs.append(mx)
            pb = jnp.concatenate(pes, axis=1)
            o_un = lax.dot_general(pb, vbd, NN_DIMS, preferred_element_type=F32)
            sums = [lax.dot_general(pb, sel, NN_DIMS, preferred_element_type=F32) for sel in (sel0, sel1)]
            o = o_un * jnp.where(lane < 64, 1.0 / sums[0], 1.0 / sums[1])
            for half in range(2):
                lse_ref[h * 2 + half] = mxs[half] + jnp.log(sums[half])
            for p in range(4):
                o_ref[:, h * 512 + p * 128: h * 512 + (p + 1) * 128] = o[p * 128:(p + 1) * 128].astype(BF16)

    return pl.pallas_call(
        _carried(body, 4, 2, ex, nb), grid=(nb,),
        in_specs=[pl.BlockSpec((BLOCK, NQ), lambda n: (n, 0)),
                  pl.BlockSpec((BLOCK, KVW), lambda n: (n, NQ // KVW)),
                  pl.BlockSpec((BLOCK, KVW), lambda n: (jnp.maximum(n - 1, 0), NQ // KVW)),
                  pl.BlockSpec((None, 4, 512, 256), lambda n: (jnp.minimum(n, 1), 0, 0, 0))] + _ex(ex, "in_specs"),
        out_specs=[pl.BlockSpec((BLOCK, NQ), lambda n: (n, 0)),
                   pl.BlockSpec((None, 4, 512, 128), lambda n: (n, 0, 0, 0))] + _ex(ex, "out_specs"),
        out_shape=[jax.ShapeDtypeStruct((s, NQ), BF16), jax.ShapeDtypeStruct((nb, 4, 512, 128), F32)] + _ex(ex, "out_shape"),
        scratch_shapes=_ex(ex, "scratch"),
        compiler_params=_params("arbitrary"), name="attn_fwd")(qkv, qkv, qkv, bias, *ex_args)


def _attn_bwd(qkv, do, o, lse, bias, ex=None, ex_args=()):
    s = qkv.shape[0]
    nb = s // BLOCK
    scale = 1.0 / math.sqrt(HEAD_DIM)

    def body(q_ref, kvc_ref, kvp_ref, do_ref, o_ref, lse_ref, sk_ref, dq_ref, dkv_ref, dsk_ref, ck_ref, cv_ref):
        n = pl.program_id(0)

        @pl.when(n == 0)
        def _():
            dsk_ref[...] = jnp.zeros_like(dsk_ref)
            ck_ref[...] = jnp.zeros_like(ck_ref)
            cv_ref[...] = jnp.zeros_like(cv_ref)

        @pl.when(n < nb)
        def _():
            kk, vv = _band(kvp_ref[...], kvc_ref[...])
            sel0, sel1 = _half_selector((128, 128), 64)
            dkk = jnp.zeros((256, 128), F32)
            dvv = jnp.zeros((256, 128), F32)
            for h in range(N_KV):
                kbd, vbd = _blockdiag(kk, h), _blockdiag(vv, h)
                qp = _stack_heads(q_ref, h)
                dop = _stack_heads(do_ref, h)
                sc = lax.dot_general(qp, kbd, NT_DIMS, preferred_element_type=F32) * scale
                dp = lax.dot_general(dop, vbd, NT_DIMS, preferred_element_type=F32)
                dd = (dop.astype(F32) * _stack_heads(o_ref, h).astype(F32)).astype(BF16)
                probs, dss = [], []
                for half, sel in enumerate((sel0, sel1)):
                    delta = lax.dot_general(dd, sel, NN_DIMS, preferred_element_type=F32)
                    lse_h = lse_ref[h * 2 + half]
                    sh = sc[:, half * 256:(half + 1) * 256] + sk_ref[h * 2 + half]
                    pr = jnp.exp(sh - jnp.concatenate([lse_h, lse_h], axis=1))
                    dsf = pr * (dp[:, half * 256:(half + 1) * 256] - jnp.concatenate([delta, delta], axis=1))
                    dsk_ref[h * 2 + half] += dsf[:, 0:1]
                    dss.append((dsf * scale).astype(BF16))
                    probs.append(pr.astype(BF16))
                ds = jnp.concatenate(dss, axis=1)
                pb = jnp.concatenate(probs, axis=1)
                dqp = lax.dot_general(ds, kbd, NN_DIMS, preferred_element_type=F32)
                for p in range(4):
                    dq_ref[:, h * 512 + p * 128: h * 512 + (p + 1) * 128] = dqp[p * 128:(p + 1) * 128].astype(BF16)
                dkk = dkk + _from_blockdiag(lax.dot_general(ds, qp, TN_DIMS, preferred_element_type=F32), h)
                dvv = dvv + _from_blockdiag(lax.dot_general(pb, dop, TN_DIMS, preferred_element_type=F32), h)
            row = lax.broadcasted_iota(jnp.int32, (256, 128), 0)
            dkk = jnp.where(row == 0, 0.0, dkk)
            dvv = jnp.where(row == 0, 0.0, dvv)

            @pl.when(n >= 1)
            def _():
                dkv_ref[:, :128] = (ck_ref[...] + dkk[:128]).astype(BF16)
                dkv_ref[:, 128:] = (cv_ref[...] + dvv[:128]).astype(BF16)

            ck_ref[...] = dkk[128:]
            cv_ref[...] = dvv[128:]

        @pl.when(n == nb)
        def _():
            dkv_ref[:, :128] = ck_ref[...].astype(BF16)
            dkv_ref[:, 128:] = cv_ref[...].astype(BF16)

    last = nb - 1
    blk = lambda n: (jnp.minimum(n, last), 0)
    return pl.pallas_call(
        _carried(body, 7, 3, ex, nb + 1), grid=(nb + 1,),
        in_specs=[pl.BlockSpec((BLOCK, NQ), blk),
                  pl.BlockSpec((BLOCK, KVW), lambda n: (jnp.minimum(n, last), NQ // KVW)),
                  pl.BlockSpec((BLOCK, KVW), lambda n: (jnp.clip(n - 1, 0, last), NQ // KVW)),
                  pl.BlockSpec((BLOCK, NQ), blk), pl.BlockSpec((BLOCK, NQ), blk),
                  pl.BlockSpec((None, 4, 512, 128), lambda n: (jnp.minimum(n, last), 0, 0, 0)),
                  pl.BlockSpec((None, 4, 512, 256), lambda n: (jnp.minimum(n, 1), 0, 0, 0))] + _ex(ex, "in_specs"),
        out_specs=[pl.BlockSpec((BLOCK, NQ), blk),
                   pl.BlockSpec((BLOCK, KVW), lambda n: (jnp.maximum(n - 1, 0), 0)),
                   pl.BlockSpec((4, 512, 1), lambda n: (0, 0, 0))] + _ex(ex, "out_specs"),
        out_shape=[jax.ShapeDtypeStruct((s, NQ), BF16), jax.ShapeDtypeStruct((s, KVW), BF16),
                   jax.ShapeDtypeStruct((4, 512, 1), F32)] + _ex(ex, "out_shape"),
        scratch_shapes=[pltpu.VMEM((BLOCK, 128), F32), pltpu.VMEM((BLOCK, 128), F32)] + _ex(ex, "scratch"),
        compiler_params=_params("arbitrary"), name="attn_bwd")(qkv, qkv, qkv, do, o, lse, bias, *ex_args)


LANE = 128
FFN_HALF = F // 2
FFN_PAIR = 2 * LANE
FFN_PAIRS = F // FFN_PAIR
FFN_GROUPS = ((0, 2), (2, 4), (4, 6), (6, 8), (8, 10), (10, 11))
FFN_HALO = 8


def _resident(shape):
    return pl.BlockSpec(shape, lambda i: (0,) * len(shape), pipeline_mode=pl.Buffered(1))


def _ffn_fwd(tag, x_in, g, w_up_t, w_dw, b_dw, w_down, *, tm, pre=None, head=None, ex=None, ex_args=()):
    s = x_in.shape[0]
    tm = min(tm, s)
    n_pre = 3 if pre is not None else 0
    n_head = 2 if head is not None else 0

    def body(*refs):
        x_ref, g_ref, wup_ref, wdw_ref, bdw_ref, wd_ref = refs[:6]
        pre_refs = refs[6:6 + n_pre]
        head_refs = refs[6 + n_pre:6 + n_pre + n_head]
        p = 6 + n_pre + n_head
        o_ref, h_ref, up_ref, act_ref, gate_ref = refs[p:p + 5]
        p += 5
        if pre is not None:
            xmid_ref, p = refs[p], p + 1
        if head is not None:
            loss_ref, dgf_ref, p = refs[p], refs[p + 1], p + 2
        carry_ref, ext_ref = refs[p], refs[p + 1]

        @pl.when(pl.program_id(0) == 0)
        def _():
            carry_ref[...] = jnp.zeros_like(carry_ref)
            if head is not None:
                loss_ref[...] = jnp.zeros_like(loss_ref)
                dgf_ref[...] = jnp.zeros_like(dgf_ref)

        xf = x_ref[...]
        if pre is not None:
            oin_ref, wo_ref, bo_ref = pre_refs
            xf = xf + lax.dot_general(oin_ref[...], wo_ref[...], NN_DIMS, preferred_element_type=F32) + bo_ref[...]
            xmid_ref[...] = xf
        r = lax.rsqrt(jnp.mean(xf * xf, axis=-1, keepdims=True) + RMS_EPS)
        h = ((xf * r) * g_ref[...]).astype(BF16)
        h_ref[...] = h

        def up_mm(p):
            for off in (p * FFN_PAIR, F + p * FFN_PAIR):
                rs = slice(off, off + FFN_PAIR)
                up_ref[:, rs] = lax.dot_general(h, wup_ref[rs, :], NT_DIMS, preferred_element_type=F32).astype(BF16)

        def elementwise(p):
            for c in (2 * p, 2 * p + 1):
                cs = slice(c * LANE, (c + 1) * LANE)
                vs = slice(F + c * LANE, F + (c + 1) * LANE)
                gcol = up_ref[:, cs].astype(F32)
                ext_ref[pl.ds(0, FFN_HALO), :] = carry_ref[:, cs]
                ext_ref[pl.ds(FFN_HALO, tm), :] = gcol
                gate_b = (wdw_ref[2:3, cs] * gcol + wdw_ref[1:2, cs] * ext_ref[pl.ds(FFN_HALO - 1, tm), :]
                          + wdw_ref[0:1, cs] * ext_ref[pl.ds(FFN_HALO - 2, tm), :] + bdw_ref[:, cs]).astype(BF16)
                gate_ref[:, cs] = gate_b
                gate = gate_b.astype(F32)
                act_ref[:, cs] = (gate * _sigmoid(gate) * up_ref[:, vs].astype(F32)).astype(BF16)
                carry_ref[:, cs] = gcol[tm - FFN_HALO:]

        def down_mm(lo, hi):
            ks = slice(lo * FFN_PAIR, hi * FFN_PAIR)
            return lax.dot_general(act_ref[:, ks], wd_ref[ks, :], NN_DIMS, preferred_element_type=F32)

        acc = xf
        up_mm(0)
        pending = None
        for lo, hi in FFN_GROUPS:
            for p in range(lo, hi):
                if p + 1 < FFN_PAIRS:
                    up_mm(p + 1)
                if pending is not None:
                    acc = acc + down_mm(*pending)
                    pending = None
                elementwise(p)
            pending = (lo, hi)
        x_out = acc + down_mm(*pending)
        if head is None:
            o_ref[...] = x_out
        else:
            gf_ref, tgt_ref = head_refs
            gf = gf_ref[...]
            rf = lax.rsqrt(jnp.mean(x_out * x_out, axis=-1, keepdims=True) + RMS_EPS)
            diff = (x_out * rf) * gf - tgt_ref[...]
            loss_ref[...] += 0.5 * jnp.sum(jnp.mean(diff * diff, axis=-1, keepdims=True))
            dx, dg_rows = _rms_bwd(diff * (1.0 / D), x_out, gf)
            o_ref[...] = dx
            dgf_ref[...] += jnp.sum(dg_rows, axis=0, keepdims=True)

    row = lambda i: (i, 0)
    const = lambda i: (0, 0)
    in_specs = [pl.BlockSpec((tm, D), row), _resident((1, D)), _resident((2 * F, D)), _resident((3, F)),
                _resident((1, F)), _resident((F, D))]
    out_specs = [pl.BlockSpec((tm, D), row), pl.BlockSpec((tm, D), row), pl.BlockSpec((tm, 2 * F), row),
                 pl.BlockSpec((tm, F), row), pl.BlockSpec((tm, F), row)]
    out_shape = [jax.ShapeDtypeStruct((s, D), F32), jax.ShapeDtypeStruct((s, D), BF16),
                 jax.ShapeDtypeStruct((s, 2 * F), BF16), jax.ShapeDtypeStruct((s, F), BF16),
                 jax.ShapeDtypeStruct((s, F), BF16)]
    args = [x_in, g, w_up_t, w_dw, b_dw, w_down]
    if pre is not None:
        in_specs += [pl.BlockSpec((tm, NQ), row), _resident((NQ, D)), _resident((1, D))]
        out_specs.append(pl.BlockSpec((tm, D), row))
        out_shape.append(jax.ShapeDtypeStruct((s, D), F32))
        args += list(pre)
    if head is not None:
        in_specs += [_resident((1, D)), pl.BlockSpec((tm, D), row)]
        out_specs += [pl.BlockSpec((1, 128), const), pl.BlockSpec((1, D), const)]
        out_shape += [jax.ShapeDtypeStruct((1, 128), F32), jax.ShapeDtypeStruct((1, D), F32)]
        args += list(head)
    return pl.pallas_call(
        _carried(body, len(args), len(out_shape), ex, s // tm), grid=(s // tm,),
        in_specs=in_specs + _ex(ex, "in_specs"), out_specs=out_specs + _ex(ex, "out_specs"),
        out_shape=out_shape + _ex(ex, "out_shape"),
        scratch_shapes=[pltpu.VMEM((FFN_HALO, F), F32), pltpu.VMEM((tm + FFN_HALO, LANE), F32)] + _ex(ex, "scratch"),
        compiler_params=_params("arbitrary"), name=f"ffn{tag}_fwd")(*args, *ex_args)


def _ffn_bwd(tag, dx_out, x_in, g, w_up_t, w_dw, w_down, up, gate, *, tm):
    s = x_in.shape[0]
    tm = min(tm, s)
    n_i = s // tm

    def body(dx_ref, x_ref, g_ref, up_ref, gate_ref, wup_ref, wdw_ref, wd_ref,
             dxin_ref, dup_ref, dg_ref, dwb_ref, carry_ref, da0_ref, da1_ref, extd_ref):
        i = pl.program_id(0)

        @pl.when(i == 0)
        def _():
            carry_ref[...] = jnp.zeros_like(carry_ref)
            dg_ref[...] = jnp.zeros_like(dg_ref)
            dwb_ref[...] = jnp.zeros_like(dwb_ref)

        dxf = dx_ref[...]
        dxb = dxf.astype(BF16)
        da_refs = (da0_ref, da1_ref)

        def da_mm(p):
            ks = slice(p * FFN_PAIR, (p + 1) * FFN_PAIR)
            da_refs[p % 2][...] = lax.dot_general(dxb, wd_ref[ks, :], NT_DIMS, preferred_element_type=F32)

        def elementwise(p):
            da_ref = da_refs[p % 2]
            for c in range(2):
                cc = 2 * p + c
                cs = slice(cc * LANE, (cc + 1) * LANE)
                vs = slice(F + cc * LANE, F + (cc + 1) * LANE)
                gate = gate_ref[:, cs].astype(F32)
                sg = _sigmoid(gate)
                silu = gate * sg
                dac = da_ref[:, c * LANE:(c + 1) * LANE]
                dup_ref[:, vs] = (dac * silu).astype(BF16)
                dgate = dac * up_ref[:, vs].astype(F32) * (sg + silu * (1.0 - sg))
                extd_ref[pl.ds(0, tm), :] = dgate
                extd_ref[pl.ds(tm, FFN_HALO), :] = carry_ref[:, cs]
                d1 = extd_ref[pl.ds(1, tm), :]
                d2 = extd_ref[pl.ds(2, tm), :]
                dup_ref[:, cs] = (wdw_ref[2:3, cs] * dgate + wdw_ref[1:2, cs] * d1 + wdw_ref[0:1, cs] * d2).astype(BF16)
                carry_ref[:, cs] = dgate[:FFN_HALO]
                gcol = up_ref[:, cs].astype(F32)
                dwb_ref[2, :, cs] += _sum8(dgate * gcol)
                dwb_ref[1, :, cs] += _sum8(d1 * gcol)
                dwb_ref[0, :, cs] += _sum8(d2 * gcol)
                dwb_ref[3, :, cs] += _sum8(dgate)

        def dh_mm(lo, hi):
            ks = slice(lo * FFN_PAIR, hi * FFN_PAIR)
            vks = slice(F + lo * FFN_PAIR, F + hi * FFN_PAIR)
            return (lax.dot_general(dup_ref[:, ks], wup_ref[ks, :], NN_DIMS, preferred_element_type=F32)
                    + lax.dot_general(dup_ref[:, vks], wup_ref[vks, :], NN_DIMS, preferred_element_type=F32))

        acc = None
        da_mm(0)
        pending = None
        for lo, hi in FFN_GROUPS:
            for p in range(lo, hi):
                if p + 1 < FFN_PAIRS:
                    da_mm(p + 1)
                if pending is not None:
                    part = dh_mm(*pending)
                    acc = part if acc is None else acc + part
                    pending = None
                elementwise(p)
            pending = (lo, hi)
        acc = acc + dh_mm(*pending)
        dx, dg_rows = _rms_bwd(acc, x_ref[...], g_ref[...])
        dxin_ref[...] = dxf + dx
        dg_ref[...] += jnp.sum(dg_rows, axis=0, keepdims=True)

    rev = lambda i: (n_i - 1 - i, 0)
    return pl.pallas_call(
        body, grid=(n_i,),
        in_specs=[pl.BlockSpec((tm, D), rev), pl.BlockSpec((tm, D), rev), _resident((1, D)),
                  pl.BlockSpec((tm, 2 * F), rev), pl.BlockSpec((tm, F), rev),
                  _resident((2 * F, D)), _resident((3, F)), _resident((F, D))],
        out_specs=[pl.BlockSpec((tm, D), rev), pl.BlockSpec((tm, 2 * F), rev),
                   pl.BlockSpec((1, D), lambda i: (0, 0)), pl.BlockSpec((4, 8, F), lambda i: (0, 0, 0))],
        out_shape=[jax.ShapeDtypeStruct((s, D), F32), jax.ShapeDtypeStruct((s, 2 * F), BF16),
                   jax.ShapeDtypeStruct((1, D), F32), jax.ShapeDtypeStruct((4, 8, F), F32)],
        scratch_shapes=[pltpu.VMEM((FFN_HALO, F), F32), pltpu.VMEM((tm, FFN_PAIR), F32), pltpu.VMEM((tm, FFN_PAIR), F32),
                        pltpu.VMEM((tm + FFN_HALO, LANE), F32)],
        compiler_params=_params("arbitrary"), name=f"ffn{tag}_bwd")(dx_out, x_in, g, up, gate, w_up_t, w_dw, w_down)


CONV_HALO = 32
CONV_CHUNK = 256
CONV_ROWS = 64


def _ln(u2, g, b):
    mu = jnp.mean(u2, axis=-1, keepdims=True)
    xc = u2 - mu
    rstd = lax.rsqrt(jnp.mean(xc * xc, axis=-1, keepdims=True) + LN_EPS)
    xhat = xc * rstd
    return xhat, rstd, xhat * g + b


def _phase_taps(ph):
    first = CONV_HALO - (CONV_K - 1)
    return [(k, first + k - ph) for k in range(CONV_K) if (first + k) % 8 == ph]


def _sum8(v):
    out = v[0:8]
    for j in range(8, v.shape[0], 8):
        out = out + v[j:j + 8]
    return out


def _store_phases(src_ref, sh_ref, tt, cols=slice(None)):
    sh_ref[0] = src_ref[:, cols]
    for ph in range(1, 8):
        sh_ref[ph, pl.ds(0, tt + CONV_HALO - 8), :] = src_ref[pl.ds(ph, tt + CONV_HALO - 8), cols]


def _conv_taps_fwd(w_ref, sh_ref, u2_ref, cs, tt):
    for base in range(0, tt, CONV_ROWS):
        acc = jnp.zeros((CONV_ROWS, CONV_CHUNK), F32) + w_ref[CONV_K:CONV_K + 1, cs]
        for ph in range(8):
            for k, off in _phase_taps(ph):
                acc = acc + w_ref[k:k + 1, cs] * sh_ref[ph, pl.ds(base + off, CONV_ROWS), :]
        u2_ref[pl.ds(base, CONV_ROWS), cs] = acc.astype(BF16)


def _conv_fwd(x_in, g, w1t, b1, wdw, ln_g, ln_b, w2, b2, *, tt):
    s = x_in.shape[0]
    tt = min(tt, s)
    n_c = D // CONV_CHUNK

    def body(x_ref, g_ref, w1_ref, b1_ref, w_ref, lg_ref, lb_ref, w2_ref, b2_ref,
             o_ref, h_ref, a_ref, u2_ref, s_ref, carry_ref, ext_ref, sh_ref):
        @pl.when(pl.program_id(0) == 0)
        def _():
            carry_ref[...] = jnp.zeros_like(carry_ref)

        xf = x_ref[...]
        r = lax.rsqrt(jnp.mean(xf * xf, axis=-1, keepdims=True) + RMS_EPS)
        h = ((xf * r) * g_ref[...]).astype(BF16)
        h_ref[...] = h

        def pw1_mm(c):
            for off in (c * CONV_CHUNK, D + c * CONV_CHUNK):
                rs = slice(off, off + CONV_CHUNK)
                a_ref[:, rs] = (lax.dot_general(h, w1_ref[rs, :], NT_DIMS, preferred_element_type=F32)
                                + b1_ref[:, rs]).astype(BF16)

        pw1_mm(0)
        for c in range(n_c):
            if c + 1 < n_c:
                pw1_mm(c + 1)
            cs = slice(c * CONV_CHUNK, (c + 1) * CONV_CHUNK)
            gs = slice(D + c * CONV_CHUNK, D + (c + 1) * CONV_CHUNK)
            u = a_ref[:, cs].astype(F32) * _sigmoid(a_ref[:, gs].astype(F32))
            ext_ref[pl.ds(0, CONV_HALO), :] = carry_ref[:, cs]
            ext_ref[pl.ds(CONV_HALO, tt), :] = u
            carry_ref[:, cs] = u[tt - CONV_HALO:]
            _store_phases(ext_ref, sh_ref, tt)
            _conv_taps_fwd(w_ref, sh_ref, u2_ref, cs, tt)
        _, _, ln = _ln(u2_ref[...].astype(F32), lg_ref[...], lb_ref[...])
        sv = (ln * _sigmoid(ln)).astype(BF16)
        s_ref[...] = sv
        o_ref[...] = xf + lax.dot_general(sv, w2_ref[...], NN_DIMS, preferred_element_type=F32) + b2_ref[...]

    row = lambda i: (i, 0)
    return pl.pallas_call(
        body, grid=(s // tt,),
        in_specs=[pl.BlockSpec((tt, D), row), _resident((1, D)), _resident((2 * D, D)), _resident((1, 2 * D)),
                  _resident((32, D)), _resident((1, D)), _resident((1, D)), _resident((D, D)), _resident((1, D))],
        out_specs=[pl.BlockSpec((tt, D), row), pl.BlockSpec((tt, D), row), pl.BlockSpec((tt, 2 * D), row),
                   pl.BlockSpec((tt, D), row), pl.BlockSpec((tt, D), row)],
        out_shape=[jax.ShapeDtypeStruct((s, D), F32), jax.ShapeDtypeStruct((s, D), BF16),
                   jax.ShapeDtypeStruct((s, 2 * D), BF16), jax.ShapeDtypeStruct((s, D), BF16),
                   jax.ShapeDtypeStruct((s, D), BF16)],
        scratch_shapes=[pltpu.VMEM((CONV_HALO, D), F32), pltpu.VMEM((tt + CONV_HALO, CONV_CHUNK), F32),
                        pltpu.VMEM((8, tt + CONV_HALO, CONV_CHUNK), F32)],
        compiler_params=_params("arbitrary"), name="conv_fwd")(x_in, g, w1t, b1, wdw, ln_g, ln_b, w2, b2)


def _conv_bwd(dx_out, x_in, g, a, u2, w1t, wdw, ln_g, ln_b, w2, *, tt, ex=None, ex_args=()):
    s = x_in.shape[0]
    tt = min(tt, s)
    hb = tt // CONV_HALO
    n_i = s // tt
    te = tt + CONV_HALO
    n_c = D // CONV_CHUNK

    def body(dx_ref, x_ref, g_ref, a_ref, prev_ref, u2_ref, w1_ref, w_ref, lg_ref, lb_ref, w2_ref,
             dxin_ref, da_ref, dg_ref, acc_ref, db1_ref, db2_ref,
             carry_ref, e2_ref, ext_ref, ush_ref, dsh_ref, du_ref):
        i = pl.program_id(0)

        @pl.when(i == 0)
        def _():
            for ref in (dg_ref, acc_ref, db1_ref, db2_ref, carry_ref):
                ref[...] = jnp.zeros_like(ref)

        dxf = dx_ref[...]
        db2_ref[...] += _sum8(dxf)
        dse = lax.dot_general(dxf.astype(BF16), w2_ref[...], NT_DIMS, preferred_element_type=F32)
        gv = lg_ref[...]
        xhat, rstd, ln = _ln(u2_ref[...].astype(F32), gv, lb_ref[...])
        sg = _sigmoid(ln)
        dln = dse * (sg * (1.0 + ln * (1.0 - sg)))
        acc_ref[32] += _sum8(dln * xhat)
        acc_ref[33] += _sum8(dln)
        dxh = dln * gv
        du2 = rstd * (dxh - jnp.mean(dxh, axis=-1, keepdims=True) - xhat * jnp.mean(dxh * xhat, axis=-1, keepdims=True))
        e2_ref[pl.ds(0, tt), :] = du2
        e2_ref[pl.ds(tt, CONV_HALO), :] = carry_ref[...]
        carry_ref[...] = du2[:CONV_HALO]
        first_tile = i == n_i - 1

        def dh_mm(c):
            cs = slice(c * CONV_CHUNK, (c + 1) * CONV_CHUNK)
            gs = slice(D + c * CONV_CHUNK, D + (c + 1) * CONV_CHUNK)
            return (lax.dot_general(da_ref[:, cs], w1_ref[cs, :], NN_DIMS, preferred_element_type=F32)
                    + lax.dot_general(da_ref[:, gs], w1_ref[gs, :], NN_DIMS, preferred_element_type=F32))

        dh = None
        for c in range(n_c):
            if c >= 1:
                part = dh_mm(c - 1)
                dh = part if dh is None else dh + part
            cs = slice(c * CONV_CHUNK, (c + 1) * CONV_CHUNK)
            gs = slice(D + c * CONV_CHUNK, D + (c + 1) * CONV_CHUNK)
            uh = prev_ref[:, cs].astype(F32) * _sigmoid(prev_ref[:, gs].astype(F32))
            ext_ref[pl.ds(0, CONV_HALO), :] = jnp.where(first_tile, 0.0, uh)
            a1 = a_ref[:, cs].astype(F32)
            sg2 = _sigmoid(a_ref[:, gs].astype(F32))
            ext_ref[pl.ds(CONV_HALO, tt), :] = a1 * sg2
            _store_phases(ext_ref, ush_ref, tt)
            _store_phases(e2_ref, dsh_ref, tt, cs)
            for base in range(0, tt, CONV_ROWS):
                d0 = e2_ref[pl.ds(base, CONV_ROWS), cs]
                du = jnp.zeros((CONV_ROWS, CONV_CHUNK), F32)
                for ph in range(8):
                    for k, off in _phase_taps(ph):
                        acc_ref[k, :, cs] += _sum8(d0 * ush_ref[ph, pl.ds(base + off, CONV_ROWS), :])
                    for k in range(CONV_K):
                        if (CONV_K - 1 - k) % 8 == ph:
                            du = du + w_ref[k:k + 1, cs] * dsh_ref[ph, pl.ds(base + CONV_K - 1 - k - ph, CONV_ROWS), :]
                acc_ref[CONV_K, :, cs] += _sum8(d0)
                du_ref[pl.ds(base, CONV_ROWS), :] = du
            du = du_ref[...]
            da1 = du * sg2
            da2 = du * a1 * (sg2 * (1.0 - sg2))
            da_ref[:, cs] = da1.astype(BF16)
            da_ref[:, gs] = da2.astype(BF16)
            db1_ref[:, cs] += _sum8(da1)
            db1_ref[:, gs] += _sum8(da2)
        dh = dh + dh_mm(n_c - 1)
        dx, dg_rows = _rms_bwd(dh, x_ref[...], g_ref[...])
        dxin_ref[...] = dxf + dx
        dg_ref[...] += jnp.sum(dg_rows, axis=0, keepdims=True)

    rev = lambda i: (n_i - 1 - i, 0)
    return pl.pallas_call(
        _carried(body, 11, 6, ex, n_i), grid=(n_i,),
        in_specs=[pl.BlockSpec((tt, D), rev), pl.BlockSpec((tt, D), rev), _resident((1, D)),
                  pl.BlockSpec((tt, 2 * D), rev),
                  pl.BlockSpec((CONV_HALO, 2 * D), lambda i: (jnp.maximum((n_i - 1 - i) * hb - 1, 0), 0)),
                  pl.BlockSpec((tt, D), rev), _resident((2 * D, D)), _resident((32, D)), _resident((1, D)),
                  _resident((1, D)), _resident((D, D))] + _ex(ex, "in_specs"),
        out_specs=[pl.BlockSpec((tt, D), rev), pl.BlockSpec((tt, 2 * D), rev), pl.BlockSpec((1, D), lambda i: (0, 0)),
                   pl.BlockSpec((40, 8, D), lambda i: (0, 0, 0)), pl.BlockSpec((8, 2 * D), lambda i: (0, 0)),
                   pl.BlockSpec((8, D), lambda i: (0, 0))] + _ex(ex, "out_specs"),
        out_shape=[jax.ShapeDtypeStruct((s, D), F32), jax.ShapeDtypeStruct((s, 2 * D), BF16),
                   jax.ShapeDtypeStruct((1, D), F32), jax.ShapeDtypeStruct((40, 8, D), F32),
                   jax.ShapeDtypeStruct((8, 2 * D), F32), jax.ShapeDtypeStruct((8, D), F32)] + _ex(ex, "out_shape"),
        scratch_shapes=[pltpu.VMEM((CONV_HALO, D), F32), pltpu.VMEM((te, D), F32), pltpu.VMEM((te, CONV_CHUNK), F32),
                        pltpu.VMEM((8, te, CONV_CHUNK), F32), pltpu.VMEM((8, te, CONV_CHUNK), F32),
                        pltpu.VMEM((tt, CONV_CHUNK), F32)] + _ex(ex, "scratch"),
        compiler_params=_params("arbitrary"), name="conv_bwd")(dx_out, x_in, g, a, a, u2, w1t, wdw, ln_g, ln_b, w2, *ex_args)


def _adamw(name, w, g, m, v):
    rows, cols = w.shape
    tr = _row_tile(rows, 256)

    def body(w_ref, g_ref, m_ref, v_ref, d_ref, nm_ref, nv_ref):
        gv = g_ref[...]
        m2 = ADAM_B1 * m_ref[...] + (1.0 - ADAM_B1) * gv
        v2 = ADAM_B2 * v_ref[...] + (1.0 - ADAM_B2) * (gv * gv)
        m_hat = m2 / (1.0 - ADAM_B1 ** ADAM_STEP)
        v_hat = v2 / (1.0 - ADAM_B2 ** ADAM_STEP)
        d_ref[...] = -ADAM_LR * (m_hat / (jnp.sqrt(v_hat) + ADAM_EPS) + ADAM_WD * w_ref[...])
        nm_ref[...] = m2
        nv_ref[...] = v2

    spec = pl.BlockSpec((tr, cols), lambda i: (i, 0))
    return pl.pallas_call(
        body, grid=(rows // tr,), in_specs=[spec] * 4, out_specs=[spec] * 3,
        out_shape=[jax.ShapeDtypeStruct((rows, cols), F32)] * 3,
        compiler_params=_params("parallel"), name=name)(w, g, m, v)


def _sum_slabs(name, buf):
    rows = buf.shape[1]
    tr = _row_tile(rows, 512, 16)

    def body(b_ref, o_ref):
        acc = b_ref[0].astype(F32)
        for k in range(1, N_DEV):
            acc = acc + b_ref[k].astype(F32)
        o_ref[...] = acc

    return pl.pallas_call(
        body, grid=(rows // tr,), in_specs=[pl.BlockSpec((N_DEV, tr, D), lambda i: (0, i, 0))],
        out_specs=pl.BlockSpec((tr, D), lambda i: (i, 0)), out_shape=jax.ShapeDtypeStruct((rows, D), F32),
        compiler_params=_params("parallel"), name=name)(buf)


def _ex(ex, field):
    return list(getattr(ex, field)) if ex is not None else []


def _me():
    x, y, c = lax.axis_index("x"), lax.axis_index("y"), lax.axis_index("c")
    return x, y, c, 4 * x + 2 * y + c


def _peer(x, y, c, k):
    return (x ^ (k >> 2), y ^ ((k >> 1) & 1), c ^ (k & 1))


class _Gather:
    def __init__(self, mats, slab_base=0):
        self.mats = tuple(mats)
        self.slab_base = slab_base
        self.rows = sum(MAT_ROWS[i] for i in self.mats)
        any_spec = pl.BlockSpec(memory_space=pl.ANY)
        self.n_in, self.n_out = 1, len(self.mats)
        self.in_specs = [any_spec]
        self.out_specs = [any_spec] * self.n_out
        self.out_shape = [jax.ShapeDtypeStruct((N_DEV * MAT_ROWS[i], D), BF16) for i in self.mats]
        self.scratch = [pltpu.SemaphoreType.DMA((N_DEV - 1,)), pltpu.SemaphoreType.DMA((N_DEV - 1,)),
                        pltpu.SemaphoreType.DMA]
        assert max(s.shape[0] for s in self.out_shape) >= self.rows

    def start(self, ins, outs, scr):
        (slab_ref,), (send_sems, recv_sems, local_sem) = ins, scr
        x, y, c, me = _me()
        for n, i in enumerate(self.mats):
            src = slab_ref.at[pl.ds(MAT_OFF[i] - self.slab_base, MAT_ROWS[i])]
            dst = outs[n].at[pl.ds(me * MAT_ROWS[i], MAT_ROWS[i])]
            pltpu.make_async_copy(src, dst, local_sem).start()
            for k in range(1, N_DEV):
                pltpu.make_async_remote_copy(src_ref=src, dst_ref=dst, send_sem=send_sems.at[k - 1],
                                             recv_sem=recv_sems.at[k - 1], device_id=_peer(x, y, c, k),
                                             device_id_type=MESH).start()

    def wait(self, ins, outs, scr):
        send_sems, recv_sems, local_sem = scr
        x, y, c, _ = _me()
        big = max(range(self.n_out), key=lambda n: self.out_shape[n].shape[0])
        whole = outs[big].at[pl.ds(0, self.rows)]
        for k in range(1, N_DEV):
            pltpu.make_async_remote_copy(src_ref=whole, dst_ref=whole, send_sem=send_sems.at[k - 1],
                                         recv_sem=recv_sems.at[k - 1], device_id=_peer(x, y, c, k),
                                         device_id_type=MESH).wait()
        pltpu.make_async_copy(whole, whole, local_sem).wait()


class _Scatter:
    def __init__(self, mats):
        self.mats = tuple(mats)
        self.rows = sum(MAT_ROWS[i] for i in self.mats)
        any_spec = pl.BlockSpec(memory_space=pl.ANY)
        self.n_in, self.n_out = len(self.mats), 1
        self.in_specs = [any_spec] * self.n_in
        self.out_specs = [any_spec]
        self.out_shape = [jax.ShapeDtypeStruct((N_DEV, self.rows, D), BF16)]
        self.scratch = [pltpu.SemaphoreType.DMA((N_DEV - 1,)), pltpu.SemaphoreType.DMA((N_DEV - 1,)),
                        pltpu.SemaphoreType.DMA]

    def offsets(self):
        return [sum(MAT_ROWS[i] for i in self.mats[:n]) for n in range(len(self.mats))]

    def start(self, ins, outs, scr):
        (buf_ref,), (send_sems, recv_sems, local_sem) = outs, scr
        x, y, c, me = _me()
        for n, (i, off) in enumerate(zip(self.mats, self.offsets())):
            r = MAT_ROWS[i]
            pltpu.make_async_copy(ins[n].at[pl.ds(me * r, r)], buf_ref.at[0, pl.ds(off, r)], local_sem).start()
            for k in range(1, N_DEV):
                pltpu.make_async_remote_copy(src_ref=ins[n].at[pl.ds((me ^ k) * r, r)], dst_ref=buf_ref.at[k, pl.ds(off, r)],
                                             send_sem=send_sems.at[k - 1], recv_sem=recv_sems.at[k - 1],
                                             device_id=_peer(x, y, c, k), device_id_type=MESH).start()

    def wait(self, ins, outs, scr):
        (buf_ref,), (send_sems, recv_sems, local_sem) = outs, scr
        x, y, c, _ = _me()
        whole = buf_ref.at[0]
        for k in range(1, N_DEV):
            pltpu.make_async_remote_copy(src_ref=whole, dst_ref=whole, send_sem=send_sems.at[k - 1],
                                         recv_sem=recv_sems.at[k - 1], device_id=_peer(x, y, c, k),
                                         device_id_type=MESH).wait()
        pltpu.make_async_copy(whole, whole, local_sem).wait()


def _carried(body, n_in, n_out, ex, n_steps):
    if ex is None:
        return body

    def wrapped(*refs):
        ins, ex_ins = refs[:n_in], refs[n_in:n_in + ex.n_in]
        p = n_in + ex.n_in
        outs, ex_outs = refs[p:p + n_out], refs[p + n_out:p + n_out + ex.n_out]
        p += n_out + ex.n_out
        n_scr = len(refs) - p - len(ex.scratch)
        scr, ex_scr = refs[p:p + n_scr], refs[p + n_scr:]

        @pl.when(pl.program_id(0) == 0)
        def _():
            ex.start(ex_ins, ex_outs, ex_scr)

        body(*ins, *outs, *scr)

        @pl.when(pl.program_id(0) == n_steps - 1)
        def _():
            ex.wait(ex_ins, ex_outs, ex_scr)

    return wrapped


def _exchange_small(name, ex, ex_args, small, reduce):
    vmem_spec = pl.BlockSpec(memory_space=pltpu.VMEM)
    n_small = small.shape[0]

    def body(*refs):
        ex_ins, small_ref = refs[:ex.n_in], refs[ex.n_in]
        p = ex.n_in + 1
        ex_outs, res_ref = refs[p:p + ex.n_out], refs[p + ex.n_out]
        p += ex.n_out + 1
        if reduce:
            all_ref, p = refs[p], p + 1
        else:
            all_ref = res_ref
        sm_send, sm_recv = refs[p], refs[p + 1]
        ex_scr = refs[p + 2:]
        x, y, c, me = _me()
        ex.start(ex_ins, ex_outs, ex_scr)
        copies = [pltpu.make_async_remote_copy(
            src_ref=small_ref, dst_ref=all_ref.at[me], send_sem=sm_send.at[k - 1], recv_sem=sm_recv.at[k - 1],
            device_id=_peer(x, y, c, k), device_id_type=MESH) for k in range(1, N_DEV)]
        for cp in copies:
            cp.start()
        all_ref[me] = small_ref[...]
        for cp in copies:
            cp.wait()
        if reduce:
            acc = all_ref[0]
            for d in range(1, N_DEV):
                acc = acc + all_ref[d]
            res_ref[...] = acc
        ex.wait(ex_ins, ex_outs, ex_scr)

    res_shape = (n_small, D) if reduce else (N_DEV, n_small, D)
    scratch = ([pltpu.VMEM((N_DEV, n_small, D), F32)] if reduce else []) + [
        pltpu.SemaphoreType.DMA((N_DEV - 1,)), pltpu.SemaphoreType.DMA((N_DEV - 1,))] + ex.scratch
    outs = pl.pallas_call(
        body, in_specs=ex.in_specs + [vmem_spec], out_specs=ex.out_specs + [vmem_spec],
        out_shape=ex.out_shape + [jax.ShapeDtypeStruct(res_shape, F32)], scratch_shapes=scratch,
        compiler_params=pltpu.CompilerParams(vmem_limit_bytes=VMEM_LIMIT_BYTES), name=name)(*ex_args, small)
    return outs


def _pack_rows(arrs):
    rows = []
    for a in arrs:
        flat = a.reshape(-1).astype(F32)
        pad = (-flat.shape[0]) % D
        rows.append(jnp.pad(flat, (0, pad)).reshape(-1, D))
    slab = jnp.concatenate(rows, axis=0)
    return jnp.pad(slab, ((0, (-slab.shape[0]) % 8), (0, 0)))


def _unpack_rows(slab, shapes):
    out, r = [], 0
    for shp in shapes:
        size = math.prod(shp)
        nrows = -(-size // D)
        out.append(slab[r:r + nrows].reshape(-1)[:size].reshape(shp))
        r += nrows
    return out


def kernel(x, norm_mix, attn_w_qkv, attn_b_qkv, attn_sinks, attn_w_o, attn_b_o, conv_w_pw1, conv_b_pw1, conv_w_dw, conv_b_dw, conv_ln_g, conv_ln_b, conv_w_pw2, conv_b_pw2, norm_ffn, ffn_w_up, ffn_w_dw, ffn_b_dw, ffn_w_down, final_norm, loss_target, m_norm_mix, m_attn_w_qkv, m_attn_b_qkv, m_attn_sinks, m_attn_w_o, m_attn_b_o, m_conv_w_pw1, m_conv_b_pw1, m_conv_w_dw, m_conv_b_dw, m_conv_ln_g, m_conv_ln_b, m_conv_w_pw2, m_conv_b_pw2, m_norm_ffn, m_ffn_w_up, m_ffn_w_dw, m_ffn_b_dw, m_ffn_w_down, m_final_norm, v_norm_mix, v_attn_w_qkv, v_attn_b_qkv, v_attn_sinks, v_attn_w_o, v_attn_b_o, v_conv_w_pw1, v_conv_b_pw1, v_conv_w_dw, v_conv_b_dw, v_conv_ln_g, v_conv_ln_b, v_conv_w_pw2, v_conv_b_pw2, v_norm_ffn, v_ffn_w_up, v_ffn_w_dw, v_ffn_b_dw, v_ffn_w_down, v_final_norm):
    s = x.shape[1]
    me = 4 * lax.axis_index("x") + 2 * lax.axis_index("y") + lax.axis_index("c")
    x0 = x.reshape(s, D)
    tgt = loss_target.reshape(s, D)

    slab_qkv = attn_w_qkv[0].T.astype(BF16)
    slab = jnp.concatenate([attn_w_o[0], conv_w_pw1[0].T, conv_w_pw2[0],
                            ffn_w_up[0].T, ffn_w_down[0], ffn_w_up[1].T, ffn_w_down[1]], axis=0).astype(BF16)
    sharded_small = [conv_b_pw1, conv_w_dw, conv_b_dw, conv_ln_g, conv_ln_b, conv_b_pw2, ffn_w_dw]
    small_local = _pack_rows(sharded_small)
    w_qkv_t, small_all = _exchange_small("gather_first", _Gather((0,)), (slab_qkv,), small_local, reduce=False)
    parts = [_unpack_rows(small_all[d], [a.shape for a in sharded_small]) for d in range(N_DEV)]
    b_pw1, w_cdw, b_cdw, ln_g, ln_b, b_pw2, w_fdw = [jnp.concatenate([parts[d][i] for d in range(N_DEV)], axis=-1)
                                                      for i in range(len(sharded_small))]
    conv_w32 = jnp.concatenate([w_cdw[0], b_cdw], axis=0)
    score_bias = _score_bias(attn_sinks)

    qkv, h0 = _mm("qkv", x0, w_qkv_t, nt=True, tm=1024, tn=NQ + KVW, out_dtype=BF16, rms_g=norm_mix[0:1], bias=attn_b_qkv)
    o, lse, w_o, w_up0_t, w_dn0 = _attn_fwd(qkv, score_bias, _Gather((1, 4, 5), MAT_OFF[1]), (slab,))
    x2, h1, up0, act0, gate0, x1, w_pw1_t, w_pw2, w_up1_t, w_dn1 = _ffn_fwd(
        0, x0, norm_ffn[0:1], w_up0_t, w_fdw[0], ffn_b_dw[0:1], w_dn0, tm=256, pre=(o, w_o, attn_b_o),
        ex=_Gather((2, 3, 6, 7), MAT_OFF[1]), ex_args=(slab,))
    x3, h2, a, u2, s_act = _conv_fwd(x2, norm_mix[1:2], w_pw1_t, b_pw1, conv_w32, ln_g, ln_b, w_pw2, b_pw2, tt=256)
    dx4, h3, up1, act1, gate1, loss_acc, dg_final = _ffn_fwd(1, x3, norm_ffn[1:2], w_up1_t, w_fdw[1], ffn_b_dw[1:2], w_dn1,
                                                             tm=256, head=(final_norm.reshape(1, D), tgt))
    loss = lax.psum(loss_acc[0, 0], ("x", "y", "c"))

    dx3, d_up1, dg_f1, dwb1 = _ffn_bwd(1, dx4, x3, norm_ffn[1:2], w_up1_t, w_fdw[1], w_dn1, up1, gate1, tm=256)
    dw_dn1 = _mm_tn("ffn1_dwdown", act1, dx4, tn=FFN_HALF, tt=2048)
    dw_up1_t = _mm_tn("ffn1_dwup", d_up1, h3, tn=FFN_HALF, tt=2048)
    dw_pw2 = _mm_tn("dw_pw2", s_act, dx3, tn=D, tt=2048)
    scatter_a = _Scatter((6, 7))
    dx2, da, dg_m1, conv_acc8, db_pw1_8, db_pw2_8, buf_a = _conv_bwd(
        dx3, x2, norm_mix[1:2], a, u2, w_pw1_t, conv_w32, ln_g, ln_b, w_pw2, tt=256, ex=scatter_a, ex_args=(dw_up1_t, dw_dn1))
    conv_acc = jnp.sum(conv_acc8, axis=1)
    db_pw1 = jnp.sum(db_pw1_8, axis=0, keepdims=True)
    db_pw2 = jnp.sum(db_pw2_8, axis=0, keepdims=True)
    dw_pw1_t = _mm_tn("dw_pw1", da, h2, tn=D, tt=2048)
    dx1, d_up0, dg_f0, dwb0 = _ffn_bwd(0, dx2, x1, norm_ffn[0:1], w_up0_t, w_fdw[0], w_dn0, up0, gate0, tm=256)
    dw_dn0 = _mm_tn("ffn0_dwdown", act0, dx2, tn=FFN_HALF, tt=2048)
    dw_up0_t = _mm_tn("ffn0_dwup", d_up0, h1, tn=FFN_HALF, tt=2048)
    do = _mm("d_attn_out", dx1, w_o, nt=True, tm=1024, tn=D, out_dtype=BF16)
    dw_o, db_o = _mm_tn("dw_o", o, dx1, tn=D, tt=2048, colsum_r=True)
    scatter_b = _Scatter((1, 2, 3, 4, 5))
    dq, dkv, dsk, buf_b = _attn_bwd(qkv, do, o, lse, score_bias, scatter_b, (dw_o, dw_pw1_t, dw_pw2, dw_up0_t, dw_dn0))
    dqkv = jnp.concatenate([dq, dkv], axis=1)
    grad_x, dg_m0 = _dgrad_rms("d_qkv", dqkv, w_qkv_t, x0, norm_mix[0:1], dx1, tm=1024, tk=NQ + KVW)
    dw_qkv_t, db_qkv = _mm_tn("dw_qkv", dqkv, h0, tn=NQ + KVW, tt=2048, colsum_l=True)

    dwb0, dwb1 = jnp.sum(dwb0, axis=1), jnp.sum(dwb1, axis=1)
    d_sinks = jnp.transpose(jnp.sum(dsk.reshape(N_KV, 2, 4, BLOCK), axis=-1), (0, 2, 1)).reshape(1, 16)
    small_grads = [jnp.concatenate([dg_m0, dg_m1], axis=0), db_qkv, d_sinks, db_o, jnp.concatenate([dg_f0, dg_f1], axis=0),
                   jnp.stack([dwb0[3], dwb1[3]]), dg_final, db_pw1, conv_acc[0:CONV_K], conv_acc[31:32], conv_acc[32:33],
                   conv_acc[33:34], db_pw2, jnp.stack([dwb0[0:3], dwb1[0:3]])]
    small_shapes = [(2, D), (1, NQ + KVW), (1, 16), (1, D), (2, D), (2, F), (D,), (1, 2 * D), (1, CONV_K, D), (1, D),
                    (1, D), (1, D), (1, D), (2, 3, F)]
    scatter_c = _Scatter((0,))
    buf_c, small_sum = _exchange_small("scatter_last", scatter_c, (dw_qkv_t,), _pack_rows(small_grads), reduce=True)
    gm = [None] * N_MAT
    for tag, sc, buf in (("a", scatter_a, buf_a), ("b", scatter_b, buf_b), ("c", scatter_c, buf_c)):
        gslab = _sum_slabs("sum_slabs_" + tag, buf)
        for i, off in zip(sc.mats, sc.offsets()):
            gm[i] = gslab[off:off + MAT_ROWS[i]]
    (g_norm_mix, g_b_qkv, g_sinks, g_b_o, g_norm_ffn, g_ffn_b_dw, g_final, g_b_pw1_full, g_w_cdw_full, g_b_cdw_full,
     g_ln_g_full, g_ln_b_full, g_b_pw2_full, g_w_fdw_full) = _unpack_rows(small_sum, small_shapes)

    def shard(a, width):
        return lax.dynamic_slice_in_dim(a, me * width, width, axis=a.ndim - 1)

    grads = {
        "norm_mix": g_norm_mix, "attn_w_qkv": gm[0].T[None], "attn_b_qkv": g_b_qkv, "attn_sinks": g_sinks,
        "attn_w_o": gm[1][None], "attn_b_o": g_b_o, "conv_w_pw1": gm[2].T[None], "conv_b_pw1": shard(g_b_pw1_full, 256),
        "conv_w_dw": shard(g_w_cdw_full, 128), "conv_b_dw": shard(g_b_cdw_full, 128), "conv_ln_g": shard(g_ln_g_full, 128),
        "conv_ln_b": shard(g_ln_b_full, 128), "conv_w_pw2": gm[3][None], "conv_b_pw2": shard(g_b_pw2_full, 128),
        "norm_ffn": g_norm_ffn, "ffn_w_up": jnp.stack([gm[4].T, gm[6].T]), "ffn_w_dw": shard(g_w_fdw_full, 352),
        "ffn_b_dw": g_ffn_b_dw, "ffn_w_down": jnp.stack([gm[5], gm[7]]), "final_norm": g_final,
    }
    weights = dict(norm_mix=norm_mix, attn_w_qkv=attn_w_qkv, attn_b_qkv=attn_b_qkv, attn_sinks=attn_sinks, attn_w_o=attn_w_o, attn_b_o=attn_b_o, conv_w_pw1=conv_w_pw1, conv_b_pw1=conv_b_pw1, conv_w_dw=conv_w_dw, conv_b_dw=conv_b_dw, conv_ln_g=conv_ln_g, conv_ln_b=conv_ln_b, conv_w_pw2=conv_w_pw2, conv_b_pw2=conv_b_pw2, norm_ffn=norm_ffn, ffn_w_up=ffn_w_up, ffn_w_dw=ffn_w_dw, ffn_b_dw=ffn_b_dw, ffn_w_down=ffn_w_down, final_norm=final_norm)
    moms = dict(norm_mix=m_norm_mix, attn_w_qkv=m_attn_w_qkv, attn_b_qkv=m_attn_b_qkv, attn_sinks=m_attn_sinks, attn_w_o=m_attn_w_o, attn_b_o=m_attn_b_o, conv_w_pw1=m_conv_w_pw1, conv_b_pw1=m_conv_b_pw1, conv_w_dw=m_conv_w_dw, conv_b_dw=m_conv_b_dw, conv_ln_g=m_conv_ln_g, conv_ln_b=m_conv_ln_b, conv_w_pw2=m_conv_w_pw2, conv_b_pw2=m_conv_b_pw2, norm_ffn=m_norm_ffn, ffn_w_up=m_ffn_w_up, ffn_w_dw=m_ffn_w_dw, ffn_b_dw=m_ffn_b_dw, ffn_w_down=m_ffn_w_down, final_norm=m_final_norm)
    vars_ = dict(norm_mix=v_norm_mix, attn_w_qkv=v_attn_w_qkv, attn_b_qkv=v_attn_b_qkv, attn_sinks=v_attn_sinks, attn_w_o=v_attn_w_o, attn_b_o=v_attn_b_o, conv_w_pw1=v_conv_w_pw1, conv_b_pw1=v_conv_b_pw1, conv_w_dw=v_conv_w_dw, conv_b_dw=v_conv_b_dw, conv_ln_g=v_conv_ln_g, conv_ln_b=v_conv_ln_b, conv_w_pw2=v_conv_w_pw2, conv_b_pw2=v_conv_b_pw2, norm_ffn=v_norm_ffn, ffn_w_up=v_ffn_w_up, ffn_w_dw=v_ffn_w_dw, ffn_b_dw=v_ffn_b_dw, ffn_w_down=v_ffn_w_down, final_norm=v_final_norm)
    names = list(weights)
    big = ("attn_w_qkv", "attn_w_o", "conv_w_pw1", "conv_w_pw2", "ffn_w_up", "ffn_w_down")
    delta, new_m, new_v = {}, {}, {}
    for nm in big:
        shp = weights[nm].shape
        two_d = (shp[0] * shp[1], shp[2])
        res = _adamw("adamw_" + nm, *(t[nm].reshape(two_d) for t in (weights, grads, moms, vars_)))
        delta[nm], new_m[nm], new_v[nm] = (r.reshape(shp) for r in res)
    small_names = [nm for nm in names if nm not in big]
    small_shapes_local = [weights[nm].shape for nm in small_names]
    res = _adamw("adamw_small", *(_pack_rows([t[nm] for nm in small_names]) for t in (weights, grads, moms, vars_)))
    for tgt_dict, slab_out in zip((delta, new_m, new_v), res):
        for nm, val in zip(small_names, _unpack_rows(slab_out, small_shapes_local)):
            tgt_dict[nm] = val
    return (loss, grad_x.reshape(1, s, D), *[grads[nm] for nm in names], *[delta[nm] for nm in names],
            *[new_m[nm] for nm in names], *[new_v[nm] for nm in names])
```

```python
import functools
import math

import jax
import jax.numpy as jnp
from jax import lax
from jax.experimental import pallas as pl
from jax.experimental.pallas import tpu as pltpu

F32 = jnp.float32
BF16 = jnp.bfloat16

D = 1024
F = 2816
HEAD_DIM = 64
N_KV = 2
BLOCK = 128
NQ = 1024
KVW = 256
CONV_K = 31
RMS_EPS = 1e-6
LN_EPS = 1e-5
ADAM_LR, ADAM_B1, ADAM_B2, ADAM_EPS, ADAM_WD, ADAM_STEP = 0.001, 0.9, 0.999, 1e-08, 0.01, 10
N_DEV = 8
MESH = pl.DeviceIdType.MESH
VMEM_LIMIT_BYTES = 56 * 2**20
MASKED = -1e30

MAT_ROWS = (160, 128, 256, 128, 704, 352, 704, 352)
MAT_OFF = tuple(sum(MAT_ROWS[:i]) for i in range(len(MAT_ROWS)))
SLAB_ROWS = sum(MAT_ROWS)
N_MAT = len(MAT_ROWS)

NT_DIMS = (((1,), (1,)), ((), ()))
NN_DIMS = (((1,), (0,)), ((), ()))
TN_DIMS = (((0,), (0,)), ((), ()))


def _params(*sem):
    return pltpu.CompilerParams(dimension_semantics=tuple(sem), vmem_limit_bytes=VMEM_LIMIT_BYTES)


def _row_tile(rows, cap, mult=8):
    best = None
    for t in range(mult, min(rows, cap) + 1, mult):
        if rows % t == 0:
            best = t
    return best if best is not None else rows


def _sigmoid(x):
    return 1.0 / (1.0 + jnp.exp(-x))


def _mm(name, a, b, *, nt, tm, tn, out_dtype, rms_g=None, bias=None, res=None):
    m, k = a.shape
    n = b.shape[0] if nt else b.shape[1]
    tm = min(tm, m)
    has_rms = rms_g is not None
    dims = NT_DIMS if nt else NN_DIMS

    def body(*refs):
        a_ref, b_ref = refs[0], refs[1]
        pos = 2
        g_ref = bias_ref = res_ref = h_ref = hs_ref = None
        if has_rms:
            g_ref = refs[pos]; pos += 1
        if bias is not None:
            bias_ref = refs[pos]; pos += 1
        if res is not None:
            res_ref = refs[pos]; pos += 1
        o_ref = refs[pos]; pos += 1
        if has_rms:
            h_ref, hs_ref = refs[pos], refs[pos + 1]

            @pl.when(pl.program_id(1) == 0)
            def _():
                xf = a_ref[...]
                r = lax.rsqrt(jnp.mean(xf * xf, axis=-1, keepdims=True) + RMS_EPS)
                h = ((xf * r) * g_ref[...]).astype(BF16)
                hs_ref[...] = h
                h_ref[...] = h

            lhs = hs_ref[...]
        else:
            lhs = a_ref[...].astype(BF16)
        acc = lax.dot_general(lhs, b_ref[...], dims, preferred_element_type=F32)
        if bias is not None:
            acc = acc + bias_ref[...]
        if res is not None:
            acc = acc + res_ref[...]
        o_ref[...] = acc.astype(out_dtype)

    in_specs = [pl.BlockSpec((tm, k), lambda i, j: (i, 0)),
                pl.BlockSpec((tn, k), lambda i, j: (j, 0)) if nt else pl.BlockSpec((k, tn), lambda i, j: (0, j))]
    args = [a, b]
    if has_rms:
        in_specs.append(pl.BlockSpec((1, k), lambda i, j: (0, 0))); args.append(rms_g)
    if bias is not None:
        in_specs.append(pl.BlockSpec((1, tn), lambda i, j: (0, j))); args.append(bias)
    if res is not None:
        in_specs.append(pl.BlockSpec((tm, tn), lambda i, j: (i, j))); args.append(res)
    out_shape = [jax.ShapeDtypeStruct((m, n), out_dtype)]
    out_specs = [pl.BlockSpec((tm, tn), lambda i, j: (i, j))]
    scratch = []
    if has_rms:
        out_shape.append(jax.ShapeDtypeStruct((m, k), BF16))
        out_specs.append(pl.BlockSpec((tm, k), lambda i, j: (i, 0)))
        scratch.append(pltpu.VMEM((tm, k), BF16))
    outs = pl.pallas_call(body, grid=(m // tm, n // tn), in_specs=in_specs, out_specs=out_specs, out_shape=out_shape,
                          scratch_shapes=scratch, compiler_params=_params("parallel", "arbitrary"), name=name)(*args)
    return outs if has_rms else outs[0]


def _mm_tn(name, l, r, *, tn, tt, colsum_l=False, colsum_r=False):
    s, nl = l.shape
    nr = r.shape[1]
    tt = min(tt, s)
    n_t = s // tt

    def body(*refs):
        l_ref, r_ref, o_ref = refs[0], refs[1], refs[2]
        pos = 3
        cl_ref = cr_ref = None
        if colsum_l:
            cl_ref = refs[pos]; pos += 1
        if colsum_r:
            cr_ref = refs[pos]; pos += 1
        acc_ref = refs[pos]
        j, t = pl.program_id(0), pl.program_id(1)

        @pl.when(t == 0)
        def _():
            acc_ref[...] = jnp.zeros_like(acc_ref)

        lv = l_ref[...]
        rv = r_ref[...]
        acc_ref[...] += lax.dot_general(lv, rv.astype(BF16), TN_DIMS, preferred_element_type=F32)
        if colsum_l:
            @pl.when(t == 0)
            def _():
                cl_ref[...] = jnp.zeros_like(cl_ref)

            cl_ref[...] += jnp.sum(lv.astype(F32), axis=0, keepdims=True)
        if colsum_r:
            @pl.when((t == 0) & (j == 0))
            def _():
                cr_ref[...] = jnp.zeros_like(cr_ref)

            @pl.when(j == 0)
            def _():
                cr_ref[...] += jnp.sum(rv.astype(F32), axis=0, keepdims=True)

        @pl.when(t == n_t - 1)
        def _():
            o_ref[...] = acc_ref[...].astype(BF16)

    out_shape = [jax.ShapeDtypeStruct((nl, nr), BF16)]
    out_specs = [pl.BlockSpec((tn, nr), lambda j, t: (j, 0))]
    if colsum_l:
        out_shape.append(jax.ShapeDtypeStruct((1, nl), F32))
        out_specs.append(pl.BlockSpec((1, tn), lambda j, t: (0, j)))
    if colsum_r:
        out_shape.append(jax.ShapeDtypeStruct((1, nr), F32))
        out_specs.append(pl.BlockSpec((1, nr), lambda j, t: (0, 0)))
    outs = pl.pallas_call(
        body, grid=(nl // tn, n_t),
        in_specs=[pl.BlockSpec((tt, tn), lambda j, t: (t, j)), pl.BlockSpec((tt, nr), lambda j, t: (t, 0))],
        out_specs=out_specs, out_shape=out_shape, scratch_shapes=[pltpu.VMEM((tn, nr), F32)],
        compiler_params=_params("arbitrary", "arbitrary"), name=name)(l, r)
    return outs if (colsum_l or colsum_r) else outs[0]


def _rms_bwd(dh, xf, g):
    r = lax.rsqrt(jnp.mean(xf * xf, axis=-1, keepdims=True) + RMS_EPS)
    gd = dh * g
    dx = r * gd - xf * ((r * r * r) * jnp.mean(xf * gd, axis=-1, keepdims=True))
    return dx, dh * (xf * r)


def _dgrad_rms(name, dy, w, x, g, dres, *, tm, tk):
    m, k = dy.shape
    tm = min(tm, m)
    n_k = k // tk

    def body(dy_ref, w_ref, x_ref, g_ref, dres_ref, o_ref, dg_ref, acc_ref):
        i, kk = pl.program_id(0), pl.program_id(1)

        @pl.when(kk == 0)
        def _():
            acc_ref[...] = jnp.zeros_like(acc_ref)

        acc_ref[...] += lax.dot_general(dy_ref[...], w_ref[...], NN_DIMS, preferred_element_type=F32)

        @pl.when((i == 0) & (kk == n_k - 1))
        def _():
            dg_ref[...] = jnp.zeros_like(dg_ref)

        @pl.when(kk == n_k - 1)
        def _():
            dx, dg_rows = _rms_bwd(acc_ref[...], x_ref[...], g_ref[...])
            o_ref[...] = dres_ref[...] + dx
            dg_ref[...] += jnp.sum(dg_rows, axis=0, keepdims=True)

    return pl.pallas_call(
        body, grid=(m // tm, n_k),
        in_specs=[pl.BlockSpec((tm, tk), lambda i, kk: (i, kk)), pl.BlockSpec((tk, D), lambda i, kk: (kk, 0)),
                  pl.BlockSpec((tm, D), lambda i, kk: (i, 0)), pl.BlockSpec((1, D), lambda i, kk: (0, 0)),
                  pl.BlockSpec((tm, D), lambda i, kk: (i, 0))],
        out_specs=[pl.BlockSpec((tm, D), lambda i, kk: (i, 0)), pl.BlockSpec((1, D), lambda i, kk: (0, 0))],
        out_shape=[jax.ShapeDtypeStruct((m, D), F32), jax.ShapeDtypeStruct((1, D), F32)],
        scratch_shapes=[pltpu.VMEM((tm, D), F32)],
        compiler_params=_params("arbitrary", "arbitrary"), name=name)(dy, w, x, g, dres)


def _band(kvp, kvc):
    kk = jnp.concatenate([kvp[:, :128], kvc[:, :128]], axis=0)
    vv = jnp.concatenate([kvp[:, 128:], kvc[:, 128:]], axis=0)
    row = lax.broadcasted_iota(jnp.int32, kk.shape, 0)
    return jnp.where(row == 0, jnp.zeros_like(kk), kk), jnp.where(row == 0, jnp.zeros_like(vv), vv)


def _blockdiag(t, h):
    lane = lax.broadcasted_iota(jnp.int32, t.shape, 1)
    z = jnp.zeros_like(t)
    tr = pltpu.roll(t, 64, 1)
    if h == 0:
        top, bot = jnp.where(lane < 64, t, z), jnp.where(lane >= 64, tr, z)
    else:
        top, bot = jnp.where(lane < 64, tr, z), jnp.where(lane >= 64, t, z)
    return jnp.concatenate([top, bot], axis=0)


def _from_blockdiag(t, h):
    lane = lax.broadcasted_iota(jnp.int32, (256, 128), 1)
    top, bot = t[:256], t[256:]
    if h == 0:
        return jnp.where(lane < 64, top + pltpu.roll(bot, 64, 1), 0.0)
    return jnp.where(lane >= 64, pltpu.roll(top, 64, 1) + bot, 0.0)


def _stack_heads(ref, h):
    return jnp.concatenate([ref[:, h * 512 + p * 128: h * 512 + (p + 1) * 128] for p in range(4)], axis=0)


def _score_bias(sinks):
    row = jnp.arange(512)[:, None] & 127
    col = jnp.arange(256)[None, :]
    band = (col > row) & (col <= row + BLOCK)
    valid = jnp.stack([band & (col >= BLOCK), band])
    sink_rows = jnp.repeat(jnp.transpose(sinks.reshape(N_KV, 4, 2), (0, 2, 1)), BLOCK, axis=-1).reshape(4, 512, 1)
    bias = jnp.where(valid, 0.0, MASKED)[:, None] + jnp.zeros((1, 4, 1, 1), F32)
    return jnp.where(col == 0, sink_rows[None], bias).astype(F32)


def _half_selector(shape, split):
    row = lax.broadcasted_iota(jnp.int32, shape, 0)
    one, zero = jnp.ones(shape, BF16), jnp.zeros(shape, BF16)
    return jnp.where(row < split, one, zero), jnp.where(row >= split, one, zero)


def _attn_fwd(qkv, bias, ex=None, ex_args=()):
    s = qkv.shape[0]
    nb = s // BLOCK
    scale = 1.0 / math.sqrt(HEAD_DIM)

    def body(q_ref, kvc_ref, kvp_ref, sk_ref, o_ref, lse_ref):
        kk, vv = _band(kvp_ref[...], kvc_ref[...])
        sel0, sel1 = _half_selector((512, 128), 256)
        lane = lax.broadcasted_iota(jnp.int32, (512, 128), 1)
        for h in range(N_KV):
            kbd, vbd = _blockdiag(kk, h), _blockdiag(vv, h)
            sc = lax.dot_general(_stack_heads(q_ref, h), kbd, NT_DIMS, preferred_element_type=F32) * scale
            pes, mxs = [], []
            for half in range(2):
                sh = sc[:, half * 256:(half + 1) * 256] + sk_ref[h * 2 + half]
                mx = jnp.max(sh, axis=-1, keepdims=True)
                pes.append(jnp.exp(sh - mx).astype(BF16))
                mxs.append(mx)
            pb = jnp.concatenate(pes, axis=1)
            o_un = lax.dot_general(pb, vbd, NN_DIMS, preferred_element_type=F32)
            sums = [lax.dot_general(pb, sel, NN_DIMS, preferred_element_type=F32) for sel in (sel0, sel1)]
            o = o_un * jnp.where(lane < 64, 1.0 / sums[0], 1.0 / sums[1])
            for half in range(2):
                lse_ref[h * 2 + half] = mxs[half] + jnp.log(sums[half])
            for p in range(4):
                o_ref[:, h * 512 + p * 128: h * 512 + (p + 1) * 128] = o[p * 128:(p + 1) * 128].astype(BF16)

    return pl.pallas_call(
        _carried(body, 4, 2, ex, nb), grid=(nb,),
        in_specs=[pl.BlockSpec((BLOCK, NQ), lambda n: (n, 0)),
                  pl.BlockSpec((BLOCK, KVW), lambda n: (n, NQ // KVW)),
                  pl.BlockSpec((BLOCK, KVW), lambda n: (jnp.maximum(n - 1, 0), NQ // KVW)),
                  pl.BlockSpec((None, 4, 512, 256), lambda n: (jnp.minimum(n, 1), 0, 0, 0))] + _ex(ex, "in_specs"),
        out_specs=[pl.BlockSpec((BLOCK, NQ), lambda n: (n, 0)),
                   pl.BlockSpec((None, 4, 512, 128), lambda n: (n, 0, 0, 0))] + _ex(ex, "out_specs"),
        out_shape=[jax.ShapeDtypeStruct((s, NQ), BF16), jax.ShapeDtypeStruct((nb, 4, 512, 128), F32)] + _ex(ex, "out_shape"),
        scratch_shapes=_ex(ex, "scratch"),
        compiler_params=_params("arbitrary"), name="attn_fwd")(qkv, qkv, qkv, bias, *ex_args)


def _attn_bwd(qkv, do, o, lse, bias, ex=None, ex_args=()):
    s = qkv.shape[0]
    nb = s // BLOCK
    scale = 1.0 / math.sqrt(HEAD_DIM)

    def body(q_ref, kvc_ref, kvp_ref, do_ref, o_ref, lse_ref, sk_ref, dq_ref, dkv_ref, dsk_ref, ck_ref, cv_ref):
        n = pl.program_id(0)

        @pl.when(n == 0)
        def _():
            dsk_ref[...] = jnp.zeros_like(dsk_ref)
            ck_ref[...] = jnp.zeros_like(ck_ref)
            cv_ref[...] = jnp.zeros_like(cv_ref)

        @pl.when(n < nb)
        def _():
            kk, vv = _band(kvp_ref[...], kvc_ref[...])
            sel0, sel1 = _half_selector((128, 128), 64)
            dkk = jnp.zeros((256, 128), F32)
            dvv = jnp.zeros((256, 128), F32)
            for h in range(N_KV):
                kbd, vbd = _blockdiag(kk, h), _blockdiag(vv, h)
                qp = _stack_heads(q_ref, h)
                dop = _stack_heads(do_ref, h)
                sc = lax.dot_general(qp, kbd, NT_DIMS, preferred_element_type=F32) * scale
                dp = lax.dot_general(dop, vbd, NT_DIMS, preferred_element_type=F32)
                dd = (dop.astype(F32) * _stack_heads(o_ref, h).astype(F32)).astype(BF16)
                probs, dss = [], []
                for half, sel in enumerate((sel0, sel1)):
                    delta = lax.dot_general(dd, sel, NN_DIMS, preferred_element_type=F32)
                    lse_h = lse_ref[h * 2 + half]
                    sh = sc[:, half * 256:(half + 1) * 256] + sk_ref[h * 2 + half]
                    pr = jnp.exp(sh - jnp.concatenate([lse_h, lse_h], axis=1))
                    dsf = pr * (dp[:, half * 256:(half + 1) * 256] - jnp.concatenate([delta, delta], axis=1))
                    dsk_ref[h * 2 + half] += dsf[:, 0:1]
                    dss.append((dsf * scale).astype(BF16))
                    probs.append(pr.astype(BF16))
                ds = jnp.concatenate(dss, axis=1)
                pb = jnp.concatenate(probs, axis=1)
                dqp = lax.dot_general(ds, kbd, NN_DIMS, preferred_element_type=F32)
                for p in range(4):
                    dq_ref[:, h * 512 + p * 128: h * 512 + (p + 1) * 128] = dqp[p * 128:(p + 1) * 128].astype(BF16)
                dkk = dkk + _from_blockdiag(lax.dot_general(ds, qp, TN_DIMS, preferred_element_type=F32), h)
                dvv = dvv + _from_blockdiag(lax.dot_general(pb, dop, TN_DIMS, preferred_element_type=F32), h)
            row = lax.broadcasted_iota(jnp.int32, (256, 128), 0)
            dkk = jnp.where(row == 0, 0.0, dkk)
            dvv = jnp.where(row == 0, 0.0, dvv)

            @pl.when(n >= 1)
            def _():
                dkv_ref[:, :128] = (ck_ref[...] + dkk[:128]).astype(BF16)
                dkv_ref[:, 128:] = (cv_ref[...] + dvv[:128]).astype(BF16)

            ck_ref[...] = dkk[128:]
            cv_ref[...] = dvv[128:]

        @pl.when(n == nb)
        def _():
            dkv_ref[:, :128] = ck_ref[...].astype(BF16)
            dkv_ref[:, 128:] = cv_ref[...].astype(BF16)

    last = nb - 1
    blk = lambda n: (jnp.minimum(n, last), 0)
    return pl.pallas_call(
        _carried(body, 7, 3, ex, nb + 1), grid=(nb + 1,),
        in_specs=[pl.BlockSpec((BLOCK, NQ), blk),
                  pl.BlockSpec((BLOCK, KVW), lambda n: (jnp.minimum(n, last), NQ // KVW)),
                  pl.BlockSpec((BLOCK, KVW), lambda n: (jnp.clip(n - 1, 0, last), NQ // KVW)),
                  pl.BlockSpec((BLOCK, NQ), blk), pl.BlockSpec((BLOCK, NQ), blk),
                  pl.BlockSpec((None, 4, 512, 128), lambda n: (jnp.minimum(n, last), 0, 0, 0)),
                  pl.BlockSpec((None, 4, 512, 256), lambda n: (jnp.minimum(n, 1), 0, 0, 0))] + _ex(ex, "in_specs"),
        out_specs=[pl.BlockSpec((BLOCK, NQ), blk),
                   pl.BlockSpec((BLOCK, KVW), lambda n: (jnp.maximum(n - 1, 0), 0)),
                   pl.BlockSpec((4, 512, 1), lambda n: (0, 0, 0))] + _ex(ex, "out_specs"),
        out_shape=[jax.ShapeDtypeStruct((s, NQ), BF16), jax.ShapeDtypeStruct((s, KVW), BF16),
                   jax.ShapeDtypeStruct((4, 512, 1), F32)] + _ex(ex, "out_shape"),
        scratch_shapes=[pltpu.VMEM((BLOCK, 128), F32), pltpu.VMEM((BLOCK, 128), F32)] + _ex(ex, "scratch"),
        compiler_params=_params("arbitrary"), name="attn_bwd")(qkv, qkv, qkv, do, o, lse, bias, *ex_args)


LANE = 128
FFN_HALF = F // 2
FFN_PAIR = 2 * LANE
FFN_PAIRS = F // FFN_PAIR
FFN_GROUPS = ((0, 2), (2, 4), (4, 6), (6, 8), (8, 10), (10, 11))
FFN_HALO = 8


def _resident(shape):
    return pl.BlockSpec(shape, lambda i: (0,) * len(shape), pipeline_mode=pl.Buffered(1))


def _ffn_fwd(tag, x_in, g, w_up_t, w_dw, b_dw, w_down, *, tm, pre=None, head=None, ex=None, ex_args=()):
    s = x_in.shape[0]
    tm = min(tm, s)
    n_pre = 3 if pre is not None else 0
    n_head = 2 if head is not None else 0

    def body(*refs):
        x_ref, g_ref, wup_ref, wdw_ref, bdw_ref, wd_ref = refs[:6]
        pre_refs = refs[6:6 + n_pre]
        head_refs = refs[6 + n_pre:6 + n_pre + n_head]
        p = 6 + n_pre + n_head
        o_ref, h_ref, up_ref, act_ref, gate_ref = refs[p:p + 5]
        p += 5
        if pre is not None:
            xmid_ref, p = refs[p], p + 1
        if head is not None:
            loss_ref, dgf_ref, p = refs[p], refs[p + 1], p + 2
        carry_ref, ext_ref = refs[p], refs[p + 1]

        @pl.when(pl.program_id(0) == 0)
        def _():
            carry_ref[...] = jnp.zeros_like(carry_ref)
            if head is not None:
                loss_ref[...] = jnp.zeros_like(loss_ref)
                dgf_ref[...] = jnp.zeros_like(dgf_ref)

        xf = x_ref[...]
        if pre is not None:
            oin_ref, wo_ref, bo_ref = pre_refs
            xf = xf + lax.dot_general(oin_ref[...], wo_ref[...], NN_DIMS, preferred_element_type=F32) + bo_ref[...]
            xmid_ref[...] = xf
        r = lax.rsqrt(jnp.mean(xf * xf, axis=-1, keepdims=True) + RMS_EPS)
        h = ((xf * r) * g_ref[...]).astype(BF16)
        h_ref[...] = h

        def up_mm(p):
            for off in (p * FFN_PAIR, F + p * FFN_PAIR):
                rs = slice(off, off + FFN_PAIR)
                up_ref[:, rs] = lax.dot_general(h, wup_ref[rs, :], NT_DIMS, preferred_element_type=F32).astype(BF16)

        def elementwise(p):
            for c in (2 * p, 2 * p + 1):
                cs = slice(c * LANE, (c + 1) * LANE)
                vs = slice(F + c * LANE, F + (c + 1) * LANE)
                gcol = up_ref[:, cs].astype(F32)
                ext_ref[pl.ds(0, FFN_HALO), :] = carry_ref[:, cs]
                ext_ref[pl.ds(FFN_HALO, tm), :] = gcol
                gate_b = (wdw_ref[2:3, cs] * gcol + wdw_ref[1:2, cs] * ext_ref[pl.ds(FFN_HALO - 1, tm), :]
                          + wdw_ref[0:1, cs] * ext_ref[pl.ds(FFN_HALO - 2, tm), :] + bdw_ref[:, cs]).astype(BF16)
                gate_ref[:, cs] = gate_b
                gate = gate_b.astype(F32)
                act_ref[:, cs] = (gate * _sigmoid(gate) * up_ref[:, vs].astype(F32)).astype(BF16)
                carry_ref[:, cs] = gcol[tm - FFN_HALO:]

        def down_mm(lo, hi):
            ks = slice(lo * FFN_PAIR, hi * FFN_PAIR)
            return lax.dot_general(act_ref[:, ks], wd_ref[ks, :], NN_DIMS, preferred_element_type=F32)

        acc = xf
        up_mm(0)
        pending = None
        for lo, hi in FFN_GROUPS:
            for p in range(lo, hi):
                if p + 1 < FFN_PAIRS:
                    up_mm(p + 1)
                if pending is not None:
                    acc = acc + down_mm(*pending)
                    pending = None
                elementwise(p)
            pending = (lo, hi)
        x_out = acc + down_mm(*pending)
        if head is None:
            o_ref[...] = x_out
        else:
            gf_ref, tgt_ref = head_refs
            gf = gf_ref[...]
            rf = lax.rsqrt(jnp.mean(x_out * x_out, axis=-1, keepdims=True) + RMS_EPS)
            diff = (x_out * rf) * gf - tgt_ref[...]
            loss_ref[...] += 0.5 * jnp.sum(jnp.mean(diff * diff, axis=-1, keepdims=True))
            dx, dg_rows = _rms_bwd(diff * (1.0 / D), x_out, gf)
            o_ref[...] = dx
            dgf_ref[...] += jnp.sum(dg_rows, axis=0, keepdims=True)

    row = lambda i: (i, 0)
    const = lambda i: (0, 0)
    in_specs = [pl.BlockSpec((tm, D), row), _resident((1, D)), _resident((2 * F, D)), _resident((3, F)),
                _resident((1, F)), _resident((F, D))]
    out_specs = [pl.BlockSpec((tm, D), row), pl.BlockSpec((tm, D), row), pl.BlockSpec((tm, 2 * F), row),
                 pl.BlockSpec((tm, F), row), pl.BlockSpec((tm, F), row)]
    out_shape = [jax.ShapeDtypeStruct((s, D), F32), jax.ShapeDtypeStruct((s, D), BF16),
                 jax.ShapeDtypeStruct((s, 2 * F), BF16), jax.ShapeDtypeStruct((s, F), BF16),
                 jax.ShapeDtypeStruct((s, F), BF16)]
    args = [x_in, g, w_up_t, w_dw, b_dw, w_down]
    if pre is not None:
        in_specs += [pl.BlockSpec((tm, NQ), row), _resident((NQ, D)), _resident((1, D))]
        out_specs.append(pl.BlockSpec((tm, D), row))
        out_shape.append(jax.ShapeDtypeStruct((s, D), F32))
        args += list(pre)
    if head is not None:
        in_specs += [_resident((1, D)), pl.BlockSpec((tm, D), row)]
        out_specs += [pl.BlockSpec((1, 128), const), pl.BlockSpec((1, D), const)]
        out_shape += [jax.ShapeDtypeStruct((1, 128), F32), jax.ShapeDtypeStruct((1, D), F32)]
        args += list(head)
    return pl.pallas_call(
        _carried(body, len(args), len(out_shape), ex, s // tm), grid=(s // tm,),
        in_specs=in_specs + _ex(ex, "in_specs"), out_specs=out_specs + _ex(ex, "out_specs"),
        out_shape=out_shape + _ex(ex, "out_shape"),
        scratch_shapes=[pltpu.VMEM((FFN_HALO, F), F32), pltpu.VMEM((tm + FFN_HALO, LANE), F32)] + _ex(ex, "scratch"),
        compiler_params=_params("arbitrary"), name=f"ffn{tag}_fwd")(*args, *ex_args)


def _ffn_bwd(tag, dx_out, x_in, g, w_up_t, w_dw, w_down, up, gate, *, tm):
    s = x_in.shape[0]
    tm = min(tm, s)
    n_i = s // tm

    def body(dx_ref, x_ref, g_ref, up_ref, gate_ref, wup_ref, wdw_ref, wd_ref,
             dxin_ref, dup_ref, dg_ref, dwb_ref, carry_ref, da0_ref, da1_ref, extd_ref):
        i = pl.program_id(0)

        @pl.when(i == 0)
        def _():
            carry_ref[...] = jnp.zeros_like(carry_ref)
            dg_ref[...] = jnp.zeros_like(dg_ref)
            dwb_ref[...] = jnp.zeros_like(dwb_ref)

        dxf = dx_ref[...]
        dxb = dxf.astype(BF16)
        da_refs = (da0_ref, da1_ref)

        def da_mm(p):
            ks = slice(p * FFN_PAIR, (p + 1) * FFN_PAIR)
            da_refs[p % 2][...] = lax.dot_general(dxb, wd_ref[ks, :], NT_DIMS, preferred_element_type=F32)

        def elementwise(p):
            da_ref = da_refs[p % 2]
            for c in range(2):
                cc = 2 * p + c
                cs = slice(cc * LANE, (cc + 1) * LANE)
                vs = slice(F + cc * LANE, F + (cc + 1) * LANE)
                gate = gate_ref[:, cs].astype(F32)
                sg = _sigmoid(gate)
                silu = gate * sg
                dac = da_ref[:, c * LANE:(c + 1) * LANE]
                dup_ref[:, vs] = (dac * silu).astype(BF16)
                dgate = dac * up_ref[:, vs].astype(F32) * (sg + silu * (1.0 - sg))
                extd_ref[pl.ds(0, tm), :] = dgate
                extd_ref[pl.ds(tm, FFN_HALO), :] = carry_ref[:, cs]
                d1 = extd_ref[pl.ds(1, tm), :]
                d2 = extd_ref[pl.ds(2, tm), :]
                dup_ref[:, cs] = (wdw_ref[2:3, cs] * dgate + wdw_ref[1:2, cs] * d1 + wdw_ref[0:1, cs] * d2).astype(BF16)
                carry_ref[:, cs] = dgate[:FFN_HALO]
                gcol = up_ref[:, cs].astype(F32)
                dwb_ref[2, :, cs] += _sum8(dgate * gcol)
                dwb_ref[1, :, cs] += _sum8(d1 * gcol)
                dwb_ref[0, :, cs] += _sum8(d2 * gcol)
                dwb_ref[3, :, cs] += _sum8(dgate)

        def dh_mm(lo, hi):
            ks = slice(lo * FFN_PAIR, hi * FFN_PAIR)
            vks = slice(F + lo * FFN_PAIR, F + hi * FFN_PAIR)
            return (lax.dot_general(dup_ref[:, ks], wup_ref[ks, :], NN_DIMS, preferred_element_type=F32)
                    + lax.dot_general(dup_ref[:, vks], wup_ref[vks, :], NN_DIMS, preferred_element_type=F32))

        acc = None
        da_mm(0)
        pending = None
        for lo, hi in FFN_GROUPS:
            for p in range(lo, hi):
                if p + 1 < FFN_PAIRS:
                    da_mm(p + 1)
                if pending is not None:
                    part = dh_mm(*pending)
                    acc = part if acc is None else acc + part
                    pending = None
                elementwise(p)
            pending = (lo, hi)
        acc = acc + dh_mm(*pending)
        dx, dg_rows = _rms_bwd(acc, x_ref[...], g_ref[...])
        dxin_ref[...] = dxf + dx
        dg_ref[...] += jnp.sum(dg_rows, axis=0, keepdims=True)

    rev = lambda i: (n_i - 1 - i, 0)
    return pl.pallas_call(
        body, grid=(n_i,),
        in_specs=[pl.BlockSpec((tm, D), rev), pl.BlockSpec((tm, D), rev), _resident((1, D)),
                  pl.BlockSpec((tm, 2 * F), rev), pl.BlockSpec((tm, F), rev),
                  _resident((2 * F, D)), _resident((3, F)), _resident((F, D))],
        out_specs=[pl.BlockSpec((tm, D), rev), pl.BlockSpec((tm, 2 * F), rev),
                   pl.BlockSpec((1, D), lambda i: (0, 0)), pl.BlockSpec((4, 8, F), lambda i: (0, 0, 0))],
        out_shape=[jax.ShapeDtypeStruct((s, D), F32), jax.ShapeDtypeStruct((s, 2 * F), BF16),
                   jax.ShapeDtypeStruct((1, D), F32), jax.ShapeDtypeStruct((4, 8, F), F32)],
        scratch_shapes=[pltpu.VMEM((FFN_HALO, F), F32), pltpu.VMEM((tm, FFN_PAIR), F32), pltpu.VMEM((tm, FFN_PAIR), F32),
                        pltpu.VMEM((tm + FFN_HALO, LANE), F32)],
        compiler_params=_params("arbitrary"), name=f"ffn{tag}_bwd")(dx_out, x_in, g, up, gate, w_up_t, w_dw, w_down)


CONV_HALO = 32
CONV_CHUNK = 256
CONV_ROWS = 64


def _ln(u2, g, b):
    mu = jnp.mean(u2, axis=-1, keepdims=True)
    xc = u2 - mu
    rstd = lax.rsqrt(jnp.mean(xc * xc, axis=-1, keepdims=True) + LN_EPS)
    xhat = xc * rstd
    return xhat, rstd, xhat * g + b


def _phase_taps(ph):
    first = CONV_HALO - (CONV_K - 1)
    return [(k, first + k - ph) for k in range(CONV_K) if (first + k) % 8 == ph]


def _sum8(v):
    out = v[0:8]
    for j in range(8, v.shape[0], 8):
        out = out + v[j:j + 8]
    return out


def _store_phases(src_ref, sh_ref, tt, cols=slice(None)):
    sh_ref[0] = src_ref[:, cols]
    for ph in range(1, 8):
        sh_ref[ph, pl.ds(0, tt + CONV_HALO - 8), :] = src_ref[pl.ds(ph, tt + CONV_HALO - 8), cols]


def _conv_taps_fwd(w_ref, sh_ref, u2_ref, cs, tt):
    for base in range(0, tt, CONV_ROWS):
        acc = jnp.zeros((CONV_ROWS, CONV_CHUNK), F32) + w_ref[CONV_K:CONV_K + 1, cs]
        for ph in range(8):
            for k, off in _phase_taps(ph):
                acc = acc + w_ref[k:k + 1, cs] * sh_ref[ph, pl.ds(base + off, CONV_ROWS), :]
        u2_ref[pl.ds(base, CONV_ROWS), cs] = acc.astype(BF16)


def _conv_fwd(x_in, g, w1t, b1, wdw, ln_g, ln_b, w2, b2, *, tt):
    s = x_in.shape[0]
    tt = min(tt, s)
    n_c = D // CONV_CHUNK

    def body(x_ref, g_ref, w1_ref, b1_ref, w_ref, lg_ref, lb_ref, w2_ref, b2_ref,
             o_ref, h_ref, a_ref, u2_ref, s_ref, carry_ref, ext_ref, sh_ref):
        @pl.when(pl.program_id(0) == 0)
        def _():
            carry_ref[...] = jnp.zeros_like(carry_ref)

        xf = x_ref[...]
        r = lax.rsqrt(jnp.mean(xf * xf, axis=-1, keepdims=True) + RMS_EPS)
        h = ((xf * r) * g_ref[...]).astype(BF16)
        h_ref[...] = h

        def pw1_mm(c):
            for off in (c * CONV_CHUNK, D + c * CONV_CHUNK):
                rs = slice(off, off + CONV_CHUNK)
                a_ref[:, rs] = (lax.dot_general(h, w1_ref[rs, :], NT_DIMS, preferred_element_type=F32)
                                + b1_ref[:, rs]).astype(BF16)

        pw1_mm(0)
        for c in range(n_c):
            if c + 1 < n_c:
                pw1_mm(c + 1)
            cs = slice(c * CONV_CHUNK, (c + 1) * CONV_CHUNK)
            gs = slice(D + c * CONV_CHUNK, D + (c + 1) * CONV_CHUNK)
            u = a_ref[:, cs].astype(F32) * _sigmoid(a_ref[:, gs].astype(F32))
            ext_ref[pl.ds(0, CONV_HALO), :] = carry_ref[:, cs]
            ext_ref[pl.ds(CONV_HALO, tt), :] = u
            carry_ref[:, cs] = u[tt - CONV_HALO:]
            _store_phases(ext_ref, sh_ref, tt)
            _conv_taps_fwd(w_ref, sh_ref, u2_ref, cs, tt)
        _, _, ln = _ln(u2_ref[...].astype(F32), lg_ref[...], lb_ref[...])
        sv = (ln * _sigmoid(ln)).astype(BF16)
        s_ref[...] = sv
        o_ref[...] = xf + lax.dot_general(sv, w2_ref[...], NN_DIMS, preferred_element_type=F32) + b2_ref[...]

    row = lambda i: (i, 0)
    return pl.pallas_call(
        body, grid=(s // tt,),
        in_specs=[pl.BlockSpec((tt, D), row), _resident((1, D)), _resident((2 * D, D)), _resident((1, 2 * D)),
                  _resident((32, D)), _resident((1, D)), _resident((1, D)), _resident((D, D)), _resident((1, D))],
        out_specs=[pl.BlockSpec((tt, D), row), pl.BlockSpec((tt, D), row), pl.BlockSpec((tt, 2 * D), row),
                   pl.BlockSpec((tt, D), row), pl.BlockSpec((tt, D), row)],
        out_shape=[jax.ShapeDtypeStruct((s, D), F32), jax.ShapeDtypeStruct((s, D), BF16),
                   jax.ShapeDtypeStruct((s, 2 * D), BF16), jax.ShapeDtypeStruct((s, D), BF16),
                   jax.ShapeDtypeStruct((s, D), BF16)],
        scratch_shapes=[pltpu.VMEM((CONV_HALO, D), F32), pltpu.VMEM((tt + CONV_HALO, CONV_CHUNK), F32),
                        pltpu.VMEM((8, tt + CONV_HALO, CONV_CHUNK), F32)],
        compiler_params=_params("arbitrary"), name="conv_fwd")(x_in, g, w1t, b1, wdw, ln_g, ln_b, w2, b2)


def _conv_bwd(dx_out, x_in, g, a, u2, w1t, wdw, ln_g, ln_b, w2, *, tt, ex=None, ex_args=()):
    s = x_in.shape[0]
    tt = min(tt, s)
    hb = tt // CONV_HALO
    n_i = s // tt
    te = tt + CONV_HALO
    n_c = D // CONV_CHUNK

    def body(dx_ref, x_ref, g_ref, a_ref, prev_ref, u2_ref, w1_ref, w_ref, lg_ref, lb_ref, w2_ref,
             dxin_ref, da_ref, dg_ref, acc_ref, db1_ref, db2_ref,
             carry_ref, e2_ref, ext_ref, ush_ref, dsh_ref, du_ref):
        i = pl.program_id(0)

        @pl.when(i == 0)
        def _():
            for ref in (dg_ref, acc_ref, db1_ref, db2_ref, carry_ref):
                ref[...] = jnp.zeros_like(ref)

        dxf = dx_ref[...]
        db2_ref[...] += _sum8(dxf)
        dse = lax.dot_general(dxf.astype(BF16), w2_ref[...], NT_DIMS, preferred_element_type=F32)
        gv = lg_ref[...]
        xhat, rstd, ln = _ln(u2_ref[...].astype(F32), gv, lb_ref[...])
        sg = _sigmoid(ln)
        dln = dse * (sg * (1.0 + ln * (1.0 - sg)))
        acc_ref[32] += _sum8(dln * xhat)
        acc_ref[33] += _sum8(dln)
        dxh = dln * gv
        du2 = rstd * (dxh - jnp.mean(dxh, axis=-1, keepdims=True) - xhat * jnp.mean(dxh * xhat, axis=-1, keepdims=True))
        e2_ref[pl.ds(0, tt), :] = du2
        e2_ref[pl.ds(tt, CONV_HALO), :] = carry_ref[...]
        carry_ref[...] = du2[:CONV_HALO]
        first_tile = i == n_i - 1

        def dh_mm(c):
            cs = slice(c * CONV_CHUNK, (c + 1) * CONV_CHUNK)
            gs = slice(D + c * CONV_CHUNK, D + (c + 1) * CONV_CHUNK)
            return (lax.dot_general(da_ref[:, cs], w1_ref[cs, :], NN_DIMS, preferred_element_type=F32)
                    + lax.dot_general(da_ref[:, gs], w1_ref[gs, :], NN_DIMS, preferred_element_type=F32))

        dh = None
        for c in range(n_c):
            if c >= 1:
                part = dh_mm(c - 1)
                dh = part if dh is None else dh + part
            cs = slice(c * CONV_CHUNK, (c + 1) * CONV_CHUNK)
            gs = slice(D + c * CONV_CHUNK, D + (c + 1) * CONV_CHUNK)
            uh = prev_ref[:, cs].astype(F32) * _sigmoid(prev_ref[:, gs].astype(F32))
            ext_ref[pl.ds(0, CONV_HALO), :] = jnp.where(first_tile, 0.0, uh)
            a1 = a_ref[:, cs].astype(F32)
            sg2 = _sigmoid(a_ref[:, gs].astype(F32))
            ext_ref[pl.ds(CONV_HALO, tt), :] = a1 * sg2
            _store_phases(ext_ref, ush_ref, tt)
            _store_phases(e2_ref, dsh_ref, tt, cs)
            for base in range(0, tt, CONV_ROWS):
                d0 = e2_ref[pl.ds(base, CONV_ROWS), cs]
                du = jnp.zeros((CONV_ROWS, CONV_CHUNK), F32)
                for ph in range(8):
                    for k, off in _phase_taps(ph):
                        acc_ref[k, :, cs] += _sum8(d0 * ush_ref[ph, pl.ds(base + off, CONV_ROWS), :])
                    for k in range(CONV_K):
                        if (CONV_K - 1 - k) % 8 == ph:
                            du = du + w_ref[k:k + 1, cs] * dsh_ref[ph, pl.ds(base + CONV_K - 1 - k - ph, CONV_ROWS), :]
                acc_ref[CONV_K, :, cs] += _sum8(d0)
                du_ref[pl.ds(base, CONV_ROWS), :] = du
            du = du_ref[...]
            da1 = du * sg2
            da2 = du * a1 * (sg2 * (1.0 - sg2))
            da_ref[:, cs] = da1.astype(BF16)
            da_ref[:, gs] = da2.astype(BF16)
            db1_ref[:, cs] += _sum8(da1)
            db1_ref[:, gs] += _sum8(da2)
        dh = dh + dh_mm(n_c - 1)
        dx, dg_rows = _rms_bwd(dh, x_ref[...], g_ref[...])
        dxin_ref[...] = dxf + dx
        dg_ref[...] += jnp.sum(dg_rows, axis=0, keepdims=True)

    rev = lambda i: (n_i - 1 - i, 0)
    return pl.pallas_call(
        _carried(body, 11, 6, ex, n_i), grid=(n_i,),
        in_specs=[pl.BlockSpec((tt, D), rev), pl.BlockSpec((tt, D), rev), _resident((1, D)),
                  pl.BlockSpec((tt, 2 * D), rev),
                  pl.BlockSpec((CONV_HALO, 2 * D), lambda i: (jnp.maximum((n_i - 1 - i) * hb - 1, 0), 0)),
                  pl.BlockSpec((tt, D), rev), _resident((2 * D, D)), _resident((32, D)), _resident((1, D)),
                  _resident((1, D)), _resident((D, D))] + _ex(ex, "in_specs"),
        out_specs=[pl.BlockSpec((tt, D), rev), pl.BlockSpec((tt, 2 * D), rev), pl.BlockSpec((1, D), lambda i: (0, 0)),
                   pl.BlockSpec((40, 8, D), lambda i: (0, 0, 0)), pl.BlockSpec((8, 2 * D), lambda i: (0, 0)),
                   pl.BlockSpec((8, D), lambda i: (0, 0))] + _ex(ex, "out_specs"),
        out_shape=[jax.ShapeDtypeStruct((s, D), F32), jax.ShapeDtypeStruct((s, 2 * D), BF16),
                   jax.ShapeDtypeStruct((1, D), F32), jax.ShapeDtypeStruct((40, 8, D), F32),
                   jax.ShapeDtypeStruct((8, 2 * D), F32), jax.ShapeDtypeStruct((8, D), F32)] + _ex(ex, "out_shape"),
        scratch_shapes=[pltpu.VMEM((CONV_HALO, D), F32), pltpu.VMEM((te, D), F32), pltpu.VMEM((te, CONV_CHUNK), F32),
                        pltpu.VMEM((8, te, CONV_CHUNK), F32), pltpu.VMEM((8, te, CONV_CHUNK), F32),
                        pltpu.VMEM((tt, CONV_CHUNK), F32)] + _ex(ex, "scratch"),
        compiler_params=_params("arbitrary"), name="conv_bwd")(dx_out, x_in, g, a, a, u2, w1t, wdw, ln_g, ln_b, w2, *ex_args)


def _adamw(name, w, g, m, v):
    rows, cols = w.shape
    tr = _row_tile(rows, 256)

    def body(w_ref, g_ref, m_ref, v_ref, d_ref, nm_ref, nv_ref):
        gv = g_ref[...]
        m2 = ADAM_B1 * m_ref[...] + (1.0 - ADAM_B1) * gv
        v2 = ADAM_B2 * v_ref[...] + (1.0 - ADAM_B2) * (gv * gv)
        m_hat = m2 / (1.0 - ADAM_B1 ** ADAM_STEP)
        v_hat = v2 / (1.0 - ADAM_B2 ** ADAM_STEP)
        d_ref[...] = -ADAM_LR * (m_hat / (jnp.sqrt(v_hat) + ADAM_EPS) + ADAM_WD * w_ref[...])
        nm_ref[...] = m2
        nv_ref[...] = v2

    spec = pl.BlockSpec((tr, cols), lambda i: (i, 0))
    return pl.pallas_call(
        body, grid=(rows // tr,), in_specs=[spec] * 4, out_specs=[spec] * 3,
        out_shape=[jax.ShapeDtypeStruct((rows, cols), F32)] * 3,
        compiler_params=_params("parallel"), name=name)(w, g, m, v)


def _sum_slabs(name, buf):
    rows = buf.shape[1]
    tr = _row_tile(rows, 512, 16)

    def body(b_ref, o_ref):
        acc = b_ref[0].astype(F32)
        for k in range(1, N_DEV):
            acc = acc + b_ref[k].astype(F32)
        o_ref[...] = acc

    return pl.pallas_call(
        body, grid=(rows // tr,), in_specs=[pl.BlockSpec((N_DEV, tr, D), lambda i: (0, i, 0))],
        out_specs=pl.BlockSpec((tr, D), lambda i: (i, 0)), out_shape=jax.ShapeDtypeStruct((rows, D), F32),
        compiler_params=_params("parallel"), name=name)(buf)


def _ex(ex, field):
    return list(getattr(ex, field)) if ex is not None else []


def _me():
    x, y, c = lax.axis_index("x"), lax.axis_index("y"), lax.axis_index("c")
    return x, y, c, 4 * x + 2 * y + c


def _peer(x, y, c, k):
    return (x ^ (k >> 2), y ^ ((k >> 1) & 1), c ^ (k & 1))


class _Gather:
    def __init__(self, mats, slab_base=0):
        self.mats = tuple(mats)
        self.slab_base = slab_base
        self.rows = sum(MAT_ROWS[i] for i in self.mats)
        any_spec = pl.BlockSpec(memory_space=pl.ANY)
        self.n_in, self.n_out = 1, len(self.mats)
        self.in_specs = [any_spec]
        self.out_specs = [any_spec] * self.n_out
        self.out_shape = [jax.ShapeDtypeStruct((N_DEV * MAT_ROWS[i], D), BF16) for i in self.mats]
        self.scratch = [pltpu.SemaphoreType.DMA((N_DEV - 1,)), pltpu.SemaphoreType.DMA((N_DEV - 1,)),
                        pltpu.SemaphoreType.DMA]
        assert max(s.shape[0] for s in self.out_shape) >= self.rows

    def start(self, ins, outs, scr):
        (slab_ref,), (send_sems, recv_sems, local_sem) = ins, scr
        x, y, c, me = _me()
        for n, i in enumerate(self.mats):
            src = slab_ref.at[pl.ds(MAT_OFF[i] - self.slab_base, MAT_ROWS[i])]
            dst = outs[n].at[pl.ds(me * MAT_ROWS[i], MAT_ROWS[i])]
            pltpu.make_async_copy(src, dst, local_sem).start()
            for k in range(1, N_DEV):
                pltpu.make_async_remote_copy(src_ref=src, dst_ref=dst, send_sem=send_sems.at[k - 1],
                                             recv_sem=recv_sems.at[k - 1], device_id=_peer(x, y, c, k),
                                             device_id_type=MESH).start()

    def wait(self, ins, outs, scr):
        send_sems, recv_sems, local_sem = scr
        x, y, c, _ = _me()
        big = max(range(self.n_out), key=lambda n: self.out_shape[n].shape[0])
        whole = outs[big].at[pl.ds(0, self.rows)]
        for k in range(1, N_DEV):
            pltpu.make_async_remote_copy(src_ref=whole, dst_ref=whole, send_sem=send_sems.at[k - 1],
                                         recv_sem=recv_sems.at[k - 1], device_id=_peer(x, y, c, k),
                                         device_id_type=MESH).wait()
        pltpu.make_async_copy(whole, whole, local_sem).wait()


class _Scatter:
    def __init__(self, mats):
        self.mats = tuple(mats)
        self.rows = sum(MAT_ROWS[i] for i in self.mats)
        any_spec = pl.BlockSpec(memory_space=pl.ANY)
        self.n_in, self.n_out = len(self.mats), 1
        self.in_specs = [any_spec] * self.n_in
        self.out_specs = [any_spec]
        self.out_shape = [jax.ShapeDtypeStruct((N_DEV, self.rows, D), BF16)]
        self.scratch = [pltpu.SemaphoreType.DMA((N_DEV - 1,)), pltpu.SemaphoreType.DMA((N_DEV - 1,)),
                        pltpu.SemaphoreType.DMA]

    def offsets(self):
        return [sum(MAT_ROWS[i] for i in self.mats[:n]) for n in range(len(self.mats))]

    def start(self, ins, outs, scr):
        (buf_ref,), (send_sems, recv_sems, local_sem) = outs, scr
        x, y, c, me = _me()
        for n, (i, off) in enumerate(zip(self.mats, self.offsets())):
            r = MAT_ROWS[i]
            pltpu.make_async_copy(ins[n].at[pl.ds(me * r, r)], buf_ref.at[0, pl.ds(off, r)], local_sem).start()
            for k in range(1, N_DEV):
                pltpu.make_async_remote_copy(src_ref=ins[n].at[pl.ds((me ^ k) * r, r)], dst_ref=buf_ref.at[k, pl.ds(off, r)],
                                             send_sem=send_sems.at[k - 1], recv_sem=recv_sems.at[k - 1],
                                             device_id=_peer(x, y, c, k), device_id_type=MESH).start()

    def wait(self, ins, outs, scr):
        (buf_ref,), (send_sems, recv_sems, local_sem) = outs, scr
        x, y, c, _ = _me()
        whole = buf_ref.at[0]
        for k in range(1, N_DEV):
            pltpu.make_async_remote_copy(src_ref=whole, dst_ref=whole, send_sem=send_sems.at[k - 1],
                                         recv_sem=recv_sems.at[k - 1], device_id=_peer(x, y, c, k),
                                         device_id_type=MESH).wait()
        pltpu.make_async_copy(whole, whole, local_sem).wait()


def _carried(body, n_in, n_out, ex, n_steps):
    if ex is None:
        return body

    def wrapped(*refs):
        ins, ex_ins = refs[:n_in], refs[n_in:n_in + ex.n_in]
        p = n_in + ex.n_in
        outs, ex_outs = refs[p:p + n_out], refs[p + n_out:p + n_out + ex.n_out]
        p += n_out + ex.n_out
        n_scr = len(refs) - p - len(ex.scratch)
        scr, ex_scr = refs[p:p + n_scr], refs[p + n_scr:]

        @pl.when(pl.program_id(0) == 0)
        def _():
            ex.start(ex_ins, ex_outs, ex_scr)

        body(*ins, *outs, *scr)

        @pl.when(pl.program_id(0) == n_steps - 1)
        def _():
            ex.wait(ex_ins, ex_outs, ex_scr)

    return wrapped


def _exchange_small(name, ex, ex_args, small, reduce):
    vmem_spec = pl.BlockSpec(memory_space=pltpu.VMEM)
    n_small = small.shape[0]

    def body(*refs):
        ex_ins, small_ref = refs[:ex.n_in], refs[ex.n_in]
        p = ex.n_in + 1
        ex_outs, res_ref = refs[p:p + ex.n_out], refs[p + ex.n_out]
        p += ex.n_out + 1
        if reduce:
            all_ref, p = refs[p], p + 1
        else:
            all_ref = res_ref
        sm_send, sm_recv = refs[p], refs[p + 1]
        ex_scr = refs[p + 2:]
        x, y, c, me = _me()
        ex.start(ex_ins, ex_outs, ex_scr)
        copies = [pltpu.make_async_remote_copy(
            src_ref=small_ref, dst_ref=all_ref.at[me], send_sem=sm_send.at[k - 1], recv_sem=sm_recv.at[k - 1],
            device_id=_peer(x, y, c, k), device_id_type=MESH) for k in range(1, N_DEV)]
        for cp in copies:
            cp.start()
        all_ref[me] = small_ref[...]
        for cp in copies:
            cp.wait()
        if reduce:
            acc = all_ref[0]
            for d in range(1, N_DEV):
                acc = acc + all_ref[d]
            res_ref[...] = acc
        ex.wait(ex_ins, ex_outs, ex_scr)

    res_shape = (n_small, D) if reduce else (N_DEV, n_small, D)
    scratch = ([pltpu.VMEM((N_DEV, n_small, D), F32)] if reduce else []) + [
        pltpu.SemaphoreType.DMA((N_DEV - 1,)), pltpu.SemaphoreType.DMA((N_DEV - 1,))] + ex.scratch
    outs = pl.pallas_call(
        body, in_specs=ex.in_specs + [vmem_spec], out_specs=ex.out_specs + [vmem_spec],
        out_shape=ex.out_shape + [jax.ShapeDtypeStruct(res_shape, F32)], scratch_shapes=scratch,
        compiler_params=pltpu.CompilerParams(vmem_limit_bytes=VMEM_LIMIT_BYTES), name=name)(*ex_args, small)
    return outs


def _pack_rows(arrs):
    rows = []
    for a in arrs:
        flat = a.reshape(-1).astype(F32)
        pad = (-flat.shape[0]) % D
        rows.append(jnp.pad(flat, (0, pad)).reshape(-1, D))
    slab = jnp.concatenate(rows, axis=0)
    return jnp.pad(slab, ((0, (-slab.shape[0]) % 8), (0, 0)))


def _unpack_rows(slab, shapes):
    out, r = [], 0
    for shp in shapes:
        size = math.prod(shp)
        nrows = -(-size // D)
        out.append(slab[r:r + nrows].reshape(-1)[:size].reshape(shp))
        r += nrows
    return out


def kernel(x, norm_mix, attn_w_qkv, attn_b_qkv, attn_sinks, attn_w_o, attn_b_o, conv_w_pw1, conv_b_pw1, conv_w_dw, conv_b_dw, conv_ln_g, conv_ln_b, conv_w_pw2, conv_b_pw2, norm_ffn, ffn_w_up, ffn_w_dw, ffn_b_dw, ffn_w_down, final_norm, loss_target, m_norm_mix, m_attn_w_qkv, m_attn_b_qkv, m_attn_sinks, m_attn_w_o, m_attn_b_o, m_conv_w_pw1, m_conv_b_pw1, m_conv_w_dw, m_conv_b_dw, m_conv_ln_g, m_conv_ln_b, m_conv_w_pw2, m_conv_b_pw2, m_norm_ffn, m_ffn_w_up, m_ffn_w_dw, m_ffn_b_dw, m_ffn_w_down, m_final_norm, v_norm_mix, v_attn_w_qkv, v_attn_b_qkv, v_attn_sinks, v_attn_w_o, v_attn_b_o, v_conv_w_pw1, v_conv_b_pw1, v_conv_w_dw, v_conv_b_dw, v_conv_ln_g, v_conv_ln_b, v_conv_w_pw2, v_conv_b_pw2, v_norm_ffn, v_ffn_w_up, v_ffn_w_dw, v_ffn_b_dw, v_ffn_w_down, v_final_norm):
    s = x.shape[1]
    me = 4 * lax.axis_index("x") + 2 * lax.axis_index("y") + lax.axis_index("c")
    x0 = x.reshape(s, D)
    tgt = loss_target.reshape(s, D)

    slab_qkv = attn_w_qkv[0].T.astype(BF16)
    slab = jnp.concatenate([attn_w_o[0], conv_w_pw1[0].T, conv_w_pw2[0],
                            ffn_w_up[0].T, ffn_w_down[0], ffn_w_up[1].T, ffn_w_down[1]], axis=0).astype(BF16)
    sharded_small = [conv_b_pw1, conv_w_dw, conv_b_dw, conv_ln_g, conv_ln_b, conv_b_pw2, ffn_w_dw]
    small_local = _pack_rows(sharded_small)
    w_qkv_t, small_all = _exchange_small("gather_first", _Gather((0,)), (slab_qkv,), small_local, reduce=False)
    parts = [_unpack_rows(small_all[d], [a.shape for a in sharded_small]) for d in range(N_DEV)]
    b_pw1, w_cdw, b_cdw, ln_g, ln_b, b_pw2, w_fdw = [jnp.concatenate([parts[d][i] for d in range(N_DEV)], axis=-1)
                                                      for i in range(len(sharded_small))]
    conv_w32 = jnp.concatenate([w_cdw[0], b_cdw], axis=0)
    score_bias = _score_bias(attn_sinks)

    qkv, h0 = _mm("qkv", x0, w_qkv_t, nt=True, tm=1024, tn=NQ + KVW, out_dtype=BF16, rms_g=norm_mix[0:1], bias=attn_b_qkv)
    o, lse, w_o, w_up0_t, w_dn0 = _attn_fwd(qkv, score_bias, _Gather((1, 4, 5), MAT_OFF[1]), (slab,))
    x2, h1, up0, act0, gate0, x1, w_pw1_t, w_pw2, w_up1_t, w_dn1 = _ffn_fwd(
        0, x0, norm_ffn[0:1], w_up0_t, w_fdw[0], ffn_b_dw[0:1], w_dn0, tm=256, pre=(o, w_o, attn_b_o),
        ex=_Gather((2, 3, 6, 7), MAT_OFF[1]), ex_args=(slab,))
    x3, h2, a, u2, s_act = _conv_fwd(x2, norm_mix[1:2], w_pw1_t, b_pw1, conv_w32, ln_g, ln_b, w_pw2, b_pw2, tt=256)
    dx4, h3, up1, act1, gate1, loss_acc, dg_final = _ffn_fwd(1, x3, norm_ffn[1:2], w_up1_t, w_fdw[1], ffn_b_dw[1:2], w_dn1,
                                                             tm=256, head=(final_norm.reshape(1, D), tgt))
    loss = lax.psum(loss_acc[0, 0], ("x", "y", "c"))

    dx3, d_up1, dg_f1, dwb1 = _ffn_bwd(1, dx4, x3, norm_ffn[1:2], w_up1_t, w_fdw[1], w_dn1, up1, gate1, tm=256)
    dw_dn1 = _mm_tn("ffn1_dwdown", act1, dx4, tn=FFN_HALF, tt=2048)
    dw_up1_t = _mm_tn("ffn1_dwup", d_up1, h3, tn=FFN_HALF, tt=2048)
    dw_pw2 = _mm_tn("dw_pw2", s_act, dx3, tn=D, tt=2048)
    scatter_a = _Scatter((6, 7))
    dx2, da, dg_m1, conv_acc8, db_pw1_8, db_pw2_8, buf_a = _conv_bwd(
        dx3, x2, norm_mix[1:2], a, u2, w_pw1_t, conv_w32, ln_g, ln_b, w_pw2, tt=256, ex=scatter_a, ex_args=(dw_up1_t, dw_dn1))
    conv_acc = jnp.sum(conv_acc8, axis=1)
    db_pw1 = jnp.sum(db_pw1_8, axis=0, keepdims=True)
    db_pw2 = jnp.sum(db_pw2_8, axis=0, keepdims=True)
    dw_pw1_t = _mm_tn("dw_pw1", da, h2, tn=D, tt=2048)
    dx1, d_up0, dg_f0, dwb0 = _ffn_bwd(0, dx2, x1, norm_ffn[0:1], w_up0_t, w_fdw[0], w_dn0, up0, gate0, tm=256)
    dw_dn0 = _mm_tn("ffn0_dwdown", act0, dx2, tn=FFN_HALF, tt=2048)
    dw_up0_t = _mm_tn("ffn0_dwup", d_up0, h1, tn=FFN_HALF, tt=2048)
    do = _mm("d_attn_out", dx1, w_o, nt=True, tm=1024, tn=D, out_dtype=BF16)
    dw_o, db_o = _mm_tn("dw_o", o, dx1, tn=D, tt=2048, colsum_r=True)
    scatter_b = _Scatter((1, 2, 3, 4, 5))
    dq, dkv, dsk, buf_b = _attn_bwd(qkv, do, o, lse, score_bias, scatter_b, (dw_o, dw_pw1_t, dw_pw2, dw_up0_t, dw_dn0))
    dqkv = jnp.concatenate([dq, dkv], axis=1)
    grad_x, dg_m0 = _dgrad_rms("d_qkv", dqkv, w_qkv_t, x0, norm_mix[0:1], dx1, tm=1024, tk=NQ + KVW)
    dw_qkv_t, db_qkv = _mm_tn("dw_qkv", dqkv, h0, tn=NQ + KVW, tt=2048, colsum_l=True)

    dwb0, dwb1 = jnp.sum(dwb0, axis=1), jnp.sum(dwb1, axis=1)
    d_sinks = jnp.transpose(jnp.sum(dsk.reshape(N_KV, 2, 4, BLOCK), axis=-1), (0, 2, 1)).reshape(1, 16)
    small_grads = [jnp.concatenate([dg_m0, dg_m1], axis=0), db_qkv, d_sinks, db_o, jnp.concatenate([dg_f0, dg_f1], axis=0),
                   jnp.stack([dwb0[3], dwb1[3]]), dg_final, db_pw1, conv_acc[0:CONV_K], conv_acc[31:32], conv_acc[32:33],
                   conv_acc[33:34], db_pw2, jnp.stack([dwb0[0:3], dwb1[0:3]])]
    small_shapes = [(2, D), (1, NQ + KVW), (1, 16), (1, D), (2, D), (2, F), (D,), (1, 2 * D), (1, CONV_K, D), (1, D),
                    (1, D), (1, D), (1, D), (2, 3, F)]
    scatter_c = _Scatter((0,))
    buf_c, small_sum = _exchange_small("scatter_last", scatter_c, (dw_qkv_t,), _pack_rows(small_grads), reduce=True)
    gm = [None] * N_MAT
    for tag, sc, buf in (("a", scatter_a, buf_a), ("b", scatter_b, buf_b), ("c", scatter_c, buf_c)):
        gslab = _sum_slabs("sum_slabs_" + tag, buf)
        for i, off in zip(sc.mats, sc.offsets()):
            gm[i] = gslab[off:off + MAT_ROWS[i]]
    (g_norm_mix, g_b_qkv, g_sinks, g_b_o, g_norm_ffn, g_ffn_b_dw, g_final, g_b_pw1_full, g_w_cdw_full, g_b_cdw_full,
     g_ln_g_full, g_ln_b_full, g_b_pw2_full, g_w_fdw_full) = _unpack_rows(small_sum, small_shapes)

    def shard(a, width):
        return lax.dynamic_slice_in_dim(a, me * width, width, axis=a.ndim - 1)

    grads = {
        "norm_mix": g_norm_mix, "attn_w_qkv": gm[0].T[None], "attn_b_qkv": g_b_qkv, "attn_sinks": g_sinks,
        "attn_w_o": gm[1][None], "attn_b_o": g_b_o, "conv_w_pw1": gm[2].T[None], "conv_b_pw1": shard(g_b_pw1_full, 256),
        "conv_w_dw": shard(g_w_cdw_full, 128), "conv_b_dw": shard(g_b_cdw_full, 128), "conv_ln_g": shard(g_ln_g_full, 128),
        "conv_ln_b": shard(g_ln_b_full, 128), "conv_w_pw2": gm[3][None], "conv_b_pw2": shard(g_b_pw2_full, 128),
        "norm_ffn": g_norm_ffn, "ffn_w_up": jnp.stack([gm[4].T, gm[6].T]), "ffn_w_dw": shard(g_w_fdw_full, 352),
        "ffn_b_dw": g_ffn_b_dw, "ffn_w_down": jnp.stack([gm[5], gm[7]]), "final_norm": g_final,
    }
    weights = dict(norm_mix=norm_mix, attn_w_qkv=attn_w_qkv, attn_b_qkv=attn_b_qkv, attn_sinks=attn_sinks, attn_w_o=attn_w_o, attn_b_o=attn_b_o, conv_w_pw1=conv_w_pw1, conv_b_pw1=conv_b_pw1, conv_w_dw=conv_w_dw, conv_b_dw=conv_b_dw, conv_ln_g=conv_ln_g, conv_ln_b=conv_ln_b, conv_w_pw2=conv_w_pw2, conv_b_pw2=conv_b_pw2, norm_ffn=norm_ffn, ffn_w_up=ffn_w_up, ffn_w_dw=ffn_w_dw, ffn_b_dw=ffn_b_dw, ffn_w_down=ffn_w_down, final_norm=final_norm)
    moms = dict(norm_mix=m_norm_mix, attn_w_qkv=m_attn_w_qkv, attn_b_qkv=m_attn_b_qkv, attn_sinks=m_attn_sinks, attn_w_o=m_attn_w_o, attn_b_o=m_attn_b_o, conv_w_pw1=m_conv_w_pw1, conv_b_pw1=m_conv_b_pw1, conv_w_dw=m_conv_w_dw, conv_b_dw=m_conv_b_dw, conv_ln_g=m_conv_ln_g, conv_ln_b=m_conv_ln_b, conv_w_pw2=m_conv_w_pw2, conv_b_pw2=m_conv_b_pw2, norm_ffn=m_norm_ffn, ffn_w_up=m_ffn_w_up, ffn_w_dw=m_ffn_w_dw, ffn_b_dw=m_ffn_b_dw, ffn_w_down=m_ffn_w_down, final_norm=m_final_norm)
    vars_ = dict(norm_mix=v_norm_mix, attn_w_qkv=v_attn_w_qkv, attn_b_qkv=v_attn_b_qkv, attn_sinks=v_attn_sinks, attn_w_o=v_attn_w_o, attn_b_o=v_attn_b_o, conv_w_pw1=v_conv_w_pw1, conv_b_pw1=v_conv_b_pw1, conv_w_dw=v_conv_w_dw, conv_b_dw=v_conv_b_dw, conv_ln_g=v_conv_ln_g, conv_ln_b=v_conv_ln_b, conv_w_pw2=v_conv_w_pw2, conv_b_pw2=v_conv_b_pw2, norm_ffn=v_norm_ffn, ffn_w_up=v_ffn_w_up, ffn_w_dw=v_ffn_w_dw, ffn_b_dw=v_ffn_b_dw, ffn_w_down=v_ffn_w_down, final_norm=v_final_norm)
    names = list(weights)
    delta, new_m, new_v = {}, {}, {}
    for nm in names:
        shp = weights[nm].shape
        two_d = (math.prod(shp[:-1]), shp[-1])
        res = _adamw("adamw_" + nm, *(t[nm].reshape(two_d) for t in (weights, grads, moms, vars_)))
        delta[nm], new_m[nm], new_v[nm] = (r.reshape(shp) for r in res)
    return (loss, grad_x.reshape(1, s, D), *[grads[nm] for nm in names], *[delta[nm] for nm in names],
            *[new_m[nm] for nm in names], *[new_v[nm] for nm in names])
```

```python
import functools
import math

import jax
import jax.numpy as jnp
from jax import lax
from jax.experimental import pallas as pl
from jax.experimental.pallas import tpu as pltpu

F32 = jnp.float32
BF16 = jnp.bfloat16

D = 1024
F = 2816
HEAD_DIM = 64
N_KV = 2
BLOCK = 128
NQ = 1024
KVW = 256
CONV_K = 31
RMS_EPS = 1e-6
LN_EPS = 1e-5
ADAM_LR, ADAM_B1, ADAM_B2, ADAM_EPS, ADAM_WD, ADAM_STEP = 0.001, 0.9, 0.999, 1e-08, 0.01, 10
N_DEV = 8
MESH = pl.DeviceIdType.MESH
VMEM_LIMIT_BYTES = 56 * 2**20
MASKED = -1e30

MAT_ROWS = (160, 128, 256, 128, 704, 352, 704, 352)
MAT_OFF = tuple(sum(MAT_ROWS[:i]) for i in range(len(MAT_ROWS)))
SLAB_ROWS = sum(MAT_ROWS)
N_MAT = len(MAT_ROWS)

NT_DIMS = (((1,), (1,)), ((), ()))
NN_DIMS = (((1,), (0,)), ((), ()))
TN_DIMS = (((0,), (0,)), ((), ()))


def _params(*sem):
    return pltpu.CompilerParams(dimension_semantics=tuple(sem), vmem_limit_bytes=VMEM_LIMIT_BYTES)


def _row_tile(rows, cap, mult=8):
    best = None
    for t in range(mult, min(rows, cap) + 1, mult):
        if rows % t == 0:
            best = t
    return best if best is not None else rows


def _sigmoid(x):
    return 1.0 / (1.0 + jnp.exp(-x))


def _mm(name, a, b, *, nt, tm, tn, out_dtype, rms_g=None, bias=None, res=None):
    m, k = a.shape
    n = b.shape[0] if nt else b.shape[1]
    tm = min(tm, m)
    has_rms = rms_g is not None
    dims = NT_DIMS if nt else NN_DIMS

    def body(*refs):
        a_ref, b_ref = refs[0], refs[1]
        pos = 2
        g_ref = bias_ref = res_ref = h_ref = hs_ref = None
        if has_rms:
            g_ref = refs[pos]; pos += 1
        if bias is not None:
            bias_ref = refs[pos]; pos += 1
        if res is not None:
            res_ref = refs[pos]; pos += 1
        o_ref = refs[pos]; pos += 1
        if has_rms:
            h_ref, hs_ref = refs[pos], refs[pos + 1]

            @pl.when(pl.program_id(1) == 0)
            def _():
                xf = a_ref[...]
                r = lax.rsqrt(jnp.mean(xf * xf, axis=-1, keepdims=True) + RMS_EPS)
                h = ((xf * r) * g_ref[...]).astype(BF16)
                hs_ref[...] = h
                h_ref[...] = h

            lhs = hs_ref[...]
        else:
            lhs = a_ref[...].astype(BF16)
        acc = lax.dot_general(lhs, b_ref[...], dims, preferred_element_type=F32)
        if bias is not None:
            acc = acc + bias_ref[...]
        if res is not None:
            acc = acc + res_ref[...]
        o_ref[...] = acc.astype(out_dtype)

    in_specs = [pl.BlockSpec((tm, k), lambda i, j: (i, 0)),
                pl.BlockSpec((tn, k), lambda i, j: (j, 0)) if nt else pl.BlockSpec((k, tn), lambda i, j: (0, j))]
    args = [a, b]
    if has_rms:
        in_specs.append(pl.BlockSpec((1, k), lambda i, j: (0, 0))); args.append(rms_g)
    if bias is not None:
        in_specs.append(pl.BlockSpec((1, tn), lambda i, j: (0, j))); args.append(bias)
    if res is not None:
        in_specs.append(pl.BlockSpec((tm, tn), lambda i, j: (i, j))); args.append(res)
    out_shape = [jax.ShapeDtypeStruct((m, n), out_dtype)]
    out_specs = [pl.BlockSpec((tm, tn), lambda i, j: (i, j))]
    scratch = []
    if has_rms:
        out_shape.append(jax.ShapeDtypeStruct((m, k), BF16))
        out_specs.append(pl.BlockSpec((tm, k), lambda i, j: (i, 0)))
        scratch.append(pltpu.VMEM((tm, k), BF16))
    outs = pl.pallas_call(body, grid=(m // tm, n // tn), in_specs=in_specs, out_specs=out_specs, out_shape=out_shape,
                          scratch_shapes=scratch, compiler_params=_params("parallel", "arbitrary"), name=name)(*args)
    return outs if has_rms else outs[0]


def _mm_tn(name, l, r, *, tn, tt, colsum_l=False, colsum_r=False):
    s, nl = l.shape
    nr = r.shape[1]
    tt = min(tt, s)
    n_t = s // tt

    def body(*refs):
        l_ref, r_ref, o_ref = refs[0], refs[1], refs[2]
        pos = 3
        cl_ref = cr_ref = None
        if colsum_l:
            cl_ref = refs[pos]; pos += 1
        if colsum_r:
            cr_ref = refs[pos]; pos += 1
        acc_ref = refs[pos]
        j, t = pl.program_id(0), pl.program_id(1)

        @pl.when(t == 0)
        def _():
            acc_ref[...] = jnp.zeros_like(acc_ref)

        lv = l_ref[...]
        rv = r_ref[...]
        acc_ref[...] += lax.dot_general(lv, rv.astype(BF16), TN_DIMS, preferred_element_type=F32)
        if colsum_l:
            @pl.when(t == 0)
            def _():
                cl_ref[...] = jnp.zeros_like(cl_ref)

            cl_ref[...] += jnp.sum(lv.astype(F32), axis=0, keepdims=True)
        if colsum_r:
            @pl.when((t == 0) & (j == 0))
            def _():
                cr_ref[...] = jnp.zeros_like(cr_ref)

            @pl.when(j == 0)
            def _():
                cr_ref[...] += jnp.sum(rv.astype(F32), axis=0, keepdims=True)

        @pl.when(t == n_t - 1)
        def _():
            o_ref[...] = acc_ref[...].astype(BF16)

    out_shape = [jax.ShapeDtypeStruct((nl, nr), BF16)]
    out_specs = [pl.BlockSpec((tn, nr), lambda j, t: (j, 0))]
    if colsum_l:
        out_shape.append(jax.ShapeDtypeStruct((1, nl), F32))
        out_specs.append(pl.BlockSpec((1, tn), lambda j, t: (0, j)))
    if colsum_r:
        out_shape.append(jax.ShapeDtypeStruct((1, nr), F32))
        out_specs.append(pl.BlockSpec((1, nr), lambda j, t: (0, 0)))
    outs = pl.pallas_call(
        body, grid=(nl // tn, n_t),
        in_specs=[pl.BlockSpec((tt, tn), lambda j, t: (t, j)), pl.BlockSpec((tt, nr), lambda j, t: (t, 0))],
        out_specs=out_specs, out_shape=out_shape, scratch_shapes=[pltpu.VMEM((tn, nr), F32)],
        compiler_params=_params("arbitrary", "arbitrary"), name=name)(l, r)
    return outs if (colsum_l or colsum_r) else outs[0]


def _rms_bwd(dh, xf, g):
    r = lax.rsqrt(jnp.mean(xf * xf, axis=-1, keepdims=True) + RMS_EPS)
    gd = dh * g
    dx = r * gd - xf * ((r * r * r) * jnp.mean(xf * gd, axis=-1, keepdims=True))
    return dx, dh * (xf * r)


def _dgrad_rms(name, dy, w, x, g, dres, *, tm, tk):
    m, k = dy.shape
    tm = min(tm, m)
    n_k = k // tk

    def body(dy_ref, w_ref, x_ref, g_ref, dres_ref, o_ref, dg_ref, acc_ref):
        i, kk = pl.program_id(0), pl.program_id(1)

        @pl.when(kk == 0)
        def _():
            acc_ref[...] = jnp.zeros_like(acc_ref)

        acc_ref[...] += lax.dot_general(dy_ref[...], w_ref[...], NN_DIMS, preferred_element_type=F32)

        @pl.when((i == 0) & (kk == n_k - 1))
        def _():
            dg_ref[...] = jnp.zeros_like(dg_ref)

        @pl.when(kk == n_k - 1)
        def _():
            dx, dg_rows = _rms_bwd(acc_ref[...], x_ref[...], g_ref[...])
            o_ref[...] = dres_ref[...] + dx
            dg_ref[...] += jnp.sum(dg_rows, axis=0, keepdims=True)

    return pl.pallas_call(
        body, grid=(m // tm, n_k),
        in_specs=[pl.BlockSpec((tm, tk), lambda i, kk: (i, kk)), pl.BlockSpec((tk, D), lambda i, kk: (kk, 0)),
                  pl.BlockSpec((tm, D), lambda i, kk: (i, 0)), pl.BlockSpec((1, D), lambda i, kk: (0, 0)),
                  pl.BlockSpec((tm, D), lambda i, kk: (i, 0))],
        out_specs=[pl.BlockSpec((tm, D), lambda i, kk: (i, 0)), pl.BlockSpec((1, D), lambda i, kk: (0, 0))],
        out_shape=[jax.ShapeDtypeStruct((m, D), F32), jax.ShapeDtypeStruct((1, D), F32)],
        scratch_shapes=[pltpu.VMEM((tm, D), F32)],
        compiler_params=_params("arbitrary", "arbitrary"), name=name)(dy, w, x, g, dres)


def _band(kvp, kvc):
    kk = jnp.concatenate([kvp[:, :128], kvc[:, :128]], axis=0)
    vv = jnp.concatenate([kvp[:, 128:], kvc[:, 128:]], axis=0)
    row = lax.broadcasted_iota(jnp.int32, kk.shape, 0)
    return jnp.where(row == 0, jnp.zeros_like(kk), kk), jnp.where(row == 0, jnp.zeros_like(vv), vv)


def _blockdiag(t, h):
    lane = lax.broadcasted_iota(jnp.int32, t.shape, 1)
    z = jnp.zeros_like(t)
    tr = pltpu.roll(t, 64, 1)
    if h == 0:
        top, bot = jnp.where(lane < 64, t, z), jnp.where(lane >= 64, tr, z)
    else:
        top, bot = jnp.where(lane < 64, tr, z), jnp.where(lane >= 64, t, z)
    return jnp.concatenate([top, bot], axis=0)


def _from_blockdiag(t, h):
    lane = lax.broadcasted_iota(jnp.int32, (256, 128), 1)
    top, bot = t[:256], t[256:]
    if h == 0:
        return jnp.where(lane < 64, top + pltpu.roll(bot, 64, 1), 0.0)
    return jnp.where(lane >= 64, pltpu.roll(top, 64, 1) + bot, 0.0)


def _stack_heads(ref, h):
    return jnp.concatenate([ref[:, h * 512 + p * 128: h * 512 + (p + 1) * 128] for p in range(4)], axis=0)


def _score_bias(sinks):
    row = jnp.arange(512)[:, None] & 127
    col = jnp.arange(256)[None, :]
    band = (col > row) & (col <= row + BLOCK)
    valid = jnp.stack([band & (col >= BLOCK), band])
    sink_rows = jnp.repeat(jnp.transpose(sinks.reshape(N_KV, 4, 2), (0, 2, 1)), BLOCK, axis=-1).reshape(4, 512, 1)
    bias = jnp.where(valid, 0.0, MASKED)[:, None] + jnp.zeros((1, 4, 1, 1), F32)
    return jnp.where(col == 0, sink_rows[None], bias).astype(F32)


def _half_selector(shape, split):
    row = lax.broadcasted_iota(jnp.int32, shape, 0)
    one, zero = jnp.ones(shape, BF16), jnp.zeros(shape, BF16)
    return jnp.where(row < split, one, zero), jnp.where(row >= split, one, zero)


def _attn_fwd(qkv, bias, ex=None, ex_args=()):
    s = qkv.shape[0]
    nb = s // BLOCK
    scale = 1.0 / math.sqrt(HEAD_DIM)

    def body(q_ref, kvc_ref, kvp_ref, sk_ref, o_ref, lse_ref):
        kk, vv = _band(kvp_ref[...], kvc_ref[...])
        sel0, sel1 = _half_selector((512, 128), 256)
        lane = lax.broadcasted_iota(jnp.int32, (512, 128), 1)
        for h in range(N_KV):
            kbd, vbd = _blockdiag(kk, h), _blockdiag(vv, h)
            sc = lax.dot_general(_stack_heads(q_ref, h), kbd, NT_DIMS, preferred_element_type=F32) * scale
            pes, mxs = [], []
            for half in range(2):
                sh = sc[:, half * 256:(half + 1) * 256] + sk_ref[h * 2 + half]
                mx = jnp.max(sh, axis=-1, keepdims=True)
                pes.append(jnp.exp(sh - mx).astype(BF16))
                mxs.append(mx)
            pb = jnp.concatenate(pes, axis=1)
            o_un = lax.dot_general(pb, vbd, NN_DIMS, preferred_element_type=F32)
            sums = [lax.dot_general(pb, sel, NN_DIMS, preferred_element_type=F32) for sel in (sel0, sel1)]
            o = o_un * jnp.where(lane < 64, 1.0 / sums[0], 1.0 / sums[1])
            for half in range(2):
                lse_ref[h * 2 + half] = mxs[half] + jnp.log(sums[half])
            for p in range(4):
                o_ref[:, h * 512 + p * 128: h * 512 + (p + 1) * 128] = o[p * 128:(p + 1) * 128].astype(BF16)

    return pl.pallas_call(
        _carried(body, 4, 2, ex, nb), grid=(nb,),
        in_specs=[pl.BlockSpec((BLOCK, NQ), lambda n: (n, 0)),
                  pl.BlockSpec((BLOCK, KVW), lambda n: (n, NQ // KVW)),
                  pl.BlockSpec((BLOCK, KVW), lambda n: (jnp.maximum(n - 1, 0), NQ // KVW)),
                  pl.BlockSpec((None, 4, 512, 256), lambda n: (jnp.minimum(n, 1), 0, 0, 0))] + _ex(ex, "in_specs"),
        out_specs=[pl.BlockSpec((BLOCK, NQ), lambda n: (n, 0)),
                   pl.BlockSpec((None, 4, 512, 128), lambda n: (n, 0, 0, 0))] + _ex(ex, "out_specs"),
        out_shape=[jax.ShapeDtypeStruct((s, NQ), BF16), jax.ShapeDtypeStruct((nb, 4, 512, 128), F32)] + _ex(ex, "out_shape"),
        scratch_shapes=_ex(ex, "scratch"),
        compiler_params=_params("arbitrary"), name="attn_fwd")(qkv, qkv, qkv, bias, *ex_args)


def _attn_bwd(qkv, do, o, lse, bias, ex=None, ex_args=()):
    s = qkv.shape[0]
    nb = s // BLOCK
    scale = 1.0 / math.sqrt(HEAD_DIM)

    def body(q_ref, kvc_ref, kvp_ref, do_ref, o_ref, lse_ref, sk_ref, dq_ref, dkv_ref, dsk_ref, ck_ref, cv_ref):
        n = pl.program_id(0)

        @pl.when(n == 0)
        def _():
            dsk_ref[...] = jnp.zeros_like(dsk_ref)
            ck_ref[...] = jnp.zeros_like(ck_ref)
            cv_ref[...] = jnp.zeros_like(cv_ref)

        @pl.when(n < nb)
        def _():
            kk, vv = _band(kvp_ref[...], kvc_ref[...])
            sel0, sel1 = _half_selector((128, 128), 64)
            dkk = jnp.zeros((256, 128), F32)
            dvv = jnp.zeros((256, 128), F32)
            for h in range(N_KV):
                kbd, vbd = _blockdiag(kk, h), _blockdiag(vv, h)
                qp = _stack_heads(q_ref, h)
                dop = _stack_heads(do_ref, h)
                sc = lax.dot_general(qp, kbd, NT_DIMS, preferred_element_type=F32) * scale
                dp = lax.dot_general(dop, vbd, NT_DIMS, preferred_element_type=F32)
                dd = (dop.astype(F32) * _stack_heads(o_ref, h).astype(F32)).astype(BF16)
                probs, dss = [], []
                for half, sel in enumerate((sel0, sel1)):
                    delta = lax.dot_general(dd, sel, NN_DIMS, preferred_element_type=F32)
                    lse_h = lse_ref[h * 2 + half]
                    sh = sc[:, half * 256:(half + 1) * 256] + sk_ref[h * 2 + half]
                    pr = jnp.exp(sh - jnp.concatenate([lse_h, lse_h], axis=1))
                    dsf = pr * (dp[:, half * 256:(half + 1) * 256] - jnp.concatenate([delta, delta], axis=1))
                    dsk_ref[h * 2 + half] += dsf[:, 0:1]
                    dss.append((dsf * scale).astype(BF16))
                    probs.append(pr.astype(BF16))
                ds = jnp.concatenate(dss, axis=1)
                pb = jnp.concatenate(probs, axis=1)
                dqp = lax.dot_general(ds, kbd, NN_DIMS, preferred_element_type=F32)
                for p in range(4):
                    dq_ref[:, h * 512 + p * 128: h * 512 + (p + 1) * 128] = dqp[p * 128:(p + 1) * 128].astype(BF16)
                dkk = dkk + _from_blockdiag(lax.dot_general(ds, qp, TN_DIMS, preferred_element_type=F32), h)
                dvv = dvv + _from_blockdiag(lax.dot_general(pb, dop, TN_DIMS, preferred_element_type=F32), h)
            row = lax.broadcasted_iota(jnp.int32, (256, 128), 0)
            dkk = jnp.where(row == 0, 0.0, dkk)
            dvv = jnp.where(row == 0, 0.0, dvv)

            @pl.when(n >= 1)
            def _():
                dkv_ref[:, :128] = (ck_ref[...] + dkk[:128]).astype(BF16)
                dkv_ref[:, 128:] = (cv_ref[...] + dvv[:128]).astype(BF16)

            ck_ref[...] = dkk[128:]
            cv_ref[...] = dvv[128:]

        @pl.when(n == nb)
        def _():
            dkv_ref[:, :128] = ck_ref[...].astype(BF16)
            dkv_ref[:, 128:] = cv_ref[...].astype(BF16)

    last = nb - 1
    blk = lambda n: (jnp.minimum(n, last), 0)
    return pl.pallas_call(
        _carried(body, 7, 3, ex, nb + 1), grid=(nb + 1,),
        in_specs=[pl.BlockSpec((BLOCK, NQ), blk),
                  pl.BlockSpec((BLOCK, KVW), lambda n: (jnp.minimum(n, last), NQ // KVW)),
                  pl.BlockSpec((BLOCK, KVW), lambda n: (jnp.clip(n - 1, 0, last), NQ // KVW)),
                  pl.BlockSpec((BLOCK, NQ), blk), pl.BlockSpec((BLOCK, NQ), blk),
                  pl.BlockSpec((None, 4, 512, 128), lambda n: (jnp.minimum(n, last), 0, 0, 0)),
                  pl.BlockSpec((None, 4, 512, 256), lambda n: (jnp.minimum(n, 1), 0, 0, 0))] + _ex(ex, "in_specs"),
        out_specs=[pl.BlockSpec((BLOCK, NQ), blk),
                   pl.BlockSpec((BLOCK, KVW), lambda n: (jnp.maximum(n - 1, 0), 0)),
                   pl.BlockSpec((4, 512, 1), lambda n: (0, 0, 0))] + _ex(ex, "out_specs"),
        out_shape=[jax.ShapeDtypeStruct((s, NQ), BF16), jax.ShapeDtypeStruct((s, KVW), BF16),
                   jax.ShapeDtypeStruct((4, 512, 1), F32)] + _ex(ex, "out_shape"),
        scratch_shapes=[pltpu.VMEM((BLOCK, 128), F32), pltpu.VMEM((BLOCK, 128), F32)] + _ex(ex, "scratch"),
        compiler_params=_params("arbitrary"), name="attn_bwd")(qkv, qkv, qkv, do, o, lse, bias, *ex_args)


LANE = 128
FFN_HALF = F // 2
FFN_PAIR = 2 * LANE
FFN_PAIRS = F // FFN_PAIR
FFN_GROUPS = ((0, 2), (2, 4), (4, 6), (6, 8), (8, 10), (10, 11))
FFN_HALO = 8


def _resident(shape):
    return pl.BlockSpec(shape, lambda i: (0,) * len(shape), pipeline_mode=pl.Buffered(1))


def _ffn_fwd(tag, x_in, g, w_up_t, w_dw, b_dw, w_down, *, tm, pre=None, head=None, ex=None, ex_args=()):
    s = x_in.shape[0]
    tm = min(tm, s)
    n_pre = 3 if pre is not None else 0
    n_head = 2 if head is not None else 0

    def body(*refs):
        x_ref, g_ref, wup_ref, wdw_ref, bdw_ref, wd_ref = refs[:6]
        pre_refs = refs[6:6 + n_pre]
        head_refs = refs[6 + n_pre:6 + n_pre + n_head]
        p = 6 + n_pre + n_head
        o_ref, h_ref, up_ref, act_ref, gate_ref = refs[p:p + 5]
        p += 5
        if pre is not None:
            xmid_ref, p = refs[p], p + 1
        if head is not None:
            loss_ref, dgf_ref, p = refs[p], refs[p + 1], p + 2
        carry_ref, ext_ref = refs[p], refs[p + 1]

        @pl.when(pl.program_id(0) == 0)
        def _():
            carry_ref[...] = jnp.zeros_like(carry_ref)
            if head is not None:
                loss_ref[...] = jnp.zeros_like(loss_ref)
                dgf_ref[...] = jnp.zeros_like(dgf_ref)

        xf = x_ref[...]
        if pre is not None:
            oin_ref, wo_ref, bo_ref = pre_refs
            xf = xf + lax.dot_general(oin_ref[...], wo_ref[...], NN_DIMS, preferred_element_type=F32) + bo_ref[...]
            xmid_ref[...] = xf
        r = lax.rsqrt(jnp.mean(xf * xf, axis=-1, keepdims=True) + RMS_EPS)
        h = ((xf * r) * g_ref[...]).astype(BF16)
        h_ref[...] = h

        def up_mm(p):
            for off in (p * FFN_PAIR, F + p * FFN_PAIR):
                rs = slice(off, off + FFN_PAIR)
                up_ref[:, rs] = lax.dot_general(h, wup_ref[rs, :], NT_DIMS, preferred_element_type=F32).astype(BF16)

        def elementwise(p):
            for c in (2 * p, 2 * p + 1):
                cs = slice(c * LANE, (c + 1) * LANE)
                vs = slice(F + c * LANE, F + (c + 1) * LANE)
                gcol = up_ref[:, cs].astype(F32)
                ext_ref[pl.ds(0, FFN_HALO), :] = carry_ref[:, cs]
                ext_ref[pl.ds(FFN_HALO, tm), :] = gcol
                gate_b = (wdw_ref[2:3, cs] * gcol + wdw_ref[1:2, cs] * ext_ref[pl.ds(FFN_HALO - 1, tm), :]
                          + wdw_ref[0:1, cs] * ext_ref[pl.ds(FFN_HALO - 2, tm), :] + bdw_ref[:, cs]).astype(BF16)
                gate_ref[:, cs] = gate_b
                gate = gate_b.astype(F32)
                act_ref[:, cs] = (gate * _sigmoid(gate) * up_ref[:, vs].astype(F32)).astype(BF16)
                carry_ref[:, cs] = gcol[tm - FFN_HALO:]

        def down_mm(lo, hi):
            ks = slice(lo * FFN_PAIR, hi * FFN_PAIR)
            return lax.dot_general(act_ref[:, ks], wd_ref[ks, :], NN_DIMS, preferred_element_type=F32)

        acc = xf
        up_mm(0)
        pending = None
        for lo, hi in FFN_GROUPS:
            for p in range(lo, hi):
                if p + 1 < FFN_PAIRS:
                    up_mm(p + 1)
                if pending is not None:
                    acc = acc + down_mm(*pending)
                    pending = None
                elementwise(p)
            pending = (lo, hi)
        x_out = acc + down_mm(*pending)
        if head is None:
            o_ref[...] = x_out
        else:
            gf_ref, tgt_ref = head_refs
            gf = gf_ref[...]
            rf = lax.rsqrt(jnp.mean(x_out * x_out, axis=-1, keepdims=True) + RMS_EPS)
            diff = (x_out * rf) * gf - tgt_ref[...]
            loss_ref[...] += 0.5 * jnp.sum(jnp.mean(diff * diff, axis=-1, keepdims=True))
            dx, dg_rows = _rms_bwd(diff * (1.0 / D), x_out, gf)
            o_ref[...] = dx
            dgf_ref[...] += jnp.sum(dg_rows, axis=0, keepdims=True)

    row = lambda i: (i, 0)
    const = lambda i: (0, 0)
    in_specs = [pl.BlockSpec((tm, D), row), _resident((1, D)), _resident((2 * F, D)), _resident((3, F)),
                _resident((1, F)), _resident((F, D))]
    out_specs = [pl.BlockSpec((tm, D), row), pl.BlockSpec((tm, D), row), pl.BlockSpec((tm, 2 * F), row),
                 pl.BlockSpec((tm, F), row), pl.BlockSpec((tm, F), row)]
    out_shape = [jax.ShapeDtypeStruct((s, D), F32), jax.ShapeDtypeStruct((s, D), BF16),
                 jax.ShapeDtypeStruct((s, 2 * F), BF16), jax.ShapeDtypeStruct((s, F), BF16),
                 jax.ShapeDtypeStruct((s, F), BF16)]
    args = [x_in, g, w_up_t, w_dw, b_dw, w_down]
    if pre is not None:
        in_specs += [pl.BlockSpec((tm, NQ), row), _resident((NQ, D)), _resident((1, D))]
        out_specs.append(pl.BlockSpec((tm, D), row))
        out_shape.append(jax.ShapeDtypeStruct((s, D), F32))
        args += list(pre)
    if head is not None:
        in_specs += [_resident((1, D)), pl.BlockSpec((tm, D), row)]
        out_specs += [pl.BlockSpec((1, 128), const), pl.BlockSpec((1, D), const)]
        out_shape += [jax.ShapeDtypeStruct((1, 128), F32), jax.ShapeDtypeStruct((1, D), F32)]
        args += list(head)
    return pl.pallas_call(
        _carried(body, len(args), len(out_shape), ex, s // tm), grid=(s // tm,),
        in_specs=in_specs + _ex(ex, "in_specs"), out_specs=out_specs + _ex(ex, "out_specs"),
        out_shape=out_shape + _ex(ex, "out_shape"),
        scratch_shapes=[pltpu.VMEM((FFN_HALO, F), F32), pltpu.VMEM((tm + FFN_HALO, LANE), F32)] + _ex(ex, "scratch"),
        compiler_params=_params("arbitrary"), name=f"ffn{tag}_fwd")(*args, *ex_args)


def _ffn_bwd(tag, dx_out, x_in, g, w_up_t, w_dw, w_down, up, gate, *, tm):
    s = x_in.shape[0]
    tm = min(tm, s)
    n_i = s // tm

    def body(dx_ref, x_ref, g_ref, up_ref, gate_ref, wup_ref, wdw_ref, wd_ref,
             dxin_ref, dup_ref, dg_ref, dwb_ref, carry_ref, da0_ref, da1_ref, extd_ref):
        i = pl.program_id(0)

        @pl.when(i == 0)
        def _():
            carry_ref[...] = jnp.zeros_like(carry_ref)
            dg_ref[...] = jnp.zeros_like(dg_ref)
            dwb_ref[...] = jnp.zeros_like(dwb_ref)

        dxf = dx_ref[...]
        dxb = dxf.astype(BF16)
        da_refs = (da0_ref, da1_ref)

        def da_mm(p):
            ks = slice(p * FFN_PAIR, (p + 1) * FFN_PAIR)
            da_refs[p % 2][...] = lax.dot_general(dxb, wd_ref[ks, :], NT_DIMS, preferred_element_type=F32)

        def elementwise(p):
            da_ref = da_refs[p % 2]
            for c in range(2):
                cc = 2 * p + c
                cs = slice(cc * LANE, (cc + 1) * LANE)
                vs = slice(F + cc * LANE, F + (cc + 1) * LANE)
                gate = gate_ref[:, cs].astype(F32)
                sg = _sigmoid(gate)
                silu = gate * sg
                dac = da_ref[:, c * LANE:(c + 1) * LANE]
                dup_ref[:, vs] = (dac * silu).astype(BF16)
                dgate = dac * up_ref[:, vs].astype(F32) * (sg + silu * (1.0 - sg))
                extd_ref[pl.ds(0, tm), :] = dgate
                extd_ref[pl.ds(tm, FFN_HALO), :] = carry_ref[:, cs]
                d1 = extd_ref[pl.ds(1, tm), :]
                d2 = extd_ref[pl.ds(2, tm), :]
                dup_ref[:, cs] = (wdw_ref[2:3, cs] * dgate + wdw_ref[1:2, cs] * d1 + wdw_ref[0:1, cs] * d2).astype(BF16)
                carry_ref[:, cs] = dgate[:FFN_HALO]
                gcol = up_ref[:, cs].astype(F32)
                dwb_ref[2, :, cs] += _sum8(dgate * gcol)
                dwb_ref[1, :, cs] += _sum8(d1 * gcol)
                dwb_ref[0, :, cs] += _sum8(d2 * gcol)
                dwb_ref[3, :, cs] += _sum8(dgate)

        def dh_mm(lo, hi):
            ks = slice(lo * FFN_PAIR, hi * FFN_PAIR)
            vks = slice(F + lo * FFN_PAIR, F + hi * FFN_PAIR)
            return (lax.dot_general(dup_ref[:, ks], wup_ref[ks, :], NN_DIMS, preferred_element_type=F32)
                    + lax.dot_general(dup_ref[:, vks], wup_ref[vks, :], NN_DIMS, preferred_element_type=F32))

        acc = None
        da_mm(0)
        pending = None
        for lo, hi in FFN_GROUPS:
            for p in range(lo, hi):
                if p + 1 < FFN_PAIRS:
                    da_mm(p + 1)
                if pending is not None:
                    part = dh_mm(*pending)
                    acc = part if acc is None else acc + part
                    pending = None
                elementwise(p)
            pending = (lo, hi)
        acc = acc + dh_mm(*pending)
        dx, dg_rows = _rms_bwd(acc, x_ref[...], g_ref[...])
        dxin_ref[...] = dxf + dx
        dg_ref[...] += jnp.sum(dg_rows, axis=0, keepdims=True)

    rev = lambda i: (n_i - 1 - i, 0)
    return pl.pallas_call(
        body, grid=(n_i,),
        in_specs=[pl.BlockSpec((tm, D), rev), pl.BlockSpec((tm, D), rev), _resident((1, D)),
                  pl.BlockSpec((tm, 2 * F), rev), pl.BlockSpec((tm, F), rev),
                  _resident((2 * F, D)), _resident((3, F)), _resident((F, D))],
        out_specs=[pl.BlockSpec((tm, D), rev), pl.BlockSpec((tm, 2 * F), rev),
                   pl.BlockSpec((1, D), lambda i: (0, 0)), pl.BlockSpec((4, 8, F), lambda i: (0, 0, 0))],
        out_shape=[jax.ShapeDtypeStruct((s, D), F32), jax.ShapeDtypeStruct((s, 2 * F), BF16),
                   jax.ShapeDtypeStruct((1, D), F32), jax.ShapeDtypeStruct((4, 8, F), F32)],
        scratch_shapes=[pltpu.VMEM((FFN_HALO, F), F32), pltpu.VMEM((tm, FFN_PAIR), F32), pltpu.VMEM((tm, FFN_PAIR), F32),
                        pltpu.VMEM((tm + FFN_HALO, LANE), F32)],
        compiler_params=_params("arbitrary"), name=f"ffn{tag}_bwd")(dx_out, x_in, g, up, gate, w_up_t, w_dw, w_down)


CONV_HALO = 32
CONV_CHUNK = 256
CONV_ROWS = 64


def _ln(u2, g, b):
    mu = jnp.mean(u2, axis=-1, keepdims=True)
    xc = u2 - mu
    rstd = lax.rsqrt(jnp.mean(xc * xc, axis=-1, keepdims=True) + LN_EPS)
    xhat = xc * rstd
    return xhat, rstd, xhat * g + b


def _phase_taps(ph):
    first = CONV_HALO - (CONV_K - 1)
    return [(k, first + k - ph) for k in range(CONV_K) if (first + k) % 8 == ph]


def _sum8(v):
    out = v[0:8]
    for j in range(8, v.shape[0], 8):
        out = out + v[j:j + 8]
    return out


def _store_phases(src_ref, sh_ref, tt, cols=slice(None)):
    sh_ref[0] = src_ref[:, cols]
    for ph in range(1, 8):
        sh_ref[ph, pl.ds(0, tt + CONV_HALO - 8), :] = src_ref[pl.ds(ph, tt + CONV_HALO - 8), cols]


def _conv_taps_fwd(w_ref, sh_ref, u2_ref, cs, tt):
    for base in range(0, tt, CONV_ROWS):
        acc = jnp.zeros((CONV_ROWS, CONV_CHUNK), F32) + w_ref[CONV_K:CONV_K + 1, cs]
        for ph in range(8):
            for k, off in _phase_taps(ph):
                acc = acc + w_ref[k:k + 1, cs] * sh_ref[ph, pl.ds(base + off, CONV_ROWS), :]
        u2_ref[pl.ds(base, CONV_ROWS), cs] = acc.astype(BF16)


def _conv_fwd(x_in, g, w1t, b1, wdw, ln_g, ln_b, w2, b2, *, tt):
    s = x_in.shape[0]
    tt = min(tt, s)
    n_c = D // CONV_CHUNK

    def body(x_ref, g_ref, w1_ref, b1_ref, w_ref, lg_ref, lb_ref, w2_ref, b2_ref,
             o_ref, h_ref, a_ref, u2_ref, s_ref, carry_ref, ext_ref, sh_ref):
        @pl.when(pl.program_id(0) == 0)
        def _():
            carry_ref[...] = jnp.zeros_like(carry_ref)

        xf = x_ref[...]
        r = lax.rsqrt(jnp.mean(xf * xf, axis=-1, keepdims=True) + RMS_EPS)
        h = ((xf * r) * g_ref[...]).astype(BF16)
        h_ref[...] = h

        def pw1_mm(c):
            for off in (c * CONV_CHUNK, D + c * CONV_CHUNK):
                rs = slice(off, off + CONV_CHUNK)
                a_ref[:, rs] = (lax.dot_general(h, w1_ref[rs, :], NT_DIMS, preferred_element_type=F32)
                                + b1_ref[:, rs]).astype(BF16)

        pw1_mm(0)
        for c in range(n_c):
            if c + 1 < n_c:
                pw1_mm(c + 1)
            cs = slice(c * CONV_CHUNK, (c + 1) * CONV_CHUNK)
            gs = slice(D + c * CONV_CHUNK, D + (c + 1) * CONV_CHUNK)
            u = a_ref[:, cs].astype(F32) * _sigmoid(a_ref[:, gs].astype(F32))
            ext_ref[pl.ds(0, CONV_HALO), :] = carry_ref[:, cs]
            ext_ref[pl.ds(CONV_HALO, tt), :] = u
            carry_ref[:, cs] = u[tt - CONV_HALO:]
            _store_phases(ext_ref, sh_ref, tt)
            _conv_taps_fwd(w_ref, sh_ref, u2_ref, cs, tt)
        _, _, ln = _ln(u2_ref[...].astype(F32), lg_ref[...], lb_ref[...])
        sv = (ln * _sigmoid(ln)).astype(BF16)
        s_ref[...] = sv
        o_ref[...] = xf + lax.dot_general(sv, w2_ref[...], NN_DIMS, preferred_element_type=F32) + b2_ref[...]

    row = lambda i: (i, 0)
    return pl.pallas_call(
        body, grid=(s // tt,),
        in_specs=[pl.BlockSpec((tt, D), row), _resident((1, D)), _resident((2 * D, D)), _resident((1, 2 * D)),
                  _resident((32, D)), _resident((1, D)), _resident((1, D)), _resident((D, D)), _resident((1, D))],
        out_specs=[pl.BlockSpec((tt, D), row), pl.BlockSpec((tt, D), row), pl.BlockSpec((tt, 2 * D), row),
                   pl.BlockSpec((tt, D), row), pl.BlockSpec((tt, D), row)],
        out_shape=[jax.ShapeDtypeStruct((s, D), F32), jax.ShapeDtypeStruct((s, D), BF16),
                   jax.ShapeDtypeStruct((s, 2 * D), BF16), jax.ShapeDtypeStruct((s, D), BF16),
                   jax.ShapeDtypeStruct((s, D), BF16)],
        scratch_shapes=[pltpu.VMEM((CONV_HALO, D), F32), pltpu.VMEM((tt + CONV_HALO, CONV_CHUNK), F32),
                        pltpu.VMEM((8, tt + CONV_HALO, CONV_CHUNK), F32)],
        compiler_params=_params("arbitrary"), name="conv_fwd")(x_in, g, w1t, b1, wdw, ln_g, ln_b, w2, b2)


def _conv_bwd(dx_out, x_in, g, a, u2, w1t, wdw, ln_g, ln_b, w2, *, tt, ex=None, ex_args=()):
    s = x_in.shape[0]
    tt = min(tt, s)
    hb = tt // CONV_HALO
    n_i = s // tt
    te = tt + CONV_HALO
    n_c = D // CONV_CHUNK

    def body(dx_ref, x_ref, g_ref, a_ref, prev_ref, u2_ref, w1_ref, w_ref, lg_ref, lb_ref, w2_ref,
             dxin_ref, da_ref, dg_ref, acc_ref, db1_ref, db2_ref,
             carry_ref, e2_ref, ext_ref, ush_ref, dsh_ref, du_ref):
        i = pl.program_id(0)

        @pl.when(i == 0)
        def _():
            for ref in (dg_ref, acc_ref, db1_ref, db2_ref, carry_ref):
                ref[...] = jnp.zeros_like(ref)

        dxf = dx_ref[...]
        db2_ref[...] += _sum8(dxf)
        dse = lax.dot_general(dxf.astype(BF16), w2_ref[...], NT_DIMS, preferred_element_type=F32)
        gv = lg_ref[...]
        xhat, rstd, ln = _ln(u2_ref[...].astype(F32), gv, lb_ref[...])
        sg = _sigmoid(ln)
        dln = dse * (sg * (1.0 + ln * (1.0 - sg)))
        acc_ref[32] += _sum8(dln * xhat)
        acc_ref[33] += _sum8(dln)
        dxh = dln * gv
        du2 = rstd * (dxh - jnp.mean(dxh, axis=-1, keepdims=True) - xhat * jnp.mean(dxh * xhat, axis=-1, keepdims=True))
        e2_ref[pl.ds(0, tt), :] = du2
        e2_ref[pl.ds(tt, CONV_HALO), :] = carry_ref[...]
        carry_ref[...] = du2[:CONV_HALO]
        first_tile = i == n_i - 1

        def dh_mm(c):
            cs = slice(c * CONV_CHUNK, (c + 1) * CONV_CHUNK)
            gs = slice(D + c * CONV_CHUNK, D + (c + 1) * CONV_CHUNK)
            return (lax.dot_general(da_ref[:, cs], w1_ref[cs, :], NN_DIMS, preferred_element_type=F32)
                    + lax.dot_general(da_ref[:, gs], w1_ref[gs, :], NN_DIMS, preferred_element_type=F32))

        dh = None
        for c in range(n_c):
            if c >= 1:
                part = dh_mm(c - 1)
                dh = part if dh is None else dh + part
            cs = slice(c * CONV_CHUNK, (c + 1) * CONV_CHUNK)
            gs = slice(D + c * CONV_CHUNK, D + (c + 1) * CONV_CHUNK)
            uh = prev_ref[:, cs].astype(F32) * _sigmoid(prev_ref[:, gs].astype(F32))
            ext_ref[pl.ds(0, CONV_HALO), :] = jnp.where(first_tile, 0.0, uh)
            a1 = a_ref[:, cs].astype(F32)
            sg2 = _sigmoid(a_ref[:, gs].astype(F32))
            ext_ref[pl.ds(CONV_HALO, tt), :] = a1 * sg2
            _store_phases(ext_ref, ush_ref, tt)
            _store_phases(e2_ref, dsh_ref, tt, cs)
            for base in range(0, tt, CONV_ROWS):
                d0 = e2_ref[pl.ds(base, CONV_ROWS), cs]
                du = jnp.zeros((CONV_ROWS, CONV_CHUNK), F32)
                for ph in range(8):
                    for k, off in _phase_taps(ph):
                        acc_ref[k, :, cs] += _sum8(d0 * ush_ref[ph, pl.ds(base + off, CONV_ROWS), :])
                    for k in range(CONV_K):
                        if (CONV_K - 1 - k) % 8 == ph:
                            du = du + w_ref[k:k + 1, cs] * dsh_ref[ph, pl.ds(base + CONV_K - 1 - k - ph, CONV_ROWS), :]
                acc_ref[CONV_K, :, cs] += _sum8(d0)
                du_ref[pl.ds(base, CONV_ROWS), :] = du
            du = du_ref[...]
            da1 = du * sg2
            da2 = du * a1 * (sg2 * (1.0 - sg2))
            da_ref[:, cs] = da1.astype(BF16)
            da_ref[:, gs] = da2.astype(BF16)
            db1_ref[:, cs] += _sum8(da1)
            db1_ref[:, gs] += _sum8(da2)
        dh = dh + dh_mm(n_c - 1)
        dx, dg_rows = _rms_bwd(dh, x_ref[...], g_ref[...])
        dxin_ref[...] = dxf + dx
        dg_ref[...] += jnp.sum(dg_rows, axis=0, keepdims=True)

    rev = lambda i: (n_i - 1 - i, 0)
    return pl.pallas_call(
        _carried(body, 11, 6, ex, n_i), grid=(n_i,),
        in_specs=[pl.BlockSpec((tt, D), rev), pl.BlockSpec((tt, D), rev), _resident((1, D)),
                  pl.BlockSpec((tt, 2 * D), rev),
                  pl.BlockSpec((CONV_HALO, 2 * D), lambda i: (jnp.maximum((n_i - 1 - i) * hb - 1, 0), 0)),
                  pl.BlockSpec((tt, D), rev), _resident((2 * D, D)), _resident((32, D)), _resident((1, D)),
                  _resident((1, D)), _resident((D, D))] + _ex(ex, "in_specs"),
        out_specs=[pl.BlockSpec((tt, D), rev), pl.BlockSpec((tt, 2 * D), rev), pl.BlockSpec((1, D), lambda i: (0, 0)),
                   pl.BlockSpec((40, 8, D), lambda i: (0, 0, 0)), pl.BlockSpec((8, 2 * D), lambda i: (0, 0)),
                   pl.BlockSpec((8, D), lambda i: (0, 0))] + _ex(ex, "out_specs"),
        out_shape=[jax.ShapeDtypeStruct((s, D), F32), jax.ShapeDtypeStruct((s, 2 * D), BF16),
                   jax.ShapeDtypeStruct((1, D), F32), jax.ShapeDtypeStruct((40, 8, D), F32),
                   jax.ShapeDtypeStruct((8, 2 * D), F32), jax.ShapeDtypeStruct((8, D), F32)] + _ex(ex, "out_shape"),
        scratch_shapes=[pltpu.VMEM((CONV_HALO, D), F32), pltpu.VMEM((te, D), F32), pltpu.VMEM((te, CONV_CHUNK), F32),
                        pltpu.VMEM((8, te, CONV_CHUNK), F32), pltpu.VMEM((8, te, CONV_CHUNK), F32),
                        pltpu.VMEM((tt, CONV_CHUNK), F32)] + _ex(ex, "scratch"),
        compiler_params=_params("arbitrary"), name="conv_bwd")(dx_out, x_in, g, a, a, u2, w1t, wdw, ln_g, ln_b, w2, *ex_args)


def _adamw(name, w, g, m, v):
    rows, cols = w.shape
    tr = _row_tile(rows, 256)

    def body(w_ref, g_ref, m_ref, v_ref, d_ref, nm_ref, nv_ref):
        gv = g_ref[...]
        m2 = ADAM_B1 * m_ref[...] + (1.0 - ADAM_B1) * gv
        v2 = ADAM_B2 * v_ref[...] + (1.0 - ADAM_B2) * (gv * gv)
        m_hat = m2 / (1.0 - ADAM_B1 ** ADAM_STEP)
        v_hat = v2 / (1.0 - ADAM_B2 ** ADAM_STEP)
        d_ref[...] = -ADAM_LR * (m_hat / (jnp.sqrt(v_hat) + ADAM_EPS) + ADAM_WD * w_ref[...])
        nm_ref[...] = m2
        nv_ref[...] = v2

    spec = pl.BlockSpec((tr, cols), lambda i: (i, 0))
    return pl.pallas_call(
        body, grid=(rows // tr,), in_specs=[spec] * 4, out_specs=[spec] * 3,
        out_shape=[jax.ShapeDtypeStruct((rows, cols), F32)] * 3,
        compiler_params=_params("parallel"), name=name)(w, g, m, v)


def _sum_slabs(name, buf):
    rows = buf.shape[1]
    tr = _row_tile(rows, 512, 16)

    def body(b_ref, o_ref):
        acc = b_ref[0].astype(F32)
        for k in range(1, N_DEV):
            acc = acc + b_ref[k].astype(F32)
        o_ref[...] = acc

    return pl.pallas_call(
        body, grid=(rows // tr,), in_specs=[pl.BlockSpec((N_DEV, tr, D), lambda i: (0, i, 0))],
        out_specs=pl.BlockSpec((tr, D), lambda i: (i, 0)), out_shape=jax.ShapeDtypeStruct((rows, D), F32),
        compiler_params=_params("parallel"), name=name)(buf)


def _ex(ex, field):
    return list(getattr(ex, field)) if ex is not None else []


def _me():
    x, y, c = lax.axis_index("x"), lax.axis_index("y"), lax.axis_index("c")
    return x, y, c, 4 * x + 2 * y + c


def _peer(x, y, c, k):
    return (x ^ (k >> 2), y ^ ((k >> 1) & 1), c ^ (k & 1))


class _Gather:
    def __init__(self, mats, slab_base=0):
        self.mats = tuple(mats)
        self.slab_base = slab_base
        self.rows = sum(MAT_ROWS[i] for i in self.mats)
        any_spec = pl.BlockSpec(memory_space=pl.ANY)
        self.n_in, self.n_out = 1, len(self.mats)
        self.in_specs = [any_spec]
        self.out_specs = [any_spec] * self.n_out
        self.out_shape = [jax.ShapeDtypeStruct((N_DEV * MAT_ROWS[i], D), BF16) for i in self.mats]
        self.scratch = [pltpu.SemaphoreType.DMA((N_DEV - 1,)), pltpu.SemaphoreType.DMA((N_DEV - 1,)),
                        pltpu.SemaphoreType.DMA]
        assert max(s.shape[0] for s in self.out_shape) >= self.rows

    def start(self, ins, outs, scr):
        (slab_ref,), (send_sems, recv_sems, local_sem) = ins, scr
        x, y, c, me = _me()
        for n, i in enumerate(self.mats):
            src = slab_ref.at[pl.ds(MAT_OFF[i] - self.slab_base, MAT_ROWS[i])]
            dst = outs[n].at[pl.ds(me * MAT_ROWS[i], MAT_ROWS[i])]
            pltpu.make_async_copy(src, dst, local_sem).start()
            for k in range(1, N_DEV):
                pltpu.make_async_remote_copy(src_ref=src, dst_ref=dst, send_sem=send_sems.at[k - 1],
                                             recv_sem=recv_sems.at[k - 1], device_id=_peer(x, y, c, k),
                                             device_id_type=MESH).start()

    def wait(self, ins, outs, scr):
        send_sems, recv_sems, local_sem = scr
        x, y, c, _ = _me()
        big = max(range(self.n_out), key=lambda n: self.out_shape[n].shape[0])
        whole = outs[big].at[pl.ds(0, self.rows)]
        for k in range(1, N_DEV):
            pltpu.make_async_remote_copy(src_ref=whole, dst_ref=whole, send_sem=send_sems.at[k - 1],
                                         recv_sem=recv_sems.at[k - 1], device_id=_peer(x, y, c, k),
                                         device_id_type=MESH).wait()
        pltpu.make_async_copy(whole, whole, local_sem).wait()


class _Scatter:
    def __init__(self, mats):
        self.mats = tuple(mats)
        self.rows = sum(MAT_ROWS[i] for i in self.mats)
        any_spec = pl.BlockSpec(memory_space=pl.ANY)
        self.n_in, self.n_out = len(self.mats), 1
        self.in_specs = [any_spec] * self.n_in
        self.out_specs = [any_spec]
        self.out_shape = [jax.ShapeDtypeStruct((N_DEV, self.rows, D), BF16)]
        self.scratch = [pltpu.SemaphoreType.DMA((N_DEV - 1,)), pltpu.SemaphoreType.DMA((N_DEV - 1,)),
                        pltpu.SemaphoreType.DMA]

    def offsets(self):
        return [sum(MAT_ROWS[i] for i in self.mats[:n]) for n in range(len(self.mats))]

    def start(self, ins, outs, scr):
        (buf_ref,), (send_sems, recv_sems, local_sem) = outs, scr
        x, y, c, me = _me()
        for n, (i, off) in enumerate(zip(self.mats, self.offsets())):
            r = MAT_ROWS[i]
            pltpu.make_async_copy(ins[n].at[pl.ds(me * r, r)], buf_ref.at[0, pl.ds(off, r)], local_sem).start()
            for k in range(1, N_DEV):
                pltpu.make_async_remote_copy(src_ref=ins[n].at[pl.ds((me ^ k) * r, r)], dst_ref=buf_ref.at[k, pl.ds(off, r)],
                                             send_sem=send_sems.at[k - 1], recv_sem=recv_sems.at[k - 1],
                                             device_id=_peer(x, y, c, k), device_id_type=MESH).start()

    def wait(self, ins, outs, scr):
        (buf_ref,), (send_sems, recv_sems, local_sem) = outs, scr
        x, y, c, _ = _me()
        whole = buf_ref.at[0]
        for k in range(1, N_DEV):
            pltpu.make_async_remote_copy(src_ref=whole, dst_ref=whole, send_sem=send_sems.at[k - 1],
                                         recv_sem=recv_sems.at[k - 1], device_id=_peer(x, y, c, k),
                                         device_id_type=MESH).wait()
        pltpu.make_async_copy(whole, whole, local_sem).wait()


def _carried(body, n_in, n_out, ex, n_steps):
    if ex is None:
        return body

    def wrapped(*refs):
        ins, ex_ins = refs[:n_in], refs[n_in:n_in + ex.n_in]
        p = n_in + ex.n_in
        outs, ex_outs = refs[p:p + n_out], refs[p + n_out:p + n_out + ex.n_out]
        p += n_out + ex.n_out
        n_scr = len(refs) - p - len(ex.scratch)
        scr, ex_scr = refs[p:p + n_scr], refs[p + n_scr:]

        @pl.when(pl.program_id(0) == 0)
        def _():
            ex.start(ex_ins, ex_outs, ex_scr)

        body(*ins, *outs, *scr)

        @pl.when(pl.program_id(0) == n_steps - 1)
        def _():
            ex.wait(ex_ins, ex_outs, ex_scr)

    return wrapped


def _exchange_small(name, ex, ex_args, small, reduce):
    vmem_spec = pl.BlockSpec(memory_space=pltpu.VMEM)
    n_small = small.shape[0]

    def body(*refs):
        ex_ins, small_ref = refs[:ex.n_in], refs[ex.n_in]
        p = ex.n_in + 1
        ex_outs, res_ref = refs[p:p + ex.n_out], refs[p + ex.n_out]
        p += ex.n_out + 1
        if reduce:
            all_ref, p = refs[p], p + 1
        else:
            all_ref = res_ref
        sm_send, sm_recv = refs[p], refs[p + 1]
        ex_scr = refs[p + 2:]
        x, y, c, me = _me()
        ex.start(ex_ins, ex_outs, ex_scr)
        copies = [pltpu.make_async_remote_copy(
            src_ref=small_ref, dst_ref=all_ref.at[me], send_sem=sm_send.at[k - 1], recv_sem=sm_recv.at[k - 1],
            device_id=_peer(x, y, c, k), device_id_type=MESH) for k in range(1, N_DEV)]
        for cp in copies:
            cp.start()
        all_ref[me] = small_ref[...]
        for cp in copies:
            cp.wait()
        if reduce:
            acc = all_ref[0]
            for d in range(1, N_DEV):
                acc = acc + all_ref[d]
            res_ref[...] = acc
        ex.wait(ex_ins, ex_outs, ex_scr)

    res_shape = (n_small, D) if reduce else (N_DEV, n_small, D)
    scratch = ([pltpu.VMEM((N_DEV, n_small, D), F32)] if reduce else []) + [
        pltpu.SemaphoreType.DMA((N_DEV - 1,)), pltpu.SemaphoreType.DMA((N_DEV - 1,))] + ex.scratch
    outs = pl.pallas_call(
        body, in_specs=ex.in_specs + [vmem_spec], out_specs=ex.out_specs + [vmem_spec],
        out_shape=ex.out_shape + [jax.ShapeDtypeStruct(res_shape, F32)], scratch_shapes=scratch,
        compiler_params=pltpu.CompilerParams(vmem_limit_bytes=VMEM_LIMIT_BYTES), name=name)(*ex_args, small)
    return outs


def _pack_rows(arrs):
    rows = []
    for a in arrs:
        flat = a.reshape(-1).astype(F32)
        pad = (-flat.shape[0]) % D
        rows.append(jnp.pad(flat, (0, pad)).reshape(-1, D))
    slab = jnp.concatenate(rows, axis=0)
    return jnp.pad(slab, ((0, (-slab.shape[0]) % 8), (0, 0)))


def _unpack_rows(slab, shapes):
    out, r = [], 0
    for shp in shapes:
        size = math.prod(shp)
        nrows = -(-size // D)
        out.append(slab[r:r + nrows].reshape(-1)[:size].reshape(shp))
        r += nrows
    return out


def _unpack_gathered(slabs, shapes):
    out, r = [], 0
    for shp in shapes:
        size = math.prod(shp)
        nrows = -(-size // D)
        blocks = slabs[:, r:r + nrows].reshape(N_DEV, -1)[:, :size].reshape((N_DEV,) + tuple(shp))
        out.append(jnp.moveaxis(blocks, 0, -2).reshape(tuple(shp[:-1]) + (N_DEV * shp[-1],)))
        r += nrows
    return out


def kernel(x, norm_mix, attn_w_qkv, attn_b_qkv, attn_sinks, attn_w_o, attn_b_o, conv_w_pw1, conv_b_pw1, conv_w_dw, conv_b_dw, conv_ln_g, conv_ln_b, conv_w_pw2, conv_b_pw2, norm_ffn, ffn_w_up, ffn_w_dw, ffn_b_dw, ffn_w_down, final_norm, loss_target, m_norm_mix, m_attn_w_qkv, m_attn_b_qkv, m_attn_sinks, m_attn_w_o, m_attn_b_o, m_conv_w_pw1, m_conv_b_pw1, m_conv_w_dw, m_conv_b_dw, m_conv_ln_g, m_conv_ln_b, m_conv_w_pw2, m_conv_b_pw2, m_norm_ffn, m_ffn_w_up, m_ffn_w_dw, m_ffn_b_dw, m_ffn_w_down, m_final_norm, v_norm_mix, v_attn_w_qkv, v_attn_b_qkv, v_attn_sinks, v_attn_w_o, v_attn_b_o, v_conv_w_pw1, v_conv_b_pw1, v_conv_w_dw, v_conv_b_dw, v_conv_ln_g, v_conv_ln_b, v_conv_w_pw2, v_conv_b_pw2, v_norm_ffn, v_ffn_w_up, v_ffn_w_dw, v_ffn_b_dw, v_ffn_w_down, v_final_norm):
    s = x.shape[1]
    me = 4 * lax.axis_index("x") + 2 * lax.axis_index("y") + lax.axis_index("c")
    x0 = x.reshape(s, D)
    tgt = loss_target.reshape(s, D)

    slab_qkv = attn_w_qkv[0].T.astype(BF16)
    slab = jnp.concatenate([attn_w_o[0], conv_w_pw1[0].T, conv_w_pw2[0],
                            ffn_w_up[0].T, ffn_w_down[0], ffn_w_up[1].T, ffn_w_down[1]], axis=0).astype(BF16)
    sharded_small = [conv_b_pw1, conv_w_dw, conv_b_dw, conv_ln_g, conv_ln_b, conv_b_pw2, ffn_w_dw]
    small_local = _pack_rows(sharded_small)
    w_qkv_t, small_all = _exchange_small("gather_first", _Gather((0,)), (slab_qkv,), small_local, reduce=False)
    b_pw1, w_cdw, b_cdw, ln_g, ln_b, b_pw2, w_fdw = _unpack_gathered(small_all, [a.shape for a in sharded_small])
    conv_w32 = jnp.concatenate([w_cdw[0], b_cdw], axis=0)
    score_bias = _score_bias(attn_sinks)

    qkv, h0 = _mm("qkv", x0, w_qkv_t, nt=True, tm=1024, tn=NQ + KVW, out_dtype=BF16, rms_g=norm_mix[0:1], bias=attn_b_qkv)
    o, lse, w_o, w_up0_t, w_dn0 = _attn_fwd(qkv, score_bias, _Gather((1, 4, 5), MAT_OFF[1]), (slab,))
    x2, h1, up0, act0, gate0, x1, w_pw1_t, w_pw2, w_up1_t, w_dn1 = _ffn_fwd(
        0, x0, norm_ffn[0:1], w_up0_t, w_fdw[0], ffn_b_dw[0:1], w_dn0, tm=256, pre=(o, w_o, attn_b_o),
        ex=_Gather((2, 3, 6, 7), MAT_OFF[1]), ex_args=(slab,))
    x3, h2, a, u2, s_act = _conv_fwd(x2, norm_mix[1:2], w_pw1_t, b_pw1, conv_w32, ln_g, ln_b, w_pw2, b_pw2, tt=256)
    dx4, h3, up1, act1, gate1, loss_acc, dg_final = _ffn_fwd(1, x3, norm_ffn[1:2], w_up1_t, w_fdw[1], ffn_b_dw[1:2], w_dn1,
                                                             tm=256, head=(final_norm.reshape(1, D), tgt))
    loss = lax.psum(loss_acc[0, 0], ("x", "y", "c"))

    dx3, d_up1, dg_f1, dwb1 = _ffn_bwd(1, dx4, x3, norm_ffn[1:2], w_up1_t, w_fdw[1], w_dn1, up1, gate1, tm=256)
    dw_dn1 = _mm_tn("ffn1_dwdown", act1, dx4, tn=FFN_HALF, tt=2048)
    dw_up1_t = _mm_tn("ffn1_dwup", d_up1, h3, tn=FFN_HALF, tt=2048)
    dw_pw2 = _mm_tn("dw_pw2", s_act, dx3, tn=D, tt=2048)
    scatter_a = _Scatter((6, 7))
    dx2, da, dg_m1, conv_acc8, db_pw1_8, db_pw2_8, buf_a = _conv_bwd(
        dx3, x2, norm_mix[1:2], a, u2, w_pw1_t, conv_w32, ln_g, ln_b, w_pw2, tt=256, ex=scatter_a, ex_args=(dw_up1_t, dw_dn1))
    conv_acc = jnp.sum(conv_acc8, axis=1)
    db_pw1 = jnp.sum(db_pw1_8, axis=0, keepdims=True)
    db_pw2 = jnp.sum(db_pw2_8, axis=0, keepdims=True)
    dw_pw1_t = _mm_tn("dw_pw1", da, h2, tn=D, tt=2048)
    dx1, d_up0, dg_f0, dwb0 = _ffn_bwd(0, dx2, x1, norm_ffn[0:1], w_up0_t, w_fdw[0], w_dn0, up0, gate0, tm=256)
    dw_dn0 = _mm_tn("ffn0_dwdown", act0, dx2, tn=FFN_HALF, tt=2048)
    dw_up0_t = _mm_tn("ffn0_dwup", d_up0, h1, tn=FFN_HALF, tt=2048)
    do = _mm("d_attn_out", dx1, w_o, nt=True, tm=1024, tn=D, out_dtype=BF16)
    dw_o, db_o = _mm_tn("dw_o", o, dx1, tn=D, tt=2048, colsum_r=True)
    scatter_b = _Scatter((1, 2, 3, 4, 5))
    dq, dkv, dsk, buf_b = _attn_bwd(qkv, do, o, lse, score_bias, scatter_b, (dw_o, dw_pw1_t, dw_pw2, dw_up0_t, dw_dn0))
    dqkv = jnp.concatenate([dq, dkv], axis=1)
    grad_x, dg_m0 = _dgrad_rms("d_qkv", dqkv, w_qkv_t, x0, norm_mix[0:1], dx1, tm=1024, tk=NQ + KVW)
    dw_qkv_t, db_qkv = _mm_tn("dw_qkv", dqkv, h0, tn=NQ + KVW, tt=2048, colsum_l=True)

    dwb0, dwb1 = jnp.sum(dwb0, axis=1), jnp.sum(dwb1, axis=1)
    d_sinks = jnp.transpose(jnp.sum(dsk.reshape(N_KV, 2, 4, BLOCK), axis=-1), (0, 2, 1)).reshape(1, 16)
    small_grads = [jnp.concatenate([dg_m0, dg_m1], axis=0), db_qkv, d_sinks, db_o, jnp.concatenate([dg_f0, dg_f1], axis=0),
                   jnp.stack([dwb0[3], dwb1[3]]), dg_final, db_pw1, conv_acc[0:CONV_K], conv_acc[31:32], conv_acc[32:33],
                   conv_acc[33:34], db_pw2, jnp.stack([dwb0[0:3], dwb1[0:3]])]
    small_shapes = [(2, D), (1, NQ + KVW), (1, 16), (1, D), (2, D), (2, F), (D,), (1, 2 * D), (1, CONV_K, D), (1, D),
                    (1, D), (1, D), (1, D), (2, 3, F)]
    scatter_c = _Scatter((0,))
    buf_c, small_sum = _exchange_small("scatter_last", scatter_c, (dw_qkv_t,), _pack_rows(small_grads), reduce=True)
    gm = [None] * N_MAT
    for tag, sc, buf in (("a", scatter_a, buf_a), ("b", scatter_b, buf_b), ("c", scatter_c, buf_c)):
        gslab = _sum_slabs("sum_slabs_" + tag, buf)
        for i, off in zip(sc.mats, sc.offsets()):
            gm[i] = gslab[off:off + MAT_ROWS[i]]
    (g_norm_mix, g_b_qkv, g_sinks, g_b_o, g_norm_ffn, g_ffn_b_dw, g_final, g_b_pw1_full, g_w_cdw_full, g_b_cdw_full,
     g_ln_g_full, g_ln_b_full, g_b_pw2_full, g_w_fdw_full) = _unpack_rows(small_sum, small_shapes)

    def shard(a, width):
        return lax.dynamic_slice_in_dim(a, me * width, width, axis=a.ndim - 1)

    grads = {
        "norm_mix": g_norm_mix, "attn_w_qkv": gm[0].T[None], "attn_b_qkv": g_b_qkv, "attn_sinks": g_sinks,
        "attn_w_o": gm[1][None], "attn_b_o": g_b_o, "conv_w_pw1": gm[2].T[None], "conv_b_pw1": shard(g_b_pw1_full, 256),
        "conv_w_dw": shard(g_w_cdw_full, 128), "conv_b_dw": shard(g_b_cdw_full, 128), "conv_ln_g": shard(g_ln_g_full, 128),
        "conv_ln_b": shard(g_ln_b_full, 128), "conv_w_pw2": gm[3][None], "conv_b_pw2": shard(g_b_pw2_full, 128),
        "norm_ffn": g_norm_ffn, "ffn_w_up": jnp.stack([gm[4].T, gm[6].T]), "ffn_w_dw": shard(g_w_fdw_full, 352),
        "ffn_b_dw": g_ffn_b_dw, "ffn_w_down": jnp.stack([gm[5], gm[7]]), "final_norm": g_final,
    }
    weights = dict(norm_mix=norm_mix, attn_w_qkv=attn_w_qkv, attn_b_qkv=attn_b_qkv, attn_sinks=attn_sinks, attn_w_o=attn_w_o, attn_b_o=attn_b_o, conv_w_pw1=conv_w_pw1, conv_b_pw1=conv_b_pw1, conv_w_dw=conv_w_dw, conv_b_dw=conv_b_dw, conv_ln_g=conv_ln_g, conv_ln_b=conv_ln_b, conv_w_pw2=conv_w_pw2, conv_b_pw2=conv_b_pw2, norm_ffn=norm_ffn, ffn_w_up=ffn_w_up, ffn_w_dw=ffn_w_dw, ffn_b_dw=ffn_b_dw, ffn_w_down=ffn_w_down, final_norm=final_norm)
    moms = dict(norm_mix=m_norm_mix, attn_w_qkv=m_attn_w_qkv, attn_b_qkv=m_attn_b_qkv, attn_sinks=m_attn_sinks, attn_w_o=m_attn_w_o, attn_b_o=m_attn_b_o, conv_w_pw1=m_conv_w_pw1, conv_b_pw1=m_conv_b_pw1, conv_w_dw=m_conv_w_dw, conv_b_dw=m_conv_b_dw, conv_ln_g=m_conv_ln_g, conv_ln_b=m_conv_ln_b, conv_w_pw2=m_conv_w_pw2, conv_b_pw2=m_conv_b_pw2, norm_ffn=m_norm_ffn, ffn_w_up=m_ffn_w_up, ffn_w_dw=m_ffn_w_dw, ffn_b_dw=m_ffn_b_dw, ffn_w_down=m_ffn_w_down, final_norm=m_final_norm)
    vars_ = dict(norm_mix=v_norm_mix, attn_w_qkv=v_attn_w_qkv, attn_b_qkv=v_attn_b_qkv, attn_sinks=v_attn_sinks, attn_w_o=v_attn_w_o, attn_b_o=v_attn_b_o, conv_w_pw1=v_conv_w_pw1, conv_b_pw1=v_conv_b_pw1, conv_w_dw=v_conv_w_dw, conv_b_dw=v_conv_b_dw, conv_ln_g=v_conv_ln_g, conv_ln_b=v_conv_ln_b, conv_w_pw2=v_conv_w_pw2, conv_b_pw2=v_conv_b_pw2, norm_ffn=v_norm_ffn, ffn_w_up=v_ffn_w_up, ffn_w_dw=v_ffn_w_dw, ffn_b_dw=v_ffn_b_dw, ffn_w_down=v_ffn_w_down, final_norm=v_final_norm)
    names = list(weights)
    delta, new_m, new_v = {}, {}, {}
    for nm in names:
        shp = weights[nm].shape
        two_d = (math.prod(shp[:-1]), shp[-1])
        res = _adamw("adamw_" + nm, *(t[nm].reshape(two_d) for t in (weights, grads, moms, vars_)))
        delta[nm], new_m[nm], new_v[nm] = (r.reshape(shp) for r in res)
    return (loss, grad_x.reshape(1, s, D), *[grads[nm] for nm in names], *[delta[nm] for nm in names],
            *[new_m[nm] for nm in names], *[new_v[nm] for nm in names])
```

```python
import math

import jax
import jax.numpy as jnp
from jax import lax
from jax.experimental import pallas as pl
from jax.experimental.pallas import tpu as pltpu

F32 = jnp.float32
BF16 = jnp.bfloat16

D = 1024
F = 2816
HEAD_DIM = 64
N_KV = 2
BLOCK = 128
NQ = 1024
KVW = 256
CONV_K = 31
RMS_EPS = 1e-6
LN_EPS = 1e-5
ADAM_LR, ADAM_B1, ADAM_B2, ADAM_EPS, ADAM_WD, ADAM_STEP = 0.001, 0.9, 0.999, 1e-08, 0.01, 10
N_DEV = 8
MESH = pl.DeviceIdType.MESH
VMEM_LIMIT_BYTES = 56 * 2**20
MASKED = -1e30

MAT_ROWS = (160, 128, 256, 128, 704, 352, 704, 352)
MAT_OFF = tuple(sum(MAT_ROWS[:i]) for i in range(len(MAT_ROWS)))
SLAB_ROWS = sum(MAT_ROWS)
N_MAT = len(MAT_ROWS)

NT_DIMS = (((1,), (1,)), ((), ()))
NN_DIMS = (((1,), (0,)), ((), ()))
TN_DIMS = (((0,), (0,)), ((), ()))


def _params(*sem):
    return pltpu.CompilerParams(dimension_semantics=tuple(sem), vmem_limit_bytes=VMEM_LIMIT_BYTES)


def _row_tile(rows, cap, mult=8):
    best = None
    for t in range(mult, min(rows, cap) + 1, mult):
        if rows % t == 0:
            best = t
    return best if best is not None else rows


def _sigmoid(x):
    return 1.0 / (1.0 + jnp.exp(-x))


def _mm(name, a, b, *, nt, tm, tn, out_dtype, rms_g=None, bias=None):
    m, k = a.shape
    n = b.shape[0] if nt else b.shape[1]
    tm = min(tm, m)
    has_rms = rms_g is not None
    dims = NT_DIMS if nt else NN_DIMS

    def body(*refs):
        a_ref, b_ref = refs[0], refs[1]
        pos = 2
        g_ref = bias_ref = h_ref = hs_ref = None
        if has_rms:
            g_ref = refs[pos]; pos += 1
        if bias is not None:
            bias_ref = refs[pos]; pos += 1
        o_ref = refs[pos]; pos += 1
        if has_rms:
            h_ref, hs_ref = refs[pos], refs[pos + 1]

            @pl.when(pl.program_id(1) == 0)
            def _():
                xf = a_ref[...]
                r = lax.rsqrt(jnp.mean(xf * xf, axis=-1, keepdims=True) + RMS_EPS)
                h = ((xf * r) * g_ref[...]).astype(BF16)
                hs_ref[...] = h
                h_ref[...] = h

            lhs = hs_ref[...]
        else:
            lhs = a_ref[...].astype(BF16)
        acc = lax.dot_general(lhs, b_ref[...], dims, preferred_element_type=F32)
        if bias is not None:
            acc = acc + bias_ref[...]
        o_ref[...] = acc.astype(out_dtype)

    in_specs = [pl.BlockSpec((tm, k), lambda i, j: (i, 0)),
                pl.BlockSpec((tn, k), lambda i, j: (j, 0)) if nt else pl.BlockSpec((k, tn), lambda i, j: (0, j))]
    args = [a, b]
    if has_rms:
        in_specs.append(pl.BlockSpec((1, k), lambda i, j: (0, 0))); args.append(rms_g)
    if bias is not None:
        in_specs.append(pl.BlockSpec((1, tn), lambda i, j: (0, j))); args.append(bias)
    out_shape = [jax.ShapeDtypeStruct((m, n), out_dtype)]
    out_specs = [pl.BlockSpec((tm, tn), lambda i, j: (i, j))]
    scratch = []
    if has_rms:
        out_shape.append(jax.ShapeDtypeStruct((m, k), BF16))
        out_specs.append(pl.BlockSpec((tm, k), lambda i, j: (i, 0)))
        scratch.append(pltpu.VMEM((tm, k), BF16))
    outs = pl.pallas_call(body, grid=(m // tm, n // tn), in_specs=in_specs, out_specs=out_specs, out_shape=out_shape,
                          scratch_shapes=scratch, compiler_params=_params("parallel", "arbitrary"), name=name)(*args)
    return outs if has_rms else outs[0]


def _mm_tn(name, l, r, *, tn, tt, colsum_l=False, colsum_r=False):
    s, nl = l.shape
    nr = r.shape[1]
    tt = min(tt, s)
    n_t = s // tt

    def body(*refs):
        l_ref, r_ref, o_ref = refs[0], refs[1], refs[2]
        pos = 3
        cl_ref = cr_ref = None
        if colsum_l:
            cl_ref = refs[pos]; pos += 1
        if colsum_r:
            cr_ref = refs[pos]; pos += 1
        acc_ref = refs[pos]
        j, t = pl.program_id(0), pl.program_id(1)

        @pl.when(t == 0)
        def _():
            acc_ref[...] = jnp.zeros_like(acc_ref)

        lv = l_ref[...]
        rv = r_ref[...]
        acc_ref[...] += lax.dot_general(lv, rv.astype(BF16), TN_DIMS, preferred_element_type=F32)
        if colsum_l:
            @pl.when(t == 0)
            def _():
                cl_ref[...] = jnp.zeros_like(cl_ref)

            cl_ref[...] += jnp.sum(lv.astype(F32), axis=0, keepdims=True)
        if colsum_r:
            @pl.when((t == 0) & (j == 0))
            def _():
                cr_ref[...] = jnp.zeros_like(cr_ref)

            @pl.when(j == 0)
            def _():
                cr_ref[...] += jnp.sum(rv.astype(F32), axis=0, keepdims=True)

        @pl.when(t == n_t - 1)
        def _():
            o_ref[...] = acc_ref[...].astype(BF16)

    out_shape = [jax.ShapeDtypeStruct((nl, nr), BF16)]
    out_specs = [pl.BlockSpec((tn, nr), lambda j, t: (j, 0))]
    if colsum_l:
        out_shape.append(jax.ShapeDtypeStruct((1, nl), F32))
        out_specs.append(pl.BlockSpec((1, tn), lambda j, t: (0, j)))
    if colsum_r:
        out_shape.append(jax.ShapeDtypeStruct((1, nr), F32))
        out_specs.append(pl.BlockSpec((1, nr), lambda j, t: (0, 0)))
    outs = pl.pallas_call(
        body, grid=(nl // tn, n_t),
        in_specs=[pl.BlockSpec((tt, tn), lambda j, t: (t, j)), pl.BlockSpec((tt, nr), lambda j, t: (t, 0))],
        out_specs=out_specs, out_shape=out_shape, scratch_shapes=[pltpu.VMEM((tn, nr), F32)],
        compiler_params=_params("arbitrary", "arbitrary"), name=name)(l, r)
    return outs if (colsum_l or colsum_r) else outs[0]


def _rms_bwd(dh, xf, g):
    r = lax.rsqrt(jnp.mean(xf * xf, axis=-1, keepdims=True) + RMS_EPS)
    gd = dh * g
    dx = r * gd - xf * ((r * r * r) * jnp.mean(xf * gd, axis=-1, keepdims=True))
    return dx, dh * (xf * r)


def _dgrad_rms(name, dy, w, x, g, dres, *, tm, tk):
    m, k = dy.shape
    tm = min(tm, m)
    n_k = k // tk

    def body(dy_ref, w_ref, x_ref, g_ref, dres_ref, o_ref, dg_ref, acc_ref):
        i, kk = pl.program_id(0), pl.program_id(1)

        @pl.when(kk == 0)
        def _():
            acc_ref[...] = jnp.zeros_like(acc_ref)

        acc_ref[...] += lax.dot_general(dy_ref[...], w_ref[...], NN_DIMS, preferred_element_type=F32)

        @pl.when((i == 0) & (kk == n_k - 1))
        def _():
            dg_ref[...] = jnp.zeros_like(dg_ref)

        @pl.when(kk == n_k - 1)
        def _():
            dx, dg_rows = _rms_bwd(acc_ref[...], x_ref[...], g_ref[...])
            o_ref[...] = dres_ref[...] + dx
            dg_ref[...] += jnp.sum(dg_rows, axis=0, keepdims=True)

    return pl.pallas_call(
        body, grid=(m // tm, n_k),
        in_specs=[pl.BlockSpec((tm, tk), lambda i, kk: (i, kk)), pl.BlockSpec((tk, D), lambda i, kk: (kk, 0)),
                  pl.BlockSpec((tm, D), lambda i, kk: (i, 0)), pl.BlockSpec((1, D), lambda i, kk: (0, 0)),
                  pl.BlockSpec((tm, D), lambda i, kk: (i, 0))],
        out_specs=[pl.BlockSpec((tm, D), lambda i, kk: (i, 0)), pl.BlockSpec((1, D), lambda i, kk: (0, 0))],
        out_shape=[jax.ShapeDtypeStruct((m, D), F32), jax.ShapeDtypeStruct((1, D), F32)],
        scratch_shapes=[pltpu.VMEM((tm, D), F32)],
        compiler_params=_params("arbitrary", "arbitrary"), name=name)(dy, w, x, g, dres)


def _band(kvp, kvc):
    kk = jnp.concatenate([kvp[:, :128], kvc[:, :128]], axis=0)
    vv = jnp.concatenate([kvp[:, 128:], kvc[:, 128:]], axis=0)
    row = lax.broadcasted_iota(jnp.int32, kk.shape, 0)
    return jnp.where(row == 0, jnp.zeros_like(kk), kk), jnp.where(row == 0, jnp.zeros_like(vv), vv)


def _blockdiag(t, h):
    lane = lax.broadcasted_iota(jnp.int32, t.shape, 1)
    z = jnp.zeros_like(t)
    tr = pltpu.roll(t, 64, 1)
    if h == 0:
        top, bot = jnp.where(lane < 64, t, z), jnp.where(lane >= 64, tr, z)
    else:
        top, bot = jnp.where(lane < 64, tr, z), jnp.where(lane >= 64, t, z)
    return jnp.concatenate([top, bot], axis=0)


def _from_blockdiag(t, h):
    lane = lax.broadcasted_iota(jnp.int32, (256, 128), 1)
    top, bot = t[:256], t[256:]
    if h == 0:
        return jnp.where(lane < 64, top + pltpu.roll(bot, 64, 1), 0.0)
    return jnp.where(lane >= 64, pltpu.roll(top, 64, 1) + bot, 0.0)


def _stack_heads(ref, h):
    return jnp.concatenate([ref[:, h * 512 + p * 128: h * 512 + (p + 1) * 128] for p in range(4)], axis=0)


def _score_bias(sinks):
    row = jnp.arange(512)[:, None] & 127
    col = jnp.arange(256)[None, :]
    band = (col > row) & (col <= row + BLOCK)
    valid = jnp.stack([band & (col >= BLOCK), band])
    sink_rows = jnp.repeat(jnp.transpose(sinks.reshape(N_KV, 4, 2), (0, 2, 1)), BLOCK, axis=-1).reshape(4, 512, 1)
    bias = jnp.where(valid, 0.0, MASKED)[:, None] + jnp.zeros((1, 4, 1, 1), F32)
    return jnp.where(col == 0, sink_rows[None], bias).astype(F32)


def _half_selector(shape, split):
    row = lax.broadcasted_iota(jnp.int32, shape, 0)
    one, zero = jnp.ones(shape, BF16), jnp.zeros(shape, BF16)
    return jnp.where(row < split, one, zero), jnp.where(row >= split, one, zero)


def _attn_fwd(qkv, bias, ex=None, ex_args=()):
    s = qkv.shape[0]
    nb = s // BLOCK
    scale = 1.0 / math.sqrt(HEAD_DIM)

    def body(q_ref, kvc_ref, kvp_ref, sk_ref, o_ref, lse_ref):
        kk, vv = _band(kvp_ref[...], kvc_ref[...])
        sel0, sel1 = _half_selector((512, 128), 256)
        lane = lax.broadcasted_iota(jnp.int32, (512, 128), 1)
        for h in range(N_KV):
            kbd, vbd = _blockdiag(kk, h), _blockdiag(vv, h)
            sc = lax.dot_general(_stack_heads(q_ref, h), kbd, NT_DIMS, preferred_element_type=F32) * scale
            pes, mxs = [], []
            for half in range(2):
                sh = sc[:, half * 256:(half + 1) * 256] + sk_ref[h * 2 + half]
                mx = jnp.max(sh, axis=-1, keepdims=True)
                pes.append(jnp.exp(sh - mx).astype(BF16))
                mxs.append(mx)
            pb = jnp.concatenate(pes, axis=1)
            o_un = lax.dot_general(pb, vbd, NN_DIMS, preferred_element_type=F32)
            sums = [lax.dot_general(pb, sel, NN_DIMS, preferred_element_type=F32) for sel in (sel0, sel1)]
            o = o_un * jnp.where(lane < 64, 1.0 / sums[0], 1.0 / sums[1])
            for half in range(2):
                lse_ref[h * 2 + half] = mxs[half] + jnp.log(sums[half])
            for p in range(4):
                o_ref[:, h * 512 + p * 128: h * 512 + (p + 1) * 128] = o[p * 128:(p + 1) * 128].astype(BF16)

    return pl.pallas_call(
        _carried(body, 4, 2, ex, nb), grid=(nb,),
        in_specs=[pl.BlockSpec((BLOCK, NQ), lambda n: (n, 0)),
                  pl.BlockSpec((BLOCK, KVW), lambda n: (n, NQ // KVW)),
                  pl.BlockSpec((BLOCK, KVW), lambda n: (jnp.maximum(n - 1, 0), NQ // KVW)),
                  pl.BlockSpec((None, 4, 512, 256), lambda n: (jnp.minimum(n, 1), 0, 0, 0))] + _ex(ex, "in_specs"),
        out_specs=[pl.BlockSpec((BLOCK, NQ), lambda n: (n, 0)),
                   pl.BlockSpec((None, 4, 512, 128), lambda n: (n, 0, 0, 0))] + _ex(ex, "out_specs"),
        out_shape=[jax.ShapeDtypeStruct((s, NQ), BF16), jax.ShapeDtypeStruct((nb, 4, 512, 128), F32)] + _ex(ex, "out_shape"),
        scratch_shapes=_ex(ex, "scratch"),
        compiler_params=_params("arbitrary"), name="attn_fwd")(qkv, qkv, qkv, bias, *ex_args)


def _attn_bwd(qkv, do, o, lse, bias, ex=None, ex_args=()):
    s = qkv.shape[0]
    nb = s // BLOCK
    scale = 1.0 / math.sqrt(HEAD_DIM)

    def body(q_ref, kvc_ref, kvp_ref, do_ref, o_ref, lse_ref, sk_ref, dq_ref, dkv_ref, dsk_ref, ck_ref, cv_ref):
        n = pl.program_id(0)

        @pl.when(n == 0)
        def _():
            dsk_ref[...] = jnp.zeros_like(dsk_ref)
            ck_ref[...] = jnp.zeros_like(ck_ref)
            cv_ref[...] = jnp.zeros_like(cv_ref)

        @pl.when(n < nb)
        def _():
            kk, vv = _band(kvp_ref[...], kvc_ref[...])
            sel0, sel1 = _half_selector((128, 128), 64)
            dkk = jnp.zeros((256, 128), F32)
            dvv = jnp.zeros((256, 128), F32)
            for h in range(N_KV):
                kbd, vbd = _blockdiag(kk, h), _blockdiag(vv, h)
                qp = _stack_heads(q_ref, h)
                dop = _stack_heads(do_ref, h)
                sc = lax.dot_general(qp, kbd, NT_DIMS, preferred_element_type=F32) * scale
                dp = lax.dot_general(dop, vbd, NT_DIMS, preferred_element_type=F32)
                dd = (dop.astype(F32) * _stack_heads(o_ref, h).astype(F32)).astype(BF16)
                probs, dss = [], []
                for half, sel in enumerate((sel0, sel1)):
                    delta = lax.dot_general(dd, sel, NN_DIMS, preferred_element_type=F32)
                    lse_h = lse_ref[h * 2 + half]
                    sh = sc[:, half * 256:(half + 1) * 256] + sk_ref[h * 2 + half]
                    pr = jnp.exp(sh - jnp.concatenate([lse_h, lse_h], axis=1))
                    dsf = pr * (dp[:, half * 256:(half + 1) * 256] - jnp.concatenate([delta, delta], axis=1))
                    dsk_ref[h * 2 + half] += dsf[:, 0:1]
                    dss.append((dsf * scale).astype(BF16))
                    probs.append(pr.astype(BF16))
                ds = jnp.concatenate(dss, axis=1)
                pb = jnp.concatenate(probs, axis=1)
                dqp = lax.dot_general(ds, kbd, NN_DIMS, preferred_element_type=F32)
                for p in range(4):
                    dq_ref[:, h * 512 + p * 128: h * 512 + (p + 1) * 128] = dqp[p * 128:(p + 1) * 128].astype(BF16)
                dkk = dkk + _from_blockdiag(lax.dot_general(ds, qp, TN_DIMS, preferred_element_type=F32), h)
                dvv = dvv + _from_blockdiag(lax.dot_general(pb, dop, TN_DIMS, preferred_element_type=F32), h)
            row = lax.broadcasted_iota(jnp.int32, (256, 128), 0)
            dkk = jnp.where(row == 0, 0.0, dkk)
            dvv = jnp.where(row == 0, 0.0, dvv)

            @pl.when(n >= 1)
            def _():
                dkv_ref[:, :128] = (ck_ref[...] + dkk[:128]).astype(BF16)
                dkv_ref[:, 128:] = (cv_ref[...] + dvv[:128]).astype(BF16)

            ck_ref[...] = dkk[128:]
            cv_ref[...] = dvv[128:]

        @pl.when(n == nb)
        def _():
            dkv_ref[:, :128] = ck_ref[...].astype(BF16)
            dkv_ref[:, 128:] = cv_ref[...].astype(BF16)

    last = nb - 1
    blk = lambda n: (jnp.minimum(n, last), 0)
    return pl.pallas_call(
        _carried(body, 7, 3, ex, nb + 1), grid=(nb + 1,),
        in_specs=[pl.BlockSpec((BLOCK, NQ), blk),
                  pl.BlockSpec((BLOCK, KVW), lambda n: (jnp.minimum(n, last), NQ // KVW)),
                  pl.BlockSpec((BLOCK, KVW), lambda n: (jnp.clip(n - 1, 0, last), NQ // KVW)),
                  pl.BlockSpec((BLOCK, NQ), blk), pl.BlockSpec((BLOCK, NQ), blk),
                  pl.BlockSpec((None, 4, 512, 128), lambda n: (jnp.minimum(n, last), 0, 0, 0)),
                  pl.BlockSpec((None, 4, 512, 256), lambda n: (jnp.minimum(n, 1), 0, 0, 0))] + _ex(ex, "in_specs"),
        out_specs=[pl.BlockSpec((BLOCK, NQ), blk),
                   pl.BlockSpec((BLOCK, KVW), lambda n: (jnp.maximum(n - 1, 0), 0)),
                   pl.BlockSpec((4, 512, 1), lambda n: (0, 0, 0))] + _ex(ex, "out_specs"),
        out_shape=[jax.ShapeDtypeStruct((s, NQ), BF16), jax.ShapeDtypeStruct((s, KVW), BF16),
                   jax.ShapeDtypeStruct((4, 512, 1), F32)] + _ex(ex, "out_shape"),
        scratch_shapes=[pltpu.VMEM((BLOCK, 128), F32), pltpu.VMEM((BLOCK, 128), F32)] + _ex(ex, "scratch"),
        compiler_params=_params("arbitrary"), name="attn_bwd")(qkv, qkv, qkv, do, o, lse, bias, *ex_args)


LANE = 128
FFN_HALF = F // 2
FFN_PAIR = 2 * LANE
FFN_PAIRS = F // FFN_PAIR
FFN_GROUPS = ((0, 2), (2, 4), (4, 6), (6, 8), (8, 10), (10, 11))
FFN_HALO = 8


def _resident(shape):
    return pl.BlockSpec(shape, lambda i: (0,) * len(shape), pipeline_mode=pl.Buffered(1))


def _ffn_fwd(tag, x_in, g, w_up_t, w_dw, b_dw, w_down, *, tm, pre=None, head=None, ex=None, ex_args=()):
    s = x_in.shape[0]
    tm = min(tm, s)
    n_pre = 3 if pre is not None else 0
    n_head = 2 if head is not None else 0

    def body(*refs):
        x_ref, g_ref, wup_ref, wdw_ref, bdw_ref, wd_ref = refs[:6]
        pre_refs = refs[6:6 + n_pre]
        head_refs = refs[6 + n_pre:6 + n_pre + n_head]
        p = 6 + n_pre + n_head
        o_ref, h_ref, up_ref, act_ref, gate_ref = refs[p:p + 5]
        p += 5
        if pre is not None:
            xmid_ref, p = refs[p], p + 1
        if head is not None:
            loss_ref, dgf_ref, p = refs[p], refs[p + 1], p + 2
        carry_ref, ext_ref = refs[p], refs[p + 1]

        @pl.when(pl.program_id(0) == 0)
        def _():
            carry_ref[...] = jnp.zeros_like(carry_ref)
            if head is not None:
                loss_ref[...] = jnp.zeros_like(loss_ref)
                dgf_ref[...] = jnp.zeros_like(dgf_ref)

        xf = x_ref[...]
        if pre is not None:
            oin_ref, wo_ref, bo_ref = pre_refs
            xf = xf + lax.dot_general(oin_ref[...], wo_ref[...], NN_DIMS, preferred_element_type=F32) + bo_ref[...]
            xmid_ref[...] = xf
        r = lax.rsqrt(jnp.mean(xf * xf, axis=-1, keepdims=True) + RMS_EPS)
        h = ((xf * r) * g_ref[...]).astype(BF16)
        h_ref[...] = h

        def up_mm(p):
            for off in (p * FFN_PAIR, F + p * FFN_PAIR):
                rs = slice(off, off + FFN_PAIR)
                up_ref[:, rs] = lax.dot_general(h, wup_ref[rs, :], NT_DIMS, preferred_element_type=F32).astype(BF16)

        def elementwise(p):
            for c in (2 * p, 2 * p + 1):
                cs = slice(c * LANE, (c + 1) * LANE)
                vs = slice(F + c * LANE, F + (c + 1) * LANE)
                gcol = up_ref[:, cs].astype(F32)
                ext_ref[pl.ds(0, FFN_HALO), :] = carry_ref[:, cs]
                ext_ref[pl.ds(FFN_HALO, tm), :] = gcol
                gate_b = (wdw_ref[2:3, cs] * gcol + wdw_ref[1:2, cs] * ext_ref[pl.ds(FFN_HALO - 1, tm), :]
                          + wdw_ref[0:1, cs] * ext_ref[pl.ds(FFN_HALO - 2, tm), :] + bdw_ref[:, cs]).astype(BF16)
                gate_ref[:, cs] = gate_b
                gate = gate_b.astype(F32)
                act_ref[:, cs] = (gate * _sigmoid(gate) * up_ref[:, vs].astype(F32)).astype(BF16)
                carry_ref[:, cs] = gcol[tm - FFN_HALO:]

        def down_mm(lo, hi):
            ks = slice(lo * FFN_PAIR, hi * FFN_PAIR)
            return lax.dot_general(act_ref[:, ks], wd_ref[ks, :], NN_DIMS, preferred_element_type=F32)

        acc = xf
        up_mm(0)
        pending = None
        for lo, hi in FFN_GROUPS:
            for p in range(lo, hi):
                if p + 1 < FFN_PAIRS:
                    up_mm(p + 1)
                if pending is not None:
                    acc = acc + down_mm(*pending)
                    pending = None
                elementwise(p)
            pending = (lo, hi)
        x_out = acc + down_mm(*pending)
        if head is None:
            o_ref[...] = x_out
        else:
            gf_ref, tgt_ref = head_refs
            gf = gf_ref[...]
            rf = lax.rsqrt(jnp.mean(x_out * x_out, axis=-1, keepdims=True) + RMS_EPS)
            diff = (x_out * rf) * gf - tgt_ref[...]
            loss_ref[...] += 0.5 * jnp.sum(jnp.mean(diff * diff, axis=-1, keepdims=True))
            dx, dg_rows = _rms_bwd(diff * (1.0 / D), x_out, gf)
            o_ref[...] = dx
            dgf_ref[...] += jnp.sum(dg_rows, axis=0, keepdims=True)

    row = lambda i: (i, 0)
    const = lambda i: (0, 0)
    in_specs = [pl.BlockSpec((tm, D), row), _resident((1, D)), _resident((2 * F, D)), _resident((3, F)),
                _resident((1, F)), _resident((F, D))]
    out_specs = [pl.BlockSpec((tm, D), row), pl.BlockSpec((tm, D), row), pl.BlockSpec((tm, 2 * F), row),
                 pl.BlockSpec((tm, F), row), pl.BlockSpec((tm, F), row)]
    out_shape = [jax.ShapeDtypeStruct((s, D), F32), jax.ShapeDtypeStruct((s, D), BF16),
                 jax.ShapeDtypeStruct((s, 2 * F), BF16), jax.ShapeDtypeStruct((s, F), BF16),
                 jax.ShapeDtypeStruct((s, F), BF16)]
    args = [x_in, g, w_up_t, w_dw, b_dw, w_down]
    if pre is not None:
        in_specs += [pl.BlockSpec((tm, NQ), row), _resident((NQ, D)), _resident((1, D))]
        out_specs.append(pl.BlockSpec((tm, D), row))
        out_shape.append(jax.ShapeDtypeStruct((s, D), F32))
        args += list(pre)
    if head is not None:
        in_specs += [_resident((1, D)), pl.BlockSpec((tm, D), row)]
        out_specs += [pl.BlockSpec((1, 128), const), pl.BlockSpec((1, D), const)]
        out_shape += [jax.ShapeDtypeStruct((1, 128), F32), jax.ShapeDtypeStruct((1, D), F32)]
        args += list(head)
    return pl.pallas_call(
        _carried(body, len(args), len(out_shape), ex, s // tm), grid=(s // tm,),
        in_specs=in_specs + _ex(ex, "in_specs"), out_specs=out_specs + _ex(ex, "out_specs"),
        out_shape=out_shape + _ex(ex, "out_shape"),
        scratch_shapes=[pltpu.VMEM((FFN_HALO, F), F32), pltpu.VMEM((tm + FFN_HALO, LANE), F32)] + _ex(ex, "scratch"),
        compiler_params=_params("arbitrary"), name=f"ffn{tag}_fwd")(*args, *ex_args)


def _ffn_bwd(tag, dx_out, x_in, g, w_up_t, w_dw, w_down, up, gate, *, tm):
    s = x_in.shape[0]
    tm = min(tm, s)
    n_i = s // tm

    def body(dx_ref, x_ref, g_ref, up_ref, gate_ref, wup_ref, wdw_ref, wd_ref,
             dxin_ref, dup_ref, dg_ref, dwb_ref, carry_ref, da0_ref, da1_ref, extd_ref):
        i = pl.program_id(0)

        @pl.when(i == 0)
        def _():
            carry_ref[...] = jnp.zeros_like(carry_ref)
            dg_ref[...] = jnp.zeros_like(dg_ref)
            dwb_ref[...] = jnp.zeros_like(dwb_ref)

        dxf = dx_ref[...]
        dxb = dxf.astype(BF16)
        da_refs = (da0_ref, da1_ref)

        def da_mm(p):
            ks = slice(p * FFN_PAIR, (p + 1) * FFN_PAIR)
            da_refs[p % 2][...] = lax.dot_general(dxb, wd_ref[ks, :], NT_DIMS, preferred_element_type=F32)

        def elementwise(p):
            da_ref = da_refs[p % 2]
            for c in range(2):
                cc = 2 * p + c
                cs = slice(cc * LANE, (cc + 1) * LANE)
                vs = slice(F + cc * LANE, F + (cc + 1) * LANE)
                gate = gate_ref[:, cs].astype(F32)
                sg = _sigmoid(gate)
                silu = gate * sg
                dac = da_ref[:, c * LANE:(c + 1) * LANE]
                dup_ref[:, vs] = (dac * silu).astype(BF16)
                dgate = dac * up_ref[:, vs].astype(F32) * (sg + silu * (1.0 - sg))
                extd_ref[pl.ds(0, tm), :] = dgate
                extd_ref[pl.ds(tm, FFN_HALO), :] = carry_ref[:, cs]
                d1 = extd_ref[pl.ds(1, tm), :]
                d2 = extd_ref[pl.ds(2, tm), :]
                dup_ref[:, cs] = (wdw_ref[2:3, cs] * dgate + wdw_ref[1:2, cs] * d1 + wdw_ref[0:1, cs] * d2).astype(BF16)
                carry_ref[:, cs] = dgate[:FFN_HALO]
                gcol = up_ref[:, cs].astype(F32)
                dwb_ref[2, :, cs] += _sum8(dgate * gcol)
                dwb_ref[1, :, cs] += _sum8(d1 * gcol)
                dwb_ref[0, :, cs] += _sum8(d2 * gcol)
                dwb_ref[3, :, cs] += _sum8(dgate)

        def dh_mm(lo, hi):
            ks = slice(lo * FFN_PAIR, hi * FFN_PAIR)
            vks = slice(F + lo * FFN_PAIR, F + hi * FFN_PAIR)
            return (lax.dot_general(dup_ref[:, ks], wup_ref[ks, :], NN_DIMS, preferred_element_type=F32)
                    + lax.dot_general(dup_ref[:, vks], wup_ref[vks, :], NN_DIMS, preferred_element_type=F32))

        acc = None
        da_mm(0)
        pending = None
        for lo, hi in FFN_GROUPS:
            for p in range(lo, hi):
                if p + 1 < FFN_PAIRS:
                    da_mm(p + 1)
                if pending is not None:
                    part = dh_mm(*pending)
                    acc = part if acc is None else acc + part
                    pending = None
                elementwise(p)
            pending = (lo, hi)
        acc = acc + dh_mm(*pending)
        dx, dg_rows = _rms_bwd(acc, x_ref[...], g_ref[...])
        dxin_ref[...] = dxf + dx
        dg_ref[...] += jnp.sum(dg_rows, axis=0, keepdims=True)

    rev = lambda i: (n_i - 1 - i, 0)
    return pl.pallas_call(
        body, grid=(n_i,),
        in_specs=[pl.BlockSpec((tm, D), rev), pl.BlockSpec((tm, D), rev), _resident((1, D)),
                  pl.BlockSpec((tm, 2 * F), rev), pl.BlockSpec((tm, F), rev),
                  _resident((2 * F, D)), _resident((3, F)), _resident((F, D))],
        out_specs=[pl.BlockSpec((tm, D), rev), pl.BlockSpec((tm, 2 * F), rev),
                   pl.BlockSpec((1, D), lambda i: (0, 0)), pl.BlockSpec((4, 8, F), lambda i: (0, 0, 0))],
        out_shape=[jax.ShapeDtypeStruct((s, D), F32), jax.ShapeDtypeStruct((s, 2 * F), BF16),
                   jax.ShapeDtypeStruct((1, D), F32), jax.ShapeDtypeStruct((4, 8, F), F32)],
        scratch_shapes=[pltpu.VMEM((FFN_HALO, F), F32), pltpu.VMEM((tm, FFN_PAIR), F32), pltpu.VMEM((tm, FFN_PAIR), F32),
                        pltpu.VMEM((tm + FFN_HALO, LANE), F32)],
        compiler_params=_params("arbitrary"), name=f"ffn{tag}_bwd")(dx_out, x_in, g, up, gate, w_up_t, w_dw, w_down)


CONV_HALO = 32
CONV_CHUNK = 256
CONV_ROWS = 64


def _ln(u2, g, b):
    mu = jnp.mean(u2, axis=-1, keepdims=True)
    xc = u2 - mu
    rstd = lax.rsqrt(jnp.mean(xc * xc, axis=-1, keepdims=True) + LN_EPS)
    xhat = xc * rstd
    return xhat, rstd, xhat * g + b


def _phase_taps(ph):
    first = CONV_HALO - (CONV_K - 1)
    return [(k, first + k - ph) for k in range(CONV_K) if (first + k) % 8 == ph]


def _sum8(v):
    out = v[0:8]
    for j in range(8, v.shape[0], 8):
        out = out + v[j:j + 8]
    return out


def _store_phases(src_ref, sh_ref, tt, cols=slice(None)):
    sh_ref[0] = src_ref[:, cols]
    for ph in range(1, 8):
        sh_ref[ph, pl.ds(0, tt + CONV_HALO - 8), :] = src_ref[pl.ds(ph, tt + CONV_HALO - 8), cols]


def _conv_taps_fwd(w_ref, sh_ref, u2_ref, cs, tt):
    for base in range(0, tt, CONV_ROWS):
        acc = jnp.zeros((CONV_ROWS, CONV_CHUNK), F32) + w_ref[CONV_K:CONV_K + 1, cs]
        for ph in range(8):
            for k, off in _phase_taps(ph):
                acc = acc + w_ref[k:k + 1, cs] * sh_ref[ph, pl.ds(base + off, CONV_ROWS), :]
        u2_ref[pl.ds(base, CONV_ROWS), cs] = acc.astype(BF16)


def _conv_fwd(x_in, g, w1t, b1, wdw, ln_g, ln_b, w2, b2, *, tt):
    s = x_in.shape[0]
    tt = min(tt, s)
    n_c = D // CONV_CHUNK

    def body(x_ref, g_ref, w1_ref, b1_ref, w_ref, lg_ref, lb_ref, w2_ref, b2_ref,
             o_ref, h_ref, a_ref, u2_ref, s_ref, carry_ref, ext_ref, sh_ref):
        @pl.when(pl.program_id(0) == 0)
        def _():
            carry_ref[...] = jnp.zeros_like(carry_ref)

        xf = x_ref[...]
        r = lax.rsqrt(jnp.mean(xf * xf, axis=-1, keepdims=True) + RMS_EPS)
        h = ((xf * r) * g_ref[...]).astype(BF16)
        h_ref[...] = h

        def pw1_mm(c):
            for off in (c * CONV_CHUNK, D + c * CONV_CHUNK):
                rs = slice(off, off + CONV_CHUNK)
                a_ref[:, rs] = (lax.dot_general(h, w1_ref[rs, :], NT_DIMS, preferred_element_type=F32)
                                + b1_ref[:, rs]).astype(BF16)

        pw1_mm(0)
        for c in range(n_c):
            if c + 1 < n_c:
                pw1_mm(c + 1)
            cs = slice(c * CONV_CHUNK, (c + 1) * CONV_CHUNK)
            gs = slice(D + c * CONV_CHUNK, D + (c + 1) * CONV_CHUNK)
            u = a_ref[:, cs].astype(F32) * _sigmoid(a_ref[:, gs].astype(F32))
            ext_ref[pl.ds(0, CONV_HALO), :] = carry_ref[:, cs]
            ext_ref[pl.ds(CONV_HALO, tt), :] = u
            carry_ref[:, cs] = u[tt - CONV_HALO:]
            _store_phases(ext_ref, sh_ref, tt)
            _conv_taps_fwd(w_ref, sh_ref, u2_ref, cs, tt)
        _, _, ln = _ln(u2_ref[...].astype(F32), lg_ref[...], lb_ref[...])
        sv = (ln * _sigmoid(ln)).astype(BF16)
        s_ref[...] = sv
        o_ref[...] = xf + lax.dot_general(sv, w2_ref[...], NN_DIMS, preferred_element_type=F32) + b2_ref[...]

    row = lambda i: (i, 0)
    return pl.pallas_call(
        body, grid=(s // tt,),
        in_specs=[pl.BlockSpec((tt, D), row), _resident((1, D)), _resident((2 * D, D)), _resident((1, 2 * D)),
                  _resident((32, D)), _resident((1, D)), _resident((1, D)), _resident((D, D)), _resident((1, D))],
        out_specs=[pl.BlockSpec((tt, D), row), pl.BlockSpec((tt, D), row), pl.BlockSpec((tt, 2 * D), row),
                   pl.BlockSpec((tt, D), row), pl.BlockSpec((tt, D), row)],
        out_shape=[jax.ShapeDtypeStruct((s, D), F32), jax.ShapeDtypeStruct((s, D), BF16),
                   jax.ShapeDtypeStruct((s, 2 * D), BF16), jax.ShapeDtypeStruct((s, D), BF16),
                   jax.ShapeDtypeStruct((s, D), BF16)],
        scratch_shapes=[pltpu.VMEM((CONV_HALO, D), F32), pltpu.VMEM((tt + CONV_HALO, CONV_CHUNK), F32),
                        pltpu.VMEM((8, tt + CONV_HALO, CONV_CHUNK), F32)],
        compiler_params=_params("arbitrary"), name="conv_fwd")(x_in, g, w1t, b1, wdw, ln_g, ln_b, w2, b2)


def _conv_bwd(dx_out, x_in, g, a, u2, w1t, wdw, ln_g, ln_b, w2, *, tt, ex=None, ex_args=()):
    s = x_in.shape[0]
    tt = min(tt, s)
    hb = tt // CONV_HALO
    n_i = s // tt
    te = tt + CONV_HALO
    n_c = D // CONV_CHUNK

    def body(dx_ref, x_ref, g_ref, a_ref, prev_ref, u2_ref, w1_ref, w_ref, lg_ref, lb_ref, w2_ref,
             dxin_ref, da_ref, dg_ref, acc_ref, db1_ref, db2_ref,
             carry_ref, e2_ref, ext_ref, ush_ref, dsh_ref, du_ref):
        i = pl.program_id(0)

        @pl.when(i == 0)
        def _():
            for ref in (dg_ref, acc_ref, db1_ref, db2_ref, carry_ref):
                ref[...] = jnp.zeros_like(ref)

        dxf = dx_ref[...]
        db2_ref[...] += _sum8(dxf)
        dse = lax.dot_general(dxf.astype(BF16), w2_ref[...], NT_DIMS, preferred_element_type=F32)
        gv = lg_ref[...]
        xhat, rstd, ln = _ln(u2_ref[...].astype(F32), gv, lb_ref[...])
        sg = _sigmoid(ln)
        dln = dse * (sg * (1.0 + ln * (1.0 - sg)))
        acc_ref[32] += _sum8(dln * xhat)
        acc_ref[33] += _sum8(dln)
        dxh = dln * gv
        du2 = rstd * (dxh - jnp.mean(dxh, axis=-1, keepdims=True) - xhat * jnp.mean(dxh * xhat, axis=-1, keepdims=True))
        e2_ref[pl.ds(0, tt), :] = du2
        e2_ref[pl.ds(tt, CONV_HALO), :] = carry_ref[...]
        carry_ref[...] = du2[:CONV_HALO]
        first_tile = i == n_i - 1

        def dh_mm(c):
            cs = slice(c * CONV_CHUNK, (c + 1) * CONV_CHUNK)
            gs = slice(D + c * CONV_CHUNK, D + (c + 1) * CONV_CHUNK)
            return (lax.dot_general(da_ref[:, cs], w1_ref[cs, :], NN_DIMS, preferred_element_type=F32)
                    + lax.dot_general(da_ref[:, gs], w1_ref[gs, :], NN_DIMS, preferred_element_type=F32))

        dh = None
        for c in range(n_c):
            if c >= 1:
                part = dh_mm(c - 1)
                dh = part if dh is None else dh + part
            cs = slice(c * CONV_CHUNK, (c + 1) * CONV_CHUNK)
            gs = slice(D + c * CONV_CHUNK, D + (c + 1) * CONV_CHUNK)
            uh = prev_ref[:, cs].astype(F32) * _sigmoid(prev_ref[:, gs].astype(F32))
            ext_ref[pl.ds(0, CONV_HALO), :] = jnp.where(first_tile, 0.0, uh)
            a1 = a_ref[:, cs].astype(F32)
            sg2 = _sigmoid(a_ref[:, gs].astype(F32))
            ext_ref[pl.ds(CONV_HALO, tt), :] = a1 * sg2
            _store_phases(ext_ref, ush_ref, tt)
            _store_phases(e2_ref, dsh_ref, tt, cs)
            for base in range(0, tt, CONV_ROWS):
                d0 = e2_ref[pl.ds(base, CONV_ROWS), cs]
                du = jnp.zeros((CONV_ROWS, CONV_CHUNK), F32)
                for ph in range(8):
                    for k, off in _phase_taps(ph):
                        acc_ref[k, :, cs] += _sum8(d0 * ush_ref[ph, pl.ds(base + off, CONV_ROWS), :])
                    for k in range(CONV_K):
                        if (CONV_K - 1 - k) % 8 == ph:
                            du = du + w_ref[k:k + 1, cs] * dsh_ref[ph, pl.ds(base + CONV_K - 1 - k - ph, CONV_ROWS), :]
                acc_ref[CONV_K, :, cs] += _sum8(d0)
                du_ref[pl.ds(base, CONV_ROWS), :] = du
            du = du_ref[...]
            da1 = du * sg2
            da2 = du * a1 * (sg2 * (1.0 - sg2))
            da_ref[:, cs] = da1.astype(BF16)
            da_ref[:, gs] = da2.astype(BF16)
            db1_ref[:, cs] += _sum8(da1)
            db1_ref[:, gs] += _sum8(da2)
        dh = dh + dh_mm(n_c - 1)
        dx, dg_rows = _rms_bwd(dh, x_ref[...], g_ref[...])
        dxin_ref[...] = dxf + dx
        dg_ref[...] += jnp.sum(dg_rows, axis=0, keepdims=True)

    rev = lambda i: (n_i - 1 - i, 0)
    return pl.pallas_call(
        _carried(body, 11, 6, ex, n_i), grid=(n_i,),
        in_specs=[pl.BlockSpec((tt, D), rev), pl.BlockSpec((tt, D), rev), _resident((1, D)),
                  pl.BlockSpec((tt, 2 * D), rev),
                  pl.BlockSpec((CONV_HALO, 2 * D), lambda i: (jnp.maximum((n_i - 1 - i) * hb - 1, 0), 0)),
                  pl.BlockSpec((tt, D), rev), _resident((2 * D, D)), _resident((32, D)), _resident((1, D)),
                  _resident((1, D)), _resident((D, D))] + _ex(ex, "in_specs"),
        out_specs=[pl.BlockSpec((tt, D), rev), pl.BlockSpec((tt, 2 * D), rev), pl.BlockSpec((1, D), lambda i: (0, 0)),
                   pl.BlockSpec((40, 8, D), lambda i: (0, 0, 0)), pl.BlockSpec((8, 2 * D), lambda i: (0, 0)),
                   pl.BlockSpec((8, D), lambda i: (0, 0))] + _ex(ex, "out_specs"),
        out_shape=[jax.ShapeDtypeStruct((s, D), F32), jax.ShapeDtypeStruct((s, 2 * D), BF16),
                   jax.ShapeDtypeStruct((1, D), F32), jax.ShapeDtypeStruct((40, 8, D), F32),
                   jax.ShapeDtypeStruct((8, 2 * D), F32), jax.ShapeDtypeStruct((8, D), F32)] + _ex(ex, "out_shape"),
        scratch_shapes=[pltpu.VMEM((CONV_HALO, D), F32), pltpu.VMEM((te, D), F32), pltpu.VMEM((te, CONV_CHUNK), F32),
                        pltpu.VMEM((8, te, CONV_CHUNK), F32), pltpu.VMEM((8, te, CONV_CHUNK), F32),
                        pltpu.VMEM((tt, CONV_CHUNK), F32)] + _ex(ex, "scratch"),
        compiler_params=_params("arbitrary"), name="conv_bwd")(dx_out, x_in, g, a, a, u2, w1t, wdw, ln_g, ln_b, w2, *ex_args)


def _adamw(name, w, g, m, v):
    rows, cols = w.shape
    tr = _row_tile(rows, 256)

    def body(w_ref, g_ref, m_ref, v_ref, d_ref, nm_ref, nv_ref):
        gv = g_ref[...]
        m2 = ADAM_B1 * m_ref[...] + (1.0 - ADAM_B1) * gv
        v2 = ADAM_B2 * v_ref[...] + (1.0 - ADAM_B2) * (gv * gv)
        m_hat = m2 / (1.0 - ADAM_B1 ** ADAM_STEP)
        v_hat = v2 / (1.0 - ADAM_B2 ** ADAM_STEP)
        d_ref[...] = -ADAM_LR * (m_hat / (jnp.sqrt(v_hat) + ADAM_EPS) + ADAM_WD * w_ref[...])
        nm_ref[...] = m2
        nv_ref[...] = v2

    spec = pl.BlockSpec((tr, cols), lambda i: (i, 0))
    return pl.pallas_call(
        body, grid=(rows // tr,), in_specs=[spec] * 4, out_specs=[spec] * 3,
        out_shape=[jax.ShapeDtypeStruct((rows, cols), F32)] * 3,
        compiler_params=_params("parallel"), name=name)(w, g, m, v)


def _sum_slabs(name, buf):
    rows = buf.shape[1]
    tr = _row_tile(rows, 512, 16)

    def body(b_ref, o_ref):
        acc = b_ref[0].astype(F32)
        for k in range(1, N_DEV):
            acc = acc + b_ref[k].astype(F32)
        o_ref[...] = acc

    return pl.pallas_call(
        body, grid=(rows // tr,), in_specs=[pl.BlockSpec((N_DEV, tr, D), lambda i: (0, i, 0))],
        out_specs=pl.BlockSpec((tr, D), lambda i: (i, 0)), out_shape=jax.ShapeDtypeStruct((rows, D), F32),
        compiler_params=_params("parallel"), name=name)(buf)


def _ex(ex, field):
    return list(getattr(ex, field)) if ex is not None else []


def _me():
    x, y, c = lax.axis_index("x"), lax.axis_index("y"), lax.axis_index("c")
    return x, y, c, 4 * x + 2 * y + c


def _peer(x, y, c, k):
    return (x ^ (k >> 2), y ^ ((k >> 1) & 1), c ^ (k & 1))


class _Gather:
    def __init__(self, mats, slab_base=0):
        self.mats = tuple(mats)
        self.slab_base = slab_base
        self.rows = sum(MAT_ROWS[i] for i in self.mats)
        any_spec = pl.BlockSpec(memory_space=pl.ANY)
        self.n_in, self.n_out = 1, len(self.mats)
        self.in_specs = [any_spec]
        self.out_specs = [any_spec] * self.n_out
        self.out_shape = [jax.ShapeDtypeStruct((N_DEV * MAT_ROWS[i], D), BF16) for i in self.mats]
        self.scratch = [pltpu.SemaphoreType.DMA((N_DEV - 1,)), pltpu.SemaphoreType.DMA((N_DEV - 1,)),
                        pltpu.SemaphoreType.DMA]
        assert max(s.shape[0] for s in self.out_shape) >= self.rows

    def start(self, ins, outs, scr):
        (slab_ref,), (send_sems, recv_sems, local_sem) = ins, scr
        x, y, c, me = _me()
        for n, i in enumerate(self.mats):
            src = slab_ref.at[pl.ds(MAT_OFF[i] - self.slab_base, MAT_ROWS[i])]
            dst = outs[n].at[pl.ds(me * MAT_ROWS[i], MAT_ROWS[i])]
            pltpu.make_async_copy(src, dst, local_sem).start()
            for k in range(1, N_DEV):
                pltpu.make_async_remote_copy(src_ref=src, dst_ref=dst, send_sem=send_sems.at[k - 1],
                                             recv_sem=recv_sems.at[k - 1], device_id=_peer(x, y, c, k),
                                             device_id_type=MESH).start()

    def wait(self, ins, outs, scr):
        send_sems, recv_sems, local_sem = scr
        x, y, c, _ = _me()
        big = max(range(self.n_out), key=lambda n: self.out_shape[n].shape[0])
        whole = outs[big].at[pl.ds(0, self.rows)]
        for k in range(1, N_DEV):
            pltpu.make_async_remote_copy(src_ref=whole, dst_ref=whole, send_sem=send_sems.at[k - 1],
                                         recv_sem=recv_sems.at[k - 1], device_id=_peer(x, y, c, k),
                                         device_id_type=MESH).wait()
        pltpu.make_async_copy(whole, whole, local_sem).wait()


class _Scatter:
    def __init__(self, mats):
        self.mats = tuple(mats)
        self.rows = sum(MAT_ROWS[i] for i in self.mats)
        any_spec = pl.BlockSpec(memory_space=pl.ANY)
        self.n_in, self.n_out = len(self.mats), 1
        self.in_specs = [any_spec] * self.n_in
        self.out_specs = [any_spec]
        self.out_shape = [jax.ShapeDtypeStruct((N_DEV, self.rows, D), BF16)]
        self.scratch = [pltpu.SemaphoreType.DMA((N_DEV - 1,)), pltpu.SemaphoreType.DMA((N_DEV - 1,)),
                        pltpu.SemaphoreType.DMA]

    def offsets(self):
        return [sum(MAT_ROWS[i] for i in self.mats[:n]) for n in range(len(self.mats))]

    def start(self, ins, outs, scr):
        (buf_ref,), (send_sems, recv_sems, local_sem) = outs, scr
        x, y, c, me = _me()
        for n, (i, off) in enumerate(zip(self.mats, self.offsets())):
            r = MAT_ROWS[i]
            pltpu.make_async_copy(ins[n].at[pl.ds(me * r, r)], buf_ref.at[0, pl.ds(off, r)], local_sem).start()
            for k in range(1, N_DEV):
                pltpu.make_async_remote_copy(src_ref=ins[n].at[pl.ds((me ^ k) * r, r)], dst_ref=buf_ref.at[k, pl.ds(off, r)],
                                             send_sem=send_sems.at[k - 1], recv_sem=recv_sems.at[k - 1],
                                             device_id=_peer(x, y, c, k), device_id_type=MESH).start()

    def wait(self, ins, outs, scr):
        (buf_ref,), (send_sems, recv_sems, local_sem) = outs, scr
        x, y, c, _ = _me()
        whole = buf_ref.at[0]
        for k in range(1, N_DEV):
            pltpu.make_async_remote_copy(src_ref=whole, dst_ref=whole, send_sem=send_sems.at[k - 1],
                                         recv_sem=recv_sems.at[k - 1], device_id=_peer(x, y, c, k),
                                         device_id_type=MESH).wait()
        pltpu.make_async_copy(whole, whole, local_sem).wait()


def _carried(body, n_in, n_out, ex, n_steps):
    if ex is None:
        return body

    def wrapped(*refs):
        ins, ex_ins = refs[:n_in], refs[n_in:n_in + ex.n_in]
        p = n_in + ex.n_in
        outs, ex_outs = refs[p:p + n_out], refs[p + n_out:p + n_out + ex.n_out]
        p += n_out + ex.n_out
        n_scr = len(refs) - p - len(ex.scratch)
        scr, ex_scr = refs[p:p + n_scr], refs[p + n_scr:]

        @pl.when(pl.program_id(0) == 0)
        def _():
            ex.start(ex_ins, ex_outs, ex_scr)

        body(*ins, *outs, *scr)

        @pl.when(pl.program_id(0) == n_steps - 1)
        def _():
            ex.wait(ex_ins, ex_outs, ex_scr)

    return wrapped


def _exchange_small(name, ex, ex_args, small, reduce):
    vmem_spec = pl.BlockSpec(memory_space=pltpu.VMEM)
    n_small = small.shape[0]

    def body(*refs):
        ex_ins, small_ref = refs[:ex.n_in], refs[ex.n_in]
        p = ex.n_in + 1
        ex_outs, res_ref = refs[p:p + ex.n_out], refs[p + ex.n_out]
        p += ex.n_out + 1
        if reduce:
            all_ref, p = refs[p], p + 1
        else:
            all_ref = res_ref
        sm_send, sm_recv = refs[p], refs[p + 1]
        ex_scr = refs[p + 2:]
        x, y, c, me = _me()
        ex.start(ex_ins, ex_outs, ex_scr)
        copies = [pltpu.make_async_remote_copy(
            src_ref=small_ref, dst_ref=all_ref.at[me], send_sem=sm_send.at[k - 1], recv_sem=sm_recv.at[k - 1],
            device_id=_peer(x, y, c, k), device_id_type=MESH) for k in range(1, N_DEV)]
        for cp in copies:
            cp.start()
        all_ref[me] = small_ref[...]
        for cp in copies:
            cp.wait()
        if reduce:
            acc = all_ref[0]
            for d in range(1, N_DEV):
                acc = acc + all_ref[d]
            res_ref[...] = acc
        ex.wait(ex_ins, ex_outs, ex_scr)

    res_shape = (n_small, D) if reduce else (N_DEV, n_small, D)
    scratch = ([pltpu.VMEM((N_DEV, n_small, D), F32)] if reduce else []) + [
        pltpu.SemaphoreType.DMA((N_DEV - 1,)), pltpu.SemaphoreType.DMA((N_DEV - 1,))] + ex.scratch
    outs = pl.pallas_call(
        body, in_specs=ex.in_specs + [vmem_spec], out_specs=ex.out_specs + [vmem_spec],
        out_shape=ex.out_shape + [jax.ShapeDtypeStruct(res_shape, F32)], scratch_shapes=scratch,
        compiler_params=pltpu.CompilerParams(vmem_limit_bytes=VMEM_LIMIT_BYTES), name=name)(*ex_args, small)
    return outs


def _pack_rows(arrs):
    rows = []
    for a in arrs:
        flat = a.reshape(-1).astype(F32)
        pad = (-flat.shape[0]) % D
        rows.append(jnp.pad(flat, (0, pad)).reshape(-1, D))
    slab = jnp.concatenate(rows, axis=0)
    return jnp.pad(slab, ((0, (-slab.shape[0]) % 8), (0, 0)))


def _unpack_rows(slab, shapes):
    out, r = [], 0
    for shp in shapes:
        size = math.prod(shp)
        nrows = -(-size // D)
        out.append(slab[r:r + nrows].reshape(-1)[:size].reshape(shp))
        r += nrows
    return out


def _unpack_gathered(slabs, shapes):
    out, r = [], 0
    for shp in shapes:
        size = math.prod(shp)
        nrows = -(-size // D)
        blocks = slabs[:, r:r + nrows].reshape(N_DEV, -1)[:, :size].reshape((N_DEV,) + tuple(shp))
        out.append(jnp.moveaxis(blocks, 0, -2).reshape(tuple(shp[:-1]) + (N_DEV * shp[-1],)))
        r += nrows
    return out


def kernel(x, norm_mix, attn_w_qkv, attn_b_qkv, attn_sinks, attn_w_o, attn_b_o, conv_w_pw1, conv_b_pw1, conv_w_dw, conv_b_dw, conv_ln_g, conv_ln_b, conv_w_pw2, conv_b_pw2, norm_ffn, ffn_w_up, ffn_w_dw, ffn_b_dw, ffn_w_down, final_norm, loss_target, m_norm_mix, m_attn_w_qkv, m_attn_b_qkv, m_attn_sinks, m_attn_w_o, m_attn_b_o, m_conv_w_pw1, m_conv_b_pw1, m_conv_w_dw, m_conv_b_dw, m_conv_ln_g, m_conv_ln_b, m_conv_w_pw2, m_conv_b_pw2, m_norm_ffn, m_ffn_w_up, m_ffn_w_dw, m_ffn_b_dw, m_ffn_w_down, m_final_norm, v_norm_mix, v_attn_w_qkv, v_attn_b_qkv, v_attn_sinks, v_attn_w_o, v_attn_b_o, v_conv_w_pw1, v_conv_b_pw1, v_conv_w_dw, v_conv_b_dw, v_conv_ln_g, v_conv_ln_b, v_conv_w_pw2, v_conv_b_pw2, v_norm_ffn, v_ffn_w_up, v_ffn_w_dw, v_ffn_b_dw, v_ffn_w_down, v_final_norm):
    s = x.shape[1]
    me = 4 * lax.axis_index("x") + 2 * lax.axis_index("y") + lax.axis_index("c")
    x0 = x.reshape(s, D)
    tgt = loss_target.reshape(s, D)

    slab_qkv = attn_w_qkv[0].T.astype(BF16)
    slab = jnp.concatenate([attn_w_o[0], conv_w_pw1[0].T, conv_w_pw2[0],
                            ffn_w_up[0].T, ffn_w_down[0], ffn_w_up[1].T, ffn_w_down[1]], axis=0).astype(BF16)
    sharded_small = [conv_b_pw1, conv_w_dw, conv_b_dw, conv_ln_g, conv_ln_b, conv_b_pw2, ffn_w_dw]
    small_local = _pack_rows(sharded_small)
    w_qkv_t, small_all = _exchange_small("gather_first", _Gather((0,)), (slab_qkv,), small_local, reduce=False)
    b_pw1, w_cdw, b_cdw, ln_g, ln_b, b_pw2, w_fdw = _unpack_gathered(small_all, [a.shape for a in sharded_small])
    conv_w32 = jnp.concatenate([w_cdw[0], b_cdw], axis=0)
    score_bias = _score_bias(attn_sinks)

    qkv, h0 = _mm("qkv", x0, w_qkv_t, nt=True, tm=2048, tn=NQ + KVW, out_dtype=BF16, rms_g=norm_mix[0:1], bias=attn_b_qkv)
    o, lse, w_o, w_up0_t, w_dn0 = _attn_fwd(qkv, score_bias, _Gather((1, 4, 5), MAT_OFF[1]), (slab,))
    x2, h1, up0, act0, gate0, x1, w_pw1_t, w_pw2, w_up1_t, w_dn1 = _ffn_fwd(
        0, x0, norm_ffn[0:1], w_up0_t, w_fdw[0], ffn_b_dw[0:1], w_dn0, tm=256, pre=(o, w_o, attn_b_o),
        ex=_Gather((2, 3, 6, 7), MAT_OFF[1]), ex_args=(slab,))
    x3, h2, a, u2, s_act = _conv_fwd(x2, norm_mix[1:2], w_pw1_t, b_pw1, conv_w32, ln_g, ln_b, w_pw2, b_pw2, tt=256)
    dx4, h3, up1, act1, gate1, loss_acc, dg_final = _ffn_fwd(1, x3, norm_ffn[1:2], w_up1_t, w_fdw[1], ffn_b_dw[1:2], w_dn1,
                                                             tm=256, head=(final_norm.reshape(1, D), tgt))
    loss = lax.psum(loss_acc[0, 0], ("x", "y", "c"))

    dx3, d_up1, dg_f1, dwb1 = _ffn_bwd(1, dx4, x3, norm_ffn[1:2], w_up1_t, w_fdw[1], w_dn1, up1, gate1, tm=256)
    dw_dn1 = _mm_tn("ffn1_dwdown", act1, dx4, tn=FFN_HALF, tt=2048)
    dw_up1_t = _mm_tn("ffn1_dwup", d_up1, h3, tn=FFN_HALF, tt=2048)
    dw_pw2 = _mm_tn("dw_pw2", s_act, dx3, tn=D, tt=2048)
    scatter_a = _Scatter((6, 7))
    dx2, da, dg_m1, conv_acc8, db_pw1_8, db_pw2_8, buf_a = _conv_bwd(
        dx3, x2, norm_mix[1:2], a, u2, w_pw1_t, conv_w32, ln_g, ln_b, w_pw2, tt=256, ex=scatter_a, ex_args=(dw_up1_t, dw_dn1))
    conv_acc = jnp.sum(conv_acc8, axis=1)
    db_pw1 = jnp.sum(db_pw1_8, axis=0, keepdims=True)
    db_pw2 = jnp.sum(db_pw2_8, axis=0, keepdims=True)
    dw_pw1_t = _mm_tn("dw_pw1", da, h2, tn=2 * D, tt=2048)
    dx1, d_up0, dg_f0, dwb0 = _ffn_bwd(0, dx2, x1, norm_ffn[0:1], w_up0_t, w_fdw[0], w_dn0, up0, gate0, tm=256)
    dw_dn0 = _mm_tn("ffn0_dwdown", act0, dx2, tn=FFN_HALF, tt=2048)
    dw_up0_t = _mm_tn("ffn0_dwup", d_up0, h1, tn=FFN_HALF, tt=2048)
    do = _mm("d_attn_out", dx1, w_o, nt=True, tm=1024, tn=D, out_dtype=BF16)
    dw_o, db_o = _mm_tn("dw_o", o, dx1, tn=D, tt=2048, colsum_r=True)
    scatter_b = _Scatter((1, 2, 3, 4, 5))
    dq, dkv, dsk, buf_b = _attn_bwd(qkv, do, o, lse, score_bias, scatter_b, (dw_o, dw_pw1_t, dw_pw2, dw_up0_t, dw_dn0))
    dqkv = jnp.concatenate([dq, dkv], axis=1)
    grad_x, dg_m0 = _dgrad_rms("d_qkv", dqkv, w_qkv_t, x0, norm_mix[0:1], dx1, tm=1024, tk=NQ + KVW)
    dw_qkv_t, db_qkv = _mm_tn("dw_qkv", dqkv, h0, tn=NQ + KVW, tt=2048, colsum_l=True)

    dwb0, dwb1 = jnp.sum(dwb0, axis=1), jnp.sum(dwb1, axis=1)
    d_sinks = jnp.transpose(jnp.sum(dsk.reshape(N_KV, 2, 4, BLOCK), axis=-1), (0, 2, 1)).reshape(1, 16)
    small_grads = [jnp.concatenate([dg_m0, dg_m1], axis=0), db_qkv, d_sinks, db_o, jnp.concatenate([dg_f0, dg_f1], axis=0),
                   jnp.stack([dwb0[3], dwb1[3]]), dg_final, db_pw1, conv_acc[0:CONV_K], conv_acc[31:32], conv_acc[32:33],
                   conv_acc[33:34], db_pw2, jnp.stack([dwb0[0:3], dwb1[0:3]])]
    small_shapes = [(2, D), (1, NQ + KVW), (1, 16), (1, D), (2, D), (2, F), (D,), (1, 2 * D), (1, CONV_K, D), (1, D),
                    (1, D), (1, D), (1, D), (2, 3, F)]
    scatter_c = _Scatter((0,))
    buf_c, small_sum = _exchange_small("scatter_last", scatter_c, (dw_qkv_t,), _pack_rows(small_grads), reduce=True)
    gm = [None] * N_MAT
    for tag, sc, buf in (("a", scatter_a, buf_a), ("b", scatter_b, buf_b), ("c", scatter_c, buf_c)):
        gslab = _sum_slabs("sum_slabs_" + tag, buf)
        for i, off in zip(sc.mats, sc.offsets()):
            gm[i] = gslab[off:off + MAT_ROWS[i]]
    (g_norm_mix, g_b_qkv, g_sinks, g_b_o, g_norm_ffn, g_ffn_b_dw, g_final, g_b_pw1_full, g_w_cdw_full, g_b_cdw_full,
     g_ln_g_full, g_ln_b_full, g_b_pw2_full, g_w_fdw_full) = _unpack_rows(small_sum, small_shapes)

    def shard(a, width):
        return lax.dynamic_slice_in_dim(a, me * width, width, axis=a.ndim - 1)

    grads = {
        "norm_mix": g_norm_mix, "attn_w_qkv": gm[0].T[None], "attn_b_qkv": g_b_qkv, "attn_sinks": g_sinks,
        "attn_w_o": gm[1][None], "attn_b_o": g_b_o, "conv_w_pw1": gm[2].T[None], "conv_b_pw1": shard(g_b_pw1_full, 256),
        "conv_w_dw": shard(g_w_cdw_full, 128), "conv_b_dw": shard(g_b_cdw_full, 128), "conv_ln_g": shard(g_ln_g_full, 128),
        "conv_ln_b": shard(g_ln_b_full, 128), "conv_w_pw2": gm[3][None], "conv_b_pw2": shard(g_b_pw2_full, 128),
        "norm_ffn": g_norm_ffn, "ffn_w_up": jnp.stack([gm[4].T, gm[6].T]), "ffn_w_dw": shard(g_w_fdw_full, 352),
        "ffn_b_dw": g_ffn_b_dw, "ffn_w_down": jnp.stack([gm[5], gm[7]]), "final_norm": g_final,
    }
    weights = dict(norm_mix=norm_mix, attn_w_qkv=attn_w_qkv, attn_b_qkv=attn_b_qkv, attn_sinks=attn_sinks, attn_w_o=attn_w_o, attn_b_o=attn_b_o, conv_w_pw1=conv_w_pw1, conv_b_pw1=conv_b_pw1, conv_w_dw=conv_w_dw, conv_b_dw=conv_b_dw, conv_ln_g=conv_ln_g, conv_ln_b=conv_ln_b, conv_w_pw2=conv_w_pw2, conv_b_pw2=conv_b_pw2, norm_ffn=norm_ffn, ffn_w_up=ffn_w_up, ffn_w_dw=ffn_w_dw, ffn_b_dw=ffn_b_dw, ffn_w_down=ffn_w_down, final_norm=final_norm)
    moms = dict(norm_mix=m_norm_mix, attn_w_qkv=m_attn_w_qkv, attn_b_qkv=m_attn_b_qkv, attn_sinks=m_attn_sinks, attn_w_o=m_attn_w_o, attn_b_o=m_attn_b_o, conv_w_pw1=m_conv_w_pw1, conv_b_pw1=m_conv_b_pw1, conv_w_dw=m_conv_w_dw, conv_b_dw=m_conv_b_dw, conv_ln_g=m_conv_ln_g, conv_ln_b=m_conv_ln_b, conv_w_pw2=m_conv_w_pw2, conv_b_pw2=m_conv_b_pw2, norm_ffn=m_norm_ffn, ffn_w_up=m_ffn_w_up, ffn_w_dw=m_ffn_w_dw, ffn_b_dw=m_ffn_b_dw, ffn_w_down=m_ffn_w_down, final_norm=m_final_norm)
    vars_ = dict(norm_mix=v_norm_mix, attn_w_qkv=v_attn_w_qkv, attn_b_qkv=v_attn_b_qkv, attn_sinks=v_attn_sinks, attn_w_o=v_attn_w_o, attn_b_o=v_attn_b_o, conv_w_pw1=v_conv_w_pw1, conv_b_pw1=v_conv_b_pw1, conv_w_dw=v_conv_w_dw, conv_b_dw=v_conv_b_dw, conv_ln_g=v_conv_ln_g, conv_ln_b=v_conv_ln_b, conv_w_pw2=v_conv_w_pw2, conv_b_pw2=v_conv_b_pw2, norm_ffn=v_norm_ffn, ffn_w_up=v_ffn_w_up, ffn_w_dw=v_ffn_w_dw, ffn_b_dw=v_ffn_b_dw, ffn_w_down=v_ffn_w_down, final_norm=v_final_norm)
    names = list(weights)
    delta, new_m, new_v = {}, {}, {}
    for nm in names:
        shp = weights[nm].shape
        two_d = (math.prod(shp[:-1]), shp[-1])
        res = _adamw("adamw_" + nm, *(t[nm].reshape(two_d) for t in (weights, grads, moms, vars_)))
        delta[nm], new_m[nm], new_v[nm] = (r.reshape(shp) for r in res)
    return (loss, grad_x.reshape(1, s, D), *[grads[nm] for nm in names], *[delta[nm] for nm in names],
            *[new_m[nm] for nm in names], *[new_v[nm] for nm in names])
```

```python
import math

import jax
import jax.numpy as jnp
from jax import lax
from jax.experimental import pallas as pl
from jax.experimental.pallas import tpu as pltpu

F32 = jnp.float32
BF16 = jnp.bfloat16

D = 1024
F = 2816
HEAD_DIM = 64
N_KV = 2
BLOCK = 128
NQ = 1024
KVW = 256
CONV_K = 31
RMS_EPS = 1e-6
LN_EPS = 1e-5
ADAM_LR, ADAM_B1, ADAM_B2, ADAM_EPS, ADAM_WD, ADAM_STEP = 0.001, 0.9, 0.999, 1e-08, 0.01, 10
N_DEV = 8
MESH = pl.DeviceIdType.MESH
VMEM_LIMIT_BYTES = 56 * 2**20
MASKED = -1e30

MAT_ROWS = (160, 128, 256, 128, 704, 352, 704, 352)
MAT_OFF = tuple(sum(MAT_ROWS[:i]) for i in range(len(MAT_ROWS)))
SLAB_ROWS = sum(MAT_ROWS)
N_MAT = len(MAT_ROWS)

NT_DIMS = (((1,), (1,)), ((), ()))
NN_DIMS = (((1,), (0,)), ((), ()))
TN_DIMS = (((0,), (0,)), ((), ()))


def _params(*sem):
    return pltpu.CompilerParams(dimension_semantics=tuple(sem), vmem_limit_bytes=VMEM_LIMIT_BYTES)


def _row_tile(rows, cap, mult=8):
    best = None
    for t in range(mult, min(rows, cap) + 1, mult):
        if rows % t == 0:
            best = t
    return best if best is not None else rows


def _sigmoid(x):
    return 1.0 / (1.0 + jnp.exp(-x))


def _mm(name, a, b, *, nt, tm, tn, out_dtype, rms_g=None, bias=None):
    m, k = a.shape
    n = b.shape[0] if nt else b.shape[1]
    tm = min(tm, m)
    has_rms = rms_g is not None
    dims = NT_DIMS if nt else NN_DIMS

    def body(*refs):
        a_ref, b_ref = refs[0], refs[1]
        pos = 2
        g_ref = bias_ref = h_ref = hs_ref = None
        if has_rms:
            g_ref = refs[pos]; pos += 1
        if bias is not None:
            bias_ref = refs[pos]; pos += 1
        o_ref = refs[pos]; pos += 1
        if has_rms:
            h_ref, hs_ref = refs[pos], refs[pos + 1]

            @pl.when(pl.program_id(1) == 0)
            def _():
                xf = a_ref[...]
                r = lax.rsqrt(jnp.mean(xf * xf, axis=-1, keepdims=True) + RMS_EPS)
                h = ((xf * r) * g_ref[...]).astype(BF16)
                hs_ref[...] = h
                h_ref[...] = h

            lhs = hs_ref[...]
        else:
            lhs = a_ref[...].astype(BF16)
        acc = lax.dot_general(lhs, b_ref[...], dims, preferred_element_type=F32)
        if bias is not None:
            acc = acc + bias_ref[...]
        o_ref[...] = acc.astype(out_dtype)

    in_specs = [pl.BlockSpec((tm, k), lambda i, j: (i, 0)),
                pl.BlockSpec((tn, k), lambda i, j: (j, 0)) if nt else pl.BlockSpec((k, tn), lambda i, j: (0, j))]
    args = [a, b]
    if has_rms:
        in_specs.append(pl.BlockSpec((1, k), lambda i, j: (0, 0))); args.append(rms_g)
    if bias is not None:
        in_specs.append(pl.BlockSpec((1, tn), lambda i, j: (0, j))); args.append(bias)
    out_shape = [jax.ShapeDtypeStruct((m, n), out_dtype)]
    out_specs = [pl.BlockSpec((tm, tn), lambda i, j: (i, j))]
    scratch = []
    if has_rms:
        out_shape.append(jax.ShapeDtypeStruct((m, k), BF16))
        out_specs.append(pl.BlockSpec((tm, k), lambda i, j: (i, 0)))
        scratch.append(pltpu.VMEM((tm, k), BF16))
    outs = pl.pallas_call(body, grid=(m // tm, n // tn), in_specs=in_specs, out_specs=out_specs, out_shape=out_shape,
                          scratch_shapes=scratch, compiler_params=_params("parallel", "arbitrary"), name=name)(*args)
    return outs if has_rms else outs[0]


def _mm_tn(name, l, r, *, tn, tt, colsum_l=False, colsum_r=False):
    s, nl = l.shape
    nr = r.shape[1]
    tt = min(tt, s)
    n_t = s // tt

    def body(*refs):
        l_ref, r_ref, o_ref = refs[0], refs[1], refs[2]
        pos = 3
        cl_ref = cr_ref = None
        if colsum_l:
            cl_ref = refs[pos]; pos += 1
        if colsum_r:
            cr_ref = refs[pos]; pos += 1
        acc_ref = refs[pos]
        j, t = pl.program_id(0), pl.program_id(1)

        @pl.when(t == 0)
        def _():
            acc_ref[...] = jnp.zeros_like(acc_ref)

        lv = l_ref[...]
        rv = r_ref[...]
        acc_ref[...] += lax.dot_general(lv, rv.astype(BF16), TN_DIMS, preferred_element_type=F32)
        if colsum_l:
            @pl.when(t == 0)
            def _():
                cl_ref[...] = jnp.zeros_like(cl_ref)

            cl_ref[...] += jnp.sum(lv.astype(F32), axis=0, keepdims=True)
        if colsum_r:
            @pl.when((t == 0) & (j == 0))
            def _():
                cr_ref[...] = jnp.zeros_like(cr_ref)

            @pl.when(j == 0)
            def _():
                cr_ref[...] += jnp.sum(rv.astype(F32), axis=0, keepdims=True)

        @pl.when(t == n_t - 1)
        def _():
            o_ref[...] = acc_ref[...].astype(BF16)

    out_shape = [jax.ShapeDtypeStruct((nl, nr), BF16)]
    out_specs = [pl.BlockSpec((tn, nr), lambda j, t: (j, 0))]
    if colsum_l:
        out_shape.append(jax.ShapeDtypeStruct((1, nl), F32))
        out_specs.append(pl.BlockSpec((1, tn), lambda j, t: (0, j)))
    if colsum_r:
        out_shape.append(jax.ShapeDtypeStruct((1, nr), F32))
        out_specs.append(pl.BlockSpec((1, nr), lambda j, t: (0, 0)))
    outs = pl.pallas_call(
        body, grid=(nl // tn, n_t),
        in_specs=[pl.BlockSpec((tt, tn), lambda j, t: (t, j)), pl.BlockSpec((tt, nr), lambda j, t: (t, 0))],
        out_specs=out_specs, out_shape=out_shape, scratch_shapes=[pltpu.VMEM((tn, nr), F32)],
        compiler_params=_params("arbitrary", "arbitrary"), name=name)(l, r)
    return outs if (colsum_l or colsum_r) else outs[0]


def _rms_bwd(dh, xf, g):
    r = lax.rsqrt(jnp.mean(xf * xf, axis=-1, keepdims=True) + RMS_EPS)
    gd = dh * g
    dx = r * gd - xf * ((r * r * r) * jnp.mean(xf * gd, axis=-1, keepdims=True))
    return dx, dh * (xf * r)


def _dgrad_rms(name, dy, w, x, g, dres, *, tm, tk):
    m, k = dy.shape
    tm = min(tm, m)
    n_k = k // tk

    def body(dy_ref, w_ref, x_ref, g_ref, dres_ref, o_ref, dg_ref, acc_ref):
        i, kk = pl.program_id(0), pl.program_id(1)

        @pl.when(kk == 0)
        def _():
            acc_ref[...] = jnp.zeros_like(acc_ref)

        acc_ref[...] += lax.dot_general(dy_ref[...], w_ref[...], NN_DIMS, preferred_element_type=F32)

        @pl.when((i == 0) & (kk == n_k - 1))
        def _():
            dg_ref[...] = jnp.zeros_like(dg_ref)

        @pl.when(kk == n_k - 1)
        def _():
            dx, dg_rows = _rms_bwd(acc_ref[...], x_ref[...], g_ref[...])
            o_ref[...] = dres_ref[...] + dx
            dg_ref[...] += jnp.sum(dg_rows, axis=0, keepdims=True)

    return pl.pallas_call(
        body, grid=(m // tm, n_k),
        in_specs=[pl.BlockSpec((tm, tk), lambda i, kk: (i, kk)), pl.BlockSpec((tk, D), lambda i, kk: (kk, 0)),
                  pl.BlockSpec((tm, D), lambda i, kk: (i, 0)), pl.BlockSpec((1, D), lambda i, kk: (0, 0)),
                  pl.BlockSpec((tm, D), lambda i, kk: (i, 0))],
        out_specs=[pl.BlockSpec((tm, D), lambda i, kk: (i, 0)), pl.BlockSpec((1, D), lambda i, kk: (0, 0))],
        out_shape=[jax.ShapeDtypeStruct((m, D), F32), jax.ShapeDtypeStruct((1, D), F32)],
        scratch_shapes=[pltpu.VMEM((tm, D), F32)],
        compiler_params=_params("arbitrary", "arbitrary"), name=name)(dy, w, x, g, dres)


def _band(kvp, kvc):
    kk = jnp.concatenate([kvp[:, :128], kvc[:, :128]], axis=0)
    vv = jnp.concatenate([kvp[:, 128:], kvc[:, 128:]], axis=0)
    row = lax.broadcasted_iota(jnp.int32, kk.shape, 0)
    return jnp.where(row == 0, jnp.zeros_like(kk), kk), jnp.where(row == 0, jnp.zeros_like(vv), vv)


def _blockdiag(t, h):
    lane = lax.broadcasted_iota(jnp.int32, t.shape, 1)
    z = jnp.zeros_like(t)
    tr = pltpu.roll(t, 64, 1)
    if h == 0:
        top, bot = jnp.where(lane < 64, t, z), jnp.where(lane >= 64, tr, z)
    else:
        top, bot = jnp.where(lane < 64, tr, z), jnp.where(lane >= 64, t, z)
    return jnp.concatenate([top, bot], axis=0)


def _from_blockdiag(t, h):
    lane = lax.broadcasted_iota(jnp.int32, (256, 128), 1)
    top, bot = t[:256], t[256:]
    if h == 0:
        return jnp.where(lane < 64, top + pltpu.roll(bot, 64, 1), 0.0)
    return jnp.where(lane >= 64, pltpu.roll(top, 64, 1) + bot, 0.0)


def _stack_heads(ref, h):
    return jnp.concatenate([ref[:, h * 512 + p * 128: h * 512 + (p + 1) * 128] for p in range(4)], axis=0)


def _score_bias(sinks):
    row = jnp.arange(512)[:, None] & 127
    col = jnp.arange(256)[None, :]
    band = (col > row) & (col <= row + BLOCK)
    valid = jnp.stack([band & (col >= BLOCK), band])
    sink_rows = jnp.repeat(jnp.transpose(sinks.reshape(N_KV, 4, 2), (0, 2, 1)), BLOCK, axis=-1).reshape(4, 512, 1)
    bias = jnp.where(valid, 0.0, MASKED)[:, None] + jnp.zeros((1, 4, 1, 1), F32)
    return jnp.where(col == 0, sink_rows[None], bias).astype(F32)


def _half_selector(shape, split):
    row = lax.broadcasted_iota(jnp.int32, shape, 0)
    one, zero = jnp.ones(shape, BF16), jnp.zeros(shape, BF16)
    return jnp.where(row < split, one, zero), jnp.where(row >= split, one, zero)


def _attn_fwd(qkv, bias, ex=None, ex_args=()):
    s = qkv.shape[0]
    nb = s // BLOCK
    scale = 1.0 / math.sqrt(HEAD_DIM)

    def body(q_ref, kvc_ref, kvp_ref, sk_ref, o_ref, lse_ref):
        kk, vv = _band(kvp_ref[...], kvc_ref[...])
        sel0, sel1 = _half_selector((512, 128), 256)
        lane = lax.broadcasted_iota(jnp.int32, (512, 128), 1)
        for h in range(N_KV):
            kbd, vbd = _blockdiag(kk, h), _blockdiag(vv, h)
            sc = lax.dot_general(_stack_heads(q_ref, h), kbd, NT_DIMS, preferred_element_type=F32) * scale
            pes, mxs = [], []
            for half in range(2):
                sh = sc[:, half * 256:(half + 1) * 256] + sk_ref[h * 2 + half]
                mx = jnp.max(sh, axis=-1, keepdims=True)
                pes.append(jnp.exp(sh - mx).astype(BF16))
                mxs.append(mx)
            pb = jnp.concatenate(pes, axis=1)
            o_un = lax.dot_general(pb, vbd, NN_DIMS, preferred_element_type=F32)
            sums = [lax.dot_general(pb, sel, NN_DIMS, preferred_element_type=F32) for sel in (sel0, sel1)]
            o = o_un * jnp.where(lane < 64, 1.0 / sums[0], 1.0 / sums[1])
            for half in range(2):
                lse_ref[h * 2 + half] = mxs[half] + jnp.log(sums[half])
            for p in range(4):
                o_ref[:, h * 512 + p * 128: h * 512 + (p + 1) * 128] = o[p * 128:(p + 1) * 128].astype(BF16)

    return pl.pallas_call(
        _carried(body, 4, 2, ex, nb), grid=(nb,),
        in_specs=[pl.BlockSpec((BLOCK, NQ), lambda n: (n, 0)),
                  pl.BlockSpec((BLOCK, KVW), lambda n: (n, NQ // KVW)),
                  pl.BlockSpec((BLOCK, KVW), lambda n: (jnp.maximum(n - 1, 0), NQ // KVW)),
                  pl.BlockSpec((None, 4, 512, 256), lambda n: (jnp.minimum(n, 1), 0, 0, 0))] + _ex(ex, "in_specs"),
        out_specs=[pl.BlockSpec((BLOCK, NQ), lambda n: (n, 0)),
                   pl.BlockSpec((None, 4, 512, 128), lambda n: (n, 0, 0, 0))] + _ex(ex, "out_specs"),
        out_shape=[jax.ShapeDtypeStruct((s, NQ), BF16), jax.ShapeDtypeStruct((nb, 4, 512, 128), F32)] + _ex(ex, "out_shape"),
        scratch_shapes=_ex(ex, "scratch"),
        compiler_params=_params("arbitrary"), name="attn_fwd")(qkv, qkv, qkv, bias, *ex_args)


def _attn_bwd(qkv, do, o, lse, bias, ex=None, ex_args=()):
    s = qkv.shape[0]
    nb = s // BLOCK
    scale = 1.0 / math.sqrt(HEAD_DIM)

    def body(q_ref, kvc_ref, kvp_ref, do_ref, o_ref, lse_ref, sk_ref, dq_ref, dkv_ref, dsk_ref, ck_ref, cv_ref):
        n = pl.program_id(0)

        @pl.when(n == 0)
        def _():
            dsk_ref[...] = jnp.zeros_like(dsk_ref)
            ck_ref[...] = jnp.zeros_like(ck_ref)
            cv_ref[...] = jnp.zeros_like(cv_ref)

        @pl.when(n < nb)
        def _():
            kk, vv = _band(kvp_ref[...], kvc_ref[...])
            sel0, sel1 = _half_selector((128, 128), 64)
            dkk = jnp.zeros((256, 128), F32)
            dvv = jnp.zeros((256, 128), F32)
            for h in range(N_KV):
                kbd, vbd = _blockdiag(kk, h), _blockdiag(vv, h)
                qp = _stack_heads(q_ref, h)
                dop = _stack_heads(do_ref, h)
                sc = lax.dot_general(qp, kbd, NT_DIMS, preferred_element_type=F32) * scale
                dp = lax.dot_general(dop, vbd, NT_DIMS, preferred_element_type=F32)
                dd = (dop.astype(F32) * _stack_heads(o_ref, h).astype(F32)).astype(BF16)
                probs, dss = [], []
                for half, sel in enumerate((sel0, sel1)):
                    delta = lax.dot_general(dd, sel, NN_DIMS, preferred_element_type=F32)
                    lse_h = lse_ref[h * 2 + half]
                    sh = sc[:, half * 256:(half + 1) * 256] + sk_ref[h * 2 + half]
                    pr = jnp.exp(sh - jnp.concatenate([lse_h, lse_h], axis=1))
                    dsf = pr * (dp[:, half * 256:(half + 1) * 256] - jnp.concatenate([delta, delta], axis=1))
                    dsk_ref[h * 2 + half] += dsf[:, 0:1]
                    dss.append((dsf * scale).astype(BF16))
                    probs.append(pr.astype(BF16))
                ds = jnp.concatenate(dss, axis=1)
                pb = jnp.concatenate(probs, axis=1)
                dqp = lax.dot_general(ds, kbd, NN_DIMS, preferred_element_type=F32)
                for p in range(4):
                    dq_ref[:, h * 512 + p * 128: h * 512 + (p + 1) * 128] = dqp[p * 128:(p + 1) * 128].astype(BF16)
                dkk = dkk + _from_blockdiag(lax.dot_general(ds, qp, TN_DIMS, preferred_element_type=F32), h)
                dvv = dvv + _from_blockdiag(lax.dot_general(pb, dop, TN_DIMS, preferred_element_type=F32), h)
            row = lax.broadcasted_iota(jnp.int32, (256, 128), 0)
            dkk = jnp.where(row == 0, 0.0, dkk)
            dvv = jnp.where(row == 0, 0.0, dvv)

            @pl.when(n >= 1)
            def _():
                dkv_ref[:, :128] = (ck_ref[...] + dkk[:128]).astype(BF16)
                dkv_ref[:, 128:] = (cv_ref[...] + dvv[:128]).astype(BF16)

            ck_ref[...] = dkk[128:]
            cv_ref[...] = dvv[128:]

        @pl.when(n == nb)
        def _():
            dkv_ref[:, :128] = ck_ref[...].astype(BF16)
            dkv_ref[:, 128:] = cv_ref[...].astype(BF16)

    last = nb - 1
    blk = lambda n: (jnp.minimum(n, last), 0)
    return pl.pallas_call(
        _carried(body, 7, 3, ex, nb + 1), grid=(nb + 1,),
        in_specs=[pl.BlockSpec((BLOCK, NQ), blk),
                  pl.BlockSpec((BLOCK, KVW), lambda n: (jnp.minimum(n, last), NQ // KVW)),
                  pl.BlockSpec((BLOCK, KVW), lambda n: (jnp.clip(n - 1, 0, last), NQ // KVW)),
                  pl.BlockSpec((BLOCK, NQ), blk), pl.BlockSpec((BLOCK, NQ), blk),
                  pl.BlockSpec((None, 4, 512, 128), lambda n: (jnp.minimum(n, last), 0, 0, 0)),
                  pl.BlockSpec((None, 4, 512, 256), lambda n: (jnp.minimum(n, 1), 0, 0, 0))] + _ex(ex, "in_specs"),
        out_specs=[pl.BlockSpec((BLOCK, NQ), blk),
                   pl.BlockSpec((BLOCK, KVW), lambda n: (jnp.maximum(n - 1, 0), 0)),
                   pl.BlockSpec((4, 512, 1), lambda n: (0, 0, 0))] + _ex(ex, "out_specs"),
        out_shape=[jax.ShapeDtypeStruct((s, NQ), BF16), jax.ShapeDtypeStruct((s, KVW), BF16),
                   jax.ShapeDtypeStruct((4, 512, 1), F32)] + _ex(ex, "out_shape"),
        scratch_shapes=[pltpu.VMEM((BLOCK, 128), F32), pltpu.VMEM((BLOCK, 128), F32)] + _ex(ex, "scratch"),
        compiler_params=_params("arbitrary"), name="attn_bwd")(qkv, qkv, qkv, do, o, lse, bias, *ex_args)


LANE = 128
FFN_HALF = F // 2
FFN_PAIR = 2 * LANE
FFN_PAIRS = F // FFN_PAIR
FFN_GROUPS = ((0, 2), (2, 4), (4, 6), (6, 8), (8, 10), (10, 11))
FFN_HALO = 8


def _resident(shape):
    return pl.BlockSpec(shape, lambda i: (0,) * len(shape), pipeline_mode=pl.Buffered(1))


def _ffn_fwd(tag, x_in, g, w_up_t, w_dw, b_dw, w_down, *, tm, pre=None, head=None, ex=None, ex_args=()):
    s = x_in.shape[0]
    tm = min(tm, s)
    n_pre = 3 if pre is not None else 0
    n_head = 2 if head is not None else 0

    def body(*refs):
        x_ref, g_ref, wup_ref, wdw_ref, bdw_ref, wd_ref = refs[:6]
        pre_refs = refs[6:6 + n_pre]
        head_refs = refs[6 + n_pre:6 + n_pre + n_head]
        p = 6 + n_pre + n_head
        o_ref, h_ref, up_ref, act_ref, gate_ref = refs[p:p + 5]
        p += 5
        if pre is not None:
            xmid_ref, p = refs[p], p + 1
        if head is not None:
            loss_ref, dgf_ref, p = refs[p], refs[p + 1], p + 2
        carry_ref, ext_ref = refs[p], refs[p + 1]

        @pl.when(pl.program_id(0) == 0)
        def _():
            carry_ref[...] = jnp.zeros_like(carry_ref)
            if head is not None:
                loss_ref[...] = jnp.zeros_like(loss_ref)
                dgf_ref[...] = jnp.zeros_like(dgf_ref)

        xf = x_ref[...]
        if pre is not None:
            oin_ref, wo_ref, bo_ref = pre_refs
            xf = xf + lax.dot_general(oin_ref[...], wo_ref[...], NN_DIMS, preferred_element_type=F32) + bo_ref[...]
            xmid_ref[...] = xf
        r = lax.rsqrt(jnp.mean(xf * xf, axis=-1, keepdims=True) + RMS_EPS)
        h = ((xf * r) * g_ref[...]).astype(BF16)
        h_ref[...] = h

        def up_mm(p):
            for off in (p * FFN_PAIR, F + p * FFN_PAIR):
                rs = slice(off, off + FFN_PAIR)
                up_ref[:, rs] = lax.dot_general(h, wup_ref[rs, :], NT_DIMS, preferred_element_type=F32).astype(BF16)

        def elementwise(p):
            for c in (2 * p, 2 * p + 1):
                cs = slice(c * LANE, (c + 1) * LANE)
                vs = slice(F + c * LANE, F + (c + 1) * LANE)
                gcol = up_ref[:, cs].astype(F32)
                ext_ref[pl.ds(0, FFN_HALO), :] = carry_ref[:, cs]
                ext_ref[pl.ds(FFN_HALO, tm), :] = gcol
                gate_b = (wdw_ref[2:3, cs] * gcol + wdw_ref[1:2, cs] * ext_ref[pl.ds(FFN_HALO - 1, tm), :]
                          + wdw_ref[0:1, cs] * ext_ref[pl.ds(FFN_HALO - 2, tm), :] + bdw_ref[:, cs]).astype(BF16)
                gate_ref[:, cs] = gate_b
                gate = gate_b.astype(F32)
                act_ref[:, cs] = (gate * _sigmoid(gate) * up_ref[:, vs].astype(F32)).astype(BF16)
                carry_ref[:, cs] = gcol[tm - FFN_HALO:]

        def down_mm(lo, hi):
            ks = slice(lo * FFN_PAIR, hi * FFN_PAIR)
            return lax.dot_general(act_ref[:, ks], wd_ref[ks, :], NN_DIMS, preferred_element_type=F32)

        acc = xf
        up_mm(0)
        pending = None
        for lo, hi in FFN_GROUPS:
            for p in range(lo, hi):
                if p + 1 < FFN_PAIRS:
                    up_mm(p + 1)
                if pending is not None:
                    acc = acc + down_mm(*pending)
                    pending = None
                elementwise(p)
            pending = (lo, hi)
        x_out = acc + down_mm(*pending)
        if head is None:
            o_ref[...] = x_out
        else:
            gf_ref, tgt_ref = head_refs
            gf = gf_ref[...]
            rf = lax.rsqrt(jnp.mean(x_out * x_out, axis=-1, keepdims=True) + RMS_EPS)
            diff = (x_out * rf) * gf - tgt_ref[...]
            loss_ref[...] += 0.5 * jnp.sum(jnp.mean(diff * diff, axis=-1, keepdims=True))
            dx, dg_rows = _rms_bwd(diff * (1.0 / D), x_out, gf)
            o_ref[...] = dx
            dgf_ref[...] += jnp.sum(dg_rows, axis=0, keepdims=True)

    row = lambda i: (i, 0)
    const = lambda i: (0, 0)
    in_specs = [pl.BlockSpec((tm, D), row), _resident((1, D)), _resident((2 * F, D)), _resident((3, F)),
                _resident((1, F)), _resident((F, D))]
    out_specs = [pl.BlockSpec((tm, D), row), pl.BlockSpec((tm, D), row), pl.BlockSpec((tm, 2 * F), row),
                 pl.BlockSpec((tm, F), row), pl.BlockSpec((tm, F), row)]
    out_shape = [jax.ShapeDtypeStruct((s, D), F32), jax.ShapeDtypeStruct((s, D), BF16),
                 jax.ShapeDtypeStruct((s, 2 * F), BF16), jax.ShapeDtypeStruct((s, F), BF16),
                 jax.ShapeDtypeStruct((s, F), BF16)]
    args = [x_in, g, w_up_t, w_dw, b_dw, w_down]
    if pre is not None:
        in_specs += [pl.BlockSpec((tm, NQ), row), _resident((NQ, D)), _resident((1, D))]
        out_specs.append(pl.BlockSpec((tm, D), row))
        out_shape.append(jax.ShapeDtypeStruct((s, D), F32))
        args += list(pre)
    if head is not None:
        in_specs += [_resident((1, D)), pl.BlockSpec((tm, D), row)]
        out_specs += [pl.BlockSpec((1, 128), const), pl.BlockSpec((1, D), const)]
        out_shape += [jax.ShapeDtypeStruct((1, 128), F32), jax.ShapeDtypeStruct((1, D), F32)]
        args += list(head)
    return pl.pallas_call(
        _carried(body, len(args), len(out_shape), ex, s // tm), grid=(s // tm,),
        in_specs=in_specs + _ex(ex, "in_specs"), out_specs=out_specs + _ex(ex, "out_specs"),
        out_shape=out_shape + _ex(ex, "out_shape"),
        scratch_shapes=[pltpu.VMEM((FFN_HALO, F), F32), pltpu.VMEM((tm + FFN_HALO, LANE), F32)] + _ex(ex, "scratch"),
        compiler_params=_params("arbitrary"), name=f"ffn{tag}_fwd")(*args, *ex_args)


def _ffn_bwd(tag, dx_out, x_in, g, w_up_t, w_dw, w_down, up, gate, *, tm):
    s = x_in.shape[0]
    tm = min(tm, s)
    n_i = s // tm

    def body(dx_ref, x_ref, g_ref, up_ref, gate_ref, wup_ref, wdw_ref, wd_ref,
             dxin_ref, dup_ref, dg_ref, dwb_ref, carry_ref, da0_ref, da1_ref, extd_ref):
        i = pl.program_id(0)

        @pl.when(i == 0)
        def _():
            carry_ref[...] = jnp.zeros_like(carry_ref)
            dg_ref[...] = jnp.zeros_like(dg_ref)
            dwb_ref[...] = jnp.zeros_like(dwb_ref)

        dxf = dx_ref[...]
        dxb = dxf.astype(BF16)
        da_refs = (da0_ref, da1_ref)

        def da_mm(p):
            ks = slice(p * FFN_PAIR, (p + 1) * FFN_PAIR)
            da_refs[p % 2][...] = lax.dot_general(dxb, wd_ref[ks, :], NT_DIMS, preferred_element_type=F32)

        def elementwise(p):
            da_ref = da_refs[p % 2]
            for c in range(2):
                cc = 2 * p + c
                cs = slice(cc * LANE, (cc + 1) * LANE)
                vs = slice(F + cc * LANE, F + (cc + 1) * LANE)
                gate = gate_ref[:, cs].astype(F32)
                sg = _sigmoid(gate)
                silu = gate * sg
                dac = da_ref[:, c * LANE:(c + 1) * LANE]
                dup_ref[:, vs] = (dac * silu).astype(BF16)
                dgate = dac * up_ref[:, vs].astype(F32) * (sg + silu * (1.0 - sg))
                extd_ref[pl.ds(0, tm), :] = dgate
                extd_ref[pl.ds(tm, FFN_HALO), :] = carry_ref[:, cs]
                d1 = extd_ref[pl.ds(1, tm), :]
                d2 = extd_ref[pl.ds(2, tm), :]
                dup_ref[:, cs] = (wdw_ref[2:3, cs] * dgate + wdw_ref[1:2, cs] * d1 + wdw_ref[0:1, cs] * d2).astype(BF16)
                carry_ref[:, cs] = dgate[:FFN_HALO]
                gcol = up_ref[:, cs].astype(F32)
                dwb_ref[2, :, cs] += _sum8(dgate * gcol)
                dwb_ref[1, :, cs] += _sum8(d1 * gcol)
                dwb_ref[0, :, cs] += _sum8(d2 * gcol)
                dwb_ref[3, :, cs] += _sum8(dgate)

        def dh_mm(lo, hi):
            ks = slice(lo * FFN_PAIR, hi * FFN_PAIR)
            vks = slice(F + lo * FFN_PAIR, F + hi * FFN_PAIR)
            return (lax.dot_general(dup_ref[:, ks], wup_ref[ks, :], NN_DIMS, preferred_element_type=F32)
                    + lax.dot_general(dup_ref[:, vks], wup_ref[vks, :], NN_DIMS, preferred_element_type=F32))

        acc = None
        da_mm(0)
        pending = None
        for lo, hi in FFN_GROUPS:
            for p in range(lo, hi):
                if p + 1 < FFN_PAIRS:
                    da_mm(p + 1)
                if pending is not None:
                    part = dh_mm(*pending)
                    acc = part if acc is None else acc + part
                    pending = None
                elementwise(p)
            pending = (lo, hi)
        acc = acc + dh_mm(*pending)
        dx, dg_rows = _rms_bwd(acc, x_ref[...], g_ref[...])
        dxin_ref[...] = dxf + dx
        dg_ref[...] += jnp.sum(dg_rows, axis=0, keepdims=True)

    rev = lambda i: (n_i - 1 - i, 0)
    return pl.pallas_call(
        body, grid=(n_i,),
        in_specs=[pl.BlockSpec((tm, D), rev), pl.BlockSpec((tm, D), rev), _resident((1, D)),
                  pl.BlockSpec((tm, 2 * F), rev), pl.BlockSpec((tm, F), rev),
                  _resident((2 * F, D)), _resident((3, F)), _resident((F, D))],
        out_specs=[pl.BlockSpec((tm, D), rev), pl.BlockSpec((tm, 2 * F), rev),
                   pl.BlockSpec((1, D), lambda i: (0, 0)), pl.BlockSpec((4, 8, F), lambda i: (0, 0, 0))],
        out_shape=[jax.ShapeDtypeStruct((s, D), F32), jax.ShapeDtypeStruct((s, 2 * F), BF16),
                   jax.ShapeDtypeStruct((1, D), F32), jax.ShapeDtypeStruct((4, 8, F), F32)],
        scratch_shapes=[pltpu.VMEM((FFN_HALO, F), F32), pltpu.VMEM((tm, FFN_PAIR), F32), pltpu.VMEM((tm, FFN_PAIR), F32),
                        pltpu.VMEM((tm + FFN_HALO, LANE), F32)],
        compiler_params=_params("arbitrary"), name=f"ffn{tag}_bwd")(dx_out, x_in, g, up, gate, w_up_t, w_dw, w_down)


CONV_HALO = 32
CONV_CHUNK = 256
CONV_ROWS = 64


def _ln(u2, g, b):
    mu = jnp.mean(u2, axis=-1, keepdims=True)
    xc = u2 - mu
    rstd = lax.rsqrt(jnp.mean(xc * xc, axis=-1, keepdims=True) + LN_EPS)
    xhat = xc * rstd
    return xhat, rstd, xhat * g + b


def _phase_taps(ph):
    first = CONV_HALO - (CONV_K - 1)
    return [(k, first + k - ph) for k in range(CONV_K) if (first + k) % 8 == ph]


def _sum8(v):
    out = v[0:8]
    for j in range(8, v.shape[0], 8):
        out = out + v[j:j + 8]
    return out


def _store_phases(src_ref, sh_ref, tt, cols=slice(None)):
    sh_ref[0] = src_ref[:, cols]
    for ph in range(1, 8):
        sh_ref[ph, pl.ds(0, tt + CONV_HALO - 8), :] = src_ref[pl.ds(ph, tt + CONV_HALO - 8), cols]


def _conv_taps_fwd(w_ref, sh_ref, u2_ref, cs, tt):
    for base in range(0, tt, CONV_ROWS):
        acc = jnp.zeros((CONV_ROWS, CONV_CHUNK), F32) + w_ref[CONV_K:CONV_K + 1, cs]
        for ph in range(8):
            for k, off in _phase_taps(ph):
                acc = acc + w_ref[k:k + 1, cs] * sh_ref[ph, pl.ds(base + off, CONV_ROWS), :]
        u2_ref[pl.ds(base, CONV_ROWS), cs] = acc.astype(BF16)


def _conv_fwd(x_in, g, w1t, b1, wdw, ln_g, ln_b, w2, b2, *, tt, ex=None, ex_args=()):
    s = x_in.shape[0]
    tt = min(tt, s)
    n_c = D // CONV_CHUNK

    def body(x_ref, g_ref, w1_ref, b1_ref, w_ref, lg_ref, lb_ref, w2_ref, b2_ref,
             o_ref, h_ref, a_ref, u2_ref, s_ref, carry_ref, ext_ref, sh_ref):
        @pl.when(pl.program_id(0) == 0)
        def _():
            carry_ref[...] = jnp.zeros_like(carry_ref)

        xf = x_ref[...]
        r = lax.rsqrt(jnp.mean(xf * xf, axis=-1, keepdims=True) + RMS_EPS)
        h = ((xf * r) * g_ref[...]).astype(BF16)
        h_ref[...] = h

        def pw1_mm(c):
            for off in (c * CONV_CHUNK, D + c * CONV_CHUNK):
                rs = slice(off, off + CONV_CHUNK)
                a_ref[:, rs] = (lax.dot_general(h, w1_ref[rs, :], NT_DIMS, preferred_element_type=F32)
                                + b1_ref[:, rs]).astype(BF16)

        pw1_mm(0)
        for c in range(n_c):
            if c + 1 < n_c:
                pw1_mm(c + 1)
            cs = slice(c * CONV_CHUNK, (c + 1) * CONV_CHUNK)
            gs = slice(D + c * CONV_CHUNK, D + (c + 1) * CONV_CHUNK)
            u = a_ref[:, cs].astype(F32) * _sigmoid(a_ref[:, gs].astype(F32))
            ext_ref[pl.ds(0, CONV_HALO), :] = carry_ref[:, cs]
            ext_ref[pl.ds(CONV_HALO, tt), :] = u
            carry_ref[:, cs] = u[tt - CONV_HALO:]
            _store_phases(ext_ref, sh_ref, tt)
            _conv_taps_fwd(w_ref, sh_ref, u2_ref, cs, tt)
        _, _, ln = _ln(u2_ref[...].astype(F32), lg_ref[...], lb_ref[...])
        sv = (ln * _sigmoid(ln)).astype(BF16)
        s_ref[...] = sv
        o_ref[...] = xf + lax.dot_general(sv, w2_ref[...], NN_DIMS, preferred_element_type=F32) + b2_ref[...]

    row = lambda i: (i, 0)
    return pl.pallas_call(
        _carried(body, 9, 5, ex, s // tt), grid=(s // tt,),
        in_specs=[pl.BlockSpec((tt, D), row), _resident((1, D)), _resident((2 * D, D)), _resident((1, 2 * D)),
                  _resident((32, D)), _resident((1, D)), _resident((1, D)), _resident((D, D)),
                  _resident((1, D))] + _ex(ex, "in_specs"),
        out_specs=[pl.BlockSpec((tt, D), row), pl.BlockSpec((tt, D), row), pl.BlockSpec((tt, 2 * D), row),
                   pl.BlockSpec((tt, D), row), pl.BlockSpec((tt, D), row)] + _ex(ex, "out_specs"),
        out_shape=[jax.ShapeDtypeStruct((s, D), F32), jax.ShapeDtypeStruct((s, D), BF16),
                   jax.ShapeDtypeStruct((s, 2 * D), BF16), jax.ShapeDtypeStruct((s, D), BF16),
                   jax.ShapeDtypeStruct((s, D), BF16)] + _ex(ex, "out_shape"),
        scratch_shapes=[pltpu.VMEM((CONV_HALO, D), F32), pltpu.VMEM((tt + CONV_HALO, CONV_CHUNK), F32),
                        pltpu.VMEM((8, tt + CONV_HALO, CONV_CHUNK), F32)] + _ex(ex, "scratch"),
        compiler_params=_params("arbitrary"), name="conv_fwd")(x_in, g, w1t, b1, wdw, ln_g, ln_b, w2, b2, *ex_args)


def _conv_bwd(dx_out, x_in, g, a, u2, w1t, wdw, ln_g, ln_b, w2, *, tt, ex=None, ex_args=()):
    s = x_in.shape[0]
    tt = min(tt, s)
    hb = tt // CONV_HALO
    n_i = s // tt
    te = tt + CONV_HALO
    n_c = D // CONV_CHUNK

    def body(dx_ref, x_ref, g_ref, a_ref, prev_ref, u2_ref, w1_ref, w_ref, lg_ref, lb_ref, w2_ref,
             dxin_ref, da_ref, dg_ref, acc_ref, db1_ref, db2_ref,
             carry_ref, e2_ref, ext_ref, ush_ref, dsh_ref, du_ref):
        i = pl.program_id(0)

        @pl.when(i == 0)
        def _():
            for ref in (dg_ref, acc_ref, db1_ref, db2_ref, carry_ref):
                ref[...] = jnp.zeros_like(ref)

        dxf = dx_ref[...]
        db2_ref[...] += _sum8(dxf)
        dse = lax.dot_general(dxf.astype(BF16), w2_ref[...], NT_DIMS, preferred_element_type=F32)
        gv = lg_ref[...]
        xhat, rstd, ln = _ln(u2_ref[...].astype(F32), gv, lb_ref[...])
        sg = _sigmoid(ln)
        dln = dse * (sg * (1.0 + ln * (1.0 - sg)))
        acc_ref[32] += _sum8(dln * xhat)
        acc_ref[33] += _sum8(dln)
        dxh = dln * gv
        du2 = rstd * (dxh - jnp.mean(dxh, axis=-1, keepdims=True) - xhat * jnp.mean(dxh * xhat, axis=-1, keepdims=True))
        e2_ref[pl.ds(0, tt), :] = du2
        e2_ref[pl.ds(tt, CONV_HALO), :] = carry_ref[...]
        carry_ref[...] = du2[:CONV_HALO]
        first_tile = i == n_i - 1

        def dh_mm(c):
            cs = slice(c * CONV_CHUNK, (c + 1) * CONV_CHUNK)
            gs = slice(D + c * CONV_CHUNK, D + (c + 1) * CONV_CHUNK)
            return (lax.dot_general(da_ref[:, cs], w1_ref[cs, :], NN_DIMS, preferred_element_type=F32)
                    + lax.dot_general(da_ref[:, gs], w1_ref[gs, :], NN_DIMS, preferred_element_type=F32))

        dh = None
        for c in range(n_c):
            if c >= 1:
                part = dh_mm(c - 1)
                dh = part if dh is None else dh + part
            cs = slice(c * CONV_CHUNK, (c + 1) * CONV_CHUNK)
            gs = slice(D + c * CONV_CHUNK, D + (c + 1) * CONV_CHUNK)
            uh = prev_ref[:, cs].astype(F32) * _sigmoid(prev_ref[:, gs].astype(F32))
            ext_ref[pl.ds(0, CONV_HALO), :] = jnp.where(first_tile, 0.0, uh)
            a1 = a_ref[:, cs].astype(F32)
            sg2 = _sigmoid(a_ref[:, gs].astype(F32))
            ext_ref[pl.ds(CONV_HALO, tt), :] = a1 * sg2
            _store_phases(ext_ref, ush_ref, tt)
            _store_phases(e2_ref, dsh_ref, tt, cs)
            for base in range(0, tt, CONV_ROWS):
                d0 = e2_ref[pl.ds(base, CONV_ROWS), cs]
                du = jnp.zeros((CONV_ROWS, CONV_CHUNK), F32)
                for ph in range(8):
                    for k, off in _phase_taps(ph):
                        acc_ref[k, :, cs] += _sum8(d0 * ush_ref[ph, pl.ds(base + off, CONV_ROWS), :])
                    for k in range(CONV_K):
                        if (CONV_K - 1 - k) % 8 == ph:
                            du = du + w_ref[k:k + 1, cs] * dsh_ref[ph, pl.ds(base + CONV_K - 1 - k - ph, CONV_ROWS), :]
                acc_ref[CONV_K, :, cs] += _sum8(d0)
                du_ref[pl.ds(base, CONV_ROWS), :] = du
            du = du_ref[...]
            da1 = du * sg2
            da2 = du * a1 * (sg2 * (1.0 - sg2))
            da_ref[:, cs] = da1.astype(BF16)
            da_ref[:, gs] = da2.astype(BF16)
            db1_ref[:, cs] += _sum8(da1)
            db1_ref[:, gs] += _sum8(da2)
        dh = dh + dh_mm(n_c - 1)
        dx, dg_rows = _rms_bwd(dh, x_ref[...], g_ref[...])
        dxin_ref[...] = dxf + dx
        dg_ref[...] += jnp.sum(dg_rows, axis=0, keepdims=True)

    rev = lambda i: (n_i - 1 - i, 0)
    return pl.pallas_call(
        _carried(body, 11, 6, ex, n_i), grid=(n_i,),
        in_specs=[pl.BlockSpec((tt, D), rev), pl.BlockSpec((tt, D), rev), _resident((1, D)),
                  pl.BlockSpec((tt, 2 * D), rev),
                  pl.BlockSpec((CONV_HALO, 2 * D), lambda i: (jnp.maximum((n_i - 1 - i) * hb - 1, 0), 0)),
                  pl.BlockSpec((tt, D), rev), _resident((2 * D, D)), _resident((32, D)), _resident((1, D)),
                  _resident((1, D)), _resident((D, D))] + _ex(ex, "in_specs"),
        out_specs=[pl.BlockSpec((tt, D), rev), pl.BlockSpec((tt, 2 * D), rev), pl.BlockSpec((1, D), lambda i: (0, 0)),
                   pl.BlockSpec((40, 8, D), lambda i: (0, 0, 0)), pl.BlockSpec((8, 2 * D), lambda i: (0, 0)),
                   pl.BlockSpec((8, D), lambda i: (0, 0))] + _ex(ex, "out_specs"),
        out_shape=[jax.ShapeDtypeStruct((s, D), F32), jax.ShapeDtypeStruct((s, 2 * D), BF16),
                   jax.ShapeDtypeStruct((1, D), F32), jax.ShapeDtypeStruct((40, 8, D), F32),
                   jax.ShapeDtypeStruct((8, 2 * D), F32), jax.ShapeDtypeStruct((8, D), F32)] + _ex(ex, "out_shape"),
        scratch_shapes=[pltpu.VMEM((CONV_HALO, D), F32), pltpu.VMEM((te, D), F32), pltpu.VMEM((te, CONV_CHUNK), F32),
                        pltpu.VMEM((8, te, CONV_CHUNK), F32), pltpu.VMEM((8, te, CONV_CHUNK), F32),
                        pltpu.VMEM((tt, CONV_CHUNK), F32)] + _ex(ex, "scratch"),
        compiler_params=_params("arbitrary"), name="conv_bwd")(dx_out, x_in, g, a, a, u2, w1t, wdw, ln_g, ln_b, w2, *ex_args)


def _adamw(name, w, g, m, v):
    rows, cols = w.shape
    tr = _row_tile(rows, 256)

    def body(w_ref, g_ref, m_ref, v_ref, d_ref, nm_ref, nv_ref):
        gv = g_ref[...]
        m2 = ADAM_B1 * m_ref[...] + (1.0 - ADAM_B1) * gv
        v2 = ADAM_B2 * v_ref[...] + (1.0 - ADAM_B2) * (gv * gv)
        m_hat = m2 / (1.0 - ADAM_B1 ** ADAM_STEP)
        v_hat = v2 / (1.0 - ADAM_B2 ** ADAM_STEP)
        d_ref[...] = -ADAM_LR * (m_hat / (jnp.sqrt(v_hat) + ADAM_EPS) + ADAM_WD * w_ref[...])
        nm_ref[...] = m2
        nv_ref[...] = v2

    spec = pl.BlockSpec((tr, cols), lambda i: (i, 0))
    return pl.pallas_call(
        body, grid=(rows // tr,), in_specs=[spec] * 4, out_specs=[spec] * 3,
        out_shape=[jax.ShapeDtypeStruct((rows, cols), F32)] * 3,
        compiler_params=_params("parallel"), name=name)(w, g, m, v)


def _sum_slabs(name, buf):
    rows = buf.shape[1]
    tr = _row_tile(rows, 512, 16)

    def body(b_ref, o_ref):
        acc = b_ref[0].astype(F32)
        for k in range(1, N_DEV):
            acc = acc + b_ref[k].astype(F32)
        o_ref[...] = acc

    return pl.pallas_call(
        body, grid=(rows // tr,), in_specs=[pl.BlockSpec((N_DEV, tr, D), lambda i: (0, i, 0))],
        out_specs=pl.BlockSpec((tr, D), lambda i: (i, 0)), out_shape=jax.ShapeDtypeStruct((rows, D), F32),
        compiler_params=_params("parallel"), name=name)(buf)


def _ex(ex, field):
    return list(getattr(ex, field)) if ex is not None else []


def _me():
    x, y, c = lax.axis_index("x"), lax.axis_index("y"), lax.axis_index("c")
    return x, y, c, 4 * x + 2 * y + c


def _peer(x, y, c, k):
    return (x ^ (k >> 2), y ^ ((k >> 1) & 1), c ^ (k & 1))


class _Gather:
    def __init__(self, mats, slab_base=0):
        self.mats = tuple(mats)
        self.slab_base = slab_base
        self.rows = sum(MAT_ROWS[i] for i in self.mats)
        any_spec = pl.BlockSpec(memory_space=pl.ANY)
        self.n_in, self.n_out = 1, len(self.mats)
        self.in_specs = [any_spec]
        self.out_specs = [any_spec] * self.n_out
        self.out_shape = [jax.ShapeDtypeStruct((N_DEV * MAT_ROWS[i], D), BF16) for i in self.mats]
        self.scratch = [pltpu.SemaphoreType.DMA((N_DEV - 1,)), pltpu.SemaphoreType.DMA((N_DEV - 1,)),
                        pltpu.SemaphoreType.DMA]
        assert max(s.shape[0] for s in self.out_shape) >= self.rows

    def start(self, ins, outs, scr):
        (slab_ref,), (send_sems, recv_sems, local_sem) = ins, scr
        x, y, c, me = _me()
        for n, i in enumerate(self.mats):
            src = slab_ref.at[pl.ds(MAT_OFF[i] - self.slab_base, MAT_ROWS[i])]
            dst = outs[n].at[pl.ds(me * MAT_ROWS[i], MAT_ROWS[i])]
            pltpu.make_async_copy(src, dst, local_sem).start()
            for k in range(1, N_DEV):
                pltpu.make_async_remote_copy(src_ref=src, dst_ref=dst, send_sem=send_sems.at[k - 1],
                                             recv_sem=recv_sems.at[k - 1], device_id=_peer(x, y, c, k),
                                             device_id_type=MESH).start()

    def wait(self, ins, outs, scr):
        send_sems, recv_sems, local_sem = scr
        x, y, c, _ = _me()
        big = max(range(self.n_out), key=lambda n: self.out_shape[n].shape[0])
        whole = outs[big].at[pl.ds(0, self.rows)]
        for k in range(1, N_DEV):
            pltpu.make_async_remote_copy(src_ref=whole, dst_ref=whole, send_sem=send_sems.at[k - 1],
                                         recv_sem=recv_sems.at[k - 1], device_id=_peer(x, y, c, k),
                                         device_id_type=MESH).wait()
        pltpu.make_async_copy(whole, whole, local_sem).wait()


class _Scatter:
    def __init__(self, mats):
        self.mats = tuple(mats)
        self.rows = sum(MAT_ROWS[i] for i in self.mats)
        any_spec = pl.BlockSpec(memory_space=pl.ANY)
        self.n_in, self.n_out = len(self.mats), 1
        self.in_specs = [any_spec] * self.n_in
        self.out_specs = [any_spec]
        self.out_shape = [jax.ShapeDtypeStruct((N_DEV, self.rows, D), BF16)]
        self.scratch = [pltpu.SemaphoreType.DMA((N_DEV - 1,)), pltpu.SemaphoreType.DMA((N_DEV - 1,)),
                        pltpu.SemaphoreType.DMA]

    def offsets(self):
        return [sum(MAT_ROWS[i] for i in self.mats[:n]) for n in range(len(self.mats))]

    def start(self, ins, outs, scr):
        (buf_ref,), (send_sems, recv_sems, local_sem) = outs, scr
        x, y, c, me = _me()
        for n, (i, off) in enumerate(zip(self.mats, self.offsets())):
            r = MAT_ROWS[i]
            pltpu.make_async_copy(ins[n].at[pl.ds(me * r, r)], buf_ref.at[0, pl.ds(off, r)], local_sem).start()
            for k in range(1, N_DEV):
                pltpu.make_async_remote_copy(src_ref=ins[n].at[pl.ds((me ^ k) * r, r)], dst_ref=buf_ref.at[k, pl.ds(off, r)],
                                             send_sem=send_sems.at[k - 1], recv_sem=recv_sems.at[k - 1],
                                             device_id=_peer(x, y, c, k), device_id_type=MESH).start()

    def wait(self, ins, outs, scr):
        (buf_ref,), (send_sems, recv_sems, local_sem) = outs, scr
        x, y, c, _ = _me()
        whole = buf_ref.at[0]
        for k in range(1, N_DEV):
            pltpu.make_async_remote_copy(src_ref=whole, dst_ref=whole, send_sem=send_sems.at[k - 1],
                                         recv_sem=recv_sems.at[k - 1], device_id=_peer(x, y, c, k),
                                         device_id_type=MESH).wait()
        pltpu.make_async_copy(whole, whole, local_sem).wait()


def _carried(body, n_in, n_out, ex, n_steps):
    if ex is None:
        return body

    def wrapped(*refs):
        ins, ex_ins = refs[:n_in], refs[n_in:n_in + ex.n_in]
        p = n_in + ex.n_in
        outs, ex_outs = refs[p:p + n_out], refs[p + n_out:p + n_out + ex.n_out]
        p += n_out + ex.n_out
        n_scr = len(refs) - p - len(ex.scratch)
        scr, ex_scr = refs[p:p + n_scr], refs[p + n_scr:]

        @pl.when(pl.program_id(0) == 0)
        def _():
            ex.start(ex_ins, ex_outs, ex_scr)

        body(*ins, *outs, *scr)

        @pl.when(pl.program_id(0) == n_steps - 1)
        def _():
            ex.wait(ex_ins, ex_outs, ex_scr)

    return wrapped


def _exchange_small(name, ex, ex_args, small, reduce):
    vmem_spec = pl.BlockSpec(memory_space=pltpu.VMEM)
    n_small = small.shape[0]

    def body(*refs):
        ex_ins, small_ref = refs[:ex.n_in], refs[ex.n_in]
        p = ex.n_in + 1
        ex_outs, res_ref = refs[p:p + ex.n_out], refs[p + ex.n_out]
        p += ex.n_out + 1
        if reduce:
            all_ref, p = refs[p], p + 1
        else:
            all_ref = res_ref
        sm_send, sm_recv = refs[p], refs[p + 1]
        ex_scr = refs[p + 2:]
        x, y, c, me = _me()
        ex.start(ex_ins, ex_outs, ex_scr)
        copies = [pltpu.make_async_remote_copy(
            src_ref=small_ref, dst_ref=all_ref.at[me], send_sem=sm_send.at[k - 1], recv_sem=sm_recv.at[k - 1],
            device_id=_peer(x, y, c, k), device_id_type=MESH) for k in range(1, N_DEV)]
        for cp in copies:
            cp.start()
        all_ref[me] = small_ref[...]
        for cp in copies:
            cp.wait()
        if reduce:
            acc = all_ref[0]
            for d in range(1, N_DEV):
                acc = acc + all_ref[d]
            res_ref[...] = acc
        ex.wait(ex_ins, ex_outs, ex_scr)

    res_shape = (n_small, D) if reduce else (N_DEV, n_small, D)
    scratch = ([pltpu.VMEM((N_DEV, n_small, D), F32)] if reduce else []) + [
        pltpu.SemaphoreType.DMA((N_DEV - 1,)), pltpu.SemaphoreType.DMA((N_DEV - 1,))] + ex.scratch
    outs = pl.pallas_call(
        body, in_specs=ex.in_specs + [vmem_spec], out_specs=ex.out_specs + [vmem_spec],
        out_shape=ex.out_shape + [jax.ShapeDtypeStruct(res_shape, F32)], scratch_shapes=scratch,
        compiler_params=pltpu.CompilerParams(vmem_limit_bytes=VMEM_LIMIT_BYTES), name=name)(*ex_args, small)
    return outs


def _pack_rows(arrs):
    rows = []
    for a in arrs:
        flat = a.reshape(-1).astype(F32)
        pad = (-flat.shape[0]) % D
        rows.append(jnp.pad(flat, (0, pad)).reshape(-1, D))
    slab = jnp.concatenate(rows, axis=0)
    return jnp.pad(slab, ((0, (-slab.shape[0]) % 8), (0, 0)))


def _unpack_rows(slab, shapes):
    out, r = [], 0
    for shp in shapes:
        size = math.prod(shp)
        nrows = -(-size // D)
        out.append(slab[r:r + nrows].reshape(-1)[:size].reshape(shp))
        r += nrows
    return out


def _unpack_gathered(slabs, shapes):
    out, r = [], 0
    for shp in shapes:
        size = math.prod(shp)
        nrows = -(-size // D)
        blocks = slabs[:, r:r + nrows].reshape(N_DEV, -1)[:, :size].reshape((N_DEV,) + tuple(shp))
        out.append(jnp.moveaxis(blocks, 0, -2).reshape(tuple(shp[:-1]) + (N_DEV * shp[-1],)))
        r += nrows
    return out


def kernel(x, norm_mix, attn_w_qkv, attn_b_qkv, attn_sinks, attn_w_o, attn_b_o, conv_w_pw1, conv_b_pw1, conv_w_dw, conv_b_dw, conv_ln_g, conv_ln_b, conv_w_pw2, conv_b_pw2, norm_ffn, ffn_w_up, ffn_w_dw, ffn_b_dw, ffn_w_down, final_norm, loss_target, m_norm_mix, m_attn_w_qkv, m_attn_b_qkv, m_attn_sinks, m_attn_w_o, m_attn_b_o, m_conv_w_pw1, m_conv_b_pw1, m_conv_w_dw, m_conv_b_dw, m_conv_ln_g, m_conv_ln_b, m_conv_w_pw2, m_conv_b_pw2, m_norm_ffn, m_ffn_w_up, m_ffn_w_dw, m_ffn_b_dw, m_ffn_w_down, m_final_norm, v_norm_mix, v_attn_w_qkv, v_attn_b_qkv, v_attn_sinks, v_attn_w_o, v_attn_b_o, v_conv_w_pw1, v_conv_b_pw1, v_conv_w_dw, v_conv_b_dw, v_conv_ln_g, v_conv_ln_b, v_conv_w_pw2, v_conv_b_pw2, v_norm_ffn, v_ffn_w_up, v_ffn_w_dw, v_ffn_b_dw, v_ffn_w_down, v_final_norm):
    s = x.shape[1]
    me = 4 * lax.axis_index("x") + 2 * lax.axis_index("y") + lax.axis_index("c")
    x0 = x.reshape(s, D)
    tgt = loss_target.reshape(s, D)

    slab_qkv = attn_w_qkv[0].T.astype(BF16)
    slab = jnp.concatenate([attn_w_o[0], conv_w_pw1[0].T, conv_w_pw2[0],
                            ffn_w_up[0].T, ffn_w_down[0], ffn_w_up[1].T, ffn_w_down[1]], axis=0).astype(BF16)
    sharded_small = [conv_b_pw1, conv_w_dw, conv_b_dw, conv_ln_g, conv_ln_b, conv_b_pw2, ffn_w_dw]
    small_local = _pack_rows(sharded_small)
    w_qkv_t, small_all = _exchange_small("gather_first", _Gather((0,)), (slab_qkv,), small_local, reduce=False)
    b_pw1, w_cdw, b_cdw, ln_g, ln_b, b_pw2, w_fdw = _unpack_gathered(small_all, [a.shape for a in sharded_small])
    conv_w32 = jnp.concatenate([w_cdw[0], b_cdw], axis=0)
    score_bias = _score_bias(attn_sinks)

    qkv, h0 = _mm("qkv", x0, w_qkv_t, nt=True, tm=1024, tn=NQ + KVW, out_dtype=BF16, rms_g=norm_mix[0:1], bias=attn_b_qkv)
    o, lse, w_o, w_up0_t, w_dn0 = _attn_fwd(qkv, score_bias, _Gather((1, 4, 5), MAT_OFF[1]), (slab,))
    x2, h1, up0, act0, gate0, x1, w_pw1_t, w_pw2 = _ffn_fwd(
        0, x0, norm_ffn[0:1], w_up0_t, w_fdw[0], ffn_b_dw[0:1], w_dn0, tm=256, pre=(o, w_o, attn_b_o),
        ex=_Gather((2, 3), MAT_OFF[1]), ex_args=(slab,))
    x3, h2, a, u2, s_act, w_up1_t, w_dn1 = _conv_fwd(x2, norm_mix[1:2], w_pw1_t, b_pw1, conv_w32, ln_g, ln_b, w_pw2, b_pw2,
                                                     tt=256, ex=_Gather((6, 7), MAT_OFF[1]), ex_args=(slab,))
    dx4, h3, up1, act1, gate1, loss_acc, dg_final = _ffn_fwd(1, x3, norm_ffn[1:2], w_up1_t, w_fdw[1], ffn_b_dw[1:2], w_dn1,
                                                             tm=256, head=(final_norm.reshape(1, D), tgt))
    loss = lax.psum(loss_acc[0, 0], ("x", "y", "c"))

    dx3, d_up1, dg_f1, dwb1 = _ffn_bwd(1, dx4, x3, norm_ffn[1:2], w_up1_t, w_fdw[1], w_dn1, up1, gate1, tm=256)
    dw_dn1 = _mm_tn("ffn1_dwdown", act1, dx4, tn=FFN_HALF, tt=2048)
    dw_up1_t = _mm_tn("ffn1_dwup", d_up1, h3, tn=FFN_HALF, tt=2048)
    dw_pw2 = _mm_tn("dw_pw2", s_act, dx3, tn=D, tt=2048)
    scatter_a = _Scatter((6, 7))
    dx2, da, dg_m1, conv_acc8, db_pw1_8, db_pw2_8, buf_a = _conv_bwd(
        dx3, x2, norm_mix[1:2], a, u2, w_pw1_t, conv_w32, ln_g, ln_b, w_pw2, tt=256, ex=scatter_a, ex_args=(dw_up1_t, dw_dn1))
    conv_acc = jnp.sum(conv_acc8, axis=1)
    db_pw1 = jnp.sum(db_pw1_8, axis=0, keepdims=True)
    db_pw2 = jnp.sum(db_pw2_8, axis=0, keepdims=True)
    dw_pw1_t = _mm_tn("dw_pw1", da, h2, tn=D, tt=2048)
    dx1, d_up0, dg_f0, dwb0 = _ffn_bwd(0, dx2, x1, norm_ffn[0:1], w_up0_t, w_fdw[0], w_dn0, up0, gate0, tm=256)
    dw_dn0 = _mm_tn("ffn0_dwdown", act0, dx2, tn=FFN_HALF, tt=2048)
    dw_up0_t = _mm_tn("ffn0_dwup", d_up0, h1, tn=FFN_HALF, tt=2048)
    do = _mm("d_attn_out", dx1, w_o, nt=True, tm=1024, tn=D, out_dtype=BF16)
    dw_o, db_o = _mm_tn("dw_o", o, dx1, tn=D, tt=2048, colsum_r=True)
    scatter_b = _Scatter((1, 2, 3, 4, 5))
    dq, dkv, dsk, buf_b = _attn_bwd(qkv, do, o, lse, score_bias, scatter_b, (dw_o, dw_pw1_t, dw_pw2, dw_up0_t, dw_dn0))
    dqkv = jnp.concatenate([dq, dkv], axis=1)
    grad_x, dg_m0 = _dgrad_rms("d_qkv", dqkv, w_qkv_t, x0, norm_mix[0:1], dx1, tm=1024, tk=NQ + KVW)
    dw_qkv_t, db_qkv = _mm_tn("dw_qkv", dqkv, h0, tn=NQ + KVW, tt=2048, colsum_l=True)

    dwb0, dwb1 = jnp.sum(dwb0, axis=1), jnp.sum(dwb1, axis=1)
    d_sinks = jnp.transpose(jnp.sum(dsk.reshape(N_KV, 2, 4, BLOCK), axis=-1), (0, 2, 1)).reshape(1, 16)
    small_grads = [jnp.concatenate([dg_m0, dg_m1], axis=0), db_qkv, d_sinks, db_o, jnp.concatenate([dg_f0, dg_f1], axis=0),
                   jnp.stack([dwb0[3], dwb1[3]]), dg_final, db_pw1, conv_acc[0:CONV_K], conv_acc[31:32], conv_acc[32:33],
                   conv_acc[33:34], db_pw2, jnp.stack([dwb0[0:3], dwb1[0:3]])]
    small_shapes = [(2, D), (1, NQ + KVW), (1, 16), (1, D), (2, D), (2, F), (D,), (1, 2 * D), (1, CONV_K, D), (1, D),
                    (1, D), (1, D), (1, D), (2, 3, F)]
    scatter_c = _Scatter((0,))
    buf_c, small_sum = _exchange_small("scatter_last", scatter_c, (dw_qkv_t,), _pack_rows(small_grads), reduce=True)
    gm = [None] * N_MAT
    for tag, sc, buf in (("a", scatter_a, buf_a), ("b", scatter_b, buf_b), ("c", scatter_c, buf_c)):
        gslab = _sum_slabs("sum_slabs_" + tag, buf)
        for i, off in zip(sc.mats, sc.offsets()):
            gm[i] = gslab[off:off + MAT_ROWS[i]]
    (g_norm_mix, g_b_qkv, g_sinks, g_b_o, g_norm_ffn, g_ffn_b_dw, g_final, g_b_pw1_full, g_w_cdw_full, g_b_cdw_full,
     g_ln_g_full, g_ln_b_full, g_b_pw2_full, g_w_fdw_full) = _unpack_rows(small_sum, small_shapes)

    def shard(a, width):
        return lax.dynamic_slice_in_dim(a, me * width, width, axis=a.ndim - 1)

    grads = {
        "norm_mix": g_norm_mix, "attn_w_qkv": gm[0].T[None], "attn_b_qkv": g_b_qkv, "attn_sinks": g_sinks,
        "attn_w_o": gm[1][None], "attn_b_o": g_b_o, "conv_w_pw1": gm[2].T[None], "conv_b_pw1": shard(g_b_pw1_full, 256),
        "conv_w_dw": shard(g_w_cdw_full, 128), "conv_b_dw": shard(g_b_cdw_full, 128), "conv_ln_g": shard(g_ln_g_full, 128),
        "conv_ln_b": shard(g_ln_b_full, 128), "conv_w_pw2": gm[3][None], "conv_b_pw2": shard(g_b_pw2_full, 128),
        "norm_ffn": g_norm_ffn, "ffn_w_up": jnp.stack([gm[4].T, gm[6].T]), "ffn_w_dw": shard(g_w_fdw_full, 352),
        "ffn_b_dw": g_ffn_b_dw, "ffn_w_down": jnp.stack([gm[5], gm[7]]), "final_norm": g_final,
    }
    weights = dict(norm_mix=norm_mix, attn_w_qkv=attn_w_qkv, attn_b_qkv=attn_b_qkv, attn_sinks=attn_sinks, attn_w_o=attn_w_o, attn_b_o=attn_b_o, conv_w_pw1=conv_w_pw1, conv_b_pw1=conv_b_pw1, conv_w_dw=conv_w_dw, conv_b_dw=conv_b_dw, conv_ln_g=conv_ln_g, conv_ln_b=conv_ln_b, conv_w_pw2=conv_w_pw2, conv_b_pw2=conv_b_pw2, norm_ffn=norm_ffn, ffn_w_up=ffn_w_up, ffn_w_dw=ffn_w_dw, ffn_b_dw=ffn_b_dw, ffn_w_down=ffn_w_down, final_norm=final_norm)
    moms = dict(norm_mix=m_norm_mix, attn_w_qkv=m_attn_w_qkv, attn_b_qkv=m_attn_b_qkv, attn_sinks=m_attn_sinks, attn_w_o=m_attn_w_o, attn_b_o=m_attn_b_o, conv_w_pw1=m_conv_w_pw1, conv_b_pw1=m_conv_b_pw1, conv_w_dw=m_conv_w_dw, conv_b_dw=m_conv_b_dw, conv_ln_g=m_conv_ln_g, conv_ln_b=m_conv_ln_b, conv_w_pw2=m_conv_w_pw2, conv_b_pw2=m_conv_b_pw2, norm_ffn=m_norm_ffn, ffn_w_up=m_ffn_w_up, ffn_w_dw=m_ffn_w_dw, ffn_b_dw=m_ffn_b_dw, ffn_w_down=m_ffn_w_down, final_norm=m_final_norm)
    vars_ = dict(norm_mix=v_norm_mix, attn_w_qkv=v_attn_w_qkv, attn_b_qkv=v_attn_b_qkv, attn_sinks=v_attn_sinks, attn_w_o=v_attn_w_o, attn_b_o=v_attn_b_o, conv_w_pw1=v_conv_w_pw1, conv_b_pw1=v_conv_b_pw1, conv_w_dw=v_conv_w_dw, conv_b_dw=v_conv_b_dw, conv_ln_g=v_conv_ln_g, conv_ln_b=v_conv_ln_b, conv_w_pw2=v_conv_w_pw2, conv_b_pw2=v_conv_b_pw2, norm_ffn=v_norm_ffn, ffn_w_up=v_ffn_w_up, ffn_w_dw=v_ffn_w_dw, ffn_b_dw=v_ffn_b_dw, ffn_w_down=v_ffn_w_down, final_norm=v_final_norm)
    names = list(weights)
    delta, new_m, new_v = {}, {}, {}
    for nm in names:
        shp = weights[nm].shape
        two_d = (math.prod(shp[:-1]), shp[-1])
        res = _adamw("adamw_" + nm, *(t[nm].reshape(two_d) for t in (weights, grads, moms, vars_)))
        delta[nm], new_m[nm], new_v[nm] = (r.reshape(shp) for r in res)
    return (loss, grad_x.reshape(1, s, D), *[grads[nm] for nm in names], *[delta[nm] for nm in names],
            *[new_m[nm] for nm in names], *[new_v[nm] for nm in names])
```
